```python
import math
import jax, jax.numpy as jnp
from jax import lax
import numpy as np

D_MODEL = 1024
BATCH = 8
SEQ = 8192
DEPTH = 2

EPS = 1e-6
Q_BLOCK = 128
SB_HEADS = 4
SB_HEAD_DIM = 64
SB_WIDTH = SB_HEADS * SB_HEAD_DIM
SSM_HEADS = 8
SSM_HEAD_DIM = 64
SSM_INNER = SSM_HEADS * SSM_HEAD_DIM
SSM_GROUPS = 2
SSM_STATE = 64
SSM_CONV = 4
SSM_CHUNK = 128
SSM_CONV_DIM = SSM_INNER + 2 * SSM_GROUPS * SSM_STATE
MLA_HEADS = 4
MLA_NOPE = 64
MLA_ROPE = 32
MLA_V = 64
MLA_Q_RANK = 256
MLA_KV_RANK = 128
MLA_WIDTH = MLA_HEADS * MLA_V
ROPE_THETA = 10000.0
D_MIX = SB_WIDTH + SSM_INNER + MLA_WIDTH
IN_SPLITS = (3 * SB_WIDTH, SSM_INNER, SSM_CONV_DIM, SSM_HEADS, MLA_Q_RANK, MLA_KV_RANK, MLA_ROPE)
D_IN = 3 * SB_WIDTH + SSM_INNER + SSM_CONV_DIM + SSM_HEADS + MLA_Q_RANK + MLA_KV_RANK + MLA_ROPE
D_FF = 2816
FFN_CONV = 3

kernel_name = "hymba_sb_ssd_mla_convffn"


def rms_norm(x, g):
    xf = x.astype(jnp.float32)
    y = xf * lax.rsqrt(jnp.mean(xf * xf, axis=-1, keepdims=True) + EPS)
    return (y * g.astype(jnp.float32)).astype(x.dtype)


def causal_depthwise_conv(x, w, b):
    k = w.shape[0]
    c = x.shape[-1]
    y = lax.conv_general_dilated(
        x, w[:, None, :].astype(x.dtype), window_strides=(1,), padding=((k - 1, 0),),
        dimension_numbers=('NWC', 'WIO', 'NWC'), feature_group_count=c)
    return y + b.astype(y.dtype)


def query_blocks(fn, q):
    b, s, h, d = q.shape
    nb = s // Q_BLOCK
    qb = q.reshape(b, nb, Q_BLOCK, h, d).transpose(1, 0, 2, 3, 4)
    starts = jnp.arange(nb, dtype=jnp.int32) * Q_BLOCK
    out = lax.map(lambda a: fn(a[0], a[1]), (qb, starts))
    return out.transpose(1, 0, 2, 3, 4).reshape(b, s, h, out.shape[-1])


def stick_breaking_attention(q, k, v):
    s = k.shape[1]
    scale = SB_HEAD_DIM ** -0.5
    k_idx = jnp.arange(s)

    def block(qb, t0):
        z = jnp.einsum('bqhd,bkhd->bhqk', qb, k).astype(jnp.float32) * scale
        q_idx = t0 + jnp.arange(Q_BLOCK)
        mask = k_idx[None, :] < q_idx[:, None]
        log_keep = jnp.where(mask, jax.nn.log_sigmoid(-z), 0.0)
        later = lax.cumsum(log_keep, axis=3, reverse=True) - log_keep
        log_w = jnp.where(mask, jax.nn.log_sigmoid(z) + later, -jnp.inf)
        w = jnp.exp(log_w)
        return jnp.einsum('bhqk,bkhd->bqhd', w.astype(v.dtype), v)

    return query_blocks(block, q)


def causal_softmax_attention(q, k, v, scale):
    s = k.shape[1]
    k_idx = jnp.arange(s)

    def block(qb, t0):
        sc = jnp.einsum('bqhd,bkhd->bhqk', qb, k).astype(jnp.float32) * scale
        q_idx = t0 + jnp.arange(Q_BLOCK)
        mask = k_idx[None, :] <= q_idx[:, None]
        p = jax.nn.softmax(jnp.where(mask, sc, -jnp.inf), axis=-1)
        return jnp.einsum('bhqk,bkhd->bqhd', p.astype(v.dtype), v)

    return query_blocks(block, q)


def ssd_scan(xs, dt, a, bm, cm):
    f32 = jnp.float32
    b, s, h, p = xs.shape
    g, n = bm.shape[-2:]
    r = h // g
    l = SSM_CHUNK
    nc = s // l
    x = xs.astype(f32).reshape(b, nc, l, g, r, p)
    dt = dt.astype(f32).reshape(b, nc, l, g, r)
    bm = bm.astype(f32).reshape(b, nc, l, g, n)
    cm = cm.astype(f32).reshape(b, nc, l, g, n)
    da_cs = jnp.cumsum(dt * a.astype(f32).reshape(g, r), axis=2)
    xdt = x * dt[..., None]
    causal = jnp.tril(jnp.ones((l, l), dtype=bool))
    seg = da_cs[:, :, :, None] - da_cs[:, :, None, :]
    decay = jnp.exp(jnp.where(causal[:, :, None, None], seg, -jnp.inf))
    cb = jnp.einsum('bclgn,bcsgn->bclsg', cm, bm)
    y_diag = jnp.einsum('bclsg,bclsgr,bcsgrp->bclgrp', cb, decay, xdt)
    to_end = jnp.exp(da_cs[:, :, -1:] - da_cs)
    states = jnp.einsum('bclgn,bclgr,bclgrp->bcgrpn', bm, to_end, xdt)
    chunk_decay = jnp.exp(da_cs[:, :, -1])

    def step(h_state, inp):
        st, dc = inp
        return h_state * dc[..., None, None] + st, h_state

    h0 = jnp.zeros((b, g, r, p, n), f32)
    _, prev = lax.scan(step, h0, (states.transpose(1, 0, 2, 3, 4, 5), chunk_decay.transpose(1, 0, 2, 3)))
    prev = prev.transpose(1, 0, 2, 3, 4, 5)
    y_off = jnp.einsum('bclgn,bcgrpn,bclgr->bclgrp', cm, prev, jnp.exp(da_cs))
    return (y_diag + y_off).reshape(b, s, h, p)


def rope_tables(positions):
    inv_freq = 1.0 / (ROPE_THETA ** (jnp.arange(0, MLA_ROPE, 2, dtype=jnp.float32) / MLA_ROPE))
    ang = positions.astype(jnp.float32)[..., None] * inv_freq
    return jnp.cos(ang)[:, :, None, :], jnp.sin(ang)[:, :, None, :]


def apply_rope(x, cos, sin):
    xf = x.astype(jnp.float32)
    x1, x2 = xf[..., :MLA_ROPE // 2], xf[..., MLA_ROPE // 2:]
    return jnp.concatenate([x1 * cos - x2 * sin, x2 * cos + x1 * sin], axis=-1).astype(x.dtype)


def hybrid_mixer(h, positions, w_in, sb_out_norm, conv_w, conv_b, dt_bias, a_log, d_skip,
                 ssm_out_norm, q_norm, w_uq, kv_norm, w_ukv, mla_out_norm, w_out):
    b, s, _ = h.shape
    proj = h @ w_in
    offs = [int(o) for o in np.cumsum(IN_SPLITS)[:-1]]
    sb_qkv, z, xbc, dt_raw, c_q, c_kv, k_rope = jnp.split(proj, offs, axis=-1)

    qkv = sb_qkv.reshape(b, s, 3, SB_HEADS, SB_HEAD_DIM)
    y_a = stick_breaking_attention(qkv[:, :, 0], qkv[:, :, 1], qkv[:, :, 2]).reshape(b, s, SB_WIDTH)
    y_a = rms_norm(y_a, sb_out_norm)

    xbc = jax.nn.silu(causal_depthwise_conv(xbc, conv_w, conv_b))
    gn = SSM_GROUPS * SSM_STATE
    x_ssm, b_ssm, c_ssm = jnp.split(xbc, [SSM_INNER, SSM_INNER + gn], axis=-1)
    dt = jax.nn.softplus(dt_raw.astype(jnp.float32) + dt_bias.astype(jnp.float32))
    a = -jnp.exp(a_log.astype(jnp.float32))
    x_heads = x_ssm.reshape(b, s, SSM_HEADS, SSM_HEAD_DIM)
    y_ssm = ssd_scan(x_heads, dt, a,
                     b_ssm.reshape(b, s, SSM_GROUPS, SSM_STATE),
                     c_ssm.reshape(b, s, SSM_GROUPS, SSM_STATE))
    y_ssm = y_ssm + x_heads.astype(jnp.float32) * d_skip.astype(jnp.float32)[:, None]
    y_b = rms_norm(y_ssm.reshape(b, s, SSM_INNER) * jax.nn.silu(z.astype(jnp.float32)),
                   ssm_out_norm).astype(h.dtype)

    cos, sin = rope_tables(positions)
    q_c = (rms_norm(c_q, q_norm) @ w_uq).reshape(b, s, MLA_HEADS, MLA_NOPE + MLA_ROPE)
    q_full = jnp.concatenate([q_c[..., :MLA_NOPE], apply_rope(q_c[..., MLA_NOPE:], cos, sin)], axis=-1)
    kv = (rms_norm(c_kv, kv_norm) @ w_ukv).reshape(b, s, MLA_HEADS, MLA_NOPE + MLA_V)
    k_pe = apply_rope(k_rope[:, :, None, :], cos, sin)
    k_full = jnp.concatenate(
        [kv[..., :MLA_NOPE], jnp.broadcast_to(k_pe, (b, s, MLA_HEADS, MLA_ROPE))], axis=-1)
    y_c = causal_softmax_attention(q_full, k_full, kv[..., MLA_NOPE:],
                                   (MLA_NOPE + MLA_ROPE) ** -0.5).reshape(b, s, MLA_WIDTH)
    y_c = rms_norm(y_c, mla_out_norm)

    return jnp.concatenate([y_a, y_b, y_c], axis=-1) @ w_out


def conv_glu_ffn(h, w_up, conv_w, conv_b, w_down):
    u = causal_depthwise_conv(h @ w_up, conv_w, conv_b)
    gate, val = jnp.split(u, 2, axis=-1)
    return (jax.nn.silu(gate) * val) @ w_down


def _fwd_setup_inputs(seed: int = 0) -> dict:
    key = jax.random.key(seed)
    ks = jax.random.split(key, 24)
    L = DEPTH
    nrm = jax.random.normal

    def gain(k, shape):
        return 1.0 + 0.02 * nrm(k, shape, jnp.float32)

    x = nrm(ks[0], (BATCH, SEQ, D_MODEL), jnp.float32)
    offset = jax.random.randint(ks[1], (BATCH, 1), 0, 4096)
    positions = (jnp.arange(SEQ, dtype=jnp.int32)[None, :] + offset).astype(jnp.int32)
    dt0 = jnp.exp(jax.random.uniform(ks[8], (L, SSM_HEADS), jnp.float32)
                  * (math.log(0.1) - math.log(0.001)) + math.log(0.001))
    return {
        "x": x,
        "positions": positions,
        "mix_norm": gain(ks[2], (L, D_MODEL)),
        "w_in": nrm(ks[3], (L, D_MODEL, D_IN), jnp.float32) * D_MODEL ** -0.5,
        "sb_out_norm": gain(ks[4], (L, SB_WIDTH)),
        "ssm_conv_w": nrm(ks[5], (L, SSM_CONV, SSM_CONV_DIM), jnp.float32) * SSM_CONV ** -0.5,
        "ssm_conv_b": 0.01 * nrm(ks[6], (L, SSM_CONV_DIM), jnp.float32),
        "ssm_dt_bias": dt0 + jnp.log(-jnp.expm1(-dt0)),
        "ssm_a_log": jnp.log(jax.random.uniform(ks[9], (L, SSM_HEADS), jnp.float32, 1.0, 16.0)),
        "ssm_d": 1.0 + 0.1 * nrm(ks[10], (L, SSM_HEADS), jnp.float32),
        "ssm_out_norm": gain(ks[11], (L, SSM_INNER)),
        "mla_q_norm": gain(ks[12], (L, MLA_Q_RANK)),
        "mla_w_uq": nrm(ks[13], (L, MLA_Q_RANK, MLA_HEADS * (MLA_NOPE + MLA_ROPE)), jnp.float32) * MLA_Q_RANK ** -0.5,
        "mla_kv_norm": gain(ks[14], (L, MLA_KV_RANK)),
        "mla_w_ukv": nrm(ks[15], (L, MLA_KV_RANK, MLA_HEADS * (MLA_NOPE + MLA_V)), jnp.float32) * MLA_KV_RANK ** -0.5,
        "mla_out_norm": gain(ks[16], (L, MLA_WIDTH)),
        "w_out": nrm(ks[17], (L, D_MIX, D_MODEL), jnp.float32) * D_MIX ** -0.5,
        "ffn_norm": gain(ks[18], (L, D_MODEL)),
        "ffn_w_up": nrm(ks[19], (L, D_MODEL, 2 * D_FF), jnp.float32) * D_MODEL ** -0.5,
        "ffn_conv_w": nrm(ks[20], (L, FFN_CONV, 2 * D_FF), jnp.float32) * FFN_CONV ** -0.5,
        "ffn_conv_b": 0.01 * nrm(ks[21], (L, 2 * D_FF), jnp.float32),
        "ffn_w_down": nrm(ks[22], (L, D_FF, D_MODEL), jnp.float32) * D_FF ** -0.5,
        "final_norm": gain(ks[23], (D_MODEL,)),
    }


def _fwd_reference(x, positions, mix_norm, w_in, sb_out_norm, ssm_conv_w, ssm_conv_b, ssm_dt_bias,
              ssm_a_log, ssm_d, ssm_out_norm, mla_q_norm, mla_w_uq, mla_kv_norm, mla_w_ukv,
              mla_out_norm, w_out, ffn_norm, ffn_w_up, ffn_conv_w, ffn_conv_b, ffn_w_down,
              final_norm):
    h = x
    for i in range(DEPTH):
        h = h + hybrid_mixer(rms_norm(h, mix_norm[i]), positions, w_in[i], sb_out_norm[i],
                             ssm_conv_w[i], ssm_conv_b[i], ssm_dt_bias[i], ssm_a_log[i], ssm_d[i],
                             ssm_out_norm[i], mla_q_norm[i], mla_w_uq[i], mla_kv_norm[i],
                             mla_w_ukv[i], mla_out_norm[i], w_out[i])
        h = h + conv_glu_ffn(rms_norm(h, ffn_norm[i]), ffn_w_up[i], ffn_conv_w[i],
                             ffn_conv_b[i], ffn_w_down[i])
    return rms_norm(h, final_norm)


import jax as _jax
import jax.numpy as _jnp

TWIN_FORMAT = 'train_step'
FWD_PARAMS = ['x', 'positions', 'mix_norm', 'w_in', 'sb_out_norm', 'ssm_conv_w', 'ssm_conv_b', 'ssm_dt_bias', 'ssm_a_log', 'ssm_d', 'ssm_out_norm', 'mla_q_norm', 'mla_w_uq', 'mla_kv_norm', 'mla_w_ukv', 'mla_out_norm', 'w_out', 'ffn_norm', 'ffn_w_up', 'ffn_conv_w', 'ffn_conv_b', 'ffn_w_down', 'final_norm']
TWIN_WEIGHTS = ['mix_norm', 'w_in', 'sb_out_norm', 'ssm_conv_w', 'ssm_conv_b', 'ssm_dt_bias', 'ssm_a_log', 'ssm_d', 'ssm_out_norm', 'mla_q_norm', 'mla_w_uq', 'mla_kv_norm', 'mla_w_ukv', 'mla_out_norm', 'w_out', 'ffn_norm', 'ffn_w_up', 'ffn_conv_w', 'ffn_conv_b', 'ffn_w_down', 'final_norm']
TWIN_DIFF_INPUT = 'x'
TWIN_INPUTS = ['x', 'positions', 'mix_norm', 'w_in', 'sb_out_norm', 'ssm_conv_w', 'ssm_conv_b', 'ssm_dt_bias', 'ssm_a_log', 'ssm_d', 'ssm_out_norm', 'mla_q_norm', 'mla_w_uq', 'mla_kv_norm', 'mla_w_ukv', 'mla_out_norm', 'w_out', 'ffn_norm', 'ffn_w_up', 'ffn_conv_w', 'ffn_conv_b', 'ffn_w_down', 'final_norm', 'loss_target', 'm_mix_norm', 'm_w_in', 'm_sb_out_norm', 'm_ssm_conv_w', 'm_ssm_conv_b', 'm_ssm_dt_bias', 'm_ssm_a_log', 'm_ssm_d', 'm_ssm_out_norm', 'm_mla_q_norm', 'm_mla_w_uq', 'm_mla_kv_norm', 'm_mla_w_ukv', 'm_mla_out_norm', 'm_w_out', 'm_ffn_norm', 'm_ffn_w_up', 'm_ffn_conv_w', 'm_ffn_conv_b', 'm_ffn_w_down', 'm_final_norm', 'v_mix_norm', 'v_w_in', 'v_sb_out_norm', 'v_ssm_conv_w', 'v_ssm_conv_b', 'v_ssm_dt_bias', 'v_ssm_a_log', 'v_ssm_d', 'v_ssm_out_norm', 'v_mla_q_norm', 'v_mla_w_uq', 'v_mla_kv_norm', 'v_mla_w_ukv', 'v_mla_out_norm', 'v_w_out', 'v_ffn_norm', 'v_ffn_w_up', 'v_ffn_conv_w', 'v_ffn_conv_b', 'v_ffn_w_down', 'v_final_norm']
TWIN_OUTPUTS = ['loss', 'grad_x', 'grad_mix_norm', 'grad_w_in', 'grad_sb_out_norm', 'grad_ssm_conv_w', 'grad_ssm_conv_b', 'grad_ssm_dt_bias', 'grad_ssm_a_log', 'grad_ssm_d', 'grad_ssm_out_norm', 'grad_mla_q_norm', 'grad_mla_w_uq', 'grad_mla_kv_norm', 'grad_mla_w_ukv', 'grad_mla_out_norm', 'grad_w_out', 'grad_ffn_norm', 'grad_ffn_w_up', 'grad_ffn_conv_w', 'grad_ffn_conv_b', 'grad_ffn_w_down', 'grad_final_norm', 'delta_mix_norm', 'delta_w_in', 'delta_sb_out_norm', 'delta_ssm_conv_w', 'delta_ssm_conv_b', 'delta_ssm_dt_bias', 'delta_ssm_a_log', 'delta_ssm_d', 'delta_ssm_out_norm', 'delta_mla_q_norm', 'delta_mla_w_uq', 'delta_mla_kv_norm', 'delta_mla_w_ukv', 'delta_mla_out_norm', 'delta_w_out', 'delta_ffn_norm', 'delta_ffn_w_up', 'delta_ffn_conv_w', 'delta_ffn_conv_b', 'delta_ffn_w_down', 'delta_final_norm', 'new_m_mix_norm', 'new_m_w_in', 'new_m_sb_out_norm', 'new_m_ssm_conv_w', 'new_m_ssm_conv_b', 'new_m_ssm_dt_bias', 'new_m_ssm_a_log', 'new_m_ssm_d', 'new_m_ssm_out_norm', 'new_m_mla_q_norm', 'new_m_mla_w_uq', 'new_m_mla_kv_norm', 'new_m_mla_w_ukv', 'new_m_mla_out_norm', 'new_m_w_out', 'new_m_ffn_norm', 'new_m_ffn_w_up', 'new_m_ffn_conv_w', 'new_m_ffn_conv_b', 'new_m_ffn_w_down', 'new_m_final_norm', 'new_v_mix_norm', 'new_v_w_in', 'new_v_sb_out_norm', 'new_v_ssm_conv_w', 'new_v_ssm_conv_b', 'new_v_ssm_dt_bias', 'new_v_ssm_a_log', 'new_v_ssm_d', 'new_v_ssm_out_norm', 'new_v_mla_q_norm', 'new_v_mla_w_uq', 'new_v_mla_kv_norm', 'new_v_mla_w_ukv', 'new_v_mla_out_norm', 'new_v_w_out', 'new_v_ffn_norm', 'new_v_ffn_w_up', 'new_v_ffn_conv_w', 'new_v_ffn_conv_b', 'new_v_ffn_w_down', 'new_v_final_norm']
TWIN_LEAF_KINDS = {'loss': 'loss', 'grad_x': 'grad_x', 'grad_mix_norm': 'grad_w', 'grad_w_in': 'grad_w', 'grad_sb_out_norm': 'grad_w', 'grad_ssm_conv_w': 'grad_w', 'grad_ssm_conv_b': 'grad_w', 'grad_ssm_dt_bias': 'grad_w', 'grad_ssm_a_log': 'grad_w', 'grad_ssm_d': 'grad_w', 'grad_ssm_out_norm': 'grad_w', 'grad_mla_q_norm': 'grad_w', 'grad_mla_w_uq': 'grad_w', 'grad_mla_kv_norm': 'grad_w', 'grad_mla_w_ukv': 'grad_w', 'grad_mla_out_norm': 'grad_w', 'grad_w_out': 'grad_w', 'grad_ffn_norm': 'grad_w', 'grad_ffn_w_up': 'grad_w', 'grad_ffn_conv_w': 'grad_w', 'grad_ffn_conv_b': 'grad_w', 'grad_ffn_w_down': 'grad_w', 'grad_final_norm': 'grad_w', 'delta_mix_norm': 'delta_w', 'delta_w_in': 'delta_w', 'delta_sb_out_norm': 'delta_w', 'delta_ssm_conv_w': 'delta_w', 'delta_ssm_conv_b': 'delta_w', 'delta_ssm_dt_bias': 'delta_w', 'delta_ssm_a_log': 'delta_w', 'delta_ssm_d': 'delta_w', 'delta_ssm_out_norm': 'delta_w', 'delta_mla_q_norm': 'delta_w', 'delta_mla_w_uq': 'delta_w', 'delta_mla_kv_norm': 'delta_w', 'delta_mla_w_ukv': 'delta_w', 'delta_mla_out_norm': 'delta_w', 'delta_w_out': 'delta_w', 'delta_ffn_norm': 'delta_w', 'delta_ffn_w_up': 'delta_w', 'delta_ffn_conv_w': 'delta_w', 'delta_ffn_conv_b': 'delta_w', 'delta_ffn_w_down': 'delta_w', 'delta_final_norm': 'delta_w', 'new_m_mix_norm': 'new_m', 'new_m_w_in': 'new_m', 'new_m_sb_out_norm': 'new_m', 'new_m_ssm_conv_w': 'new_m', 'new_m_ssm_conv_b': 'new_m', 'new_m_ssm_dt_bias': 'new_m', 'new_m_ssm_a_log': 'new_m', 'new_m_ssm_d': 'new_m', 'new_m_ssm_out_norm': 'new_m', 'new_m_mla_q_norm': 'new_m', 'new_m_mla_w_uq': 'new_m', 'new_m_mla_kv_norm': 'new_m', 'new_m_mla_w_ukv': 'new_m', 'new_m_mla_out_norm': 'new_m', 'new_m_w_out': 'new_m', 'new_m_ffn_norm': 'new_m', 'new_m_ffn_w_up': 'new_m', 'new_m_ffn_conv_w': 'new_m', 'new_m_ffn_conv_b': 'new_m', 'new_m_ffn_w_down': 'new_m', 'new_m_final_norm': 'new_m', 'new_v_mix_norm': 'new_v', 'new_v_w_in': 'new_v', 'new_v_sb_out_norm': 'new_v', 'new_v_ssm_conv_w': 'new_v', 'new_v_ssm_conv_b': 'new_v', 'new_v_ssm_dt_bias': 'new_v', 'new_v_ssm_a_log': 'new_v', 'new_v_ssm_d': 'new_v', 'new_v_ssm_out_norm': 'new_v', 'new_v_mla_q_norm': 'new_v', 'new_v_mla_w_uq': 'new_v', 'new_v_mla_kv_norm': 'new_v', 'new_v_mla_w_ukv': 'new_v', 'new_v_mla_out_norm': 'new_v', 'new_v_w_out': 'new_v', 'new_v_ffn_norm': 'new_v', 'new_v_ffn_w_up': 'new_v', 'new_v_ffn_conv_w': 'new_v', 'new_v_ffn_conv_b': 'new_v', 'new_v_ffn_w_down': 'new_v', 'new_v_final_norm': 'new_v'}


def _forward(args):
    return _fwd_reference(*[args[k] for k in FWD_PARAMS])


def _output_shape():
    def fwd():
        inp = _fwd_setup_inputs(0)
        return _fwd_reference(*[inp[k] for k in FWD_PARAMS])
    out = _jax.eval_shape(fwd)
    return out.shape, out.dtype

N_MICROBATCH = 1
ADAM_LR = 0.001
ADAM_B1 = 0.9
ADAM_B2 = 0.999
ADAM_EPS = 1e-08
ADAM_WD = 0.01
ADAM_STEP = 10
PER_EXAMPLE_BATCH_AXIS = {'x': 0, 'positions': 0, 'loss_target': 0}
SHARED_INPUTS = []
_WEIGHT_DTYPES = {'mix_norm': _jnp.float32, 'w_in': _jnp.float32, 'sb_out_norm': _jnp.float32, 'ssm_conv_w': _jnp.float32, 'ssm_conv_b': _jnp.float32, 'ssm_dt_bias': _jnp.float32, 'ssm_a_log': _jnp.float32, 'ssm_d': _jnp.float32, 'ssm_out_norm': _jnp.float32, 'mla_q_norm': _jnp.float32, 'mla_w_uq': _jnp.float32, 'mla_kv_norm': _jnp.float32, 'mla_w_ukv': _jnp.float32, 'mla_out_norm': _jnp.float32, 'w_out': _jnp.float32, 'ffn_norm': _jnp.float32, 'ffn_w_up': _jnp.float32, 'ffn_conv_w': _jnp.float32, 'ffn_conv_b': _jnp.float32, 'ffn_w_down': _jnp.float32, 'final_norm': _jnp.float32}
MOMENT_SCALE = {'mix_norm': 2.684447e-01, 'w_in': 1.733919e-01, 'sb_out_norm': 1.894929e-01, 'ssm_conv_w': 1.640603e-01, 'ssm_conv_b': 2.106461e-01, 'ssm_dt_bias': 3.513093e-01, 'ssm_a_log': 7.987127e-01, 'ssm_d': 2.043745e+00, 'ssm_out_norm': 1.965237e-01, 'mla_q_norm': 1.565988e-01, 'mla_w_uq': 1.349424e-01, 'mla_kv_norm': 5.448271e-01, 'mla_w_ukv': 1.912694e-01, 'mla_out_norm': 2.187910e-01, 'w_out': 1.944858e-01, 'ffn_norm': 1.481483e-01, 'ffn_w_up': 6.051376e-02, 'ffn_conv_w': 6.057342e-02, 'ffn_conv_b': 7.019214e-02, 'ffn_w_down': 9.881833e-02, 'final_norm': 6.402426e+01}


def _to_microbatches(a, axis):
    t = _jnp.moveaxis(a, axis, 0)
    t = t.reshape((N_MICROBATCH, t.shape[0] // N_MICROBATCH) + t.shape[1:])
    return _jnp.moveaxis(t, 1, axis + 1)


def setup_inputs(seed: int = 0) -> dict:
    inp = _fwd_setup_inputs(seed)
    key = _jax.random.fold_in(_jax.random.key(seed), 7919)
    shape, _ = _output_shape()
    out = dict(inp)
    out["loss_target"] = _jax.random.normal(_jax.random.fold_in(key, 0), shape, _jnp.float32)
    for i, name in enumerate(TWIN_WEIGHTS):
        w = inp[name].astype(_jnp.float32)
        if MOMENT_SCALE is None:
            s = _jnp.sqrt(_jnp.mean(_jnp.square(w)) + 1e-30)
        else:
            s = MOMENT_SCALE[name]
        km, kv = _jax.random.split(_jax.random.fold_in(key, i + 1))
        out[name] = w
        out["m_" + name] = s * _jax.random.normal(km, w.shape, _jnp.float32)
        out["v_" + name] = (s * s) * _jax.random.uniform(kv, w.shape, _jnp.float32, 0.5, 1.5)
    if N_MICROBATCH > 1:
        for name, axis in PER_EXAMPLE_BATCH_AXIS.items():
            out[name] = _to_microbatches(out[name], axis)
    return {'x': out['x'], 'positions': out['positions'], 'mix_norm': out['mix_norm'], 'w_in': out['w_in'], 'sb_out_norm': out['sb_out_norm'], 'ssm_conv_w': out['ssm_conv_w'], 'ssm_conv_b': out['ssm_conv_b'], 'ssm_dt_bias': out['ssm_dt_bias'], 'ssm_a_log': out['ssm_a_log'], 'ssm_d': out['ssm_d'], 'ssm_out_norm': out['ssm_out_norm'], 'mla_q_norm': out['mla_q_norm'], 'mla_w_uq': out['mla_w_uq'], 'mla_kv_norm': out['mla_kv_norm'], 'mla_w_ukv': out['mla_w_ukv'], 'mla_out_norm': out['mla_out_norm'], 'w_out': out['w_out'], 'ffn_norm': out['ffn_norm'], 'ffn_w_up': out['ffn_w_up'], 'ffn_conv_w': out['ffn_conv_w'], 'ffn_conv_b': out['ffn_conv_b'], 'ffn_w_down': out['ffn_w_down'], 'final_norm': out['final_norm'], 'loss_target': out['loss_target'], 'm_mix_norm': out['m_mix_norm'], 'm_w_in': out['m_w_in'], 'm_sb_out_norm': out['m_sb_out_norm'], 'm_ssm_conv_w': out['m_ssm_conv_w'], 'm_ssm_conv_b': out['m_ssm_conv_b'], 'm_ssm_dt_bias': out['m_ssm_dt_bias'], 'm_ssm_a_log': out['m_ssm_a_log'], 'm_ssm_d': out['m_ssm_d'], 'm_ssm_out_norm': out['m_ssm_out_norm'], 'm_mla_q_norm': out['m_mla_q_norm'], 'm_mla_w_uq': out['m_mla_w_uq'], 'm_mla_kv_norm': out['m_mla_kv_norm'], 'm_mla_w_ukv': out['m_mla_w_ukv'], 'm_mla_out_norm': out['m_mla_out_norm'], 'm_w_out': out['m_w_out'], 'm_ffn_norm': out['m_ffn_norm'], 'm_ffn_w_up': out['m_ffn_w_up'], 'm_ffn_conv_w': out['m_ffn_conv_w'], 'm_ffn_conv_b': out['m_ffn_conv_b'], 'm_ffn_w_down': out['m_ffn_w_down'], 'm_final_norm': out['m_final_norm'], 'v_mix_norm': out['v_mix_norm'], 'v_w_in': out['v_w_in'], 'v_sb_out_norm': out['v_sb_out_norm'], 'v_ssm_conv_w': out['v_ssm_conv_w'], 'v_ssm_conv_b': out['v_ssm_conv_b'], 'v_ssm_dt_bias': out['v_ssm_dt_bias'], 'v_ssm_a_log': out['v_ssm_a_log'], 'v_ssm_d': out['v_ssm_d'], 'v_ssm_out_norm': out['v_ssm_out_norm'], 'v_mla_q_norm': out['v_mla_q_norm'], 'v_mla_w_uq': out['v_mla_w_uq'], 'v_mla_kv_norm': out['v_mla_kv_norm'], 'v_mla_w_ukv': out['v_mla_w_ukv'], 'v_mla_out_norm': out['v_mla_out_norm'], 'v_w_out': out['v_w_out'], 'v_ffn_norm': out['v_ffn_norm'], 'v_ffn_w_up': out['v_ffn_w_up'], 'v_ffn_conv_w': out['v_ffn_conv_w'], 'v_ffn_conv_b': out['v_ffn_conv_b'], 'v_ffn_w_down': out['v_ffn_w_down'], 'v_final_norm': out['v_final_norm']}


def _loss(weights, diff, rest, loss_target):
    with _jax.named_scope("forward"):
        args = {**rest, TWIN_DIFF_INPUT: diff, **{k: w.astype(_WEIGHT_DTYPES[k]) for k, w in weights.items()}}
        y = _forward(args)
    with _jax.named_scope("loss_head"):
        err = _jnp.square(y.astype(_jnp.float32) - loss_target)
        return 0.5 * _jnp.sum(_jnp.mean(err, axis=-1)) if err.ndim else 0.5 * err


def _adamw(w, g, m, v):
    m = ADAM_B1 * m + (1.0 - ADAM_B1) * g
    v = ADAM_B2 * v + (1.0 - ADAM_B2) * _jnp.square(g)
    m_hat = m / (1.0 - ADAM_B1 ** ADAM_STEP)
    v_hat = v / (1.0 - ADAM_B2 ** ADAM_STEP)
    delta = -ADAM_LR * (m_hat / (_jnp.sqrt(v_hat) + ADAM_EPS) + ADAM_WD * w)
    return delta, m, v


def reference(x, positions, mix_norm, w_in, sb_out_norm, ssm_conv_w, ssm_conv_b, ssm_dt_bias, ssm_a_log, ssm_d, ssm_out_norm, mla_q_norm, mla_w_uq, mla_kv_norm, mla_w_ukv, mla_out_norm, w_out, ffn_norm, ffn_w_up, ffn_conv_w, ffn_conv_b, ffn_w_down, final_norm, loss_target, m_mix_norm, m_w_in, m_sb_out_norm, m_ssm_conv_w, m_ssm_conv_b, m_ssm_dt_bias, m_ssm_a_log, m_ssm_d, m_ssm_out_norm, m_mla_q_norm, m_mla_w_uq, m_mla_kv_norm, m_mla_w_ukv, m_mla_out_norm, m_w_out, m_ffn_norm, m_ffn_w_up, m_ffn_conv_w, m_ffn_conv_b, m_ffn_w_down, m_final_norm, v_mix_norm, v_w_in, v_sb_out_norm, v_ssm_conv_w, v_ssm_conv_b, v_ssm_dt_bias, v_ssm_a_log, v_ssm_d, v_ssm_out_norm, v_mla_q_norm, v_mla_w_uq, v_mla_kv_norm, v_mla_w_ukv, v_mla_out_norm, v_w_out, v_ffn_norm, v_ffn_w_up, v_ffn_conv_w, v_ffn_conv_b, v_ffn_w_down, v_final_norm):
    given = dict(x=x, positions=positions, mix_norm=mix_norm, w_in=w_in, sb_out_norm=sb_out_norm, ssm_conv_w=ssm_conv_w, ssm_conv_b=ssm_conv_b, ssm_dt_bias=ssm_dt_bias, ssm_a_log=ssm_a_log, ssm_d=ssm_d, ssm_out_norm=ssm_out_norm, mla_q_norm=mla_q_norm, mla_w_uq=mla_w_uq, mla_kv_norm=mla_kv_norm, mla_w_ukv=mla_w_ukv, mla_out_norm=mla_out_norm, w_out=w_out, ffn_norm=ffn_norm, ffn_w_up=ffn_w_up, ffn_conv_w=ffn_conv_w, ffn_conv_b=ffn_conv_b, ffn_w_down=ffn_w_down, final_norm=final_norm, loss_target=loss_target, m_mix_norm=m_mix_norm, m_w_in=m_w_in, m_sb_out_norm=m_sb_out_norm, m_ssm_conv_w=m_ssm_conv_w, m_ssm_conv_b=m_ssm_conv_b, m_ssm_dt_bias=m_ssm_dt_bias, m_ssm_a_log=m_ssm_a_log, m_ssm_d=m_ssm_d, m_ssm_out_norm=m_ssm_out_norm, m_mla_q_norm=m_mla_q_norm, m_mla_w_uq=m_mla_w_uq, m_mla_kv_norm=m_mla_kv_norm, m_mla_w_ukv=m_mla_w_ukv, m_mla_out_norm=m_mla_out_norm, m_w_out=m_w_out, m_ffn_norm=m_ffn_norm, m_ffn_w_up=m_ffn_w_up, m_ffn_conv_w=m_ffn_conv_w, m_ffn_conv_b=m_ffn_conv_b, m_ffn_w_down=m_ffn_w_down, m_final_norm=m_final_norm, v_mix_norm=v_mix_norm, v_w_in=v_w_in, v_sb_out_norm=v_sb_out_norm, v_ssm_conv_w=v_ssm_conv_w, v_ssm_conv_b=v_ssm_conv_b, v_ssm_dt_bias=v_ssm_dt_bias, v_ssm_a_log=v_ssm_a_log, v_ssm_d=v_ssm_d, v_ssm_out_norm=v_ssm_out_norm, v_mla_q_norm=v_mla_q_norm, v_mla_w_uq=v_mla_w_uq, v_mla_kv_norm=v_mla_kv_norm, v_mla_w_ukv=v_mla_w_ukv, v_mla_out_norm=v_mla_out_norm, v_w_out=v_w_out, v_ffn_norm=v_ffn_norm, v_ffn_w_up=v_ffn_w_up, v_ffn_conv_w=v_ffn_conv_w, v_ffn_conv_b=v_ffn_conv_b, v_ffn_w_down=v_ffn_w_down, v_final_norm=v_final_norm)
    weights = {n: given[n] for n in TWIN_WEIGHTS}
    shared = {n: given[n] for n in SHARED_INPUTS}
    per_example = {n: given[n] for n in ['x', 'positions']}
    grad_fn = _jax.value_and_grad(_loss, argnums=(0, 1))

    def one_microbatch(ex, loss_target):
        ex = dict(ex)
        diff = ex.pop(TWIN_DIFF_INPUT)
        return grad_fn(weights, diff, {**shared, **ex}, loss_target)

    if N_MICROBATCH == 1:
        loss, (grad_w, grad_x) = one_microbatch(per_example, given["loss_target"])
    else:
        def body(carry, xs):
            loss_sum, grad_sum = carry
            l_k, (gw_k, gx_k) = one_microbatch(xs[0], xs[1])
            with _jax.named_scope("update"):
                return (loss_sum + l_k, _jax.tree.map(_jnp.add, grad_sum, gw_k)), gx_k

        init = (_jnp.zeros((), _jnp.float32), _jax.tree.map(_jnp.zeros_like, weights))
        (loss, grad_w), grad_x = _jax.lax.scan(body, init, (per_example, given["loss_target"]))
    with _jax.named_scope("update"):
        delta_w, new_m, new_v = {}, {}, {}
        for n in TWIN_WEIGHTS:
            delta_w[n], new_m[n], new_v[n] = _adamw(weights[n], grad_w[n], given["m_" + n], given["v_" + n])
    return (loss, grad_x, *[grad_w[n] for n in TWIN_WEIGHTS], *[delta_w[n] for n in TWIN_WEIGHTS],
            *[new_m[n] for n in TWIN_WEIGHTS], *[new_v[n] for n in TWIN_WEIGHTS])
```

```python
import functools
import math

import numpy as np
import jax
import jax.numpy as jnp
from jax import lax
from jax.experimental import pallas as pl
from jax.experimental.pallas import tpu as pltpu

F32 = jnp.float32
BF16 = jnp.bfloat16
MXU_DTYPE = jnp.bfloat16
HIGHEST = lax.Precision.HIGHEST

N_DEV = 8
D_MODEL = 1024
DEPTH = 2
EPS = 1e-6
SB_HEADS, SB_DIM = 4, 64
SSM_HEADS, SSM_P, SSM_GROUPS, SSM_N, SSM_CONV, SSM_CHUNK = 8, 64, 2, 64, 4, 128
SSM_INNER = SSM_HEADS * SSM_P
SSM_CONV_DIM = SSM_INNER + 2 * SSM_GROUPS * SSM_N
MLA_HEADS, MLA_NOPE, MLA_ROPE, MLA_V, MLA_Q_RANK, MLA_KV_RANK = 4, 64, 32, 64, 256, 128
MLA_QK = MLA_NOPE + MLA_ROPE
ROPE_THETA = 10000.0
D_IN = 2472
D_IN_PAD = 2560
TAIL = 2432
DT_LANE = 32
D_FF = 2816
FFN_CONV = 3
ADAM_LR, ADAM_B1, ADAM_B2, ADAM_EPS, ADAM_WD, ADAM_STEP = 0.001, 0.9, 0.999, 1e-08, 0.01, 10

LANES = 128
ATT_BLK = 256
ROW_BLK = 512
CONV_COLS = 256
FLAT_W = 1024
VMEM_LIMIT = 56 << 20

WEIGHTS = ['mix_norm', 'w_in', 'sb_out_norm', 'ssm_conv_w', 'ssm_conv_b', 'ssm_dt_bias', 'ssm_a_log', 'ssm_d',
           'ssm_out_norm', 'mla_q_norm', 'mla_w_uq', 'mla_kv_norm', 'mla_w_ukv', 'mla_out_norm', 'w_out',
           'ffn_norm', 'ffn_w_up', 'ffn_conv_w', 'ffn_conv_b', 'ffn_w_down', 'final_norm']
SHARDED = {'w_in': 2, 'ssm_conv_w': 2, 'mla_w_uq': 2, 'mla_w_ukv': 2, 'w_out': 1, 'ffn_w_up': 2, 'ffn_conv_w': 2,
           'ffn_w_down': 1}
F32_GATHER = ('ssm_conv_w', 'ffn_conv_w')
REPLICATED = [n for n in WEIGHTS if n not in SHARDED]


def _call(body, *, name, out_shape, grid=(), in_specs=None, out_specs=None, scratch=(), sem=None, **kw):
    params = dict(vmem_limit_bytes=VMEM_LIMIT)
    if sem is not None:
        params['dimension_semantics'] = sem
    return pl.pallas_call(body, name=name, out_shape=out_shape, grid=grid, in_specs=in_specs, out_specs=out_specs,
                          scratch_shapes=list(scratch), compiler_params=pltpu.CompilerParams(**params), **kw)


def _tile(n, prefs=(512, 256, 128)):
    for t in prefs:
        if n % t == 0:
            return t
    return n


def _dot(a, b, dims, precision=None):
    return lax.dot_general(a, b, (dims, ((), ())), preferred_element_type=F32, precision=precision)


def _nn(a, b, precision=None):
    return _dot(a, b, ((1,), (0,)), precision)


def _nt(a, b, precision=None):
    return _dot(a, b, ((1,), (1,)), precision)


def _tn(a, b, precision=None):
    return _dot(a, b, ((0,), (0,)), precision)


def _split2(x):
    hi = x.astype(MXU_DTYPE)
    lo = (x - hi.astype(F32)).astype(MXU_DTYPE)
    return hi, lo


def _sigmoid(x):
    return 1.0 / (1.0 + jnp.exp(-x))


def _softplus(x):
    return jnp.maximum(x, 0.0) + jnp.log1p(jnp.exp(-jnp.abs(x)))


def mm(a, b, *, name, ta=False, tb=False, res=None, out_dtype=F32):
    (kdim, m) = a.shape if ta else a.shape[::-1]
    (n, k2) = b.shape if tb else b.shape[::-1]
    assert kdim == k2, (a.shape, b.shape, ta, tb)
    tm, tn, tk = _tile(m), _tile(n), _tile(kdim)
    nk = kdim // tk
    dims = ((0 if ta else 1,), (1 if tb else 0,))

    def body(*refs):
        if res is None:
            a_ref, b_ref, o_ref, acc = refs
        else:
            a_ref, b_ref, r_ref, o_ref, acc = refs
        k = pl.program_id(2)

        @pl.when(k == 0)
        def _():
            acc[...] = jnp.zeros_like(acc)

        acc[...] += _dot(a_ref[...].astype(MXU_DTYPE), b_ref[...].astype(MXU_DTYPE), dims)

        @pl.when(k == nk - 1)
        def _():
            out = acc[...]
            if res is not None:
                out = out + r_ref[...]
            o_ref[...] = out.astype(out_dtype)

    a_spec = pl.BlockSpec((tk, tm), lambda i, j, k: (k, i)) if ta else pl.BlockSpec((tm, tk), lambda i, j, k: (i, k))
    b_spec = pl.BlockSpec((tn, tk), lambda i, j, k: (j, k)) if tb else pl.BlockSpec((tk, tn), lambda i, j, k: (k, j))
    o_spec = pl.BlockSpec((tm, tn), lambda i, j, k: (i, j))
    ins, specs = [a, b], [a_spec, b_spec]
    if res is not None:
        ins.append(res)
        specs.append(o_spec)
    return _call(body, name=name, out_shape=jax.ShapeDtypeStruct((m, n), out_dtype), grid=(m // tm, n // tn, nk),
                 in_specs=specs, out_specs=o_spec, scratch=[pltpu.VMEM((tm, tn), F32)],
                 sem=("parallel", "parallel", "arbitrary"))(*ins)


def rms_fwd(x, g, *, name, gate=None, out_dtype=F32):
    s, w = x.shape
    bs = _tile(s, (ROW_BLK,))

    def body(*refs):
        if gate is None:
            x_ref, g_ref, o_ref = refs
            u = x_ref[...]
        else:
            x_ref, z_ref, g_ref, o_ref = refs
            z = z_ref[...]
            u = x_ref[...] * (z * _sigmoid(z))
        r = lax.rsqrt(jnp.mean(u * u, axis=1, keepdims=True) + EPS)
        o_ref[...] = (u * r * g_ref[...]).astype(out_dtype)

    row = pl.BlockSpec((bs, w), lambda i: (i, 0))
    vec = pl.BlockSpec((1, w), lambda i: (0, 0))
    ins = [x] + ([] if gate is None else [gate]) + [g.reshape(1, w)]
    specs = [row] + ([] if gate is None else [row]) + [vec]
    return _call(body, name=name, out_shape=jax.ShapeDtypeStruct((s, w), out_dtype), grid=(s // bs,),
                 in_specs=specs, out_specs=row, sem=("parallel",))(*ins)


def rms_bwd(x, g, dy, *, name, gate=None, add=None):
    s, w = x.shape
    bs = _tile(s, (ROW_BLK,))

    def body(*refs):
        refs = list(refs)
        x_ref = refs.pop(0)
        z_ref = refs.pop(0) if gate is not None else None
        g_ref = refs.pop(0)
        dy_ref = refs.pop(0)
        add_ref = refs.pop(0) if add is not None else None
        dx_ref = refs.pop(0)
        dz_ref = refs.pop(0) if gate is not None else None
        dg_ref = refs.pop(0)
        i = pl.program_id(0)

        @pl.when(i == 0)
        def _():
            dg_ref[...] = jnp.zeros_like(dg_ref)

        xv = x_ref[...]
        if gate is not None:
            z = z_ref[...]
            sg = _sigmoid(z)
            act = z * sg
            u = xv * act
        else:
            u = xv
        r = lax.rsqrt(jnp.mean(u * u, axis=1, keepdims=True) + EPS)
        dy_v = dy_ref[...]
        dyg = dy_v * g_ref[...]
        du = r * dyg - u * (r * r * r * jnp.mean(dyg * u, axis=1, keepdims=True))
        dg_ref[...] += jnp.sum(dy_v * u * r, axis=0, keepdims=True)
        if gate is not None:
            dx = du * act
            dz_ref[...] = du * xv * (sg * (1.0 + z * (1.0 - sg)))
        else:
            dx = du
        if add is not None:
            dx = dx + add_ref[...]
        dx_ref[...] = dx

    row = pl.BlockSpec((bs, w), lambda i: (i, 0))
    vec = pl.BlockSpec((1, w), lambda i: (0, 0))
    ins = [x] + ([] if gate is None else [gate]) + [g.reshape(1, w), dy] + ([] if add is None else [add])
    specs = [row] + ([] if gate is None else [row]) + [vec, row] + ([] if add is None else [row])
    outs = [jax.ShapeDtypeStruct((s, w), F32)] + ([] if gate is None else [jax.ShapeDtypeStruct((s, w), F32)])
    outs.append(jax.ShapeDtypeStruct((1, w), F32))
    ospecs = [row] + ([] if gate is None else [row]) + [vec]
    return _call(body, name=name, out_shape=outs, grid=(s // bs,), in_specs=specs, out_specs=ospecs,
                 sem=("arbitrary",))(*ins)


def loss_head(y, target, *, name):
    s, w = y.shape
    bs = _tile(s, (ROW_BLK,))
    nb = s // bs

    def body(y_ref, t_ref, dy_ref, loss_ref, acc):
        i = pl.program_id(0)

        @pl.when(i == 0)
        def _():
            acc[...] = jnp.zeros_like(acc)

        e = y_ref[...] - t_ref[...]
        dy_ref[...] = e * (1.0 / w)
        acc[...] += jnp.sum(e * e, axis=0, keepdims=True)

        @pl.when(i == nb - 1)
        def _():
            loss_ref[...] = jnp.sum(acc[...], axis=1, keepdims=True) * (0.5 / w)

    row = pl.BlockSpec((bs, w), lambda i: (i, 0))
    return _call(body, name=name, out_shape=[jax.ShapeDtypeStruct((s, w), F32), jax.ShapeDtypeStruct((1, 1), F32)],
                 grid=(nb,), in_specs=[row, row], out_specs=[row, pl.BlockSpec((1, 1), lambda i: (0, 0))],
                 scratch=[pltpu.VMEM((1, w), F32)], sem=("arbitrary",))(y, target)


def _rope_tables(positions, s):
    inv_freq = 1.0 / (ROPE_THETA ** (jnp.arange(0, MLA_ROPE, 2, dtype=F32) / MLA_ROPE))
    ang = positions.reshape(s, 1).astype(F32) * inv_freq
    cos, sin = jnp.cos(ang), jnp.sin(ang)
    one, zero = jnp.ones((s, MLA_NOPE), F32), jnp.zeros((s, MLA_NOPE), F32)
    cq = jnp.tile(jnp.concatenate([one, cos, cos], axis=1), (1, MLA_HEADS))
    sq = jnp.tile(jnp.concatenate([zero, sin, sin], axis=1), (1, MLA_HEADS))
    pad1, pad0 = jnp.ones((s, LANES - MLA_ROPE), F32), jnp.zeros((s, LANES - MLA_ROPE), F32)
    ct = jnp.concatenate([cos, cos, pad1], axis=1)
    st = jnp.concatenate([sin, sin, pad0], axis=1)
    half = MLA_ROPE // 2

    def swap(width, starts):
        r = np.zeros((width, width), np.float32)
        for o in starts:
            for i in range(half):
                r[o + half + i, o + i] = -1.0
                r[o + i, o + half + i] = 1.0
        return jnp.asarray(r)

    rq = swap(MLA_HEADS * MLA_QK, [h * MLA_QK + MLA_NOPE for h in range(MLA_HEADS)])
    rt = swap(LANES, [0])
    return (cq, sq, rq), (ct, st, rt)


def rope(x, tabs, *, name, backward=False, add=None):
    cos, sin, rot = tabs
    n, s, w = x.shape
    bs = _tile(s, (ROW_BLK,))

    def body(*refs):
        if add is None:
            x_ref, c_ref, s_ref, r_ref, o_ref = refs
        else:
            x_ref, c_ref, s_ref, r_ref, a_ref, o_ref = refs
        xv = x_ref[0]
        for j in range(1, n):
            xv = xv + x_ref[j]
        if backward:
            out = xv * c_ref[...] + _nt(xv * s_ref[...], r_ref[...], HIGHEST)
        else:
            out = xv * c_ref[...] + _nn(xv, r_ref[...], HIGHEST) * s_ref[...]
        if add is not None:
            out = out + a_ref[...]
        o_ref[...] = out

    row = pl.BlockSpec((bs, w), lambda i: (i, 0))
    ins = [x, cos, sin, rot] + ([] if add is None else [add])
    specs = [pl.BlockSpec((n, bs, w), lambda i: (0, i, 0)), row, row, pl.BlockSpec((w, w), lambda i: (0, 0))]
    specs += [] if add is None else [row]
    return _call(body, name=name, out_shape=jax.ShapeDtypeStruct((s, w), F32), grid=(s // bs,), in_specs=specs,
                 out_specs=row, sem=("parallel",))(*ins)


def _tri(n, op):
    r = lax.broadcasted_iota(jnp.int32, (n, n), 0)
    c = lax.broadcasted_iota(jnp.int32, (n, n), 1)
    return r, c, op(r, c)


def sb_fwd(q, k, v, *, name):
    h, s, d = q.shape
    blk = _tile(s, (ATT_BLK,))
    scale = d ** -0.5

    def body(q_ref, k_ref, v_ref, y_ref, t_ref):
        qi = pl.program_id(1)
        qv = q_ref[...]
        row, col, later_mask = _tri(blk, lambda r, c: r > c)
        u_later = later_mask.astype(MXU_DTYPE)
        valid = col < row

        def tile(kb, carry, masked):
            c, acc = carry
            ks = pl.multiple_of(kb * blk, blk)
            kv = k_ref[pl.ds(ks, blk), :]
            vv = v_ref[pl.ds(ks, blk), :]
            z = _nt(qv, kv) * scale
            sp = _softplus(z)
            lk = jnp.where(valid, -sp, 0.0) if masked else -sp
            hi, lo = _split2(lk)
            lw = (z - sp) + (_nn(hi, u_later) + _nn(lo, u_later)) + c
            w = jnp.exp(lw)
            if masked:
                w = jnp.where(valid, w, 0.0)
            acc = acc + _nn(w.astype(MXU_DTYPE), vv)
            return c + jnp.sum(lk, axis=1, keepdims=True), acc

        carry = tile(qi, (jnp.zeros((blk, 1), F32), jnp.zeros((blk, d), F32)), True)
        c, acc = lax.fori_loop(0, qi, lambda i, cr: tile(qi - 1 - i, cr, False), carry)
        y_ref[...] = acc
        t_ref[...] = jnp.broadcast_to(c, (blk, LANES))

    qspec = pl.BlockSpec((None, blk, d), lambda hh, i: (hh, i, 0))
    full = pl.BlockSpec((None, s, d), lambda hh, i: (hh, 0, 0))
    tspec = pl.BlockSpec((None, blk, LANES), lambda hh, i: (hh, i, 0))
    return _call(body, name=name,
                 out_shape=[jax.ShapeDtypeStruct((h, s, d), F32), jax.ShapeDtypeStruct((h, s, LANES), F32)],
                 grid=(h, s // blk), in_specs=[qspec, full, full], out_specs=[qspec, tspec],
                 sem=("parallel", "arbitrary"))(q, k, v)


def sb_bwd(q, k, v, dy, tot, *, name):
    h, s, d = q.shape
    blk = _tile(s, (ATT_BLK,))
    scale = d ** -0.5

    def body(q_ref, k_ref, v_ref, dy_ref, t_ref, dq_ref, dk_ref, dv_ref):
        qi = pl.program_id(1)

        @pl.when(qi == 0)
        def _():
            dk_ref[...] = jnp.zeros_like(dk_ref)
            dv_ref[...] = jnp.zeros_like(dv_ref)

        qv = q_ref[...]
        dyv = dy_ref[...].astype(MXU_DTYPE)
        tv = t_ref[:, 0:1]
        row, col, incl_mask = _tri(blk, lambda r, c: r <= c)
        u_incl = incl_mask.astype(MXU_DTYPE)
        u_excl = (row < col).astype(MXU_DTYPE)
        valid = col < row

        def tile(kb, carry, masked):
            p, gc, dq = carry
            ks = pl.multiple_of(kb * blk, blk)
            kv = k_ref[pl.ds(ks, blk), :]
            vv = v_ref[pl.ds(ks, blk), :]
            z = _nt(qv, kv) * scale
            sp = _softplus(z)
            lk = jnp.where(valid, -sp, 0.0) if masked else -sp
            hi, lo = _split2(lk)
            incl = _nn(hi, u_incl) + _nn(lo, u_incl) + p
            w = jnp.exp((z - sp) + (tv - incl))
            if masked:
                w = jnp.where(valid, w, 0.0)
            g = w * _nt(dyv, vv)
            ghi, glo = _split2(g)
            gex = _nn(ghi, u_excl) + _nn(glo, u_excl) + gc
            keep = jnp.exp(lk)
            dz = g * keep - (1.0 - keep) * gex
            if masked:
                dz = jnp.where(valid, dz, 0.0)
            dzb = dz.astype(MXU_DTYPE)
            dq = dq + _nn(dzb, kv)
            dk_ref[pl.ds(ks, blk), :] += _tn(dzb, qv) * scale
            dv_ref[pl.ds(ks, blk), :] += _tn(w.astype(MXU_DTYPE), dyv)
            return p + jnp.sum(lk, axis=1, keepdims=True), gc + jnp.sum(g, axis=1, keepdims=True), dq

        zero = jnp.zeros((blk, 1), F32)
        carry = lax.fori_loop(0, qi, lambda i, cr: tile(i, cr, False), (zero, zero, jnp.zeros((blk, d), F32)))
        _, _, dq = tile(qi, carry, True)
        dq_ref[...] = dq * scale

    qspec = pl.BlockSpec((None, blk, d), lambda hh, i: (hh, i, 0))
    full = pl.BlockSpec((None, s, d), lambda hh, i: (hh, 0, 0))
    tspec = pl.BlockSpec((None, blk, LANES), lambda hh, i: (hh, i, 0))
    out = jax.ShapeDtypeStruct((h, s, d), F32)
    return _call(body, name=name, out_shape=[out, out, out], grid=(h, s // blk),
                 in_specs=[qspec, full, full, qspec, tspec], out_specs=[qspec, full, full],
                 sem=("parallel", "arbitrary"))(q, k, v, dy, tot)


def mla_fwd(q, k, v, *, name):
    h, s, dk = q.shape
    dv = v.shape[-1]
    blk = _tile(s, (ATT_BLK,))
    scale = dk ** -0.5

    def body(q_ref, k_ref, v_ref, y_ref, l_ref):
        qi = pl.program_id(1)
        qv = q_ref[...]
        row, col, valid = _tri(blk, lambda r, c: c <= r)

        def tile(kb, carry, masked):
            m, l, acc = carry
            ks = pl.multiple_of(kb * blk, blk)
            sc = _nt(qv, k_ref[pl.ds(ks, blk), :]) * scale
            if masked:
                sc = jnp.where(valid, sc, -1e30)
            m2 = jnp.maximum(m, jnp.max(sc, axis=1, keepdims=True))
            p = jnp.exp(sc - m2)
            a = jnp.exp(m - m2)
            l = a * l + jnp.sum(p, axis=1, keepdims=True)
            acc = a * acc + _nn(p.astype(MXU_DTYPE), v_ref[pl.ds(ks, blk), :])
            return m2, l, acc

        init = (jnp.full((blk, 1), -1e30, F32), jnp.zeros((blk, 1), F32), jnp.zeros((blk, dv), F32))
        carry = lax.fori_loop(0, qi, lambda i, cr: tile(i, cr, False), init)
        m, l, acc = tile(qi, carry, True)
        y_ref[...] = acc / l
        l_ref[...] = jnp.broadcast_to(m + jnp.log(l), (blk, LANES))

    qspec = pl.BlockSpec((None, blk, dk), lambda hh, i: (hh, i, 0))
    kfull = pl.BlockSpec((None, s, dk), lambda hh, i: (hh, 0, 0))
    vfull = pl.BlockSpec((None, s, dv), lambda hh, i: (hh, 0, 0))
    yspec = pl.BlockSpec((None, blk, dv), lambda hh, i: (hh, i, 0))
    lspec = pl.BlockSpec((None, blk, LANES), lambda hh, i: (hh, i, 0))
    return _call(body, name=name,
                 out_shape=[jax.ShapeDtypeStruct((h, s, dv), F32), jax.ShapeDtypeStruct((h, s, LANES), F32)],
                 grid=(h, s // blk), in_specs=[qspec, kfull, vfull], out_specs=[yspec, lspec],
                 sem=("parallel", "arbitrary"))(q, k, v)


def mla_bwd(q, k, v, y, dy, lse, *, name):
    h, s, dk = q.shape
    dv = v.shape[-1]
    blk = _tile(s, (ATT_BLK,))
    scale = dk ** -0.5

    def body(q_ref, k_ref, v_ref, y_ref, dy_ref, l_ref, dq_ref, dk_ref, dv_ref):
        qi = pl.program_id(1)

        @pl.when(qi == 0)
        def _():
            dk_ref[...] = jnp.zeros_like(dk_ref)
            dv_ref[...] = jnp.zeros_like(dv_ref)

        qv = q_ref[...]
        dyf = dy_ref[...]
        dyv = dyf.astype(MXU_DTYPE)
        delta = jnp.sum(dyf * y_ref[...], axis=1, keepdims=True)
        lv = l_ref[:, 0:1]
        row, col, valid = _tri(blk, lambda r, c: c <= r)

        def tile(kb, dq, masked):
            ks = pl.multiple_of(kb * blk, blk)
            kv = k_ref[pl.ds(ks, blk), :]
            vv = v_ref[pl.ds(ks, blk), :]
            p = jnp.exp(_nt(qv, kv) * scale - lv)
            if masked:
                p = jnp.where(valid, p, 0.0)
            ds = (p * (_nt(dyv, vv) - delta)).astype(MXU_DTYPE)
            dk_ref[pl.ds(ks, blk), :] += _tn(ds, qv) * scale
            dv_ref[pl.ds(ks, blk), :] += _tn(p.astype(MXU_DTYPE), dyv)
            return dq + _nn(ds, kv)

        dq = lax.fori_loop(0, qi, lambda i, cr: tile(i, cr, False), jnp.zeros((blk, dk), F32))
        dq_ref[...] = tile(qi, dq, True) * scale

    qspec = pl.BlockSpec((None, blk, dk), lambda hh, i: (hh, i, 0))
    kfull = pl.BlockSpec((None, s, dk), lambda hh, i: (hh, 0, 0))
    vfull = pl.BlockSpec((None, s, dv), lambda hh, i: (hh, 0, 0))
    yspec = pl.BlockSpec((None, blk, dv), lambda hh, i: (hh, i, 0))
    lspec = pl.BlockSpec((None, blk, LANES), lambda hh, i: (hh, i, 0))
    return _call(body, name=name,
                 out_shape=[jax.ShapeDtypeStruct((h, s, dk), F32), jax.ShapeDtypeStruct((h, s, dk), F32),
                            jax.ShapeDtypeStruct((h, s, dv), F32)],
                 grid=(h, s // blk), in_specs=[qspec, kfull, vfull, yspec, yspec, lspec],
                 out_specs=[qspec, kfull, vfull], sem=("parallel", "arbitrary"))(q, k, v, y, dy, lse)


HALO = 8


def _conv_specs(s, c, bs, cw, col_of):
    blk = pl.BlockSpec((bs, cw), lambda j, i: (i, col_of(j)))
    halo = pl.BlockSpec((HALO, cw), lambda j, i: (jnp.maximum(i * (bs // HALO) - 1, 0), col_of(j)))
    return blk, halo


def _stage(scr, x_ref, halo_ref, first):
    scr[0:HALO, :] = jnp.where(first, 0.0, halo_ref[...])
    scr[HALO:, :] = x_ref[...]


def _conv_taps(scr, kk, bs):
    return [scr[pl.ds(HALO - (kk - 1) + k, bs), :] for k in range(kk)]


def _conv_sum(taps, w_ref, b_ref):
    u = b_ref[...] + taps[0] * w_ref[0:1, :]
    for k in range(1, len(taps)):
        u = u + taps[k] * w_ref[k:k + 1, :]
    return u


def conv_silu_fwd(x, w, b, *, name):
    s, c = x.shape
    kk = w.shape[0]
    bs, cw = _tile(s, (ROW_BLK,)), _tile(c, (CONV_COLS,))

    def body(x_ref, h_ref, w_ref, b_ref, o_ref, scr):
        _stage(scr, x_ref, h_ref, pl.program_id(1) == 0)
        u = _conv_sum(_conv_taps(scr, kk, bs), w_ref, b_ref)
        o_ref[...] = u * _sigmoid(u)

    blk, halo = _conv_specs(s, c, bs, cw, lambda j: j)
    wspec = pl.BlockSpec((kk, cw), lambda j, i: (0, j))
    bspec = pl.BlockSpec((1, cw), lambda j, i: (0, j))
    return _call(body, name=name, out_shape=jax.ShapeDtypeStruct((s, c), F32), grid=(c // cw, s // bs),
                 in_specs=[blk, halo, wspec, bspec], out_specs=blk, scratch=[pltpu.VMEM((bs + HALO, cw), F32)],
                 sem=("parallel", "arbitrary"))(x, x, w, b.reshape(1, c))


def conv_silu_bwd(x, dy, w, b, *, name):
    s, c = x.shape
    kk = w.shape[0]
    bs, cw = _tile(s, (ROW_BLK,)), _tile(c, (CONV_COLS,))

    def body(x_ref, h_ref, w_ref, b_ref, dy_ref, du_ref, dw_ref, db_ref, scr):
        i = pl.program_id(1)

        @pl.when(i == 0)
        def _():
            dw_ref[...] = jnp.zeros_like(dw_ref)
            db_ref[...] = jnp.zeros_like(db_ref)

        _stage(scr, x_ref, h_ref, i == 0)
        taps = _conv_taps(scr, kk, bs)
        u = _conv_sum(taps, w_ref, b_ref)
        sg = _sigmoid(u)
        du = dy_ref[...] * (sg * (1.0 + u * (1.0 - sg)))
        du_ref[...] = du
        for k in range(kk):
            dw_ref[k:k + 1, :] += jnp.sum(du * taps[k], axis=0, keepdims=True)
        db_ref[...] += jnp.sum(du, axis=0, keepdims=True)

    blk, halo = _conv_specs(s, c, bs, cw, lambda j: j)
    wspec = pl.BlockSpec((kk, cw), lambda j, i: (0, j))
    bspec = pl.BlockSpec((1, cw), lambda j, i: (0, j))
    return _call(body, name=name,
                 out_shape=[jax.ShapeDtypeStruct((s, c), F32), jax.ShapeDtypeStruct((kk, c), F32),
                            jax.ShapeDtypeStruct((1, c), F32)],
                 grid=(c // cw, s // bs), in_specs=[blk, halo, wspec, bspec, blk], out_specs=[blk, wspec, bspec],
                 scratch=[pltpu.VMEM((bs + HALO, cw), F32)], sem=("parallel", "arbitrary"))(x, x, w, b.reshape(1, c), dy)


def conv_glu_fwd(x, w, b, *, name):
    s, c = x.shape
    kk = w.shape[0]
    half = c // 2
    bs, cw = _tile(s, (ROW_BLK,)), _tile(half, (CONV_COLS,))
    nj = half // cw

    def body(g_ref, gh_ref, v_ref, vh_ref, wg_ref, wv_ref, bg_ref, bv_ref, o_ref, gscr, vscr):
        first = pl.program_id(1) == 0
        _stage(gscr, g_ref, gh_ref, first)
        _stage(vscr, v_ref, vh_ref, first)
        gate = _conv_sum(_conv_taps(gscr, kk, bs), wg_ref, bg_ref)
        val = _conv_sum(_conv_taps(vscr, kk, bs), wv_ref, bv_ref)
        o_ref[...] = (gate * _sigmoid(gate) * val).astype(o_ref.dtype)

    gblk, ghalo = _conv_specs(s, c, bs, cw, lambda j: j)
    vblk, vhalo = _conv_specs(s, c, bs, cw, lambda j: j + nj)
    wg = pl.BlockSpec((kk, cw), lambda j, i: (0, j))
    wv = pl.BlockSpec((kk, cw), lambda j, i: (0, j + nj))
    bg = pl.BlockSpec((1, cw), lambda j, i: (0, j))
    bv = pl.BlockSpec((1, cw), lambda j, i: (0, j + nj))
    b2 = b.reshape(1, c)
    return _call(body, name=name, out_shape=jax.ShapeDtypeStruct((s, half), MXU_DTYPE), grid=(nj, s // bs),
                 in_specs=[gblk, ghalo, vblk, vhalo, wg, wv, bg, bv], out_specs=pl.BlockSpec((bs, cw), lambda j, i: (i, j)),
                 scratch=[pltpu.VMEM((bs + HALO, cw), F32)] * 2,
                 sem=("parallel", "arbitrary"))(x, x, x, x, w, w, b2, b2)


def conv_glu_bwd(x, da, w, b, *, name):
    s, c = x.shape
    kk = w.shape[0]
    half = c // 2
    bs, cw = _tile(s, (ROW_BLK,)), _tile(half, (CONV_COLS,))
    nj = half // cw

    def body(o_ref, oh_ref, p_ref, ph_ref, wo_ref, wp_ref, bo_ref, bp_ref, da_ref, du_ref, dw_ref, db_ref, oscr, pscr):
        j, i = pl.program_id(0), pl.program_id(1)

        @pl.when(i == 0)
        def _():
            dw_ref[...] = jnp.zeros_like(dw_ref)
            db_ref[...] = jnp.zeros_like(db_ref)

        _stage(oscr, o_ref, oh_ref, i == 0)
        _stage(pscr, p_ref, ph_ref, i == 0)
        taps = _conv_taps(oscr, kk, bs)
        own = _conv_sum(taps, wo_ref, bo_ref)
        par = _conv_sum(_conv_taps(pscr, kk, bs), wp_ref, bp_ref)
        is_gate = j < nj
        gate = jnp.where(is_gate, own, par)
        val = jnp.where(is_gate, par, own)
        sg = _sigmoid(gate)
        du = da_ref[...] * jnp.where(is_gate, val * (sg * (1.0 + gate * (1.0 - sg))), gate * sg)
        du_ref[...] = du
        for k in range(kk):
            dw_ref[k:k + 1, :] += jnp.sum(du * taps[k], axis=0, keepdims=True)
        db_ref[...] += jnp.sum(du, axis=0, keepdims=True)

    oblk, ohalo = _conv_specs(s, c, bs, cw, lambda j: j)
    pblk, phalo = _conv_specs(s, c, bs, cw, lambda j: (j + nj) % (2 * nj))
    wo = pl.BlockSpec((kk, cw), lambda j, i: (0, j))
    wp = pl.BlockSpec((kk, cw), lambda j, i: (0, (j + nj) % (2 * nj)))
    bo = pl.BlockSpec((1, cw), lambda j, i: (0, j))
    bp = pl.BlockSpec((1, cw), lambda j, i: (0, (j + nj) % (2 * nj)))
    daspec = pl.BlockSpec((bs, cw), lambda j, i: (i, j % nj))
    b2 = b.reshape(1, c)
    return _call(body, name=name,
                 out_shape=[jax.ShapeDtypeStruct((s, c), F32), jax.ShapeDtypeStruct((kk, c), F32),
                            jax.ShapeDtypeStruct((1, c), F32)],
                 grid=(2 * nj, s // bs), in_specs=[oblk, ohalo, pblk, phalo, wo, wp, bo, bp, daspec],
                 out_specs=[oblk, wo, bo], scratch=[pltpu.VMEM((bs + HALO, cw), F32)] * 2,
                 sem=("parallel", "arbitrary"))(x, x, x, x, w, w, b2, b2, da)


def conv_t(du, w, *, name):
    s, c = du.shape
    kk = w.shape[0]
    bs, cw = _tile(s, (ROW_BLK,)), _tile(c, (CONV_COLS,))
    nb = s // bs

    def body(d_ref, h_ref, w_ref, o_ref, scr):
        last = pl.program_id(1) == nb - 1
        scr[0:bs, :] = d_ref[...]
        scr[bs:, :] = jnp.where(last, 0.0, h_ref[...])
        acc = scr[pl.ds(kk - 1, bs), :] * w_ref[0:1, :]
        for k in range(1, kk):
            acc = acc + scr[pl.ds(kk - 1 - k, bs), :] * w_ref[k:k + 1, :]
        o_ref[...] = acc

    blk = pl.BlockSpec((bs, cw), lambda j, i: (i, j))
    halo = pl.BlockSpec((HALO, cw), lambda j, i: (jnp.minimum((i + 1) * (bs // HALO), s // HALO - 1), j))
    wspec = pl.BlockSpec((kk, cw), lambda j, i: (0, j))
    return _call(body, name=name, out_shape=jax.ShapeDtypeStruct((s, c), F32), grid=(c // cw, nb),
                 in_specs=[blk, halo, wspec], out_specs=blk, scratch=[pltpu.VMEM((bs + HALO, cw), F32)],
                 sem=("parallel", "arbitrary"))(du, du, w)


def _ssd_common(xbc_ref, tail_ref, dtrt_ref, bias_ref, biast_ref, alog_ref, alogt_ref):
    L = SSM_CHUNK
    raw = tail_ref[...] + bias_ref[...]
    dt = _softplus(raw)
    dtt = _softplus(dtrt_ref[...] + biast_ref[...])
    a = -jnp.exp(alog_ref[...])
    at = -jnp.exp(alogt_ref[...])
    row, col, lower = _tri(L, lambda r, c: r >= c)
    tril = lower.astype(F32)
    cs = _nn(tril, dt * a, HIGHEST)
    cst = _nt(dtt * at, tril, HIGHEST)
    bm = [xbc_ref[:, SSM_INNER + g * SSM_N: SSM_INNER + (g + 1) * SSM_N] for g in range(SSM_GROUPS)]
    off = SSM_INNER + SSM_GROUPS * SSM_N
    cm = [xbc_ref[:, off + g * SSM_N: off + (g + 1) * SSM_N] for g in range(SSM_GROUPS)]
    cb = [_nt(cm[g], bm[g], HIGHEST) for g in range(SSM_GROUPS)]
    return raw, dt, a, lower, tril, cs, cst, bm, cm, cb


def _ssd_head(hh, xbc_ref, dt, cs, cst, lower):
    L = SSM_CHUNK
    ln = DT_LANE + hh
    x = xbc_ref[:, hh * SSM_P:(hh + 1) * SSM_P]
    dtc = dt[:, ln:ln + 1]
    csc = cs[:, ln:ln + 1]
    csr = cst[hh:hh + 1, :]
    decay = jnp.exp(jnp.where(lower, csc - csr, -1e30))
    last = cs[L - 1:L, ln:ln + 1]
    return x, dtc, csc, decay, jnp.exp(csc), jnp.exp(last - csc), jnp.exp(last)


def _ssd_inputs(tail, dt_bias, a_log, d_skip):
    H = SSM_HEADS
    lanes = lambda vec: jnp.pad(vec.reshape(1, H), ((0, 0), (DT_LANE, LANES - DT_LANE - H)))
    return (tail, tail[:, DT_LANE:DT_LANE + H].T, lanes(dt_bias), dt_bias.reshape(H, 1), lanes(a_log),
            a_log.reshape(H, 1), lanes(d_skip))


def ssd_fwd(xbc, tail, dt_bias, a_log, d_skip, *, name):
    s = xbc.shape[0]
    L, H, P, N = SSM_CHUNK, SSM_HEADS, SSM_P, SSM_N
    nc = s // L

    def body(xbc_ref, tail_ref, dtrt_ref, bias_ref, biast_ref, alog_ref, alogt_ref, d_ref, y_ref, hp_ref, state):
        @pl.when(pl.program_id(0) == 0)
        def _():
            state[...] = jnp.zeros_like(state)

        raw, dt, a, lower, tril, cs, cst, bm, cm, cb = _ssd_common(
            xbc_ref, tail_ref, dtrt_ref, bias_ref, biast_ref, alog_ref, alogt_ref)
        for hh in range(H):
            g = hh // (H // SSM_GROUPS)
            x, dtc, csc, decay, e, tau, gamma = _ssd_head(hh, xbc_ref, dt, cs, cst, lower)
            xdt = x * dtc
            hprev = state[hh]
            hp_ref[hh] = hprev
            skip = d_ref[:, DT_LANE + hh:DT_LANE + hh + 1]
            y = _nn(cb[g] * decay, xdt, HIGHEST) + _nn(cm[g], hprev, HIGHEST) * e + x * skip
            y_ref[:, hh * P:(hh + 1) * P] = y
            state[hh] = hprev * gamma + _tn(bm[g] * tau, xdt, HIGHEST)

    row = lambda w: pl.BlockSpec((L, w), lambda c: (c, 0))
    small = lambda shp: pl.BlockSpec(shp, lambda c: (0, 0))
    return _call(body, name=name,
                 out_shape=[jax.ShapeDtypeStruct((s, SSM_INNER), F32), jax.ShapeDtypeStruct((nc, H, N, P), F32)],
                 grid=(nc,),
                 in_specs=[row(SSM_CONV_DIM), row(LANES), pl.BlockSpec((H, L), lambda c: (0, c)), small((1, LANES)),
                           small((H, 1)), small((1, LANES)), small((H, 1)), small((1, LANES))],
                 out_specs=[row(SSM_INNER), pl.BlockSpec((None, H, N, P), lambda c: (c, 0, 0, 0))],
                 scratch=[pltpu.VMEM((H, N, P), F32)], sem=("arbitrary",))(xbc, *_ssd_inputs(tail, dt_bias, a_log, d_skip))


def ssd_bwd(xbc, tail, dt_bias, a_log, d_skip, hprev_all, dy, *, name):
    s = xbc.shape[0]
    L, H, P, N = SSM_CHUNK, SSM_HEADS, SSM_P, SSM_N
    nc = s // L
    hg = H // SSM_GROUPS

    def body(xbc_ref, tail_ref, dtrt_ref, bias_ref, biast_ref, alog_ref, alogt_ref, d_ref, hp_ref, dy_ref,
             dxbc_ref, ddt_ref, dbias_ref, dalog_ref, dd_ref, dstate):
        @pl.when(pl.program_id(0) == 0)
        def _():
            dstate[...] = jnp.zeros_like(dstate)
            dbias_ref[...] = jnp.zeros_like(dbias_ref)
            dalog_ref[...] = jnp.zeros_like(dalog_ref)
            dd_ref[...] = jnp.zeros_like(dd_ref)

        raw, dt, a, lower, tril, cs, cst, bm, cm, cb = _ssd_common(
            xbc_ref, tail_ref, dtrt_ref, bias_ref, biast_ref, alog_ref, alogt_ref)
        lane = lax.broadcasted_iota(jnp.int32, (L, LANES), 1)
        lane1 = lax.broadcasted_iota(jnp.int32, (1, LANES), 1)
        rowi = lax.broadcasted_iota(jnp.int32, (L, 1), 0)
        ones = jnp.ones((L, LANES), F32)
        dcs_all = jnp.zeros((L, LANES), F32)
        ddt_x = jnp.zeros((L, LANES), F32)
        dd_row = jnp.zeros((1, LANES), F32)
        dbm = [jnp.zeros((L, N), F32) for _ in range(SSM_GROUPS)]
        dcm = [jnp.zeros((L, N), F32) for _ in range(SSM_GROUPS)]
        dcb = [jnp.zeros((L, L), F32) for _ in range(SSM_GROUPS)]
        for hh in range(H):
            g = hh // hg
            ln = DT_LANE + hh
            x, dtc, csc, decay, e, tau, gamma = _ssd_head(hh, xbc_ref, dt, cs, cst, lower)
            xdt = x * dtc
            hprev = hp_ref[hh]
            dhn = dstate[hh]
            dyh = dy_ref[:, hh * P:(hh + 1) * P]
            m = cb[g] * decay
            dxdt = _tn(m, dyh, HIGHEST) + _nn(bm[g] * tau, dhn, HIGHEST)
            dm = jnp.where(lower, _nt(dyh, xdt, HIGHEST), 0.0)
            dcb[g] = dcb[g] + dm * decay
            dseg = dm * m
            dcs = jnp.sum(dseg, axis=1, keepdims=True) - _tn(dseg, ones, HIGHEST)[:, 0:1]
            edy = e * dyh
            dcm[g] = dcm[g] + _nt(edy, hprev, HIGHEST)
            dcs = dcs + e * jnp.sum(dyh * _nn(cm[g], hprev, HIGHEST), axis=1, keepdims=True)
            xdh = _nt(xdt, dhn, HIGHEST)
            dbm[g] = dbm[g] + tau * xdh
            dtau_tau = jnp.sum(bm[g] * xdh, axis=1, keepdims=True) * tau
            dlast = jnp.sum(dtau_tau, axis=0, keepdims=True) + gamma * jnp.sum(dhn * hprev, keepdims=True)
            dcs = dcs - dtau_tau + jnp.where(rowi == L - 1, dlast, 0.0)
            dstate[hh] = gamma * dhn + _tn(cm[g], edy, HIGHEST)
            dcs_all = jnp.where(lane == ln, dcs, dcs_all)
            ddt_x = jnp.where(lane == ln, jnp.sum(dxdt * x, axis=1, keepdims=True), ddt_x)
            dxbc_ref[:, hh * P:(hh + 1) * P] = dxdt * dtc + d_ref[:, ln:ln + 1] * dyh
            dd_row = jnp.where(lane1 == ln, jnp.sum(dyh * x, keepdims=True), dd_row)
        off = SSM_INNER + SSM_GROUPS * SSM_N
        for g in range(SSM_GROUPS):
            dxbc_ref[:, SSM_INNER + g * N: SSM_INNER + (g + 1) * N] = dbm[g] + _tn(dcb[g], cm[g], HIGHEST)
            dxbc_ref[:, off + g * N: off + (g + 1) * N] = dcm[g] + _nn(dcb[g], bm[g], HIGHEST)
        dda = _tn(tril, dcs_all, HIGHEST)
        head_lane = (lane >= DT_LANE) & (lane < DT_LANE + H)
        draw = jnp.where(head_lane, (dda * a + ddt_x) * _sigmoid(raw), 0.0)
        ddt_ref[...] = draw
        dbias_ref[...] += jnp.sum(draw, axis=0, keepdims=True)
        dalog_ref[...] += jnp.sum(jnp.where(head_lane, dda * dt, 0.0), axis=0, keepdims=True) * a
        dd_ref[...] += dd_row

    rev = lambda c: nc - 1 - c
    row = lambda w: pl.BlockSpec((L, w), lambda c: (rev(c), 0))
    small = lambda shp: pl.BlockSpec(shp, lambda c: (0, 0))
    acc = pl.BlockSpec((1, LANES), lambda c: (0, 0))
    vec = jax.ShapeDtypeStruct((1, LANES), F32)
    return _call(body, name=name,
                 out_shape=[jax.ShapeDtypeStruct((s, SSM_CONV_DIM), F32), jax.ShapeDtypeStruct((s, LANES), F32), vec, vec, vec],
                 grid=(nc,),
                 in_specs=[row(SSM_CONV_DIM), row(LANES), pl.BlockSpec((H, L), lambda c: (0, rev(c))), small((1, LANES)),
                           small((H, 1)), small((1, LANES)), small((H, 1)), small((1, LANES)),
                           pl.BlockSpec((None, H, N, P), lambda c: (rev(c), 0, 0, 0)), row(SSM_INNER)],
                 out_specs=[row(SSM_CONV_DIM), row(LANES), acc, acc, acc],
                 scratch=[pltpu.VMEM((H, N, P), F32)], sem=("arbitrary",))(
        xbc, *_ssd_inputs(tail, dt_bias, a_log, d_skip), hprev_all, dy)


def _heads(x2d, n, d):
    s = x2d.shape[0]
    return x2d.reshape(s, n, d).transpose(1, 0, 2)


def _unheads(x3d):
    n, s, d = x3d.shape
    return x3d.transpose(1, 0, 2).reshape(s, n * d)


def layer_fwd(h, p, tabs, li):
    s = h.shape[0]
    tabq, tabt = tabs
    nm = lambda t: f"L{li}_{t}"
    r = {'h': h}
    hn = rms_fwd(h, p['mix_norm'], name=nm('mixnorm'), out_dtype=MXU_DTYPE)
    proj = mm(hn, p['w_in'], name=nm('proj'))
    r.update(hn=hn, proj=proj)
    qkv = proj[:, :3 * SB_HEADS * SB_DIM].astype(MXU_DTYPE).reshape(s, 3, SB_HEADS, SB_DIM).transpose(1, 2, 0, 3)
    ya_h, tot = sb_fwd(qkv[0], qkv[1], qkv[2], name=nm('sb_fwd'))
    ya = _unheads(ya_h)
    yan = rms_fwd(ya, p['sb_out_norm'], name=nm('sbnorm'), out_dtype=MXU_DTYPE)
    r.update(qkv=qkv, ya=ya, tot=tot)
    z = proj[:, 768:1280]
    xbc = proj[:, 1280:2048]
    tail = proj[:, TAIL:TAIL + LANES]
    xbc_act = conv_silu_fwd(xbc, p['ssm_conv_w'], p['ssm_conv_b'], name=nm('ssmconv'))
    y_ssm, hprev = ssd_fwd(xbc_act, tail, p['ssm_dt_bias'], p['ssm_a_log'], p['ssm_d'], name=nm('ssd_fwd'))
    ybn = rms_fwd(y_ssm, p['ssm_out_norm'], name=nm('ssmnorm'), gate=z, out_dtype=MXU_DTYPE)
    r.update(z=z, xbc=xbc, tail=tail, xbc_act=xbc_act, y_ssm=y_ssm, hprev=hprev)
    cq = proj[:, 2048:2304]
    ckv = proj[:, 2304:2432]
    qn = rms_fwd(cq, p['mla_q_norm'], name=nm('qnorm'), out_dtype=MXU_DTYPE)
    q_r = rope(mm(qn, p['mla_w_uq'], name=nm('uq'))[None], tabq, name=nm('ropeq'))
    kvn = rms_fwd(ckv, p['mla_kv_norm'], name=nm('kvnorm'), out_dtype=MXU_DTYPE)
    kv = mm(kvn, p['mla_w_ukv'], name=nm('ukv'))
    k_pe = rope(tail[None], tabt, name=nm('ropek'))[:, :MLA_ROPE]
    qh = _heads(q_r, MLA_HEADS, MLA_QK).astype(MXU_DTYPE)
    kvh = _heads(kv, MLA_HEADS, MLA_NOPE + MLA_V)
    kh = jnp.concatenate([kvh[..., :MLA_NOPE], jnp.broadcast_to(k_pe[None], (MLA_HEADS, s, MLA_ROPE))],
                         axis=-1).astype(MXU_DTYPE)
    vh = kvh[..., MLA_NOPE:].astype(MXU_DTYPE)
    yc_h, lse = mla_fwd(qh, kh, vh, name=nm('mla_fwd'))
    yc = _unheads(yc_h)
    ycn = rms_fwd(yc, p['mla_out_norm'], name=nm('mlanorm'), out_dtype=MXU_DTYPE)
    r.update(cq=cq, ckv=ckv, qn=qn, kvn=kvn, qh=qh, kh=kh, vh=vh, yc_h=yc_h, yc=yc, lse=lse)
    ycat = jnp.concatenate([yan, ybn, ycn], axis=1)
    h1 = mm(ycat, p['w_out'], name=nm('outproj'), res=h)
    hn2 = rms_fwd(h1, p['ffn_norm'], name=nm('ffnnorm'), out_dtype=MXU_DTYPE)
    up = mm(hn2, p['ffn_w_up'], name=nm('up'))
    act = conv_glu_fwd(up, p['ffn_conv_w'], p['ffn_conv_b'], name=nm('glu'))
    h2 = mm(act, p['ffn_w_down'], name=nm('down'), res=h1)
    r.update(ycat=ycat, h1=h1, hn2=hn2, up=up, act=act)
    return h2, r


def layer_bwd(dh2, p, r, tabs, li):
    s = dh2.shape[0]
    tabq, tabt = tabs
    nm = lambda t: f"L{li}_{t}"
    g = {}
    dact = mm(dh2, p['ffn_w_down'], name=nm('d_down_x'), tb=True)
    g['ffn_w_down'] = mm(r['act'], dh2, name=nm('d_down_w'), ta=True)
    du, g['ffn_conv_w'], dcb = conv_glu_bwd(r['up'], dact, p['ffn_conv_w'], p['ffn_conv_b'], name=nm('d_glu'))
    g['ffn_conv_b'] = dcb[0]
    dup = conv_t(du, p['ffn_conv_w'], name=nm('d_ffnconv'))
    g['ffn_w_up'] = mm(r['hn2'], dup, name=nm('d_up_w'), ta=True)
    dhn2 = mm(dup, p['ffn_w_up'], name=nm('d_up_x'), tb=True)
    dh1, dg = rms_bwd(r['h1'], p['ffn_norm'], dhn2, name=nm('d_ffnnorm'), add=dh2)
    g['ffn_norm'] = dg[0]
    dycat = mm(dh1, p['w_out'], name=nm('d_out_x'), tb=True)
    g['w_out'] = mm(r['ycat'], dh1, name=nm('d_out_w'), ta=True)
    dya, dg = rms_bwd(r['ya'], p['sb_out_norm'], dycat[:, :256], name=nm('d_sbnorm'))
    g['sb_out_norm'] = dg[0]
    qkv = r['qkv']
    dq, dk, dv = sb_bwd(qkv[0], qkv[1], qkv[2], _heads(dya, SB_HEADS, SB_DIM), r['tot'], name=nm('sb_bwd'))
    dsb = jnp.stack([dq, dk, dv]).transpose(2, 0, 1, 3).reshape(s, 3 * SB_HEADS * SB_DIM)
    dyssm, dz, dg = rms_bwd(r['y_ssm'], p['ssm_out_norm'], dycat[:, 256:768], name=nm('d_ssmnorm'), gate=r['z'])
    g['ssm_out_norm'] = dg[0]
    dxbc_act, ddt_tail, dbias, dalog, dd = ssd_bwd(r['xbc_act'], r['tail'], p['ssm_dt_bias'], p['ssm_a_log'],
                                                   p['ssm_d'], r['hprev'], dyssm, name=nm('ssd_bwd'))
    hl = slice(DT_LANE, DT_LANE + SSM_HEADS)
    g['ssm_dt_bias'], g['ssm_a_log'], g['ssm_d'] = dbias[0, hl], dalog[0, hl], dd[0, hl]
    dxbc_u, g['ssm_conv_w'], dcb = conv_silu_bwd(r['xbc'], dxbc_act, p['ssm_conv_w'], p['ssm_conv_b'], name=nm('d_ssmact'))
    g['ssm_conv_b'] = dcb[0]
    dxbc = conv_t(dxbc_u, p['ssm_conv_w'], name=nm('d_ssmconv'))
    dyc, dg = rms_bwd(r['yc'], p['mla_out_norm'], dycat[:, 768:], name=nm('d_mlanorm'))
    g['mla_out_norm'] = dg[0]
    dqh, dkh, dvh = mla_bwd(r['qh'], r['kh'], r['vh'], r['yc_h'], _heads(dyc, MLA_HEADS, MLA_V), r['lse'], name=nm('mla_bwd'))
    dq_c = rope(_unheads(dqh)[None], tabq, name=nm('d_ropeq'), backward=True)
    g['mla_w_uq'] = mm(r['qn'], dq_c, name=nm('d_uq_w'), ta=True)
    dcq, dg = rms_bwd(r['cq'], p['mla_q_norm'], mm(dq_c, p['mla_w_uq'], name=nm('d_uq_x'), tb=True), name=nm('d_qnorm'))
    g['mla_q_norm'] = dg[0]
    dkv = _unheads(jnp.concatenate([dkh[..., :MLA_NOPE], dvh], axis=-1))
    g['mla_w_ukv'] = mm(r['kvn'], dkv, name=nm('d_ukv_w'), ta=True)
    dckv, dg = rms_bwd(r['ckv'], p['mla_kv_norm'], mm(dkv, p['mla_w_ukv'], name=nm('d_ukv_x'), tb=True), name=nm('d_kvnorm'))
    g['mla_kv_norm'] = dg[0]
    dkpe = jnp.pad(dkh[..., MLA_NOPE:], ((0, 0), (0, 0), (0, LANES - MLA_ROPE)))
    dtail = rope(dkpe, tabt, name=nm('d_ropek'), backward=True, add=ddt_tail)
    dproj = jnp.concatenate([dsb, dz, dxbc, dcq, dckv, dtail], axis=1)
    g['w_in'] = mm(r['hn'], dproj, name=nm('d_proj_w'), ta=True)
    dhn = mm(dproj, p['w_in'], name=nm('d_proj_x'), tb=True)
    dh, dg = rms_bwd(r['h'], p['mix_norm'], dhn, name=nm('d_mixnorm'), add=dh1)
    g['mix_norm'] = dg[0]
    return dh, g


def pad_w_in(w):
    zeros = jnp.zeros(w.shape[:-1] + (D_IN_PAD - D_IN,), w.dtype)
    return jnp.concatenate([w[..., :2048], w[..., 2056:], w[..., 2048:2056], zeros], axis=-1)


def unpad_w_in(w):
    return jnp.concatenate([w[..., :2048], w[..., 2464:2472], w[..., 2048:2464]], axis=-1)


def local_step(x, positions, target, params):
    s = x.shape[0]
    tabs = _rope_tables(positions, s)
    h = x
    saved = []
    for li in range(DEPTH):
        p = {n: params[n][li] for n in WEIGHTS if n != 'final_norm'}
        h, r = layer_fwd(h, p, tabs, li)
        saved.append((p, r))
    y = rms_fwd(h, params['final_norm'], name='finalnorm')
    dy, loss = loss_head(y, target, name='loss')
    dh, dg = rms_bwd(h, params['final_norm'], dy, name='d_finalnorm')
    grads = {'final_norm': dg[0]}
    per_layer = []
    for li in reversed(range(DEPTH)):
        p, r = saved[li]
        dh, g = layer_bwd(dh, p, r, tabs, li)
        per_layer.append(g)
    per_layer.reverse()
    for n in WEIGHTS:
        if n != 'final_norm':
            grads[n] = jnp.stack([per_layer[li][n] for li in range(DEPTH)])
    grads['w_in'] = unpad_w_in(grads['w_in'])
    return loss[0, 0], dh, grads


MESH = pl.DeviceIdType.MESH
HBM = pl.BlockSpec(memory_space=pltpu.HBM)


def _flip(v, bit):
    return 1 - v if bit else v


def all_gather(block, *, name):
    def body(x_ref, out_ref, send_sems, recv_sems, local_sem):
        x, y, c = lax.axis_index("x"), lax.axis_index("y"), lax.axis_index("c")
        me, sibling = (x, y, c), (x, y, 1 - c)
        chips = [(1 - x, y), (x, 1 - y), (1 - x, 1 - y)]

        def slot(px, py, pc):
            return out_ref.at[4 * px + 2 * py + pc]

        def copy(k, blk, to, src=None):
            return pltpu.make_async_remote_copy(src_ref=slot(*blk) if src is None else src, dst_ref=slot(*blk),
                                                send_sem=send_sems.at[k], recv_sem=recv_sems.at[k],
                                                device_id=to, device_id_type=MESH)

        mine = pltpu.make_async_copy(x_ref, slot(*me), local_sem)
        mine.start()
        first = [copy(0, me, sibling, src=x_ref)]
        first += [copy(1 + j, me, (*chip, c), src=x_ref) for j, chip in enumerate(chips)]
        for cp in first:
            cp.start()
        passed = [copy(4 + j, (*chip, c), sibling) for j, chip in enumerate(chips)]
        for j, chip in enumerate(chips):
            copy(1 + j, (*chip, c), me).wait_recv()
            passed[j].start()
        copy(0, sibling, me).wait_recv()
        for j, chip in enumerate(chips):
            copy(4 + j, (*chip, 1 - c), me).wait_recv()
        for cp in first + passed:
            cp.wait_send()
        mine.wait()

    return pl.pallas_call(
        body, name=name, out_shape=jax.ShapeDtypeStruct((N_DEV,) + block.shape, block.dtype),
        in_specs=[HBM], out_specs=HBM,
        scratch_shapes=[pltpu.SemaphoreType.DMA((7,)), pltpu.SemaphoreType.DMA((7,)), pltpu.SemaphoreType.DMA],
    )(block)


def all_to_all(parts, *, name):
    def body(g_ref, r_ref, send_sems, recv_sems, local_sem):
        x, y, c = lax.axis_index("x"), lax.axis_index("y"), lax.axis_index("c")
        me = 4 * x + 2 * y + c
        mine = pltpu.make_async_copy(g_ref.at[me], r_ref.at[me], local_sem)
        mine.start()
        copies = []
        for k in range(1, N_DEV):
            px, py, pc = _flip(x, k & 4), _flip(y, k & 2), _flip(c, k & 1)
            peer = 4 * px + 2 * py + pc
            cp = pltpu.make_async_remote_copy(src_ref=g_ref.at[peer], dst_ref=r_ref.at[me], send_sem=send_sems.at[k - 1],
                                              recv_sem=recv_sems.at[k - 1], device_id=(px, py, pc), device_id_type=MESH)
            cp.start()
            copies.append(cp)
        for cp in copies:
            cp.wait_send()
            cp.wait_recv()
        mine.wait()

    return pl.pallas_call(
        body, name=name, out_shape=jax.ShapeDtypeStruct(parts.shape, parts.dtype), in_specs=[HBM], out_specs=HBM,
        scratch_shapes=[pltpu.SemaphoreType.DMA((7,)), pltpu.SemaphoreType.DMA((7,)), pltpu.SemaphoreType.DMA],
    )(parts)


def adamw(parts, w, m, v, *, name):
    r, wd = w.shape
    br = _tile(r, (256, 128, 64, 32, 16, 8))
    c1 = 1.0 - ADAM_B1 ** ADAM_STEP
    c2 = 1.0 - ADAM_B2 ** ADAM_STEP

    def body(p_ref, w_ref, m_ref, v_ref, g_ref, d_ref, mo_ref, vo_ref):
        g = p_ref[0]
        for j in range(1, N_DEV):
            g = g + p_ref[j]
        mn = ADAM_B1 * m_ref[...] + (1.0 - ADAM_B1) * g
        vn = ADAM_B2 * v_ref[...] + (1.0 - ADAM_B2) * (g * g)
        g_ref[...] = g
        mo_ref[...] = mn
        vo_ref[...] = vn
        d_ref[...] = -ADAM_LR * ((mn / c1) / (jnp.sqrt(vn / c2) + ADAM_EPS) + ADAM_WD * w_ref[...])

    blk = pl.BlockSpec((br, wd), lambda i: (i, 0))
    out = jax.ShapeDtypeStruct((r, wd), F32)
    return _call(body, name=name, out_shape=[out] * 4, grid=(r // br,),
                 in_specs=[pl.BlockSpec((N_DEV, br, wd), lambda i: (0, i, 0)), blk, blk, blk], out_specs=[blk] * 4,
                 sem=("parallel",))(parts, w, m, v)


def _pack(arrs, rows_mult):
    lead = arrs[0].shape[:-1]
    flat = jnp.concatenate(arrs, axis=-1)
    n = flat.shape[-1]
    rows = -(-n // FLAT_W)
    rows = -(-rows // rows_mult) * rows_mult
    flat = jnp.pad(flat, [(0, 0)] * len(lead) + [(0, rows * FLAT_W - n)])
    return flat.reshape(lead + (rows, FLAT_W))


def _unpack(flat, shapes):
    lead = flat.shape[:-2]
    flat = flat.reshape(lead + (-1,))
    out, off = [], 0
    for shp in shapes:
        n = int(np.prod(shp))
        out.append(flat[..., off:off + n].reshape(lead + tuple(shp)))
        off += n
    return out


def _shard_split(full, axis):
    shp = full.shape
    return jnp.moveaxis(full.reshape(shp[:axis] + (N_DEV, shp[axis] // N_DEV) + shp[axis + 1:]), axis, 0)


def _shard_join(parts, axis):
    moved = jnp.moveaxis(parts, 0, axis)
    shp = moved.shape
    return moved.reshape(shp[:axis] + (shp[axis] * shp[axis + 1],) + shp[axis + 2:])


def kernel(x, positions, mix_norm, w_in, sb_out_norm, ssm_conv_w, ssm_conv_b, ssm_dt_bias, ssm_a_log, ssm_d, ssm_out_norm, mla_q_norm, mla_w_uq, mla_kv_norm, mla_w_ukv, mla_out_norm, w_out, ffn_norm, ffn_w_up, ffn_conv_w, ffn_conv_b, ffn_w_down, final_norm, loss_target, m_mix_norm, m_w_in, m_sb_out_norm, m_ssm_conv_w, m_ssm_conv_b, m_ssm_dt_bias, m_ssm_a_log, m_ssm_d, m_ssm_out_norm, m_mla_q_norm, m_mla_w_uq, m_mla_kv_norm, m_mla_w_ukv, m_mla_out_norm, m_w_out, m_ffn_norm, m_ffn_w_up, m_ffn_conv_w, m_ffn_conv_b, m_ffn_w_down, m_final_norm, v_mix_norm, v_w_in, v_sb_out_norm, v_ssm_conv_w, v_ssm_conv_b, v_ssm_dt_bias, v_ssm_a_log, v_ssm_d, v_ssm_out_norm, v_mla_q_norm, v_mla_w_uq, v_mla_kv_norm, v_mla_w_ukv, v_mla_out_norm, v_w_out, v_ffn_norm, v_ffn_w_up, v_ffn_conv_w, v_ffn_conv_b, v_ffn_w_down, v_final_norm):
    args = locals()
    w = {n: args[n] for n in WEIGHTS}
    m = {n: args['m_' + n] for n in WEIGHTS}
    v = {n: args['v_' + n] for n in WEIGHTS}
    sharded = list(SHARDED)

    def wire(n):
        if n in F32_GATHER:
            return lax.bitcast_convert_type(w[n].reshape(-1), BF16).reshape(-1)
        return w[n].astype(BF16).reshape(-1)

    wire_shapes = [(w[n].size * (2 if n in F32_GATHER else 1),) for n in sharded]
    gathered = all_gather(_pack([wire(n) for n in sharded], 32), name='gather_weights')
    pieces = _unpack(gathered, wire_shapes)
    params = {n: w[n] for n in REPLICATED}
    for n, piece in zip(sharded, pieces):
        if n in F32_GATHER:
            piece = lax.bitcast_convert_type(piece.reshape(N_DEV, -1, 2), F32)
        params[n] = _shard_join(piece.reshape((N_DEV,) + w[n].shape), SHARDED[n])
    params['w_in'] = pad_w_in(params['w_in'])

    loss, dx, grads = local_step(x[0], positions[0], loss_target[0], params)
    loss = lax.psum(loss, ("x", "y", "c"))

    shard_shapes = [w[n].shape for n in sharded]
    parts = _pack([_shard_split(grads[n], SHARDED[n]).reshape(N_DEV, -1) for n in sharded], 8)
    recv = all_to_all(parts, name='scatter_grads')
    flat = lambda d: _pack([d[n].reshape(-1) for n in sharded], 8)
    outs = adamw(recv, flat(w), flat(m), flat(v), name='adamw_sharded')
    res = {kind: dict(zip(sharded, _unpack(o, shard_shapes))) for kind, o in zip(('g', 'd', 'm', 'v'), outs)}

    rep_shapes = [w[n].shape for n in REPLICATED]
    rparts = all_gather(_pack([grads[n].reshape(-1) for n in REPLICATED], 8), name='gather_small_grads')
    rflat = lambda d: _pack([d[n].reshape(-1) for n in REPLICATED], 8)
    routs = adamw(rparts, rflat(w), rflat(m), rflat(v), name='adamw_replicated')
    for kind, o in zip(('g', 'd', 'm', 'v'), routs):
        res[kind].update(zip(REPLICATED, _unpack(o, rep_shapes)))

    return (loss, dx[None], *[res['g'][n] for n in WEIGHTS], *[res['d'][n] for n in WEIGHTS],
            *[res['m'][n] for n in WEIGHTS], *[res['v'][n] for n in WEIGHTS])
```

```python
import numpy as np
import jax
import jax.numpy as jnp
from jax import lax
from jax.experimental import pallas as pl
from jax.experimental.pallas import tpu as pltpu

F32 = jnp.float32
BF16 = jnp.bfloat16
MXU_DTYPE = jnp.bfloat16
HIGHEST = lax.Precision.HIGHEST

N_DEV = 8
D_MODEL = 1024
DEPTH = 2
EPS = 1e-6
SB_HEADS, SB_DIM = 4, 64
SB_WIDTH = SB_HEADS * SB_DIM
SSM_HEADS, SSM_P, SSM_GROUPS, SSM_N, SSM_CONV, SSM_CHUNK = 8, 64, 2, 64, 4, 128
SSM_INNER = SSM_HEADS * SSM_P
SSM_CONV_DIM = SSM_INNER + 2 * SSM_GROUPS * SSM_N
MLA_HEADS, MLA_NOPE, MLA_ROPE, MLA_V, MLA_Q_RANK, MLA_KV_RANK = 4, 64, 32, 64, 256, 128
MLA_QK = MLA_NOPE + MLA_ROPE
ROPE_THETA = 10000.0
D_IN = 2472
D_IN_PAD = 2560
TAIL = 2432
DT_LANE = 32
D_FF = 2816
FF_SHARD = 2 * D_FF // N_DEV
ADAM_LR, ADAM_B1, ADAM_B2, ADAM_EPS, ADAM_WD, ADAM_STEP = 0.001, 0.9, 0.999, 1e-08, 0.01, 10

LANES = 128
ATT_BLK = 256
ROW_BLK = 512
CONV_COLS = 256
FLAT_W = 1024
VMEM_LIMIT = 56 << 20
MM_TM = (1024, 512, 256, 128)
MM_TN = (1280, 1024, 768, 640, 512, 384, 256, 128)
MM_TK = (1280, 1024, 512, 256, 128)

WEIGHTS = ['mix_norm', 'w_in', 'sb_out_norm', 'ssm_conv_w', 'ssm_conv_b', 'ssm_dt_bias', 'ssm_a_log', 'ssm_d',
           'ssm_out_norm', 'mla_q_norm', 'mla_w_uq', 'mla_kv_norm', 'mla_w_ukv', 'mla_out_norm', 'w_out',
           'ffn_norm', 'ffn_w_up', 'ffn_conv_w', 'ffn_conv_b', 'ffn_w_down', 'final_norm']
SHARDED = {'w_in': 2, 'ssm_conv_w': 2, 'mla_w_uq': 2, 'mla_w_ukv': 2, 'w_out': 1, 'ffn_w_up': 2, 'ffn_conv_w': 2,
           'ffn_w_down': 1}
VPU_WEIGHTS = ('ssm_conv_w', 'ffn_conv_w')
REPLICATED = [n for n in WEIGHTS if n not in SHARDED]


def _call(body, *, name, out_shape, grid=(), in_specs=None, out_specs=None, scratch=(), sem=None, **kw):
    params = dict(vmem_limit_bytes=VMEM_LIMIT)
    if sem is not None:
        params['dimension_semantics'] = sem
    return pl.pallas_call(body, name=name, out_shape=out_shape, grid=grid, in_specs=in_specs, out_specs=out_specs,
                          scratch_shapes=list(scratch), compiler_params=pltpu.CompilerParams(**params), **kw)


def _tile(n, prefs):
    for t in prefs:
        if n % t == 0:
            return t
    return n


def _dot(a, b, dims, precision=None):
    return lax.dot_general(a, b, (dims, ((), ())), preferred_element_type=F32, precision=precision)


def _nn(a, b, precision=None):
    return _dot(a, b, ((1,), (0,)), precision)


def _nt(a, b, precision=None):
    return _dot(a, b, ((1,), (1,)), precision)


def _tn(a, b, precision=None):
    return _dot(a, b, ((0,), (0,)), precision)


def _split2(x):
    hi = x.astype(MXU_DTYPE)
    lo = (x - hi.astype(F32)).astype(MXU_DTYPE)
    return hi, lo


def _sigmoid(x):
    return 1.0 / (1.0 + jnp.exp(-x))


def _softplus(x):
    return jnp.maximum(x, 0.0) + jnp.log1p(jnp.exp(-jnp.abs(x)))


def _softplus_att(x):
    return jnp.maximum(x, 0.0) + jnp.log(1.0 + jnp.exp(-jnp.abs(x)))


def _cum(x, u2):
    return _nn(jnp.concatenate(_split2(x), axis=1), u2)


def _causal_loop(qi, tile, carry, reverse):
    def two(i, cr):
        kb = qi - 1 - 2 * i if reverse else 2 * i
        return tile(kb - 1 if reverse else kb + 1, tile(kb, cr, False), False)

    def rest(cr):
        cr = lax.fori_loop(0, qi // 2, two, cr)
        return lax.cond(qi % 2 == 1, lambda c: tile(0 if reverse else qi - 1, c, False), lambda c: c, cr)

    return rest(tile(qi, carry, True)) if reverse else tile(qi, rest(carry), True)


def mm(a, b, *, name, ta=False, tb=False, res=None, out_dtype=F32, ab=None, bb=None, precision=None):
    a2, b2 = a.shape[-2:], b.shape[-2:]
    (kdim, m) = a2 if ta else a2[::-1]
    (n, k2) = b2 if tb else b2[::-1]
    assert kdim == k2, (a.shape, b.shape, ta, tb)
    assert (ab == 'k') == (bb == 'k')
    kb = ab == 'k'
    nb = a.shape[0] if ab == 'o' else (b.shape[0] if bb == 'o' else None)
    tm, tn = _tile(m, MM_TM), _tile(n, MM_TN)
    tk = kdim if kb else _tile(kdim, MM_TK)
    nk = a.shape[0] if kb else kdim // tk
    dims = ((0 if ta else 1,), (1 if tb else 0,))
    op_dtype = F32 if precision is not None else MXU_DTYPE

    def body(*refs):
        a_ref, b_ref = refs[0], refs[1]
        r_ref = refs[2] if res is not None else None
        o_ref = refs[3] if res is not None else refs[2]
        part = _dot(a_ref[...].astype(op_dtype), b_ref[...].astype(op_dtype), dims, precision)

        def finish(out):
            if res is not None:
                out = out + r_ref[...]
            o_ref[...] = out.astype(out_dtype)

        if nk == 1:
            finish(part)
            return
        acc = refs[-1]
        k = pl.program_id(3)

        @pl.when(k == 0)
        def _():
            acc[...] = part

        @pl.when(k > 0)
        def _():
            acc[...] += part

        @pl.when(k == nk - 1)
        def _():
            finish(acc[...])

    def spec(blk, idx, how):
        if how is None:
            return pl.BlockSpec(blk, idx)
        if how == 'o':
            return pl.BlockSpec((None,) + blk, lambda p, i, j, k: (p,) + idx(p, i, j, k))
        return pl.BlockSpec((None,) + blk, lambda p, i, j, k: (k,) + idx(p, i, j, 0))

    a_spec = spec((tk, tm), lambda p, i, j, k: (k, i), ab) if ta else spec((tm, tk), lambda p, i, j, k: (i, k), ab)
    b_spec = spec((tn, tk), lambda p, i, j, k: (j, k), bb) if tb else spec((tk, tn), lambda p, i, j, k: (k, j), bb)
    o_spec = spec((tm, tn), lambda p, i, j, k: (i, j), None if nb is None else 'o')
    ins, specs = [a, b], [a_spec, b_spec]
    if res is not None:
        ins.append(res)
        specs.append(o_spec)
    out_shape = (m, n) if nb is None else (nb, m, n)
    return _call(body, name=name, out_shape=jax.ShapeDtypeStruct(out_shape, out_dtype),
                 grid=(1 if nb is None else nb, m // tm, n // tn, nk), in_specs=specs, out_specs=o_spec,
                 scratch=[] if nk == 1 else [pltpu.VMEM((tm, tn), F32)],
                 sem=("parallel", "parallel", "parallel", "arbitrary"))(*ins)


def rms_fwd(x, g, *, name, gate=None, out_dtype=F32):
    s, w = x.shape
    bs = _tile(s, (ROW_BLK,))

    def body(*refs):
        if gate is None:
            x_ref, g_ref, o_ref = refs
            u = x_ref[...]
        else:
            x_ref, z_ref, g_ref, o_ref = refs
            z = z_ref[...]
            u = x_ref[...] * (z * _sigmoid(z))
        r = lax.rsqrt(jnp.mean(u * u, axis=1, keepdims=True) + EPS)
        o_ref[...] = (u * r * g_ref[...]).astype(out_dtype)

    row = pl.BlockSpec((bs, w), lambda i: (i, 0))
    vec = pl.BlockSpec((1, w), lambda i: (0, 0))
    ins = [x] + ([] if gate is None else [gate]) + [g.reshape(1, w)]
    specs = [row] + ([] if gate is None else [row]) + [vec]
    return _call(body, name=name, out_shape=jax.ShapeDtypeStruct((s, w), out_dtype), grid=(s // bs,),
                 in_specs=specs, out_specs=row, sem=("parallel",))(*ins)


def rms_bwd(x, g, dy, *, name, gate=None, add=None):
    s, w = x.shape
    bs = _tile(s, (ROW_BLK,))

    def body(*refs):
        refs = list(refs)
        x_ref = refs.pop(0)
        z_ref = refs.pop(0) if gate is not None else None
        g_ref = refs.pop(0)
        dy_ref = refs.pop(0)
        add_ref = refs.pop(0) if add is not None else None
        dx_ref = refs.pop(0)
        dz_ref = refs.pop(0) if gate is not None else None
        dg_ref = refs.pop(0)
        i = pl.program_id(0)

        @pl.when(i == 0)
        def _():
            dg_ref[...] = jnp.zeros_like(dg_ref)

        xv = x_ref[...]
        if gate is not None:
            z = z_ref[...]
            sg = _sigmoid(z)
            act = z * sg
            u = xv * act
        else:
            u = xv
        r = lax.rsqrt(jnp.mean(u * u, axis=1, keepdims=True) + EPS)
        dy_v = dy_ref[...]
        dyg = dy_v * g_ref[...]
        du = r * dyg - u * (r * r * r * jnp.mean(dyg * u, axis=1, keepdims=True))
        dg_ref[...] += jnp.sum(dy_v * u * r, axis=0, keepdims=True)
        if gate is not None:
            dx = du * act
            dz_ref[...] = du * xv * (sg * (1.0 + z * (1.0 - sg)))
        else:
            dx = du
        if add is not None:
            dx = dx + add_ref[...]
        dx_ref[...] = dx

    row = pl.BlockSpec((bs, w), lambda i: (i, 0))
    vec = pl.BlockSpec((1, w), lambda i: (0, 0))
    ins = [x] + ([] if gate is None else [gate]) + [g.reshape(1, w), dy] + ([] if add is None else [add])
    specs = [row] + ([] if gate is None else [row]) + [vec, row] + ([] if add is None else [row])
    outs = [jax.ShapeDtypeStruct((s, w), F32)] + ([] if gate is None else [jax.ShapeDtypeStruct((s, w), F32)])
    outs.append(jax.ShapeDtypeStruct((1, w), F32))
    ospecs = [row] + ([] if gate is None else [row]) + [vec]
    return _call(body, name=name, out_shape=outs, grid=(s // bs,), in_specs=specs, out_specs=ospecs,
                 sem=("arbitrary",))(*ins)


def loss_head(y, target, *, name):
    s, w = y.shape
    bs = _tile(s, (ROW_BLK,))
    nb = s // bs

    def body(y_ref, t_ref, dy_ref, loss_ref, acc):
        i = pl.program_id(0)

        @pl.when(i == 0)
        def _():
            acc[...] = jnp.zeros_like(acc)

        e = y_ref[...] - t_ref[...]
        dy_ref[...] = e * (1.0 / w)
        acc[...] += jnp.sum(e * e, axis=0, keepdims=True)

        @pl.when(i == nb - 1)
        def _():
            loss_ref[...] = jnp.sum(acc[...], axis=1, keepdims=True) * (0.5 / w)

    row = pl.BlockSpec((bs, w), lambda i: (i, 0))
    return _call(body, name=name, out_shape=[jax.ShapeDtypeStruct((s, w), F32), jax.ShapeDtypeStruct((1, 1), F32)],
                 grid=(nb,), in_specs=[row, row], out_specs=[row, pl.BlockSpec((1, 1), lambda i: (0, 0))],
                 scratch=[pltpu.VMEM((1, w), F32)], sem=("arbitrary",))(y, target)


def _rope_tables(positions, s):
    inv_freq = 1.0 / (ROPE_THETA ** (jnp.arange(0, MLA_ROPE, 2, dtype=F32) / MLA_ROPE))
    ang = positions.reshape(s, 1).astype(F32) * inv_freq
    cos, sin = jnp.cos(ang), jnp.sin(ang)
    one, zero = jnp.ones((s, MLA_NOPE), F32), jnp.zeros((s, MLA_NOPE), F32)
    cq = jnp.tile(jnp.concatenate([one, cos, cos], axis=1), (1, MLA_HEADS))
    sq = jnp.tile(jnp.concatenate([zero, sin, sin], axis=1), (1, MLA_HEADS))
    pad1, pad0 = jnp.ones((s, LANES - MLA_ROPE), F32), jnp.zeros((s, LANES - MLA_ROPE), F32)
    ct = jnp.concatenate([cos, cos, pad1], axis=1)
    st = jnp.concatenate([sin, sin, pad0], axis=1)
    half = MLA_ROPE // 2

    def swap(width, starts):
        r = np.zeros((width, width), np.float32)
        for o in starts:
            for i in range(half):
                r[o + half + i, o + i] = -1.0
                r[o + i, o + half + i] = 1.0
        return jnp.asarray(r)

    rq = swap(MLA_HEADS * MLA_QK, [h * MLA_QK + MLA_NOPE for h in range(MLA_HEADS)])
    rt = swap(LANES, [0])
    return (cq, sq, rq), (ct, st, rt)


def rope(x, tabs, *, name, backward=False, add=None):
    cos, sin, rot = tabs
    n, s, w = x.shape
    bs = _tile(s, (ROW_BLK,))

    def body(*refs):
        if add is None:
            x_ref, c_ref, s_ref, r_ref, o_ref = refs
        else:
            x_ref, c_ref, s_ref, r_ref, a_ref, o_ref = refs
        xv = x_ref[0]
        for j in range(1, n):
            xv = xv + x_ref[j]
        if backward:
            out = xv * c_ref[...] + _nt(xv * s_ref[...], r_ref[...], HIGHEST)
        else:
            out = xv * c_ref[...] + _nn(xv, r_ref[...], HIGHEST) * s_ref[...]
        if add is not None:
            out = out + a_ref[...]
        o_ref[...] = out

    row = pl.BlockSpec((bs, w), lambda i: (i, 0))
    ins = [x, cos, sin, rot] + ([] if add is None else [add])
    specs = [pl.BlockSpec((n, bs, w), lambda i: (0, i, 0)), row, row, pl.BlockSpec((w, w), lambda i: (0, 0))]
    specs += [] if add is None else [row]
    return _call(body, name=name, out_shape=jax.ShapeDtypeStruct((s, w), F32), grid=(s // bs,), in_specs=specs,
                 out_specs=row, sem=("parallel",))(*ins)


def _tri(n, op):
    r = lax.broadcasted_iota(jnp.int32, (n, n), 0)
    c = lax.broadcasted_iota(jnp.int32, (n, n), 1)
    return r, c, op(r, c)


def _pair_split(x, first):
    zero = jnp.zeros_like(x)
    return jnp.where(first, x, zero), jnp.where(first, zero, x)


def _sb_specs(s, blk):
    npair = SB_WIDTH // LANES
    q = pl.BlockSpec((blk, LANES), lambda j, i: (i, j))
    k = pl.BlockSpec((s, LANES), lambda j, i: (0, npair + j))
    v = pl.BlockSpec((s, LANES), lambda j, i: (0, 2 * npair + j))
    full = pl.BlockSpec((s, LANES), lambda j, i: (0, j))
    return npair, q, k, v, full


def sb_fwd(qkv, *, name):
    s = qkv.shape[0]
    blk = _tile(s, (ATT_BLK,))
    scale = SB_DIM ** -0.5

    def body(q_ref, k_ref, v_ref, y_ref, t_ref):
        qi = pl.program_id(1)
        first = lax.broadcasted_iota(jnp.int32, (blk, LANES), 1) < SB_DIM
        qh = _pair_split((q_ref[...].astype(F32) * scale).astype(MXU_DTYPE), first)
        row, col, later_mask = _tri(blk, lambda r, c: r > c)
        u_later = jnp.tile(later_mask.astype(MXU_DTYPE), (2, 1))
        valid = col < row

        def tile(kb, carry, masked):
            ks = pl.multiple_of(kb * blk, blk)
            kv = k_ref[pl.ds(ks, blk), :]
            vv = v_ref[pl.ds(ks, blk), :]
            out = []
            for hh in range(2):
                c, acc = carry[hh]
                z = _nt(qh[hh], kv)
                sp = _softplus_att(z)
                lk = jnp.where(valid, -sp, 0.0) if masked else -sp
                w = jnp.exp((z - sp) + _cum(lk, u_later) + c)
                if masked:
                    w = jnp.where(valid, w, 0.0)
                out.append((c + jnp.sum(lk, axis=1, keepdims=True), acc + _nn(w.astype(MXU_DTYPE), vv)))
            return tuple(out)

        zero = (jnp.zeros((blk, 1), F32), jnp.zeros((blk, LANES), F32))
        (ca, ya), (cb, yb) = _causal_loop(qi, tile, (zero, zero), True)
        y_ref[...] = jnp.where(first, ya, yb)
        t_ref[...] = jnp.where(first, ca, cb)

    npair, qspec, kspec, vspec, _ = _sb_specs(s, blk)
    out = jax.ShapeDtypeStruct((s, SB_WIDTH), F32)
    return _call(body, name=name, out_shape=[out, out], grid=(npair, s // blk), in_specs=[qspec, kspec, vspec],
                 out_specs=[qspec, qspec], sem=("parallel", "arbitrary"))(qkv, qkv, qkv)


def sb_bwd(qkv, dy, tot, *, name):
    s = qkv.shape[0]
    blk = _tile(s, (ATT_BLK,))
    scale = SB_DIM ** -0.5

    def body(q_ref, k_ref, v_ref, dy_ref, t_ref, dq_ref, dk_ref, dv_ref):
        qi = pl.program_id(1)

        @pl.when(qi == 0)
        def _():
            dk_ref[...] = jnp.zeros_like(dk_ref)
            dv_ref[...] = jnp.zeros_like(dv_ref)

        first = lax.broadcasted_iota(jnp.int32, (blk, LANES), 1) < SB_DIM
        qh = _pair_split((q_ref[...].astype(F32) * scale).astype(MXU_DTYPE), first)
        dyh = _pair_split(dy_ref[...].astype(MXU_DTYPE), first)
        q_cat = jnp.concatenate(qh, axis=0)
        dy_cat = jnp.concatenate(dyh, axis=0)
        tv = (t_ref[:, 0:1], t_ref[:, SB_DIM:SB_DIM + 1])
        row, col, incl_mask = _tri(blk, lambda r, c: r <= c)
        u_incl = jnp.tile(incl_mask.astype(MXU_DTYPE), (2, 1))
        u_excl = jnp.tile((row < col).astype(MXU_DTYPE), (2, 1))
        valid = col < row

        def tile(kb, carry, masked):
            ks = pl.multiple_of(kb * blk, blk)
            kv = k_ref[pl.ds(ks, blk), :]
            vv = v_ref[pl.ds(ks, blk), :]
            out, dzs, ws = [], [], []
            for hh in range(2):
                p, gc, dq = carry[hh]
                z = _nt(qh[hh], kv)
                sp = _softplus_att(z)
                lk = jnp.where(valid, -sp, 0.0) if masked else -sp
                w = jnp.exp((z - sp) + (tv[hh] - (_cum(lk, u_incl) + p)))
                if masked:
                    w = jnp.where(valid, w, 0.0)
                g = w * _nt(dyh[hh], vv)
                gex = _cum(g, u_excl) + gc
                keep = jnp.exp(lk)
                dz = g * keep - (1.0 - keep) * gex
                if masked:
                    dz = jnp.where(valid, dz, 0.0)
                dzb = dz.astype(MXU_DTYPE)
                dzs.append(dzb)
                ws.append(w.astype(MXU_DTYPE))
                out.append((p + jnp.sum(lk, axis=1, keepdims=True), gc + jnp.sum(g, axis=1, keepdims=True),
                            dq + _nn(dzb, kv)))
            dk_ref[pl.ds(ks, blk), :] += _tn(jnp.concatenate(dzs, axis=0), q_cat)
            dv_ref[pl.ds(ks, blk), :] += _tn(jnp.concatenate(ws, axis=0), dy_cat)
            return tuple(out)

        zero = jnp.zeros((blk, 1), F32)
        init = (zero, zero, jnp.zeros((blk, LANES), F32))
        (_, _, dqa), (_, _, dqb) = _causal_loop(qi, tile, (init, init), False)
        dq_ref[...] = jnp.where(first, dqa, dqb) * scale

    npair, qspec, kspec, vspec, full = _sb_specs(s, blk)
    out = jax.ShapeDtypeStruct((s, SB_WIDTH), F32)
    return _call(body, name=name, out_shape=[out, out, out], grid=(npair, s // blk),
                 in_specs=[qspec, kspec, vspec, qspec, qspec], out_specs=[qspec, full, full],
                 sem=("parallel", "arbitrary"))(qkv, qkv, qkv, dy, tot)


ATT_PAIR = 2


def _mla_specs(s, blk, dk, dv):
    q = pl.BlockSpec((ATT_PAIR, blk, dk), lambda hp, i: (hp, i, 0))
    k = pl.BlockSpec((ATT_PAIR, s, dk), lambda hp, i: (hp, 0, 0))
    v = pl.BlockSpec((ATT_PAIR, s, dv), lambda hp, i: (hp, 0, 0))
    y = pl.BlockSpec((ATT_PAIR, blk, dv), lambda hp, i: (hp, i, 0))
    lse = pl.BlockSpec((ATT_PAIR, blk, LANES), lambda hp, i: (hp, i, 0))
    return q, k, v, y, lse


def mla_fwd(q, k, v, *, name):
    h, s, dk = q.shape
    dv = v.shape[-1]
    blk = _tile(s, (ATT_BLK,))
    scale = dk ** -0.5

    def body(q_ref, k_ref, v_ref, y_ref, l_ref):
        qi = pl.program_id(1)
        row, col, valid = _tri(blk, lambda r, c: c <= r)

        def tile(kb, carry, masked):
            ks = pl.multiple_of(kb * blk, blk)
            out = []
            for hh in range(ATT_PAIR):
                m, l, acc = carry[hh]
                sc = _nt(q_ref[hh], k_ref[hh, pl.ds(ks, blk), :]) * scale
                if masked:
                    sc = jnp.where(valid, sc, -1e30)
                m2 = jnp.maximum(m, jnp.max(sc, axis=1, keepdims=True))
                p = jnp.exp(sc - m2)
                a = jnp.exp(m - m2)
                out.append((m2, a * l + jnp.sum(p, axis=1, keepdims=True),
                            a * acc + _nn(p.astype(MXU_DTYPE), v_ref[hh, pl.ds(ks, blk), :])))
            return tuple(out)

        init = (jnp.full((blk, 1), -1e30, F32), jnp.zeros((blk, 1), F32), jnp.zeros((blk, dv), F32))
        for hh, (m, l, acc) in enumerate(_causal_loop(qi, tile, (init,) * ATT_PAIR, False)):
            y_ref[hh] = acc / l
            l_ref[hh] = jnp.broadcast_to(m + jnp.log(l), (blk, LANES))

    qspec, kspec, vspec, yspec, lspec = _mla_specs(s, blk, dk, dv)
    return _call(body, name=name,
                 out_shape=[jax.ShapeDtypeStruct((h, s, dv), F32), jax.ShapeDtypeStruct((h, s, LANES), F32)],
                 grid=(h // ATT_PAIR, s // blk), in_specs=[qspec, kspec, vspec], out_specs=[yspec, lspec],
                 sem=("parallel", "arbitrary"))(q, k, v)


def mla_bwd(q, k, v, y, dy, lse, *, name):
    h, s, dk = q.shape
    dv = v.shape[-1]
    blk = _tile(s, (ATT_BLK,))
    scale = dk ** -0.5

    def body(q_ref, k_ref, v_ref, y_ref, dy_ref, l_ref, dq_ref, dk_ref, dv_ref):
        qi = pl.program_id(1)

        @pl.when(qi == 0)
        def _():
            dk_ref[...] = jnp.zeros_like(dk_ref)
            dv_ref[...] = jnp.zeros_like(dv_ref)

        dyv = [dy_ref[hh].astype(MXU_DTYPE) for hh in range(ATT_PAIR)]
        delta = [jnp.sum(dy_ref[hh] * y_ref[hh], axis=1, keepdims=True) for hh in range(ATT_PAIR)]
        lv = [l_ref[hh, :, 0:1] for hh in range(ATT_PAIR)]
        row, col, valid = _tri(blk, lambda r, c: c <= r)

        def tile(kb, dqs, masked):
            ks = pl.multiple_of(kb * blk, blk)
            out = []
            for hh in range(ATT_PAIR):
                qv = q_ref[hh]
                kv = k_ref[hh, pl.ds(ks, blk), :]
                vv = v_ref[hh, pl.ds(ks, blk), :]
                p = jnp.exp(_nt(qv, kv) * scale - lv[hh])
                if masked:
                    p = jnp.where(valid, p, 0.0)
                ds = (p * (_nt(dyv[hh], vv) - delta[hh])).astype(MXU_DTYPE)
                dk_ref[hh, pl.ds(ks, blk), :] += _tn(ds, qv) * scale
                dv_ref[hh, pl.ds(ks, blk), :] += _tn(p.astype(MXU_DTYPE), dyv[hh])
                out.append(dqs[hh] + _nn(ds, kv))
            return tuple(out)

        for hh, dq in enumerate(_causal_loop(qi, tile, (jnp.zeros((blk, dk), F32),) * ATT_PAIR, False)):
            dq_ref[hh] = dq * scale

    qspec, kspec, vspec, yspec, lspec = _mla_specs(s, blk, dk, dv)
    return _call(body, name=name,
                 out_shape=[jax.ShapeDtypeStruct((h, s, dk), F32), jax.ShapeDtypeStruct((h, s, dk), F32),
                            jax.ShapeDtypeStruct((h, s, dv), F32)],
                 grid=(h // ATT_PAIR, s // blk), in_specs=[qspec, kspec, vspec, yspec, yspec, lspec],
                 out_specs=[qspec, kspec, vspec], sem=("parallel", "arbitrary"))(q, k, v, y, dy, lse)


HALO = 8


def _conv_tiles(x):
    p, s, c = x.shape
    return p, s, c, _tile(s, (ROW_BLK,)), _tile(c, (CONV_COLS,))


def _conv_specs(bs, cw, pmap=lambda p: p):
    blk = pl.BlockSpec((None, bs, cw), lambda p, j, i: (pmap(p), i, j))
    halo = pl.BlockSpec((None, HALO, cw), lambda p, j, i: (pmap(p), jnp.maximum(i * (bs // HALO) - 1, 0), j))
    w = lambda kk: pl.BlockSpec((None, kk, cw), lambda p, j, i: (pmap(p), 0, j))
    return blk, halo, w


def _stage(scr, x_ref, halo_ref, first):
    scr[0:HALO, :] = jnp.where(first, 0.0, halo_ref[...])
    scr[HALO:, :] = x_ref[...]


def _conv_taps(scr, kk, bs):
    return [scr[pl.ds(HALO - (kk - 1) + k, bs), :] for k in range(kk)]


def _conv_sum(taps, w_ref, b_ref):
    u = b_ref[...] + taps[0] * w_ref[0:1, :]
    for k in range(1, len(taps)):
        u = u + taps[k] * w_ref[k:k + 1, :]
    return u


def conv_silu_fwd(x, w, b, *, name):
    p, s, c, bs, cw = _conv_tiles(x)
    kk = w.shape[1]

    def body(x_ref, h_ref, w_ref, b_ref, o_ref, scr):
        _stage(scr, x_ref, h_ref, pl.program_id(2) == 0)
        u = _conv_sum(_conv_taps(scr, kk, bs), w_ref, b_ref)
        o_ref[...] = u * _sigmoid(u)

    blk, halo, wspec = _conv_specs(bs, cw)
    return _call(body, name=name, out_shape=jax.ShapeDtypeStruct(x.shape, F32), grid=(p, c // cw, s // bs),
                 in_specs=[blk, halo, wspec(kk), wspec(1)], out_specs=blk, scratch=[pltpu.VMEM((bs + HALO, cw), F32)],
                 sem=("parallel", "parallel", "arbitrary"))(x, x, w, b)


def conv_silu_bwd(x, dy, w, b, *, name):
    p, s, c, bs, cw = _conv_tiles(x)
    kk = w.shape[1]

    def body(x_ref, h_ref, w_ref, b_ref, dy_ref, du_ref, dw_ref, db_ref, scr):
        i = pl.program_id(2)

        @pl.when(i == 0)
        def _():
            dw_ref[...] = jnp.zeros_like(dw_ref)
            db_ref[...] = jnp.zeros_like(db_ref)

        _stage(scr, x_ref, h_ref, i == 0)
        taps = _conv_taps(scr, kk, bs)
        u = _conv_sum(taps, w_ref, b_ref)
        sg = _sigmoid(u)
        du = dy_ref[...] * (sg * (1.0 + u * (1.0 - sg)))
        du_ref[...] = du
        for k in range(kk):
            dw_ref[k:k + 1, :] += jnp.sum(du * taps[k], axis=0, keepdims=True)
        db_ref[...] += jnp.sum(du, axis=0, keepdims=True)

    blk, halo, wspec = _conv_specs(bs, cw)
    return _call(body, name=name,
                 out_shape=[jax.ShapeDtypeStruct(x.shape, F32), jax.ShapeDtypeStruct(w.shape, F32),
                            jax.ShapeDtypeStruct(b.shape, F32)],
                 grid=(p, c // cw, s // bs), in_specs=[blk, halo, wspec(kk), wspec(1), blk],
                 out_specs=[blk, wspec(kk), wspec(1)], scratch=[pltpu.VMEM((bs + HALO, cw), F32)],
                 sem=("parallel", "parallel", "arbitrary"))(x, x, w, b, dy)


def conv_glu_fwd(x, w, b, *, name):
    p, s, c, bs, cw = _conv_tiles(x)
    kk = w.shape[1]
    half = p // 2

    def body(g_ref, gh_ref, v_ref, vh_ref, wg_ref, wv_ref, bg_ref, bv_ref, o_ref, gscr, vscr):
        first = pl.program_id(2) == 0
        _stage(gscr, g_ref, gh_ref, first)
        _stage(vscr, v_ref, vh_ref, first)
        gate = _conv_sum(_conv_taps(gscr, kk, bs), wg_ref, bg_ref)
        val = _conv_sum(_conv_taps(vscr, kk, bs), wv_ref, bv_ref)
        o_ref[...] = (gate * _sigmoid(gate) * val).astype(o_ref.dtype)

    gblk, ghalo, gw = _conv_specs(bs, cw)
    vblk, vhalo, vw = _conv_specs(bs, cw, lambda q: q + half)
    return _call(body, name=name, out_shape=jax.ShapeDtypeStruct((half, s, c), MXU_DTYPE), grid=(half, c // cw, s // bs),
                 in_specs=[gblk, ghalo, vblk, vhalo, gw(kk), vw(kk), gw(1), vw(1)], out_specs=gblk,
                 scratch=[pltpu.VMEM((bs + HALO, cw), F32)] * 2,
                 sem=("parallel", "parallel", "arbitrary"))(x, x, x, x, w, w, b, b)


def conv_glu_bwd(x, da, w, b, *, name):
    p, s, c, bs, cw = _conv_tiles(x)
    kk = w.shape[1]
    half = p // 2

    def body(o_ref, oh_ref, p_ref, ph_ref, wo_ref, wp_ref, bo_ref, bp_ref, da_ref, du_ref, dw_ref, db_ref, oscr, pscr):
        grp, i = pl.program_id(0), pl.program_id(2)

        @pl.when(i == 0)
        def _():
            dw_ref[...] = jnp.zeros_like(dw_ref)
            db_ref[...] = jnp.zeros_like(db_ref)

        _stage(oscr, o_ref, oh_ref, i == 0)
        _stage(pscr, p_ref, ph_ref, i == 0)
        taps = _conv_taps(oscr, kk, bs)
        own = _conv_sum(taps, wo_ref, bo_ref)
        par = _conv_sum(_conv_taps(pscr, kk, bs), wp_ref, bp_ref)
        is_gate = grp < half
        gate = jnp.where(is_gate, own, par)
        val = jnp.where(is_gate, par, own)
        sg = _sigmoid(gate)
        du = da_ref[...] * jnp.where(is_gate, val * (sg * (1.0 + gate * (1.0 - sg))), gate * sg)
        du_ref[...] = du
        for k in range(kk):
            dw_ref[k:k + 1, :] += jnp.sum(du * taps[k], axis=0, keepdims=True)
        db_ref[...] += jnp.sum(du, axis=0, keepdims=True)

    oblk, ohalo, ow = _conv_specs(bs, cw)
    pblk, phalo, pw = _conv_specs(bs, cw, lambda q: (q + half) % p)
    daspec = pl.BlockSpec((None, bs, cw), lambda q, j, i: (q % half, i, j))
    return _call(body, name=name,
                 out_shape=[jax.ShapeDtypeStruct(x.shape, F32), jax.ShapeDtypeStruct(w.shape, F32),
                            jax.ShapeDtypeStruct(b.shape, F32)],
                 grid=(p, c // cw, s // bs), in_specs=[oblk, ohalo, pblk, phalo, ow(kk), pw(kk), ow(1), pw(1), daspec],
                 out_specs=[oblk, ow(kk), ow(1)], scratch=[pltpu.VMEM((bs + HALO, cw), F32)] * 2,
                 sem=("parallel", "parallel", "arbitrary"))(x, x, x, x, w, w, b, b, da)


def conv_t(du, w, *, name):
    p, s, c, bs, cw = _conv_tiles(du)
    kk = w.shape[1]
    nb = s // bs

    def body(d_ref, h_ref, w_ref, o_ref, scr):
        last = pl.program_id(2) == nb - 1
        scr[0:bs, :] = d_ref[...]
        scr[bs:, :] = jnp.where(last, 0.0, h_ref[...])
        acc = scr[pl.ds(kk - 1, bs), :] * w_ref[0:1, :]
        for k in range(1, kk):
            acc = acc + scr[pl.ds(kk - 1 - k, bs), :] * w_ref[k:k + 1, :]
        o_ref[...] = acc

    blk, _, wspec = _conv_specs(bs, cw)
    halo = pl.BlockSpec((None, HALO, cw), lambda q, j, i: (q, jnp.minimum((i + 1) * (bs // HALO), s // HALO - 1), j))
    return _call(body, name=name, out_shape=jax.ShapeDtypeStruct(du.shape, F32), grid=(p, c // cw, nb),
                 in_specs=[blk, halo, wspec(kk)], out_specs=blk, scratch=[pltpu.VMEM((bs + HALO, cw), F32)],
                 sem=("parallel", "parallel", "arbitrary"))(du, du, w)


def _ssd_common(xbc_ref, tail_ref, dtrt_ref, bias_ref, biast_ref, alog_ref, alogt_ref):
    L = SSM_CHUNK
    raw = tail_ref[...] + bias_ref[...]
    dt = _softplus(raw)
    dtt = _softplus(dtrt_ref[...] + biast_ref[...])
    a = -jnp.exp(alog_ref[...])
    at = -jnp.exp(alogt_ref[...])
    row, col, lower = _tri(L, lambda r, c: r >= c)
    tril = lower.astype(F32)
    cs = _nn(tril, dt * a, HIGHEST)
    cst = _nt(dtt * at, tril, HIGHEST)
    bm = [xbc_ref[:, SSM_INNER + g * SSM_N: SSM_INNER + (g + 1) * SSM_N] for g in range(SSM_GROUPS)]
    off = SSM_INNER + SSM_GROUPS * SSM_N
    cm = [xbc_ref[:, off + g * SSM_N: off + (g + 1) * SSM_N] for g in range(SSM_GROUPS)]
    cb = [_nt(cm[g], bm[g], HIGHEST) for g in range(SSM_GROUPS)]
    return raw, dt, a, lower, tril, cs, cst, bm, cm, cb


def _ssd_head(hh, xbc_ref, dt, cs, cst, lower):
    L = SSM_CHUNK
    ln = DT_LANE + hh
    x = xbc_ref[:, hh * SSM_P:(hh + 1) * SSM_P]
    dtc = dt[:, ln:ln + 1]
    csc = cs[:, ln:ln + 1]
    csr = cst[hh:hh + 1, :]
    decay = jnp.exp(jnp.where(lower, csc - csr, -1e30))
    last = cs[L - 1:L, ln:ln + 1]
    return x, dtc, csc, decay, jnp.exp(csc), jnp.exp(last - csc), jnp.exp(last)


def _ssd_inputs(tail, dt_bias, a_log, d_skip):
    H = SSM_HEADS
    lanes = lambda vec: jnp.pad(vec.reshape(1, H), ((0, 0), (DT_LANE, LANES - DT_LANE - H)))
    return (tail, tail[:, DT_LANE:DT_LANE + H].T, lanes(dt_bias), dt_bias.reshape(H, 1), lanes(a_log),
            a_log.reshape(H, 1), lanes(d_skip))


def ssd_fwd(xbc, tail, dt_bias, a_log, d_skip, *, name):
    s = xbc.shape[0]
    L, H, P, N = SSM_CHUNK, SSM_HEADS, SSM_P, SSM_N
    nc = s // L

    def body(xbc_ref, tail_ref, dtrt_ref, bias_ref, biast_ref, alog_ref, alogt_ref, d_ref, y_ref, hp_ref, state):
        @pl.when(pl.program_id(0) == 0)
        def _():
            state[...] = jnp.zeros_like(state)

        raw, dt, a, lower, tril, cs, cst, bm, cm, cb = _ssd_common(
            xbc_ref, tail_ref, dtrt_ref, bias_ref, biast_ref, alog_ref, alogt_ref)
        for hh in range(H):
            g = hh // (H // SSM_GROUPS)
            x, dtc, csc, decay, e, tau, gamma = _ssd_head(hh, xbc_ref, dt, cs, cst, lower)
            xdt = x * dtc
            hprev = state[hh]
            hp_ref[hh] = hprev
            skip = d_ref[:, DT_LANE + hh:DT_LANE + hh + 1]
            y = _nn(cb[g] * decay, xdt, HIGHEST) + _nn(cm[g], hprev, HIGHEST) * e + x * skip
            y_ref[:, hh * P:(hh + 1) * P] = y
            state[hh] = hprev * gamma + _tn(bm[g] * tau, xdt, HIGHEST)

    row = lambda w: pl.BlockSpec((L, w), lambda c: (c, 0))
    small = lambda shp: pl.BlockSpec(shp, lambda c: (0, 0))
    return _call(body, name=name,
                 out_shape=[jax.ShapeDtypeStruct((s, SSM_INNER), F32), jax.ShapeDtypeStruct((nc, H, N, P), F32)],
                 grid=(nc,),
                 in_specs=[row(SSM_CONV_DIM), row(LANES), pl.BlockSpec((H, L), lambda c: (0, c)), small((1, LANES)),
                           small((H, 1)), small((1, LANES)), small((H, 1)), small((1, LANES))],
                 out_specs=[row(SSM_INNER), pl.BlockSpec((None, H, N, P), lambda c: (c, 0, 0, 0))],
                 scratch=[pltpu.VMEM((H, N, P), F32)], sem=("arbitrary",))(xbc, *_ssd_inputs(tail, dt_bias, a_log, d_skip))


def ssd_bwd(xbc, tail, dt_bias, a_log, d_skip, hprev_all, dy, *, name):
    s = xbc.shape[0]
    L, H, P, N = SSM_CHUNK, SSM_HEADS, SSM_P, SSM_N
    nc = s // L
    hg = H // SSM_GROUPS

    def body(xbc_ref, tail_ref, dtrt_ref, bias_ref, biast_ref, alog_ref, alogt_ref, d_ref, hp_ref, dy_ref,
             dxbc_ref, ddt_ref, dbias_ref, dalog_ref, dd_ref, dstate):
        @pl.when(pl.program_id(0) == 0)
        def _():
            dstate[...] = jnp.zeros_like(dstate)
            dbias_ref[...] = jnp.zeros_like(dbias_ref)
            dalog_ref[...] = jnp.zeros_like(dalog_ref)
            dd_ref[...] = jnp.zeros_like(dd_ref)

        raw, dt, a, lower, tril, cs, cst, bm, cm, cb = _ssd_common(
            xbc_ref, tail_ref, dtrt_ref, bias_ref, biast_ref, alog_ref, alogt_ref)
        lane = lax.broadcasted_iota(jnp.int32, (L, LANES), 1)
        lane1 = lax.broadcasted_iota(jnp.int32, (1, LANES), 1)
        rowi = lax.broadcasted_iota(jnp.int32, (L, 1), 0)
        ones = jnp.ones((L, LANES), F32)
        dcs_all = jnp.zeros((L, LANES), F32)
        ddt_x = jnp.zeros((L, LANES), F32)
        dd_row = jnp.zeros((1, LANES), F32)
        dbm = [jnp.zeros((L, N), F32) for _ in range(SSM_GROUPS)]
        dcm = [jnp.zeros((L, N), F32) for _ in range(SSM_GROUPS)]
        dcb = [jnp.zeros((L, L), F32) for _ in range(SSM_GROUPS)]
        for hh in range(H):
            g = hh // hg
            ln = DT_LANE + hh
            x, dtc, csc, decay, e, tau, gamma = _ssd_head(hh, xbc_ref, dt, cs, cst, lower)
            xdt = x * dtc
            hprev = hp_ref[hh]
            dhn = dstate[hh]
            dyh = dy_ref[:, hh * P:(hh + 1) * P]
            m = cb[g] * decay
            dxdt = _tn(m, dyh, HIGHEST) + _nn(bm[g] * tau, dhn, HIGHEST)
            dm = jnp.where(lower, _nt(dyh, xdt, HIGHEST), 0.0)
            dcb[g] = dcb[g] + dm * decay
            dseg = dm * m
            dcs = jnp.sum(dseg, axis=1, keepdims=True) - _tn(dseg, ones, HIGHEST)[:, 0:1]
            edy = e * dyh
            dcm[g] = dcm[g] + _nt(edy, hprev, HIGHEST)
            dcs = dcs + e * jnp.sum(dyh * _nn(cm[g], hprev, HIGHEST), axis=1, keepdims=True)
            xdh = _nt(xdt, dhn, HIGHEST)
            dbm[g] = dbm[g] + tau * xdh
            dtau_tau = jnp.sum(bm[g] * xdh, axis=1, keepdims=True) * tau
            dlast = jnp.sum(dtau_tau, axis=0, keepdims=True) + gamma * jnp.sum(dhn * hprev, keepdims=True)
            dcs = dcs - dtau_tau + jnp.where(rowi == L - 1, dlast, 0.0)
            dstate[hh] = gamma * dhn + _tn(cm[g], edy, HIGHEST)
            dcs_all = jnp.where(lane == ln, dcs, dcs_all)
            ddt_x = jnp.where(lane == ln, jnp.sum(dxdt * x, axis=1, keepdims=True), ddt_x)
            dxbc_ref[:, hh * P:(hh + 1) * P] = dxdt * dtc + d_ref[:, ln:ln + 1] * dyh
            dd_row = jnp.where(lane1 == ln, jnp.sum(dyh * x, keepdims=True), dd_row)
        off = SSM_INNER + SSM_GROUPS * SSM_N
        for g in range(SSM_GROUPS):
            dxbc_ref[:, SSM_INNER + g * N: SSM_INNER + (g + 1) * N] = dbm[g] + _tn(dcb[g], cm[g], HIGHEST)
            dxbc_ref[:, off + g * N: off + (g + 1) * N] = dcm[g] + _nn(dcb[g], bm[g], HIGHEST)
        dda = _tn(tril, dcs_all, HIGHEST)
        head_lane = (lane >= DT_LANE) & (lane < DT_LANE + H)
        draw = jnp.where(head_lane, (dda * a + ddt_x) * _sigmoid(raw), 0.0)
        ddt_ref[...] = draw
        dbias_ref[...] += jnp.sum(draw, axis=0, keepdims=True)
        dalog_ref[...] += jnp.sum(jnp.where(head_lane, dda * dt, 0.0), axis=0, keepdims=True) * a
        dd_ref[...] += dd_row

    rev = lambda c: nc - 1 - c
    row = lambda w: pl.BlockSpec((L, w), lambda c: (rev(c), 0))
    small = lambda shp: pl.BlockSpec(shp, lambda c: (0, 0))
    acc = pl.BlockSpec((1, LANES), lambda c: (0, 0))
    vec = jax.ShapeDtypeStruct((1, LANES), F32)
    return _call(body, name=name,
                 out_shape=[jax.ShapeDtypeStruct((s, SSM_CONV_DIM), F32), jax.ShapeDtypeStruct((s, LANES), F32), vec, vec, vec],
                 grid=(nc,),
                 in_specs=[row(SSM_CONV_DIM), row(LANES), pl.BlockSpec((H, L), lambda c: (0, rev(c))), small((1, LANES)),
                           small((H, 1)), small((1, LANES)), small((H, 1)), small((1, LANES)),
                           pl.BlockSpec((None, H, N, P), lambda c: (rev(c), 0, 0, 0)), row(SSM_INNER)],
                 out_specs=[row(SSM_CONV_DIM), row(LANES), acc, acc, acc],
                 scratch=[pltpu.VMEM((H, N, P), F32)], sem=("arbitrary",))(
        xbc, *_ssd_inputs(tail, dt_bias, a_log, d_skip), hprev_all, dy)


def _heads(x2d, n, d):
    s = x2d.shape[0]
    return x2d.reshape(s, n, d).transpose(1, 0, 2)


def _unheads(x3d):
    n, s, d = x3d.shape
    return x3d.transpose(1, 0, 2).reshape(s, n * d)


def layer_fwd(h, p, tabs, li):
    s = h.shape[0]
    tabq, tabt = tabs
    nm = lambda t: f"L{li}_{t}"
    r = {'h': h}
    hn = rms_fwd(h, p['mix_norm'], name=nm('mixnorm'), out_dtype=MXU_DTYPE)
    proj = mm(hn, p['w_in'], name=nm('proj'))
    r.update(hn=hn, proj=proj)
    qkv = proj[:, :3 * SB_WIDTH].astype(MXU_DTYPE)
    ya, tot = sb_fwd(qkv, name=nm('sb_fwd'))
    yan = rms_fwd(ya, p['sb_out_norm'], name=nm('sbnorm'), out_dtype=MXU_DTYPE)
    r.update(qkv=qkv, ya=ya, tot=tot)
    z = proj[:, 768:1280]
    xbc = proj[None, :, 1280:2048]
    tail = proj[:, TAIL:TAIL + LANES]
    xbc_act = conv_silu_fwd(xbc, p['ssm_conv_w'], p['ssm_conv_b'], name=nm('ssmconv'))[0]
    y_ssm, hprev = ssd_fwd(xbc_act, tail, p['ssm_dt_bias'], p['ssm_a_log'], p['ssm_d'], name=nm('ssd_fwd'))
    ybn = rms_fwd(y_ssm, p['ssm_out_norm'], name=nm('ssmnorm'), gate=z, out_dtype=MXU_DTYPE)
    r.update(z=z, xbc=xbc, tail=tail, xbc_act=xbc_act, y_ssm=y_ssm, hprev=hprev)
    cq = proj[:, 2048:2304]
    ckv = proj[:, 2304:2432]
    qn = rms_fwd(cq, p['mla_q_norm'], name=nm('qnorm'), out_dtype=MXU_DTYPE)
    q_r = rope(mm(qn, p['mla_w_uq'], name=nm('uq'))[None], tabq, name=nm('ropeq'))
    kvn = rms_fwd(ckv, p['mla_kv_norm'], name=nm('kvnorm'), out_dtype=MXU_DTYPE)
    kv = mm(kvn, p['mla_w_ukv'], name=nm('ukv'))
    k_pe = rope(tail[None], tabt, name=nm('ropek'))[:, :MLA_ROPE]
    qh = _heads(q_r, MLA_HEADS, MLA_QK).astype(MXU_DTYPE)
    kvh = _heads(kv, MLA_HEADS, MLA_NOPE + MLA_V)
    kh = jnp.concatenate([kvh[..., :MLA_NOPE], jnp.broadcast_to(k_pe[None], (MLA_HEADS, s, MLA_ROPE))],
                         axis=-1).astype(MXU_DTYPE)
    vh = kvh[..., MLA_NOPE:].astype(MXU_DTYPE)
    yc_h, lse = mla_fwd(qh, kh, vh, name=nm('mla_fwd'))
    yc = _unheads(yc_h)
    ycn = rms_fwd(yc, p['mla_out_norm'], name=nm('mlanorm'), out_dtype=MXU_DTYPE)
    r.update(cq=cq, ckv=ckv, qn=qn, kvn=kvn, qh=qh, kh=kh, vh=vh, yc_h=yc_h, yc=yc, lse=lse)
    ycat = jnp.concatenate([yan, ybn, ycn], axis=1)
    h1 = mm(ycat, p['w_out'], name=nm('outproj'), res=h)
    hn2 = rms_fwd(h1, p['ffn_norm'], name=nm('ffnnorm'), out_dtype=MXU_DTYPE)
    up = mm(hn2, p['ffn_w_up'], name=nm('up'), bb='o')
    act = conv_glu_fwd(up, p['ffn_conv_w'], p['ffn_conv_b'], name=nm('glu'))
    h2 = mm(act, p['ffn_w_down'], name=nm('down'), ab='k', bb='k', res=h1)
    r.update(ycat=ycat, h1=h1, hn2=hn2, up=up, act=act)
    return h2, r


def layer_bwd(dh2, p, r, tabs, li):
    s = dh2.shape[0]
    tabq, tabt = tabs
    nm = lambda t: f"L{li}_{t}"
    g = {}
    dact = mm(dh2, p['ffn_w_down'], name=nm('d_down_x'), tb=True, bb='o')
    g['ffn_w_down'] = mm(r['act'], dh2, name=nm('d_down_w'), ta=True, ab='o')
    du, g['ffn_conv_w'], g['ffn_conv_b'] = conv_glu_bwd(r['up'], dact, p['ffn_conv_w'], p['ffn_conv_b'], name=nm('d_glu'))
    dup = conv_t(du, p['ffn_conv_w'], name=nm('d_ffnconv'))
    g['ffn_w_up'] = mm(r['hn2'], dup, name=nm('d_up_w'), ta=True, bb='o')
    dhn2 = mm(dup, p['ffn_w_up'], name=nm('d_up_x'), tb=True, ab='k', bb='k')
    dh1, dg = rms_bwd(r['h1'], p['ffn_norm'], dhn2, name=nm('d_ffnnorm'), add=dh2)
    g['ffn_norm'] = dg[0]
    dycat = mm(dh1, p['w_out'], name=nm('d_out_x'), tb=True)
    g['w_out'] = mm(r['ycat'], dh1, name=nm('d_out_w'), ta=True)
    dya, dg = rms_bwd(r['ya'], p['sb_out_norm'], dycat[:, :256], name=nm('d_sbnorm'))
    g['sb_out_norm'] = dg[0]
    dq, dk, dv = sb_bwd(r['qkv'], dya, r['tot'], name=nm('sb_bwd'))
    dyssm, dz, dg = rms_bwd(r['y_ssm'], p['ssm_out_norm'], dycat[:, 256:768], name=nm('d_ssmnorm'), gate=r['z'])
    g['ssm_out_norm'] = dg[0]
    dxbc_act, ddt_tail, dbias, dalog, dd = ssd_bwd(r['xbc_act'], r['tail'], p['ssm_dt_bias'], p['ssm_a_log'],
                                                   p['ssm_d'], r['hprev'], dyssm, name=nm('ssd_bwd'))
    hl = slice(DT_LANE, DT_LANE + SSM_HEADS)
    g['ssm_dt_bias'], g['ssm_a_log'], g['ssm_d'] = dbias[0, hl], dalog[0, hl], dd[0, hl]
    dxbc_u, g['ssm_conv_w'], g['ssm_conv_b'] = conv_silu_bwd(r['xbc'], dxbc_act[None], p['ssm_conv_w'], p['ssm_conv_b'],
                                                             name=nm('d_ssmact'))
    dxbc = conv_t(dxbc_u, p['ssm_conv_w'], name=nm('d_ssmconv'))[0]
    dyc, dg = rms_bwd(r['yc'], p['mla_out_norm'], dycat[:, 768:], name=nm('d_mlanorm'))
    g['mla_out_norm'] = dg[0]
    dqh, dkh, dvh = mla_bwd(r['qh'], r['kh'], r['vh'], r['yc_h'], _heads(dyc, MLA_HEADS, MLA_V), r['lse'], name=nm('mla_bwd'))
    dq_c = rope(_unheads(dqh)[None], tabq, name=nm('d_ropeq'), backward=True)
    g['mla_w_uq'] = mm(r['qn'], dq_c, name=nm('d_uq_w'), ta=True)
    dcq, dg = rms_bwd(r['cq'], p['mla_q_norm'], mm(dq_c, p['mla_w_uq'], name=nm('d_uq_x'), tb=True), name=nm('d_qnorm'))
    g['mla_q_norm'] = dg[0]
    dkv = _unheads(jnp.concatenate([dkh[..., :MLA_NOPE], dvh], axis=-1))
    g['mla_w_ukv'] = mm(r['kvn'], dkv, name=nm('d_ukv_w'), ta=True)
    dckv, dg = rms_bwd(r['ckv'], p['mla_kv_norm'], mm(dkv, p['mla_w_ukv'], name=nm('d_ukv_x'), tb=True), name=nm('d_kvnorm'))
    g['mla_kv_norm'] = dg[0]
    dkpe = jnp.pad(dkh[..., MLA_NOPE:], ((0, 0), (0, 0), (0, LANES - MLA_ROPE)))
    dtail = rope(dkpe, tabt, name=nm('d_ropek'), backward=True, add=ddt_tail)
    dproj = jnp.concatenate([dq, dk, dv, dz, dxbc, dcq, dckv, dtail], axis=1)
    g['w_in'] = mm(r['hn'], dproj, name=nm('d_proj_w'), ta=True)
    dhn = mm(dproj, p['w_in'], name=nm('d_proj_x'), tb=True)
    dh, dg = rms_bwd(r['h'], p['mix_norm'], dhn, name=nm('d_mixnorm'), add=dh1)
    g['mix_norm'] = dg[0]
    return dh, g


def _w_in_placement():
    c = np.arange(D_IN)
    dest = np.where(c < 2048, c, np.where(c < 2056, c + (D_IN - 2056), c - 8))
    dest = jnp.asarray(dest.reshape(N_DEV, D_IN // N_DEV, 1), jnp.int32)
    return (dest == jnp.arange(D_IN_PAD, dtype=jnp.int32)).astype(MXU_DTYPE)


def _owner_major(full, axis):
    shp = full.shape
    return jnp.moveaxis(full.reshape(shp[:axis] + (N_DEV, shp[axis] // N_DEV) + shp[axis + 1:]), axis, 0)


def _owner_join(parts, axis):
    moved = jnp.moveaxis(parts, 0, axis)
    shp = moved.shape
    return moved.reshape(shp[:axis] + (shp[axis] * shp[axis + 1],) + shp[axis + 2:])


def assemble_params(gathered, replicated):
    L = DEPTH
    place = _w_in_placement()
    w_in = mm(gathered['w_in'].reshape(N_DEV, L * D_MODEL, D_IN // N_DEV), place, name='place_w_in', ab='k', bb='k',
              out_dtype=MXU_DTYPE).reshape(L, D_MODEL, D_IN_PAD)
    out = dict(replicated)
    out['w_in'] = w_in
    out['ffn_w_up'] = jnp.moveaxis(gathered['ffn_w_up'], 1, 0)
    out['w_out'] = _owner_join(gathered['w_out'], 1)
    out['ffn_w_down'] = _owner_join(gathered['ffn_w_down'], 1).reshape(L, N_DEV // 2, FF_SHARD, D_MODEL)
    out['mla_w_uq'] = _owner_join(gathered['mla_w_uq'], 2)
    out['mla_w_ukv'] = _owner_join(gathered['mla_w_ukv'], 2)
    out['ssm_conv_w'] = _owner_join(gathered['ssm_conv_w'], 2)[:, None]
    out['ffn_conv_w'] = jnp.moveaxis(gathered['ffn_conv_w'], 1, 0)
    out['ssm_conv_b'] = replicated['ssm_conv_b'].reshape(L, 1, 1, SSM_CONV_DIM)
    out['ffn_conv_b'] = replicated['ffn_conv_b'].reshape(L, N_DEV, 1, FF_SHARD)
    return out


def owner_parts(grads):
    L = DEPTH
    st = lambda n: jnp.stack([g[n] for g in grads])
    place = _w_in_placement()
    parts = {
        'w_in': mm(st('w_in').reshape(L * D_MODEL, D_IN_PAD), place, name='unplace_w_in', tb=True, bb='o',
                   precision=HIGHEST).reshape(N_DEV, L, D_MODEL, D_IN // N_DEV),
        'ffn_w_up': jnp.moveaxis(st('ffn_w_up'), 1, 0),
        'w_out': _owner_major(st('w_out'), 1),
        'ffn_w_down': _owner_major(st('ffn_w_down').reshape(L, D_FF, D_MODEL), 1),
        'mla_w_uq': _owner_major(st('mla_w_uq'), 2),
        'mla_w_ukv': _owner_major(st('mla_w_ukv'), 2),
        'ssm_conv_w': _owner_major(st('ssm_conv_w')[:, 0], 2),
        'ffn_conv_w': jnp.moveaxis(st('ffn_conv_w'), 1, 0),
    }
    rep = {n: st(n) for n in REPLICATED if n not in ('final_norm', 'ssm_conv_b', 'ffn_conv_b')}
    rep['ssm_conv_b'] = st('ssm_conv_b').reshape(L, SSM_CONV_DIM)
    rep['ffn_conv_b'] = st('ffn_conv_b').reshape(L, 2 * D_FF)
    return parts, rep


def local_step(x, positions, target, params):
    s = x.shape[0]
    tabs = _rope_tables(positions, s)
    h = x
    saved = []
    for li in range(DEPTH):
        p = {n: params[n][li] for n in WEIGHTS if n != 'final_norm'}
        h, r = layer_fwd(h, p, tabs, li)
        saved.append((p, r))
    y = rms_fwd(h, params['final_norm'], name='finalnorm')
    dy, loss = loss_head(y, target, name='loss')
    dh, dg = rms_bwd(h, params['final_norm'], dy, name='d_finalnorm')
    per_layer = []
    for li in reversed(range(DEPTH)):
        p, r = saved[li]
        dh, g = layer_bwd(dh, p, r, tabs, li)
        per_layer.append(g)
    per_layer.reverse()
    parts, rep = owner_parts(per_layer)
    rep['final_norm'] = dg[0]
    return loss[0, 0], dh, parts, rep


MESH = pl.DeviceIdType.MESH
HBM = pl.BlockSpec(memory_space=pltpu.HBM)


def _flip(v, bit):
    return 1 - v if bit else v


def all_gather(blocks, *, name):
    n = len(blocks)

    def body(*refs):
        x_refs, out_refs = refs[:n], refs[n:2 * n]
        send_sems, recv_sems, local_sems = refs[2 * n:]
        x, y, c = lax.axis_index("x"), lax.axis_index("y"), lax.axis_index("c")
        me, sibling = (x, y, c), (x, y, 1 - c)
        chips = [(1 - x, y), (x, 1 - y), (1 - x, 1 - y)]

        def slot(b, px, py, pc):
            return out_refs[b].at[4 * px + 2 * py + pc]

        def copy(b, k, blk, to, src=None):
            return pltpu.make_async_remote_copy(src_ref=slot(b, *blk) if src is None else src, dst_ref=slot(b, *blk),
                                                send_sem=send_sems.at[b, k], recv_sem=recv_sems.at[b, k],
                                                device_id=to, device_id_type=MESH)

        mine = [pltpu.make_async_copy(x_refs[b], slot(b, *me), local_sems.at[b]) for b in range(n)]
        for cp in mine:
            cp.start()
        first = []
        for b in range(n):
            first.append(copy(b, 0, me, sibling, src=x_refs[b]))
            first += [copy(b, 1 + j, me, (*chip, c), src=x_refs[b]) for j, chip in enumerate(chips)]
        for cp in first:
            cp.start()
        passed = []
        for j, chip in enumerate(chips):
            for b in range(n):
                copy(b, 1 + j, (*chip, c), me).wait_recv()
                fwd = copy(b, 4 + j, (*chip, c), sibling)
                fwd.start()
                passed.append(fwd)
        for b in range(n):
            copy(b, 0, sibling, me).wait_recv()
            for j, chip in enumerate(chips):
                copy(b, 4 + j, (*chip, 1 - c), me).wait_recv()
        for cp in first + passed:
            cp.wait_send()
        for cp in mine:
            cp.wait()

    return pl.pallas_call(
        body, name=name, out_shape=[jax.ShapeDtypeStruct((N_DEV,) + b.shape, b.dtype) for b in blocks],
        in_specs=[HBM] * n, out_specs=[HBM] * n,
        scratch_shapes=[pltpu.SemaphoreType.DMA((n, 7)), pltpu.SemaphoreType.DMA((n, 7)), pltpu.SemaphoreType.DMA((n,))],
    )(*blocks)


def all_to_all(parts, *, name):
    n = len(parts)

    def body(*refs):
        g_refs, r_refs = refs[:n], refs[n:2 * n]
        send_sems, recv_sems, local_sems = refs[2 * n:]
        x, y, c = lax.axis_index("x"), lax.axis_index("y"), lax.axis_index("c")
        me = 4 * x + 2 * y + c
        mine = [pltpu.make_async_copy(g_refs[b].at[me], r_refs[b].at[me], local_sems.at[b]) for b in range(n)]
        for cp in mine:
            cp.start()
        copies = []
        for k in range(1, N_DEV):
            px, py, pc = _flip(x, k & 4), _flip(y, k & 2), _flip(c, k & 1)
            peer = 4 * px + 2 * py + pc
            for b in range(n):
                cp = pltpu.make_async_remote_copy(src_ref=g_refs[b].at[peer], dst_ref=r_refs[b].at[me],
                                                  send_sem=send_sems.at[b, k - 1], recv_sem=recv_sems.at[b, k - 1],
                                                  device_id=(px, py, pc), device_id_type=MESH)
                cp.start()
                copies.append(cp)
        for cp in copies:
            cp.wait_send()
            cp.wait_recv()
        for cp in mine:
            cp.wait()

    return pl.pallas_call(
        body, name=name, out_shape=[jax.ShapeDtypeStruct(p.shape, p.dtype) for p in parts],
        in_specs=[HBM] * n, out_specs=[HBM] * n,
        scratch_shapes=[pltpu.SemaphoreType.DMA((n, 7)), pltpu.SemaphoreType.DMA((n, 7)), pltpu.SemaphoreType.DMA((n,))],
    )(*parts)


def adamw(parts, w, m, v, *, name):
    r, wd = w.shape
    br = _tile(r, (256, 128, 64, 32, 16, 8))
    c1 = 1.0 - ADAM_B1 ** ADAM_STEP
    c2 = 1.0 - ADAM_B2 ** ADAM_STEP

    def body(p_ref, w_ref, m_ref, v_ref, g_ref, d_ref, mo_ref, vo_ref):
        g = p_ref[0]
        for j in range(1, N_DEV):
            g = g + p_ref[j]
        mn = ADAM_B1 * m_ref[...] + (1.0 - ADAM_B1) * g
        vn = ADAM_B2 * v_ref[...] + (1.0 - ADAM_B2) * (g * g)
        g_ref[...] = g
        mo_ref[...] = mn
        vo_ref[...] = vn
        d_ref[...] = -ADAM_LR * ((mn / c1) / (jnp.sqrt(vn / c2) + ADAM_EPS) + ADAM_WD * w_ref[...])

    blk = pl.BlockSpec((br, wd), lambda i: (i, 0))
    out = jax.ShapeDtypeStruct((r, wd), F32)
    return _call(body, name=name, out_shape=[out] * 4, grid=(r // br,),
                 in_specs=[pl.BlockSpec((N_DEV, br, wd), lambda i: (0, i, 0)), blk, blk, blk], out_specs=[blk] * 4,
                 sem=("parallel",))(parts, w, m, v)


def _pack(arrs):
    flat = jnp.concatenate([a.reshape(-1) for a in arrs])
    rows = -(-flat.shape[0] // (8 * FLAT_W)) * 8
    return jnp.pad(flat, (0, rows * FLAT_W - flat.shape[0])).reshape(rows, FLAT_W)


def _unpack(flat, shapes):
    flat = flat.reshape(-1)
    out, off = [], 0
    for shp in shapes:
        n = int(np.prod(shp))
        out.append(flat[off:off + n].reshape(shp))
        off += n
    return out


def kernel(x, positions, mix_norm, w_in, sb_out_norm, ssm_conv_w, ssm_conv_b, ssm_dt_bias, ssm_a_log, ssm_d, ssm_out_norm, mla_q_norm, mla_w_uq, mla_kv_norm, mla_w_ukv, mla_out_norm, w_out, ffn_norm, ffn_w_up, ffn_conv_w, ffn_conv_b, ffn_w_down, final_norm, loss_target, m_mix_norm, m_w_in, m_sb_out_norm, m_ssm_conv_w, m_ssm_conv_b, m_ssm_dt_bias, m_ssm_a_log, m_ssm_d, m_ssm_out_norm, m_mla_q_norm, m_mla_w_uq, m_mla_kv_norm, m_mla_w_ukv, m_mla_out_norm, m_w_out, m_ffn_norm, m_ffn_w_up, m_ffn_conv_w, m_ffn_conv_b, m_ffn_w_down, m_final_norm, v_mix_norm, v_w_in, v_sb_out_norm, v_ssm_conv_w, v_ssm_conv_b, v_ssm_dt_bias, v_ssm_a_log, v_ssm_d, v_ssm_out_norm, v_mla_q_norm, v_mla_w_uq, v_mla_kv_norm, v_mla_w_ukv, v_mla_out_norm, v_w_out, v_ffn_norm, v_ffn_w_up, v_ffn_conv_w, v_ffn_conv_b, v_ffn_w_down, v_final_norm):
    args = locals()
    w = {n: args[n] for n in WEIGHTS}
    m = {n: args['m_' + n] for n in WEIGHTS}
    v = {n: args['v_' + n] for n in WEIGHTS}
    sharded = list(SHARDED)

    wire = [w[n] if n in VPU_WEIGHTS else w[n].astype(BF16) for n in sharded]
    gathered = dict(zip(sharded, all_gather(wire, name='gather_weights')))
    params = assemble_params(gathered, {n: w[n] for n in REPLICATED})

    loss, dx, parts, rep = local_step(x[0], positions[0], loss_target[0], params)
    loss = lax.psum(loss, ("x", "y", "c"))

    recv = all_to_all([parts[n] for n in sharded], name='scatter_grads')
    res = {kind: {} for kind in 'gdmv'}
    for n, rv in zip(sharded, recv):
        shp = w[n].shape
        two_d = (int(np.prod(shp[:-1])), shp[-1])
        outs = adamw(rv.reshape((N_DEV,) + two_d), w[n].reshape(two_d), m[n].reshape(two_d), v[n].reshape(two_d),
                     name='adamw_' + n)
        for kind, o in zip('gdmv', outs):
            res[kind][n] = o.reshape(shp)

    rep_shapes = [w[n].shape for n in REPLICATED]
    (rparts,) = all_gather([_pack([rep[n] for n in REPLICATED])], name='gather_small_grads')
    rflat = lambda d: _pack([d[n] for n in REPLICATED])
    routs = adamw(rparts, rflat(w), rflat(m), rflat(v), name='adamw_replicated')
    for kind, o in zip('gdmv', routs):
        res[kind].update(zip(REPLICATED, _unpack(o, rep_shapes)))

    return (loss, dx[None], *[res['g'][n] for n in WEIGHTS], *[res['d'][n] for n in WEIGHTS],
            *[res['m'][n] for n in WEIGHTS], *[res['v'][n] for n in WEIGHTS])
```

```python
import numpy as np
import jax
import jax.numpy as jnp
from jax import lax
from jax.experimental import pallas as pl
from jax.experimental.pallas import tpu as pltpu

F32 = jnp.float32
BF16 = jnp.bfloat16
MXU_DTYPE = jnp.bfloat16
HIGHEST = lax.Precision.HIGHEST

N_DEV = 8
D_MODEL = 1024
DEPTH = 2
EPS = 1e-6
SB_HEADS, SB_DIM = 4, 64
SB_WIDTH = SB_HEADS * SB_DIM
SSM_HEADS, SSM_P, SSM_GROUPS, SSM_N, SSM_CONV, SSM_CHUNK = 8, 64, 2, 64, 4, 128
SSM_INNER = SSM_HEADS * SSM_P
SSM_CONV_DIM = SSM_INNER + 2 * SSM_GROUPS * SSM_N
MLA_HEADS, MLA_NOPE, MLA_ROPE, MLA_V, MLA_Q_RANK, MLA_KV_RANK = 4, 64, 32, 64, 256, 128
MLA_QK = MLA_NOPE + MLA_ROPE
ROPE_THETA = 10000.0
D_IN = 2472
D_IN_PAD = 2560
TAIL = 2432
DT_LANE = 32
D_FF = 2816
FF_SHARD = 2 * D_FF // N_DEV
ADAM_LR, ADAM_B1, ADAM_B2, ADAM_EPS, ADAM_WD, ADAM_STEP = 0.001, 0.9, 0.999, 1e-08, 0.01, 10

LANES = 128
ATT_BLK = 256
ATT_UNROLL = 4
ROW_BLK = 512
CONV_COLS = 256
FLAT_W = 1024
VMEM_LIMIT = 56 << 20
MM_TM = (1024, 512, 256, 128)
MM_TN = (1280, 1024, 768, 640, 512, 384, 256, 128)
MM_TK = (1280, 1024, 512, 256, 128)

WEIGHTS = ['mix_norm', 'w_in', 'sb_out_norm', 'ssm_conv_w', 'ssm_conv_b', 'ssm_dt_bias', 'ssm_a_log', 'ssm_d',
           'ssm_out_norm', 'mla_q_norm', 'mla_w_uq', 'mla_kv_norm', 'mla_w_ukv', 'mla_out_norm', 'w_out',
           'ffn_norm', 'ffn_w_up', 'ffn_conv_w', 'ffn_conv_b', 'ffn_w_down', 'final_norm']
SHARDED = {'w_in': 2, 'ssm_conv_w': 2, 'mla_w_uq': 2, 'mla_w_ukv': 2, 'w_out': 1, 'ffn_w_up': 2, 'ffn_conv_w': 2,
           'ffn_w_down': 1}
VPU_WEIGHTS = ('ssm_conv_w', 'ffn_conv_w')
REPLICATED = [n for n in WEIGHTS if n not in SHARDED]


def _call(body, *, name, out_shape, grid=(), in_specs=None, out_specs=None, scratch=(), sem=None, **kw):
    params = dict(vmem_limit_bytes=VMEM_LIMIT)
    if sem is not None:
        params['dimension_semantics'] = sem
    return pl.pallas_call(body, name=name, out_shape=out_shape, grid=grid, in_specs=in_specs, out_specs=out_specs,
                          scratch_shapes=list(scratch), compiler_params=pltpu.CompilerParams(**params), **kw)


def _tile(n, prefs):
    for t in prefs:
        if n % t == 0:
            return t
    return n


def _dot(a, b, dims, precision=None):
    return lax.dot_general(a, b, (dims, ((), ())), preferred_element_type=F32, precision=precision)


def _nn(a, b, precision=None):
    return _dot(a, b, ((1,), (0,)), precision)


def _nt(a, b, precision=None):
    return _dot(a, b, ((1,), (1,)), precision)


def _tn(a, b, precision=None):
    return _dot(a, b, ((0,), (0,)), precision)


def _mxu(f):
    return lambda a, b: f(a.astype(MXU_DTYPE), b.astype(MXU_DTYPE))


_bnn, _bnt, _btn = _mxu(_nn), _mxu(_nt), _mxu(_tn)


def _split2(x):
    hi = x.astype(MXU_DTYPE)
    lo = (x - hi.astype(F32)).astype(MXU_DTYPE)
    return hi, lo


def _sigmoid(x):
    return 0.5 * jnp.tanh(0.5 * x) + 0.5


def _softplus(x):
    return jnp.maximum(x, 0.0) + jnp.log1p(jnp.exp(-jnp.abs(x)))


def _softplus_att(x):
    return jnp.maximum(x, 0.0) + jnp.log(1.0 + jnp.exp(-jnp.abs(x)))


def _cum(x, u):
    n = x.shape[0]
    r = _nn(jnp.concatenate(_split2(x), axis=0), u)
    return r[:n] + r[n:]


def _causal_loop(qi, tile, carry, reverse):
    block = lambda j: qi - 1 - j if reverse else j

    def several(i, cr):
        for u in range(ATT_UNROLL):
            cr = tile(block(i * ATT_UNROLL + u), cr, False)
        return cr

    def rest(cr):
        full = qi // ATT_UNROLL
        cr = lax.fori_loop(0, full, several, cr)
        return lax.fori_loop(full * ATT_UNROLL, qi, lambda j, c: tile(block(j), c, False), cr)

    return rest(tile(qi, carry, True)) if reverse else tile(qi, rest(carry), True)


def mm(a, b, *, name, ta=False, tb=False, res=None, out_dtype=F32, ab=None, bb=None, precision=None):
    a2, b2 = a.shape[-2:], b.shape[-2:]
    (kdim, m) = a2 if ta else a2[::-1]
    (n, k2) = b2 if tb else b2[::-1]
    assert kdim == k2, (a.shape, b.shape, ta, tb)
    assert (ab == 'k') == (bb == 'k')
    kb = ab == 'k'
    nb = a.shape[0] if ab == 'o' else (b.shape[0] if bb == 'o' else None)
    tm, tn = _tile(m, MM_TM), _tile(n, MM_TN)
    tk = kdim if kb else _tile(kdim, MM_TK)
    nk = a.shape[0] if kb else kdim // tk
    dims = ((0 if ta else 1,), (1 if tb else 0,))
    op_dtype = F32 if precision is not None else MXU_DTYPE

    def body(*refs):
        a_ref, b_ref = refs[0], refs[1]
        r_ref = refs[2] if res is not None else None
        o_ref = refs[3] if res is not None else refs[2]
        part = _dot(a_ref[...].astype(op_dtype), b_ref[...].astype(op_dtype), dims, precision)

        def finish(out):
            if res is not None:
                out = out + r_ref[...]
            o_ref[...] = out.astype(out_dtype)

        if nk == 1:
            finish(part)
            return
        acc = refs[-1]
        k = pl.program_id(3)

        @pl.when(k == 0)
        def _():
            acc[...] = part

        @pl.when(k > 0)
        def _():
            acc[...] += part

        @pl.when(k == nk - 1)
        def _():
            finish(acc[...])

    def spec(blk, idx, how):
        if how is None:
            return pl.BlockSpec(blk, idx)
        if how == 'o':
            return pl.BlockSpec((None,) + blk, lambda p, i, j, k: (p,) + idx(p, i, j, k))
        return pl.BlockSpec((None,) + blk, lambda p, i, j, k: (k,) + idx(p, i, j, 0))

    a_spec = spec((tk, tm), lambda p, i, j, k: (k, i), ab) if ta else spec((tm, tk), lambda p, i, j, k: (i, k), ab)
    b_spec = spec((tn, tk), lambda p, i, j, k: (j, k), bb) if tb else spec((tk, tn), lambda p, i, j, k: (k, j), bb)
    o_spec = spec((tm, tn), lambda p, i, j, k: (i, j), None if nb is None else 'o')
    ins, specs = [a, b], [a_spec, b_spec]
    if res is not None:
        ins.append(res)
        specs.append(o_spec)
    out_shape = (m, n) if nb is None else (nb, m, n)
    return _call(body, name=name, out_shape=jax.ShapeDtypeStruct(out_shape, out_dtype),
                 grid=(1 if nb is None else nb, m // tm, n // tn, nk), in_specs=specs, out_specs=o_spec,
                 scratch=[] if nk == 1 else [pltpu.VMEM((tm, tn), F32)],
                 sem=("parallel", "parallel", "parallel", "arbitrary"))(*ins)


def rms_fwd(x, g, *, name, gate=None, out_dtype=F32):
    s, w = x.shape
    bs = _tile(s, (ROW_BLK,))

    def body(*refs):
        if gate is None:
            x_ref, g_ref, o_ref = refs
            u = x_ref[...]
        else:
            x_ref, z_ref, g_ref, o_ref = refs
            z = z_ref[...]
            u = x_ref[...] * (z * _sigmoid(z))
        r = lax.rsqrt(jnp.mean(u * u, axis=1, keepdims=True) + EPS)
        o_ref[...] = (u * r * g_ref[...]).astype(out_dtype)

    row = pl.BlockSpec((bs, w), lambda i: (i, 0))
    vec = pl.BlockSpec((1, w), lambda i: (0, 0))
    ins = [x] + ([] if gate is None else [gate]) + [g.reshape(1, w)]
    specs = [row] + ([] if gate is None else [row]) + [vec]
    return _call(body, name=name, out_shape=jax.ShapeDtypeStruct((s, w), out_dtype), grid=(s // bs,),
                 in_specs=specs, out_specs=row, sem=("parallel",))(*ins)


def rms_bwd(x, g, dy, *, name, gate=None, add=None):
    s, w = x.shape
    bs = _tile(s, (ROW_BLK,))

    def body(*refs):
        refs = list(refs)
        x_ref = refs.pop(0)
        z_ref = refs.pop(0) if gate is not None else None
        g_ref = refs.pop(0)
        dy_ref = refs.pop(0)
        add_ref = refs.pop(0) if add is not None else None
        dx_ref = refs.pop(0)
        dz_ref = refs.pop(0) if gate is not None else None
        dg_ref = refs.pop(0)
        i = pl.program_id(0)

        @pl.when(i == 0)
        def _():
            dg_ref[...] = jnp.zeros_like(dg_ref)

        xv = x_ref[...]
        if gate is not None:
            z = z_ref[...]
            sg = _sigmoid(z)
            act = z * sg
            u = xv * act
        else:
            u = xv
        r = lax.rsqrt(jnp.mean(u * u, axis=1, keepdims=True) + EPS)
        dy_v = dy_ref[...]
        dyg = dy_v * g_ref[...]
        du = r * dyg - u * (r * r * r * jnp.mean(dyg * u, axis=1, keepdims=True))
        dg_ref[...] += jnp.sum(dy_v * u * r, axis=0, keepdims=True)
        if gate is not None:
            dx = du * act
            dz_ref[...] = du * xv * (sg * (1.0 + z * (1.0 - sg)))
        else:
            dx = du
        if add is not None:
            dx = dx + add_ref[...]
        dx_ref[...] = dx

    row = pl.BlockSpec((bs, w), lambda i: (i, 0))
    vec = pl.BlockSpec((1, w), lambda i: (0, 0))
    ins = [x] + ([] if gate is None else [gate]) + [g.reshape(1, w), dy] + ([] if add is None else [add])
    specs = [row] + ([] if gate is None else [row]) + [vec, row] + ([] if add is None else [row])
    outs = [jax.ShapeDtypeStruct((s, w), F32)] + ([] if gate is None else [jax.ShapeDtypeStruct((s, w), F32)])
    outs.append(jax.ShapeDtypeStruct((1, w), F32))
    ospecs = [row] + ([] if gate is None else [row]) + [vec]
    return _call(body, name=name, out_shape=outs, grid=(s // bs,), in_specs=specs, out_specs=ospecs,
                 sem=("arbitrary",))(*ins)


def loss_head(y, target, *, name):
    s, w = y.shape
    bs = _tile(s, (ROW_BLK,))
    nb = s // bs

    def body(y_ref, t_ref, dy_ref, loss_ref, acc):
        i = pl.program_id(0)

        @pl.when(i == 0)
        def _():
            acc[...] = jnp.zeros_like(acc)

        e = y_ref[...] - t_ref[...]
        dy_ref[...] = e * (1.0 / w)
        acc[...] += jnp.sum(e * e, axis=0, keepdims=True)

        @pl.when(i == nb - 1)
        def _():
            loss_ref[...] = jnp.sum(acc[...], axis=1, keepdims=True) * (0.5 / w)

    row = pl.BlockSpec((bs, w), lambda i: (i, 0))
    return _call(body, name=name, out_shape=[jax.ShapeDtypeStruct((s, w), F32), jax.ShapeDtypeStruct((1, 1), F32)],
                 grid=(nb,), in_specs=[row, row], out_specs=[row, pl.BlockSpec((1, 1), lambda i: (0, 0))],
                 scratch=[pltpu.VMEM((1, w), F32)], sem=("arbitrary",))(y, target)


def _rope_tables(positions, s):
    inv_freq = 1.0 / (ROPE_THETA ** (jnp.arange(0, MLA_ROPE, 2, dtype=F32) / MLA_ROPE))
    ang = positions.reshape(s, 1).astype(F32) * inv_freq
    cos, sin = jnp.cos(ang), jnp.sin(ang)
    one, zero = jnp.ones((s, MLA_NOPE), F32), jnp.zeros((s, MLA_NOPE), F32)
    cq = jnp.tile(jnp.concatenate([one, cos, cos], axis=1), (1, MLA_HEADS))
    sq = jnp.tile(jnp.concatenate([zero, sin, sin], axis=1), (1, MLA_HEADS))
    pad1, pad0 = jnp.ones((s, LANES - MLA_ROPE), F32), jnp.zeros((s, LANES - MLA_ROPE), F32)
    ct = jnp.concatenate([cos, cos, pad1], axis=1)
    st = jnp.concatenate([sin, sin, pad0], axis=1)
    half = MLA_ROPE // 2

    def swap(width, starts):
        r = np.zeros((width, width), np.float32)
        for o in starts:
            for i in range(half):
                r[o + half + i, o + i] = -1.0
                r[o + i, o + half + i] = 1.0
        return jnp.asarray(r)

    rq = swap(MLA_HEADS * MLA_QK, [h * MLA_QK + MLA_NOPE for h in range(MLA_HEADS)])
    rt = swap(LANES, [0])
    return (cq, sq, rq), (ct, st, rt)


def rope(x, tabs, *, name, backward=False, add=None):
    cos, sin, rot = tabs
    n, s, w = x.shape
    bs = _tile(s, (ROW_BLK,))

    def body(*refs):
        if add is None:
            x_ref, c_ref, s_ref, r_ref, o_ref = refs
        else:
            x_ref, c_ref, s_ref, r_ref, a_ref, o_ref = refs
        xv = x_ref[0]
        for j in range(1, n):
            xv = xv + x_ref[j]
        if backward:
            out = xv * c_ref[...] + _nt(xv * s_ref[...], r_ref[...], HIGHEST)
        else:
            out = xv * c_ref[...] + _nn(xv, r_ref[...], HIGHEST) * s_ref[...]
        if add is not None:
            out = out + a_ref[...]
        o_ref[...] = out

    row = pl.BlockSpec((bs, w), lambda i: (i, 0))
    ins = [x, cos, sin, rot] + ([] if add is None else [add])
    specs = [pl.BlockSpec((n, bs, w), lambda i: (0, i, 0)), row, row, pl.BlockSpec((w, w), lambda i: (0, 0))]
    specs += [] if add is None else [row]
    return _call(body, name=name, out_shape=jax.ShapeDtypeStruct((s, w), F32), grid=(s // bs,), in_specs=specs,
                 out_specs=row, sem=("parallel",))(*ins)


def _tri(n, op):
    r = lax.broadcasted_iota(jnp.int32, (n, n), 0)
    c = lax.broadcasted_iota(jnp.int32, (n, n), 1)
    return r, c, op(r, c)


def _pair_split(x, first):
    zero = jnp.zeros_like(x)
    return jnp.where(first, x, zero), jnp.where(first, zero, x)


def _sb_specs(s, blk):
    npair = SB_WIDTH // LANES
    q = pl.BlockSpec((blk, LANES), lambda j, i: (i, j))
    k = pl.BlockSpec((s, LANES), lambda j, i: (0, npair + j))
    v = pl.BlockSpec((s, LANES), lambda j, i: (0, 2 * npair + j))
    full = pl.BlockSpec((s, LANES), lambda j, i: (0, j))
    return npair, q, k, v, full


def _stack_heads(x, first):
    return jnp.concatenate(_pair_split(x, first), axis=0)


def _unstack_heads(x, first, blk):
    return jnp.where(first, x[:blk], x[blk:])


def sb_fwd(qkv, *, name):
    s = qkv.shape[0]
    blk = _tile(s, (ATT_BLK,))
    scale = SB_DIM ** -0.5

    def body(q_ref, k_ref, v_ref, y_ref, t_ref):
        qi = pl.program_id(1)
        first = lax.broadcasted_iota(jnp.int32, (blk, LANES), 1) < SB_DIM
        q2 = _stack_heads((q_ref[...].astype(F32) * scale).astype(MXU_DTYPE), first)
        row, col, later_mask = _tri(blk, lambda r, c: r > c)
        u_later = later_mask.astype(MXU_DTYPE)
        valid = jnp.tile(col < row, (2, 1))

        def tile(kb, carry, masked):
            c, acc = carry
            ks = pl.multiple_of(kb * blk, blk)
            z = _nt(q2, k_ref[pl.ds(ks, blk), :])
            sp = _softplus_att(z)
            lk = jnp.where(valid, -sp, 0.0) if masked else -sp
            w = jnp.exp((z - sp) + _cum(lk, u_later) + c)
            if masked:
                w = jnp.where(valid, w, 0.0)
            return c + jnp.sum(lk, axis=1, keepdims=True), acc + _nn(w.astype(MXU_DTYPE), v_ref[pl.ds(ks, blk), :])

        zero = (jnp.zeros((2 * blk, 1), F32), jnp.zeros((2 * blk, LANES), F32))
        c, acc = _causal_loop(qi, tile, zero, True)
        y_ref[...] = _unstack_heads(acc, first, blk)
        t_ref[...] = _unstack_heads(c, first, blk)

    npair, qspec, kspec, vspec, _ = _sb_specs(s, blk)
    out = jax.ShapeDtypeStruct((s, SB_WIDTH), F32)
    return _call(body, name=name, out_shape=[out, out], grid=(npair, s // blk), in_specs=[qspec, kspec, vspec],
                 out_specs=[qspec, qspec], sem=("parallel", "arbitrary"))(qkv, qkv, qkv)


def sb_bwd(qkv, dy, tot, *, name):
    s = qkv.shape[0]
    blk = _tile(s, (ATT_BLK,))
    scale = SB_DIM ** -0.5

    def body(q_ref, k_ref, v_ref, dy_ref, t_ref, dq_ref, dk_ref, dv_ref):
        qi = pl.program_id(1)

        @pl.when(qi == 0)
        def _():
            dk_ref[...] = jnp.zeros_like(dk_ref)
            dv_ref[...] = jnp.zeros_like(dv_ref)

        first = lax.broadcasted_iota(jnp.int32, (blk, LANES), 1) < SB_DIM
        q2 = _stack_heads((q_ref[...].astype(F32) * scale).astype(MXU_DTYPE), first)
        dy2 = _stack_heads(dy_ref[...].astype(MXU_DTYPE), first)
        tv = jnp.concatenate([t_ref[:, 0:1], t_ref[:, SB_DIM:SB_DIM + 1]], axis=0)
        row, col, incl_mask = _tri(blk, lambda r, c: r <= c)
        u_incl = incl_mask.astype(MXU_DTYPE)
        u_excl = (row < col).astype(MXU_DTYPE)
        valid = jnp.tile(col < row, (2, 1))

        def tile(kb, carry, masked):
            p, gc, dq = carry
            ks = pl.multiple_of(kb * blk, blk)
            kv = k_ref[pl.ds(ks, blk), :]
            z = _nt(q2, kv)
            sp = _softplus_att(z)
            lk = jnp.where(valid, -sp, 0.0) if masked else -sp
            w = jnp.exp((z - sp) + (tv - (_cum(lk, u_incl) + p)))
            if masked:
                w = jnp.where(valid, w, 0.0)
            g = w * _nt(dy2, v_ref[pl.ds(ks, blk), :])
            gex = _cum(g, u_excl) + gc
            keep = jnp.exp(lk)
            dz = g * keep - (1.0 - keep) * gex
            if masked:
                dz = jnp.where(valid, dz, 0.0)
            dzb = dz.astype(MXU_DTYPE)
            dk_ref[pl.ds(ks, blk), :] += _tn(dzb, q2)
            dv_ref[pl.ds(ks, blk), :] += _tn(w.astype(MXU_DTYPE), dy2)
            return (p + jnp.sum(lk, axis=1, keepdims=True), gc + jnp.sum(g, axis=1, keepdims=True), dq + _nn(dzb, kv))

        zero = jnp.zeros((2 * blk, 1), F32)
        _, _, dq = _causal_loop(qi, tile, (zero, zero, jnp.zeros((2 * blk, LANES), F32)), False)
        dq_ref[...] = _unstack_heads(dq, first, blk) * scale

    npair, qspec, kspec, vspec, full = _sb_specs(s, blk)
    out = jax.ShapeDtypeStruct((s, SB_WIDTH), F32)
    return _call(body, name=name, out_shape=[out, out, out], grid=(npair, s // blk),
                 in_specs=[qspec, kspec, vspec, qspec, qspec], out_specs=[qspec, full, full],
                 sem=("parallel", "arbitrary"))(qkv, qkv, qkv, dy, tot)


ATT_PAIR = 2


def _mla_specs(s, blk, dk, dv):
    q = pl.BlockSpec((ATT_PAIR, blk, dk), lambda hp, i: (hp, i, 0))
    k = pl.BlockSpec((ATT_PAIR, s, dk), lambda hp, i: (hp, 0, 0))
    v = pl.BlockSpec((ATT_PAIR, s, dv), lambda hp, i: (hp, 0, 0))
    y = pl.BlockSpec((ATT_PAIR, blk, dv), lambda hp, i: (hp, i, 0))
    lse = pl.BlockSpec((ATT_PAIR, blk, LANES), lambda hp, i: (hp, i, 0))
    return q, k, v, y, lse


def mla_fwd(q, k, v, *, name):
    h, s, dk = q.shape
    dv = v.shape[-1]
    blk = _tile(s, (ATT_BLK,))
    scale = dk ** -0.5

    def body(q_ref, k_ref, v_ref, y_ref, l_ref):
        qi = pl.program_id(1)
        row, col, valid = _tri(blk, lambda r, c: c <= r)

        def tile(kb, carry, masked):
            ks = pl.multiple_of(kb * blk, blk)
            out = []
            for hh in range(ATT_PAIR):
                m, l, acc = carry[hh]
                sc = _nt(q_ref[hh], k_ref[hh, pl.ds(ks, blk), :]) * scale
                if masked:
                    sc = jnp.where(valid, sc, -1e30)
                m2 = jnp.maximum(m, jnp.max(sc, axis=1, keepdims=True))
                p = jnp.exp(sc - m2)
                a = jnp.exp(m - m2)
                out.append((m2, a * l + jnp.sum(p, axis=1, keepdims=True),
                            a * acc + _nn(p.astype(MXU_DTYPE), v_ref[hh, pl.ds(ks, blk), :])))
            return tuple(out)

        init = (jnp.full((blk, 1), -1e30, F32), jnp.zeros((blk, 1), F32), jnp.zeros((blk, dv), F32))
        for hh, (m, l, acc) in enumerate(_causal_loop(qi, tile, (init,) * ATT_PAIR, False)):
            y_ref[hh] = acc / l
            l_ref[hh] = jnp.broadcast_to(m + jnp.log(l), (blk, LANES))

    qspec, kspec, vspec, yspec, lspec = _mla_specs(s, blk, dk, dv)
    return _call(body, name=name,
                 out_shape=[jax.ShapeDtypeStruct((h, s, dv), F32), jax.ShapeDtypeStruct((h, s, LANES), F32)],
                 grid=(h // ATT_PAIR, s // blk), in_specs=[qspec, kspec, vspec], out_specs=[yspec, lspec],
                 sem=("parallel", "arbitrary"))(q, k, v)


def mla_bwd(q, k, v, y, dy, lse, *, name):
    h, s, dk = q.shape
    dv = v.shape[-1]
    blk = _tile(s, (ATT_BLK,))
    scale = dk ** -0.5

    def body(q_ref, k_ref, v_ref, y_ref, dy_ref, l_ref, dq_ref, dk_ref, dv_ref):
        qi = pl.program_id(1)

        @pl.when(qi == 0)
        def _():
            dk_ref[...] = jnp.zeros_like(dk_ref)
            dv_ref[...] = jnp.zeros_like(dv_ref)

        dyv = [dy_ref[hh].astype(MXU_DTYPE) for hh in range(ATT_PAIR)]
        delta = [jnp.sum(dy_ref[hh] * y_ref[hh], axis=1, keepdims=True) for hh in range(ATT_PAIR)]
        lv = [l_ref[hh, :, 0:1] for hh in range(ATT_PAIR)]
        row, col, valid = _tri(blk, lambda r, c: c <= r)

        def tile(kb, dqs, masked):
            ks = pl.multiple_of(kb * blk, blk)
            out = []
            for hh in range(ATT_PAIR):
                qv = q_ref[hh]
                kv = k_ref[hh, pl.ds(ks, blk), :]
                vv = v_ref[hh, pl.ds(ks, blk), :]
                p = jnp.exp(_nt(qv, kv) * scale - lv[hh])
                if masked:
                    p = jnp.where(valid, p, 0.0)
                ds = (p * (_nt(dyv[hh], vv) - delta[hh])).astype(MXU_DTYPE)
                dk_ref[hh, pl.ds(ks, blk), :] += _tn(ds, qv) * scale
                dv_ref[hh, pl.ds(ks, blk), :] += _tn(p.astype(MXU_DTYPE), dyv[hh])
                out.append(dqs[hh] + _nn(ds, kv))
            return tuple(out)

        for hh, dq in enumerate(_causal_loop(qi, tile, (jnp.zeros((blk, dk), F32),) * ATT_PAIR, False)):
            dq_ref[hh] = dq * scale

    qspec, kspec, vspec, yspec, lspec = _mla_specs(s, blk, dk, dv)
    return _call(body, name=name,
                 out_shape=[jax.ShapeDtypeStruct((h, s, dk), F32), jax.ShapeDtypeStruct((h, s, dk), F32),
                            jax.ShapeDtypeStruct((h, s, dv), F32)],
                 grid=(h // ATT_PAIR, s // blk), in_specs=[qspec, kspec, vspec, yspec, yspec, lspec],
                 out_specs=[qspec, kspec, vspec], sem=("parallel", "arbitrary"))(q, k, v, y, dy, lse)


HALO = 8
CONV_CHUNK = 16


def _conv_tiles(x):
    s, c = x.shape[-2:]
    return s, c, _tile(s, (ROW_BLK,)), _tile(c, (CONV_COLS,))


def _conv_specs(bs, cw, lead=()):
    zero = (0,) * len(lead)
    blk = pl.BlockSpec(lead + (None, bs, cw), lambda p, j, i: zero + (p, i, j))
    halo = pl.BlockSpec(lead + (None, HALO, cw), lambda p, j, i: zero + (p, jnp.maximum(i * (bs // HALO) - 1, 0), j))
    w = lambda kk: pl.BlockSpec(lead + (None, kk, cw), lambda p, j, i: zero + (p, 0, j))
    return blk, halo, w


def _stage(scr, x_ref, halo_ref, first):
    scr[0:HALO, :] = jnp.where(first, 0.0, halo_ref[...])
    scr[HALO:, :] = x_ref[...]


def _conv_taps(scr, kk, r0):
    return [scr[pl.ds(HALO - (kk - 1) + k + r0, CONV_CHUNK), :] for k in range(kk)]


def _conv_sum(taps, w_ref, b_ref):
    u = b_ref[...] + taps[0] * w_ref[0:1, :]
    for k in range(1, len(taps)):
        u = u + taps[k] * w_ref[k:k + 1, :]
    return u


def _fold(x):
    out = x[0:8]
    for r in range(8, CONV_CHUNK, 8):
        out = out + x[r:r + 8]
    return out


class _TapSums:
    def __init__(self, kk, cw):
        self.w = [jnp.zeros((8, cw), F32) for _ in range(kk)]
        self.b = jnp.zeros((8, cw), F32)

    def add(self, du, taps):
        self.w = [a + _fold(du * t) for a, t in zip(self.w, taps)]
        self.b = self.b + _fold(du)

    def flush(self, dw_ref, db_ref):
        for k, a in enumerate(self.w):
            dw_ref[k:k + 1, :] += jnp.sum(a, axis=0, keepdims=True)
        db_ref[...] += jnp.sum(self.b, axis=0, keepdims=True)


def _silu_grad(u):
    sg = _sigmoid(u)
    return sg * (1.0 + u * (1.0 - sg))


def conv_silu_fwd(x, w, b, *, name):
    s, c, bs, cw = _conv_tiles(x)
    kk = w.shape[1]

    def body(x_ref, h_ref, w_ref, b_ref, o_ref, scr):
        _stage(scr, x_ref, h_ref, pl.program_id(2) == 0)
        for r0 in range(0, bs, CONV_CHUNK):
            u = _conv_sum(_conv_taps(scr, kk, r0), w_ref, b_ref)
            o_ref[pl.ds(r0, CONV_CHUNK), :] = u * _sigmoid(u)

    blk, halo, wspec = _conv_specs(bs, cw)
    return _call(body, name=name, out_shape=jax.ShapeDtypeStruct(x.shape, F32), grid=(x.shape[0], c // cw, s // bs),
                 in_specs=[blk, halo, wspec(kk), wspec(1)], out_specs=blk, scratch=[pltpu.VMEM((bs + HALO, cw), F32)],
                 sem=("parallel", "parallel", "arbitrary"))(x, x, w, b)


def conv_silu_bwd(x, dy, w, b, *, name):
    s, c, bs, cw = _conv_tiles(x)
    kk = w.shape[1]

    def body(x_ref, h_ref, w_ref, b_ref, dy_ref, du_ref, dw_ref, db_ref, scr):
        i = pl.program_id(2)

        @pl.when(i == 0)
        def _():
            dw_ref[...] = jnp.zeros_like(dw_ref)
            db_ref[...] = jnp.zeros_like(db_ref)

        _stage(scr, x_ref, h_ref, i == 0)
        sums = _TapSums(kk, cw)
        for r0 in range(0, bs, CONV_CHUNK):
            taps = _conv_taps(scr, kk, r0)
            du = dy_ref[pl.ds(r0, CONV_CHUNK), :] * _silu_grad(_conv_sum(taps, w_ref, b_ref))
            du_ref[pl.ds(r0, CONV_CHUNK), :] = du
            sums.add(du, taps)
        sums.flush(dw_ref, db_ref)

    blk, halo, wspec = _conv_specs(bs, cw)
    return _call(body, name=name,
                 out_shape=[jax.ShapeDtypeStruct(x.shape, F32), jax.ShapeDtypeStruct(w.shape, F32),
                            jax.ShapeDtypeStruct(b.shape, F32)],
                 grid=(x.shape[0], c // cw, s // bs), in_specs=[blk, halo, wspec(kk), wspec(1), blk],
                 out_specs=[blk, wspec(kk), wspec(1)], scratch=[pltpu.VMEM((bs + HALO, cw), F32)],
                 sem=("parallel", "parallel", "arbitrary"))(x, x, w, b, dy)


def _glu_view(a):
    return a.reshape((2, a.shape[0] // 2) + a.shape[1:])


def conv_glu_fwd(x, w, b, *, name):
    s, c, bs, cw = _conv_tiles(x)
    kk = w.shape[1]
    half = x.shape[0] // 2

    def body(x_ref, h_ref, w_ref, b_ref, o_ref, gscr, vscr):
        first = pl.program_id(2) == 0
        _stage(gscr, x_ref.at[0], h_ref.at[0], first)
        _stage(vscr, x_ref.at[1], h_ref.at[1], first)
        for r0 in range(0, bs, CONV_CHUNK):
            gate = _conv_sum(_conv_taps(gscr, kk, r0), w_ref.at[0], b_ref.at[0])
            val = _conv_sum(_conv_taps(vscr, kk, r0), w_ref.at[1], b_ref.at[1])
            o_ref[pl.ds(r0, CONV_CHUNK), :] = (gate * _sigmoid(gate) * val).astype(o_ref.dtype)

    blk, halo, wspec = _conv_specs(bs, cw, lead=(2,))
    out, _, _ = _conv_specs(bs, cw)
    xv = _glu_view(x)
    return _call(body, name=name, out_shape=jax.ShapeDtypeStruct((half, s, c), MXU_DTYPE), grid=(half, c // cw, s // bs),
                 in_specs=[blk, halo, wspec(kk), wspec(1)], out_specs=out, scratch=[pltpu.VMEM((bs + HALO, cw), F32)] * 2,
                 sem=("parallel", "parallel", "arbitrary"))(xv, xv, _glu_view(w), _glu_view(b))


def conv_glu_bwd(x, da, w, b, *, name):
    s, c, bs, cw = _conv_tiles(x)
    kk = w.shape[1]
    half = x.shape[0] // 2

    def body(x_ref, h_ref, w_ref, b_ref, da_ref, du_ref, dw_ref, db_ref, gscr, vscr):
        i = pl.program_id(2)

        @pl.when(i == 0)
        def _():
            dw_ref[...] = jnp.zeros_like(dw_ref)
            db_ref[...] = jnp.zeros_like(db_ref)

        _stage(gscr, x_ref.at[0], h_ref.at[0], i == 0)
        _stage(vscr, x_ref.at[1], h_ref.at[1], i == 0)
        gsums, vsums = _TapSums(kk, cw), _TapSums(kk, cw)
        for r0 in range(0, bs, CONV_CHUNK):
            gtaps, vtaps = _conv_taps(gscr, kk, r0), _conv_taps(vscr, kk, r0)
            gate = _conv_sum(gtaps, w_ref.at[0], b_ref.at[0])
            val = _conv_sum(vtaps, w_ref.at[1], b_ref.at[1])
            dav = da_ref[pl.ds(r0, CONV_CHUNK), :]
            dgate = dav * val * _silu_grad(gate)
            dval = dav * gate * _sigmoid(gate)
            du_ref[0, pl.ds(r0, CONV_CHUNK), :] = dgate
            du_ref[1, pl.ds(r0, CONV_CHUNK), :] = dval
            gsums.add(dgate, gtaps)
            vsums.add(dval, vtaps)
        gsums.flush(dw_ref.at[0], db_ref.at[0])
        vsums.flush(dw_ref.at[1], db_ref.at[1])

    blk, halo, wspec = _conv_specs(bs, cw, lead=(2,))
    daspec, _, _ = _conv_specs(bs, cw)
    xv, wv, bv = _glu_view(x), _glu_view(w), _glu_view(b)
    du, dw, db = _call(body, name=name,
                       out_shape=[jax.ShapeDtypeStruct(xv.shape, F32), jax.ShapeDtypeStruct(wv.shape, F32),
                                  jax.ShapeDtypeStruct(bv.shape, F32)],
                       grid=(half, c // cw, s // bs), in_specs=[blk, halo, wspec(kk), wspec(1), daspec],
                       out_specs=[blk, wspec(kk), wspec(1)], scratch=[pltpu.VMEM((bs + HALO, cw), F32)] * 2,
                       sem=("parallel", "parallel", "arbitrary"))(xv, xv, wv, bv, da)
    return du.reshape(x.shape), dw.reshape(w.shape), db.reshape(b.shape)


def conv_t(du, w, *, name):
    s, c, bs, cw = _conv_tiles(du)
    kk = w.shape[1]
    nb = s // bs

    def body(d_ref, h_ref, w_ref, o_ref, scr):
        last = pl.program_id(2) == nb - 1
        scr[0:bs, :] = d_ref[...]
        scr[bs:, :] = jnp.where(last, 0.0, h_ref[...])
        for r0 in range(0, bs, CONV_CHUNK):
            acc = scr[pl.ds(r0 + kk - 1, CONV_CHUNK), :] * w_ref[0:1, :]
            for k in range(1, kk):
                acc = acc + scr[pl.ds(r0 + kk - 1 - k, CONV_CHUNK), :] * w_ref[k:k + 1, :]
            o_ref[pl.ds(r0, CONV_CHUNK), :] = acc

    blk, _, wspec = _conv_specs(bs, cw)
    halo = pl.BlockSpec((None, HALO, cw), lambda q, j, i: (q, jnp.minimum((i + 1) * (bs // HALO), s // HALO - 1), j))
    return _call(body, name=name, out_shape=jax.ShapeDtypeStruct(du.shape, F32), grid=(du.shape[0], c // cw, nb),
                 in_specs=[blk, halo, wspec(kk)], out_specs=blk, scratch=[pltpu.VMEM((bs + HALO, cw), F32)],
                 sem=("parallel", "parallel", "arbitrary"))(du, du, w)


def _ssd_common(xbc_ref, tail_ref, dtrt_ref, bias_ref, biast_ref, alog_ref, alogt_ref):
    L = SSM_CHUNK
    raw = tail_ref[...] + bias_ref[...]
    dt = _softplus(raw)
    dtt = _softplus(dtrt_ref[...] + biast_ref[...])
    a = -jnp.exp(alog_ref[...])
    at = -jnp.exp(alogt_ref[...])
    row, col, lower = _tri(L, lambda r, c: r >= c)
    tril = lower.astype(F32)
    cs = _nn(tril, dt * a, HIGHEST)
    cst = _nt(dtt * at, tril, HIGHEST)
    bm = [xbc_ref[:, SSM_INNER + g * SSM_N: SSM_INNER + (g + 1) * SSM_N] for g in range(SSM_GROUPS)]
    off = SSM_INNER + SSM_GROUPS * SSM_N
    cm = [xbc_ref[:, off + g * SSM_N: off + (g + 1) * SSM_N] for g in range(SSM_GROUPS)]
    cb = [_bnt(cm[g], bm[g]) for g in range(SSM_GROUPS)]
    return raw, dt, a, lower, tril, cs, cst, bm, cm, cb


def _ssd_head(hh, xbc_ref, dt, cs, cst, lower):
    L = SSM_CHUNK
    ln = DT_LANE + hh
    x = xbc_ref[:, hh * SSM_P:(hh + 1) * SSM_P]
    dtc = dt[:, ln:ln + 1]
    csc = cs[:, ln:ln + 1]
    csr = cst[hh:hh + 1, :]
    decay = jnp.exp(jnp.where(lower, csc - csr, -1e30))
    last = cs[L - 1:L, ln:ln + 1]
    return x, dtc, csc, decay, jnp.exp(csc), jnp.exp(last - csc), jnp.exp(last)


def _ssd_inputs(tail, dt_bias, a_log, d_skip):
    H = SSM_HEADS
    lanes = lambda vec: jnp.pad(vec.reshape(1, H), ((0, 0), (DT_LANE, LANES - DT_LANE - H)))
    return (tail, tail[:, DT_LANE:DT_LANE + H].T, lanes(dt_bias), dt_bias.reshape(H, 1), lanes(a_log),
            a_log.reshape(H, 1), lanes(d_skip))


def ssd_fwd(xbc, tail, dt_bias, a_log, d_skip, *, name):
    s = xbc.shape[0]
    L, H, P, N = SSM_CHUNK, SSM_HEADS, SSM_P, SSM_N
    nc = s // L

    def body(xbc_ref, tail_ref, dtrt_ref, bias_ref, biast_ref, alog_ref, alogt_ref, d_ref, y_ref, hp_ref, state):
        @pl.when(pl.program_id(0) == 0)
        def _():
            state[...] = jnp.zeros_like(state)

        raw, dt, a, lower, tril, cs, cst, bm, cm, cb = _ssd_common(
            xbc_ref, tail_ref, dtrt_ref, bias_ref, biast_ref, alog_ref, alogt_ref)
        for hh in range(H):
            g = hh // (H // SSM_GROUPS)
            x, dtc, csc, decay, e, tau, gamma = _ssd_head(hh, xbc_ref, dt, cs, cst, lower)
            xdt = x * dtc
            hprev = state[hh]
            hp_ref[hh] = hprev
            skip = d_ref[:, DT_LANE + hh:DT_LANE + hh + 1]
            y = _bnn(cb[g] * decay, xdt) + _bnn(cm[g], hprev) * e + x * skip
            y_ref[:, hh * P:(hh + 1) * P] = y
            state[hh] = hprev * gamma + _btn(bm[g] * tau, xdt)

    row = lambda w: pl.BlockSpec((L, w), lambda c: (c, 0))
    small = lambda shp: pl.BlockSpec(shp, lambda c: (0, 0))
    return _call(body, name=name,
                 out_shape=[jax.ShapeDtypeStruct((s, SSM_INNER), F32), jax.ShapeDtypeStruct((nc, H, N, P), F32)],
                 grid=(nc,),
                 in_specs=[row(SSM_CONV_DIM), row(LANES), pl.BlockSpec((H, L), lambda c: (0, c)), small((1, LANES)),
                           small((H, 1)), small((1, LANES)), small((H, 1)), small((1, LANES))],
                 out_specs=[row(SSM_INNER), pl.BlockSpec((None, H, N, P), lambda c: (c, 0, 0, 0))],
                 scratch=[pltpu.VMEM((H, N, P), F32)], sem=("arbitrary",))(xbc, *_ssd_inputs(tail, dt_bias, a_log, d_skip))


def ssd_bwd(xbc, tail, dt_bias, a_log, d_skip, hprev_all, dy, *, name):
    s = xbc.shape[0]
    L, H, P, N = SSM_CHUNK, SSM_HEADS, SSM_P, SSM_N
    nc = s // L
    hg = H // SSM_GROUPS

    def body(xbc_ref, tail_ref, dtrt_ref, bias_ref, biast_ref, alog_ref, alogt_ref, d_ref, hp_ref, dy_ref,
             dxbc_ref, ddt_ref, dbias_ref, dalog_ref, dd_ref, dstate):
        @pl.when(pl.program_id(0) == 0)
        def _():
            dstate[...] = jnp.zeros_like(dstate)
            dbias_ref[...] = jnp.zeros_like(dbias_ref)
            dalog_ref[...] = jnp.zeros_like(dalog_ref)
            dd_ref[...] = jnp.zeros_like(dd_ref)

        raw, dt, a, lower, tril, cs, cst, bm, cm, cb = _ssd_common(
            xbc_ref, tail_ref, dtrt_ref, bias_ref, biast_ref, alog_ref, alogt_ref)
        lane = lax.broadcasted_iota(jnp.int32, (L, LANES), 1)
        lane1 = lax.broadcasted_iota(jnp.int32, (1, LANES), 1)
        rowi = lax.broadcasted_iota(jnp.int32, (L, 1), 0)
        ones = jnp.ones((L, LANES), F32)
        dcs_all = jnp.zeros((L, LANES), F32)
        ddt_x = jnp.zeros((L, LANES), F32)
        dd_row = jnp.zeros((1, LANES), F32)
        dbm = [jnp.zeros((L, N), F32) for _ in range(SSM_GROUPS)]
        dcm = [jnp.zeros((L, N), F32) for _ in range(SSM_GROUPS)]
        dcb = [jnp.zeros((L, L), F32) for _ in range(SSM_GROUPS)]
        for hh in range(H):
            g = hh // hg
            ln = DT_LANE + hh
            x, dtc, csc, decay, e, tau, gamma = _ssd_head(hh, xbc_ref, dt, cs, cst, lower)
            xdt = x * dtc
            hprev = hp_ref[hh]
            dhn = dstate[hh]
            dyh = dy_ref[:, hh * P:(hh + 1) * P]
            m = cb[g] * decay
            dxdt = _btn(m, dyh) + _bnn(bm[g] * tau, dhn)
            dm = jnp.where(lower, _bnt(dyh, xdt), 0.0)
            dcb[g] = dcb[g] + dm * decay
            dseg = dm * m
            dcs = jnp.sum(dseg, axis=1, keepdims=True) - _tn(dseg, ones, HIGHEST)[:, 0:1]
            edy = e * dyh
            dcm[g] = dcm[g] + _bnt(edy, hprev)
            dcs = dcs + e * jnp.sum(dyh * _bnn(cm[g], hprev), axis=1, keepdims=True)
            xdh = _bnt(xdt, dhn)
            dbm[g] = dbm[g] + tau * xdh
            dtau_tau = jnp.sum(bm[g] * xdh, axis=1, keepdims=True) * tau
            dlast = jnp.sum(dtau_tau, axis=0, keepdims=True) + gamma * jnp.sum(dhn * hprev, keepdims=True)
            dcs = dcs - dtau_tau + jnp.where(rowi == L - 1, dlast, 0.0)
            dstate[hh] = gamma * dhn + _btn(cm[g], edy)
            dcs_all = jnp.where(lane == ln, dcs, dcs_all)
            ddt_x = jnp.where(lane == ln, jnp.sum(dxdt * x, axis=1, keepdims=True), ddt_x)
            dxbc_ref[:, hh * P:(hh + 1) * P] = dxdt * dtc + d_ref[:, ln:ln + 1] * dyh
            dd_row = jnp.where(lane1 == ln, jnp.sum(dyh * x, keepdims=True), dd_row)
        off = SSM_INNER + SSM_GROUPS * SSM_N
        for g in range(SSM_GROUPS):
            dxbc_ref[:, SSM_INNER + g * N: SSM_INNER + (g + 1) * N] = dbm[g] + _btn(dcb[g], cm[g])
            dxbc_ref[:, off + g * N: off + (g + 1) * N] = dcm[g] + _bnn(dcb[g], bm[g])
        dda = _tn(tril, dcs_all, HIGHEST)
        head_lane = (lane >= DT_LANE) & (lane < DT_LANE + H)
        draw = jnp.where(head_lane, (dda * a + ddt_x) * _sigmoid(raw), 0.0)
        ddt_ref[...] = draw
        dbias_ref[...] += jnp.sum(draw, axis=0, keepdims=True)
        dalog_ref[...] += jnp.sum(jnp.where(head_lane, dda * dt, 0.0), axis=0, keepdims=True) * a
        dd_ref[...] += dd_row

    rev = lambda c: nc - 1 - c
    row = lambda w: pl.BlockSpec((L, w), lambda c: (rev(c), 0))
    small = lambda shp: pl.BlockSpec(shp, lambda c: (0, 0))
    acc = pl.BlockSpec((1, LANES), lambda c: (0, 0))
    vec = jax.ShapeDtypeStruct((1, LANES), F32)
    return _call(body, name=name,
                 out_shape=[jax.ShapeDtypeStruct((s, SSM_CONV_DIM), F32), jax.ShapeDtypeStruct((s, LANES), F32), vec, vec, vec],
                 grid=(nc,),
                 in_specs=[row(SSM_CONV_DIM), row(LANES), pl.BlockSpec((H, L), lambda c: (0, rev(c))), small((1, LANES)),
                           small((H, 1)), small((1, LANES)), small((H, 1)), small((1, LANES)),
                           pl.BlockSpec((None, H, N, P), lambda c: (rev(c), 0, 0, 0)), row(SSM_INNER)],
                 out_specs=[row(SSM_CONV_DIM), row(LANES), acc, acc, acc],
                 scratch=[pltpu.VMEM((H, N, P), F32)], sem=("arbitrary",))(
        xbc, *_ssd_inputs(tail, dt_bias, a_log, d_skip), hprev_all, dy)


def _heads(x2d, n, d):
    s = x2d.shape[0]
    return x2d.reshape(s, n, d).transpose(1, 0, 2)


def _unheads(x3d):
    n, s, d = x3d.shape
    return x3d.transpose(1, 0, 2).reshape(s, n * d)


def layer_fwd(h, p, tabs, li):
    s = h.shape[0]
    tabq, tabt = tabs
    nm = lambda t: f"L{li}_{t}"
    r = {'h': h}
    hn = rms_fwd(h, p['mix_norm'], name=nm('mixnorm'), out_dtype=MXU_DTYPE)
    proj = mm(hn, p['w_in'], name=nm('proj'))
    r.update(hn=hn, proj=proj)
    qkv = proj[:, :3 * SB_WIDTH].astype(MXU_DTYPE)
    ya, tot = sb_fwd(qkv, name=nm('sb_fwd'))
    yan = rms_fwd(ya, p['sb_out_norm'], name=nm('sbnorm'), out_dtype=MXU_DTYPE)
    r.update(qkv=qkv, ya=ya, tot=tot)
    z = proj[:, 768:1280]
    xbc = proj[None, :, 1280:2048]
    tail = proj[:, TAIL:TAIL + LANES]
    xbc_act = conv_silu_fwd(xbc, p['ssm_conv_w'], p['ssm_conv_b'], name=nm('ssmconv'))[0]
    y_ssm, hprev = ssd_fwd(xbc_act, tail, p['ssm_dt_bias'], p['ssm_a_log'], p['ssm_d'], name=nm('ssd_fwd'))
    ybn = rms_fwd(y_ssm, p['ssm_out_norm'], name=nm('ssmnorm'), gate=z, out_dtype=MXU_DTYPE)
    r.update(z=z, xbc=xbc, tail=tail, xbc_act=xbc_act, y_ssm=y_ssm, hprev=hprev)
    cq = proj[:, 2048:2304]
    ckv = proj[:, 2304:2432]
    qn = rms_fwd(cq, p['mla_q_norm'], name=nm('qnorm'), out_dtype=MXU_DTYPE)
    q_r = rope(mm(qn, p['mla_w_uq'], name=nm('uq'))[None], tabq, name=nm('ropeq'))
    kvn = rms_fwd(ckv, p['mla_kv_norm'], name=nm('kvnorm'), out_dtype=MXU_DTYPE)
    kv = mm(kvn, p['mla_w_ukv'], name=nm('ukv'))
    k_pe = rope(tail[None], tabt, name=nm('ropek'))[:, :MLA_ROPE]
    qh = _heads(q_r, MLA_HEADS, MLA_QK).astype(MXU_DTYPE)
    kvh = _heads(kv, MLA_HEADS, MLA_NOPE + MLA_V)
    kh = jnp.concatenate([kvh[..., :MLA_NOPE], jnp.broadcast_to(k_pe[None], (MLA_HEADS, s, MLA_ROPE))],
                         axis=-1).astype(MXU_DTYPE)
    vh = kvh[..., MLA_NOPE:].astype(MXU_DTYPE)
    yc_h, lse = mla_fwd(qh, kh, vh, name=nm('mla_fwd'))
    yc = _unheads(yc_h)
    ycn = rms_fwd(yc, p['mla_out_norm'], name=nm('mlanorm'), out_dtype=MXU_DTYPE)
    r.update(cq=cq, ckv=ckv, qn=qn, kvn=kvn, qh=qh, kh=kh, vh=vh, yc_h=yc_h, yc=yc, lse=lse)
    ycat = jnp.concatenate([yan, ybn, ycn], axis=1)
    h1 = mm(ycat, p['w_out'], name=nm('outproj'), res=h)
    hn2 = rms_fwd(h1, p['ffn_norm'], name=nm('ffnnorm'), out_dtype=MXU_DTYPE)
    up = mm(hn2, p['ffn_w_up'], name=nm('up'), bb='o')
    act = conv_glu_fwd(up, p['ffn_conv_w'], p['ffn_conv_b'], name=nm('glu'))
    h2 = mm(act, p['ffn_w_down'], name=nm('down'), ab='k', bb='k', res=h1)
    r.update(ycat=ycat, h1=h1, hn2=hn2, up=up, act=act)
    return h2, r


def layer_bwd(dh2, p, r, tabs, li):
    s = dh2.shape[0]
    tabq, tabt = tabs
    nm = lambda t: f"L{li}_{t}"
    g = {}
    dact = mm(dh2, p['ffn_w_down'], name=nm('d_down_x'), tb=True, bb='o')
    g['ffn_w_down'] = mm(r['act'], dh2, name=nm('d_down_w'), ta=True, ab='o')
    du, g['ffn_conv_w'], g['ffn_conv_b'] = conv_glu_bwd(r['up'], dact, p['ffn_conv_w'], p['ffn_conv_b'], name=nm('d_glu'))
    dup = conv_t(du, p['ffn_conv_w'], name=nm('d_ffnconv'))
    g['ffn_w_up'] = mm(r['hn2'], dup, name=nm('d_up_w'), ta=True, bb='o')
    dhn2 = mm(dup, p['ffn_w_up'], name=nm('d_up_x'), tb=True, ab='k', bb='k')
    dh1, dg = rms_bwd(r['h1'], p['ffn_norm'], dhn2, name=nm('d_ffnnorm'), add=dh2)
    g['ffn_norm'] = dg[0]
    dycat = mm(dh1, p['w_out'], name=nm('d_out_x'), tb=True)
    g['w_out'] = mm(r['ycat'], dh1, name=nm('d_out_w'), ta=True)
    dya, dg = rms_bwd(r['ya'], p['sb_out_norm'], dycat[:, :256], name=nm('d_sbnorm'))
    g['sb_out_norm'] = dg[0]
    dq, dk, dv = sb_bwd(r['qkv'], dya, r['tot'], name=nm('sb_bwd'))
    dyssm, dz, dg = rms_bwd(r['y_ssm'], p['ssm_out_norm'], dycat[:, 256:768], name=nm('d_ssmnorm'), gate=r['z'])
    g['ssm_out_norm'] = dg[0]
    dxbc_act, ddt_tail, dbias, dalog, dd = ssd_bwd(r['xbc_act'], r['tail'], p['ssm_dt_bias'], p['ssm_a_log'],
                                                   p['ssm_d'], r['hprev'], dyssm, name=nm('ssd_bwd'))
    hl = slice(DT_LANE, DT_LANE + SSM_HEADS)
    g['ssm_dt_bias'], g['ssm_a_log'], g['ssm_d'] = dbias[0, hl], dalog[0, hl], dd[0, hl]
    dxbc_u, g['ssm_conv_w'], g['ssm_conv_b'] = conv_silu_bwd(r['xbc'], dxbc_act[None], p['ssm_conv_w'], p['ssm_conv_b'],
                                                             name=nm('d_ssmact'))
    dxbc = conv_t(dxbc_u, p['ssm_conv_w'], name=nm('d_ssmconv'))[0]
    dyc, dg = rms_bwd(r['yc'], p['mla_out_norm'], dycat[:, 768:], name=nm('d_mlanorm'))
    g['mla_out_norm'] = dg[0]
    dqh, dkh, dvh = mla_bwd(r['qh'], r['kh'], r['vh'], r['yc_h'], _heads(dyc, MLA_HEADS, MLA_V), r['lse'], name=nm('mla_bwd'))
    dq_c = rope(_unheads(dqh)[None], tabq, name=nm('d_ropeq'), backward=True)
    g['mla_w_uq'] = mm(r['qn'], dq_c, name=nm('d_uq_w'), ta=True)
    dcq, dg = rms_bwd(r['cq'], p['mla_q_norm'], mm(dq_c, p['mla_w_uq'], name=nm('d_uq_x'), tb=True), name=nm('d_qnorm'))
    g['mla_q_norm'] = dg[0]
    dkv = _unheads(jnp.concatenate([dkh[..., :MLA_NOPE], dvh], axis=-1))
    g['mla_w_ukv'] = mm(r['kvn'], dkv, name=nm('d_ukv_w'), ta=True)
    dckv, dg = rms_bwd(r['ckv'], p['mla_kv_norm'], mm(dkv, p['mla_w_ukv'], name=nm('d_ukv_x'), tb=True), name=nm('d_kvnorm'))
    g['mla_kv_norm'] = dg[0]
    dkpe = jnp.pad(dkh[..., MLA_NOPE:], ((0, 0), (0, 0), (0, LANES - MLA_ROPE)))
    dtail = rope(dkpe, tabt, name=nm('d_ropek'), backward=True, add=ddt_tail)
    dproj = jnp.concatenate([dq, dk, dv, dz, dxbc, dcq, dckv, dtail], axis=1)
    g['w_in'] = mm(r['hn'], dproj, name=nm('d_proj_w'), ta=True)
    dhn = mm(dproj, p['w_in'], name=nm('d_proj_x'), tb=True)
    dh, dg = rms_bwd(r['h'], p['mix_norm'], dhn, name=nm('d_mixnorm'), add=dh1)
    g['mix_norm'] = dg[0]
    return dh, g


def _w_in_placement():
    c = np.arange(D_IN)
    dest = np.where(c < 2048, c, np.where(c < 2056, c + (D_IN - 2056), c - 8))
    dest = jnp.asarray(dest.reshape(N_DEV, D_IN // N_DEV, 1), jnp.int32)
    return (dest == jnp.arange(D_IN_PAD, dtype=jnp.int32)).astype(MXU_DTYPE)


def _owner_major(full, axis):
    shp = full.shape
    return jnp.moveaxis(full.reshape(shp[:axis] + (N_DEV, shp[axis] // N_DEV) + shp[axis + 1:]), axis, 0)


def _owner_join(parts, axis):
    moved = jnp.moveaxis(parts, 0, axis)
    shp = moved.shape
    return moved.reshape(shp[:axis] + (shp[axis] * shp[axis + 1],) + shp[axis + 2:])


def assemble_params(gathered, replicated):
    L = DEPTH
    place = _w_in_placement()
    w_in = mm(gathered['w_in'].reshape(N_DEV, L * D_MODEL, D_IN // N_DEV), place, name='place_w_in', ab='k', bb='k',
              out_dtype=MXU_DTYPE).reshape(L, D_MODEL, D_IN_PAD)
    out = dict(replicated)
    out['w_in'] = w_in
    out['ffn_w_up'] = jnp.moveaxis(gathered['ffn_w_up'], 1, 0)
    out['w_out'] = _owner_join(gathered['w_out'], 1)
    out['ffn_w_down'] = _owner_join(gathered['ffn_w_down'], 1).reshape(L, N_DEV // 2, FF_SHARD, D_MODEL)
    out['mla_w_uq'] = _owner_join(gathered['mla_w_uq'], 2)
    out['mla_w_ukv'] = _owner_join(gathered['mla_w_ukv'], 2)
    out['ssm_conv_w'] = _owner_join(gathered['ssm_conv_w'], 2)[:, None]
    out['ffn_conv_w'] = jnp.moveaxis(gathered['ffn_conv_w'], 1, 0)
    out['ssm_conv_b'] = replicated['ssm_conv_b'].reshape(L, 1, 1, SSM_CONV_DIM)
    out['ffn_conv_b'] = replicated['ffn_conv_b'].reshape(L, N_DEV, 1, FF_SHARD)
    return out


def owner_parts(grads):
    L = DEPTH
    st = lambda n: jnp.stack([g[n] for g in grads])
    place = _w_in_placement()
    parts = {
        'w_in': mm(st('w_in').reshape(L * D_MODEL, D_IN_PAD), place, name='unplace_w_in', tb=True, bb='o',
                   precision=HIGHEST).reshape(N_DEV, L, D_MODEL, D_IN // N_DEV),
        'ffn_w_up': jnp.moveaxis(st('ffn_w_up'), 1, 0),
        'w_out': _owner_major(st('w_out'), 1),
        'ffn_w_down': _owner_major(st('ffn_w_down').reshape(L, D_FF, D_MODEL), 1),
        'mla_w_uq': _owner_major(st('mla_w_uq'), 2),
        'mla_w_ukv': _owner_major(st('mla_w_ukv'), 2),
        'ssm_conv_w': _owner_major(st('ssm_conv_w')[:, 0], 2),
        'ffn_conv_w': jnp.moveaxis(st('ffn_conv_w'), 1, 0),
    }
    rep = {n: st(n) for n in REPLICATED if n not in ('final_norm', 'ssm_conv_b', 'ffn_conv_b')}
    rep['ssm_conv_b'] = st('ssm_conv_b').reshape(L, SSM_CONV_DIM)
    rep['ffn_conv_b'] = st('ffn_conv_b').reshape(L, 2 * D_FF)
    return parts, rep


def local_step(x, positions, target, params):
    s = x.shape[0]
    tabs = _rope_tables(positions, s)
    h = x
    saved = []
    for li in range(DEPTH):
        p = {n: params[n][li] for n in WEIGHTS if n != 'final_norm'}
        h, r = layer_fwd(h, p, tabs, li)
        saved.append((p, r))
    y = rms_fwd(h, params['final_norm'], name='finalnorm')
    dy, loss = loss_head(y, target, name='loss')
    dh, dg = rms_bwd(h, params['final_norm'], dy, name='d_finalnorm')
    per_layer = []
    for li in reversed(range(DEPTH)):
        p, r = saved[li]
        dh, g = layer_bwd(dh, p, r, tabs, li)
        per_layer.append(g)
    per_layer.reverse()
    parts, rep = owner_parts(per_layer)
    rep['final_norm'] = dg[0]
    return loss[0, 0], dh, parts, rep


MESH = pl.DeviceIdType.MESH
HBM = pl.BlockSpec(memory_space=pltpu.HBM)


def _flip(v, bit):
    return 1 - v if bit else v


def all_gather(blocks, *, name):
    n = len(blocks)

    def body(*refs):
        x_refs, out_refs = refs[:n], refs[n:2 * n]
        send_sems, recv_sems, local_sems = refs[2 * n:]
        x, y, c = lax.axis_index("x"), lax.axis_index("y"), lax.axis_index("c")
        me, sibling = (x, y, c), (x, y, 1 - c)
        chips = [(1 - x, y), (x, 1 - y), (1 - x, 1 - y)]

        def slot(b, px, py, pc):
            return out_refs[b].at[4 * px + 2 * py + pc]

        def copy(b, k, blk, to, src=None):
            return pltpu.make_async_remote_copy(src_ref=slot(b, *blk) if src is None else src, dst_ref=slot(b, *blk),
                                                send_sem=send_sems.at[b, k], recv_sem=recv_sems.at[b, k],
                                                device_id=to, device_id_type=MESH)

        mine = [pltpu.make_async_copy(x_refs[b], slot(b, *me), local_sems.at[b]) for b in range(n)]
        for cp in mine:
            cp.start()
        first = []
        for b in range(n):
            first.append(copy(b, 0, me, sibling, src=x_refs[b]))
            first += [copy(b, 1 + j, me, (*chip, c), src=x_refs[b]) for j, chip in enumerate(chips)]
        for cp in first:
            cp.start()
        passed = []
        for j, chip in enumerate(chips):
            for b in range(n):
                copy(b, 1 + j, (*chip, c), me).wait_recv()
                fwd = copy(b, 4 + j, (*chip, c), sibling)
                fwd.start()
                passed.append(fwd)
        for b in range(n):
            copy(b, 0, sibling, me).wait_recv()
            for j, chip in enumerate(chips):
                copy(b, 4 + j, (*chip, 1 - c), me).wait_recv()
        for cp in first + passed:
            cp.wait_send()
        for cp in mine:
            cp.wait()

    return pl.pallas_call(
        body, name=name, out_shape=[jax.ShapeDtypeStruct((N_DEV,) + b.shape, b.dtype) for b in blocks],
        in_specs=[HBM] * n, out_specs=[HBM] * n,
        scratch_shapes=[pltpu.SemaphoreType.DMA((n, 7)), pltpu.SemaphoreType.DMA((n, 7)), pltpu.SemaphoreType.DMA((n,))],
    )(*blocks)


def all_to_all(parts, *, name):
    n = len(parts)

    def body(*refs):
        g_refs, r_refs = refs[:n], refs[n:2 * n]
        send_sems, recv_sems, local_sems = refs[2 * n:]
        x, y, c = lax.axis_index("x"), lax.axis_index("y"), lax.axis_index("c")
        me = 4 * x + 2 * y + c
        mine = [pltpu.make_async_copy(g_refs[b].at[me], r_refs[b].at[me], local_sems.at[b]) for b in range(n)]
        for cp in mine:
            cp.start()
        copies = []
        for k in range(1, N_DEV):
            px, py, pc = _flip(x, k & 4), _flip(y, k & 2), _flip(c, k & 1)
            peer = 4 * px + 2 * py + pc
            for b in range(n):
                cp = pltpu.make_async_remote_copy(src_ref=g_refs[b].at[peer], dst_ref=r_refs[b].at[me],
                                                  send_sem=send_sems.at[b, k - 1], recv_sem=recv_sems.at[b, k - 1],
                                                  device_id=(px, py, pc), device_id_type=MESH)
                cp.start()
                copies.append(cp)
        for cp in copies:
            cp.wait_send()
            cp.wait_recv()
        for cp in mine:
            cp.wait()

    return pl.pallas_call(
        body, name=name, out_shape=[jax.ShapeDtypeStruct(p.shape, p.dtype) for p in parts],
        in_specs=[HBM] * n, out_specs=[HBM] * n,
        scratch_shapes=[pltpu.SemaphoreType.DMA((n, 7)), pltpu.SemaphoreType.DMA((n, 7)), pltpu.SemaphoreType.DMA((n,))],
    )(*parts)


def adamw(parts, w, m, v, *, name):
    r, wd = w.shape
    br = _tile(r, (256, 128, 64, 32, 16, 8))
    c1 = 1.0 - ADAM_B1 ** ADAM_STEP
    c2 = 1.0 - ADAM_B2 ** ADAM_STEP

    def body(p_ref, w_ref, m_ref, v_ref, g_ref, d_ref, mo_ref, vo_ref):
        g = p_ref[0].astype(F32)
        for j in range(1, N_DEV):
            g = g + p_ref[j].astype(F32)
        mn = ADAM_B1 * m_ref[...] + (1.0 - ADAM_B1) * g
        vn = ADAM_B2 * v_ref[...] + (1.0 - ADAM_B2) * (g * g)
        g_ref[...] = g
        mo_ref[...] = mn
        vo_ref[...] = vn
        d_ref[...] = -ADAM_LR * ((mn / c1) / (jnp.sqrt(vn / c2) + ADAM_EPS) + ADAM_WD * w_ref[...])

    blk = pl.BlockSpec((br, wd), lambda i: (i, 0))
    out = jax.ShapeDtypeStruct((r, wd), F32)
    return _call(body, name=name, out_shape=[out] * 4, grid=(r // br,),
                 in_specs=[pl.BlockSpec((N_DEV, br, wd), lambda i: (0, i, 0)), blk, blk, blk], out_specs=[blk] * 4,
                 sem=("parallel",))(parts, w, m, v)


def _pack(arrs):
    flat = jnp.concatenate([a.reshape(-1) for a in arrs])
    rows = -(-flat.shape[0] // (8 * FLAT_W)) * 8
    return jnp.pad(flat, (0, rows * FLAT_W - flat.shape[0])).reshape(rows, FLAT_W)


def _unpack(flat, shapes):
    flat = flat.reshape(-1)
    out, off = [], 0
    for shp in shapes:
        n = int(np.prod(shp))
        out.append(flat[off:off + n].reshape(shp))
        off += n
    return out


def kernel(x, positions, mix_norm, w_in, sb_out_norm, ssm_conv_w, ssm_conv_b, ssm_dt_bias, ssm_a_log, ssm_d, ssm_out_norm, mla_q_norm, mla_w_uq, mla_kv_norm, mla_w_ukv, mla_out_norm, w_out, ffn_norm, ffn_w_up, ffn_conv_w, ffn_conv_b, ffn_w_down, final_norm, loss_target, m_mix_norm, m_w_in, m_sb_out_norm, m_ssm_conv_w, m_ssm_conv_b, m_ssm_dt_bias, m_ssm_a_log, m_ssm_d, m_ssm_out_norm, m_mla_q_norm, m_mla_w_uq, m_mla_kv_norm, m_mla_w_ukv, m_mla_out_norm, m_w_out, m_ffn_norm, m_ffn_w_up, m_ffn_conv_w, m_ffn_conv_b, m_ffn_w_down, m_final_norm, v_mix_norm, v_w_in, v_sb_out_norm, v_ssm_conv_w, v_ssm_conv_b, v_ssm_dt_bias, v_ssm_a_log, v_ssm_d, v_ssm_out_norm, v_mla_q_norm, v_mla_w_uq, v_mla_kv_norm, v_mla_w_ukv, v_mla_out_norm, v_w_out, v_ffn_norm, v_ffn_w_up, v_ffn_conv_w, v_ffn_conv_b, v_ffn_w_down, v_final_norm):
    args = locals()
    w = {n: args[n] for n in WEIGHTS}
    m = {n: args['m_' + n] for n in WEIGHTS}
    v = {n: args['v_' + n] for n in WEIGHTS}
    sharded = list(SHARDED)

    wire = [w[n] if n in VPU_WEIGHTS else w[n].astype(BF16) for n in sharded]
    gathered = dict(zip(sharded, all_gather(wire, name='gather_weights')))
    params = assemble_params(gathered, {n: w[n] for n in REPLICATED})

    loss, dx, parts, rep = local_step(x[0], positions[0], loss_target[0], params)
    loss = lax.psum(loss, ("x", "y", "c"))

    recv = all_to_all([parts[n].astype(BF16) for n in sharded], name='scatter_grads')
    res = {kind: {} for kind in 'gdmv'}
    for n, rv in zip(sharded, recv):
        shp = w[n].shape
        two_d = (int(np.prod(shp[:-1])), shp[-1])
        outs = adamw(rv.reshape((N_DEV,) + two_d), w[n].reshape(two_d), m[n].reshape(two_d), v[n].reshape(two_d),
                     name='adamw_' + n)
        for kind, o in zip('gdmv', outs):
            res[kind][n] = o.reshape(shp)

    rep_shapes = [w[n].shape for n in REPLICATED]
    (rparts,) = all_gather([_pack([rep[n] for n in REPLICATED])], name='gather_small_grads')
    rflat = lambda d: _pack([d[n] for n in REPLICATED])
    routs = adamw(rparts, rflat(w), rflat(m), rflat(v), name='adamw_replicated')
    for kind, o in zip('gdmv', routs):
        res[kind].update(zip(REPLICATED, _unpack(o, rep_shapes)))

    return (loss, dx[None], *[res['g'][n] for n in WEIGHTS], *[res['d'][n] for n in WEIGHTS],
            *[res['m'][n] for n in WEIGHTS], *[res['v'][n] for n in WEIGHTS])
```

```python
import numpy as np
import jax
import jax.numpy as jnp
from jax import lax
from jax.experimental import pallas as pl
from jax.experimental.pallas import tpu as pltpu

F32 = jnp.float32
BF16 = jnp.bfloat16
MXU_DTYPE = jnp.bfloat16
HIGHEST = lax.Precision.HIGHEST
WIRE_DTYPE = jnp.bfloat16

N_DEV = 8
D_MODEL = 1024
DEPTH = 2
EPS = 1e-6
SB_HEADS, SB_DIM = 4, 64
SB_WIDTH = SB_HEADS * SB_DIM
SSM_HEADS, SSM_P, SSM_GROUPS, SSM_N, SSM_CONV, SSM_CHUNK = 8, 64, 2, 64, 4, 128
SSM_INNER = SSM_HEADS * SSM_P
SSM_CONV_DIM = SSM_INNER + 2 * SSM_GROUPS * SSM_N
MLA_HEADS, MLA_NOPE, MLA_ROPE, MLA_V, MLA_Q_RANK, MLA_KV_RANK = 4, 64, 32, 64, 256, 128
MLA_QK = MLA_NOPE + MLA_ROPE
ROPE_THETA = 10000.0
D_IN = 2472
D_IN_PAD = 2560
TAIL = 2432
DT_LANE = 32
D_FF = 2816
FF_SHARD = 2 * D_FF // N_DEV
ADAM_LR, ADAM_B1, ADAM_B2, ADAM_EPS, ADAM_WD, ADAM_STEP = 0.001, 0.9, 0.999, 1e-08, 0.01, 10

LANES = 128
ATT_BLK = 256
ATT_UNROLL = 4
ROW_BLK = 512
CONV_COLS = 256
FLAT_W = 1024
VMEM_LIMIT = 56 << 20
MM_TM = (1024, 512, 256, 128)
MM_TN = (1280, 1024, 768, 640, 512, 384, 256, 128)
MM_TK = (1280, 1024, 512, 256, 128)

WEIGHTS = ['mix_norm', 'w_in', 'sb_out_norm', 'ssm_conv_w', 'ssm_conv_b', 'ssm_dt_bias', 'ssm_a_log', 'ssm_d',
           'ssm_out_norm', 'mla_q_norm', 'mla_w_uq', 'mla_kv_norm', 'mla_w_ukv', 'mla_out_norm', 'w_out',
           'ffn_norm', 'ffn_w_up', 'ffn_conv_w', 'ffn_conv_b', 'ffn_w_down', 'final_norm']
SHARDED = {'w_in': 2, 'ssm_conv_w': 2, 'mla_w_uq': 2, 'mla_w_ukv': 2, 'w_out': 1, 'ffn_w_up': 2, 'ffn_conv_w': 2,
           'ffn_w_down': 1}
VPU_WEIGHTS = ('ssm_conv_w', 'ffn_conv_w')
REPLICATED = [n for n in WEIGHTS if n not in SHARDED]


def _call(body, *, name, out_shape, grid=(), in_specs=None, out_specs=None, scratch=(), sem=None, **kw):
    params = dict(vmem_limit_bytes=VMEM_LIMIT)
    if sem is not None:
        params['dimension_semantics'] = sem
    return pl.pallas_call(body, name=name, out_shape=out_shape, grid=grid, in_specs=in_specs, out_specs=out_specs,
                          scratch_shapes=list(scratch), compiler_params=pltpu.CompilerParams(**params), **kw)


def _tile(n, prefs):
    for t in prefs:
        if n % t == 0:
            return t
    return n


def _dot(a, b, dims, precision=None):
    return lax.dot_general(a, b, (dims, ((), ())), preferred_element_type=F32, precision=precision)


def _nn(a, b, precision=None):
    return _dot(a, b, ((1,), (0,)), precision)


def _nt(a, b, precision=None):
    return _dot(a, b, ((1,), (1,)), precision)


def _tn(a, b, precision=None):
    return _dot(a, b, ((0,), (0,)), precision)


def _mxu(f):
    return lambda a, b: f(a.astype(MXU_DTYPE), b.astype(MXU_DTYPE))


_bnn, _bnt, _btn = _mxu(_nn), _mxu(_nt), _mxu(_tn)


def _split2(x):
    hi = x.astype(MXU_DTYPE)
    lo = (x - hi.astype(F32)).astype(MXU_DTYPE)
    return hi, lo


def _sigmoid(x):
    return 0.5 * jnp.tanh(0.5 * x) + 0.5


def _softplus(x):
    return jnp.maximum(x, 0.0) + jnp.log1p(jnp.exp(-jnp.abs(x)))


def _softplus_att(x):
    return jnp.maximum(x, 0.0) + jnp.log(1.0 + jnp.exp(-jnp.abs(x)))


def _cum(x, u):
    n = x.shape[0]
    r = _nn(jnp.concatenate(_split2(x), axis=0), u)
    return r[:n] + r[n:]


def _causal_loop(qi, tile, carry, reverse):
    block = lambda j: qi - 1 - j if reverse else j

    def several(i, cr):
        for u in range(ATT_UNROLL):
            cr = tile(block(i * ATT_UNROLL + u), cr, False)
        return cr

    def rest(cr):
        full = qi // ATT_UNROLL
        cr = lax.fori_loop(0, full, several, cr)
        return lax.fori_loop(full * ATT_UNROLL, qi, lambda j, c: tile(block(j), c, False), cr)

    return rest(tile(qi, carry, True)) if reverse else tile(qi, rest(carry), True)


def mm(a, b, *, name, ta=False, tb=False, res=None, out_dtype=F32, ab=None, bb=None, precision=None):
    a2, b2 = a.shape[-2:], b.shape[-2:]
    (kdim, m) = a2 if ta else a2[::-1]
    (n, k2) = b2 if tb else b2[::-1]
    assert kdim == k2, (a.shape, b.shape, ta, tb)
    assert (ab == 'k') == (bb == 'k')
    kb = ab == 'k'
    nb = a.shape[0] if ab == 'o' else (b.shape[0] if bb == 'o' else None)
    tm, tn = _tile(m, MM_TM), _tile(n, MM_TN)
    tk = kdim if kb else _tile(kdim, MM_TK)
    nk = a.shape[0] if kb else kdim // tk
    dims = ((0 if ta else 1,), (1 if tb else 0,))
    op_dtype = F32 if precision is not None else MXU_DTYPE

    def body(*refs):
        a_ref, b_ref = refs[0], refs[1]
        r_ref = refs[2] if res is not None else None
        o_ref = refs[3] if res is not None else refs[2]
        part = _dot(a_ref[...].astype(op_dtype), b_ref[...].astype(op_dtype), dims, precision)

        def finish(out):
            if res is not None:
                out = out + r_ref[...]
            o_ref[...] = out.astype(out_dtype)

        if nk == 1:
            finish(part)
            return
        acc = refs[-1]
        k = pl.program_id(3)

        @pl.when(k == 0)
        def _():
            acc[...] = part

        @pl.when(k > 0)
        def _():
            acc[...] += part

        @pl.when(k == nk - 1)
        def _():
            finish(acc[...])

    def spec(blk, idx, how):
        if how is None:
            return pl.BlockSpec(blk, idx)
        if how == 'o':
            return pl.BlockSpec((None,) + blk, lambda p, i, j, k: (p,) + idx(p, i, j, k))
        return pl.BlockSpec((None,) + blk, lambda p, i, j, k: (k,) + idx(p, i, j, 0))

    a_spec = spec((tk, tm), lambda p, i, j, k: (k, i), ab) if ta else spec((tm, tk), lambda p, i, j, k: (i, k), ab)
    b_spec = spec((tn, tk), lambda p, i, j, k: (j, k), bb) if tb else spec((tk, tn), lambda p, i, j, k: (k, j), bb)
    o_spec = spec((tm, tn), lambda p, i, j, k: (i, j), None if nb is None else 'o')
    ins, specs = [a, b], [a_spec, b_spec]
    if res is not None:
        ins.append(res)
        specs.append(o_spec)
    out_shape = (m, n) if nb is None else (nb, m, n)
    return _call(body, name=name, out_shape=jax.ShapeDtypeStruct(out_shape, out_dtype),
                 grid=(1 if nb is None else nb, m // tm, n // tn, nk), in_specs=specs, out_specs=o_spec,
                 scratch=[] if nk == 1 else [pltpu.VMEM((tm, tn), F32)],
                 sem=("parallel", "parallel", "parallel", "arbitrary"))(*ins)


def rms_fwd(x, g, *, name, gate=None, out_dtype=F32):
    s, w = x.shape
    bs = _tile(s, (ROW_BLK,))

    def body(*refs):
        if gate is None:
            x_ref, g_ref, o_ref = refs
            u = x_ref[...]
        else:
            x_ref, z_ref, g_ref, o_ref = refs
            z = z_ref[...]
            u = x_ref[...] * (z * _sigmoid(z))
        r = lax.rsqrt(jnp.mean(u * u, axis=1, keepdims=True) + EPS)
        o_ref[...] = (u * r * g_ref[...]).astype(out_dtype)

    row = pl.BlockSpec((bs, w), lambda i: (i, 0))
    vec = pl.BlockSpec((1, w), lambda i: (0, 0))
    ins = [x] + ([] if gate is None else [gate]) + [g.reshape(1, w)]
    specs = [row] + ([] if gate is None else [row]) + [vec]
    return _call(body, name=name, out_shape=jax.ShapeDtypeStruct((s, w), out_dtype), grid=(s // bs,),
                 in_specs=specs, out_specs=row, sem=("parallel",))(*ins)


def rms_bwd(x, g, dy, *, name, gate=None, add=None):
    s, w = x.shape
    bs = _tile(s, (ROW_BLK,))

    def body(*refs):
        refs = list(refs)
        x_ref = refs.pop(0)
        z_ref = refs.pop(0) if gate is not None else None
        g_ref = refs.pop(0)
        dy_ref = refs.pop(0)
        add_ref = refs.pop(0) if add is not None else None
        dx_ref = refs.pop(0)
        dz_ref = refs.pop(0) if gate is not None else None
        dg_ref = refs.pop(0)
        i = pl.program_id(0)

        @pl.when(i == 0)
        def _():
            dg_ref[...] = jnp.zeros_like(dg_ref)

        xv = x_ref[...]
        if gate is not None:
            z = z_ref[...]
            sg = _sigmoid(z)
            act = z * sg
            u = xv * act
        else:
            u = xv
        r = lax.rsqrt(jnp.mean(u * u, axis=1, keepdims=True) + EPS)
        dy_v = dy_ref[...]
        dyg = dy_v * g_ref[...]
        du = r * dyg - u * (r * r * r * jnp.mean(dyg * u, axis=1, keepdims=True))
        dg_ref[...] += jnp.sum(dy_v * u * r, axis=0, keepdims=True)
        if gate is not None:
            dx = du * act
            dz_ref[...] = du * xv * (sg * (1.0 + z * (1.0 - sg)))
        else:
            dx = du
        if add is not None:
            dx = dx + add_ref[...]
        dx_ref[...] = dx

    row = pl.BlockSpec((bs, w), lambda i: (i, 0))
    vec = pl.BlockSpec((1, w), lambda i: (0, 0))
    ins = [x] + ([] if gate is None else [gate]) + [g.reshape(1, w), dy] + ([] if add is None else [add])
    specs = [row] + ([] if gate is None else [row]) + [vec, row] + ([] if add is None else [row])
    outs = [jax.ShapeDtypeStruct((s, w), F32)] + ([] if gate is None else [jax.ShapeDtypeStruct((s, w), F32)])
    outs.append(jax.ShapeDtypeStruct((1, w), F32))
    ospecs = [row] + ([] if gate is None else [row]) + [vec]
    return _call(body, name=name, out_shape=outs, grid=(s // bs,), in_specs=specs, out_specs=ospecs,
                 sem=("arbitrary",))(*ins)


def loss_head(y, target, *, name):
    s, w = y.shape
    bs = _tile(s, (ROW_BLK,))
    nb = s // bs

    def body(y_ref, t_ref, dy_ref, loss_ref, acc):
        i = pl.program_id(0)

        @pl.when(i == 0)
        def _():
            acc[...] = jnp.zeros_like(acc)

        e = y_ref[...] - t_ref[...]
        dy_ref[...] = e * (1.0 / w)
        acc[...] += jnp.sum(e * e, axis=0, keepdims=True)

        @pl.when(i == nb - 1)
        def _():
            loss_ref[...] = jnp.sum(acc[...], axis=1, keepdims=True) * (0.5 / w)

    row = pl.BlockSpec((bs, w), lambda i: (i, 0))
    return _call(body, name=name, out_shape=[jax.ShapeDtypeStruct((s, w), F32), jax.ShapeDtypeStruct((1, 1), F32)],
                 grid=(nb,), in_specs=[row, row], out_specs=[row, pl.BlockSpec((1, 1), lambda i: (0, 0))],
                 scratch=[pltpu.VMEM((1, w), F32)], sem=("arbitrary",))(y, target)


def _rope_tables(positions, s):
    inv_freq = 1.0 / (ROPE_THETA ** (jnp.arange(0, MLA_ROPE, 2, dtype=F32) / MLA_ROPE))
    ang = positions.reshape(s, 1).astype(F32) * inv_freq
    cos, sin = jnp.cos(ang), jnp.sin(ang)
    one, zero = jnp.ones((s, MLA_NOPE), F32), jnp.zeros((s, MLA_NOPE), F32)
    cq = jnp.tile(jnp.concatenate([one, cos, cos], axis=1), (1, MLA_HEADS))
    sq = jnp.tile(jnp.concatenate([zero, sin, sin], axis=1), (1, MLA_HEADS))
    pad1, pad0 = jnp.ones((s, LANES - MLA_ROPE), F32), jnp.zeros((s, LANES - MLA_ROPE), F32)
    ct = jnp.concatenate([cos, cos, pad1], axis=1)
    st = jnp.concatenate([sin, sin, pad0], axis=1)
    half = MLA_ROPE // 2

    def swap(width, starts):
        r = np.zeros((width, width), np.float32)
        for o in starts:
            for i in range(half):
                r[o + half + i, o + i] = -1.0
                r[o + i, o + half + i] = 1.0
        return jnp.asarray(r)

    rq = swap(MLA_HEADS * MLA_QK, [h * MLA_QK + MLA_NOPE for h in range(MLA_HEADS)])
    rt = swap(LANES, [0])
    return (cq, sq, rq), (ct, st, rt)


def rope(x, tabs, *, name, backward=False, add=None):
    cos, sin, rot = tabs
    n, s, w = x.shape
    bs = _tile(s, (ROW_BLK,))

    def body(*refs):
        if add is None:
            x_ref, c_ref, s_ref, r_ref, o_ref = refs
        else:
            x_ref, c_ref, s_ref, r_ref, a_ref, o_ref = refs
        xv = x_ref[0]
        for j in range(1, n):
            xv = xv + x_ref[j]
        if backward:
            out = xv * c_ref[...] + _nt(xv * s_ref[...], r_ref[...], HIGHEST)
        else:
            out = xv * c_ref[...] + _nn(xv, r_ref[...], HIGHEST) * s_ref[...]
        if add is not None:
            out = out + a_ref[...]
        o_ref[...] = out

    row = pl.BlockSpec((bs, w), lambda i: (i, 0))
    ins = [x, cos, sin, rot] + ([] if add is None else [add])
    specs = [pl.BlockSpec((n, bs, w), lambda i: (0, i, 0)), row, row, pl.BlockSpec((w, w), lambda i: (0, 0))]
    specs += [] if add is None else [row]
    return _call(body, name=name, out_shape=jax.ShapeDtypeStruct((s, w), F32), grid=(s // bs,), in_specs=specs,
                 out_specs=row, sem=("parallel",))(*ins)


def _tri(n, op):
    r = lax.broadcasted_iota(jnp.int32, (n, n), 0)
    c = lax.broadcasted_iota(jnp.int32, (n, n), 1)
    return r, c, op(r, c)


def _pair_split(x, first):
    zero = jnp.zeros_like(x)
    return jnp.where(first, x, zero), jnp.where(first, zero, x)


def _sb_specs(s, blk):
    npair = SB_WIDTH // LANES
    q = pl.BlockSpec((blk, LANES), lambda j, i: (i, j))
    k = pl.BlockSpec((s, LANES), lambda j, i: (0, npair + j))
    v = pl.BlockSpec((s, LANES), lambda j, i: (0, 2 * npair + j))
    full = pl.BlockSpec((s, LANES), lambda j, i: (0, j))
    return npair, q, k, v, full


def _stack_heads(x, first):
    return jnp.concatenate(_pair_split(x, first), axis=0)


def _unstack_heads(x, first, blk):
    return jnp.where(first, x[:blk], x[blk:])


def sb_fwd(qkv, *, name):
    s = qkv.shape[0]
    blk = _tile(s, (ATT_BLK,))
    scale = SB_DIM ** -0.5

    def body(q_ref, k_ref, v_ref, y_ref, t_ref):
        qi = pl.program_id(1)
        first = lax.broadcasted_iota(jnp.int32, (blk, LANES), 1) < SB_DIM
        q2 = _stack_heads((q_ref[...].astype(F32) * scale).astype(MXU_DTYPE), first)
        row, col, later_mask = _tri(blk, lambda r, c: r > c)
        u_later = later_mask.astype(MXU_DTYPE)
        valid = jnp.tile(col < row, (2, 1))

        def tile(kb, carry, masked):
            c, acc = carry
            ks = pl.multiple_of(kb * blk, blk)
            z = _nt(q2, k_ref[pl.ds(ks, blk), :])
            sp = _softplus_att(z)
            spm = jnp.where(valid, sp, 0.0) if masked else sp
            w = jnp.exp((z - sp) - _cum(spm, u_later) + c)
            if masked:
                w = jnp.where(valid, w, 0.0)
            return c - jnp.sum(spm, axis=1, keepdims=True), acc + _nn(w.astype(MXU_DTYPE), v_ref[pl.ds(ks, blk), :])

        zero = (jnp.zeros((2 * blk, 1), F32), jnp.zeros((2 * blk, LANES), F32))
        c, acc = _causal_loop(qi, tile, zero, True)
        y_ref[...] = _unstack_heads(acc, first, blk)
        t_ref[...] = _unstack_heads(c, first, blk)

    npair, qspec, kspec, vspec, _ = _sb_specs(s, blk)
    out = jax.ShapeDtypeStruct((s, SB_WIDTH), F32)
    return _call(body, name=name, out_shape=[out, out], grid=(npair, s // blk), in_specs=[qspec, kspec, vspec],
                 out_specs=[qspec, qspec], sem=("parallel", "arbitrary"))(qkv, qkv, qkv)


def sb_bwd(qkv, dy, tot, *, name):
    s = qkv.shape[0]
    blk = _tile(s, (ATT_BLK,))
    scale = SB_DIM ** -0.5

    def body(q_ref, k_ref, v_ref, dy_ref, t_ref, dq_ref, dk_ref, dv_ref):
        qi = pl.program_id(1)

        @pl.when(qi == 0)
        def _():
            dk_ref[...] = jnp.zeros_like(dk_ref)
            dv_ref[...] = jnp.zeros_like(dv_ref)

        first = lax.broadcasted_iota(jnp.int32, (blk, LANES), 1) < SB_DIM
        q2 = _stack_heads((q_ref[...].astype(F32) * scale).astype(MXU_DTYPE), first)
        dy2 = _stack_heads(dy_ref[...].astype(MXU_DTYPE), first)
        tv = jnp.concatenate([t_ref[:, 0:1], t_ref[:, SB_DIM:SB_DIM + 1]], axis=0)
        row, col, incl_mask = _tri(blk, lambda r, c: r <= c)
        u_incl = incl_mask.astype(MXU_DTYPE)
        u_excl = (row < col).astype(MXU_DTYPE)
        valid = jnp.tile(col < row, (2, 1))

        def tile(kb, carry, masked):
            p, gc, dq = carry
            ks = pl.multiple_of(kb * blk, blk)
            kv = k_ref[pl.ds(ks, blk), :]
            z = _nt(q2, kv)
            sp = _softplus_att(z)
            spm = jnp.where(valid, sp, 0.0) if masked else sp
            w = jnp.exp((z - sp) + (tv + (_cum(spm, u_incl) + p)))
            if masked:
                w = jnp.where(valid, w, 0.0)
            g = w * _nt(dy2, v_ref[pl.ds(ks, blk), :])
            gex = _nn(g.astype(MXU_DTYPE), u_excl) + gc
            keep = jnp.exp(-spm)
            dz = g * keep - (1.0 - keep) * gex
            if masked:
                dz = jnp.where(valid, dz, 0.0)
            dzb = dz.astype(MXU_DTYPE)
            dk_ref[pl.ds(ks, blk), :] += _tn(dzb, q2)
            dv_ref[pl.ds(ks, blk), :] += _tn(w.astype(MXU_DTYPE), dy2)
            return (p + jnp.sum(spm, axis=1, keepdims=True), gc + jnp.sum(g, axis=1, keepdims=True), dq + _nn(dzb, kv))

        zero = jnp.zeros((2 * blk, 1), F32)
        _, _, dq = _causal_loop(qi, tile, (zero, zero, jnp.zeros((2 * blk, LANES), F32)), False)
        dq_ref[...] = _unstack_heads(dq, first, blk) * scale

    npair, qspec, kspec, vspec, full = _sb_specs(s, blk)
    out = jax.ShapeDtypeStruct((s, SB_WIDTH), F32)
    return _call(body, name=name, out_shape=[out, out, out], grid=(npair, s // blk),
                 in_specs=[qspec, kspec, vspec, qspec, qspec], out_specs=[qspec, full, full],
                 sem=("parallel", "arbitrary"))(qkv, qkv, qkv, dy, tot)


ATT_PAIR = 2


def _mla_specs(s, blk, dk, dv):
    q = pl.BlockSpec((ATT_PAIR, blk, dk), lambda hp, i: (hp, i, 0))
    k = pl.BlockSpec((ATT_PAIR, s, dk), lambda hp, i: (hp, 0, 0))
    v = pl.BlockSpec((ATT_PAIR, s, dv), lambda hp, i: (hp, 0, 0))
    y = pl.BlockSpec((ATT_PAIR, blk, dv), lambda hp, i: (hp, i, 0))
    lse = pl.BlockSpec((ATT_PAIR, blk, LANES), lambda hp, i: (hp, i, 0))
    return q, k, v, y, lse


def mla_fwd(q, k, v, *, name):
    h, s, dk = q.shape
    dv = v.shape[-1]
    blk = _tile(s, (ATT_BLK,))
    scale = dk ** -0.5

    def body(q_ref, k_ref, v_ref, y_ref, l_ref):
        qi = pl.program_id(1)
        row, col, valid = _tri(blk, lambda r, c: c <= r)

        def tile(kb, carry, masked):
            ks = pl.multiple_of(kb * blk, blk)
            out = []
            for hh in range(ATT_PAIR):
                m, l, acc = carry[hh]
                sc = _nt(q_ref[hh], k_ref[hh, pl.ds(ks, blk), :]) * scale
                if masked:
                    sc = jnp.where(valid, sc, -1e30)
                m2 = jnp.maximum(m, jnp.max(sc, axis=1, keepdims=True))
                p = jnp.exp(sc - m2)
                a = jnp.exp(m - m2)
                out.append((m2, a * l + jnp.sum(p, axis=1, keepdims=True),
                            a * acc + _nn(p.astype(MXU_DTYPE), v_ref[hh, pl.ds(ks, blk), :])))
            return tuple(out)

        init = (jnp.full((blk, 1), -1e30, F32), jnp.zeros((blk, 1), F32), jnp.zeros((blk, dv), F32))
        for hh, (m, l, acc) in enumerate(_causal_loop(qi, tile, (init,) * ATT_PAIR, False)):
            y_ref[hh] = acc / l
            l_ref[hh] = jnp.broadcast_to(m + jnp.log(l), (blk, LANES))

    qspec, kspec, vspec, yspec, lspec = _mla_specs(s, blk, dk, dv)
    return _call(body, name=name,
                 out_shape=[jax.ShapeDtypeStruct((h, s, dv), F32), jax.ShapeDtypeStruct((h, s, LANES), F32)],
                 grid=(h // ATT_PAIR, s // blk), in_specs=[qspec, kspec, vspec], out_specs=[yspec, lspec],
                 sem=("parallel", "arbitrary"))(q, k, v)


def mla_bwd(q, k, v, y, dy, lse, *, name):
    h, s, dk = q.shape
    dv = v.shape[-1]
    blk = _tile(s, (ATT_BLK,))
    scale = dk ** -0.5

    def body(q_ref, k_ref, v_ref, y_ref, dy_ref, l_ref, dq_ref, dk_ref, dv_ref):
        qi = pl.program_id(1)

        @pl.when(qi == 0)
        def _():
            dk_ref[...] = jnp.zeros_like(dk_ref)
            dv_ref[...] = jnp.zeros_like(dv_ref)

        as_row = lambda col: jnp.transpose(jnp.broadcast_to(col, (blk, LANES)))[0:1, :]
        dyv = [dy_ref[hh].astype(MXU_DTYPE) for hh in range(ATT_PAIR)]
        delta = [as_row(jnp.sum(dy_ref[hh] * y_ref[hh], axis=1, keepdims=True)) for hh in range(ATT_PAIR)]
        lv = [as_row(l_ref[hh, :, 0:1]) for hh in range(ATT_PAIR)]
        row, col, valid = _tri(blk, lambda r, c: r <= c)

        def tile(kb, dqs, masked):
            ks = pl.multiple_of(kb * blk, blk)
            out = []
            for hh in range(ATT_PAIR):
                qv = q_ref[hh]
                kv = k_ref[hh, pl.ds(ks, blk), :]
                vv = v_ref[hh, pl.ds(ks, blk), :]
                p = jnp.exp(_nt(kv, qv) * scale - lv[hh])
                if masked:
                    p = jnp.where(valid, p, 0.0)
                ds = (p * (_nt(vv, dyv[hh]) - delta[hh])).astype(MXU_DTYPE)
                dk_ref[hh, pl.ds(ks, blk), :] += _nn(ds, qv) * scale
                dv_ref[hh, pl.ds(ks, blk), :] += _nn(p.astype(MXU_DTYPE), dyv[hh])
                out.append(dqs[hh] + _tn(ds, kv))
            return tuple(out)

        for hh, dq in enumerate(_causal_loop(qi, tile, (jnp.zeros((blk, dk), F32),) * ATT_PAIR, False)):
            dq_ref[hh] = dq * scale

    qspec, kspec, vspec, yspec, lspec = _mla_specs(s, blk, dk, dv)
    return _call(body, name=name,
                 out_shape=[jax.ShapeDtypeStruct((h, s, dk), F32), jax.ShapeDtypeStruct((h, s, dk), F32),
                            jax.ShapeDtypeStruct((h, s, dv), F32)],
                 grid=(h // ATT_PAIR, s // blk), in_specs=[qspec, kspec, vspec, yspec, yspec, lspec],
                 out_specs=[qspec, kspec, vspec], sem=("parallel", "arbitrary"))(q, k, v, y, dy, lse)


HALO = 8
CONV_CHUNK = 16


def _conv_tiles(x):
    s, c = x.shape[-2:]
    return s, c, _tile(s, (ROW_BLK,)), _tile(c, (CONV_COLS,))


def _conv_specs(bs, cw, lead=()):
    zero = (0,) * len(lead)
    blk = pl.BlockSpec(lead + (None, bs, cw), lambda p, j, i: zero + (p, i, j))
    halo = pl.BlockSpec(lead + (None, HALO, cw), lambda p, j, i: zero + (p, jnp.maximum(i * (bs // HALO) - 1, 0), j))
    w = lambda kk: pl.BlockSpec(lead + (None, kk, cw), lambda p, j, i: zero + (p, 0, j))
    return blk, halo, w


def _stage(scr, x_ref, halo_ref, first):
    scr[0:HALO, :] = jnp.where(first, 0.0, halo_ref[...])
    scr[HALO:, :] = x_ref[...]


def _conv_taps(scr, kk, r0):
    return [scr[pl.ds(HALO - (kk - 1) + k + r0, CONV_CHUNK), :] for k in range(kk)]


def _conv_sum(taps, w_ref, b_ref):
    u = b_ref[...] + taps[0] * w_ref[0:1, :]
    for k in range(1, len(taps)):
        u = u + taps[k] * w_ref[k:k + 1, :]
    return u


def _fold(x):
    out = x[0:8]
    for r in range(8, CONV_CHUNK, 8):
        out = out + x[r:r + 8]
    return out


class _TapSums:
    def __init__(self, kk, cw):
        self.w = [jnp.zeros((8, cw), F32) for _ in range(kk)]
        self.b = jnp.zeros((8, cw), F32)

    def add(self, du, taps):
        self.w = [a + _fold(du * t) for a, t in zip(self.w, taps)]
        self.b = self.b + _fold(du)

    def flush(self, dw_ref, db_ref):
        for k, a in enumerate(self.w):
            dw_ref[k:k + 1, :] += jnp.sum(a, axis=0, keepdims=True)
        db_ref[...] += jnp.sum(self.b, axis=0, keepdims=True)


def _silu_grad(u):
    sg = _sigmoid(u)
    return sg * (1.0 + u * (1.0 - sg))


def conv_silu_fwd(x, w, b, *, name):
    s, c, bs, cw = _conv_tiles(x)
    kk = w.shape[1]

    def body(x_ref, h_ref, w_ref, b_ref, o_ref, scr):
        _stage(scr, x_ref, h_ref, pl.program_id(2) == 0)
        for r0 in range(0, bs, CONV_CHUNK):
            u = _conv_sum(_conv_taps(scr, kk, r0), w_ref, b_ref)
            o_ref[pl.ds(r0, CONV_CHUNK), :] = u * _sigmoid(u)

    blk, halo, wspec = _conv_specs(bs, cw)
    return _call(body, name=name, out_shape=jax.ShapeDtypeStruct(x.shape, F32), grid=(x.shape[0], c // cw, s // bs),
                 in_specs=[blk, halo, wspec(kk), wspec(1)], out_specs=blk, scratch=[pltpu.VMEM((bs + HALO, cw), F32)],
                 sem=("parallel", "parallel", "arbitrary"))(x, x, w, b)


def conv_silu_bwd(x, dy, w, b, *, name):
    s, c, bs, cw = _conv_tiles(x)
    kk = w.shape[1]

    def body(x_ref, h_ref, w_ref, b_ref, dy_ref, du_ref, dw_ref, db_ref, scr):
        i = pl.program_id(2)

        @pl.when(i == 0)
        def _():
            dw_ref[...] = jnp.zeros_like(dw_ref)
            db_ref[...] = jnp.zeros_like(db_ref)

        _stage(scr, x_ref, h_ref, i == 0)
        sums = _TapSums(kk, cw)
        for r0 in range(0, bs, CONV_CHUNK):
            taps = _conv_taps(scr, kk, r0)
            du = dy_ref[pl.ds(r0, CONV_CHUNK), :] * _silu_grad(_conv_sum(taps, w_ref, b_ref))
            du_ref[pl.ds(r0, CONV_CHUNK), :] = du
            sums.add(du, taps)
        sums.flush(dw_ref, db_ref)

    blk, halo, wspec = _conv_specs(bs, cw)
    return _call(body, name=name,
                 out_shape=[jax.ShapeDtypeStruct(x.shape, F32), jax.ShapeDtypeStruct(w.shape, F32),
                            jax.ShapeDtypeStruct(b.shape, F32)],
                 grid=(x.shape[0], c // cw, s // bs), in_specs=[blk, halo, wspec(kk), wspec(1), blk],
                 out_specs=[blk, wspec(kk), wspec(1)], scratch=[pltpu.VMEM((bs + HALO, cw), F32)],
                 sem=("parallel", "parallel", "arbitrary"))(x, x, w, b, dy)


def _glu_view(a):
    return a.reshape((2, a.shape[0] // 2) + a.shape[1:])


def conv_glu_fwd(x, w, b, *, name):
    s, c, bs, cw = _conv_tiles(x)
    kk = w.shape[1]
    half = x.shape[0] // 2

    def body(x_ref, h_ref, w_ref, b_ref, o_ref, gscr, vscr):
        first = pl.program_id(2) == 0
        _stage(gscr, x_ref.at[0], h_ref.at[0], first)
        _stage(vscr, x_ref.at[1], h_ref.at[1], first)
        for r0 in range(0, bs, CONV_CHUNK):
            gate = _conv_sum(_conv_taps(gscr, kk, r0), w_ref.at[0], b_ref.at[0])
            val = _conv_sum(_conv_taps(vscr, kk, r0), w_ref.at[1], b_ref.at[1])
            o_ref[pl.ds(r0, CONV_CHUNK), :] = (gate * _sigmoid(gate) * val).astype(o_ref.dtype)

    blk, halo, wspec = _conv_specs(bs, cw, lead=(2,))
    out, _, _ = _conv_specs(bs, cw)
    xv = _glu_view(x)
    return _call(body, name=name, out_shape=jax.ShapeDtypeStruct((half, s, c), MXU_DTYPE), grid=(half, c // cw, s // bs),
                 in_specs=[blk, halo, wspec(kk), wspec(1)], out_specs=out, scratch=[pltpu.VMEM((bs + HALO, cw), F32)] * 2,
                 sem=("parallel", "parallel", "arbitrary"))(xv, xv, _glu_view(w), _glu_view(b))


def conv_glu_bwd(x, da, w, b, *, name):
    s, c, bs, cw = _conv_tiles(x)
    kk = w.shape[1]
    half = x.shape[0] // 2

    def body(x_ref, h_ref, w_ref, b_ref, da_ref, du_ref, dw_ref, db_ref, gscr, vscr):
        i = pl.program_id(2)

        @pl.when(i == 0)
        def _():
            dw_ref[...] = jnp.zeros_like(dw_ref)
            db_ref[...] = jnp.zeros_like(db_ref)

        _stage(gscr, x_ref.at[0], h_ref.at[0], i == 0)
        _stage(vscr, x_ref.at[1], h_ref.at[1], i == 0)
        gsums, vsums = _TapSums(kk, cw), _TapSums(kk, cw)
        for r0 in range(0, bs, CONV_CHUNK):
            gtaps, vtaps = _conv_taps(gscr, kk, r0), _conv_taps(vscr, kk, r0)
            gate = _conv_sum(gtaps, w_ref.at[0], b_ref.at[0])
            val = _conv_sum(vtaps, w_ref.at[1], b_ref.at[1])
            dav = da_ref[pl.ds(r0, CONV_CHUNK), :]
            dgate = dav * val * _silu_grad(gate)
            dval = dav * gate * _sigmoid(gate)
            du_ref[0, pl.ds(r0, CONV_CHUNK), :] = dgate
            du_ref[1, pl.ds(r0, CONV_CHUNK), :] = dval
            gsums.add(dgate, gtaps)
            vsums.add(dval, vtaps)
        gsums.flush(dw_ref.at[0], db_ref.at[0])
        vsums.flush(dw_ref.at[1], db_ref.at[1])

    blk, halo, wspec = _conv_specs(bs, cw, lead=(2,))
    daspec, _, _ = _conv_specs(bs, cw)
    xv, wv, bv = _glu_view(x), _glu_view(w), _glu_view(b)
    du, dw, db = _call(body, name=name,
                       out_shape=[jax.ShapeDtypeStruct(xv.shape, F32), jax.ShapeDtypeStruct(wv.shape, F32),
                                  jax.ShapeDtypeStruct(bv.shape, F32)],
                       grid=(half, c // cw, s // bs), in_specs=[blk, halo, wspec(kk), wspec(1), daspec],
                       out_specs=[blk, wspec(kk), wspec(1)], scratch=[pltpu.VMEM((bs + HALO, cw), F32)] * 2,
                       sem=("parallel", "parallel", "arbitrary"))(xv, xv, wv, bv, da)
    return du.reshape(x.shape), dw.reshape(w.shape), db.reshape(b.shape)


def conv_t(du, w, *, name):
    s, c, bs, cw = _conv_tiles(du)
    kk = w.shape[1]
    nb = s // bs

    def body(d_ref, h_ref, w_ref, o_ref, scr):
        last = pl.program_id(2) == nb - 1
        scr[0:bs, :] = d_ref[...]
        scr[bs:, :] = jnp.where(last, 0.0, h_ref[...])
        for r0 in range(0, bs, CONV_CHUNK):
            acc = scr[pl.ds(r0 + kk - 1, CONV_CHUNK), :] * w_ref[0:1, :]
            for k in range(1, kk):
                acc = acc + scr[pl.ds(r0 + kk - 1 - k, CONV_CHUNK), :] * w_ref[k:k + 1, :]
            o_ref[pl.ds(r0, CONV_CHUNK), :] = acc

    blk, _, wspec = _conv_specs(bs, cw)
    halo = pl.BlockSpec((None, HALO, cw), lambda q, j, i: (q, jnp.minimum((i + 1) * (bs // HALO), s // HALO - 1), j))
    return _call(body, name=name, out_shape=jax.ShapeDtypeStruct(du.shape, F32), grid=(du.shape[0], c // cw, nb),
                 in_specs=[blk, halo, wspec(kk)], out_specs=blk, scratch=[pltpu.VMEM((bs + HALO, cw), F32)],
                 sem=("parallel", "parallel", "arbitrary"))(du, du, w)


def _ssd_common(xbc_ref, tail_ref, dtrt_ref, bias_ref, biast_ref, alog_ref, alogt_ref):
    L = SSM_CHUNK
    raw = tail_ref[...] + bias_ref[...]
    dt = _softplus(raw)
    dtt = _softplus(dtrt_ref[...] + biast_ref[...])
    a = -jnp.exp(alog_ref[...])
    at = -jnp.exp(alogt_ref[...])
    row, col, lower = _tri(L, lambda r, c: r >= c)
    tril = lower.astype(F32)
    cs = _nn(tril, dt * a, HIGHEST)
    cst = _nt(dtt * at, tril, HIGHEST)
    bm = [xbc_ref[:, SSM_INNER + g * SSM_N: SSM_INNER + (g + 1) * SSM_N] for g in range(SSM_GROUPS)]
    off = SSM_INNER + SSM_GROUPS * SSM_N
    cm = [xbc_ref[:, off + g * SSM_N: off + (g + 1) * SSM_N] for g in range(SSM_GROUPS)]
    cb = [_bnt(cm[g], bm[g]) for g in range(SSM_GROUPS)]
    return raw, dt, a, lower, tril, cs, cst, bm, cm, cb


def _ssd_head(hh, xbc_ref, dt, cs, cst, lower):
    L = SSM_CHUNK
    ln = DT_LANE + hh
    x = xbc_ref[:, hh * SSM_P:(hh + 1) * SSM_P]
    dtc = dt[:, ln:ln + 1]
    csc = cs[:, ln:ln + 1]
    csr = cst[hh:hh + 1, :]
    decay = jnp.exp(jnp.where(lower, csc - csr, -1e30))
    last = cs[L - 1:L, ln:ln + 1]
    return x, dtc, csc, decay, jnp.exp(csc), jnp.exp(last - csc), jnp.exp(last)


def _ssd_inputs(tail, dt_bias, a_log, d_skip):
    H = SSM_HEADS
    lanes = lambda vec: jnp.pad(vec.reshape(1, H), ((0, 0), (DT_LANE, LANES - DT_LANE - H)))
    return (tail, tail[:, DT_LANE:DT_LANE + H].T, lanes(dt_bias), dt_bias.reshape(H, 1), lanes(a_log),
            a_log.reshape(H, 1), lanes(d_skip))


def ssd_fwd(xbc, tail, dt_bias, a_log, d_skip, *, name):
    s = xbc.shape[0]
    L, H, P, N = SSM_CHUNK, SSM_HEADS, SSM_P, SSM_N
    nc = s // L

    def body(xbc_ref, tail_ref, dtrt_ref, bias_ref, biast_ref, alog_ref, alogt_ref, d_ref, y_ref, hp_ref, state):
        @pl.when(pl.program_id(0) == 0)
        def _():
            state[...] = jnp.zeros_like(state)

        raw, dt, a, lower, tril, cs, cst, bm, cm, cb = _ssd_common(
            xbc_ref, tail_ref, dtrt_ref, bias_ref, biast_ref, alog_ref, alogt_ref)
        for hh in range(H):
            g = hh // (H // SSM_GROUPS)
            x, dtc, csc, decay, e, tau, gamma = _ssd_head(hh, xbc_ref, dt, cs, cst, lower)
            xdt = x * dtc
            hprev = state[hh]
            hp_ref[hh] = hprev
            skip = d_ref[:, DT_LANE + hh:DT_LANE + hh + 1]
            y = _bnn(cb[g] * decay, xdt) + _bnn(cm[g], hprev) * e + x * skip
            y_ref[:, hh * P:(hh + 1) * P] = y
            state[hh] = hprev * gamma + _btn(bm[g] * tau, xdt)

    row = lambda w: pl.BlockSpec((L, w), lambda c: (c, 0))
    small = lambda shp: pl.BlockSpec(shp, lambda c: (0, 0))
    return _call(body, name=name,
                 out_shape=[jax.ShapeDtypeStruct((s, SSM_INNER), F32), jax.ShapeDtypeStruct((nc, H, N, P), F32)],
                 grid=(nc,),
                 in_specs=[row(SSM_CONV_DIM), row(LANES), pl.BlockSpec((H, L), lambda c: (0, c)), small((1, LANES)),
                           small((H, 1)), small((1, LANES)), small((H, 1)), small((1, LANES))],
                 out_specs=[row(SSM_INNER), pl.BlockSpec((None, H, N, P), lambda c: (c, 0, 0, 0))],
                 scratch=[pltpu.VMEM((H, N, P), F32)], sem=("arbitrary",))(xbc, *_ssd_inputs(tail, dt_bias, a_log, d_skip))


def ssd_bwd(xbc, tail, dt_bias, a_log, d_skip, hprev_all, dy, *, name):
    s = xbc.shape[0]
    L, H, P, N = SSM_CHUNK, SSM_HEADS, SSM_P, SSM_N
    nc = s // L
    hg = H // SSM_GROUPS

    def body(xbc_ref, tail_ref, dtrt_ref, bias_ref, biast_ref, alog_ref, alogt_ref, d_ref, hp_ref, dy_ref,
             dxbc_ref, ddt_ref, dbias_ref, dalog_ref, dd_ref, dstate):
        @pl.when(pl.program_id(0) == 0)
        def _():
            dstate[...] = jnp.zeros_like(dstate)
            dbias_ref[...] = jnp.zeros_like(dbias_ref)
            dalog_ref[...] = jnp.zeros_like(dalog_ref)
            dd_ref[...] = jnp.zeros_like(dd_ref)

        raw, dt, a, lower, tril, cs, cst, bm, cm, cb = _ssd_common(
            xbc_ref, tail_ref, dtrt_ref, bias_ref, biast_ref, alog_ref, alogt_ref)
        lane = lax.broadcasted_iota(jnp.int32, (L, LANES), 1)
        lane1 = lax.broadcasted_iota(jnp.int32, (1, LANES), 1)
        rowi = lax.broadcasted_iota(jnp.int32, (L, 1), 0)
        ones = jnp.ones((L, LANES), F32)
        dcs_all = jnp.zeros((L, LANES), F32)
        ddt_x = jnp.zeros((L, LANES), F32)
        dd_row = jnp.zeros((1, LANES), F32)
        dbm = [jnp.zeros((L, N), F32) for _ in range(SSM_GROUPS)]
        dcm = [jnp.zeros((L, N), F32) for _ in range(SSM_GROUPS)]
        dcb = [jnp.zeros((L, L), F32) for _ in range(SSM_GROUPS)]
        for hh in range(H):
            g = hh // hg
            ln = DT_LANE + hh
            x, dtc, csc, decay, e, tau, gamma = _ssd_head(hh, xbc_ref, dt, cs, cst, lower)
            xdt = x * dtc
            hprev = hp_ref[hh]
            dhn = dstate[hh]
            dyh = dy_ref[:, hh * P:(hh + 1) * P]
            m = cb[g] * decay
            dxdt = _btn(m, dyh) + _bnn(bm[g] * tau, dhn)
            dm = jnp.where(lower, _bnt(dyh, xdt), 0.0)
            dcb[g] = dcb[g] + dm * decay
            dseg = dm * m
            dcs = jnp.sum(dseg, axis=1, keepdims=True) - _tn(dseg, ones, HIGHEST)[:, 0:1]
            edy = e * dyh
            dcm[g] = dcm[g] + _bnt(edy, hprev)
            dcs = dcs + e * jnp.sum(dyh * _bnn(cm[g], hprev), axis=1, keepdims=True)
            xdh = _bnt(xdt, dhn)
            dbm[g] = dbm[g] + tau * xdh
            dtau_tau = jnp.sum(bm[g] * xdh, axis=1, keepdims=True) * tau
            dlast = jnp.sum(dtau_tau, axis=0, keepdims=True) + gamma * jnp.sum(dhn * hprev, keepdims=True)
            dcs = dcs - dtau_tau + jnp.where(rowi == L - 1, dlast, 0.0)
            dstate[hh] = gamma * dhn + _btn(cm[g], edy)
            dcs_all = jnp.where(lane == ln, dcs, dcs_all)
            ddt_x = jnp.where(lane == ln, jnp.sum(dxdt * x, axis=1, keepdims=True), ddt_x)
            dxbc_ref[:, hh * P:(hh + 1) * P] = dxdt * dtc + d_ref[:, ln:ln + 1] * dyh
            dd_row = jnp.where(lane1 == ln, jnp.sum(dyh * x, keepdims=True), dd_row)
        off = SSM_INNER + SSM_GROUPS * SSM_N
        for g in range(SSM_GROUPS):
            dxbc_ref[:, SSM_INNER + g * N: SSM_INNER + (g + 1) * N] = dbm[g] + _btn(dcb[g], cm[g])
            dxbc_ref[:, off + g * N: off + (g + 1) * N] = dcm[g] + _bnn(dcb[g], bm[g])
        dda = _tn(tril, dcs_all, HIGHEST)
        head_lane = (lane >= DT_LANE) & (lane < DT_LANE + H)
        draw = jnp.where(head_lane, (dda * a + ddt_x) * _sigmoid(raw), 0.0)
        ddt_ref[...] = draw
        dbias_ref[...] += jnp.sum(draw, axis=0, keepdims=True)
        dalog_ref[...] += jnp.sum(jnp.where(head_lane, dda * dt, 0.0), axis=0, keepdims=True) * a
        dd_ref[...] += dd_row

    rev = lambda c: nc - 1 - c
    row = lambda w: pl.BlockSpec((L, w), lambda c: (rev(c), 0))
    small = lambda shp: pl.BlockSpec(shp, lambda c: (0, 0))
    acc = pl.BlockSpec((1, LANES), lambda c: (0, 0))
    vec = jax.ShapeDtypeStruct((1, LANES), F32)
    return _call(body, name=name,
                 out_shape=[jax.ShapeDtypeStruct((s, SSM_CONV_DIM), F32), jax.ShapeDtypeStruct((s, LANES), F32), vec, vec, vec],
                 grid=(nc,),
                 in_specs=[row(SSM_CONV_DIM), row(LANES), pl.BlockSpec((H, L), lambda c: (0, rev(c))), small((1, LANES)),
                           small((H, 1)), small((1, LANES)), small((H, 1)), small((1, LANES)),
                           pl.BlockSpec((None, H, N, P), lambda c: (rev(c), 0, 0, 0)), row(SSM_INNER)],
                 out_specs=[row(SSM_CONV_DIM), row(LANES), acc, acc, acc],
                 scratch=[pltpu.VMEM((H, N, P), F32)], sem=("arbitrary",))(
        xbc, *_ssd_inputs(tail, dt_bias, a_log, d_skip), hprev_all, dy)


def _heads(x2d, n, d):
    s = x2d.shape[0]
    return x2d.reshape(s, n, d).transpose(1, 0, 2)


def _unheads(x3d):
    n, s, d = x3d.shape
    return x3d.transpose(1, 0, 2).reshape(s, n * d)


def layer_fwd(h, p, tabs, li):
    s = h.shape[0]
    tabq, tabt = tabs
    nm = lambda t: f"L{li}_{t}"
    r = {'h': h}
    hn = rms_fwd(h, p['mix_norm'], name=nm('mixnorm'), out_dtype=MXU_DTYPE)
    proj = mm(hn, p['w_in'], name=nm('proj'))
    r.update(hn=hn, proj=proj)
    qkv = proj[:, :3 * SB_WIDTH].astype(MXU_DTYPE)
    ya, tot = sb_fwd(qkv, name=nm('sb_fwd'))
    yan = rms_fwd(ya, p['sb_out_norm'], name=nm('sbnorm'), out_dtype=MXU_DTYPE)
    r.update(qkv=qkv, ya=ya, tot=tot)
    z = proj[:, 768:1280]
    xbc = proj[None, :, 1280:2048]
    tail = proj[:, TAIL:TAIL + LANES]
    xbc_act = conv_silu_fwd(xbc, p['ssm_conv_w'], p['ssm_conv_b'], name=nm('ssmconv'))[0]
    y_ssm, hprev = ssd_fwd(xbc_act, tail, p['ssm_dt_bias'], p['ssm_a_log'], p['ssm_d'], name=nm('ssd_fwd'))
    ybn = rms_fwd(y_ssm, p['ssm_out_norm'], name=nm('ssmnorm'), gate=z, out_dtype=MXU_DTYPE)
    r.update(z=z, xbc=xbc, tail=tail, xbc_act=xbc_act, y_ssm=y_ssm, hprev=hprev)
    cq = proj[:, 2048:2304]
    ckv = proj[:, 2304:2432]
    qn = rms_fwd(cq, p['mla_q_norm'], name=nm('qnorm'), out_dtype=MXU_DTYPE)
    q_r = rope(mm(qn, p['mla_w_uq'], name=nm('uq'))[None], tabq, name=nm('ropeq'))
    kvn = rms_fwd(ckv, p['mla_kv_norm'], name=nm('kvnorm'), out_dtype=MXU_DTYPE)
    kv = mm(kvn, p['mla_w_ukv'], name=nm('ukv'))
    k_pe = rope(tail[None], tabt, name=nm('ropek'))[:, :MLA_ROPE]
    qh = _heads(q_r, MLA_HEADS, MLA_QK).astype(MXU_DTYPE)
    kvh = _heads(kv, MLA_HEADS, MLA_NOPE + MLA_V)
    kh = jnp.concatenate([kvh[..., :MLA_NOPE], jnp.broadcast_to(k_pe[None], (MLA_HEADS, s, MLA_ROPE))],
                         axis=-1).astype(MXU_DTYPE)
    vh = kvh[..., MLA_NOPE:].astype(MXU_DTYPE)
    yc_h, lse = mla_fwd(qh, kh, vh, name=nm('mla_fwd'))
    yc = _unheads(yc_h)
    ycn = rms_fwd(yc, p['mla_out_norm'], name=nm('mlanorm'), out_dtype=MXU_DTYPE)
    r.update(cq=cq, ckv=ckv, qn=qn, kvn=kvn, qh=qh, kh=kh, vh=vh, yc_h=yc_h, yc=yc, lse=lse)
    ycat = jnp.concatenate([yan, ybn, ycn], axis=1)
    h1 = mm(ycat, p['w_out'], name=nm('outproj'), res=h)
    hn2 = rms_fwd(h1, p['ffn_norm'], name=nm('ffnnorm'), out_dtype=MXU_DTYPE)
    up = mm(hn2, p['ffn_w_up'], name=nm('up'), bb='o')
    act = conv_glu_fwd(up, p['ffn_conv_w'], p['ffn_conv_b'], name=nm('glu'))
    h2 = mm(act, p['ffn_w_down'], name=nm('down'), ab='k', bb='k', res=h1)
    r.update(ycat=ycat, h1=h1, hn2=hn2, up=up, act=act)
    return h2, r


def layer_bwd(dh2, p, r, tabs, li):
    s = dh2.shape[0]
    tabq, tabt = tabs
    nm = lambda t: f"L{li}_{t}"
    g = {}
    dact = mm(dh2, p['ffn_w_down'], name=nm('d_down_x'), tb=True, bb='o')
    g['ffn_w_down'] = mm(r['act'], dh2, name=nm('d_down_w'), out_dtype=WIRE_DTYPE, ta=True, ab='o')
    du, g['ffn_conv_w'], g['ffn_conv_b'] = conv_glu_bwd(r['up'], dact, p['ffn_conv_w'], p['ffn_conv_b'], name=nm('d_glu'))
    dup = conv_t(du, p['ffn_conv_w'], name=nm('d_ffnconv'))
    g['ffn_w_up'] = mm(r['hn2'], dup, name=nm('d_up_w'), out_dtype=WIRE_DTYPE, ta=True, bb='o')
    dhn2 = mm(dup, p['ffn_w_up'], name=nm('d_up_x'), tb=True, ab='k', bb='k')
    dh1, dg = rms_bwd(r['h1'], p['ffn_norm'], dhn2, name=nm('d_ffnnorm'), add=dh2)
    g['ffn_norm'] = dg[0]
    dycat = mm(dh1, p['w_out'], name=nm('d_out_x'), tb=True)
    g['w_out'] = mm(r['ycat'], dh1, name=nm('d_out_w'), out_dtype=WIRE_DTYPE, ta=True)
    dya, dg = rms_bwd(r['ya'], p['sb_out_norm'], dycat[:, :256], name=nm('d_sbnorm'))
    g['sb_out_norm'] = dg[0]
    dq, dk, dv = sb_bwd(r['qkv'], dya, r['tot'], name=nm('sb_bwd'))
    dyssm, dz, dg = rms_bwd(r['y_ssm'], p['ssm_out_norm'], dycat[:, 256:768], name=nm('d_ssmnorm'), gate=r['z'])
    g['ssm_out_norm'] = dg[0]
    dxbc_act, ddt_tail, dbias, dalog, dd = ssd_bwd(r['xbc_act'], r['tail'], p['ssm_dt_bias'], p['ssm_a_log'],
                                                   p['ssm_d'], r['hprev'], dyssm, name=nm('ssd_bwd'))
    hl = slice(DT_LANE, DT_LANE + SSM_HEADS)
    g['ssm_dt_bias'], g['ssm_a_log'], g['ssm_d'] = dbias[0, hl], dalog[0, hl], dd[0, hl]
    dxbc_u, g['ssm_conv_w'], g['ssm_conv_b'] = conv_silu_bwd(r['xbc'], dxbc_act[None], p['ssm_conv_w'], p['ssm_conv_b'],
                                                             name=nm('d_ssmact'))
    dxbc = conv_t(dxbc_u, p['ssm_conv_w'], name=nm('d_ssmconv'))[0]
    dyc, dg = rms_bwd(r['yc'], p['mla_out_norm'], dycat[:, 768:], name=nm('d_mlanorm'))
    g['mla_out_norm'] = dg[0]
    dqh, dkh, dvh = mla_bwd(r['qh'], r['kh'], r['vh'], r['yc_h'], _heads(dyc, MLA_HEADS, MLA_V), r['lse'], name=nm('mla_bwd'))
    dq_c = rope(_unheads(dqh)[None], tabq, name=nm('d_ropeq'), backward=True)
    g['mla_w_uq'] = mm(r['qn'], dq_c, name=nm('d_uq_w'), out_dtype=WIRE_DTYPE, ta=True)
    dcq, dg = rms_bwd(r['cq'], p['mla_q_norm'], mm(dq_c, p['mla_w_uq'], name=nm('d_uq_x'), tb=True), name=nm('d_qnorm'))
    g['mla_q_norm'] = dg[0]
    dkv = _unheads(jnp.concatenate([dkh[..., :MLA_NOPE], dvh], axis=-1))
    g['mla_w_ukv'] = mm(r['kvn'], dkv, name=nm('d_ukv_w'), out_dtype=WIRE_DTYPE, ta=True)
    dckv, dg = rms_bwd(r['ckv'], p['mla_kv_norm'], mm(dkv, p['mla_w_ukv'], name=nm('d_ukv_x'), tb=True), name=nm('d_kvnorm'))
    g['mla_kv_norm'] = dg[0]
    dkpe = jnp.pad(dkh[..., MLA_NOPE:], ((0, 0), (0, 0), (0, LANES - MLA_ROPE)))
    dtail = rope(dkpe, tabt, name=nm('d_ropek'), backward=True, add=ddt_tail)
    dproj = jnp.concatenate([dq, dk, dv, dz, dxbc, dcq, dckv, dtail], axis=1)
    g['w_in'] = mm(r['hn'], dproj, name=nm('d_proj_w'), out_dtype=WIRE_DTYPE, ta=True)
    dhn = mm(dproj, p['w_in'], name=nm('d_proj_x'), tb=True)
    dh, dg = rms_bwd(r['h'], p['mix_norm'], dhn, name=nm('d_mixnorm'), add=dh1)
    g['mix_norm'] = dg[0]
    return dh, g


def _w_in_placement():
    c = np.arange(D_IN)
    dest = np.where(c < 2048, c, np.where(c < 2056, c + (D_IN - 2056), c - 8))
    dest = jnp.asarray(dest.reshape(N_DEV, D_IN // N_DEV, 1), jnp.int32)
    return (dest == jnp.arange(D_IN_PAD, dtype=jnp.int32)).astype(MXU_DTYPE)


def _owner_major(full, axis):
    shp = full.shape
    return jnp.moveaxis(full.reshape(shp[:axis] + (N_DEV, shp[axis] // N_DEV) + shp[axis + 1:]), axis, 0)


def _owner_join(parts, axis):
    moved = jnp.moveaxis(parts, 0, axis)
    shp = moved.shape
    return moved.reshape(shp[:axis] + (shp[axis] * shp[axis + 1],) + shp[axis + 2:])


def assemble_params(gathered, replicated):
    L = DEPTH
    place = _w_in_placement()
    w_in = mm(gathered['w_in'].reshape(N_DEV, L * D_MODEL, D_IN // N_DEV), place, name='place_w_in', ab='k', bb='k',
              out_dtype=MXU_DTYPE).reshape(L, D_MODEL, D_IN_PAD)
    out = dict(replicated)
    out['w_in'] = w_in
    out['ffn_w_up'] = jnp.moveaxis(gathered['ffn_w_up'], 1, 0)
    out['w_out'] = _owner_join(gathered['w_out'], 1)
    out['ffn_w_down'] = _owner_join(gathered['ffn_w_down'], 1).reshape(L, N_DEV // 2, FF_SHARD, D_MODEL)
    out['mla_w_uq'] = _owner_join(gathered['mla_w_uq'], 2)
    out['mla_w_ukv'] = _owner_join(gathered['mla_w_ukv'], 2)
    out['ssm_conv_w'] = _owner_join(gathered['ssm_conv_w'], 2)[:, None]
    out['ffn_conv_w'] = jnp.moveaxis(gathered['ffn_conv_w'], 1, 0)
    out['ssm_conv_b'] = replicated['ssm_conv_b'].reshape(L, 1, 1, SSM_CONV_DIM)
    out['ffn_conv_b'] = replicated['ffn_conv_b'].reshape(L, N_DEV, 1, FF_SHARD)
    return out


def owner_parts(grads):
    L = DEPTH
    st = lambda n: jnp.stack([g[n] for g in grads])
    place = _w_in_placement()
    parts = {
        'w_in': mm(st('w_in').reshape(L * D_MODEL, D_IN_PAD), place, name='unplace_w_in', tb=True, bb='o',
                   out_dtype=WIRE_DTYPE).reshape(N_DEV, L, D_MODEL, D_IN // N_DEV),
        'ffn_w_up': jnp.moveaxis(st('ffn_w_up'), 1, 0),
        'w_out': _owner_major(st('w_out'), 1),
        'ffn_w_down': _owner_major(st('ffn_w_down').reshape(L, D_FF, D_MODEL), 1),
        'mla_w_uq': _owner_major(st('mla_w_uq'), 2),
        'mla_w_ukv': _owner_major(st('mla_w_ukv'), 2),
        'ssm_conv_w': _owner_major(st('ssm_conv_w')[:, 0], 2),
        'ffn_conv_w': jnp.moveaxis(st('ffn_conv_w'), 1, 0),
    }
    rep = {n: st(n) for n in REPLICATED if n not in ('final_norm', 'ssm_conv_b', 'ffn_conv_b')}
    rep['ssm_conv_b'] = st('ssm_conv_b').reshape(L, SSM_CONV_DIM)
    rep['ffn_conv_b'] = st('ffn_conv_b').reshape(L, 2 * D_FF)
    return parts, rep


def local_step(x, positions, target, params):
    s = x.shape[0]
    tabs = _rope_tables(positions, s)
    h = x
    saved = []
    for li in range(DEPTH):
        p = {n: params[n][li] for n in WEIGHTS if n != 'final_norm'}
        h, r = layer_fwd(h, p, tabs, li)
        saved.append((p, r))
    y = rms_fwd(h, params['final_norm'], name='finalnorm')
    dy, loss = loss_head(y, target, name='loss')
    dh, dg = rms_bwd(h, params['final_norm'], dy, name='d_finalnorm')
    per_layer = []
    for li in reversed(range(DEPTH)):
        p, r = saved[li]
        dh, g = layer_bwd(dh, p, r, tabs, li)
        per_layer.append(g)
    per_layer.reverse()
    parts, rep = owner_parts(per_layer)
    rep['final_norm'] = dg[0]
    return loss[0, 0], dh, parts, rep


MESH = pl.DeviceIdType.MESH
HBM = pl.BlockSpec(memory_space=pltpu.HBM)


def _flip(v, bit):
    return 1 - v if bit else v


def all_gather(blocks, *, name):
    n = len(blocks)

    def body(*refs):
        x_refs, out_refs = refs[:n], refs[n:2 * n]
        send_sems, recv_sems, local_sems = refs[2 * n:]
        x, y, c = lax.axis_index("x"), lax.axis_index("y"), lax.axis_index("c")
        me, sibling = (x, y, c), (x, y, 1 - c)
        chips = [(1 - x, y), (x, 1 - y), (1 - x, 1 - y)]

        def slot(b, px, py, pc):
            return out_refs[b].at[4 * px + 2 * py + pc]

        def copy(b, k, blk, to, src=None):
            return pltpu.make_async_remote_copy(src_ref=slot(b, *blk) if src is None else src, dst_ref=slot(b, *blk),
                                                send_sem=send_sems.at[b, k], recv_sem=recv_sems.at[b, k],
                                                device_id=to, device_id_type=MESH)

        mine = [pltpu.make_async_copy(x_refs[b], slot(b, *me), local_sems.at[b]) for b in range(n)]
        for cp in mine:
            cp.start()
        first = []
        for b in range(n):
            first.append(copy(b, 0, me, sibling, src=x_refs[b]))
            first += [copy(b, 1 + j, me, (*chip, c), src=x_refs[b]) for j, chip in enumerate(chips)]
        for cp in first:
            cp.start()
        passed = []
        for j, chip in enumerate(chips):
            for b in range(n):
                copy(b, 1 + j, (*chip, c), me).wait_recv()
                fwd = copy(b, 4 + j, (*chip, c), sibling)
                fwd.start()
                passed.append(fwd)
        for b in range(n):
            copy(b, 0, sibling, me).wait_recv()
            for j, chip in enumerate(chips):
                copy(b, 4 + j, (*chip, 1 - c), me).wait_recv()
        for cp in first + passed:
            cp.wait_send()
        for cp in mine:
            cp.wait()

    return pl.pallas_call(
        body, name=name, out_shape=[jax.ShapeDtypeStruct((N_DEV,) + b.shape, b.dtype) for b in blocks],
        in_specs=[HBM] * n, out_specs=[HBM] * n,
        scratch_shapes=[pltpu.SemaphoreType.DMA((n, 7)), pltpu.SemaphoreType.DMA((n, 7)), pltpu.SemaphoreType.DMA((n,))],
    )(*blocks)


def all_to_all(parts, *, name):
    n = len(parts)

    def body(*refs):
        g_refs, r_refs = refs[:n], refs[n:2 * n]
        send_sems, recv_sems, local_sems = refs[2 * n:]
        x, y, c = lax.axis_index("x"), lax.axis_index("y"), lax.axis_index("c")
        me = 4 * x + 2 * y + c
        mine = [pltpu.make_async_copy(g_refs[b].at[me], r_refs[b].at[me], local_sems.at[b]) for b in range(n)]
        for cp in mine:
            cp.start()
        copies = []
        for k in range(1, N_DEV):
            px, py, pc = _flip(x, k & 4), _flip(y, k & 2), _flip(c, k & 1)
            peer = 4 * px + 2 * py + pc
            for b in range(n):
                cp = pltpu.make_async_remote_copy(src_ref=g_refs[b].at[peer], dst_ref=r_refs[b].at[me],
                                                  send_sem=send_sems.at[b, k - 1], recv_sem=recv_sems.at[b, k - 1],
                                                  device_id=(px, py, pc), device_id_type=MESH)
                cp.start()
                copies.append(cp)
        for cp in copies:
            cp.wait_send()
            cp.wait_recv()
        for cp in mine:
            cp.wait()

    return pl.pallas_call(
        body, name=name, out_shape=[jax.ShapeDtypeStruct(p.shape, p.dtype) for p in parts],
        in_specs=[HBM] * n, out_specs=[HBM] * n,
        scratch_shapes=[pltpu.SemaphoreType.DMA((n, 7)), pltpu.SemaphoreType.DMA((n, 7)), pltpu.SemaphoreType.DMA((n,))],
    )(*parts)


def adamw(parts, w, m, v, *, name):
    r, wd = w.shape
    br = _tile(r, (256, 128, 64, 32, 16, 8))
    c1 = 1.0 - ADAM_B1 ** ADAM_STEP
    c2 = 1.0 - ADAM_B2 ** ADAM_STEP

    def body(p_ref, w_ref, m_ref, v_ref, g_ref, d_ref, mo_ref, vo_ref):
        g = p_ref[0].astype(F32)
        for j in range(1, N_DEV):
            g = g + p_ref[j].astype(F32)
        mn = ADAM_B1 * m_ref[...] + (1.0 - ADAM_B1) * g
        vn = ADAM_B2 * v_ref[...] + (1.0 - ADAM_B2) * (g * g)
        g_ref[...] = g
        mo_ref[...] = mn
        vo_ref[...] = vn
        d_ref[...] = -ADAM_LR * ((mn / c1) / (jnp.sqrt(vn / c2) + ADAM_EPS) + ADAM_WD * w_ref[...])

    blk = pl.BlockSpec((br, wd), lambda i: (i, 0))
    out = jax.ShapeDtypeStruct((r, wd), F32)
    return _call(body, name=name, out_shape=[out] * 4, grid=(r // br,),
                 in_specs=[pl.BlockSpec((N_DEV, br, wd), lambda i: (0, i, 0)), blk, blk, blk], out_specs=[blk] * 4,
                 sem=("parallel",))(parts, w, m, v)


def _pack(arrs):
    flat = jnp.concatenate([a.reshape(-1) for a in arrs])
    rows = -(-flat.shape[0] // (8 * FLAT_W)) * 8
    return jnp.pad(flat, (0, rows * FLAT_W - flat.shape[0])).reshape(rows, FLAT_W)


def _unpack(flat, shapes):
    flat = flat.reshape(-1)
    out, off = [], 0
    for shp in shapes:
        n = int(np.prod(shp))
        out.append(flat[off:off + n].reshape(shp))
        off += n
    return out


def kernel(x, positions, mix_norm, w_in, sb_out_norm, ssm_conv_w, ssm_conv_b, ssm_dt_bias, ssm_a_log, ssm_d, ssm_out_norm, mla_q_norm, mla_w_uq, mla_kv_norm, mla_w_ukv, mla_out_norm, w_out, ffn_norm, ffn_w_up, ffn_conv_w, ffn_conv_b, ffn_w_down, final_norm, loss_target, m_mix_norm, m_w_in, m_sb_out_norm, m_ssm_conv_w, m_ssm_conv_b, m_ssm_dt_bias, m_ssm_a_log, m_ssm_d, m_ssm_out_norm, m_mla_q_norm, m_mla_w_uq, m_mla_kv_norm, m_mla_w_ukv, m_mla_out_norm, m_w_out, m_ffn_norm, m_ffn_w_up, m_ffn_conv_w, m_ffn_conv_b, m_ffn_w_down, m_final_norm, v_mix_norm, v_w_in, v_sb_out_norm, v_ssm_conv_w, v_ssm_conv_b, v_ssm_dt_bias, v_ssm_a_log, v_ssm_d, v_ssm_out_norm, v_mla_q_norm, v_mla_w_uq, v_mla_kv_norm, v_mla_w_ukv, v_mla_out_norm, v_w_out, v_ffn_norm, v_ffn_w_up, v_ffn_conv_w, v_ffn_conv_b, v_ffn_w_down, v_final_norm):
    args = locals()
    w = {n: args[n] for n in WEIGHTS}
    m = {n: args['m_' + n] for n in WEIGHTS}
    v = {n: args['v_' + n] for n in WEIGHTS}
    sharded = list(SHARDED)

    wire = [w[n] if n in VPU_WEIGHTS else w[n].astype(BF16) for n in sharded]
    gathered = dict(zip(sharded, all_gather(wire, name='gather_weights')))
    params = assemble_params(gathered, {n: w[n] for n in REPLICATED})

    loss, dx, parts, rep = local_step(x[0], positions[0], loss_target[0], params)
    loss = lax.psum(loss, ("x", "y", "c"))

    recv = all_to_all([parts[n].astype(WIRE_DTYPE) for n in sharded], name='scatter_grads')
    res = {kind: {} for kind in 'gdmv'}
    for n, rv in zip(sharded, recv):
        shp = w[n].shape
        two_d = (int(np.prod(shp[:-1])), shp[-1])
        outs = adamw(rv.reshape((N_DEV,) + two_d), w[n].reshape(two_d), m[n].reshape(two_d), v[n].reshape(two_d),
                     name='adamw_' + n)
        for kind, o in zip('gdmv', outs):
            res[kind][n] = o.reshape(shp)

    rep_shapes = [w[n].shape for n in REPLICATED]
    (rparts,) = all_gather([_pack([rep[n] for n in REPLICATED])], name='gather_small_grads')
    rflat = lambda d: _pack([d[n] for n in REPLICATED])
    routs = adamw(rparts, rflat(w), rflat(m), rflat(v), name='adamw_replicated')
    for kind, o in zip('gdmv', routs):
        res[kind].update(zip(REPLICATED, _unpack(o, rep_shapes)))

    return (loss, dx[None], *[res['g'][n] for n in WEIGHTS], *[res['d'][n] for n in WEIGHTS],
            *[res['m'][n] for n in WEIGHTS], *[res['v'][n] for n in WEIGHTS])
```

```python
import numpy as np
import jax
import jax.numpy as jnp
from jax import lax
from jax.experimental import pallas as pl
from jax.experimental.pallas import tpu as pltpu

F32 = jnp.float32
BF16 = jnp.bfloat16
MXU_DTYPE = jnp.bfloat16
HIGHEST = lax.Precision.HIGHEST
WIRE_DTYPE = jnp.bfloat16

N_DEV = 8
D_MODEL = 1024
DEPTH = 2
EPS = 1e-6
SB_HEADS, SB_DIM = 4, 64
SB_WIDTH = SB_HEADS * SB_DIM
SSM_HEADS, SSM_P, SSM_GROUPS, SSM_N, SSM_CONV, SSM_CHUNK = 8, 64, 2, 64, 4, 128
SSM_INNER = SSM_HEADS * SSM_P
SSM_CONV_DIM = SSM_INNER + 2 * SSM_GROUPS * SSM_N
MLA_HEADS, MLA_NOPE, MLA_ROPE, MLA_V, MLA_Q_RANK, MLA_KV_RANK = 4, 64, 32, 64, 256, 128
MLA_QK = MLA_NOPE + MLA_ROPE
ROPE_THETA = 10000.0
D_IN = 2472
D_IN_PAD = 2560
TAIL = 2432
DT_LANE = 32
D_FF = 2816
FF_SHARD = 2 * D_FF // N_DEV
ADAM_LR, ADAM_B1, ADAM_B2, ADAM_EPS, ADAM_WD, ADAM_STEP = 0.001, 0.9, 0.999, 1e-08, 0.01, 10

LANES = 128
ATT_BLK = 256
ATT_UNROLL = 4
ROW_BLK = 512
CONV_COLS = 256
FLAT_W = 1024
VMEM_LIMIT = 56 << 20
MM_TM = (1024, 512, 256, 128)
MM_TN = (1280, 1024, 768, 640, 512, 384, 256, 128)
MM_TK = (1280, 1024, 512, 256, 128)

WEIGHTS = ['mix_norm', 'w_in', 'sb_out_norm', 'ssm_conv_w', 'ssm_conv_b', 'ssm_dt_bias', 'ssm_a_log', 'ssm_d',
           'ssm_out_norm', 'mla_q_norm', 'mla_w_uq', 'mla_kv_norm', 'mla_w_ukv', 'mla_out_norm', 'w_out',
           'ffn_norm', 'ffn_w_up', 'ffn_conv_w', 'ffn_conv_b', 'ffn_w_down', 'final_norm']
SHARDED = {'w_in': 2, 'ssm_conv_w': 2, 'mla_w_uq': 2, 'mla_w_ukv': 2, 'w_out': 1, 'ffn_w_up': 2, 'ffn_conv_w': 2,
           'ffn_w_down': 1}
VPU_WEIGHTS = ('ssm_conv_w', 'ffn_conv_w')
EARLY = ('w_in', 'mla_w_uq', 'mla_w_ukv', 'ssm_conv_w')
LATE = ('w_out', 'ffn_w_up', 'ffn_conv_w', 'ffn_w_down')
REPLICATED = [n for n in WEIGHTS if n not in SHARDED]


def _call(body, *, name, out_shape, grid=(), in_specs=None, out_specs=None, scratch=(), sem=None, **kw):
    params = dict(vmem_limit_bytes=VMEM_LIMIT)
    if sem is not None:
        params['dimension_semantics'] = sem
    return pl.pallas_call(body, name=name, out_shape=out_shape, grid=grid, in_specs=in_specs, out_specs=out_specs,
                          scratch_shapes=list(scratch), compiler_params=pltpu.CompilerParams(**params), **kw)


def _tile(n, prefs):
    for t in prefs:
        if n % t == 0:
            return t
    return n


def _dot(a, b, dims, precision=None):
    return lax.dot_general(a, b, (dims, ((), ())), preferred_element_type=F32, precision=precision)


def _nn(a, b, precision=None):
    return _dot(a, b, ((1,), (0,)), precision)


def _nt(a, b, precision=None):
    return _dot(a, b, ((1,), (1,)), precision)


def _tn(a, b, precision=None):
    return _dot(a, b, ((0,), (0,)), precision)


def _mxu(f):
    return lambda a, b: f(a.astype(MXU_DTYPE), b.astype(MXU_DTYPE))


_bnn, _bnt, _btn = _mxu(_nn), _mxu(_nt), _mxu(_tn)


def _split2(x):
    hi = x.astype(MXU_DTYPE)
    lo = (x - hi.astype(F32)).astype(MXU_DTYPE)
    return hi, lo


def _sigmoid(x):
    return 0.5 * jnp.tanh(0.5 * x) + 0.5


def _softplus(x):
    return jnp.maximum(x, 0.0) + jnp.log1p(jnp.exp(-jnp.abs(x)))


def _softplus_att(x):
    return jnp.maximum(x, 0.0) + jnp.log(1.0 + jnp.exp(-jnp.abs(x)))


def _cum(x, u):
    n = x.shape[0]
    r = _nn(jnp.concatenate(_split2(x), axis=0), u)
    return r[:n] + r[n:]


def _causal_loop(qi, tile, carry, reverse):
    block = lambda j: qi - 1 - j if reverse else j

    def several(i, cr):
        for u in range(ATT_UNROLL):
            cr = tile(block(i * ATT_UNROLL + u), cr, False)
        return cr

    def rest(cr):
        full = qi // ATT_UNROLL
        cr = lax.fori_loop(0, full, several, cr)
        return lax.fori_loop(full * ATT_UNROLL, qi, lambda j, c: tile(block(j), c, False), cr)

    return rest(tile(qi, carry, True)) if reverse else tile(qi, rest(carry), True)


def mm(a, b, *, name, ta=False, tb=False, res=None, out_dtype=F32, ab=None, bb=None, precision=None):
    a2, b2 = a.shape[-2:], b.shape[-2:]
    (kdim, m) = a2 if ta else a2[::-1]
    (n, k2) = b2 if tb else b2[::-1]
    assert kdim == k2, (a.shape, b.shape, ta, tb)
    assert (ab == 'k') == (bb == 'k')
    kb = ab == 'k'
    nb = a.shape[0] if ab == 'o' else (b.shape[0] if bb == 'o' else None)
    tm, tn = _tile(m, MM_TM), _tile(n, MM_TN)
    tk = kdim if kb else _tile(kdim, MM_TK)
    nk = a.shape[0] if kb else kdim // tk
    dims = ((0 if ta else 1,), (1 if tb else 0,))
    op_dtype = F32 if precision is not None else MXU_DTYPE

    def body(*refs):
        a_ref, b_ref = refs[0], refs[1]
        r_ref = refs[2] if res is not None else None
        o_ref = refs[3] if res is not None else refs[2]
        part = _dot(a_ref[...].astype(op_dtype), b_ref[...].astype(op_dtype), dims, precision)

        def finish(out):
            if res is not None:
                out = out + r_ref[...]
            o_ref[...] = out.astype(out_dtype)

        if nk == 1:
            finish(part)
            return
        acc = refs[-1]
        k = pl.program_id(3)

        @pl.when(k == 0)
        def _():
            acc[...] = part

        @pl.when(k > 0)
        def _():
            acc[...] += part

        @pl.when(k == nk - 1)
        def _():
            finish(acc[...])

    def spec(blk, idx, how):
        if how is None:
            return pl.BlockSpec(blk, idx)
        if how == 'o':
            return pl.BlockSpec((None,) + blk, lambda p, i, j, k: (p,) + idx(p, i, j, k))
        return pl.BlockSpec((None,) + blk, lambda p, i, j, k: (k,) + idx(p, i, j, 0))

    a_spec = spec((tk, tm), lambda p, i, j, k: (k, i), ab) if ta else spec((tm, tk), lambda p, i, j, k: (i, k), ab)
    b_spec = spec((tn, tk), lambda p, i, j, k: (j, k), bb) if tb else spec((tk, tn), lambda p, i, j, k: (k, j), bb)
    o_spec = spec((tm, tn), lambda p, i, j, k: (i, j), None if nb is None else 'o')
    ins, specs = [a, b], [a_spec, b_spec]
    if res is not None:
        ins.append(res)
        specs.append(o_spec)
    out_shape = (m, n) if nb is None else (nb, m, n)
    return _call(body, name=name, out_shape=jax.ShapeDtypeStruct(out_shape, out_dtype),
                 grid=(1 if nb is None else nb, m // tm, n // tn, nk), in_specs=specs, out_specs=o_spec,
                 scratch=[] if nk == 1 else [pltpu.VMEM((tm, tn), F32)],
                 sem=("parallel", "parallel", "parallel", "arbitrary"))(*ins)


def rms_fwd(x, g, *, name, gate=None, out_dtype=F32):
    s, w = x.shape
    bs = _tile(s, (ROW_BLK,))

    def body(*refs):
        if gate is None:
            x_ref, g_ref, o_ref = refs
            u = x_ref[...]
        else:
            x_ref, z_ref, g_ref, o_ref = refs
            z = z_ref[...]
            u = x_ref[...] * (z * _sigmoid(z))
        r = lax.rsqrt(jnp.mean(u * u, axis=1, keepdims=True) + EPS)
        o_ref[...] = (u * r * g_ref[...]).astype(out_dtype)

    row = pl.BlockSpec((bs, w), lambda i: (i, 0))
    vec = pl.BlockSpec((1, w), lambda i: (0, 0))
    ins = [x] + ([] if gate is None else [gate]) + [g.reshape(1, w)]
    specs = [row] + ([] if gate is None else [row]) + [vec]
    return _call(body, name=name, out_shape=jax.ShapeDtypeStruct((s, w), out_dtype), grid=(s // bs,),
                 in_specs=specs, out_specs=row, sem=("parallel",))(*ins)


def rms_bwd(x, g, dy, *, name, gate=None, add=None):
    s, w = x.shape
    bs = _tile(s, (ROW_BLK,))

    def body(*refs):
        refs = list(refs)
        x_ref = refs.pop(0)
        z_ref = refs.pop(0) if gate is not None else None
        g_ref = refs.pop(0)
        dy_ref = refs.pop(0)
        add_ref = refs.pop(0) if add is not None else None
        dx_ref = refs.pop(0)
        dz_ref = refs.pop(0) if gate is not None else None
        dg_ref = refs.pop(0)
        i = pl.program_id(0)

        @pl.when(i == 0)
        def _():
            dg_ref[...] = jnp.zeros_like(dg_ref)

        xv = x_ref[...]
        if gate is not None:
            z = z_ref[...]
            sg = _sigmoid(z)
            act = z * sg
            u = xv * act
        else:
            u = xv
        r = lax.rsqrt(jnp.mean(u * u, axis=1, keepdims=True) + EPS)
        dy_v = dy_ref[...]
        dyg = dy_v * g_ref[...]
        du = r * dyg - u * (r * r * r * jnp.mean(dyg * u, axis=1, keepdims=True))
        dg_ref[...] += jnp.sum(dy_v * u * r, axis=0, keepdims=True)
        if gate is not None:
            dx = du * act
            dz_ref[...] = du * xv * (sg * (1.0 + z * (1.0 - sg)))
        else:
            dx = du
        if add is not None:
            dx = dx + add_ref[...]
        dx_ref[...] = dx

    row = pl.BlockSpec((bs, w), lambda i: (i, 0))
    vec = pl.BlockSpec((1, w), lambda i: (0, 0))
    ins = [x] + ([] if gate is None else [gate]) + [g.reshape(1, w), dy] + ([] if add is None else [add])
    specs = [row] + ([] if gate is None else [row]) + [vec, row] + ([] if add is None else [row])
    outs = [jax.ShapeDtypeStruct((s, w), F32)] + ([] if gate is None else [jax.ShapeDtypeStruct((s, w), F32)])
    outs.append(jax.ShapeDtypeStruct((1, w), F32))
    ospecs = [row] + ([] if gate is None else [row]) + [vec]
    return _call(body, name=name, out_shape=outs, grid=(s // bs,), in_specs=specs, out_specs=ospecs,
                 sem=("arbitrary",))(*ins)


def loss_head(y, target, *, name):
    s, w = y.shape
    bs = _tile(s, (ROW_BLK,))
    nb = s // bs

    def body(y_ref, t_ref, dy_ref, loss_ref, acc):
        i = pl.program_id(0)

        @pl.when(i == 0)
        def _():
            acc[...] = jnp.zeros_like(acc)

        e = y_ref[...] - t_ref[...]
        dy_ref[...] = e * (1.0 / w)
        acc[...] += jnp.sum(e * e, axis=0, keepdims=True)

        @pl.when(i == nb - 1)
        def _():
            loss_ref[...] = jnp.sum(acc[...], axis=1, keepdims=True) * (0.5 / w)

    row = pl.BlockSpec((bs, w), lambda i: (i, 0))
    return _call(body, name=name, out_shape=[jax.ShapeDtypeStruct((s, w), F32), jax.ShapeDtypeStruct((1, 1), F32)],
                 grid=(nb,), in_specs=[row, row], out_specs=[row, pl.BlockSpec((1, 1), lambda i: (0, 0))],
                 scratch=[pltpu.VMEM((1, w), F32)], sem=("arbitrary",))(y, target)


def _rope_tables(positions, s):
    inv_freq = 1.0 / (ROPE_THETA ** (jnp.arange(0, MLA_ROPE, 2, dtype=F32) / MLA_ROPE))
    ang = positions.reshape(s, 1).astype(F32) * inv_freq
    cos, sin = jnp.cos(ang), jnp.sin(ang)
    one, zero = jnp.ones((s, MLA_NOPE), F32), jnp.zeros((s, MLA_NOPE), F32)
    cq = jnp.tile(jnp.concatenate([one, cos, cos], axis=1), (1, MLA_HEADS))
    sq = jnp.tile(jnp.concatenate([zero, sin, sin], axis=1), (1, MLA_HEADS))
    pad1, pad0 = jnp.ones((s, LANES - MLA_ROPE), F32), jnp.zeros((s, LANES - MLA_ROPE), F32)
    ct = jnp.concatenate([cos, cos, pad1], axis=1)
    st = jnp.concatenate([sin, sin, pad0], axis=1)
    half = MLA_ROPE // 2

    def swap(width, starts):
        r = np.zeros((width, width), np.float32)
        for o in starts:
            for i in range(half):
                r[o + half + i, o + i] = -1.0
                r[o + i, o + half + i] = 1.0
        return jnp.asarray(r)

    rq = swap(MLA_HEADS * MLA_QK, [h * MLA_QK + MLA_NOPE for h in range(MLA_HEADS)])
    rt = swap(LANES, [0])
    return (cq, sq, rq), (ct, st, rt)


def rope(x, tabs, *, name, backward=False, add=None):
    cos, sin, rot = tabs
    n, s, w = x.shape
    bs = _tile(s, (ROW_BLK,))

    def body(*refs):
        if add is None:
            x_ref, c_ref, s_ref, r_ref, o_ref = refs
        else:
            x_ref, c_ref, s_ref, r_ref, a_ref, o_ref = refs
        xv = x_ref[0]
        for j in range(1, n):
            xv = xv + x_ref[j]
        if backward:
            out = xv * c_ref[...] + _nt(xv * s_ref[...], r_ref[...], HIGHEST)
        else:
            out = xv * c_ref[...] + _nn(xv, r_ref[...], HIGHEST) * s_ref[...]
        if add is not None:
            out = out + a_ref[...]
        o_ref[...] = out

    row = pl.BlockSpec((bs, w), lambda i: (i, 0))
    ins = [x, cos, sin, rot] + ([] if add is None else [add])
    specs = [pl.BlockSpec((n, bs, w), lambda i: (0, i, 0)), row, row, pl.BlockSpec((w, w), lambda i: (0, 0))]
    specs += [] if add is None else [row]
    return _call(body, name=name, out_shape=jax.ShapeDtypeStruct((s, w), F32), grid=(s // bs,), in_specs=specs,
                 out_specs=row, sem=("parallel",))(*ins)


MESH = pl.DeviceIdType.MESH
HBM = pl.BlockSpec(memory_space=pltpu.HBM)


def _flip(v, bit):
    return 1 - v if bit else v


class Carried:
    def __init__(self, kind, arrays):
        assert kind in ('gather', 'scatter')
        self.kind, self.arrays, self.n = kind, list(arrays), len(arrays)

    @property
    def out_shape(self):
        lead = (N_DEV,) if self.kind == 'gather' else ()
        return [jax.ShapeDtypeStruct(lead + a.shape, a.dtype) for a in self.arrays]

    @property
    def scratch(self):
        return [pltpu.SemaphoreType.DMA((self.n, N_DEV - 1)), pltpu.SemaphoreType.DMA((self.n, N_DEV - 1)),
                pltpu.SemaphoreType.DMA((self.n,))]

    def _copies(self, in_refs, out_refs, sems):
        send_sems, recv_sems, local_sems = sems
        x, y, c = lax.axis_index("x"), lax.axis_index("y"), lax.axis_index("c")
        me = 4 * x + 2 * y + c
        part = (lambda b, p: in_refs[b]) if self.kind == 'gather' else (lambda b, p: in_refs[b].at[p])
        local = [pltpu.make_async_copy(part(b, me), out_refs[b].at[me], local_sems.at[b]) for b in range(self.n)]
        remote = []
        for k in range(1, N_DEV):
            px, py, pc = _flip(x, k & 4), _flip(y, k & 2), _flip(c, k & 1)
            for b in range(self.n):
                remote.append(pltpu.make_async_remote_copy(
                    src_ref=part(b, 4 * px + 2 * py + pc), dst_ref=out_refs[b].at[me], send_sem=send_sems.at[b, k - 1],
                    recv_sem=recv_sems.at[b, k - 1], device_id=(px, py, pc), device_id_type=MESH))
        return local, remote

    def start(self, in_refs, out_refs, sems):
        local, remote = self._copies(in_refs, out_refs, sems)
        for cp in local + remote:
            cp.start()

    def wait(self, in_refs, out_refs, sems):
        local, remote = self._copies(in_refs, out_refs, sems)
        for cp in remote:
            cp.wait_send()
            cp.wait_recv()
        for cp in local:
            cp.wait()


def _ride(carried, refs, n_in, n_out):
    n = 0 if carried is None else carried.n
    own_in, ride_in = refs[:n_in], refs[n_in:n_in + n]
    own_out, ride_out = refs[n_in + n:n_in + n + n_out], refs[n_in + n + n_out:n_in + 2 * n + n_out]
    return own_in, own_out, (ride_in, ride_out, refs[n_in + 2 * n + n_out:])


def _tri(n, op):
    r = lax.broadcasted_iota(jnp.int32, (n, n), 0)
    c = lax.broadcasted_iota(jnp.int32, (n, n), 1)
    return r, c, op(r, c)


def _pair_split(x, first):
    zero = jnp.zeros_like(x)
    return jnp.where(first, x, zero), jnp.where(first, zero, x)


def _sb_specs(s, blk):
    npair = SB_WIDTH // LANES
    q = pl.BlockSpec((blk, LANES), lambda j, i: (i, j))
    k = pl.BlockSpec((s, LANES), lambda j, i: (0, npair + j))
    v = pl.BlockSpec((s, LANES), lambda j, i: (0, 2 * npair + j))
    full = pl.BlockSpec((s, LANES), lambda j, i: (0, j))
    return npair, q, k, v, full


def _stack_heads(x, first):
    return jnp.concatenate(_pair_split(x, first), axis=0)


def _unstack_heads(x, first, blk):
    return jnp.where(first, x[:blk], x[blk:])


def sb_fwd(qkv, *, name, carried=None):
    s = qkv.shape[0]
    blk = _tile(s, (ATT_BLK,))
    scale = SB_DIM ** -0.5
    npair, nq = SB_WIDTH // LANES, s // blk

    def body(*refs):
        (q_ref, k_ref, v_ref), (y_ref, t_ref), ride = _ride(carried, refs, 3, 2)
        pair, qi = pl.program_id(0), pl.program_id(1)
        if carried is not None:
            @pl.when((pair == 0) & (qi == 0))
            def _():
                carried.start(*ride)

        first = lax.broadcasted_iota(jnp.int32, (blk, LANES), 1) < SB_DIM
        q2 = _stack_heads((q_ref[...].astype(F32) * scale).astype(MXU_DTYPE), first)
        row, col, later_mask = _tri(blk, lambda r, c: r > c)
        u_later = later_mask.astype(MXU_DTYPE)
        valid = jnp.tile(col < row, (2, 1))

        def tile(kb, carry, masked):
            c, acc = carry
            ks = pl.multiple_of(kb * blk, blk)
            z = _nt(q2, k_ref[pl.ds(ks, blk), :])
            sp = _softplus_att(z)
            spm = jnp.where(valid, sp, 0.0) if masked else sp
            w = jnp.exp((z - sp) - _cum(spm, u_later) + c)
            if masked:
                w = jnp.where(valid, w, 0.0)
            return c - jnp.sum(spm, axis=1, keepdims=True), acc + _nn(w.astype(MXU_DTYPE), v_ref[pl.ds(ks, blk), :])

        zero = (jnp.zeros((2 * blk, 1), F32), jnp.zeros((2 * blk, LANES), F32))
        c, acc = _causal_loop(qi, tile, zero, True)
        y_ref[...] = _unstack_heads(acc, first, blk)
        t_ref[...] = _unstack_heads(c, first, blk)
        if carried is not None:
            @pl.when((pair == npair - 1) & (qi == nq - 1))
            def _():
                carried.wait(*ride)

    _, qspec, kspec, vspec, _ = _sb_specs(s, blk)
    out = jax.ShapeDtypeStruct((s, SB_WIDTH), F32)
    extra = [] if carried is None else carried.arrays
    return _call(body, name=name, out_shape=[out, out] + ([] if carried is None else carried.out_shape),
                 grid=(npair, nq), in_specs=[qspec, kspec, vspec] + [HBM] * len(extra),
                 out_specs=[qspec, qspec] + [HBM] * len(extra), scratch=[] if carried is None else carried.scratch,
                 sem=("arbitrary", "arbitrary"))(qkv, qkv, qkv, *extra)


def sb_bwd(qkv, dy, tot, *, name, carried=None):
    s = qkv.shape[0]
    blk = _tile(s, (ATT_BLK,))
    scale = SB_DIM ** -0.5
    npair, nq = SB_WIDTH // LANES, s // blk

    def body(*refs):
        (q_ref, k_ref, v_ref, dy_ref, t_ref), (dq_ref, dk_ref, dv_ref), ride = _ride(carried, refs, 5, 3)
        pair, qi = pl.program_id(0), pl.program_id(1)
        if carried is not None:
            @pl.when((pair == 0) & (qi == 0))
            def _():
                carried.start(*ride)

        @pl.when(qi == 0)
        def _():
            dk_ref[...] = jnp.zeros_like(dk_ref)
            dv_ref[...] = jnp.zeros_like(dv_ref)

        first = lax.broadcasted_iota(jnp.int32, (blk, LANES), 1) < SB_DIM
        q2 = _stack_heads((q_ref[...].astype(F32) * scale).astype(MXU_DTYPE), first)
        dy2 = _stack_heads(dy_ref[...].astype(MXU_DTYPE), first)
        tv = jnp.concatenate([t_ref[:, 0:1], t_ref[:, SB_DIM:SB_DIM + 1]], axis=0)
        row, col, incl_mask = _tri(blk, lambda r, c: r <= c)
        u_incl = incl_mask.astype(MXU_DTYPE)
        u_excl = (row < col).astype(MXU_DTYPE)
        valid = jnp.tile(col < row, (2, 1))

        def tile(kb, carry, masked):
            p, gc, dq = carry
            ks = pl.multiple_of(kb * blk, blk)
            kv = k_ref[pl.ds(ks, blk), :]
            z = _nt(q2, kv)
            sp = _softplus_att(z)
            spm = jnp.where(valid, sp, 0.0) if masked else sp
            w = jnp.exp((z - sp) + (tv + (_cum(spm, u_incl) + p)))
            if masked:
                w = jnp.where(valid, w, 0.0)
            g = w * _nt(dy2, v_ref[pl.ds(ks, blk), :])
            gex = _nn(g.astype(MXU_DTYPE), u_excl) + gc
            keep = jnp.exp(-spm)
            dz = g * keep - (1.0 - keep) * gex
            if masked:
                dz = jnp.where(valid, dz, 0.0)
            dzb = dz.astype(MXU_DTYPE)
            dk_ref[pl.ds(ks, blk), :] += _tn(dzb, q2)
            dv_ref[pl.ds(ks, blk), :] += _tn(w.astype(MXU_DTYPE), dy2)
            return (p + jnp.sum(spm, axis=1, keepdims=True), gc + jnp.sum(g, axis=1, keepdims=True), dq + _nn(dzb, kv))

        zero = jnp.zeros((2 * blk, 1), F32)
        _, _, dq = _causal_loop(qi, tile, (zero, zero, jnp.zeros((2 * blk, LANES), F32)), False)
        dq_ref[...] = _unstack_heads(dq, first, blk) * scale
        if carried is not None:
            @pl.when((pair == npair - 1) & (qi == nq - 1))
            def _():
                carried.wait(*ride)

    _, qspec, kspec, vspec, full = _sb_specs(s, blk)
    out = jax.ShapeDtypeStruct((s, SB_WIDTH), F32)
    extra = [] if carried is None else carried.arrays
    return _call(body, name=name, out_shape=[out, out, out] + ([] if carried is None else carried.out_shape),
                 grid=(npair, nq), in_specs=[qspec, kspec, vspec, qspec, qspec] + [HBM] * len(extra),
                 out_specs=[qspec, full, full] + [HBM] * len(extra), scratch=[] if carried is None else carried.scratch,
                 sem=("arbitrary", "arbitrary"))(qkv, qkv, qkv, dy, tot, *extra)


ATT_PAIR = 2


def _mla_specs(s, blk, dk, dv):
    q = pl.BlockSpec((ATT_PAIR, blk, dk), lambda hp, i: (hp, i, 0))
    k = pl.BlockSpec((ATT_PAIR, s, dk), lambda hp, i: (hp, 0, 0))
    v = pl.BlockSpec((ATT_PAIR, s, dv), lambda hp, i: (hp, 0, 0))
    y = pl.BlockSpec((ATT_PAIR, blk, dv), lambda hp, i: (hp, i, 0))
    lse = pl.BlockSpec((ATT_PAIR, blk, LANES), lambda hp, i: (hp, i, 0))
    return q, k, v, y, lse


def mla_fwd(q, k, v, *, name):
    h, s, dk = q.shape
    dv = v.shape[-1]
    blk = _tile(s, (ATT_BLK,))
    scale = dk ** -0.5

    def body(q_ref, k_ref, v_ref, y_ref, l_ref):
        qi = pl.program_id(1)
        row, col, valid = _tri(blk, lambda r, c: c <= r)

        def tile(kb, carry, masked):
            ks = pl.multiple_of(kb * blk, blk)
            out = []
            for hh in range(ATT_PAIR):
                m, l, acc = carry[hh]
                sc = _nt(q_ref[hh], k_ref[hh, pl.ds(ks, blk), :]) * scale
                if masked:
                    sc = jnp.where(valid, sc, -1e30)
                m2 = jnp.maximum(m, jnp.max(sc, axis=1, keepdims=True))
                p = jnp.exp(sc - m2)
                a = jnp.exp(m - m2)
                out.append((m2, a * l + jnp.sum(p, axis=1, keepdims=True),
                            a * acc + _nn(p.astype(MXU_DTYPE), v_ref[hh, pl.ds(ks, blk), :])))
            return tuple(out)

        init = (jnp.full((blk, 1), -1e30, F32), jnp.zeros((blk, 1), F32), jnp.zeros((blk, dv), F32))
        for hh, (m, l, acc) in enumerate(_causal_loop(qi, tile, (init,) * ATT_PAIR, False)):
            y_ref[hh] = acc / l
            l_ref[hh] = jnp.broadcast_to(m + jnp.log(l), (blk, LANES))

    qspec, kspec, vspec, yspec, lspec = _mla_specs(s, blk, dk, dv)
    return _call(body, name=name,
                 out_shape=[jax.ShapeDtypeStruct((h, s, dv), F32), jax.ShapeDtypeStruct((h, s, LANES), F32)],
                 grid=(h // ATT_PAIR, s // blk), in_specs=[qspec, kspec, vspec], out_specs=[yspec, lspec],
                 sem=("parallel", "arbitrary"))(q, k, v)


def mla_bwd(q, k, v, y, dy, lse, *, name):
    h, s, dk = q.shape
    dv = v.shape[-1]
    blk = _tile(s, (ATT_BLK,))
    scale = dk ** -0.5

    def body(q_ref, k_ref, v_ref, y_ref, dy_ref, l_ref, dq_ref, dk_ref, dv_ref):
        qi = pl.program_id(1)

        @pl.when(qi == 0)
        def _():
            dk_ref[...] = jnp.zeros_like(dk_ref)
            dv_ref[...] = jnp.zeros_like(dv_ref)

        as_row = lambda col: jnp.transpose(jnp.broadcast_to(col, (blk, LANES)))[0:1, :]
        dyv = [dy_ref[hh].astype(MXU_DTYPE) for hh in range(ATT_PAIR)]
        delta = [as_row(jnp.sum(dy_ref[hh] * y_ref[hh], axis=1, keepdims=True)) for hh in range(ATT_PAIR)]
        lv = [as_row(l_ref[hh, :, 0:1]) for hh in range(ATT_PAIR)]
        row, col, valid = _tri(blk, lambda r, c: r <= c)

        def tile(kb, dqs, masked):
            ks = pl.multiple_of(kb * blk, blk)
            out = []
            for hh in range(ATT_PAIR):
                qv = q_ref[hh]
                kv = k_ref[hh, pl.ds(ks, blk), :]
                vv = v_ref[hh, pl.ds(ks, blk), :]
                p = jnp.exp(_nt(kv, qv) * scale - lv[hh])
                if masked:
                    p = jnp.where(valid, p, 0.0)
                ds = (p * (_nt(vv, dyv[hh]) - delta[hh])).astype(MXU_DTYPE)
                dk_ref[hh, pl.ds(ks, blk), :] += _nn(ds, qv) * scale
                dv_ref[hh, pl.ds(ks, blk), :] += _nn(p.astype(MXU_DTYPE), dyv[hh])
                out.append(dqs[hh] + _tn(ds, kv))
            return tuple(out)

        for hh, dq in enumerate(_causal_loop(qi, tile, (jnp.zeros((blk, dk), F32),) * ATT_PAIR, False)):
            dq_ref[hh] = dq * scale

    qspec, kspec, vspec, yspec, lspec = _mla_specs(s, blk, dk, dv)
    return _call(body, name=name,
                 out_shape=[jax.ShapeDtypeStruct((h, s, dk), F32), jax.ShapeDtypeStruct((h, s, dk), F32),
                            jax.ShapeDtypeStruct((h, s, dv), F32)],
                 grid=(h // ATT_PAIR, s // blk), in_specs=[qspec, kspec, vspec, yspec, yspec, lspec],
                 out_specs=[qspec, kspec, vspec], sem=("parallel", "arbitrary"))(q, k, v, y, dy, lse)


HALO = 8
CONV_CHUNK = 16


def _conv_tiles(x):
    s, c = x.shape[-2:]
    return s, c, _tile(s, (ROW_BLK,)), _tile(c, (CONV_COLS,))


def _conv_specs(bs, cw, lead=()):
    zero = (0,) * len(lead)
    blk = pl.BlockSpec(lead + (None, bs, cw), lambda p, j, i: zero + (p, i, j))
    halo = pl.BlockSpec(lead + (None, HALO, cw), lambda p, j, i: zero + (p, jnp.maximum(i * (bs // HALO) - 1, 0), j))
    w = lambda kk: pl.BlockSpec(lead + (None, kk, cw), lambda p, j, i: zero + (p, 0, j))
    return blk, halo, w


def _stage(scr, x_ref, halo_ref, first):
    scr[0:HALO, :] = jnp.where(first, 0.0, halo_ref[...])
    scr[HALO:, :] = x_ref[...]


def _conv_taps(scr, kk, r0):
    return [scr[pl.ds(HALO - (kk - 1) + k + r0, CONV_CHUNK), :] for k in range(kk)]


def _conv_sum(taps, w_ref, b_ref):
    u = b_ref[...] + taps[0] * w_ref[0:1, :]
    for k in range(1, len(taps)):
        u = u + taps[k] * w_ref[k:k + 1, :]
    return u


def _fold(x):
    out = x[0:8]
    for r in range(8, CONV_CHUNK, 8):
        out = out + x[r:r + 8]
    return out


class _TapSums:
    def __init__(self, kk, cw):
        self.w = [jnp.zeros((8, cw), F32) for _ in range(kk)]
        self.b = jnp.zeros((8, cw), F32)

    def add(self, du, taps):
        self.w = [a + _fold(du * t) for a, t in zip(self.w, taps)]
        self.b = self.b + _fold(du)

    def flush(self, dw_ref, db_ref):
        for k, a in enumerate(self.w):
            dw_ref[k:k + 1, :] += jnp.sum(a, axis=0, keepdims=True)
        db_ref[...] += jnp.sum(self.b, axis=0, keepdims=True)


def _silu_grad(u):
    sg = _sigmoid(u)
    return sg * (1.0 + u * (1.0 - sg))


def conv_silu_fwd(x, w, b, *, name):
    s, c, bs, cw = _conv_tiles(x)
    kk = w.shape[1]

    def body(x_ref, h_ref, w_ref, b_ref, o_ref, scr):
        _stage(scr, x_ref, h_ref, pl.program_id(2) == 0)
        for r0 in range(0, bs, CONV_CHUNK):
            u = _conv_sum(_conv_taps(scr, kk, r0), w_ref, b_ref)
            o_ref[pl.ds(r0, CONV_CHUNK), :] = u * _sigmoid(u)

    blk, halo, wspec = _conv_specs(bs, cw)
    return _call(body, name=name, out_shape=jax.ShapeDtypeStruct(x.shape, F32), grid=(x.shape[0], c // cw, s // bs),
                 in_specs=[blk, halo, wspec(kk), wspec(1)], out_specs=blk, scratch=[pltpu.VMEM((bs + HALO, cw), F32)],
                 sem=("parallel", "parallel", "arbitrary"))(x, x, w, b)


def conv_silu_bwd(x, dy, w, b, *, name):
    s, c, bs, cw = _conv_tiles(x)
    kk = w.shape[1]

    def body(x_ref, h_ref, w_ref, b_ref, dy_ref, du_ref, dw_ref, db_ref, scr):
        i = pl.program_id(2)

        @pl.when(i == 0)
        def _():
            dw_ref[...] = jnp.zeros_like(dw_ref)
            db_ref[...] = jnp.zeros_like(db_ref)

        _stage(scr, x_ref, h_ref, i == 0)
        sums = _TapSums(kk, cw)
        for r0 in range(0, bs, CONV_CHUNK):
            taps = _conv_taps(scr, kk, r0)
            du = dy_ref[pl.ds(r0, CONV_CHUNK), :] * _silu_grad(_conv_sum(taps, w_ref, b_ref))
            du_ref[pl.ds(r0, CONV_CHUNK), :] = du
            sums.add(du, taps)
        sums.flush(dw_ref, db_ref)

    blk, halo, wspec = _conv_specs(bs, cw)
    return _call(body, name=name,
                 out_shape=[jax.ShapeDtypeStruct(x.shape, F32), jax.ShapeDtypeStruct(w.shape, F32),
                            jax.ShapeDtypeStruct(b.shape, F32)],
                 grid=(x.shape[0], c // cw, s // bs), in_specs=[blk, halo, wspec(kk), wspec(1), blk],
                 out_specs=[blk, wspec(kk), wspec(1)], scratch=[pltpu.VMEM((bs + HALO, cw), F32)],
                 sem=("parallel", "parallel", "arbitrary"))(x, x, w, b, dy)


def _glu_view(a):
    return a.reshape((2, a.shape[0] // 2) + a.shape[1:])


def conv_glu_fwd(x, w, b, *, name):
    s, c, bs, cw = _conv_tiles(x)
    kk = w.shape[1]
    half = x.shape[0] // 2

    def body(x_ref, h_ref, w_ref, b_ref, o_ref, gscr, vscr):
        first = pl.program_id(2) == 0
        _stage(gscr, x_ref.at[0], h_ref.at[0], first)
        _stage(vscr, x_ref.at[1], h_ref.at[1], first)
        for r0 in range(0, bs, CONV_CHUNK):
            gate = _conv_sum(_conv_taps(gscr, kk, r0), w_ref.at[0], b_ref.at[0])
            val = _conv_sum(_conv_taps(vscr, kk, r0), w_ref.at[1], b_ref.at[1])
            o_ref[pl.ds(r0, CONV_CHUNK), :] = (gate * _sigmoid(gate) * val).astype(o_ref.dtype)

    blk, halo, wspec = _conv_specs(bs, cw, lead=(2,))
    out, _, _ = _conv_specs(bs, cw)
    xv = _glu_view(x)
    return _call(body, name=name, out_shape=jax.ShapeDtypeStruct((half, s, c), MXU_DTYPE), grid=(half, c // cw, s // bs),
                 in_specs=[blk, halo, wspec(kk), wspec(1)], out_specs=out, scratch=[pltpu.VMEM((bs + HALO, cw), F32)] * 2,
                 sem=("parallel", "parallel", "arbitrary"))(xv, xv, _glu_view(w), _glu_view(b))


def conv_glu_bwd(x, da, w, b, *, name):
    s, c, bs, cw = _conv_tiles(x)
    kk = w.shape[1]
    half = x.shape[0] // 2

    def body(x_ref, h_ref, w_ref, b_ref, da_ref, du_ref, dw_ref, db_ref, gscr, vscr):
        i = pl.program_id(2)

        @pl.when(i == 0)
        def _():
            dw_ref[...] = jnp.zeros_like(dw_ref)
            db_ref[...] = jnp.zeros_like(db_ref)

        _stage(gscr, x_ref.at[0], h_ref.at[0], i == 0)
        _stage(vscr, x_ref.at[1], h_ref.at[1], i == 0)
        gsums, vsums = _TapSums(kk, cw), _TapSums(kk, cw)
        for r0 in range(0, bs, CONV_CHUNK):
            gtaps, vtaps = _conv_taps(gscr, kk, r0), _conv_taps(vscr, kk, r0)
            gate = _conv_sum(gtaps, w_ref.at[0], b_ref.at[0])
            val = _conv_sum(vtaps, w_ref.at[1], b_ref.at[1])
            dav = da_ref[pl.ds(r0, CONV_CHUNK), :]
            dgate = dav * val * _silu_grad(gate)
            dval = dav * gate * _sigmoid(gate)
            du_ref[0, pl.ds(r0, CONV_CHUNK), :] = dgate
            du_ref[1, pl.ds(r0, CONV_CHUNK), :] = dval
            gsums.add(dgate, gtaps)
            vsums.add(dval, vtaps)
        gsums.flush(dw_ref.at[0], db_ref.at[0])
        vsums.flush(dw_ref.at[1], db_ref.at[1])

    blk, halo, wspec = _conv_specs(bs, cw, lead=(2,))
    daspec, _, _ = _conv_specs(bs, cw)
    xv, wv, bv = _glu_view(x), _glu_view(w), _glu_view(b)
    du, dw, db = _call(body, name=name,
                       out_shape=[jax.ShapeDtypeStruct(xv.shape, F32), jax.ShapeDtypeStruct(wv.shape, F32),
                                  jax.ShapeDtypeStruct(bv.shape, F32)],
                       grid=(half, c // cw, s // bs), in_specs=[blk, halo, wspec(kk), wspec(1), daspec],
                       out_specs=[blk, wspec(kk), wspec(1)], scratch=[pltpu.VMEM((bs + HALO, cw), F32)] * 2,
                       sem=("parallel", "parallel", "arbitrary"))(xv, xv, wv, bv, da)
    return du.reshape(x.shape), dw.reshape(w.shape), db.reshape(b.shape)


def conv_t(du, w, *, name):
    s, c, bs, cw = _conv_tiles(du)
    kk = w.shape[1]
    nb = s // bs

    def body(d_ref, h_ref, w_ref, o_ref, scr):
        last = pl.program_id(2) == nb - 1
        scr[0:bs, :] = d_ref[...]
        scr[bs:, :] = jnp.where(last, 0.0, h_ref[...])
        for r0 in range(0, bs, CONV_CHUNK):
            acc = scr[pl.ds(r0 + kk - 1, CONV_CHUNK), :] * w_ref[0:1, :]
            for k in range(1, kk):
                acc = acc + scr[pl.ds(r0 + kk - 1 - k, CONV_CHUNK), :] * w_ref[k:k + 1, :]
            o_ref[pl.ds(r0, CONV_CHUNK), :] = acc

    blk, _, wspec = _conv_specs(bs, cw)
    halo = pl.BlockSpec((None, HALO, cw), lambda q, j, i: (q, jnp.minimum((i + 1) * (bs // HALO), s // HALO - 1), j))
    return _call(body, name=name, out_shape=jax.ShapeDtypeStruct(du.shape, F32), grid=(du.shape[0], c // cw, nb),
                 in_specs=[blk, halo, wspec(kk)], out_specs=blk, scratch=[pltpu.VMEM((bs + HALO, cw), F32)],
                 sem=("parallel", "parallel", "arbitrary"))(du, du, w)


def _ssd_common(xbc_ref, tail_ref, dtrt_ref, bias_ref, biast_ref, alog_ref, alogt_ref):
    L = SSM_CHUNK
    raw = tail_ref[...] + bias_ref[...]
    dt = _softplus(raw)
    dtt = _softplus(dtrt_ref[...] + biast_ref[...])
    a = -jnp.exp(alog_ref[...])
    at = -jnp.exp(alogt_ref[...])
    row, col, lower = _tri(L, lambda r, c: r >= c)
    tril = lower.astype(F32)
    cs = _nn(tril, dt * a, HIGHEST)
    cst = _nt(dtt * at, tril, HIGHEST)
    bm = [xbc_ref[:, SSM_INNER + g * SSM_N: SSM_INNER + (g + 1) * SSM_N] for g in range(SSM_GROUPS)]
    off = SSM_INNER + SSM_GROUPS * SSM_N
    cm = [xbc_ref[:, off + g * SSM_N: off + (g + 1) * SSM_N] for g in range(SSM_GROUPS)]
    cb = [_bnt(cm[g], bm[g]) for g in range(SSM_GROUPS)]
    return raw, dt, a, lower, tril, cs, cst, bm, cm, cb


def _ssd_head(hh, xbc_ref, dt, cs, cst, lower):
    L = SSM_CHUNK
    ln = DT_LANE + hh
    x = xbc_ref[:, hh * SSM_P:(hh + 1) * SSM_P]
    dtc = dt[:, ln:ln + 1]
    csc = cs[:, ln:ln + 1]
    csr = cst[hh:hh + 1, :]
    decay = jnp.exp(jnp.where(lower, csc - csr, -1e30))
    last = cs[L - 1:L, ln:ln + 1]
    return x, dtc, csc, decay, jnp.exp(csc), jnp.exp(last - csc), jnp.exp(last)


def _ssd_inputs(tail, dt_bias, a_log, d_skip):
    H = SSM_HEADS
    lanes = lambda vec: jnp.pad(vec.reshape(1, H), ((0, 0), (DT_LANE, LANES - DT_LANE - H)))
    return (tail, tail[:, DT_LANE:DT_LANE + H].T, lanes(dt_bias), dt_bias.reshape(H, 1), lanes(a_log),
            a_log.reshape(H, 1), lanes(d_skip))


def ssd_fwd(xbc, tail, dt_bias, a_log, d_skip, *, name):
    s = xbc.shape[0]
    L, H, P, N = SSM_CHUNK, SSM_HEADS, SSM_P, SSM_N
    nc = s // L

    def body(xbc_ref, tail_ref, dtrt_ref, bias_ref, biast_ref, alog_ref, alogt_ref, d_ref, y_ref, hp_ref, state):
        @pl.when(pl.program_id(0) == 0)
        def _():
            state[...] = jnp.zeros_like(state)

        raw, dt, a, lower, tril, cs, cst, bm, cm, cb = _ssd_common(
            xbc_ref, tail_ref, dtrt_ref, bias_ref, biast_ref, alog_ref, alogt_ref)
        for hh in range(H):
            g = hh // (H // SSM_GROUPS)
            x, dtc, csc, decay, e, tau, gamma = _ssd_head(hh, xbc_ref, dt, cs, cst, lower)
            xdt = x * dtc
            hprev = state[hh]
            hp_ref[hh] = hprev
            skip = d_ref[:, DT_LANE + hh:DT_LANE + hh + 1]
            y = _bnn(cb[g] * decay, xdt) + _bnn(cm[g], hprev) * e + x * skip
            y_ref[:, hh * P:(hh + 1) * P] = y
            state[hh] = hprev * gamma + _btn(bm[g] * tau, xdt)

    row = lambda w: pl.BlockSpec((L, w), lambda c: (c, 0))
    small = lambda shp: pl.BlockSpec(shp, lambda c: (0, 0))
    return _call(body, name=name,
                 out_shape=[jax.ShapeDtypeStruct((s, SSM_INNER), F32), jax.ShapeDtypeStruct((nc, H, N, P), F32)],
                 grid=(nc,),
                 in_specs=[row(SSM_CONV_DIM), row(LANES), pl.BlockSpec((H, L), lambda c: (0, c)), small((1, LANES)),
                           small((H, 1)), small((1, LANES)), small((H, 1)), small((1, LANES))],
                 out_specs=[row(SSM_INNER), pl.BlockSpec((None, H, N, P), lambda c: (c, 0, 0, 0))],
                 scratch=[pltpu.VMEM((H, N, P), F32)], sem=("arbitrary",))(xbc, *_ssd_inputs(tail, dt_bias, a_log, d_skip))


def ssd_bwd(xbc, tail, dt_bias, a_log, d_skip, hprev_all, dy, *, name):
    s = xbc.shape[0]
    L, H, P, N = SSM_CHUNK, SSM_HEADS, SSM_P, SSM_N
    nc = s // L
    hg = H // SSM_GROUPS

    def body(xbc_ref, tail_ref, dtrt_ref, bias_ref, biast_ref, alog_ref, alogt_ref, d_ref, hp_ref, dy_ref,
             dxbc_ref, ddt_ref, dbias_ref, dalog_ref, dd_ref, dstate):
        @pl.when(pl.program_id(0) == 0)
        def _():
            dstate[...] = jnp.zeros_like(dstate)
            dbias_ref[...] = jnp.zeros_like(dbias_ref)
            dalog_ref[...] = jnp.zeros_like(dalog_ref)
            dd_ref[...] = jnp.zeros_like(dd_ref)

        raw, dt, a, lower, tril, cs, cst, bm, cm, cb = _ssd_common(
            xbc_ref, tail_ref, dtrt_ref, bias_ref, biast_ref, alog_ref, alogt_ref)
        lane = lax.broadcasted_iota(jnp.int32, (L, LANES), 1)
        lane1 = lax.broadcasted_iota(jnp.int32, (1, LANES), 1)
        rowi = lax.broadcasted_iota(jnp.int32, (L, 1), 0)
        ones = jnp.ones((L, LANES), F32)
        dcs_all = jnp.zeros((L, LANES), F32)
        ddt_x = jnp.zeros((L, LANES), F32)
        dd_row = jnp.zeros((1, LANES), F32)
        dbm = [jnp.zeros((L, N), F32) for _ in range(SSM_GROUPS)]
        dcm = [jnp.zeros((L, N), F32) for _ in range(SSM_GROUPS)]
        dcb = [jnp.zeros((L, L), F32) for _ in range(SSM_GROUPS)]
        for hh in range(H):
            g = hh // hg
            ln = DT_LANE + hh
            x, dtc, csc, decay, e, tau, gamma = _ssd_head(hh, xbc_ref, dt, cs, cst, lower)
            xdt = x * dtc
            hprev = hp_ref[hh]
            dhn = dstate[hh]
            dyh = dy_ref[:, hh * P:(hh + 1) * P]
            m = cb[g] * decay
            dxdt = _btn(m, dyh) + _bnn(bm[g] * tau, dhn)
            dm = jnp.where(lower, _bnt(dyh, xdt), 0.0)
            dcb[g] = dcb[g] + dm * decay
            dseg = dm * m
            dcs = jnp.sum(dseg, axis=1, keepdims=True) - _tn(dseg, ones, HIGHEST)[:, 0:1]
            edy = e * dyh
            dcm[g] = dcm[g] + _bnt(edy, hprev)
            dcs = dcs + e * jnp.sum(dyh * _bnn(cm[g], hprev), axis=1, keepdims=True)
            xdh = _bnt(xdt, dhn)
            dbm[g] = dbm[g] + tau * xdh
            dtau_tau = jnp.sum(bm[g] * xdh, axis=1, keepdims=True) * tau
            dlast = jnp.sum(dtau_tau, axis=0, keepdims=True) + gamma * jnp.sum(dhn * hprev, keepdims=True)
            dcs = dcs - dtau_tau + jnp.where(rowi == L - 1, dlast, 0.0)
            dstate[hh] = gamma * dhn + _btn(cm[g], edy)
            dcs_all = jnp.where(lane == ln, dcs, dcs_all)
            ddt_x = jnp.where(lane == ln, jnp.sum(dxdt * x, axis=1, keepdims=True), ddt_x)
            dxbc_ref[:, hh * P:(hh + 1) * P] = dxdt * dtc + d_ref[:, ln:ln + 1] * dyh
            dd_row = jnp.where(lane1 == ln, jnp.sum(dyh * x, keepdims=True), dd_row)
        off = SSM_INNER + SSM_GROUPS * SSM_N
        for g in range(SSM_GROUPS):
            dxbc_ref[:, SSM_INNER + g * N: SSM_INNER + (g + 1) * N] = dbm[g] + _btn(dcb[g], cm[g])
            dxbc_ref[:, off + g * N: off + (g + 1) * N] = dcm[g] + _bnn(dcb[g], bm[g])
        dda = _tn(tril, dcs_all, HIGHEST)
        head_lane = (lane >= DT_LANE) & (lane < DT_LANE + H)
        draw = jnp.where(head_lane, (dda * a + ddt_x) * _sigmoid(raw), 0.0)
        ddt_ref[...] = draw
        dbias_ref[...] += jnp.sum(draw, axis=0, keepdims=True)
        dalog_ref[...] += jnp.sum(jnp.where(head_lane, dda * dt, 0.0), axis=0, keepdims=True) * a
        dd_ref[...] += dd_row

    rev = lambda c: nc - 1 - c
    row = lambda w: pl.BlockSpec((L, w), lambda c: (rev(c), 0))
    small = lambda shp: pl.BlockSpec(shp, lambda c: (0, 0))
    acc = pl.BlockSpec((1, LANES), lambda c: (0, 0))
    vec = jax.ShapeDtypeStruct((1, LANES), F32)
    return _call(body, name=name,
                 out_shape=[jax.ShapeDtypeStruct((s, SSM_CONV_DIM), F32), jax.ShapeDtypeStruct((s, LANES), F32), vec, vec, vec],
                 grid=(nc,),
                 in_specs=[row(SSM_CONV_DIM), row(LANES), pl.BlockSpec((H, L), lambda c: (0, rev(c))), small((1, LANES)),
                           small((H, 1)), small((1, LANES)), small((H, 1)), small((1, LANES)),
                           pl.BlockSpec((None, H, N, P), lambda c: (rev(c), 0, 0, 0)), row(SSM_INNER)],
                 out_specs=[row(SSM_CONV_DIM), row(LANES), acc, acc, acc],
                 scratch=[pltpu.VMEM((H, N, P), F32)], sem=("arbitrary",))(
        xbc, *_ssd_inputs(tail, dt_bias, a_log, d_skip), hprev_all, dy)


def _heads(x2d, n, d):
    s = x2d.shape[0]
    return x2d.reshape(s, n, d).transpose(1, 0, 2)


def _unheads(x3d):
    n, s, d = x3d.shape
    return x3d.transpose(1, 0, 2).reshape(s, n * d)


def layer_fwd(h, p, tabs, li, gather_late=None):
    s = h.shape[0]
    tabq, tabt = tabs
    nm = lambda t: f"L{li}_{t}"
    r = {'h': h}
    hn = rms_fwd(h, p['mix_norm'], name=nm('mixnorm'), out_dtype=MXU_DTYPE)
    proj = mm(hn, p['w_in'], name=nm('proj'))
    r.update(hn=hn, proj=proj)
    qkv = proj[:, :3 * SB_WIDTH].astype(MXU_DTYPE)
    late = None
    if gather_late is None:
        ya, tot = sb_fwd(qkv, name=nm('sb_fwd'))
    else:
        ya, tot, *got = sb_fwd(qkv, name=nm('sb_fwd'), carried=Carried('gather', [gather_late[n] for n in LATE]))
        late = assemble_late(dict(zip(LATE, got)))
        p = dict(p, **{n: late[n][li] for n in LATE})
    yan = rms_fwd(ya, p['sb_out_norm'], name=nm('sbnorm'), out_dtype=MXU_DTYPE)
    r.update(qkv=qkv, ya=ya, tot=tot)
    z = proj[:, 768:1280]
    xbc = proj[None, :, 1280:2048]
    tail = proj[:, TAIL:TAIL + LANES]
    xbc_act = conv_silu_fwd(xbc, p['ssm_conv_w'], p['ssm_conv_b'], name=nm('ssmconv'))[0]
    y_ssm, hprev = ssd_fwd(xbc_act, tail, p['ssm_dt_bias'], p['ssm_a_log'], p['ssm_d'], name=nm('ssd_fwd'))
    ybn = rms_fwd(y_ssm, p['ssm_out_norm'], name=nm('ssmnorm'), gate=z, out_dtype=MXU_DTYPE)
    r.update(z=z, xbc=xbc, tail=tail, xbc_act=xbc_act, y_ssm=y_ssm, hprev=hprev)
    cq = proj[:, 2048:2304]
    ckv = proj[:, 2304:2432]
    qn = rms_fwd(cq, p['mla_q_norm'], name=nm('qnorm'), out_dtype=MXU_DTYPE)
    q_r = rope(mm(qn, p['mla_w_uq'], name=nm('uq'))[None], tabq, name=nm('ropeq'))
    kvn = rms_fwd(ckv, p['mla_kv_norm'], name=nm('kvnorm'), out_dtype=MXU_DTYPE)
    kv = mm(kvn, p['mla_w_ukv'], name=nm('ukv'))
    k_pe = rope(tail[None], tabt, name=nm('ropek'))[:, :MLA_ROPE]
    qh = _heads(q_r, MLA_HEADS, MLA_QK).astype(MXU_DTYPE)
    kvh = _heads(kv, MLA_HEADS, MLA_NOPE + MLA_V)
    kh = jnp.concatenate([kvh[..., :MLA_NOPE], jnp.broadcast_to(k_pe[None], (MLA_HEADS, s, MLA_ROPE))],
                         axis=-1).astype(MXU_DTYPE)
    vh = kvh[..., MLA_NOPE:].astype(MXU_DTYPE)
    yc_h, lse = mla_fwd(qh, kh, vh, name=nm('mla_fwd'))
    yc = _unheads(yc_h)
    ycn = rms_fwd(yc, p['mla_out_norm'], name=nm('mlanorm'), out_dtype=MXU_DTYPE)
    r.update(cq=cq, ckv=ckv, qn=qn, kvn=kvn, qh=qh, kh=kh, vh=vh, yc_h=yc_h, yc=yc, lse=lse)
    ycat = jnp.concatenate([yan, ybn, ycn], axis=1)
    h1 = mm(ycat, p['w_out'], name=nm('outproj'), res=h)
    hn2 = rms_fwd(h1, p['ffn_norm'], name=nm('ffnnorm'), out_dtype=MXU_DTYPE)
    up = mm(hn2, p['ffn_w_up'], name=nm('up'), bb='o')
    act = conv_glu_fwd(up, p['ffn_conv_w'], p['ffn_conv_b'], name=nm('glu'))
    h2 = mm(act, p['ffn_w_down'], name=nm('down'), ab='k', bb='k', res=h1)
    r.update(ycat=ycat, h1=h1, hn2=hn2, up=up, act=act)
    return h2, r, p, late


def layer_bwd(dh2, p, r, tabs, li, scatter_late=None):
    s = dh2.shape[0]
    tabq, tabt = tabs
    nm = lambda t: f"L{li}_{t}"
    g = {}
    dact = mm(dh2, p['ffn_w_down'], name=nm('d_down_x'), tb=True, bb='o')
    g['ffn_w_down'] = mm(r['act'], dh2, name=nm('d_down_w'), out_dtype=WIRE_DTYPE, ta=True, ab='o')
    du, g['ffn_conv_w'], g['ffn_conv_b'] = conv_glu_bwd(r['up'], dact, p['ffn_conv_w'], p['ffn_conv_b'], name=nm('d_glu'))
    dup = conv_t(du, p['ffn_conv_w'], name=nm('d_ffnconv'))
    g['ffn_w_up'] = mm(r['hn2'], dup, name=nm('d_up_w'), out_dtype=WIRE_DTYPE, ta=True, bb='o')
    dhn2 = mm(dup, p['ffn_w_up'], name=nm('d_up_x'), tb=True, ab='k', bb='k')
    dh1, dg = rms_bwd(r['h1'], p['ffn_norm'], dhn2, name=nm('d_ffnnorm'), add=dh2)
    g['ffn_norm'] = dg[0]
    dycat = mm(dh1, p['w_out'], name=nm('d_out_x'), tb=True)
    g['w_out'] = mm(r['ycat'], dh1, name=nm('d_out_w'), out_dtype=WIRE_DTYPE, ta=True)
    dya, dg = rms_bwd(r['ya'], p['sb_out_norm'], dycat[:, :256], name=nm('d_sbnorm'))
    g['sb_out_norm'] = dg[0]
    recv_late = None
    if scatter_late is None:
        dq, dk, dv = sb_bwd(r['qkv'], dya, r['tot'], name=nm('sb_bwd'))
    else:
        parts = owner_parts_late([g] + list(scatter_late))
        dq, dk, dv, *got = sb_bwd(r['qkv'], dya, r['tot'], name=nm('sb_bwd'),
                                  carried=Carried('scatter', [parts[n].astype(WIRE_DTYPE) for n in LATE]))
        recv_late = dict(zip(LATE, got))
    dyssm, dz, dg = rms_bwd(r['y_ssm'], p['ssm_out_norm'], dycat[:, 256:768], name=nm('d_ssmnorm'), gate=r['z'])
    g['ssm_out_norm'] = dg[0]
    dxbc_act, ddt_tail, dbias, dalog, dd = ssd_bwd(r['xbc_act'], r['tail'], p['ssm_dt_bias'], p['ssm_a_log'],
                                                   p['ssm_d'], r['hprev'], dyssm, name=nm('ssd_bwd'))
    hl = slice(DT_LANE, DT_LANE + SSM_HEADS)
    g['ssm_dt_bias'], g['ssm_a_log'], g['ssm_d'] = dbias[0, hl], dalog[0, hl], dd[0, hl]
    dxbc_u, g['ssm_conv_w'], g['ssm_conv_b'] = conv_silu_bwd(r['xbc'], dxbc_act[None], p['ssm_conv_w'], p['ssm_conv_b'],
                                                             name=nm('d_ssmact'))
    dxbc = conv_t(dxbc_u, p['ssm_conv_w'], name=nm('d_ssmconv'))[0]
    dyc, dg = rms_bwd(r['yc'], p['mla_out_norm'], dycat[:, 768:], name=nm('d_mlanorm'))
    g['mla_out_norm'] = dg[0]
    dqh, dkh, dvh = mla_bwd(r['qh'], r['kh'], r['vh'], r['yc_h'], _heads(dyc, MLA_HEADS, MLA_V), r['lse'], name=nm('mla_bwd'))
    dq_c = rope(_unheads(dqh)[None], tabq, name=nm('d_ropeq'), backward=True)
    g['mla_w_uq'] = mm(r['qn'], dq_c, name=nm('d_uq_w'), out_dtype=WIRE_DTYPE, ta=True)
    dcq, dg = rms_bwd(r['cq'], p['mla_q_norm'], mm(dq_c, p['mla_w_uq'], name=nm('d_uq_x'), tb=True), name=nm('d_qnorm'))
    g['mla_q_norm'] = dg[0]
    dkv = _unheads(jnp.concatenate([dkh[..., :MLA_NOPE], dvh], axis=-1))
    g['mla_w_ukv'] = mm(r['kvn'], dkv, name=nm('d_ukv_w'), out_dtype=WIRE_DTYPE, ta=True)
    dckv, dg = rms_bwd(r['ckv'], p['mla_kv_norm'], mm(dkv, p['mla_w_ukv'], name=nm('d_ukv_x'), tb=True), name=nm('d_kvnorm'))
    g['mla_kv_norm'] = dg[0]
    dkpe = jnp.pad(dkh[..., MLA_NOPE:], ((0, 0), (0, 0), (0, LANES - MLA_ROPE)))
    dtail = rope(dkpe, tabt, name=nm('d_ropek'), backward=True, add=ddt_tail)
    dproj = jnp.concatenate([dq, dk, dv, dz, dxbc, dcq, dckv, dtail], axis=1)
    g['w_in'] = mm(r['hn'], dproj, name=nm('d_proj_w'), out_dtype=WIRE_DTYPE, ta=True)
    dhn = mm(dproj, p['w_in'], name=nm('d_proj_x'), tb=True)
    dh, dg = rms_bwd(r['h'], p['mix_norm'], dhn, name=nm('d_mixnorm'), add=dh1)
    g['mix_norm'] = dg[0]
    return dh, g, recv_late


def _w_in_placement():
    c = np.arange(D_IN)
    dest = np.where(c < 2048, c, np.where(c < 2056, c + (D_IN - 2056), c - 8))
    dest = jnp.asarray(dest.reshape(N_DEV, D_IN // N_DEV, 1), jnp.int32)
    return (dest == jnp.arange(D_IN_PAD, dtype=jnp.int32)).astype(MXU_DTYPE)


def _owner_major(full, axis):
    shp = full.shape
    return jnp.moveaxis(full.reshape(shp[:axis] + (N_DEV, shp[axis] // N_DEV) + shp[axis + 1:]), axis, 0)


def _owner_join(parts, axis):
    moved = jnp.moveaxis(parts, 0, axis)
    shp = moved.shape
    return moved.reshape(shp[:axis] + (shp[axis] * shp[axis + 1],) + shp[axis + 2:])


def assemble_early(gathered, replicated):
    L = DEPTH
    out = dict(replicated)
    out['w_in'] = mm(gathered['w_in'].reshape(N_DEV, L * D_MODEL, D_IN // N_DEV), _w_in_placement(), name='place_w_in',
                     ab='k', bb='k', out_dtype=MXU_DTYPE).reshape(L, D_MODEL, D_IN_PAD)
    out['mla_w_uq'] = _owner_join(gathered['mla_w_uq'], 2)
    out['mla_w_ukv'] = _owner_join(gathered['mla_w_ukv'], 2)
    out['ssm_conv_w'] = _owner_join(gathered['ssm_conv_w'], 2)[:, None]
    out['ssm_conv_b'] = replicated['ssm_conv_b'].reshape(L, 1, 1, SSM_CONV_DIM)
    out['ffn_conv_b'] = replicated['ffn_conv_b'].reshape(L, N_DEV, 1, FF_SHARD)
    return out


def assemble_late(gathered):
    L = DEPTH
    return {'ffn_w_up': jnp.moveaxis(gathered['ffn_w_up'], 1, 0),
            'w_out': _owner_join(gathered['w_out'], 1),
            'ffn_w_down': _owner_join(gathered['ffn_w_down'], 1).reshape(L, N_DEV // 2, FF_SHARD, D_MODEL),
            'ffn_conv_w': jnp.moveaxis(gathered['ffn_conv_w'], 1, 0)}


def owner_parts_late(grads):
    L = DEPTH
    st = lambda n: jnp.stack([g[n] for g in grads])
    return {'ffn_w_up': jnp.moveaxis(st('ffn_w_up'), 1, 0),
            'w_out': _owner_major(st('w_out'), 1),
            'ffn_w_down': _owner_major(st('ffn_w_down').reshape(L, D_FF, D_MODEL), 1),
            'ffn_conv_w': jnp.moveaxis(st('ffn_conv_w'), 1, 0)}


def owner_parts_early(grads):
    L = DEPTH
    st = lambda n: jnp.stack([g[n] for g in grads])
    parts = {
        'w_in': mm(st('w_in').reshape(L * D_MODEL, D_IN_PAD), _w_in_placement(), name='unplace_w_in', tb=True, bb='o',
                   out_dtype=WIRE_DTYPE).reshape(N_DEV, L, D_MODEL, D_IN // N_DEV),
        'mla_w_uq': _owner_major(st('mla_w_uq'), 2),
        'mla_w_ukv': _owner_major(st('mla_w_ukv'), 2),
        'ssm_conv_w': _owner_major(st('ssm_conv_w')[:, 0], 2),
    }
    rep = {n: st(n) for n in REPLICATED if n not in ('final_norm', 'ssm_conv_b', 'ffn_conv_b')}
    rep['ssm_conv_b'] = st('ssm_conv_b').reshape(L, SSM_CONV_DIM)
    rep['ffn_conv_b'] = st('ffn_conv_b').reshape(L, 2 * D_FF)
    return parts, rep


def local_step(x, positions, target, early, late_shards, replicated):
    s = x.shape[0]
    tabs = _rope_tables(positions, s)
    params = assemble_early(early, replicated)
    layer = lambda li: {n: params[n][li] for n in params if n != 'final_norm'}
    h, r0, p0, late = layer_fwd(x, layer(0), tabs, 0, gather_late=late_shards)
    saved = [(p0, r0)]
    for li in range(1, DEPTH):
        h, r, p, _ = layer_fwd(h, dict(layer(li), **{n: late[n][li] for n in LATE}), tabs, li)
        saved.append((p, r))
    y = rms_fwd(h, params['final_norm'], name='finalnorm')
    dy, loss = loss_head(y, target, name='loss')
    dh, dg = rms_bwd(h, params['final_norm'], dy, name='d_finalnorm')
    above = []
    for li in reversed(range(1, DEPTH)):
        dh, g, _ = layer_bwd(dh, *saved[li], tabs, li)
        above.insert(0, g)
    dh, g0, recv_late = layer_bwd(dh, *saved[0], tabs, 0, scatter_late=above)
    parts, rep = owner_parts_early([g0] + above)
    rep['final_norm'] = dg[0]
    return loss[0, 0], dh, parts, recv_late, rep


def all_gather(blocks, *, name):
    n = len(blocks)

    def body(*refs):
        x_refs, out_refs = refs[:n], refs[n:2 * n]
        send_sems, recv_sems, local_sems = refs[2 * n:]
        x, y, c = lax.axis_index("x"), lax.axis_index("y"), lax.axis_index("c")
        me, sibling = (x, y, c), (x, y, 1 - c)
        chips = [(1 - x, y), (x, 1 - y), (1 - x, 1 - y)]

        def slot(b, px, py, pc):
            return out_refs[b].at[4 * px + 2 * py + pc]

        def copy(b, k, blk, to, src=None):
            return pltpu.make_async_remote_copy(src_ref=slot(b, *blk) if src is None else src, dst_ref=slot(b, *blk),
                                                send_sem=send_sems.at[b, k], recv_sem=recv_sems.at[b, k],
                                                device_id=to, device_id_type=MESH)

        mine = [pltpu.make_async_copy(x_refs[b], slot(b, *me), local_sems.at[b]) for b in range(n)]
        for cp in mine:
            cp.start()
        first = []
        for b in range(n):
            first.append(copy(b, 0, me, sibling, src=x_refs[b]))
            first += [copy(b, 1 + j, me, (*chip, c), src=x_refs[b]) for j, chip in enumerate(chips)]
        for cp in first:
            cp.start()
        passed = []
        for j, chip in enumerate(chips):
            for b in range(n):
                copy(b, 1 + j, (*chip, c), me).wait_recv()
                fwd = copy(b, 4 + j, (*chip, c), sibling)
                fwd.start()
                passed.append(fwd)
        for b in range(n):
            copy(b, 0, sibling, me).wait_recv()
            for j, chip in enumerate(chips):
                copy(b, 4 + j, (*chip, 1 - c), me).wait_recv()
        for cp in first + passed:
            cp.wait_send()
        for cp in mine:
            cp.wait()

    return pl.pallas_call(
        body, name=name, out_shape=[jax.ShapeDtypeStruct((N_DEV,) + b.shape, b.dtype) for b in blocks],
        in_specs=[HBM] * n, out_specs=[HBM] * n,
        scratch_shapes=[pltpu.SemaphoreType.DMA((n, 7)), pltpu.SemaphoreType.DMA((n, 7)), pltpu.SemaphoreType.DMA((n,))],
    )(*blocks)


def all_to_all(parts, *, name):
    n = len(parts)

    def body(*refs):
        g_refs, r_refs = refs[:n], refs[n:2 * n]
        send_sems, recv_sems, local_sems = refs[2 * n:]
        x, y, c = lax.axis_index("x"), lax.axis_index("y"), lax.axis_index("c")
        me = 4 * x + 2 * y + c
        mine = [pltpu.make_async_copy(g_refs[b].at[me], r_refs[b].at[me], local_sems.at[b]) for b in range(n)]
        for cp in mine:
            cp.start()
        copies = []
        for k in range(1, N_DEV):
            px, py, pc = _flip(x, k & 4), _flip(y, k & 2), _flip(c, k & 1)
            peer = 4 * px + 2 * py + pc
            for b in range(n):
                cp = pltpu.make_async_remote_copy(src_ref=g_refs[b].at[peer], dst_ref=r_refs[b].at[me],
                                                  send_sem=send_sems.at[b, k - 1], recv_sem=recv_sems.at[b, k - 1],
                                                  device_id=(px, py, pc), device_id_type=MESH)
                cp.start()
                copies.append(cp)
        for cp in copies:
            cp.wait_send()
            cp.wait_recv()
        for cp in mine:
            cp.wait()

    return pl.pallas_call(
        body, name=name, out_shape=[jax.ShapeDtypeStruct(p.shape, p.dtype) for p in parts],
        in_specs=[HBM] * n, out_specs=[HBM] * n,
        scratch_shapes=[pltpu.SemaphoreType.DMA((n, 7)), pltpu.SemaphoreType.DMA((n, 7)), pltpu.SemaphoreType.DMA((n,))],
    )(*parts)


def adamw(parts, w, m, v, *, name):
    r, wd = w.shape
    br = _tile(r, (256, 128, 64, 32, 16, 8))
    c1 = 1.0 - ADAM_B1 ** ADAM_STEP
    c2 = 1.0 - ADAM_B2 ** ADAM_STEP

    def body(p_ref, w_ref, m_ref, v_ref, g_ref, d_ref, mo_ref, vo_ref):
        g = p_ref[0].astype(F32)
        for j in range(1, N_DEV):
            g = g + p_ref[j].astype(F32)
        mn = ADAM_B1 * m_ref[...] + (1.0 - ADAM_B1) * g
        vn = ADAM_B2 * v_ref[...] + (1.0 - ADAM_B2) * (g * g)
        g_ref[...] = g
        mo_ref[...] = mn
        vo_ref[...] = vn
        d_ref[...] = -ADAM_LR * ((mn / c1) / (jnp.sqrt(vn / c2) + ADAM_EPS) + ADAM_WD * w_ref[...])

    blk = pl.BlockSpec((br, wd), lambda i: (i, 0))
    out = jax.ShapeDtypeStruct((r, wd), F32)
    return _call(body, name=name, out_shape=[out] * 4, grid=(r // br,),
                 in_specs=[pl.BlockSpec((N_DEV, br, wd), lambda i: (0, i, 0)), blk, blk, blk], out_specs=[blk] * 4,
                 sem=("parallel",))(parts, w, m, v)


def _pack(arrs):
    flat = jnp.concatenate([a.reshape(-1) for a in arrs])
    rows = -(-flat.shape[0] // (8 * FLAT_W)) * 8
    return jnp.pad(flat, (0, rows * FLAT_W - flat.shape[0])).reshape(rows, FLAT_W)


def _unpack(flat, shapes):
    flat = flat.reshape(-1)
    out, off = [], 0
    for shp in shapes:
        n = int(np.prod(shp))
        out.append(flat[off:off + n].reshape(shp))
        off += n
    return out


def kernel(x, positions, mix_norm, w_in, sb_out_norm, ssm_conv_w, ssm_conv_b, ssm_dt_bias, ssm_a_log, ssm_d, ssm_out_norm, mla_q_norm, mla_w_uq, mla_kv_norm, mla_w_ukv, mla_out_norm, w_out, ffn_norm, ffn_w_up, ffn_conv_w, ffn_conv_b, ffn_w_down, final_norm, loss_target, m_mix_norm, m_w_in, m_sb_out_norm, m_ssm_conv_w, m_ssm_conv_b, m_ssm_dt_bias, m_ssm_a_log, m_ssm_d, m_ssm_out_norm, m_mla_q_norm, m_mla_w_uq, m_mla_kv_norm, m_mla_w_ukv, m_mla_out_norm, m_w_out, m_ffn_norm, m_ffn_w_up, m_ffn_conv_w, m_ffn_conv_b, m_ffn_w_down, m_final_norm, v_mix_norm, v_w_in, v_sb_out_norm, v_ssm_conv_w, v_ssm_conv_b, v_ssm_dt_bias, v_ssm_a_log, v_ssm_d, v_ssm_out_norm, v_mla_q_norm, v_mla_w_uq, v_mla_kv_norm, v_mla_w_ukv, v_mla_out_norm, v_w_out, v_ffn_norm, v_ffn_w_up, v_ffn_conv_w, v_ffn_conv_b, v_ffn_w_down, v_final_norm):
    args = locals()
    w = {n: args[n] for n in WEIGHTS}
    m = {n: args['m_' + n] for n in WEIGHTS}
    v = {n: args['v_' + n] for n in WEIGHTS}
    wire = lambda n: w[n] if n in VPU_WEIGHTS else w[n].astype(BF16)
    early = dict(zip(EARLY, all_gather([wire(n) for n in EARLY], name='gather_early')))

    loss, dx, parts, recv, rep = local_step(x[0], positions[0], loss_target[0], early, {n: wire(n) for n in LATE},
                                            {n: w[n] for n in REPLICATED})
    loss = lax.psum(loss, ("x", "y", "c"))

    recv.update(zip(EARLY, all_to_all([parts[n].astype(WIRE_DTYPE) for n in EARLY], name='scatter_early')))
    res = {kind: {} for kind in 'gdmv'}
    for n, rv in recv.items():
        shp = w[n].shape
        two_d = (int(np.prod(shp[:-1])), shp[-1])
        outs = adamw(rv.reshape((N_DEV,) + two_d), w[n].reshape(two_d), m[n].reshape(two_d), v[n].reshape(two_d),
                     name='adamw_' + n)
        for kind, o in zip('gdmv', outs):
            res[kind][n] = o.reshape(shp)

    rep_shapes = [w[n].shape for n in REPLICATED]
    (rparts,) = all_gather([_pack([rep[n] for n in REPLICATED])], name='gather_small_grads')
    rflat = lambda d: _pack([d[n] for n in REPLICATED])
    routs = adamw(rparts, rflat(w), rflat(m), rflat(v), name='adamw_replicated')
    for kind, o in zip('gdmv', routs):
        res[kind].update(zip(REPLICATED, _unpack(o, rep_shapes)))

    return (loss, dx[None], *[res['g'][n] for n in WEIGHTS], *[res['d'][n] for n in WEIGHTS],
            *[res['m'][n] for n in WEIGHTS], *[res['v'][n] for n in WEIGHTS])
```

```python
import numpy as np
import jax
import jax.numpy as jnp
from jax import lax
from jax.experimental import pallas as pl
from jax.experimental.pallas import tpu as pltpu

F32 = jnp.float32
BF16 = jnp.bfloat16
MXU_DTYPE = jnp.bfloat16
HIGHEST = lax.Precision.HIGHEST
WIRE_DTYPE = jnp.bfloat16

N_DEV = 8
D_MODEL = 1024
DEPTH = 2
EPS = 1e-6
SB_HEADS, SB_DIM = 4, 64
SB_WIDTH = SB_HEADS * SB_DIM
SSM_HEADS, SSM_P, SSM_GROUPS, SSM_N, SSM_CONV, SSM_CHUNK = 8, 64, 2, 64, 4, 128
SSM_INNER = SSM_HEADS * SSM_P
SSM_CONV_DIM = SSM_INNER + 2 * SSM_GROUPS * SSM_N
MLA_HEADS, MLA_NOPE, MLA_ROPE, MLA_V, MLA_Q_RANK, MLA_KV_RANK = 4, 64, 32, 64, 256, 128
MLA_QK = MLA_NOPE + MLA_ROPE
ROPE_THETA = 10000.0
D_IN = 2472
D_IN_PAD = 2560
TAIL = 2432
DT_LANE = 32
D_FF = 2816
FF_SHARD = 2 * D_FF // N_DEV
ADAM_LR, ADAM_B1, ADAM_B2, ADAM_EPS, ADAM_WD, ADAM_STEP = 0.001, 0.9, 0.999, 1e-08, 0.01, 10

LANES = 128
ATT_BLK = 256
SB_WIDE = 2
MLA_WIDE = 4
ROW_BLK = 512
CONV_COLS = 256
FLAT_W = 1024
VMEM_LIMIT = 56 << 20
MM_TM = (1024, 512, 256, 128)
MM_TN = (1280, 1024, 768, 640, 512, 384, 256, 128)
MM_TK = (1280, 1024, 512, 256, 128)

WEIGHTS = ['mix_norm', 'w_in', 'sb_out_norm', 'ssm_conv_w', 'ssm_conv_b', 'ssm_dt_bias', 'ssm_a_log', 'ssm_d',
           'ssm_out_norm', 'mla_q_norm', 'mla_w_uq', 'mla_kv_norm', 'mla_w_ukv', 'mla_out_norm', 'w_out',
           'ffn_norm', 'ffn_w_up', 'ffn_conv_w', 'ffn_conv_b', 'ffn_w_down', 'final_norm']
SHARDED = {'w_in': 2, 'ssm_conv_w': 2, 'mla_w_uq': 2, 'mla_w_ukv': 2, 'w_out': 1, 'ffn_w_up': 2, 'ffn_conv_w': 2,
           'ffn_w_down': 1}
VPU_WEIGHTS = ('ssm_conv_w', 'ffn_conv_w')
EARLY = ('w_in', 'mla_w_uq', 'mla_w_ukv', 'ssm_conv_w')
LATE = ('w_out', 'ffn_w_up', 'ffn_conv_w', 'ffn_w_down')
REPLICATED = [n for n in WEIGHTS if n not in SHARDED]


def _call(body, *, name, out_shape, grid=(), in_specs=None, out_specs=None, scratch=(), sem=None, **kw):
    params = dict(vmem_limit_bytes=VMEM_LIMIT)
    if sem is not None:
        params['dimension_semantics'] = sem
    return pl.pallas_call(body, name=name, out_shape=out_shape, grid=grid, in_specs=in_specs, out_specs=out_specs,
                          scratch_shapes=list(scratch), compiler_params=pltpu.CompilerParams(**params), **kw)


def _tile(n, prefs):
    for t in prefs:
        if n % t == 0:
            return t
    return n


def _dot(a, b, dims, precision=None):
    return lax.dot_general(a, b, (dims, ((), ())), preferred_element_type=F32, precision=precision)


def _nn(a, b, precision=None):
    return _dot(a, b, ((1,), (0,)), precision)


def _nt(a, b, precision=None):
    return _dot(a, b, ((1,), (1,)), precision)


def _tn(a, b, precision=None):
    return _dot(a, b, ((0,), (0,)), precision)


def _mxu(f):
    return lambda a, b: f(a.astype(MXU_DTYPE), b.astype(MXU_DTYPE))


_bnn, _bnt, _btn = _mxu(_nn), _mxu(_nt), _mxu(_tn)


def _split2(x):
    hi = x.astype(MXU_DTYPE)
    lo = (x - hi.astype(F32)).astype(MXU_DTYPE)
    return hi, lo


def _sigmoid(x):
    return 0.5 * jnp.tanh(0.5 * x) + 0.5


def _softplus(x):
    return jnp.maximum(x, 0.0) + jnp.log1p(jnp.exp(-jnp.abs(x)))


def _softplus_att(x):
    return jnp.maximum(x, 0.0) + jnp.log(1.0 + jnp.exp(-jnp.abs(x)))


def _cum(x, u):
    rows, b = x.shape[0], u.shape[0]
    n = x.shape[1] // b
    hi, lo = _split2(x)
    stack = [part[:, t * b:(t + 1) * b] for part in (hi, lo) for t in range(n)]
    r = _nn(jnp.concatenate(stack, axis=0), u)
    return jnp.concatenate([r[t * rows:(t + 1) * rows] + r[(n + t) * rows:(n + t + 1) * rows] for t in range(n)], axis=1)


def _causal_loop(qi, tile, carry, reverse, width):
    last = qi // width
    if reverse:
        return lax.fori_loop(0, last, lambda i, c: tile((last - 1 - i) * width, c, False), tile(last * width, carry, True))
    return tile(last * width, lax.fori_loop(0, last, lambda i, c: tile(i * width, c, False), carry), True)


def _causal_mask(blk, width, qi, kb, heads, strict, keys_on_rows=False):
    shape = (width * blk, blk) if keys_on_rows else (heads * blk, width * blk)
    q_idx = lax.broadcasted_iota(jnp.int32, shape, 1 if keys_on_rows else 0)
    k_idx = lax.broadcasted_iota(jnp.int32, shape, 0 if keys_on_rows else 1)
    if heads > 1:
        q_idx = q_idx % blk
    gap = (qi - kb) * blk
    return k_idx < q_idx + gap if strict else k_idx <= q_idx + gap


def mm(a, b, *, name, ta=False, tb=False, res=None, out_dtype=F32, ab=None, bb=None, precision=None):
    a2, b2 = a.shape[-2:], b.shape[-2:]
    (kdim, m) = a2 if ta else a2[::-1]
    (n, k2) = b2 if tb else b2[::-1]
    assert kdim == k2, (a.shape, b.shape, ta, tb)
    assert (ab == 'k') == (bb == 'k')
    kb = ab == 'k'
    nb = a.shape[0] if ab == 'o' else (b.shape[0] if bb == 'o' else None)
    tm, tn = _tile(m, MM_TM), _tile(n, MM_TN)
    tk = kdim if kb else _tile(kdim, MM_TK)
    nk = a.shape[0] if kb else kdim // tk
    dims = ((0 if ta else 1,), (1 if tb else 0,))
    op_dtype = F32 if precision is not None else MXU_DTYPE

    def body(*refs):
        a_ref, b_ref = refs[0], refs[1]
        r_ref = refs[2] if res is not None else None
        o_ref = refs[3] if res is not None else refs[2]
        part = _dot(a_ref[...].astype(op_dtype), b_ref[...].astype(op_dtype), dims, precision)

        def finish(out):
            if res is not None:
                out = out + r_ref[...]
            o_ref[...] = out.astype(out_dtype)

        if nk == 1:
            finish(part)
            return
        acc = refs[-1]
        k = pl.program_id(3)

        @pl.when(k == 0)
        def _():
            acc[...] = part

        @pl.when(k > 0)
        def _():
            acc[...] += part

        @pl.when(k == nk - 1)
        def _():
            finish(acc[...])

    def spec(blk, idx, how):
        if how is None:
            return pl.BlockSpec(blk, idx)
        if how == 'o':
            return pl.BlockSpec((None,) + blk, lambda p, i, j, k: (p,) + idx(p, i, j, k))
        return pl.BlockSpec((None,) + blk, lambda p, i, j, k: (k,) + idx(p, i, j, 0))

    a_spec = spec((tk, tm), lambda p, i, j, k: (k, i), ab) if ta else spec((tm, tk), lambda p, i, j, k: (i, k), ab)
    b_spec = spec((tn, tk), lambda p, i, j, k: (j, k), bb) if tb else spec((tk, tn), lambda p, i, j, k: (k, j), bb)
    o_spec = spec((tm, tn), lambda p, i, j, k: (i, j), None if nb is None else 'o')
    ins, specs = [a, b], [a_spec, b_spec]
    if res is not None:
        ins.append(res)
        specs.append(o_spec)
    out_shape = (m, n) if nb is None else (nb, m, n)
    return _call(body, name=name, out_shape=jax.ShapeDtypeStruct(out_shape, out_dtype),
                 grid=(1 if nb is None else nb, m // tm, n // tn, nk), in_specs=specs, out_specs=o_spec,
                 scratch=[] if nk == 1 else [pltpu.VMEM((tm, tn), F32)],
                 sem=("parallel", "parallel", "parallel", "arbitrary"))(*ins)


def rms_fwd(x, g, *, name, gate=None, out_dtype=F32):
    s, w = x.shape
    bs = _tile(s, (ROW_BLK,))

    def body(*refs):
        if gate is None:
            x_ref, g_ref, o_ref = refs
            u = x_ref[...]
        else:
            x_ref, z_ref, g_ref, o_ref = refs
            z = z_ref[...]
            u = x_ref[...] * (z * _sigmoid(z))
        r = lax.rsqrt(jnp.mean(u * u, axis=1, keepdims=True) + EPS)
        o_ref[...] = (u * r * g_ref[...]).astype(out_dtype)

    row = pl.BlockSpec((bs, w), lambda i: (i, 0))
    vec = pl.BlockSpec((1, w), lambda i: (0, 0))
    ins = [x] + ([] if gate is None else [gate]) + [g.reshape(1, w)]
    specs = [row] + ([] if gate is None else [row]) + [vec]
    return _call(body, name=name, out_shape=jax.ShapeDtypeStruct((s, w), out_dtype), grid=(s // bs,),
                 in_specs=specs, out_specs=row, sem=("parallel",))(*ins)


def rms_bwd(x, g, dy, *, name, gate=None, add=None):
    s, w = x.shape
    bs = _tile(s, (ROW_BLK,))

    def body(*refs):
        refs = list(refs)
        x_ref = refs.pop(0)
        z_ref = refs.pop(0) if gate is not None else None
        g_ref = refs.pop(0)
        dy_ref = refs.pop(0)
        add_ref = refs.pop(0) if add is not None else None
        dx_ref = refs.pop(0)
        dz_ref = refs.pop(0) if gate is not None else None
        dg_ref = refs.pop(0)
        i = pl.program_id(0)

        @pl.when(i == 0)
        def _():
            dg_ref[...] = jnp.zeros_like(dg_ref)

        xv = x_ref[...]
        if gate is not None:
            z = z_ref[...]
            sg = _sigmoid(z)
            act = z * sg
            u = xv * act
        else:
            u = xv
        r = lax.rsqrt(jnp.mean(u * u, axis=1, keepdims=True) + EPS)
        dy_v = dy_ref[...]
        dyg = dy_v * g_ref[...]
        du = r * dyg - u * (r * r * r * jnp.mean(dyg * u, axis=1, keepdims=True))
        dg_ref[...] += jnp.sum(dy_v * u * r, axis=0, keepdims=True)
        if gate is not None:
            dx = du * act
            dz_ref[...] = du * xv * (sg * (1.0 + z * (1.0 - sg)))
        else:
            dx = du
        if add is not None:
            dx = dx + add_ref[...]
        dx_ref[...] = dx

    row = pl.BlockSpec((bs, w), lambda i: (i, 0))
    vec = pl.BlockSpec((1, w), lambda i: (0, 0))
    ins = [x] + ([] if gate is None else [gate]) + [g.reshape(1, w), dy] + ([] if add is None else [add])
    specs = [row] + ([] if gate is None else [row]) + [vec, row] + ([] if add is None else [row])
    outs = [jax.ShapeDtypeStruct((s, w), F32)] + ([] if gate is None else [jax.ShapeDtypeStruct((s, w), F32)])
    outs.append(jax.ShapeDtypeStruct((1, w), F32))
    ospecs = [row] + ([] if gate is None else [row]) + [vec]
    return _call(body, name=name, out_shape=outs, grid=(s // bs,), in_specs=specs, out_specs=ospecs,
                 sem=("arbitrary",))(*ins)


def loss_head(y, target, *, name):
    s, w = y.shape
    bs = _tile(s, (ROW_BLK,))
    nb = s // bs

    def body(y_ref, t_ref, dy_ref, loss_ref, acc):
        i = pl.program_id(0)

        @pl.when(i == 0)
        def _():
            acc[...] = jnp.zeros_like(acc)

        e = y_ref[...] - t_ref[...]
        dy_ref[...] = e * (1.0 / w)
        acc[...] += jnp.sum(e * e, axis=0, keepdims=True)

        @pl.when(i == nb - 1)
        def _():
            loss_ref[...] = jnp.sum(acc[...], axis=1, keepdims=True) * (0.5 / w)

    row = pl.BlockSpec((bs, w), lambda i: (i, 0))
    return _call(body, name=name, out_shape=[jax.ShapeDtypeStruct((s, w), F32), jax.ShapeDtypeStruct((1, 1), F32)],
                 grid=(nb,), in_specs=[row, row], out_specs=[row, pl.BlockSpec((1, 1), lambda i: (0, 0))],
                 scratch=[pltpu.VMEM((1, w), F32)], sem=("arbitrary",))(y, target)


def _rope_tables(positions, s):
    inv_freq = 1.0 / (ROPE_THETA ** (jnp.arange(0, MLA_ROPE, 2, dtype=F32) / MLA_ROPE))
    ang = positions.reshape(s, 1).astype(F32) * inv_freq
    cos, sin = jnp.cos(ang), jnp.sin(ang)
    one, zero = jnp.ones((s, MLA_NOPE), F32), jnp.zeros((s, MLA_NOPE), F32)
    cq = jnp.tile(jnp.concatenate([one, cos, cos], axis=1), (1, MLA_HEADS))
    sq = jnp.tile(jnp.concatenate([zero, sin, sin], axis=1), (1, MLA_HEADS))
    pad1, pad0 = jnp.ones((s, LANES - MLA_ROPE), F32), jnp.zeros((s, LANES - MLA_ROPE), F32)
    ct = jnp.concatenate([cos, cos, pad1], axis=1)
    st = jnp.concatenate([sin, sin, pad0], axis=1)
    half = MLA_ROPE // 2

    def swap(width, starts):
        r = np.zeros((width, width), np.float32)
        for o in starts:
            for i in range(half):
                r[o + half + i, o + i] = -1.0
                r[o + i, o + half + i] = 1.0
        return jnp.asarray(r)

    rq = swap(MLA_HEADS * MLA_QK, [h * MLA_QK + MLA_NOPE for h in range(MLA_HEADS)])
    rt = swap(LANES, [0])
    return (cq, sq, rq), (ct, st, rt)


def rope(x, tabs, *, name, backward=False, add=None):
    cos, sin, rot = tabs
    n, s, w = x.shape
    bs = _tile(s, (ROW_BLK,))

    def body(*refs):
        if add is None:
            x_ref, c_ref, s_ref, r_ref, o_ref = refs
        else:
            x_ref, c_ref, s_ref, r_ref, a_ref, o_ref = refs
        xv = x_ref[0]
        for j in range(1, n):
            xv = xv + x_ref[j]
        if backward:
            out = xv * c_ref[...] + _nt(xv * s_ref[...], r_ref[...], HIGHEST)
        else:
            out = xv * c_ref[...] + _nn(xv, r_ref[...], HIGHEST) * s_ref[...]
        if add is not None:
            out = out + a_ref[...]
        o_ref[...] = out

    row = pl.BlockSpec((bs, w), lambda i: (i, 0))
    ins = [x, cos, sin, rot] + ([] if add is None else [add])
    specs = [pl.BlockSpec((n, bs, w), lambda i: (0, i, 0)), row, row, pl.BlockSpec((w, w), lambda i: (0, 0))]
    specs += [] if add is None else [row]
    return _call(body, name=name, out_shape=jax.ShapeDtypeStruct((s, w), F32), grid=(s // bs,), in_specs=specs,
                 out_specs=row, sem=("parallel",))(*ins)


MESH = pl.DeviceIdType.MESH
HBM = pl.BlockSpec(memory_space=pltpu.HBM)


def _flip(v, bit):
    return 1 - v if bit else v


class Carried:
    def __init__(self, kind, arrays):
        assert kind in ('gather', 'scatter')
        self.kind, self.arrays, self.n = kind, list(arrays), len(arrays)

    @property
    def out_shape(self):
        lead = (N_DEV,) if self.kind == 'gather' else ()
        return [jax.ShapeDtypeStruct(lead + a.shape, a.dtype) for a in self.arrays]

    @property
    def scratch(self):
        return [pltpu.SemaphoreType.DMA((self.n, N_DEV - 1)), pltpu.SemaphoreType.DMA((self.n, N_DEV - 1)),
                pltpu.SemaphoreType.DMA((self.n,))]

    def _copies(self, in_refs, out_refs, sems):
        send_sems, recv_sems, local_sems = sems
        x, y, c = lax.axis_index("x"), lax.axis_index("y"), lax.axis_index("c")
        me = 4 * x + 2 * y + c
        part = (lambda b, p: in_refs[b]) if self.kind == 'gather' else (lambda b, p: in_refs[b].at[p])
        local = [pltpu.make_async_copy(part(b, me), out_refs[b].at[me], local_sems.at[b]) for b in range(self.n)]
        remote = []
        for k in range(1, N_DEV):
            px, py, pc = _flip(x, k & 4), _flip(y, k & 2), _flip(c, k & 1)
            for b in range(self.n):
                remote.append(pltpu.make_async_remote_copy(
                    src_ref=part(b, 4 * px + 2 * py + pc), dst_ref=out_refs[b].at[me], send_sem=send_sems.at[b, k - 1],
                    recv_sem=recv_sems.at[b, k - 1], device_id=(px, py, pc), device_id_type=MESH))
        return local, remote

    def start(self, in_refs, out_refs, sems):
        local, remote = self._copies(in_refs, out_refs, sems)
        for cp in local + remote:
            cp.start()

    def wait(self, in_refs, out_refs, sems):
        local, remote = self._copies(in_refs, out_refs, sems)
        for cp in remote:
            cp.wait_send()
            cp.wait_recv()
        for cp in local:
            cp.wait()


def _ride(carried, refs, n_in, n_out):
    n = 0 if carried is None else carried.n
    own_in, ride_in = refs[:n_in], refs[n_in:n_in + n]
    own_out, ride_out = refs[n_in + n:n_in + n + n_out], refs[n_in + n + n_out:n_in + 2 * n + n_out]
    return own_in, own_out, (ride_in, ride_out, refs[n_in + 2 * n + n_out:])


def _tri(n, op):
    r = lax.broadcasted_iota(jnp.int32, (n, n), 0)
    c = lax.broadcasted_iota(jnp.int32, (n, n), 1)
    return r, c, op(r, c)


def _pair_split(x, first):
    zero = jnp.zeros_like(x)
    return jnp.where(first, x, zero), jnp.where(first, zero, x)


def _sb_specs(s, blk):
    npair = SB_WIDTH // LANES
    q = pl.BlockSpec((blk, LANES), lambda j, i: (i, j))
    k = pl.BlockSpec((s, LANES), lambda j, i: (0, npair + j))
    v = pl.BlockSpec((s, LANES), lambda j, i: (0, 2 * npair + j))
    full = pl.BlockSpec((s, LANES), lambda j, i: (0, j))
    return npair, q, k, v, full


def _stack_heads(x, first):
    return jnp.concatenate(_pair_split(x, first), axis=0)


def _unstack_heads(x, first, blk):
    return jnp.where(first, x[:blk], x[blk:])


def sb_fwd(qkv, *, name, carried=None):
    s = qkv.shape[0]
    blk = _tile(s, (ATT_BLK,))
    scale = SB_DIM ** -0.5
    npair, nq = SB_WIDTH // LANES, s // blk
    assert nq % SB_WIDE == 0

    def body(*refs):
        (q_ref, k_ref, v_ref), (y_ref, t_ref), ride = _ride(carried, refs, 3, 2)
        pair, qi = pl.program_id(0), pl.program_id(1)
        if carried is not None:
            @pl.when((pair == 0) & (qi == 0))
            def _():
                carried.start(*ride)

        first = lax.broadcasted_iota(jnp.int32, (blk, LANES), 1) < SB_DIM
        q2 = _stack_heads((q_ref[...].astype(F32) * scale).astype(MXU_DTYPE), first)
        row, col, later_mask = _tri(blk, lambda r, c: r > c)
        u_later = later_mask.astype(MXU_DTYPE)
        n = SB_WIDE

        def tile(kb, carry, masked):
            c, acc = carry
            keys = pl.ds(pl.multiple_of(kb * blk, blk), n * blk)
            z = _nt(q2, k_ref[keys, :])
            sp = _softplus_att(z)
            if masked:
                valid = _causal_mask(blk, n, qi, kb, 2, True)
            spm = jnp.where(valid, sp, 0.0) if masked else sp
            later = _cum(spm, u_later)
            sums = [jnp.sum(spm[:, t * blk:(t + 1) * blk], axis=1, keepdims=True) for t in range(n)]
            after, cols = c, [None] * n
            for t in reversed(range(n)):
                cols[t] = jnp.broadcast_to(after, (2 * blk, blk))
                after = after - sums[t]
            w = jnp.exp((z - sp) - later + (cols[0] if n == 1 else jnp.concatenate(cols, axis=1)))
            if masked:
                w = jnp.where(valid, w, 0.0)
            return after, acc + _nn(w.astype(MXU_DTYPE), v_ref[keys, :])

        zero = (jnp.zeros((2 * blk, 1), F32), jnp.zeros((2 * blk, LANES), F32))
        c, acc = _causal_loop(qi, tile, zero, True, SB_WIDE)
        y_ref[...] = _unstack_heads(acc, first, blk)
        t_ref[...] = _unstack_heads(c, first, blk)
        if carried is not None:
            @pl.when((pair == npair - 1) & (qi == nq - 1))
            def _():
                carried.wait(*ride)

    _, qspec, kspec, vspec, _ = _sb_specs(s, blk)
    out = jax.ShapeDtypeStruct((s, SB_WIDTH), F32)
    extra = [] if carried is None else carried.arrays
    return _call(body, name=name, out_shape=[out, out] + ([] if carried is None else carried.out_shape),
                 grid=(npair, nq), in_specs=[qspec, kspec, vspec] + [HBM] * len(extra),
                 out_specs=[qspec, qspec] + [HBM] * len(extra), scratch=[] if carried is None else carried.scratch,
                 sem=("arbitrary", "arbitrary"))(qkv, qkv, qkv, *extra)


def sb_bwd(qkv, dy, tot, *, name, carried=None):
    s = qkv.shape[0]
    blk = _tile(s, (ATT_BLK,))
    scale = SB_DIM ** -0.5
    npair, nq = SB_WIDTH // LANES, s // blk
    assert nq % SB_WIDE == 0

    def body(*refs):
        (q_ref, k_ref, v_ref, dy_ref, t_ref), (dq_ref, dk_ref, dv_ref), ride = _ride(carried, refs, 5, 3)
        pair, qi = pl.program_id(0), pl.program_id(1)
        if carried is not None:
            @pl.when((pair == 0) & (qi == 0))
            def _():
                carried.start(*ride)

        @pl.when(qi == 0)
        def _():
            dk_ref[...] = jnp.zeros_like(dk_ref)
            dv_ref[...] = jnp.zeros_like(dv_ref)

        first = lax.broadcasted_iota(jnp.int32, (blk, LANES), 1) < SB_DIM
        q2 = _stack_heads((q_ref[...].astype(F32) * scale).astype(MXU_DTYPE), first)
        dy2 = _stack_heads(dy_ref[...].astype(MXU_DTYPE), first)
        tv = jnp.concatenate([t_ref[:, 0:1], t_ref[:, SB_DIM:SB_DIM + 1]], axis=0)
        row, col, incl_mask = _tri(blk, lambda r, c: r <= c)
        u_incl = incl_mask.astype(MXU_DTYPE)
        u_excl = (row < col).astype(MXU_DTYPE)
        n = SB_WIDE

        def prefixed(x, carry):
            cols = []
            for t in range(n):
                cols.append(jnp.broadcast_to(carry, (2 * blk, blk)))
                carry = carry + jnp.sum(x[:, t * blk:(t + 1) * blk], axis=1, keepdims=True)
            return (cols[0] if n == 1 else jnp.concatenate(cols, axis=1)), carry

        def tile(kb, carry, masked):
            p, gc, dq = carry
            keys = pl.ds(pl.multiple_of(kb * blk, blk), n * blk)
            kv = k_ref[keys, :]
            z = _nt(q2, kv)
            dw = _nt(dy2, v_ref[keys, :])
            sp = _softplus_att(z)
            if masked:
                valid = _causal_mask(blk, n, qi, kb, 2, True)
            spm = jnp.where(valid, sp, 0.0) if masked else sp
            before, p = prefixed(spm, p)
            w = jnp.exp((z - sp) + (_cum(spm, u_incl) + before))
            if masked:
                w = jnp.where(valid, w, 0.0)
            g = w * dw
            gbefore, gc = prefixed(g, gc)
            gb = g.astype(MXU_DTYPE)
            gin = _nn(jnp.concatenate([gb[:, t * blk:(t + 1) * blk] for t in range(n)], axis=0), u_excl)
            gex = gbefore + jnp.concatenate([gin[t * 2 * blk:(t + 1) * 2 * blk] for t in range(n)], axis=1)
            keep = jnp.exp(-spm)
            dz = keep * (g + gex) - gex
            if masked:
                dz = jnp.where(valid, dz, 0.0)
            dzb = dz.astype(MXU_DTYPE)
            dk_ref[keys, :] += _tn(dzb, q2)
            dv_ref[keys, :] += _tn(w.astype(MXU_DTYPE), dy2)
            return p, gc, dq + _nn(dzb, kv)

        zero = jnp.zeros((2 * blk, 1), F32)
        _, _, dq = _causal_loop(qi, tile, (tv, zero, jnp.zeros((2 * blk, LANES), F32)), False, SB_WIDE)
        dq_ref[...] = _unstack_heads(dq, first, blk) * scale
        if carried is not None:
            @pl.when((pair == npair - 1) & (qi == nq - 1))
            def _():
                carried.wait(*ride)

    _, qspec, kspec, vspec, full = _sb_specs(s, blk)
    out = jax.ShapeDtypeStruct((s, SB_WIDTH), F32)
    extra = [] if carried is None else carried.arrays
    return _call(body, name=name, out_shape=[out, out, out] + ([] if carried is None else carried.out_shape),
                 grid=(npair, nq), in_specs=[qspec, kspec, vspec, qspec, qspec] + [HBM] * len(extra),
                 out_specs=[qspec, full, full] + [HBM] * len(extra), scratch=[] if carried is None else carried.scratch,
                 sem=("arbitrary", "arbitrary"))(qkv, qkv, qkv, dy, tot, *extra)


ATT_PAIR = 2


def _mla_specs(s, blk, dk, dv):
    q = pl.BlockSpec((ATT_PAIR, blk, dk), lambda hp, i: (hp, i, 0))
    k = pl.BlockSpec((ATT_PAIR, s, dk), lambda hp, i: (hp, 0, 0))
    v = pl.BlockSpec((ATT_PAIR, s, dv), lambda hp, i: (hp, 0, 0))
    y = pl.BlockSpec((ATT_PAIR, blk, dv), lambda hp, i: (hp, i, 0))
    lse = pl.BlockSpec((ATT_PAIR, blk, LANES), lambda hp, i: (hp, i, 0))
    return q, k, v, y, lse


def mla_fwd(q, k, v, *, name):
    h, s, dk = q.shape
    dv = v.shape[-1]
    blk = _tile(s, (ATT_BLK,))
    scale = dk ** -0.5
    assert (s // blk) % MLA_WIDE == 0

    def body(q_ref, k_ref, v_ref, y_ref, l_ref):
        qi = pl.program_id(1)
        n = MLA_WIDE

        def tile(kb, carry, masked):
            keys = pl.ds(pl.multiple_of(kb * blk, blk), n * blk)
            out = []
            for hh in range(ATT_PAIR):
                m, l, acc = carry[hh]
                sc = _nt(q_ref[hh], k_ref[hh, keys, :]) * scale
                if masked:
                    sc = jnp.where(_causal_mask(blk, n, qi, kb, 1, False), sc, -1e30)
                m2 = jnp.maximum(m, jnp.max(sc, axis=1, keepdims=True))
                p = jnp.exp(sc - m2)
                a = jnp.exp(m - m2)
                out.append((m2, a * l + jnp.sum(p, axis=1, keepdims=True),
                            a * acc + _nn(p.astype(MXU_DTYPE), v_ref[hh, keys, :])))
            return tuple(out)

        init = (jnp.full((blk, 1), -1e30, F32), jnp.zeros((blk, 1), F32), jnp.zeros((blk, dv), F32))
        for hh, (m, l, acc) in enumerate(_causal_loop(qi, tile, (init,) * ATT_PAIR, False, MLA_WIDE)):
            y_ref[hh] = acc / l
            l_ref[hh] = jnp.broadcast_to(m + jnp.log(l), (blk, LANES))

    qspec, kspec, vspec, yspec, lspec = _mla_specs(s, blk, dk, dv)
    return _call(body, name=name,
                 out_shape=[jax.ShapeDtypeStruct((h, s, dv), F32), jax.ShapeDtypeStruct((h, s, LANES), F32)],
                 grid=(h // ATT_PAIR, s // blk), in_specs=[qspec, kspec, vspec], out_specs=[yspec, lspec],
                 sem=("parallel", "arbitrary"))(q, k, v)


def mla_bwd(q, k, v, y, dy, lse, *, name):
    h, s, dk = q.shape
    dv = v.shape[-1]
    blk = _tile(s, (ATT_BLK,))
    scale = dk ** -0.5
    assert (s // blk) % MLA_WIDE == 0

    def body(q_ref, k_ref, v_ref, y_ref, dy_ref, l_ref, dq_ref, dk_ref, dv_ref):
        qi = pl.program_id(1)

        @pl.when(qi == 0)
        def _():
            dk_ref[...] = jnp.zeros_like(dk_ref)
            dv_ref[...] = jnp.zeros_like(dv_ref)

        as_row = lambda col: jnp.transpose(jnp.broadcast_to(col, (blk, LANES)))[0:1, :]
        dyv = [dy_ref[hh].astype(MXU_DTYPE) for hh in range(ATT_PAIR)]
        delta = [as_row(jnp.sum(dy_ref[hh] * y_ref[hh], axis=1, keepdims=True)) for hh in range(ATT_PAIR)]
        lv = [as_row(l_ref[hh, :, 0:1]) for hh in range(ATT_PAIR)]
        n = MLA_WIDE

        def tile(kb, dqs, masked):
            keys = pl.ds(pl.multiple_of(kb * blk, blk), n * blk)
            out = []
            for hh in range(ATT_PAIR):
                qv = q_ref[hh]
                kv = k_ref[hh, keys, :]
                p = jnp.exp(_nt(kv, qv) * scale - lv[hh])
                if masked:
                    p = jnp.where(_causal_mask(blk, n, qi, kb, 1, False, keys_on_rows=True), p, 0.0)
                ds = (p * (_nt(v_ref[hh, keys, :], dyv[hh]) - delta[hh])).astype(MXU_DTYPE)
                dk_ref[hh, keys, :] += _nn(ds, qv) * scale
                dv_ref[hh, keys, :] += _nn(p.astype(MXU_DTYPE), dyv[hh])
                out.append(dqs[hh] + _tn(ds, kv))
            return tuple(out)

        for hh, dq in enumerate(_causal_loop(qi, tile, (jnp.zeros((blk, dk), F32),) * ATT_PAIR, False, MLA_WIDE)):
            dq_ref[hh] = dq * scale

    qspec, kspec, vspec, yspec, lspec = _mla_specs(s, blk, dk, dv)
    return _call(body, name=name,
                 out_shape=[jax.ShapeDtypeStruct((h, s, dk), F32), jax.ShapeDtypeStruct((h, s, dk), F32),
                            jax.ShapeDtypeStruct((h, s, dv), F32)],
                 grid=(h // ATT_PAIR, s // blk), in_specs=[qspec, kspec, vspec, yspec, yspec, lspec],
                 out_specs=[qspec, kspec, vspec], sem=("parallel", "arbitrary"))(q, k, v, y, dy, lse)


HALO = 8
CONV_CHUNK = 16


def _conv_tiles(x):
    s, c = x.shape[-2:]
    return s, c, _tile(s, (ROW_BLK,)), _tile(c, (CONV_COLS,))


def _conv_specs(bs, cw, lead=()):
    zero = (0,) * len(lead)
    blk = pl.BlockSpec(lead + (None, bs, cw), lambda p, j, i: zero + (p, i, j))
    halo = pl.BlockSpec(lead + (None, HALO, cw), lambda p, j, i: zero + (p, jnp.maximum(i * (bs // HALO) - 1, 0), j))
    w = lambda kk: pl.BlockSpec(lead + (None, kk, cw), lambda p, j, i: zero + (p, 0, j))
    return blk, halo, w


def _stage(scr, x_ref, halo_ref, first):
    scr[0:HALO, :] = jnp.where(first, 0.0, halo_ref[...])
    scr[HALO:, :] = x_ref[...]


def _conv_taps(scr, kk, r0):
    return [scr[pl.ds(HALO - (kk - 1) + k + r0, CONV_CHUNK), :] for k in range(kk)]


def _conv_sum(taps, w_ref, b_ref):
    u = b_ref[...] + taps[0] * w_ref[0:1, :]
    for k in range(1, len(taps)):
        u = u + taps[k] * w_ref[k:k + 1, :]
    return u


def _fold(x):
    out = x[0:8]
    for r in range(8, CONV_CHUNK, 8):
        out = out + x[r:r + 8]
    return out


class _TapSums:
    def __init__(self, kk, cw):
        self.w = [jnp.zeros((8, cw), F32) for _ in range(kk)]
        self.b = jnp.zeros((8, cw), F32)

    def add(self, du, taps):
        self.w = [a + _fold(du * t) for a, t in zip(self.w, taps)]
        self.b = self.b + _fold(du)

    def flush(self, dw_ref, db_ref):
        for k, a in enumerate(self.w):
            dw_ref[k:k + 1, :] += jnp.sum(a, axis=0, keepdims=True)
        db_ref[...] += jnp.sum(self.b, axis=0, keepdims=True)


def _silu_grad(u):
    sg = _sigmoid(u)
    return sg * (1.0 + u * (1.0 - sg))


def conv_silu_fwd(x, w, b, *, name):
    s, c, bs, cw = _conv_tiles(x)
    kk = w.shape[1]

    def body(x_ref, h_ref, w_ref, b_ref, o_ref, scr):
        _stage(scr, x_ref, h_ref, pl.program_id(2) == 0)
        for r0 in range(0, bs, CONV_CHUNK):
            u = _conv_sum(_conv_taps(scr, kk, r0), w_ref, b_ref)
            o_ref[pl.ds(r0, CONV_CHUNK), :] = u * _sigmoid(u)

    blk, halo, wspec = _conv_specs(bs, cw)
    return _call(body, name=name, out_shape=jax.ShapeDtypeStruct(x.shape, F32), grid=(x.shape[0], c // cw, s // bs),
                 in_specs=[blk, halo, wspec(kk), wspec(1)], out_specs=blk, scratch=[pltpu.VMEM((bs + HALO, cw), F32)],
                 sem=("parallel", "parallel", "arbitrary"))(x, x, w, b)


def conv_silu_bwd(x, dy, w, b, *, name):
    s, c, bs, cw = _conv_tiles(x)
    kk = w.shape[1]

    def body(x_ref, h_ref, w_ref, b_ref, dy_ref, du_ref, dw_ref, db_ref, scr):
        i = pl.program_id(2)

        @pl.when(i == 0)
        def _():
            dw_ref[...] = jnp.zeros_like(dw_ref)
            db_ref[...] = jnp.zeros_like(db_ref)

        _stage(scr, x_ref, h_ref, i == 0)
        sums = _TapSums(kk, cw)
        for r0 in range(0, bs, CONV_CHUNK):
            taps = _conv_taps(scr, kk, r0)
            du = dy_ref[pl.ds(r0, CONV_CHUNK), :] * _silu_grad(_conv_sum(taps, w_ref, b_ref))
            du_ref[pl.ds(r0, CONV_CHUNK), :] = du
            sums.add(du, taps)
        sums.flush(dw_ref, db_ref)

    blk, halo, wspec = _conv_specs(bs, cw)
    return _call(body, name=name,
                 out_shape=[jax.ShapeDtypeStruct(x.shape, F32), jax.ShapeDtypeStruct(w.shape, F32),
                            jax.ShapeDtypeStruct(b.shape, F32)],
                 grid=(x.shape[0], c // cw, s // bs), in_specs=[blk, halo, wspec(kk), wspec(1), blk],
                 out_specs=[blk, wspec(kk), wspec(1)], scratch=[pltpu.VMEM((bs + HALO, cw), F32)],
                 sem=("parallel", "parallel", "arbitrary"))(x, x, w, b, dy)


def _glu_view(a):
    return a.reshape((2, a.shape[0] // 2) + a.shape[1:])


def conv_glu_fwd(x, w, b, *, name):
    s, c, bs, cw = _conv_tiles(x)
    kk = w.shape[1]
    half = x.shape[0] // 2

    def body(x_ref, h_ref, w_ref, b_ref, o_ref, gscr, vscr):
        first = pl.program_id(2) == 0
        _stage(gscr, x_ref.at[0], h_ref.at[0], first)
        _stage(vscr, x_ref.at[1], h_ref.at[1], first)
        for r0 in range(0, bs, CONV_CHUNK):
            gate = _conv_sum(_conv_taps(gscr, kk, r0), w_ref.at[0], b_ref.at[0])
            val = _conv_sum(_conv_taps(vscr, kk, r0), w_ref.at[1], b_ref.at[1])
            o_ref[pl.ds(r0, CONV_CHUNK), :] = (gate * _sigmoid(gate) * val).astype(o_ref.dtype)

    blk, halo, wspec = _conv_specs(bs, cw, lead=(2,))
    out, _, _ = _conv_specs(bs, cw)
    xv = _glu_view(x)
    return _call(body, name=name, out_shape=jax.ShapeDtypeStruct((half, s, c), MXU_DTYPE), grid=(half, c // cw, s // bs),
                 in_specs=[blk, halo, wspec(kk), wspec(1)], out_specs=out, scratch=[pltpu.VMEM((bs + HALO, cw), F32)] * 2,
                 sem=("parallel", "parallel", "arbitrary"))(xv, xv, _glu_view(w), _glu_view(b))


def conv_glu_bwd(x, da, w, b, *, name):
    s, c, bs, cw = _conv_tiles(x)
    kk = w.shape[1]
    half = x.shape[0] // 2

    def body(x_ref, h_ref, w_ref, b_ref, da_ref, du_ref, dw_ref, db_ref, gscr, vscr):
        i = pl.program_id(2)

        @pl.when(i == 0)
        def _():
            dw_ref[...] = jnp.zeros_like(dw_ref)
            db_ref[...] = jnp.zeros_like(db_ref)

        _stage(gscr, x_ref.at[0], h_ref.at[0], i == 0)
        _stage(vscr, x_ref.at[1], h_ref.at[1], i == 0)
        gsums, vsums = _TapSums(kk, cw), _TapSums(kk, cw)
        for r0 in range(0, bs, CONV_CHUNK):
            gtaps, vtaps = _conv_taps(gscr, kk, r0), _conv_taps(vscr, kk, r0)
            gate = _conv_sum(gtaps, w_ref.at[0], b_ref.at[0])
            val = _conv_sum(vtaps, w_ref.at[1], b_ref.at[1])
            dav = da_ref[pl.ds(r0, CONV_CHUNK), :]
            dgate = dav * val * _silu_grad(gate)
            dval = dav * gate * _sigmoid(gate)
            du_ref[0, pl.ds(r0, CONV_CHUNK), :] = dgate
            du_ref[1, pl.ds(r0, CONV_CHUNK), :] = dval
            gsums.add(dgate, gtaps)
            vsums.add(dval, vtaps)
        gsums.flush(dw_ref.at[0], db_ref.at[0])
        vsums.flush(dw_ref.at[1], db_ref.at[1])

    blk, halo, wspec = _conv_specs(bs, cw, lead=(2,))
    daspec, _, _ = _conv_specs(bs, cw)
    xv, wv, bv = _glu_view(x), _glu_view(w), _glu_view(b)
    du, dw, db = _call(body, name=name,
                       out_shape=[jax.ShapeDtypeStruct(xv.shape, F32), jax.ShapeDtypeStruct(wv.shape, F32),
                                  jax.ShapeDtypeStruct(bv.shape, F32)],
                       grid=(half, c // cw, s // bs), in_specs=[blk, halo, wspec(kk), wspec(1), daspec],
                       out_specs=[blk, wspec(kk), wspec(1)], scratch=[pltpu.VMEM((bs + HALO, cw), F32)] * 2,
                       sem=("parallel", "parallel", "arbitrary"))(xv, xv, wv, bv, da)
    return du.reshape(x.shape), dw.reshape(w.shape), db.reshape(b.shape)


def conv_t(du, w, *, name):
    s, c, bs, cw = _conv_tiles(du)
    kk = w.shape[1]
    nb = s // bs

    def body(d_ref, h_ref, w_ref, o_ref, scr):
        last = pl.program_id(2) == nb - 1
        scr[0:bs, :] = d_ref[...]
        scr[bs:, :] = jnp.where(last, 0.0, h_ref[...])
        for r0 in range(0, bs, CONV_CHUNK):
            acc = scr[pl.ds(r0 + kk - 1, CONV_CHUNK), :] * w_ref[0:1, :]
            for k in range(1, kk):
                acc = acc + scr[pl.ds(r0 + kk - 1 - k, CONV_CHUNK), :] * w_ref[k:k + 1, :]
            o_ref[pl.ds(r0, CONV_CHUNK), :] = acc

    blk, _, wspec = _conv_specs(bs, cw)
    halo = pl.BlockSpec((None, HALO, cw), lambda q, j, i: (q, jnp.minimum((i + 1) * (bs // HALO), s // HALO - 1), j))
    return _call(body, name=name, out_shape=jax.ShapeDtypeStruct(du.shape, F32), grid=(du.shape[0], c // cw, nb),
                 in_specs=[blk, halo, wspec(kk)], out_specs=blk, scratch=[pltpu.VMEM((bs + HALO, cw), F32)],
                 sem=("parallel", "parallel", "arbitrary"))(du, du, w)


def _ssd_common(xbc_ref, tail_ref, dtrt_ref, bias_ref, biast_ref, alog_ref, alogt_ref):
    L = SSM_CHUNK
    raw = tail_ref[...] + bias_ref[...]
    dt = _softplus(raw)
    dtt = _softplus(dtrt_ref[...] + biast_ref[...])
    a = -jnp.exp(alog_ref[...])
    at = -jnp.exp(alogt_ref[...])
    row, col, lower = _tri(L, lambda r, c: r >= c)
    tril = lower.astype(F32)
    cs = _nn(tril, dt * a, HIGHEST)
    cst = _nt(dtt * at, tril, HIGHEST)
    bm = [xbc_ref[:, SSM_INNER + g * SSM_N: SSM_INNER + (g + 1) * SSM_N] for g in range(SSM_GROUPS)]
    off = SSM_INNER + SSM_GROUPS * SSM_N
    cm = [xbc_ref[:, off + g * SSM_N: off + (g + 1) * SSM_N] for g in range(SSM_GROUPS)]
    cb = [_bnt(cm[g], bm[g]) for g in range(SSM_GROUPS)]
    return raw, dt, a, lower, tril, cs, cst, bm, cm, cb


def _ssd_head(hh, xbc_ref, dt, cs, cst, lower):
    L = SSM_CHUNK
    ln = DT_LANE + hh
    x = xbc_ref[:, hh * SSM_P:(hh + 1) * SSM_P]
    dtc = dt[:, ln:ln + 1]
    csc = cs[:, ln:ln + 1]
    csr = cst[hh:hh + 1, :]
    decay = jnp.exp(jnp.where(lower, csc - csr, -1e30))
    last = cs[L - 1:L, ln:ln + 1]
    return x, dtc, csc, decay, jnp.exp(csc), jnp.exp(last - csc), jnp.exp(last)


def _ssd_inputs(tail, dt_bias, a_log, d_skip):
    H = SSM_HEADS
    lanes = lambda vec: jnp.pad(vec.reshape(1, H), ((0, 0), (DT_LANE, LANES - DT_LANE - H)))
    return (tail, tail[:, DT_LANE:DT_LANE + H].T, lanes(dt_bias), dt_bias.reshape(H, 1), lanes(a_log),
            a_log.reshape(H, 1), lanes(d_skip))


def ssd_fwd(xbc, tail, dt_bias, a_log, d_skip, *, name):
    s = xbc.shape[0]
    L, H, P, N = SSM_CHUNK, SSM_HEADS, SSM_P, SSM_N
    nc = s // L

    def body(xbc_ref, tail_ref, dtrt_ref, bias_ref, biast_ref, alog_ref, alogt_ref, d_ref, y_ref, hp_ref, state):
        @pl.when(pl.program_id(0) == 0)
        def _():
            state[...] = jnp.zeros_like(state)

        raw, dt, a, lower, tril, cs, cst, bm, cm, cb = _ssd_common(
            xbc_ref, tail_ref, dtrt_ref, bias_ref, biast_ref, alog_ref, alogt_ref)
        for hh in range(H):
            g = hh // (H // SSM_GROUPS)
            x, dtc, csc, decay, e, tau, gamma = _ssd_head(hh, xbc_ref, dt, cs, cst, lower)
            xdt = x * dtc
            hprev = state[hh]
            hp_ref[hh] = hprev
            skip = d_ref[:, DT_LANE + hh:DT_LANE + hh + 1]
            y = _bnn(cb[g] * decay, xdt) + _bnn(cm[g], hprev) * e + x * skip
            y_ref[:, hh * P:(hh + 1) * P] = y
            state[hh] = hprev * gamma + _btn(bm[g] * tau, xdt)

    row = lambda w: pl.BlockSpec((L, w), lambda c: (c, 0))
    small = lambda shp: pl.BlockSpec(shp, lambda c: (0, 0))
    return _call(body, name=name,
                 out_shape=[jax.ShapeDtypeStruct((s, SSM_INNER), F32), jax.ShapeDtypeStruct((nc, H, N, P), F32)],
                 grid=(nc,),
                 in_specs=[row(SSM_CONV_DIM), row(LANES), pl.BlockSpec((H, L), lambda c: (0, c)), small((1, LANES)),
                           small((H, 1)), small((1, LANES)), small((H, 1)), small((1, LANES))],
                 out_specs=[row(SSM_INNER), pl.BlockSpec((None, H, N, P), lambda c: (c, 0, 0, 0))],
                 scratch=[pltpu.VMEM((H, N, P), F32)], sem=("arbitrary",))(xbc, *_ssd_inputs(tail, dt_bias, a_log, d_skip))


def ssd_bwd(xbc, tail, dt_bias, a_log, d_skip, hprev_all, dy, *, name):
    s = xbc.shape[0]
    L, H, P, N = SSM_CHUNK, SSM_HEADS, SSM_P, SSM_N
    nc = s // L
    hg = H // SSM_GROUPS

    def body(xbc_ref, tail_ref, dtrt_ref, bias_ref, biast_ref, alog_ref, alogt_ref, d_ref, hp_ref, dy_ref,
             dxbc_ref, ddt_ref, dbias_ref, dalog_ref, dd_ref, dstate):
        @pl.when(pl.program_id(0) == 0)
        def _():
            dstate[...] = jnp.zeros_like(dstate)
            dbias_ref[...] = jnp.zeros_like(dbias_ref)
            dalog_ref[...] = jnp.zeros_like(dalog_ref)
            dd_ref[...] = jnp.zeros_like(dd_ref)

        raw, dt, a, lower, tril, cs, cst, bm, cm, cb = _ssd_common(
            xbc_ref, tail_ref, dtrt_ref, bias_ref, biast_ref, alog_ref, alogt_ref)
        lane = lax.broadcasted_iota(jnp.int32, (L, LANES), 1)
        lane1 = lax.broadcasted_iota(jnp.int32, (1, LANES), 1)
        rowi = lax.broadcasted_iota(jnp.int32, (L, 1), 0)
        ones = jnp.ones((L, LANES), F32)
        dcs_all = jnp.zeros((L, LANES), F32)
        ddt_x = jnp.zeros((L, LANES), F32)
        dd_row = jnp.zeros((1, LANES), F32)
        dbm = [jnp.zeros((L, N), F32) for _ in range(SSM_GROUPS)]
        dcm = [jnp.zeros((L, N), F32) for _ in range(SSM_GROUPS)]
        dcb = [jnp.zeros((L, L), F32) for _ in range(SSM_GROUPS)]
        for hh in range(H):
            g = hh // hg
            ln = DT_LANE + hh
            x, dtc, csc, decay, e, tau, gamma = _ssd_head(hh, xbc_ref, dt, cs, cst, lower)
            xdt = x * dtc
            hprev = hp_ref[hh]
            dhn = dstate[hh]
            dyh = dy_ref[:, hh * P:(hh + 1) * P]
            m = cb[g] * decay
            dxdt = _btn(m, dyh) + _bnn(bm[g] * tau, dhn)
            dm = jnp.where(lower, _bnt(dyh, xdt), 0.0)
            dcb[g] = dcb[g] + dm * decay
            dseg = dm * m
            dcs = jnp.sum(dseg, axis=1, keepdims=True) - _tn(dseg, ones, HIGHEST)[:, 0:1]
            edy = e * dyh
            dcm[g] = dcm[g] + _bnt(edy, hprev)
            dcs = dcs + e * jnp.sum(dyh * _bnn(cm[g], hprev), axis=1, keepdims=True)
            xdh = _bnt(xdt, dhn)
            dbm[g] = dbm[g] + tau * xdh
            dtau_tau = jnp.sum(bm[g] * xdh, axis=1, keepdims=True) * tau
            dlast = jnp.sum(dtau_tau, axis=0, keepdims=True) + gamma * jnp.sum(dhn * hprev, keepdims=True)
            dcs = dcs - dtau_tau + jnp.where(rowi == L - 1, dlast, 0.0)
            dstate[hh] = gamma * dhn + _btn(cm[g], edy)
            dcs_all = jnp.where(lane == ln, dcs, dcs_all)
            ddt_x = jnp.where(lane == ln, jnp.sum(dxdt * x, axis=1, keepdims=True), ddt_x)
            dxbc_ref[:, hh * P:(hh + 1) * P] = dxdt * dtc + d_ref[:, ln:ln + 1] * dyh
            dd_row = jnp.where(lane1 == ln, jnp.sum(dyh * x, keepdims=True), dd_row)
        off = SSM_INNER + SSM_GROUPS * SSM_N
        for g in range(SSM_GROUPS):
            dxbc_ref[:, SSM_INNER + g * N: SSM_INNER + (g + 1) * N] = dbm[g] + _btn(dcb[g], cm[g])
            dxbc_ref[:, off + g * N: off + (g + 1) * N] = dcm[g] + _bnn(dcb[g], bm[g])
        dda = _tn(tril, dcs_all, HIGHEST)
        head_lane = (lane >= DT_LANE) & (lane < DT_LANE + H)
        draw = jnp.where(head_lane, (dda * a + ddt_x) * _sigmoid(raw), 0.0)
        ddt_ref[...] = draw
        dbias_ref[...] += jnp.sum(draw, axis=0, keepdims=True)
        dalog_ref[...] += jnp.sum(jnp.where(head_lane, dda * dt, 0.0), axis=0, keepdims=True) * a
        dd_ref[...] += dd_row

    rev = lambda c: nc - 1 - c
    row = lambda w: pl.BlockSpec((L, w), lambda c: (rev(c), 0))
    small = lambda shp: pl.BlockSpec(shp, lambda c: (0, 0))
    acc = pl.BlockSpec((1, LANES), lambda c: (0, 0))
    vec = jax.ShapeDtypeStruct((1, LANES), F32)
    return _call(body, name=name,
                 out_shape=[jax.ShapeDtypeStruct((s, SSM_CONV_DIM), F32), jax.ShapeDtypeStruct((s, LANES), F32), vec, vec, vec],
                 grid=(nc,),
                 in_specs=[row(SSM_CONV_DIM), row(LANES), pl.BlockSpec((H, L), lambda c: (0, rev(c))), small((1, LANES)),
                           small((H, 1)), small((1, LANES)), small((H, 1)), small((1, LANES)),
                           pl.BlockSpec((None, H, N, P), lambda c: (rev(c), 0, 0, 0)), row(SSM_INNER)],
                 out_specs=[row(SSM_CONV_DIM), row(LANES), acc, acc, acc],
                 scratch=[pltpu.VMEM((H, N, P), F32)], sem=("arbitrary",))(
        xbc, *_ssd_inputs(tail, dt_bias, a_log, d_skip), hprev_all, dy)


def _heads(x2d, n, d):
    s = x2d.shape[0]
    return x2d.reshape(s, n, d).transpose(1, 0, 2)


def _unheads(x3d):
    n, s, d = x3d.shape
    return x3d.transpose(1, 0, 2).reshape(s, n * d)


def layer_fwd(h, p, tabs, li, gather_late=None):
    s = h.shape[0]
    tabq, tabt = tabs
    nm = lambda t: f"L{li}_{t}"
    r = {'h': h}
    hn = rms_fwd(h, p['mix_norm'], name=nm('mixnorm'), out_dtype=MXU_DTYPE)
    proj = mm(hn, p['w_in'], name=nm('proj'))
    r.update(hn=hn, proj=proj)
    qkv = proj[:, :3 * SB_WIDTH].astype(MXU_DTYPE)
    late = None
    if gather_late is None:
        ya, tot = sb_fwd(qkv, name=nm('sb_fwd'))
    else:
        ya, tot, *got = sb_fwd(qkv, name=nm('sb_fwd'), carried=Carried('gather', [gather_late[n] for n in LATE]))
        late = assemble_late(dict(zip(LATE, got)))
        p = dict(p, **{n: late[n][li] for n in LATE})
    yan = rms_fwd(ya, p['sb_out_norm'], name=nm('sbnorm'), out_dtype=MXU_DTYPE)
    r.update(qkv=qkv, ya=ya, tot=tot)
    z = proj[:, 768:1280]
    xbc = proj[None, :, 1280:2048]
    tail = proj[:, TAIL:TAIL + LANES]
    xbc_act = conv_silu_fwd(xbc, p['ssm_conv_w'], p['ssm_conv_b'], name=nm('ssmconv'))[0]
    y_ssm, hprev = ssd_fwd(xbc_act, tail, p['ssm_dt_bias'], p['ssm_a_log'], p['ssm_d'], name=nm('ssd_fwd'))
    ybn = rms_fwd(y_ssm, p['ssm_out_norm'], name=nm('ssmnorm'), gate=z, out_dtype=MXU_DTYPE)
    r.update(z=z, xbc=xbc, tail=tail, xbc_act=xbc_act, y_ssm=y_ssm, hprev=hprev)
    cq = proj[:, 2048:2304]
    ckv = proj[:, 2304:2432]
    qn = rms_fwd(cq, p['mla_q_norm'], name=nm('qnorm'), out_dtype=MXU_DTYPE)
    q_r = rope(mm(qn, p['mla_w_uq'], name=nm('uq'))[None], tabq, name=nm('ropeq'))
    kvn = rms_fwd(ckv, p['mla_kv_norm'], name=nm('kvnorm'), out_dtype=MXU_DTYPE)
    kv = mm(kvn, p['mla_w_ukv'], name=nm('ukv'))
    k_pe = rope(tail[None], tabt, name=nm('ropek'))[:, :MLA_ROPE]
    qh = _heads(q_r, MLA_HEADS, MLA_QK).astype(MXU_DTYPE)
    kvh = _heads(kv, MLA_HEADS, MLA_NOPE + MLA_V)
    kh = jnp.concatenate([kvh[..., :MLA_NOPE], jnp.broadcast_to(k_pe[None], (MLA_HEADS, s, MLA_ROPE))],
                         axis=-1).astype(MXU_DTYPE)
    vh = kvh[..., MLA_NOPE:].astype(MXU_DTYPE)
    yc_h, lse = mla_fwd(qh, kh, vh, name=nm('mla_fwd'))
    yc = _unheads(yc_h)
    ycn = rms_fwd(yc, p['mla_out_norm'], name=nm('mlanorm'), out_dtype=MXU_DTYPE)
    r.update(cq=cq, ckv=ckv, qn=qn, kvn=kvn, qh=qh, kh=kh, vh=vh, yc_h=yc_h, yc=yc, lse=lse)
    ycat = jnp.concatenate([yan, ybn, ycn], axis=1)
    h1 = mm(ycat, p['w_out'], name=nm('outproj'), res=h)
    hn2 = rms_fwd(h1, p['ffn_norm'], name=nm('ffnnorm'), out_dtype=MXU_DTYPE)
    up = mm(hn2, p['ffn_w_up'], name=nm('up'), bb='o')
    act = conv_glu_fwd(up, p['ffn_conv_w'], p['ffn_conv_b'], name=nm('glu'))
    h2 = mm(act, p['ffn_w_down'], name=nm('down'), ab='k', bb='k', res=h1)
    r.update(ycat=ycat, h1=h1, hn2=hn2, up=up, act=act)
    return h2, r, p, late


def layer_bwd(dh2, p, r, tabs, li, scatter_late=None):
    s = dh2.shape[0]
    tabq, tabt = tabs
    nm = lambda t: f"L{li}_{t}"
    g = {}
    dact = mm(dh2, p['ffn_w_down'], name=nm('d_down_x'), tb=True, bb='o')
    g['ffn_w_down'] = mm(r['act'], dh2, name=nm('d_down_w'), out_dtype=WIRE_DTYPE, ta=True, ab='o')
    du, g['ffn_conv_w'], g['ffn_conv_b'] = conv_glu_bwd(r['up'], dact, p['ffn_conv_w'], p['ffn_conv_b'], name=nm('d_glu'))
    dup = conv_t(du, p['ffn_conv_w'], name=nm('d_ffnconv'))
    g['ffn_w_up'] = mm(r['hn2'], dup, name=nm('d_up_w'), out_dtype=WIRE_DTYPE, ta=True, bb='o')
    dhn2 = mm(dup, p['ffn_w_up'], name=nm('d_up_x'), tb=True, ab='k', bb='k')
    dh1, dg = rms_bwd(r['h1'], p['ffn_norm'], dhn2, name=nm('d_ffnnorm'), add=dh2)
    g['ffn_norm'] = dg[0]
    dycat = mm(dh1, p['w_out'], name=nm('d_out_x'), tb=True)
    g['w_out'] = mm(r['ycat'], dh1, name=nm('d_out_w'), out_dtype=WIRE_DTYPE, ta=True)
    dya, dg = rms_bwd(r['ya'], p['sb_out_norm'], dycat[:, :256], name=nm('d_sbnorm'))
    g['sb_out_norm'] = dg[0]
    recv_late = None
    if scatter_late is None:
        dq, dk, dv = sb_bwd(r['qkv'], dya, r['tot'], name=nm('sb_bwd'))
    else:
        parts = owner_parts_late([g] + list(scatter_late))
        dq, dk, dv, *got = sb_bwd(r['qkv'], dya, r['tot'], name=nm('sb_bwd'),
                                  carried=Carried('scatter', [parts[n].astype(WIRE_DTYPE) for n in LATE]))
        recv_late = dict(zip(LATE, got))
    dyssm, dz, dg = rms_bwd(r['y_ssm'], p['ssm_out_norm'], dycat[:, 256:768], name=nm('d_ssmnorm'), gate=r['z'])
    g['ssm_out_norm'] = dg[0]
    dxbc_act, ddt_tail, dbias, dalog, dd = ssd_bwd(r['xbc_act'], r['tail'], p['ssm_dt_bias'], p['ssm_a_log'],
                                                   p['ssm_d'], r['hprev'], dyssm, name=nm('ssd_bwd'))
    hl = slice(DT_LANE, DT_LANE + SSM_HEADS)
    g['ssm_dt_bias'], g['ssm_a_log'], g['ssm_d'] = dbias[0, hl], dalog[0, hl], dd[0, hl]
    dxbc_u, g['ssm_conv_w'], g['ssm_conv_b'] = conv_silu_bwd(r['xbc'], dxbc_act[None], p['ssm_conv_w'], p['ssm_conv_b'],
                                                             name=nm('d_ssmact'))
    dxbc = conv_t(dxbc_u, p['ssm_conv_w'], name=nm('d_ssmconv'))[0]
    dyc, dg = rms_bwd(r['yc'], p['mla_out_norm'], dycat[:, 768:], name=nm('d_mlanorm'))
    g['mla_out_norm'] = dg[0]
    dqh, dkh, dvh = mla_bwd(r['qh'], r['kh'], r['vh'], r['yc_h'], _heads(dyc, MLA_HEADS, MLA_V), r['lse'], name=nm('mla_bwd'))
    dq_c = rope(_unheads(dqh)[None], tabq, name=nm('d_ropeq'), backward=True)
    g['mla_w_uq'] = mm(r['qn'], dq_c, name=nm('d_uq_w'), out_dtype=WIRE_DTYPE, ta=True)
    dcq, dg = rms_bwd(r['cq'], p['mla_q_norm'], mm(dq_c, p['mla_w_uq'], name=nm('d_uq_x'), tb=True), name=nm('d_qnorm'))
    g['mla_q_norm'] = dg[0]
    dkv = _unheads(jnp.concatenate([dkh[..., :MLA_NOPE], dvh], axis=-1))
    g['mla_w_ukv'] = mm(r['kvn'], dkv, name=nm('d_ukv_w'), out_dtype=WIRE_DTYPE, ta=True)
    dckv, dg = rms_bwd(r['ckv'], p['mla_kv_norm'], mm(dkv, p['mla_w_ukv'], name=nm('d_ukv_x'), tb=True), name=nm('d_kvnorm'))
    g['mla_kv_norm'] = dg[0]
    dkpe = jnp.pad(dkh[..., MLA_NOPE:], ((0, 0), (0, 0), (0, LANES - MLA_ROPE)))
    dtail = rope(dkpe, tabt, name=nm('d_ropek'), backward=True, add=ddt_tail)
    dproj = jnp.concatenate([dq, dk, dv, dz, dxbc, dcq, dckv, dtail], axis=1)
    g['w_in'] = mm(r['hn'], dproj, name=nm('d_proj_w'), out_dtype=WIRE_DTYPE, ta=True)
    dhn = mm(dproj, p['w_in'], name=nm('d_proj_x'), tb=True)
    dh, dg = rms_bwd(r['h'], p['mix_norm'], dhn, name=nm('d_mixnorm'), add=dh1)
    g['mix_norm'] = dg[0]
    return dh, g, recv_late


def _w_in_placement():
    c = np.arange(D_IN)
    dest = np.where(c < 2048, c, np.where(c < 2056, c + (D_IN - 2056), c - 8))
    dest = jnp.asarray(dest.reshape(N_DEV, D_IN // N_DEV, 1), jnp.int32)
    return (dest == jnp.arange(D_IN_PAD, dtype=jnp.int32)).astype(MXU_DTYPE)


def _owner_major(full, axis):
    shp = full.shape
    return jnp.moveaxis(full.reshape(shp[:axis] + (N_DEV, shp[axis] // N_DEV) + shp[axis + 1:]), axis, 0)


def _owner_join(parts, axis):
    moved = jnp.moveaxis(parts, 0, axis)
    shp = moved.shape
    return moved.reshape(shp[:axis] + (shp[axis] * shp[axis + 1],) + shp[axis + 2:])


def assemble_early(gathered, replicated):
    L = DEPTH
    out = dict(replicated)
    out['w_in'] = mm(gathered['w_in'].reshape(N_DEV, L * D_MODEL, D_IN // N_DEV), _w_in_placement(), name='place_w_in',
                     ab='k', bb='k', out_dtype=MXU_DTYPE).reshape(L, D_MODEL, D_IN_PAD)
    out['mla_w_uq'] = _owner_join(gathered['mla_w_uq'], 2)
    out['mla_w_ukv'] = _owner_join(gathered['mla_w_ukv'], 2)
    out['ssm_conv_w'] = _owner_join(gathered['ssm_conv_w'], 2)[:, None]
    out['ssm_conv_b'] = replicated['ssm_conv_b'].reshape(L, 1, 1, SSM_CONV_DIM)
    out['ffn_conv_b'] = replicated['ffn_conv_b'].reshape(L, N_DEV, 1, FF_SHARD)
    return out


def assemble_late(gathered):
    L = DEPTH
    return {'ffn_w_up': jnp.moveaxis(gathered['ffn_w_up'], 1, 0),
            'w_out': _owner_join(gathered['w_out'], 1),
            'ffn_w_down': _owner_join(gathered['ffn_w_down'], 1).reshape(L, N_DEV // 2, FF_SHARD, D_MODEL),
            'ffn_conv_w': jnp.moveaxis(gathered['ffn_conv_w'], 1, 0)}


def owner_parts_late(grads):
    L = DEPTH
    st = lambda n: jnp.stack([g[n] for g in grads])
    return {'ffn_w_up': jnp.moveaxis(st('ffn_w_up'), 1, 0),
            'w_out': _owner_major(st('w_out'), 1),
            'ffn_w_down': _owner_major(st('ffn_w_down').reshape(L, D_FF, D_MODEL), 1),
            'ffn_conv_w': jnp.moveaxis(st('ffn_conv_w'), 1, 0)}


def owner_parts_early(grads):
    L = DEPTH
    st = lambda n: jnp.stack([g[n] for g in grads])
    parts = {
        'w_in': mm(st('w_in').reshape(L * D_MODEL, D_IN_PAD), _w_in_placement(), name='unplace_w_in', tb=True, bb='o',
                   out_dtype=WIRE_DTYPE).reshape(N_DEV, L, D_MODEL, D_IN // N_DEV),
        'mla_w_uq': _owner_major(st('mla_w_uq'), 2),
        'mla_w_ukv': _owner_major(st('mla_w_ukv'), 2),
        'ssm_conv_w': _owner_major(st('ssm_conv_w')[:, 0], 2),
    }
    rep = {n: st(n) for n in REPLICATED if n not in ('final_norm', 'ssm_conv_b', 'ffn_conv_b')}
    rep['ssm_conv_b'] = st('ssm_conv_b').reshape(L, SSM_CONV_DIM)
    rep['ffn_conv_b'] = st('ffn_conv_b').reshape(L, 2 * D_FF)
    return parts, rep


def local_step(x, positions, target, early, late_shards, replicated):
    s = x.shape[0]
    tabs = _rope_tables(positions, s)
    params = assemble_early(early, replicated)
    layer = lambda li: {n: params[n][li] for n in params if n != 'final_norm'}
    h, r0, p0, late = layer_fwd(x, layer(0), tabs, 0, gather_late=late_shards)
    saved = [(p0, r0)]
    for li in range(1, DEPTH):
        h, r, p, _ = layer_fwd(h, dict(layer(li), **{n: late[n][li] for n in LATE}), tabs, li)
        saved.append((p, r))
    y = rms_fwd(h, params['final_norm'], name='finalnorm')
    dy, loss = loss_head(y, target, name='loss')
    dh, dg = rms_bwd(h, params['final_norm'], dy, name='d_finalnorm')
    above = []
    for li in reversed(range(1, DEPTH)):
        dh, g, _ = layer_bwd(dh, *saved[li], tabs, li)
        above.insert(0, g)
    dh, g0, recv_late = layer_bwd(dh, *saved[0], tabs, 0, scatter_late=above)
    parts, rep = owner_parts_early([g0] + above)
    rep['final_norm'] = dg[0]
    return loss[0, 0], dh, parts, recv_late, rep


def all_gather(blocks, *, name):
    n = len(blocks)

    def body(*refs):
        x_refs, out_refs = refs[:n], refs[n:2 * n]
        send_sems, recv_sems, local_sems = refs[2 * n:]
        x, y, c = lax.axis_index("x"), lax.axis_index("y"), lax.axis_index("c")
        me, sibling = (x, y, c), (x, y, 1 - c)
        chips = [(1 - x, y), (x, 1 - y), (1 - x, 1 - y)]

        def slot(b, px, py, pc):
            return out_refs[b].at[4 * px + 2 * py + pc]

        def copy(b, k, blk, to, src=None):
            return pltpu.make_async_remote_copy(src_ref=slot(b, *blk) if src is None else src, dst_ref=slot(b, *blk),
                                                send_sem=send_sems.at[b, k], recv_sem=recv_sems.at[b, k],
                                                device_id=to, device_id_type=MESH)

        mine = [pltpu.make_async_copy(x_refs[b], slot(b, *me), local_sems.at[b]) for b in range(n)]
        for cp in mine:
            cp.start()
        first = []
        for b in range(n):
            first.append(copy(b, 0, me, sibling, src=x_refs[b]))
            first += [copy(b, 1 + j, me, (*chip, c), src=x_refs[b]) for j, chip in enumerate(chips)]
        for cp in first:
            cp.start()
        passed = []
        for j, chip in enumerate(chips):
            for b in range(n):
                copy(b, 1 + j, (*chip, c), me).wait_recv()
                fwd = copy(b, 4 + j, (*chip, c), sibling)
                fwd.start()
                passed.append(fwd)
        for b in range(n):
            copy(b, 0, sibling, me).wait_recv()
            for j, chip in enumerate(chips):
                copy(b, 4 + j, (*chip, 1 - c), me).wait_recv()
        for cp in first + passed:
            cp.wait_send()
        for cp in mine:
            cp.wait()

    return pl.pallas_call(
        body, name=name, out_shape=[jax.ShapeDtypeStruct((N_DEV,) + b.shape, b.dtype) for b in blocks],
        in_specs=[HBM] * n, out_specs=[HBM] * n,
        scratch_shapes=[pltpu.SemaphoreType.DMA((n, 7)), pltpu.SemaphoreType.DMA((n, 7)), pltpu.SemaphoreType.DMA((n,))],
    )(*blocks)


def all_to_all(parts, *, name):
    n = len(parts)

    def body(*refs):
        g_refs, r_refs = refs[:n], refs[n:2 * n]
        send_sems, recv_sems, local_sems = refs[2 * n:]
        x, y, c = lax.axis_index("x"), lax.axis_index("y"), lax.axis_index("c")
        me = 4 * x + 2 * y + c
        mine = [pltpu.make_async_copy(g_refs[b].at[me], r_refs[b].at[me], local_sems.at[b]) for b in range(n)]
        for cp in mine:
            cp.start()
        copies = []
        for k in range(1, N_DEV):
            px, py, pc = _flip(x, k & 4), _flip(y, k & 2), _flip(c, k & 1)
            peer = 4 * px + 2 * py + pc
            for b in range(n):
                cp = pltpu.make_async_remote_copy(src_ref=g_refs[b].at[peer], dst_ref=r_refs[b].at[me],
                                                  send_sem=send_sems.at[b, k - 1], recv_sem=recv_sems.at[b, k - 1],
                                                  device_id=(px, py, pc), device_id_type=MESH)
                cp.start()
                copies.append(cp)
        for cp in copies:
            cp.wait_send()
            cp.wait_recv()
        for cp in mine:
            cp.wait()

    return pl.pallas_call(
        body, name=name, out_shape=[jax.ShapeDtypeStruct(p.shape, p.dtype) for p in parts],
        in_specs=[HBM] * n, out_specs=[HBM] * n,
        scratch_shapes=[pltpu.SemaphoreType.DMA((n, 7)), pltpu.SemaphoreType.DMA((n, 7)), pltpu.SemaphoreType.DMA((n,))],
    )(*parts)


def adamw(parts, w, m, v, *, name):
    r, wd = w.shape
    br = _tile(r, (256, 128, 64, 32, 16, 8))
    c1 = 1.0 - ADAM_B1 ** ADAM_STEP
    c2 = 1.0 - ADAM_B2 ** ADAM_STEP

    def body(p_ref, w_ref, m_ref, v_ref, g_ref, d_ref, mo_ref, vo_ref):
        g = p_ref[0].astype(F32)
        for j in range(1, N_DEV):
            g = g + p_ref[j].astype(F32)
        mn = ADAM_B1 * m_ref[...] + (1.0 - ADAM_B1) * g
        vn = ADAM_B2 * v_ref[...] + (1.0 - ADAM_B2) * (g * g)
        g_ref[...] = g
        mo_ref[...] = mn
        vo_ref[...] = vn
        d_ref[...] = -ADAM_LR * ((mn / c1) / (jnp.sqrt(vn / c2) + ADAM_EPS) + ADAM_WD * w_ref[...])

    blk = pl.BlockSpec((br, wd), lambda i: (i, 0))
    out = jax.ShapeDtypeStruct((r, wd), F32)
    return _call(body, name=name, out_shape=[out] * 4, grid=(r // br,),
                 in_specs=[pl.BlockSpec((N_DEV, br, wd), lambda i: (0, i, 0)), blk, blk, blk], out_specs=[blk] * 4,
                 sem=("parallel",))(parts, w, m, v)


def _pack(arrs):
    flat = jnp.concatenate([a.reshape(-1) for a in arrs])
    rows = -(-flat.shape[0] // (8 * FLAT_W)) * 8
    return jnp.pad(flat, (0, rows * FLAT_W - flat.shape[0])).reshape(rows, FLAT_W)


def _unpack(flat, shapes):
    flat = flat.reshape(-1)
    out, off = [], 0
    for shp in shapes:
        n = int(np.prod(shp))
        out.append(flat[off:off + n].reshape(shp))
        off += n
    return out


def kernel(x, positions, mix_norm, w_in, sb_out_norm, ssm_conv_w, ssm_conv_b, ssm_dt_bias, ssm_a_log, ssm_d, ssm_out_norm, mla_q_norm, mla_w_uq, mla_kv_norm, mla_w_ukv, mla_out_norm, w_out, ffn_norm, ffn_w_up, ffn_conv_w, ffn_conv_b, ffn_w_down, final_norm, loss_target, m_mix_norm, m_w_in, m_sb_out_norm, m_ssm_conv_w, m_ssm_conv_b, m_ssm_dt_bias, m_ssm_a_log, m_ssm_d, m_ssm_out_norm, m_mla_q_norm, m_mla_w_uq, m_mla_kv_norm, m_mla_w_ukv, m_mla_out_norm, m_w_out, m_ffn_norm, m_ffn_w_up, m_ffn_conv_w, m_ffn_conv_b, m_ffn_w_down, m_final_norm, v_mix_norm, v_w_in, v_sb_out_norm, v_ssm_conv_w, v_ssm_conv_b, v_ssm_dt_bias, v_ssm_a_log, v_ssm_d, v_ssm_out_norm, v_mla_q_norm, v_mla_w_uq, v_mla_kv_norm, v_mla_w_ukv, v_mla_out_norm, v_w_out, v_ffn_norm, v_ffn_w_up, v_ffn_conv_w, v_ffn_conv_b, v_ffn_w_down, v_final_norm):
    args = locals()
    w = {n: args[n] for n in WEIGHTS}
    m = {n: args['m_' + n] for n in WEIGHTS}
    v = {n: args['v_' + n] for n in WEIGHTS}
    wire = lambda n: w[n] if n in VPU_WEIGHTS else w[n].astype(BF16)
    early = dict(zip(EARLY, all_gather([wire(n) for n in EARLY], name='gather_early')))

    loss, dx, parts, recv, rep = local_step(x[0], positions[0], loss_target[0], early, {n: wire(n) for n in LATE},
                                            {n: w[n] for n in REPLICATED})
    loss = lax.psum(loss, ("x", "y", "c"))

    recv.update(zip(EARLY, all_to_all([parts[n].astype(WIRE_DTYPE) for n in EARLY], name='scatter_early')))
    res = {kind: {} for kind in 'gdmv'}
    for n, rv in recv.items():
        shp = w[n].shape
        two_d = (int(np.prod(shp[:-1])), shp[-1])
        outs = adamw(rv.reshape((N_DEV,) + two_d), w[n].reshape(two_d), m[n].reshape(two_d), v[n].reshape(two_d),
                     name='adamw_' + n)
        for kind, o in zip('gdmv', outs):
            res[kind][n] = o.reshape(shp)

    rep_shapes = [w[n].shape for n in REPLICATED]
    (rparts,) = all_gather([_pack([rep[n] for n in REPLICATED])], name='gather_small_grads')
    rflat = lambda d: _pack([d[n] for n in REPLICATED])
    routs = adamw(rparts, rflat(w), rflat(m), rflat(v), name='adamw_replicated')
    for kind, o in zip('gdmv', routs):
        res[kind].update(zip(REPLICATED, _unpack(o, rep_shapes)))

    return (loss, dx[None], *[res['g'][n] for n in WEIGHTS], *[res['d'][n] for n in WEIGHTS],
            *[res['m'][n] for n in WEIGHTS], *[res['v'][n] for n in WEIGHTS])
```

```python
import numpy as np
import jax
import jax.numpy as jnp
from jax import lax
from jax.experimental import pallas as pl
from jax.experimental.pallas import tpu as pltpu

F32 = jnp.float32
BF16 = jnp.bfloat16
MXU_DTYPE = jnp.bfloat16
HIGHEST = lax.Precision.HIGHEST
WIRE_DTYPE = jnp.bfloat16

N_DEV = 8
D_MODEL = 1024
DEPTH = 2
EPS = 1e-6
SB_HEADS, SB_DIM = 4, 64
SB_WIDTH = SB_HEADS * SB_DIM
SSM_HEADS, SSM_P, SSM_GROUPS, SSM_N, SSM_CONV, SSM_CHUNK = 8, 64, 2, 64, 4, 128
SSM_INNER = SSM_HEADS * SSM_P
SSM_CONV_DIM = SSM_INNER + 2 * SSM_GROUPS * SSM_N
MLA_HEADS, MLA_NOPE, MLA_ROPE, MLA_V, MLA_Q_RANK, MLA_KV_RANK = 4, 64, 32, 64, 256, 128
MLA_QK = MLA_NOPE + MLA_ROPE
ROPE_THETA = 10000.0
D_IN = 2472
D_IN_PAD = 2560
TAIL = 2432
DT_LANE = 32
D_FF = 2816
FF_SHARD = 2 * D_FF // N_DEV
ADAM_LR, ADAM_B1, ADAM_B2, ADAM_EPS, ADAM_WD, ADAM_STEP = 0.001, 0.9, 0.999, 1e-08, 0.01, 10

LANES = 128
ATT_BLK = 256
SB_WIDE = 2
MLA_WIDE = 4
ROW_BLK = 512
ROW_BLOCK_BYTES = 2 << 20
CONV_COLS = 256
FLAT_W = 1024
VMEM_LIMIT = 56 << 20
MM_TM = (1024, 512, 256, 128)
MM_TN = (1280, 1024, 768, 640, 512, 384, 256, 128)
MM_TK = (1280, 1024, 512, 256, 128)

WEIGHTS = ['mix_norm', 'w_in', 'sb_out_norm', 'ssm_conv_w', 'ssm_conv_b', 'ssm_dt_bias', 'ssm_a_log', 'ssm_d',
           'ssm_out_norm', 'mla_q_norm', 'mla_w_uq', 'mla_kv_norm', 'mla_w_ukv', 'mla_out_norm', 'w_out',
           'ffn_norm', 'ffn_w_up', 'ffn_conv_w', 'ffn_conv_b', 'ffn_w_down', 'final_norm']
SHARDED = {'w_in': 2, 'ssm_conv_w': 2, 'mla_w_uq': 2, 'mla_w_ukv': 2, 'w_out': 1, 'ffn_w_up': 2, 'ffn_conv_w': 2,
           'ffn_w_down': 1}
VPU_WEIGHTS = ('ssm_conv_w', 'ffn_conv_w')
EARLY = ('w_in', 'mla_w_uq', 'mla_w_ukv', 'ssm_conv_w')
LATE = ('w_out', 'ffn_w_up', 'ffn_conv_w', 'ffn_w_down')
REPLICATED = [n for n in WEIGHTS if n not in SHARDED]


def _call(body, *, name, out_shape, grid=(), in_specs=None, out_specs=None, scratch=(), sem=None, **kw):
    params = dict(vmem_limit_bytes=VMEM_LIMIT)
    if sem is not None:
        params['dimension_semantics'] = sem
    return pl.pallas_call(body, name=name, out_shape=out_shape, grid=grid, in_specs=in_specs, out_specs=out_specs,
                          scratch_shapes=list(scratch), compiler_params=pltpu.CompilerParams(**params), **kw)


def _tile(n, prefs):
    for t in prefs:
        if n % t == 0:
            return t
    return n


def _rows(s, w):
    rows = ROW_BLK
    while rows * 2 <= s and s % (rows * 2) == 0 and rows * 2 * w * 4 <= ROW_BLOCK_BYTES:
        rows *= 2
    return _tile(s, (rows,))


def _dot(a, b, dims, precision=None):
    return lax.dot_general(a, b, (dims, ((), ())), preferred_element_type=F32, precision=precision)


def _nn(a, b, precision=None):
    return _dot(a, b, ((1,), (0,)), precision)


def _nt(a, b, precision=None):
    return _dot(a, b, ((1,), (1,)), precision)


def _tn(a, b, precision=None):
    return _dot(a, b, ((0,), (0,)), precision)


def _mxu(f):
    return lambda a, b: f(a.astype(MXU_DTYPE), b.astype(MXU_DTYPE))


_bnn, _bnt, _btn = _mxu(_nn), _mxu(_nt), _mxu(_tn)


def _split2(x):
    hi = x.astype(MXU_DTYPE)
    lo = (x - hi.astype(F32)).astype(MXU_DTYPE)
    return hi, lo


def _sigmoid(x):
    return 0.5 * jnp.tanh(0.5 * x) + 0.5


def _softplus(x):
    return jnp.maximum(x, 0.0) + jnp.log1p(jnp.exp(-jnp.abs(x)))


def _softplus_att(x):
    return jnp.maximum(x, 0.0) + jnp.log(1.0 + jnp.exp(-jnp.abs(x)))


def _cum(x, u):
    rows, b = x.shape[0], u.shape[0]
    n = x.shape[1] // b
    hi, lo = _split2(x)
    stack = [part[:, t * b:(t + 1) * b] for part in (hi, lo) for t in range(n)]
    r = _nn(jnp.concatenate(stack, axis=0), u)
    return jnp.concatenate([r[t * rows:(t + 1) * rows] + r[(n + t) * rows:(n + t + 1) * rows] for t in range(n)], axis=1)


def _causal_loop(qi, tile, carry, reverse, width):
    last = qi // width
    if reverse:
        return lax.fori_loop(0, last, lambda i, c: tile((last - 1 - i) * width, c, False), tile(last * width, carry, True))
    return tile(last * width, lax.fori_loop(0, last, lambda i, c: tile(i * width, c, False), carry), True)


def _causal_mask(blk, width, qi, kb, heads, strict, keys_on_rows=False):
    shape = (width * blk, blk) if keys_on_rows else (heads * blk, width * blk)
    q_idx = lax.broadcasted_iota(jnp.int32, shape, 1 if keys_on_rows else 0)
    k_idx = lax.broadcasted_iota(jnp.int32, shape, 0 if keys_on_rows else 1)
    if heads > 1:
        q_idx = q_idx % blk
    gap = (qi - kb) * blk
    return k_idx < q_idx + gap if strict else k_idx <= q_idx + gap


def mm(a, b, *, name, ta=False, tb=False, res=None, out_dtype=F32, ab=None, bb=None, precision=None):
    a2, b2 = a.shape[-2:], b.shape[-2:]
    (kdim, m) = a2 if ta else a2[::-1]
    (n, k2) = b2 if tb else b2[::-1]
    assert kdim == k2, (a.shape, b.shape, ta, tb)
    assert (ab == 'k') == (bb == 'k')
    kb = ab == 'k'
    nb = a.shape[0] if ab == 'o' else (b.shape[0] if bb == 'o' else None)
    tm, tn = _tile(m, MM_TM), _tile(n, MM_TN)
    tk = kdim if kb else _tile(kdim, MM_TK)
    nk = a.shape[0] if kb else kdim // tk
    dims = ((0 if ta else 1,), (1 if tb else 0,))
    op_dtype = F32 if precision is not None else MXU_DTYPE

    def body(*refs):
        a_ref, b_ref = refs[0], refs[1]
        r_ref = refs[2] if res is not None else None
        o_ref = refs[3] if res is not None else refs[2]
        part = _dot(a_ref[...].astype(op_dtype), b_ref[...].astype(op_dtype), dims, precision)

        def finish(out):
            if res is not None:
                out = out + r_ref[...]
            o_ref[...] = out.astype(out_dtype)

        if nk == 1:
            finish(part)
            return
        acc = refs[-1]
        k = pl.program_id(3)

        @pl.when(k == 0)
        def _():
            acc[...] = part

        @pl.when(k > 0)
        def _():
            acc[...] += part

        @pl.when(k == nk - 1)
        def _():
            finish(acc[...])

    def spec(blk, idx, how):
        if how is None:
            return pl.BlockSpec(blk, idx)
        if how == 'o':
            return pl.BlockSpec((None,) + blk, lambda p, i, j, k: (p,) + idx(p, i, j, k))
        return pl.BlockSpec((None,) + blk, lambda p, i, j, k: (k,) + idx(p, i, j, 0))

    a_spec = spec((tk, tm), lambda p, i, j, k: (k, i), ab) if ta else spec((tm, tk), lambda p, i, j, k: (i, k), ab)
    b_spec = spec((tn, tk), lambda p, i, j, k: (j, k), bb) if tb else spec((tk, tn), lambda p, i, j, k: (k, j), bb)
    o_spec = spec((tm, tn), lambda p, i, j, k: (i, j), None if nb is None else 'o')
    ins, specs = [a, b], [a_spec, b_spec]
    if res is not None:
        ins.append(res)
        specs.append(o_spec)
    out_shape = (m, n) if nb is None else (nb, m, n)
    return _call(body, name=name, out_shape=jax.ShapeDtypeStruct(out_shape, out_dtype),
                 grid=(1 if nb is None else nb, m // tm, n // tn, nk), in_specs=specs, out_specs=o_spec,
                 scratch=[] if nk == 1 else [pltpu.VMEM((tm, tn), F32)],
                 sem=("parallel", "parallel", "parallel", "arbitrary"))(*ins)


def rms_fwd(x, g, *, name, gate=None, out_dtype=F32):
    s, w = x.shape
    bs = _rows(s, w)

    def body(*refs):
        if gate is None:
            x_ref, g_ref, o_ref = refs
            u = x_ref[...]
        else:
            x_ref, z_ref, g_ref, o_ref = refs
            z = z_ref[...]
            u = x_ref[...] * (z * _sigmoid(z))
        r = lax.rsqrt(jnp.mean(u * u, axis=1, keepdims=True) + EPS)
        o_ref[...] = (u * r * g_ref[...]).astype(out_dtype)

    row = pl.BlockSpec((bs, w), lambda i: (i, 0))
    vec = pl.BlockSpec((1, w), lambda i: (0, 0))
    ins = [x] + ([] if gate is None else [gate]) + [g.reshape(1, w)]
    specs = [row] + ([] if gate is None else [row]) + [vec]
    return _call(body, name=name, out_shape=jax.ShapeDtypeStruct((s, w), out_dtype), grid=(s // bs,),
                 in_specs=specs, out_specs=row, sem=("parallel",))(*ins)


def rms_bwd(x, g, dy, *, name, gate=None, add=None):
    s, w = x.shape
    bs = _rows(s, w)

    def body(*refs):
        refs = list(refs)
        x_ref = refs.pop(0)
        z_ref = refs.pop(0) if gate is not None else None
        g_ref = refs.pop(0)
        dy_ref = refs.pop(0)
        add_ref = refs.pop(0) if add is not None else None
        dx_ref = refs.pop(0)
        dz_ref = refs.pop(0) if gate is not None else None
        dg_ref = refs.pop(0)
        i = pl.program_id(0)

        @pl.when(i == 0)
        def _():
            dg_ref[...] = jnp.zeros_like(dg_ref)

        xv = x_ref[...]
        if gate is not None:
            z = z_ref[...]
            sg = _sigmoid(z)
            act = z * sg
            u = xv * act
        else:
            u = xv
        r = lax.rsqrt(jnp.mean(u * u, axis=1, keepdims=True) + EPS)
        dy_v = dy_ref[...]
        dyg = dy_v * g_ref[...]
        du = r * dyg - u * (r * r * r * jnp.mean(dyg * u, axis=1, keepdims=True))
        dg_ref[...] += jnp.sum(dy_v * u * r, axis=0, keepdims=True)
        if gate is not None:
            dx = du * act
            dz_ref[...] = du * xv * (sg * (1.0 + z * (1.0 - sg)))
        else:
            dx = du
        if add is not None:
            dx = dx + add_ref[...]
        dx_ref[...] = dx

    row = pl.BlockSpec((bs, w), lambda i: (i, 0))
    vec = pl.BlockSpec((1, w), lambda i: (0, 0))
    ins = [x] + ([] if gate is None else [gate]) + [g.reshape(1, w), dy] + ([] if add is None else [add])
    specs = [row] + ([] if gate is None else [row]) + [vec, row] + ([] if add is None else [row])
    outs = [jax.ShapeDtypeStruct((s, w), F32)] + ([] if gate is None else [jax.ShapeDtypeStruct((s, w), F32)])
    outs.append(jax.ShapeDtypeStruct((1, w), F32))
    ospecs = [row] + ([] if gate is None else [row]) + [vec]
    return _call(body, name=name, out_shape=outs, grid=(s // bs,), in_specs=specs, out_specs=ospecs,
                 sem=("arbitrary",))(*ins)


def loss_head(y, target, *, name):
    s, w = y.shape
    bs = _rows(s, w)
    nb = s // bs

    def body(y_ref, t_ref, dy_ref, loss_ref, acc):
        i = pl.program_id(0)

        @pl.when(i == 0)
        def _():
            acc[...] = jnp.zeros_like(acc)

        e = y_ref[...] - t_ref[...]
        dy_ref[...] = e * (1.0 / w)
        acc[...] += jnp.sum(e * e, axis=0, keepdims=True)

        @pl.when(i == nb - 1)
        def _():
            loss_ref[...] = jnp.sum(acc[...], axis=1, keepdims=True) * (0.5 / w)

    row = pl.BlockSpec((bs, w), lambda i: (i, 0))
    return _call(body, name=name, out_shape=[jax.ShapeDtypeStruct((s, w), F32), jax.ShapeDtypeStruct((1, 1), F32)],
                 grid=(nb,), in_specs=[row, row], out_specs=[row, pl.BlockSpec((1, 1), lambda i: (0, 0))],
                 scratch=[pltpu.VMEM((1, w), F32)], sem=("arbitrary",))(y, target)


def _rope_tables(positions, s):
    inv_freq = 1.0 / (ROPE_THETA ** (jnp.arange(0, MLA_ROPE, 2, dtype=F32) / MLA_ROPE))
    ang = positions.reshape(s, 1).astype(F32) * inv_freq
    cos, sin = jnp.cos(ang), jnp.sin(ang)
    one, zero = jnp.ones((s, MLA_NOPE), F32), jnp.zeros((s, MLA_NOPE), F32)
    cq = jnp.tile(jnp.concatenate([one, cos, cos], axis=1), (1, MLA_HEADS))
    sq = jnp.tile(jnp.concatenate([zero, sin, sin], axis=1), (1, MLA_HEADS))
    pad1, pad0 = jnp.ones((s, LANES - MLA_ROPE), F32), jnp.zeros((s, LANES - MLA_ROPE), F32)
    ct = jnp.concatenate([cos, cos, pad1], axis=1)
    st = jnp.concatenate([sin, sin, pad0], axis=1)
    half = MLA_ROPE // 2

    def swap(width, starts):
        r = np.zeros((width, width), np.float32)
        for o in starts:
            for i in range(half):
                r[o + half + i, o + i] = -1.0
                r[o + i, o + half + i] = 1.0
        return jnp.asarray(r)

    rq = swap(MLA_HEADS * MLA_QK, [h * MLA_QK + MLA_NOPE for h in range(MLA_HEADS)])
    rt = swap(LANES, [0])
    return (cq, sq, rq), (ct, st, rt)


def rope(x, tabs, *, name, backward=False, add=None):
    cos, sin, rot = tabs
    n, s, w = x.shape
    bs = _rows(s, w)

    def body(*refs):
        if add is None:
            x_ref, c_ref, s_ref, r_ref, o_ref = refs
        else:
            x_ref, c_ref, s_ref, r_ref, a_ref, o_ref = refs
        xv = x_ref[0]
        for j in range(1, n):
            xv = xv + x_ref[j]
        if backward:
            out = xv * c_ref[...] + _nt(xv * s_ref[...], r_ref[...], HIGHEST)
        else:
            out = xv * c_ref[...] + _nn(xv, r_ref[...], HIGHEST) * s_ref[...]
        if add is not None:
            out = out + a_ref[...]
        o_ref[...] = out

    row = pl.BlockSpec((bs, w), lambda i: (i, 0))
    ins = [x, cos, sin, rot] + ([] if add is None else [add])
    specs = [pl.BlockSpec((n, bs, w), lambda i: (0, i, 0)), row, row, pl.BlockSpec((w, w), lambda i: (0, 0))]
    specs += [] if add is None else [row]
    return _call(body, name=name, out_shape=jax.ShapeDtypeStruct((s, w), F32), grid=(s // bs,), in_specs=specs,
                 out_specs=row, sem=("parallel",))(*ins)


MESH = pl.DeviceIdType.MESH
HBM = pl.BlockSpec(memory_space=pltpu.HBM)


def _flip(v, bit):
    return 1 - v if bit else v


class Carried:
    def __init__(self, kind, arrays):
        assert kind in ('gather', 'scatter')
        self.kind, self.arrays, self.n = kind, list(arrays), len(arrays)

    @property
    def out_shape(self):
        lead = (N_DEV,) if self.kind == 'gather' else ()
        return [jax.ShapeDtypeStruct(lead + a.shape, a.dtype) for a in self.arrays]

    @property
    def scratch(self):
        return [pltpu.SemaphoreType.DMA((self.n, N_DEV - 1)), pltpu.SemaphoreType.DMA((self.n, N_DEV - 1)),
                pltpu.SemaphoreType.DMA((self.n,))]

    def _copies(self, in_refs, out_refs, sems):
        send_sems, recv_sems, local_sems = sems
        x, y, c = lax.axis_index("x"), lax.axis_index("y"), lax.axis_index("c")
        me = 4 * x + 2 * y + c
        part = (lambda b, p: in_refs[b]) if self.kind == 'gather' else (lambda b, p: in_refs[b].at[p])
        local = [pltpu.make_async_copy(part(b, me), out_refs[b].at[me], local_sems.at[b]) for b in range(self.n)]
        remote = []
        for k in range(1, N_DEV):
            px, py, pc = _flip(x, k & 4), _flip(y, k & 2), _flip(c, k & 1)
            for b in range(self.n):
                remote.append(pltpu.make_async_remote_copy(
                    src_ref=part(b, 4 * px + 2 * py + pc), dst_ref=out_refs[b].at[me], send_sem=send_sems.at[b, k - 1],
                    recv_sem=recv_sems.at[b, k - 1], device_id=(px, py, pc), device_id_type=MESH))
        return local, remote

    def start(self, in_refs, out_refs, sems):
        local, remote = self._copies(in_refs, out_refs, sems)
        for cp in local + remote:
            cp.start()

    def wait(self, in_refs, out_refs, sems):
        local, remote = self._copies(in_refs, out_refs, sems)
        for cp in remote:
            cp.wait_send()
            cp.wait_recv()
        for cp in local:
            cp.wait()


def _ride(carried, refs, n_in, n_out):
    n = 0 if carried is None else carried.n
    own_in, ride_in = refs[:n_in], refs[n_in:n_in + n]
    own_out, ride_out = refs[n_in + n:n_in + n + n_out], refs[n_in + n + n_out:n_in + 2 * n + n_out]
    return own_in, own_out, (ride_in, ride_out, refs[n_in + 2 * n + n_out:])


def _tri(n, op):
    r = lax.broadcasted_iota(jnp.int32, (n, n), 0)
    c = lax.broadcasted_iota(jnp.int32, (n, n), 1)
    return r, c, op(r, c)


def _pair_split(x, first):
    zero = jnp.zeros_like(x)
    return jnp.where(first, x, zero), jnp.where(first, zero, x)


def _sb_specs(s, blk):
    npair = SB_WIDTH // LANES
    q = pl.BlockSpec((blk, LANES), lambda j, i: (i, j))
    k = pl.BlockSpec((s, LANES), lambda j, i: (0, npair + j))
    v = pl.BlockSpec((s, LANES), lambda j, i: (0, 2 * npair + j))
    full = pl.BlockSpec((s, LANES), lambda j, i: (0, j))
    return npair, q, k, v, full


def _stack_heads(x, first):
    return jnp.concatenate(_pair_split(x, first), axis=0)


def _unstack_heads(x, first, blk):
    return jnp.where(first, x[:blk], x[blk:])


def sb_fwd(qkv, *, name, carried=None):
    s = qkv.shape[0]
    blk = _tile(s, (ATT_BLK,))
    scale = SB_DIM ** -0.5
    npair, nq = SB_WIDTH // LANES, s // blk
    assert nq % SB_WIDE == 0

    def body(*refs):
        (q_ref, k_ref, v_ref), (y_ref, t_ref), ride = _ride(carried, refs, 3, 2)
        pair, qi = pl.program_id(0), pl.program_id(1)
        if carried is not None:
            @pl.when((pair == 0) & (qi == 0))
            def _():
                carried.start(*ride)

        first = lax.broadcasted_iota(jnp.int32, (blk, LANES), 1) < SB_DIM
        q2 = _stack_heads((q_ref[...].astype(F32) * scale).astype(MXU_DTYPE), first)
        row, col, later_mask = _tri(blk, lambda r, c: r > c)
        u_later = later_mask.astype(MXU_DTYPE)
        n = SB_WIDE

        def tile(kb, carry, masked):
            c, acc = carry
            keys = pl.ds(pl.multiple_of(kb * blk, blk), n * blk)
            z = _nt(q2, k_ref[keys, :])
            sp = _softplus_att(z)
            if masked:
                valid = _causal_mask(blk, n, qi, kb, 2, True)
            spm = jnp.where(valid, sp, 0.0) if masked else sp
            later = _cum(spm, u_later)
            sums = [jnp.sum(spm[:, t * blk:(t + 1) * blk], axis=1, keepdims=True) for t in range(n)]
            after, cols = c, [None] * n
            for t in reversed(range(n)):
                cols[t] = jnp.broadcast_to(after, (2 * blk, blk))
                after = after - sums[t]
            w = jnp.exp((z - sp) - later + (cols[0] if n == 1 else jnp.concatenate(cols, axis=1)))
            if masked:
                w = jnp.where(valid, w, 0.0)
            return after, acc + _nn(w.astype(MXU_DTYPE), v_ref[keys, :])

        zero = (jnp.zeros((2 * blk, 1), F32), jnp.zeros((2 * blk, LANES), F32))
        c, acc = _causal_loop(qi, tile, zero, True, SB_WIDE)
        y_ref[...] = _unstack_heads(acc, first, blk)
        t_ref[...] = _unstack_heads(c, first, blk)
        if carried is not None:
            @pl.when((pair == npair - 1) & (qi == nq - 1))
            def _():
                carried.wait(*ride)

    _, qspec, kspec, vspec, _ = _sb_specs(s, blk)
    out = jax.ShapeDtypeStruct((s, SB_WIDTH), F32)
    extra = [] if carried is None else carried.arrays
    return _call(body, name=name, out_shape=[out, out] + ([] if carried is None else carried.out_shape),
                 grid=(npair, nq), in_specs=[qspec, kspec, vspec] + [HBM] * len(extra),
                 out_specs=[qspec, qspec] + [HBM] * len(extra), scratch=[] if carried is None else carried.scratch,
                 sem=("arbitrary", "arbitrary"))(qkv, qkv, qkv, *extra)


def sb_bwd(qkv, dy, tot, *, name, carried=None):
    s = qkv.shape[0]
    blk = _tile(s, (ATT_BLK,))
    scale = SB_DIM ** -0.5
    npair, nq = SB_WIDTH // LANES, s // blk
    assert nq % SB_WIDE == 0

    def body(*refs):
        (q_ref, k_ref, v_ref, dy_ref, t_ref), (dq_ref, dk_ref, dv_ref), ride = _ride(carried, refs, 5, 3)
        pair, qi = pl.program_id(0), pl.program_id(1)
        if carried is not None:
            @pl.when((pair == 0) & (qi == 0))
            def _():
                carried.start(*ride)

        @pl.when(qi == 0)
        def _():
            dk_ref[...] = jnp.zeros_like(dk_ref)
            dv_ref[...] = jnp.zeros_like(dv_ref)

        first = lax.broadcasted_iota(jnp.int32, (blk, LANES), 1) < SB_DIM
        q2 = _stack_heads((q_ref[...].astype(F32) * scale).astype(MXU_DTYPE), first)
        dy2 = _stack_heads(dy_ref[...].astype(MXU_DTYPE), first)
        tv = jnp.concatenate([t_ref[:, 0:1], t_ref[:, SB_DIM:SB_DIM + 1]], axis=0)
        row, col, incl_mask = _tri(blk, lambda r, c: r <= c)
        u_incl = incl_mask.astype(MXU_DTYPE)
        u_excl = (row < col).astype(MXU_DTYPE)
        n = SB_WIDE

        def prefixed(x, carry):
            cols = []
            for t in range(n):
                cols.append(jnp.broadcast_to(carry, (2 * blk, blk)))
                carry = carry + jnp.sum(x[:, t * blk:(t + 1) * blk], axis=1, keepdims=True)
            return (cols[0] if n == 1 else jnp.concatenate(cols, axis=1)), carry

        def tile(kb, carry, masked):
            p, gc, dq = carry
            keys = pl.ds(pl.multiple_of(kb * blk, blk), n * blk)
            kv = k_ref[keys, :]
            z = _nt(q2, kv)
            dw = _nt(dy2, v_ref[keys, :])
            sp = _softplus_att(z)
            if masked:
                valid = _causal_mask(blk, n, qi, kb, 2, True)
            spm = jnp.where(valid, sp, 0.0) if masked else sp
            before, p = prefixed(spm, p)
            w = jnp.exp((z - sp) + (_cum(spm, u_incl) + before))
            if masked:
                w = jnp.where(valid, w, 0.0)
            g = w * dw
            gbefore, gc = prefixed(g, gc)
            gb = g.astype(MXU_DTYPE)
            gin = _nn(jnp.concatenate([gb[:, t * blk:(t + 1) * blk] for t in range(n)], axis=0), u_excl)
            gex = gbefore + jnp.concatenate([gin[t * 2 * blk:(t + 1) * 2 * blk] for t in range(n)], axis=1)
            keep = jnp.exp(-spm)
            dz = keep * (g + gex) - gex
            if masked:
                dz = jnp.where(valid, dz, 0.0)
            dzb = dz.astype(MXU_DTYPE)
            dk_ref[keys, :] += _tn(dzb, q2)
            dv_ref[keys, :] += _tn(w.astype(MXU_DTYPE), dy2)
            return p, gc, dq + _nn(dzb, kv)

        zero = jnp.zeros((2 * blk, 1), F32)
        _, _, dq = _causal_loop(qi, tile, (tv, zero, jnp.zeros((2 * blk, LANES), F32)), False, SB_WIDE)
        dq_ref[...] = _unstack_heads(dq, first, blk) * scale
        if carried is not None:
            @pl.when((pair == npair - 1) & (qi == nq - 1))
            def _():
                carried.wait(*ride)

    _, qspec, kspec, vspec, full = _sb_specs(s, blk)
    out = jax.ShapeDtypeStruct((s, SB_WIDTH), F32)
    extra = [] if carried is None else carried.arrays
    return _call(body, name=name, out_shape=[out, out, out] + ([] if carried is None else carried.out_shape),
                 grid=(npair, nq), in_specs=[qspec, kspec, vspec, qspec, qspec] + [HBM] * len(extra),
                 out_specs=[qspec, full, full] + [HBM] * len(extra), scratch=[] if carried is None else carried.scratch,
                 sem=("arbitrary", "arbitrary"))(qkv, qkv, qkv, dy, tot, *extra)


ATT_PAIR = 2


def _mla_specs(s, blk, dk, dv):
    q = pl.BlockSpec((ATT_PAIR, blk, dk), lambda hp, i: (hp, i, 0))
    k = pl.BlockSpec((ATT_PAIR, s, dk), lambda hp, i: (hp, 0, 0))
    v = pl.BlockSpec((ATT_PAIR, s, dv), lambda hp, i: (hp, 0, 0))
    y = pl.BlockSpec((ATT_PAIR, blk, dv), lambda hp, i: (hp, i, 0))
    lse = pl.BlockSpec((ATT_PAIR, blk, LANES), lambda hp, i: (hp, i, 0))
    return q, k, v, y, lse


def mla_fwd(q, k, v, *, name):
    h, s, dk = q.shape
    dv = v.shape[-1]
    blk = _tile(s, (ATT_BLK,))
    scale = dk ** -0.5
    assert (s // blk) % MLA_WIDE == 0

    def body(q_ref, k_ref, v_ref, y_ref, l_ref):
        qi = pl.program_id(1)
        n = MLA_WIDE

        def tile(kb, carry, masked):
            keys = pl.ds(pl.multiple_of(kb * blk, blk), n * blk)
            out = []
            for hh in range(ATT_PAIR):
                m, l, acc = carry[hh]
                sc = _nt(q_ref[hh], k_ref[hh, keys, :]) * scale
                if masked:
                    sc = jnp.where(_causal_mask(blk, n, qi, kb, 1, False), sc, -1e30)
                m2 = jnp.maximum(m, jnp.max(sc, axis=1, keepdims=True))
                p = jnp.exp(sc - m2)
                a = jnp.exp(m - m2)
                out.append((m2, a * l + jnp.sum(p, axis=1, keepdims=True),
                            a * acc + _nn(p.astype(MXU_DTYPE), v_ref[hh, keys, :])))
            return tuple(out)

        init = (jnp.full((blk, 1), -1e30, F32), jnp.zeros((blk, 1), F32), jnp.zeros((blk, dv), F32))
        for hh, (m, l, acc) in enumerate(_causal_loop(qi, tile, (init,) * ATT_PAIR, False, MLA_WIDE)):
            y_ref[hh] = acc / l
            l_ref[hh] = jnp.broadcast_to(m + jnp.log(l), (blk, LANES))

    qspec, kspec, vspec, yspec, lspec = _mla_specs(s, blk, dk, dv)
    return _call(body, name=name,
                 out_shape=[jax.ShapeDtypeStruct((h, s, dv), F32), jax.ShapeDtypeStruct((h, s, LANES), F32)],
                 grid=(h // ATT_PAIR, s // blk), in_specs=[qspec, kspec, vspec], out_specs=[yspec, lspec],
                 sem=("parallel", "arbitrary"))(q, k, v)


def mla_bwd(q, k, v, y, dy, lse, *, name):
    h, s, dk = q.shape
    dv = v.shape[-1]
    blk = _tile(s, (ATT_BLK,))
    scale = dk ** -0.5
    assert (s // blk) % MLA_WIDE == 0

    def body(q_ref, k_ref, v_ref, y_ref, dy_ref, l_ref, dq_ref, dk_ref, dv_ref):
        qi = pl.program_id(1)

        @pl.when(qi == 0)
        def _():
            dk_ref[...] = jnp.zeros_like(dk_ref)
            dv_ref[...] = jnp.zeros_like(dv_ref)

        as_row = lambda col: jnp.transpose(jnp.broadcast_to(col, (blk, LANES)))[0:1, :]
        dyv = [dy_ref[hh].astype(MXU_DTYPE) for hh in range(ATT_PAIR)]
        delta = [as_row(jnp.sum(dy_ref[hh] * y_ref[hh], axis=1, keepdims=True)) for hh in range(ATT_PAIR)]
        lv = [as_row(l_ref[hh, :, 0:1]) for hh in range(ATT_PAIR)]
        n = MLA_WIDE

        def tile(kb, dqs, masked):
            keys = pl.ds(pl.multiple_of(kb * blk, blk), n * blk)
            out = []
            for hh in range(ATT_PAIR):
                qv = q_ref[hh]
                kv = k_ref[hh, keys, :]
                p = jnp.exp(_nt(kv, qv) * scale - lv[hh])
                if masked:
                    p = jnp.where(_causal_mask(blk, n, qi, kb, 1, False, keys_on_rows=True), p, 0.0)
                ds = (p * (_nt(v_ref[hh, keys, :], dyv[hh]) - delta[hh])).astype(MXU_DTYPE)
                dk_ref[hh, keys, :] += _nn(ds, qv) * scale
                dv_ref[hh, keys, :] += _nn(p.astype(MXU_DTYPE), dyv[hh])
                out.append(dqs[hh] + _tn(ds, kv))
            return tuple(out)

        for hh, dq in enumerate(_causal_loop(qi, tile, (jnp.zeros((blk, dk), F32),) * ATT_PAIR, False, MLA_WIDE)):
            dq_ref[hh] = dq * scale

    qspec, kspec, vspec, yspec, lspec = _mla_specs(s, blk, dk, dv)
    return _call(body, name=name,
                 out_shape=[jax.ShapeDtypeStruct((h, s, dk), F32), jax.ShapeDtypeStruct((h, s, dk), F32),
                            jax.ShapeDtypeStruct((h, s, dv), F32)],
                 grid=(h // ATT_PAIR, s // blk), in_specs=[qspec, kspec, vspec, yspec, yspec, lspec],
                 out_specs=[qspec, kspec, vspec], sem=("parallel", "arbitrary"))(q, k, v, y, dy, lse)


HALO = 8
CONV_CHUNK = 16


def _conv_tiles(x):
    s, c = x.shape[-2:]
    return s, c, _tile(s, (ROW_BLK,)), _tile(c, (CONV_COLS,))


def _conv_specs(bs, cw, lead=()):
    zero = (0,) * len(lead)
    blk = pl.BlockSpec(lead + (None, bs, cw), lambda p, j, i: zero + (p, i, j))
    halo = pl.BlockSpec(lead + (None, HALO, cw), lambda p, j, i: zero + (p, jnp.maximum(i * (bs // HALO) - 1, 0), j))
    w = lambda kk: pl.BlockSpec(lead + (None, kk, cw), lambda p, j, i: zero + (p, 0, j))
    return blk, halo, w


def _stage(scr, x_ref, halo_ref, first):
    scr[0:HALO, :] = jnp.where(first, 0.0, halo_ref[...])
    scr[HALO:, :] = x_ref[...]


def _conv_taps(scr, kk, r0):
    return [scr[pl.ds(HALO - (kk - 1) + k + r0, CONV_CHUNK), :] for k in range(kk)]


def _conv_sum(taps, w_ref, b_ref):
    u = b_ref[...] + taps[0] * w_ref[0:1, :]
    for k in range(1, len(taps)):
        u = u + taps[k] * w_ref[k:k + 1, :]
    return u


def _fold(x):
    out = x[0:8]
    for r in range(8, CONV_CHUNK, 8):
        out = out + x[r:r + 8]
    return out


class _TapSums:
    def __init__(self, kk, cw):
        self.w = [jnp.zeros((8, cw), F32) for _ in range(kk)]
        self.b = jnp.zeros((8, cw), F32)

    def add(self, du, taps):
        self.w = [a + _fold(du * t) for a, t in zip(self.w, taps)]
        self.b = self.b + _fold(du)

    def flush(self, dw_ref, db_ref):
        for k, a in enumerate(self.w):
            dw_ref[k:k + 1, :] += jnp.sum(a, axis=0, keepdims=True)
        db_ref[...] += jnp.sum(self.b, axis=0, keepdims=True)


def _silu_grad(u):
    sg = _sigmoid(u)
    return sg * (1.0 + u * (1.0 - sg))


def conv_silu_fwd(x, w, b, *, name):
    s, c, bs, cw = _conv_tiles(x)
    kk = w.shape[1]

    def body(x_ref, h_ref, w_ref, b_ref, o_ref, scr):
        _stage(scr, x_ref, h_ref, pl.program_id(2) == 0)
        for r0 in range(0, bs, CONV_CHUNK):
            u = _conv_sum(_conv_taps(scr, kk, r0), w_ref, b_ref)
            o_ref[pl.ds(r0, CONV_CHUNK), :] = u * _sigmoid(u)

    blk, halo, wspec = _conv_specs(bs, cw)
    return _call(body, name=name, out_shape=jax.ShapeDtypeStruct(x.shape, F32), grid=(x.shape[0], c // cw, s // bs),
                 in_specs=[blk, halo, wspec(kk), wspec(1)], out_specs=blk, scratch=[pltpu.VMEM((bs + HALO, cw), F32)],
                 sem=("parallel", "parallel", "arbitrary"))(x, x, w, b)


def conv_silu_bwd(x, dy, w, b, *, name):
    s, c, bs, cw = _conv_tiles(x)
    kk = w.shape[1]

    def body(x_ref, h_ref, w_ref, b_ref, dy_ref, du_ref, dw_ref, db_ref, scr):
        i = pl.program_id(2)

        @pl.when(i == 0)
        def _():
            dw_ref[...] = jnp.zeros_like(dw_ref)
            db_ref[...] = jnp.zeros_like(db_ref)

        _stage(scr, x_ref, h_ref, i == 0)
        sums = _TapSums(kk, cw)
        for r0 in range(0, bs, CONV_CHUNK):
            taps = _conv_taps(scr, kk, r0)
            du = dy_ref[pl.ds(r0, CONV_CHUNK), :] * _silu_grad(_conv_sum(taps, w_ref, b_ref))
            du_ref[pl.ds(r0, CONV_CHUNK), :] = du
            sums.add(du, taps)
        sums.flush(dw_ref, db_ref)

    blk, halo, wspec = _conv_specs(bs, cw)
    return _call(body, name=name,
                 out_shape=[jax.ShapeDtypeStruct(x.shape, F32), jax.ShapeDtypeStruct(w.shape, F32),
                            jax.ShapeDtypeStruct(b.shape, F32)],
                 grid=(x.shape[0], c // cw, s // bs), in_specs=[blk, halo, wspec(kk), wspec(1), blk],
                 out_specs=[blk, wspec(kk), wspec(1)], scratch=[pltpu.VMEM((bs + HALO, cw), F32)],
                 sem=("parallel", "parallel", "arbitrary"))(x, x, w, b, dy)


def _glu_view(a):
    return a.reshape((2, a.shape[0] // 2) + a.shape[1:])


def conv_glu_fwd(x, w, b, *, name):
    s, c, bs, cw = _conv_tiles(x)
    kk = w.shape[1]
    half = x.shape[0] // 2

    def body(x_ref, h_ref, w_ref, b_ref, o_ref, gscr, vscr):
        first = pl.program_id(2) == 0
        _stage(gscr, x_ref.at[0], h_ref.at[0], first)
        _stage(vscr, x_ref.at[1], h_ref.at[1], first)
        for r0 in range(0, bs, CONV_CHUNK):
            gate = _conv_sum(_conv_taps(gscr, kk, r0), w_ref.at[0], b_ref.at[0])
            val = _conv_sum(_conv_taps(vscr, kk, r0), w_ref.at[1], b_ref.at[1])
            o_ref[pl.ds(r0, CONV_CHUNK), :] = (gate * _sigmoid(gate) * val).astype(o_ref.dtype)

    blk, halo, wspec = _conv_specs(bs, cw, lead=(2,))
    out, _, _ = _conv_specs(bs, cw)
    xv = _glu_view(x)
    return _call(body, name=name, out_shape=jax.ShapeDtypeStruct((half, s, c), MXU_DTYPE), grid=(half, c // cw, s // bs),
                 in_specs=[blk, halo, wspec(kk), wspec(1)], out_specs=out, scratch=[pltpu.VMEM((bs + HALO, cw), F32)] * 2,
                 sem=("parallel", "parallel", "arbitrary"))(xv, xv, _glu_view(w), _glu_view(b))


def conv_glu_bwd(x, da, w, b, *, name):
    s, c, bs, cw = _conv_tiles(x)
    kk = w.shape[1]
    half = x.shape[0] // 2

    def body(x_ref, h_ref, w_ref, b_ref, da_ref, du_ref, dw_ref, db_ref, gscr, vscr):
        i = pl.program_id(2)

        @pl.when(i == 0)
        def _():
            dw_ref[...] = jnp.zeros_like(dw_ref)
            db_ref[...] = jnp.zeros_like(db_ref)

        _stage(gscr, x_ref.at[0], h_ref.at[0], i == 0)
        _stage(vscr, x_ref.at[1], h_ref.at[1], i == 0)
        gsums, vsums = _TapSums(kk, cw), _TapSums(kk, cw)
        for r0 in range(0, bs, CONV_CHUNK):
            gtaps, vtaps = _conv_taps(gscr, kk, r0), _conv_taps(vscr, kk, r0)
            gate = _conv_sum(gtaps, w_ref.at[0], b_ref.at[0])
            val = _conv_sum(vtaps, w_ref.at[1], b_ref.at[1])
            dav = da_ref[pl.ds(r0, CONV_CHUNK), :]
            dgate = dav * val * _silu_grad(gate)
            dval = dav * gate * _sigmoid(gate)
            du_ref[0, pl.ds(r0, CONV_CHUNK), :] = dgate
            du_ref[1, pl.ds(r0, CONV_CHUNK), :] = dval
            gsums.add(dgate, gtaps)
            vsums.add(dval, vtaps)
        gsums.flush(dw_ref.at[0], db_ref.at[0])
        vsums.flush(dw_ref.at[1], db_ref.at[1])

    blk, halo, wspec = _conv_specs(bs, cw, lead=(2,))
    daspec, _, _ = _conv_specs(bs, cw)
    xv, wv, bv = _glu_view(x), _glu_view(w), _glu_view(b)
    du, dw, db = _call(body, name=name,
                       out_shape=[jax.ShapeDtypeStruct(xv.shape, F32), jax.ShapeDtypeStruct(wv.shape, F32),
                                  jax.ShapeDtypeStruct(bv.shape, F32)],
                       grid=(half, c // cw, s // bs), in_specs=[blk, halo, wspec(kk), wspec(1), daspec],
                       out_specs=[blk, wspec(kk), wspec(1)], scratch=[pltpu.VMEM((bs + HALO, cw), F32)] * 2,
                       sem=("parallel", "parallel", "arbitrary"))(xv, xv, wv, bv, da)
    return du.reshape(x.shape), dw.reshape(w.shape), db.reshape(b.shape)


def conv_t(du, w, *, name):
    s, c, bs, cw = _conv_tiles(du)
    kk = w.shape[1]
    nb = s // bs

    def body(d_ref, h_ref, w_ref, o_ref, scr):
        last = pl.program_id(2) == nb - 1
        scr[0:bs, :] = d_ref[...]
        scr[bs:, :] = jnp.where(last, 0.0, h_ref[...])
        for r0 in range(0, bs, CONV_CHUNK):
            acc = scr[pl.ds(r0 + kk - 1, CONV_CHUNK), :] * w_ref[0:1, :]
            for k in range(1, kk):
                acc = acc + scr[pl.ds(r0 + kk - 1 - k, CONV_CHUNK), :] * w_ref[k:k + 1, :]
            o_ref[pl.ds(r0, CONV_CHUNK), :] = acc

    blk, _, wspec = _conv_specs(bs, cw)
    halo = pl.BlockSpec((None, HALO, cw), lambda q, j, i: (q, jnp.minimum((i + 1) * (bs // HALO), s // HALO - 1), j))
    return _call(body, name=name, out_shape=jax.ShapeDtypeStruct(du.shape, F32), grid=(du.shape[0], c // cw, nb),
                 in_specs=[blk, halo, wspec(kk)], out_specs=blk, scratch=[pltpu.VMEM((bs + HALO, cw), F32)],
                 sem=("parallel", "parallel", "arbitrary"))(du, du, w)


def _ssd_common(xbc_ref, tail_ref, dtrt_ref, bias_ref, biast_ref, alog_ref, alogt_ref):
    L = SSM_CHUNK
    raw = tail_ref[...] + bias_ref[...]
    dt = _softplus(raw)
    dtt = _softplus(dtrt_ref[...] + biast_ref[...])
    a = -jnp.exp(alog_ref[...])
    at = -jnp.exp(alogt_ref[...])
    row, col, lower = _tri(L, lambda r, c: r >= c)
    tril = lower.astype(F32)
    cs = _nn(tril, dt * a, HIGHEST)
    cst = _nt(dtt * at, tril, HIGHEST)
    bm = [xbc_ref[:, SSM_INNER + g * SSM_N: SSM_INNER + (g + 1) * SSM_N] for g in range(SSM_GROUPS)]
    off = SSM_INNER + SSM_GROUPS * SSM_N
    cm = [xbc_ref[:, off + g * SSM_N: off + (g + 1) * SSM_N] for g in range(SSM_GROUPS)]
    cb = [_bnt(cm[g], bm[g]) for g in range(SSM_GROUPS)]
    return raw, dt, a, lower, tril, cs, cst, bm, cm, cb


def _ssd_head(hh, xbc_ref, dt, cs, cst, lower):
    L = SSM_CHUNK
    ln = DT_LANE + hh
    x = xbc_ref[:, hh * SSM_P:(hh + 1) * SSM_P]
    dtc = dt[:, ln:ln + 1]
    csc = cs[:, ln:ln + 1]
    csr = cst[hh:hh + 1, :]
    decay = jnp.exp(jnp.where(lower, csc - csr, -1e30))
    last = cs[L - 1:L, ln:ln + 1]
    return x, dtc, csc, decay, jnp.exp(csc), jnp.exp(last - csc), jnp.exp(last)


def _ssd_inputs(tail, dt_bias, a_log, d_skip):
    H = SSM_HEADS
    lanes = lambda vec: jnp.pad(vec.reshape(1, H), ((0, 0), (DT_LANE, LANES - DT_LANE - H)))
    return (tail, tail[:, DT_LANE:DT_LANE + H].T, lanes(dt_bias), dt_bias.reshape(H, 1), lanes(a_log),
            a_log.reshape(H, 1), lanes(d_skip))


def ssd_fwd(xbc, tail, dt_bias, a_log, d_skip, *, name):
    s = xbc.shape[0]
    L, H, P, N = SSM_CHUNK, SSM_HEADS, SSM_P, SSM_N
    nc = s // L

    def body(xbc_ref, tail_ref, dtrt_ref, bias_ref, biast_ref, alog_ref, alogt_ref, d_ref, y_ref, hp_ref, state):
        @pl.when(pl.program_id(0) == 0)
        def _():
            state[...] = jnp.zeros_like(state)

        raw, dt, a, lower, tril, cs, cst, bm, cm, cb = _ssd_common(
            xbc_ref, tail_ref, dtrt_ref, bias_ref, biast_ref, alog_ref, alogt_ref)
        for hh in range(H):
            g = hh // (H // SSM_GROUPS)
            x, dtc, csc, decay, e, tau, gamma = _ssd_head(hh, xbc_ref, dt, cs, cst, lower)
            xdt = x * dtc
            hprev = state[hh]
            hp_ref[hh] = hprev
            skip = d_ref[:, DT_LANE + hh:DT_LANE + hh + 1]
            y = _bnn(cb[g] * decay, xdt) + _bnn(cm[g], hprev) * e + x * skip
            y_ref[:, hh * P:(hh + 1) * P] = y
            state[hh] = hprev * gamma + _btn(bm[g] * tau, xdt)

    row = lambda w: pl.BlockSpec((L, w), lambda c: (c, 0))
    small = lambda shp: pl.BlockSpec(shp, lambda c: (0, 0))
    return _call(body, name=name,
                 out_shape=[jax.ShapeDtypeStruct((s, SSM_INNER), F32), jax.ShapeDtypeStruct((nc, H, N, P), F32)],
                 grid=(nc,),
                 in_specs=[row(SSM_CONV_DIM), row(LANES), pl.BlockSpec((H, L), lambda c: (0, c)), small((1, LANES)),
                           small((H, 1)), small((1, LANES)), small((H, 1)), small((1, LANES))],
                 out_specs=[row(SSM_INNER), pl.BlockSpec((None, H, N, P), lambda c: (c, 0, 0, 0))],
                 scratch=[pltpu.VMEM((H, N, P), F32)], sem=("arbitrary",))(xbc, *_ssd_inputs(tail, dt_bias, a_log, d_skip))


def ssd_bwd(xbc, tail, dt_bias, a_log, d_skip, hprev_all, dy, *, name):
    s = xbc.shape[0]
    L, H, P, N = SSM_CHUNK, SSM_HEADS, SSM_P, SSM_N
    nc = s // L
    hg = H // SSM_GROUPS

    def body(xbc_ref, tail_ref, dtrt_ref, bias_ref, biast_ref, alog_ref, alogt_ref, d_ref, hp_ref, dy_ref,
             dxbc_ref, ddt_ref, dbias_ref, dalog_ref, dd_ref, dstate):
        @pl.when(pl.program_id(0) == 0)
        def _():
            dstate[...] = jnp.zeros_like(dstate)
            dbias_ref[...] = jnp.zeros_like(dbias_ref)
            dalog_ref[...] = jnp.zeros_like(dalog_ref)
            dd_ref[...] = jnp.zeros_like(dd_ref)

        raw, dt, a, lower, tril, cs, cst, bm, cm, cb = _ssd_common(
            xbc_ref, tail_ref, dtrt_ref, bias_ref, biast_ref, alog_ref, alogt_ref)
        lane = lax.broadcasted_iota(jnp.int32, (L, LANES), 1)
        lane1 = lax.broadcasted_iota(jnp.int32, (1, LANES), 1)
        rowi = lax.broadcasted_iota(jnp.int32, (L, 1), 0)
        slot = lax.broadcasted_iota(jnp.int32, (LANES, L), 0)
        col_sums = jnp.zeros((LANES, L), F32)
        dcs_all = jnp.zeros((L, LANES), F32)
        ddt_x = jnp.zeros((L, LANES), F32)
        dd_row = jnp.zeros((1, LANES), F32)
        dbm = [jnp.zeros((L, N), F32) for _ in range(SSM_GROUPS)]
        dcm = [jnp.zeros((L, N), F32) for _ in range(SSM_GROUPS)]
        dcb = [jnp.zeros((L, L), F32) for _ in range(SSM_GROUPS)]
        for hh in range(H):
            g = hh // hg
            ln = DT_LANE + hh
            x, dtc, csc, decay, e, tau, gamma = _ssd_head(hh, xbc_ref, dt, cs, cst, lower)
            xdt = x * dtc
            hprev = hp_ref[hh]
            dhn = dstate[hh]
            dyh = dy_ref[:, hh * P:(hh + 1) * P]
            m = cb[g] * decay
            dxdt = _btn(m, dyh) + _bnn(bm[g] * tau, dhn)
            dm = jnp.where(lower, _bnt(dyh, xdt), 0.0)
            dcb[g] = dcb[g] + dm * decay
            dseg = dm * m
            dcs = jnp.sum(dseg, axis=1, keepdims=True)
            col_sums = jnp.where(slot == ln, jnp.sum(dseg, axis=0, keepdims=True), col_sums)
            edy = e * dyh
            dcm[g] = dcm[g] + _bnt(edy, hprev)
            dcs = dcs + e * jnp.sum(dyh * _bnn(cm[g], hprev), axis=1, keepdims=True)
            xdh = _bnt(xdt, dhn)
            dbm[g] = dbm[g] + tau * xdh
            dtau_tau = jnp.sum(bm[g] * xdh, axis=1, keepdims=True) * tau
            dlast = jnp.sum(dtau_tau, axis=0, keepdims=True) + gamma * jnp.sum(dhn * hprev, keepdims=True)
            dcs = dcs - dtau_tau + jnp.where(rowi == L - 1, dlast, 0.0)
            dstate[hh] = gamma * dhn + _btn(cm[g], edy)
            dcs_all = jnp.where(lane == ln, dcs, dcs_all)
            ddt_x = jnp.where(lane == ln, jnp.sum(dxdt * x, axis=1, keepdims=True), ddt_x)
            dxbc_ref[:, hh * P:(hh + 1) * P] = dxdt * dtc + d_ref[:, ln:ln + 1] * dyh
            dd_row = jnp.where(lane1 == ln, jnp.sum(dyh * x, keepdims=True), dd_row)
        off = SSM_INNER + SSM_GROUPS * SSM_N
        for g in range(SSM_GROUPS):
            dxbc_ref[:, SSM_INNER + g * N: SSM_INNER + (g + 1) * N] = dbm[g] + _btn(dcb[g], cm[g])
            dxbc_ref[:, off + g * N: off + (g + 1) * N] = dcm[g] + _bnn(dcb[g], bm[g])
        dcs_all = dcs_all - jnp.transpose(col_sums)
        dda = _tn(tril, dcs_all, HIGHEST)
        head_lane = (lane >= DT_LANE) & (lane < DT_LANE + H)
        draw = jnp.where(head_lane, (dda * a + ddt_x) * _sigmoid(raw), 0.0)
        ddt_ref[...] = draw
        dbias_ref[...] += jnp.sum(draw, axis=0, keepdims=True)
        dalog_ref[...] += jnp.sum(jnp.where(head_lane, dda * dt, 0.0), axis=0, keepdims=True) * a
        dd_ref[...] += dd_row

    rev = lambda c: nc - 1 - c
    row = lambda w: pl.BlockSpec((L, w), lambda c: (rev(c), 0))
    small = lambda shp: pl.BlockSpec(shp, lambda c: (0, 0))
    acc = pl.BlockSpec((1, LANES), lambda c: (0, 0))
    vec = jax.ShapeDtypeStruct((1, LANES), F32)
    return _call(body, name=name,
                 out_shape=[jax.ShapeDtypeStruct((s, SSM_CONV_DIM), F32), jax.ShapeDtypeStruct((s, LANES), F32), vec, vec, vec],
                 grid=(nc,),
                 in_specs=[row(SSM_CONV_DIM), row(LANES), pl.BlockSpec((H, L), lambda c: (0, rev(c))), small((1, LANES)),
                           small((H, 1)), small((1, LANES)), small((H, 1)), small((1, LANES)),
                           pl.BlockSpec((None, H, N, P), lambda c: (rev(c), 0, 0, 0)), row(SSM_INNER)],
                 out_specs=[row(SSM_CONV_DIM), row(LANES), acc, acc, acc],
                 scratch=[pltpu.VMEM((H, N, P), F32)], sem=("arbitrary",))(
        xbc, *_ssd_inputs(tail, dt_bias, a_log, d_skip), hprev_all, dy)


def _heads(x2d, n, d):
    s = x2d.shape[0]
    return x2d.reshape(s, n, d).transpose(1, 0, 2)


def _unheads(x3d):
    n, s, d = x3d.shape
    return x3d.transpose(1, 0, 2).reshape(s, n * d)


def layer_fwd(h, p, tabs, li, gather_late=None):
    s = h.shape[0]
    tabq, tabt = tabs
    nm = lambda t: f"L{li}_{t}"
    r = {'h': h}
    hn = rms_fwd(h, p['mix_norm'], name=nm('mixnorm'), out_dtype=MXU_DTYPE)
    proj = mm(hn, p['w_in'], name=nm('proj'))
    r.update(hn=hn, proj=proj)
    qkv = proj[:, :3 * SB_WIDTH].astype(MXU_DTYPE)
    late = None
    if gather_late is None:
        ya, tot = sb_fwd(qkv, name=nm('sb_fwd'))
    else:
        ya, tot, *got = sb_fwd(qkv, name=nm('sb_fwd'), carried=Carried('gather', [gather_late[n] for n in LATE]))
        late = assemble_late(dict(zip(LATE, got)))
        p = dict(p, **{n: late[n][li] for n in LATE})
    yan = rms_fwd(ya, p['sb_out_norm'], name=nm('sbnorm'), out_dtype=MXU_DTYPE)
    r.update(qkv=qkv, ya=ya, tot=tot)
    z = proj[:, 768:1280]
    xbc = proj[None, :, 1280:2048]
    tail = proj[:, TAIL:TAIL + LANES]
    xbc_act = conv_silu_fwd(xbc, p['ssm_conv_w'], p['ssm_conv_b'], name=nm('ssmconv'))[0]
    y_ssm, hprev = ssd_fwd(xbc_act, tail, p['ssm_dt_bias'], p['ssm_a_log'], p['ssm_d'], name=nm('ssd_fwd'))
    ybn = rms_fwd(y_ssm, p['ssm_out_norm'], name=nm('ssmnorm'), gate=z, out_dtype=MXU_DTYPE)
    r.update(z=z, xbc=xbc, tail=tail, xbc_act=xbc_act, y_ssm=y_ssm, hprev=hprev)
    cq = proj[:, 2048:2304]
    ckv = proj[:, 2304:2432]
    qn = rms_fwd(cq, p['mla_q_norm'], name=nm('qnorm'), out_dtype=MXU_DTYPE)
    q_r = rope(mm(qn, p['mla_w_uq'], name=nm('uq'))[None], tabq, name=nm('ropeq'))
    kvn = rms_fwd(ckv, p['mla_kv_norm'], name=nm('kvnorm'), out_dtype=MXU_DTYPE)
    kv = mm(kvn, p['mla_w_ukv'], name=nm('ukv'))
    k_pe = rope(tail[None], tabt, name=nm('ropek'))[:, :MLA_ROPE]
    qh = _heads(q_r, MLA_HEADS, MLA_QK).astype(MXU_DTYPE)
    kvh = _heads(kv, MLA_HEADS, MLA_NOPE + MLA_V)
    kh = jnp.concatenate([kvh[..., :MLA_NOPE], jnp.broadcast_to(k_pe[None], (MLA_HEADS, s, MLA_ROPE))],
                         axis=-1).astype(MXU_DTYPE)
    vh = kvh[..., MLA_NOPE:].astype(MXU_DTYPE)
    yc_h, lse = mla_fwd(qh, kh, vh, name=nm('mla_fwd'))
    yc = _unheads(yc_h)
    ycn = rms_fwd(yc, p['mla_out_norm'], name=nm('mlanorm'), out_dtype=MXU_DTYPE)
    r.update(cq=cq, ckv=ckv, qn=qn, kvn=kvn, qh=qh, kh=kh, vh=vh, yc_h=yc_h, yc=yc, lse=lse)
    ycat = jnp.concatenate([yan, ybn, ycn], axis=1)
    h1 = mm(ycat, p['w_out'], name=nm('outproj'), res=h)
    hn2 = rms_fwd(h1, p['ffn_norm'], name=nm('ffnnorm'), out_dtype=MXU_DTYPE)
    up = mm(hn2, p['ffn_w_up'], name=nm('up'), bb='o')
    act = conv_glu_fwd(up, p['ffn_conv_w'], p['ffn_conv_b'], name=nm('glu'))
    h2 = mm(act, p['ffn_w_down'], name=nm('down'), ab='k', bb='k', res=h1)
    r.update(ycat=ycat, h1=h1, hn2=hn2, up=up, act=act)
    return h2, r, p, late


def layer_bwd(dh2, p, r, tabs, li, scatter_late=None):
    s = dh2.shape[0]
    tabq, tabt = tabs
    nm = lambda t: f"L{li}_{t}"
    g = {}
    dact = mm(dh2, p['ffn_w_down'], name=nm('d_down_x'), tb=True, bb='o')
    g['ffn_w_down'] = mm(r['act'], dh2, name=nm('d_down_w'), out_dtype=WIRE_DTYPE, ta=True, ab='o')
    du, g['ffn_conv_w'], g['ffn_conv_b'] = conv_glu_bwd(r['up'], dact, p['ffn_conv_w'], p['ffn_conv_b'], name=nm('d_glu'))
    dup = conv_t(du, p['ffn_conv_w'], name=nm('d_ffnconv'))
    g['ffn_w_up'] = mm(r['hn2'], dup, name=nm('d_up_w'), out_dtype=WIRE_DTYPE, ta=True, bb='o')
    dhn2 = mm(dup, p['ffn_w_up'], name=nm('d_up_x'), tb=True, ab='k', bb='k')
    dh1, dg = rms_bwd(r['h1'], p['ffn_norm'], dhn2, name=nm('d_ffnnorm'), add=dh2)
    g['ffn_norm'] = dg[0]
    dycat = mm(dh1, p['w_out'], name=nm('d_out_x'), tb=True)
    g['w_out'] = mm(r['ycat'], dh1, name=nm('d_out_w'), out_dtype=WIRE_DTYPE, ta=True)
    dya, dg = rms_bwd(r['ya'], p['sb_out_norm'], dycat[:, :256], name=nm('d_sbnorm'))
    g['sb_out_norm'] = dg[0]
    recv_late = None
    if scatter_late is None:
        dq, dk, dv = sb_bwd(r['qkv'], dya, r['tot'], name=nm('sb_bwd'))
    else:
        parts = owner_parts_late([g] + list(scatter_late))
        dq, dk, dv, *got = sb_bwd(r['qkv'], dya, r['tot'], name=nm('sb_bwd'),
                                  carried=Carried('scatter', [parts[n].astype(WIRE_DTYPE) for n in LATE]))
        recv_late = dict(zip(LATE, got))
    dyssm, dz, dg = rms_bwd(r['y_ssm'], p['ssm_out_norm'], dycat[:, 256:768], name=nm('d_ssmnorm'), gate=r['z'])
    g['ssm_out_norm'] = dg[0]
    dxbc_act, ddt_tail, dbias, dalog, dd = ssd_bwd(r['xbc_act'], r['tail'], p['ssm_dt_bias'], p['ssm_a_log'],
                                                   p['ssm_d'], r['hprev'], dyssm, name=nm('ssd_bwd'))
    hl = slice(DT_LANE, DT_LANE + SSM_HEADS)
    g['ssm_dt_bias'], g['ssm_a_log'], g['ssm_d'] = dbias[0, hl], dalog[0, hl], dd[0, hl]
    dxbc_u, g['ssm_conv_w'], g['ssm_conv_b'] = conv_silu_bwd(r['xbc'], dxbc_act[None], p['ssm_conv_w'], p['ssm_conv_b'],
                                                             name=nm('d_ssmact'))
    dxbc = conv_t(dxbc_u, p['ssm_conv_w'], name=nm('d_ssmconv'))[0]
    dyc, dg = rms_bwd(r['yc'], p['mla_out_norm'], dycat[:, 768:], name=nm('d_mlanorm'))
    g['mla_out_norm'] = dg[0]
    dqh, dkh, dvh = mla_bwd(r['qh'], r['kh'], r['vh'], r['yc_h'], _heads(dyc, MLA_HEADS, MLA_V), r['lse'], name=nm('mla_bwd'))
    dq_c = rope(_unheads(dqh)[None], tabq, name=nm('d_ropeq'), backward=True)
    g['mla_w_uq'] = mm(r['qn'], dq_c, name=nm('d_uq_w'), out_dtype=WIRE_DTYPE, ta=True)
    dcq, dg = rms_bwd(r['cq'], p['mla_q_norm'], mm(dq_c, p['mla_w_uq'], name=nm('d_uq_x'), tb=True), name=nm('d_qnorm'))
    g['mla_q_norm'] = dg[0]
    dkv = _unheads(jnp.concatenate([dkh[..., :MLA_NOPE], dvh], axis=-1))
    g['mla_w_ukv'] = mm(r['kvn'], dkv, name=nm('d_ukv_w'), out_dtype=WIRE_DTYPE, ta=True)
    dckv, dg = rms_bwd(r['ckv'], p['mla_kv_norm'], mm(dkv, p['mla_w_ukv'], name=nm('d_ukv_x'), tb=True), name=nm('d_kvnorm'))
    g['mla_kv_norm'] = dg[0]
    dkpe = jnp.pad(dkh[..., MLA_NOPE:], ((0, 0), (0, 0), (0, LANES - MLA_ROPE)))
    dtail = rope(dkpe, tabt, name=nm('d_ropek'), backward=True, add=ddt_tail)
    dproj = jnp.concatenate([dq, dk, dv, dz, dxbc, dcq, dckv, dtail], axis=1)
    g['w_in'] = mm(r['hn'], dproj, name=nm('d_proj_w'), out_dtype=WIRE_DTYPE, ta=True)
    dhn = mm(dproj, p['w_in'], name=nm('d_proj_x'), tb=True)
    dh, dg = rms_bwd(r['h'], p['mix_norm'], dhn, name=nm('d_mixnorm'), add=dh1)
    g['mix_norm'] = dg[0]
    return dh, g, recv_late


def _w_in_placement():
    c = np.arange(D_IN)
    dest = np.where(c < 2048, c, np.where(c < 2056, c + (D_IN - 2056), c - 8))
    dest = jnp.asarray(dest.reshape(N_DEV, D_IN // N_DEV, 1), jnp.int32)
    return (dest == jnp.arange(D_IN_PAD, dtype=jnp.int32)).astype(MXU_DTYPE)


def _owner_major(full, axis):
    shp = full.shape
    return jnp.moveaxis(full.reshape(shp[:axis] + (N_DEV, shp[axis] // N_DEV) + shp[axis + 1:]), axis, 0)


def _owner_join(parts, axis):
    moved = jnp.moveaxis(parts, 0, axis)
    shp = moved.shape
    return moved.reshape(shp[:axis] + (shp[axis] * shp[axis + 1],) + shp[axis + 2:])


def assemble_early(gathered, replicated):
    L = DEPTH
    out = dict(replicated)
    out['w_in'] = mm(gathered['w_in'].reshape(N_DEV, L * D_MODEL, D_IN // N_DEV), _w_in_placement(), name='place_w_in',
                     ab='k', bb='k', out_dtype=MXU_DTYPE).reshape(L, D_MODEL, D_IN_PAD)
    out['mla_w_uq'] = _owner_join(gathered['mla_w_uq'], 2)
    out['mla_w_ukv'] = _owner_join(gathered['mla_w_ukv'], 2)
    out['ssm_conv_w'] = _owner_join(gathered['ssm_conv_w'], 2)[:, None]
    out['ssm_conv_b'] = replicated['ssm_conv_b'].reshape(L, 1, 1, SSM_CONV_DIM)
    out['ffn_conv_b'] = replicated['ffn_conv_b'].reshape(L, N_DEV, 1, FF_SHARD)
    return out


def assemble_late(gathered):
    L = DEPTH
    return {'ffn_w_up': jnp.moveaxis(gathered['ffn_w_up'], 1, 0),
            'w_out': _owner_join(gathered['w_out'], 1),
            'ffn_w_down': _owner_join(gathered['ffn_w_down'], 1).reshape(L, N_DEV // 2, FF_SHARD, D_MODEL),
            'ffn_conv_w': jnp.moveaxis(gathered['ffn_conv_w'], 1, 0)}


def owner_parts_late(grads):
    L = DEPTH
    st = lambda n: jnp.stack([g[n] for g in grads])
    return {'ffn_w_up': jnp.moveaxis(st('ffn_w_up'), 1, 0),
            'w_out': _owner_major(st('w_out'), 1),
            'ffn_w_down': _owner_major(st('ffn_w_down').reshape(L, D_FF, D_MODEL), 1),
            'ffn_conv_w': jnp.moveaxis(st('ffn_conv_w'), 1, 0)}


def owner_parts_early(grads):
    L = DEPTH
    st = lambda n: jnp.stack([g[n] for g in grads])
    parts = {
        'w_in': mm(st('w_in').reshape(L * D_MODEL, D_IN_PAD), _w_in_placement(), name='unplace_w_in', tb=True, bb='o',
                   out_dtype=WIRE_DTYPE).reshape(N_DEV, L, D_MODEL, D_IN // N_DEV),
        'mla_w_uq': _owner_major(st('mla_w_uq'), 2),
        'mla_w_ukv': _owner_major(st('mla_w_ukv'), 2),
        'ssm_conv_w': _owner_major(st('ssm_conv_w')[:, 0], 2),
    }
    rep = {n: st(n) for n in REPLICATED if n not in ('final_norm', 'ssm_conv_b', 'ffn_conv_b')}
    rep['ssm_conv_b'] = st('ssm_conv_b').reshape(L, SSM_CONV_DIM)
    rep['ffn_conv_b'] = st('ffn_conv_b').reshape(L, 2 * D_FF)
    return parts, rep


def local_step(x, positions, target, early, late_shards, replicated):
    s = x.shape[0]
    tabs = _rope_tables(positions, s)
    params = assemble_early(early, replicated)
    layer = lambda li: {n: params[n][li] for n in params if n != 'final_norm'}
    h, r0, p0, late = layer_fwd(x, layer(0), tabs, 0, gather_late=late_shards)
    saved = [(p0, r0)]
    for li in range(1, DEPTH):
        h, r, p, _ = layer_fwd(h, dict(layer(li), **{n: late[n][li] for n in LATE}), tabs, li)
        saved.append((p, r))
    y = rms_fwd(h, params['final_norm'], name='finalnorm')
    dy, loss = loss_head(y, target, name='loss')
    dh, dg = rms_bwd(h, params['final_norm'], dy, name='d_finalnorm')
    above = []
    for li in reversed(range(1, DEPTH)):
        dh, g, _ = layer_bwd(dh, *saved[li], tabs, li)
        above.insert(0, g)
    dh, g0, recv_late = layer_bwd(dh, *saved[0], tabs, 0, scatter_late=above)
    parts, rep = owner_parts_early([g0] + above)
    rep['final_norm'] = dg[0]
    return loss[0, 0], dh, parts, recv_late, rep


def all_gather(blocks, *, name):
    n = len(blocks)

    def body(*refs):
        x_refs, out_refs = refs[:n], refs[n:2 * n]
        send_sems, recv_sems, local_sems = refs[2 * n:]
        x, y, c = lax.axis_index("x"), lax.axis_index("y"), lax.axis_index("c")
        me, sibling = (x, y, c), (x, y, 1 - c)
        chips = [(1 - x, y), (x, 1 - y), (1 - x, 1 - y)]

        def slot(b, px, py, pc):
            return out_refs[b].at[4 * px + 2 * py + pc]

        def copy(b, k, blk, to, src=None):
            return pltpu.make_async_remote_copy(src_ref=slot(b, *blk) if src is None else src, dst_ref=slot(b, *blk),
                                                send_sem=send_sems.at[b, k], recv_sem=recv_sems.at[b, k],
                                                device_id=to, device_id_type=MESH)

        mine = [pltpu.make_async_copy(x_refs[b], slot(b, *me), local_sems.at[b]) for b in range(n)]
        for cp in mine:
            cp.start()
        first = []
        for b in range(n):
            first.append(copy(b, 0, me, sibling, src=x_refs[b]))
            first += [copy(b, 1 + j, me, (*chip, c), src=x_refs[b]) for j, chip in enumerate(chips)]
        for cp in first:
            cp.start()
        passed = []
        for j, chip in enumerate(chips):
            for b in range(n):
                copy(b, 1 + j, (*chip, c), me).wait_recv()
                fwd = copy(b, 4 + j, (*chip, c), sibling)
                fwd.start()
                passed.append(fwd)
        for b in range(n):
            copy(b, 0, sibling, me).wait_recv()
            for j, chip in enumerate(chips):
                copy(b, 4 + j, (*chip, 1 - c), me).wait_recv()
        for cp in first + passed:
            cp.wait_send()
        for cp in mine:
            cp.wait()

    return pl.pallas_call(
        body, name=name, out_shape=[jax.ShapeDtypeStruct((N_DEV,) + b.shape, b.dtype) for b in blocks],
        in_specs=[HBM] * n, out_specs=[HBM] * n,
        scratch_shapes=[pltpu.SemaphoreType.DMA((n, 7)), pltpu.SemaphoreType.DMA((n, 7)), pltpu.SemaphoreType.DMA((n,))],
    )(*blocks)


def all_to_all(parts, *, name):
    n = len(parts)

    def body(*refs):
        g_refs, r_refs = refs[:n], refs[n:2 * n]
        send_sems, recv_sems, local_sems = refs[2 * n:]
        x, y, c = lax.axis_index("x"), lax.axis_index("y"), lax.axis_index("c")
        me = 4 * x + 2 * y + c
        mine = [pltpu.make_async_copy(g_refs[b].at[me], r_refs[b].at[me], local_sems.at[b]) for b in range(n)]
        for cp in mine:
            cp.start()
        copies = []
        for k in range(1, N_DEV):
            px, py, pc = _flip(x, k & 4), _flip(y, k & 2), _flip(c, k & 1)
            peer = 4 * px + 2 * py + pc
            for b in range(n):
                cp = pltpu.make_async_remote_copy(src_ref=g_refs[b].at[peer], dst_ref=r_refs[b].at[me],
                                                  send_sem=send_sems.at[b, k - 1], recv_sem=recv_sems.at[b, k - 1],
                                                  device_id=(px, py, pc), device_id_type=MESH)
                cp.start()
                copies.append(cp)
        for cp in copies:
            cp.wait_send()
            cp.wait_recv()
        for cp in mine:
            cp.wait()

    return pl.pallas_call(
        body, name=name, out_shape=[jax.ShapeDtypeStruct(p.shape, p.dtype) for p in parts],
        in_specs=[HBM] * n, out_specs=[HBM] * n,
        scratch_shapes=[pltpu.SemaphoreType.DMA((n, 7)), pltpu.SemaphoreType.DMA((n, 7)), pltpu.SemaphoreType.DMA((n,))],
    )(*parts)


def adamw(parts, w, m, v, *, name):
    r, wd = w.shape
    br = _tile(r, (256, 128, 64, 32, 16, 8))
    c1 = 1.0 - ADAM_B1 ** ADAM_STEP
    c2 = 1.0 - ADAM_B2 ** ADAM_STEP

    def body(p_ref, w_ref, m_ref, v_ref, g_ref, d_ref, mo_ref, vo_ref):
        g = p_ref[0].astype(F32)
        for j in range(1, N_DEV):
            g = g + p_ref[j].astype(F32)
        mn = ADAM_B1 * m_ref[...] + (1.0 - ADAM_B1) * g
        vn = ADAM_B2 * v_ref[...] + (1.0 - ADAM_B2) * (g * g)
        g_ref[...] = g
        mo_ref[...] = mn
        vo_ref[...] = vn
        d_ref[...] = -ADAM_LR * ((mn / c1) / (jnp.sqrt(vn / c2) + ADAM_EPS) + ADAM_WD * w_ref[...])

    blk = pl.BlockSpec((br, wd), lambda i: (i, 0))
    out = jax.ShapeDtypeStruct((r, wd), F32)
    return _call(body, name=name, out_shape=[out] * 4, grid=(r // br,),
                 in_specs=[pl.BlockSpec((N_DEV, br, wd), lambda i: (0, i, 0)), blk, blk, blk], out_specs=[blk] * 4,
                 sem=("parallel",))(parts, w, m, v)


def _pack(arrs):
    flat = jnp.concatenate([a.reshape(-1) for a in arrs])
    rows = -(-flat.shape[0] // (8 * FLAT_W)) * 8
    return jnp.pad(flat, (0, rows * FLAT_W - flat.shape[0])).reshape(rows, FLAT_W)


def _unpack(flat, shapes):
    flat = flat.reshape(-1)
    out, off = [], 0
    for shp in shapes:
        n = int(np.prod(shp))
        out.append(flat[off:off + n].reshape(shp))
        off += n
    return out


def kernel(x, positions, mix_norm, w_in, sb_out_norm, ssm_conv_w, ssm_conv_b, ssm_dt_bias, ssm_a_log, ssm_d, ssm_out_norm, mla_q_norm, mla_w_uq, mla_kv_norm, mla_w_ukv, mla_out_norm, w_out, ffn_norm, ffn_w_up, ffn_conv_w, ffn_conv_b, ffn_w_down, final_norm, loss_target, m_mix_norm, m_w_in, m_sb_out_norm, m_ssm_conv_w, m_ssm_conv_b, m_ssm_dt_bias, m_ssm_a_log, m_ssm_d, m_ssm_out_norm, m_mla_q_norm, m_mla_w_uq, m_mla_kv_norm, m_mla_w_ukv, m_mla_out_norm, m_w_out, m_ffn_norm, m_ffn_w_up, m_ffn_conv_w, m_ffn_conv_b, m_ffn_w_down, m_final_norm, v_mix_norm, v_w_in, v_sb_out_norm, v_ssm_conv_w, v_ssm_conv_b, v_ssm_dt_bias, v_ssm_a_log, v_ssm_d, v_ssm_out_norm, v_mla_q_norm, v_mla_w_uq, v_mla_kv_norm, v_mla_w_ukv, v_mla_out_norm, v_w_out, v_ffn_norm, v_ffn_w_up, v_ffn_conv_w, v_ffn_conv_b, v_ffn_w_down, v_final_norm):
    args = locals()
    w = {n: args[n] for n in WEIGHTS}
    m = {n: args['m_' + n] for n in WEIGHTS}
    v = {n: args['v_' + n] for n in WEIGHTS}
    wire = lambda n: w[n] if n in VPU_WEIGHTS else w[n].astype(BF16)
    early = dict(zip(EARLY, all_gather([wire(n) for n in EARLY], name='gather_early')))

    loss, dx, parts, recv, rep = local_step(x[0], positions[0], loss_target[0], early, {n: wire(n) for n in LATE},
                                            {n: w[n] for n in REPLICATED})
    loss = lax.psum(loss, ("x", "y", "c"))

    recv.update(zip(EARLY, all_to_all([parts[n].astype(WIRE_DTYPE) for n in EARLY], name='scatter_early')))
    res = {kind: {} for kind in 'gdmv'}
    for n, rv in recv.items():
        shp = w[n].shape
        two_d = (int(np.prod(shp[:-1])), shp[-1])
        outs = adamw(rv.reshape((N_DEV,) + two_d), w[n].reshape(two_d), m[n].reshape(two_d), v[n].reshape(two_d),
                     name='adamw_' + n)
        for kind, o in zip('gdmv', outs):
            res[kind][n] = o.reshape(shp)

    rep_shapes = [w[n].shape for n in REPLICATED]
    (rparts,) = all_gather([_pack([rep[n] for n in REPLICATED])], name='gather_small_grads')
    rflat = lambda d: _pack([d[n] for n in REPLICATED])
    routs = adamw(rparts, rflat(w), rflat(m), rflat(v), name='adamw_replicated')
    for kind, o in zip('gdmv', routs):
        res[kind].update(zip(REPLICATED, _unpack(o, rep_shapes)))

    return (loss, dx[None], *[res['g'][n] for n in WEIGHTS], *[res['d'][n] for n in WEIGHTS],
            *[res['m'][n] for n in WEIGHTS], *[res['v'][n] for n in WEIGHTS])
```

```python
import numpy as np
import jax
import jax.numpy as jnp
from jax import lax
from jax.experimental import pallas as pl
from jax.experimental.pallas import tpu as pltpu

F32 = jnp.float32
BF16 = jnp.bfloat16
MXU_DTYPE = jnp.bfloat16
HIGHEST = lax.Precision.HIGHEST
WIRE_DTYPE = jnp.bfloat16

N_DEV = 8
D_MODEL = 1024
DEPTH = 2
EPS = 1e-6
SB_HEADS, SB_DIM = 4, 64
SB_WIDTH = SB_HEADS * SB_DIM
SSM_HEADS, SSM_P, SSM_GROUPS, SSM_N, SSM_CONV, SSM_CHUNK = 8, 64, 2, 64, 4, 128
SSM_INNER = SSM_HEADS * SSM_P
SSM_CONV_DIM = SSM_INNER + 2 * SSM_GROUPS * SSM_N
MLA_HEADS, MLA_NOPE, MLA_ROPE, MLA_V, MLA_Q_RANK, MLA_KV_RANK = 4, 64, 32, 64, 256, 128
MLA_QK = MLA_NOPE + MLA_ROPE
ROPE_THETA = 10000.0
D_IN = 2472
D_IN_PAD = 2560
TAIL = 2432
DT_LANE = 32
D_FF = 2816
FF_SHARD = 2 * D_FF // N_DEV
ADAM_LR, ADAM_B1, ADAM_B2, ADAM_EPS, ADAM_WD, ADAM_STEP = 0.001, 0.9, 0.999, 1e-08, 0.01, 10

LANES = 128
ATT_BLK = 256
SB_WIDE = 2
MLA_WIDE = 4
ROW_BLK = 512
ROW_BLOCK_BYTES = 2 << 20
CONV_COLS = 256
FLAT_W = 1024
VMEM_LIMIT = 56 << 20
MM_TM = (1024, 512, 256, 128)
MM_TN = (1280, 1024, 768, 640, 512, 384, 256, 128)
MM_TK = (1280, 1024, 512, 256, 128)

WEIGHTS = ['mix_norm', 'w_in', 'sb_out_norm', 'ssm_conv_w', 'ssm_conv_b', 'ssm_dt_bias', 'ssm_a_log', 'ssm_d',
           'ssm_out_norm', 'mla_q_norm', 'mla_w_uq', 'mla_kv_norm', 'mla_w_ukv', 'mla_out_norm', 'w_out',
           'ffn_norm', 'ffn_w_up', 'ffn_conv_w', 'ffn_conv_b', 'ffn_w_down', 'final_norm']
SHARDED = {'w_in': 2, 'ssm_conv_w': 2, 'mla_w_uq': 2, 'mla_w_ukv': 2, 'w_out': 1, 'ffn_w_up': 2, 'ffn_conv_w': 2,
           'ffn_w_down': 1}
VPU_WEIGHTS = ('ssm_conv_w', 'ffn_conv_w')
EARLY = ('w_in', 'mla_w_uq', 'mla_w_ukv', 'ssm_conv_w')
LATE = ('w_out', 'ffn_w_up', 'ffn_conv_w', 'ffn_w_down')
REPLICATED = [n for n in WEIGHTS if n not in SHARDED]


def _call(body, *, name, out_shape, grid=(), in_specs=None, out_specs=None, scratch=(), sem=None, **kw):
    params = dict(vmem_limit_bytes=VMEM_LIMIT)
    if sem is not None:
        params['dimension_semantics'] = sem
    return pl.pallas_call(body, name=name, out_shape=out_shape, grid=grid, in_specs=in_specs, out_specs=out_specs,
                          scratch_shapes=list(scratch), compiler_params=pltpu.CompilerParams(**params), **kw)


def _tile(n, prefs):
    for t in prefs:
        if n % t == 0:
            return t
    return n


def _rows(s, w):
    rows = ROW_BLK
    while rows * 2 <= s and s % (rows * 2) == 0 and rows * 2 * w * 4 <= ROW_BLOCK_BYTES:
        rows *= 2
    return _tile(s, (rows,))


def _dot(a, b, dims, precision=None):
    return lax.dot_general(a, b, (dims, ((), ())), preferred_element_type=F32, precision=precision)


def _nn(a, b, precision=None):
    return _dot(a, b, ((1,), (0,)), precision)


def _nt(a, b, precision=None):
    return _dot(a, b, ((1,), (1,)), precision)


def _tn(a, b, precision=None):
    return _dot(a, b, ((0,), (0,)), precision)


def _mxu(f):
    return lambda a, b: f(a.astype(MXU_DTYPE), b.astype(MXU_DTYPE))


_bnn, _bnt, _btn = _mxu(_nn), _mxu(_nt), _mxu(_tn)


def _split2(x):
    hi = x.astype(MXU_DTYPE)
    lo = (x - hi.astype(F32)).astype(MXU_DTYPE)
    return hi, lo


def _sigmoid(x):
    return 0.5 * jnp.tanh(0.5 * x) + 0.5


def _softplus(x):
    return jnp.maximum(x, 0.0) + jnp.log1p(jnp.exp(-jnp.abs(x)))


def _softplus_att(x):
    return jnp.maximum(x, 0.0) + jnp.log(1.0 + jnp.exp(-jnp.abs(x)))


def _cum(x, u):
    rows, b = x.shape[0], u.shape[0]
    n = x.shape[1] // b
    hi, lo = _split2(x)
    stack = [part[:, t * b:(t + 1) * b] for part in (hi, lo) for t in range(n)]
    r = _nn(jnp.concatenate(stack, axis=0), u)
    return jnp.concatenate([r[t * rows:(t + 1) * rows] + r[(n + t) * rows:(n + t + 1) * rows] for t in range(n)], axis=1)


def _causal_loop(qi, tile, carry, reverse, width):
    last = qi // width
    if reverse:
        return lax.fori_loop(0, last, lambda i, c: tile((last - 1 - i) * width, c, False), tile(last * width, carry, True))
    return tile(last * width, lax.fori_loop(0, last, lambda i, c: tile(i * width, c, False), carry), True)


def _causal_mask(blk, width, qi, kb, heads, strict, keys_on_rows=False):
    shape = (width * blk, blk) if keys_on_rows else (heads * blk, width * blk)
    q_idx = lax.broadcasted_iota(jnp.int32, shape, 1 if keys_on_rows else 0)
    k_idx = lax.broadcasted_iota(jnp.int32, shape, 0 if keys_on_rows else 1)
    if heads > 1:
        q_idx = q_idx % blk
    gap = (qi - kb) * blk
    return k_idx < q_idx + gap if strict else k_idx <= q_idx + gap


def mm(a, b, *, name, ta=False, tb=False, res=None, out_dtype=F32, ab=None, bb=None, precision=None):
    a2, b2 = a.shape[-2:], b.shape[-2:]
    (kdim, m) = a2 if ta else a2[::-1]
    (n, k2) = b2 if tb else b2[::-1]
    assert kdim == k2, (a.shape, b.shape, ta, tb)
    assert (ab == 'k') == (bb == 'k')
    kb = ab == 'k'
    nb = a.shape[0] if ab == 'o' else (b.shape[0] if bb == 'o' else None)
    tm, tn = _tile(m, MM_TM), _tile(n, MM_TN)
    tk = kdim if kb else _tile(kdim, MM_TK)
    nk = a.shape[0] if kb else kdim // tk
    dims = ((0 if ta else 1,), (1 if tb else 0,))
    op_dtype = F32 if precision is not None else MXU_DTYPE

    def body(*refs):
        a_ref, b_ref = refs[0], refs[1]
        r_ref = refs[2] if res is not None else None
        o_ref = refs[3] if res is not None else refs[2]
        part = _dot(a_ref[...].astype(op_dtype), b_ref[...].astype(op_dtype), dims, precision)

        def finish(out):
            if res is not None:
                out = out + r_ref[...]
            o_ref[...] = out.astype(out_dtype)

        if nk == 1:
            finish(part)
            return
        acc = refs[-1]
        k = pl.program_id(3)

        @pl.when(k == 0)
        def _():
            acc[...] = part

        @pl.when(k > 0)
        def _():
            acc[...] += part

        @pl.when(k == nk - 1)
        def _():
            finish(acc[...])

    def spec(blk, idx, how):
        if how is None:
            return pl.BlockSpec(blk, idx)
        if how == 'o':
            return pl.BlockSpec((None,) + blk, lambda p, i, j, k: (p,) + idx(p, i, j, k))
        return pl.BlockSpec((None,) + blk, lambda p, i, j, k: (k,) + idx(p, i, j, 0))

    a_spec = spec((tk, tm), lambda p, i, j, k: (k, i), ab) if ta else spec((tm, tk), lambda p, i, j, k: (i, k), ab)
    b_spec = spec((tn, tk), lambda p, i, j, k: (j, k), bb) if tb else spec((tk, tn), lambda p, i, j, k: (k, j), bb)
    o_spec = spec((tm, tn), lambda p, i, j, k: (i, j), None if nb is None else 'o')
    ins, specs = [a, b], [a_spec, b_spec]
    if res is not None:
        ins.append(res)
        specs.append(o_spec)
    out_shape = (m, n) if nb is None else (nb, m, n)
    return _call(body, name=name, out_shape=jax.ShapeDtypeStruct(out_shape, out_dtype),
                 grid=(1 if nb is None else nb, m // tm, n // tn, nk), in_specs=specs, out_specs=o_spec,
                 scratch=[] if nk == 1 else [pltpu.VMEM((tm, tn), F32)],
                 sem=("parallel", "parallel", "parallel", "arbitrary"))(*ins)


def rms_fwd(x, g, *, name, gate=None, out_dtype=F32):
    s, w = x.shape
    bs = _rows(s, w)

    def body(*refs):
        if gate is None:
            x_ref, g_ref, o_ref = refs
            u = x_ref[...]
        else:
            x_ref, z_ref, g_ref, o_ref = refs
            z = z_ref[...]
            u = x_ref[...] * (z * _sigmoid(z))
        r = lax.rsqrt(jnp.mean(u * u, axis=1, keepdims=True) + EPS)
        o_ref[...] = (u * r * g_ref[...]).astype(out_dtype)

    row = pl.BlockSpec((bs, w), lambda i: (i, 0))
    vec = pl.BlockSpec((1, w), lambda i: (0, 0))
    ins = [x] + ([] if gate is None else [gate]) + [g.reshape(1, w)]
    specs = [row] + ([] if gate is None else [row]) + [vec]
    return _call(body, name=name, out_shape=jax.ShapeDtypeStruct((s, w), out_dtype), grid=(s // bs,),
                 in_specs=specs, out_specs=row, sem=("parallel",))(*ins)


def rms_bwd(x, g, dy, *, name, gate=None, add=None):
    s, w = x.shape
    bs = _rows(s, w)

    def body(*refs):
        refs = list(refs)
        x_ref = refs.pop(0)
        z_ref = refs.pop(0) if gate is not None else None
        g_ref = refs.pop(0)
        dy_ref = refs.pop(0)
        add_ref = refs.pop(0) if add is not None else None
        dx_ref = refs.pop(0)
        dz_ref = refs.pop(0) if gate is not None else None
        dg_ref = refs.pop(0)
        i = pl.program_id(0)

        @pl.when(i == 0)
        def _():
            dg_ref[...] = jnp.zeros_like(dg_ref)

        xv = x_ref[...]
        if gate is not None:
            z = z_ref[...]
            sg = _sigmoid(z)
            act = z * sg
            u = xv * act
        else:
            u = xv
        r = lax.rsqrt(jnp.mean(u * u, axis=1, keepdims=True) + EPS)
        dy_v = dy_ref[...]
        dyg = dy_v * g_ref[...]
        du = r * dyg - u * (r * r * r * jnp.mean(dyg * u, axis=1, keepdims=True))
        dg_ref[...] += jnp.sum(dy_v * u * r, axis=0, keepdims=True)
        if gate is not None:
            dx = du * act
            dz_ref[...] = du * xv * (sg * (1.0 + z * (1.0 - sg)))
        else:
            dx = du
        if add is not None:
            dx = dx + add_ref[...]
        dx_ref[...] = dx

    row = pl.BlockSpec((bs, w), lambda i: (i, 0))
    vec = pl.BlockSpec((1, w), lambda i: (0, 0))
    ins = [x] + ([] if gate is None else [gate]) + [g.reshape(1, w), dy] + ([] if add is None else [add])
    specs = [row] + ([] if gate is None else [row]) + [vec, row] + ([] if add is None else [row])
    outs = [jax.ShapeDtypeStruct((s, w), F32)] + ([] if gate is None else [jax.ShapeDtypeStruct((s, w), F32)])
    outs.append(jax.ShapeDtypeStruct((1, w), F32))
    ospecs = [row] + ([] if gate is None else [row]) + [vec]
    return _call(body, name=name, out_shape=outs, grid=(s // bs,), in_specs=specs, out_specs=ospecs,
                 sem=("arbitrary",))(*ins)


def loss_head(y, target, *, name):
    s, w = y.shape
    bs = _rows(s, w)
    nb = s // bs

    def body(y_ref, t_ref, dy_ref, loss_ref, acc):
        i = pl.program_id(0)

        @pl.when(i == 0)
        def _():
            acc[...] = jnp.zeros_like(acc)

        e = y_ref[...] - t_ref[...]
        dy_ref[...] = e * (1.0 / w)
        acc[...] += jnp.sum(e * e, axis=0, keepdims=True)

        @pl.when(i == nb - 1)
        def _():
            loss_ref[...] = jnp.sum(acc[...], axis=1, keepdims=True) * (0.5 / w)

    row = pl.BlockSpec((bs, w), lambda i: (i, 0))
    return _call(body, name=name, out_shape=[jax.ShapeDtypeStruct((s, w), F32), jax.ShapeDtypeStruct((1, 1), F32)],
                 grid=(nb,), in_specs=[row, row], out_specs=[row, pl.BlockSpec((1, 1), lambda i: (0, 0))],
                 scratch=[pltpu.VMEM((1, w), F32)], sem=("arbitrary",))(y, target)


def _rope_tables(positions, s):
    inv_freq = 1.0 / (ROPE_THETA ** (jnp.arange(0, MLA_ROPE, 2, dtype=F32) / MLA_ROPE))
    ang = positions.reshape(s, 1).astype(F32) * inv_freq
    cos, sin = jnp.cos(ang), jnp.sin(ang)
    one, zero = jnp.ones((s, MLA_NOPE), F32), jnp.zeros((s, MLA_NOPE), F32)
    cq = jnp.tile(jnp.concatenate([one, cos, cos], axis=1), (1, MLA_HEADS))
    sq = jnp.tile(jnp.concatenate([zero, sin, sin], axis=1), (1, MLA_HEADS))
    pad1, pad0 = jnp.ones((s, LANES - MLA_ROPE), F32), jnp.zeros((s, LANES - MLA_ROPE), F32)
    ct = jnp.concatenate([cos, cos, pad1], axis=1)
    st = jnp.concatenate([sin, sin, pad0], axis=1)
    half = MLA_ROPE // 2

    def swap(width, starts):
        r = np.zeros((width, width), np.float32)
        for o in starts:
            for i in range(half):
                r[o + half + i, o + i] = -1.0
                r[o + i, o + half + i] = 1.0
        return jnp.asarray(r)

    rq = swap(MLA_HEADS * MLA_QK, [h * MLA_QK + MLA_NOPE for h in range(MLA_HEADS)])
    rt = swap(LANES, [0])
    return (cq, sq, rq), (ct, st, rt)


def rope(x, tabs, *, name, backward=False, add=None):
    cos, sin, rot = tabs
    n, s, w = x.shape
    bs = _rows(s, w)

    def body(*refs):
        if add is None:
            x_ref, c_ref, s_ref, r_ref, o_ref = refs
        else:
            x_ref, c_ref, s_ref, r_ref, a_ref, o_ref = refs
        xv = x_ref[0]
        for j in range(1, n):
            xv = xv + x_ref[j]
        if backward:
            out = xv * c_ref[...] + _nt(xv * s_ref[...], r_ref[...], HIGHEST)
        else:
            out = xv * c_ref[...] + _nn(xv, r_ref[...], HIGHEST) * s_ref[...]
        if add is not None:
            out = out + a_ref[...]
        o_ref[...] = out

    row = pl.BlockSpec((bs, w), lambda i: (i, 0))
    ins = [x, cos, sin, rot] + ([] if add is None else [add])
    specs = [pl.BlockSpec((n, bs, w), lambda i: (0, i, 0)), row, row, pl.BlockSpec((w, w), lambda i: (0, 0))]
    specs += [] if add is None else [row]
    return _call(body, name=name, out_shape=jax.ShapeDtypeStruct((s, w), F32), grid=(s // bs,), in_specs=specs,
                 out_specs=row, sem=("parallel",))(*ins)


MESH = pl.DeviceIdType.MESH
HBM = pl.BlockSpec(memory_space=pltpu.HBM)


def _flip(v, bit):
    return 1 - v if bit else v


class Carried:
    def __init__(self, kind, arrays):
        assert kind in ('gather', 'scatter')
        self.kind, self.arrays, self.n = kind, list(arrays), len(arrays)

    @property
    def out_shape(self):
        lead = (N_DEV,) if self.kind == 'gather' else ()
        return [jax.ShapeDtypeStruct(lead + a.shape, a.dtype) for a in self.arrays]

    @property
    def scratch(self):
        return [pltpu.SemaphoreType.DMA((self.n, N_DEV - 1)), pltpu.SemaphoreType.DMA((self.n, N_DEV - 1)),
                pltpu.SemaphoreType.DMA((self.n,))]

    def _copies(self, in_refs, out_refs, sems):
        send_sems, recv_sems, local_sems = sems
        x, y, c = lax.axis_index("x"), lax.axis_index("y"), lax.axis_index("c")
        me = 4 * x + 2 * y + c
        part = (lambda b, p: in_refs[b]) if self.kind == 'gather' else (lambda b, p: in_refs[b].at[p])
        local = [pltpu.make_async_copy(part(b, me), out_refs[b].at[me], local_sems.at[b]) for b in range(self.n)]
        remote = []
        for k in range(1, N_DEV):
            px, py, pc = _flip(x, k & 4), _flip(y, k & 2), _flip(c, k & 1)
            for b in range(self.n):
                remote.append(pltpu.make_async_remote_copy(
                    src_ref=part(b, 4 * px + 2 * py + pc), dst_ref=out_refs[b].at[me], send_sem=send_sems.at[b, k - 1],
                    recv_sem=recv_sems.at[b, k - 1], device_id=(px, py, pc), device_id_type=MESH))
        return local, remote

    def start(self, in_refs, out_refs, sems):
        local, remote = self._copies(in_refs, out_refs, sems)
        for cp in local + remote:
            cp.start()

    def wait(self, in_refs, out_refs, sems):
        local, remote = self._copies(in_refs, out_refs, sems)
        for cp in remote:
            cp.wait_send()
            cp.wait_recv()
        for cp in local:
            cp.wait()


def _ride(carried, refs, n_in, n_out):
    n = 0 if carried is None else carried.n
    own_in, ride_in = refs[:n_in], refs[n_in:n_in + n]
    own_out, ride_out = refs[n_in + n:n_in + n + n_out], refs[n_in + n + n_out:n_in + 2 * n + n_out]
    return own_in, own_out, (ride_in, ride_out, refs[n_in + 2 * n + n_out:])


def _tri(n, op):
    r = lax.broadcasted_iota(jnp.int32, (n, n), 0)
    c = lax.broadcasted_iota(jnp.int32, (n, n), 1)
    return r, c, op(r, c)


def _pair_split(x, first):
    zero = jnp.zeros_like(x)
    return jnp.where(first, x, zero), jnp.where(first, zero, x)


def _sb_specs(s, blk):
    npair = SB_WIDTH // LANES
    q = pl.BlockSpec((blk, LANES), lambda j, i: (i, j))
    k = pl.BlockSpec((s, LANES), lambda j, i: (0, npair + j))
    v = pl.BlockSpec((s, LANES), lambda j, i: (0, 2 * npair + j))
    full = pl.BlockSpec((s, LANES), lambda j, i: (0, j))
    return npair, q, k, v, full


def _stack_heads(x, first):
    return jnp.concatenate(_pair_split(x, first), axis=0)


def _unstack_heads(x, first, blk):
    return jnp.where(first, x[:blk], x[blk:])


def sb_fwd(qkv, *, name, carried=None):
    s = qkv.shape[0]
    blk = _tile(s, (ATT_BLK,))
    scale = SB_DIM ** -0.5
    npair, nq = SB_WIDTH // LANES, s // blk
    assert nq % SB_WIDE == 0

    def body(*refs):
        (q_ref, k_ref, v_ref), (y_ref, t_ref), ride = _ride(carried, refs, 3, 2)
        pair, qi = pl.program_id(0), pl.program_id(1)
        if carried is not None:
            @pl.when((pair == 0) & (qi == 0))
            def _():
                carried.start(*ride)

        first = lax.broadcasted_iota(jnp.int32, (blk, LANES), 1) < SB_DIM
        q2 = _stack_heads((q_ref[...].astype(F32) * scale).astype(MXU_DTYPE), first)
        row, col, later_mask = _tri(blk, lambda r, c: r > c)
        u_later = later_mask.astype(MXU_DTYPE)
        n = SB_WIDE

        def tile(kb, carry, masked):
            c, acc = carry
            keys = pl.ds(pl.multiple_of(kb * blk, blk), n * blk)
            z = _nt(q2, k_ref[keys, :])
            sp = _softplus_att(z)
            if masked:
                valid = _causal_mask(blk, n, qi, kb, 2, True)
            spm = jnp.where(valid, sp, 0.0) if masked else sp
            later = _cum(spm, u_later)
            sums = [jnp.sum(spm[:, t * blk:(t + 1) * blk], axis=1, keepdims=True) for t in range(n)]
            after, cols = c, [None] * n
            for t in reversed(range(n)):
                cols[t] = jnp.broadcast_to(after, (2 * blk, blk))
                after = after - sums[t]
            w = jnp.exp((z - sp) - later + (cols[0] if n == 1 else jnp.concatenate(cols, axis=1)))
            if masked:
                w = jnp.where(valid, w, 0.0)
            return after, acc + _nn(w.astype(MXU_DTYPE), v_ref[keys, :])

        zero = (jnp.zeros((2 * blk, 1), F32), jnp.zeros((2 * blk, LANES), F32))
        c, acc = _causal_loop(qi, tile, zero, True, SB_WIDE)
        y_ref[...] = _unstack_heads(acc, first, blk)
        t_ref[...] = _unstack_heads(c, first, blk)
        if carried is not None:
            @pl.when((pair == npair - 1) & (qi == nq - 1))
            def _():
                carried.wait(*ride)

    _, qspec, kspec, vspec, _ = _sb_specs(s, blk)
    out = jax.ShapeDtypeStruct((s, SB_WIDTH), F32)
    extra = [] if carried is None else carried.arrays
    return _call(body, name=name, out_shape=[out, out] + ([] if carried is None else carried.out_shape),
                 grid=(npair, nq), in_specs=[qspec, kspec, vspec] + [HBM] * len(extra),
                 out_specs=[qspec, qspec] + [HBM] * len(extra), scratch=[] if carried is None else carried.scratch,
                 sem=("arbitrary", "arbitrary"))(qkv, qkv, qkv, *extra)


def sb_bwd(qkv, dy, tot, *, name, carried=None):
    s = qkv.shape[0]
    blk = _tile(s, (ATT_BLK,))
    scale = SB_DIM ** -0.5
    npair, nq = SB_WIDTH // LANES, s // blk
    assert nq % SB_WIDE == 0

    def body(*refs):
        (q_ref, k_ref, v_ref, dy_ref, t_ref), (dq_ref, dk_ref, dv_ref), ride = _ride(carried, refs, 5, 3)
        pair, qi = pl.program_id(0), pl.program_id(1)
        if carried is not None:
            @pl.when((pair == 0) & (qi == 0))
            def _():
                carried.start(*ride)

        @pl.when(qi == 0)
        def _():
            dk_ref[...] = jnp.zeros_like(dk_ref)
            dv_ref[...] = jnp.zeros_like(dv_ref)

        first = lax.broadcasted_iota(jnp.int32, (blk, LANES), 1) < SB_DIM
        q2 = _stack_heads((q_ref[...].astype(F32) * scale).astype(MXU_DTYPE), first)
        dy2 = _stack_heads(dy_ref[...].astype(MXU_DTYPE), first)
        tv = jnp.concatenate([t_ref[:, 0:1], t_ref[:, SB_DIM:SB_DIM + 1]], axis=0)
        row, col, incl_mask = _tri(blk, lambda r, c: r <= c)
        u_incl = incl_mask.astype(MXU_DTYPE)
        u_excl = (row < col).astype(MXU_DTYPE)
        n = SB_WIDE

        def prefixed(x, carry):
            cols = []
            for t in range(n):
                cols.append(jnp.broadcast_to(carry, (2 * blk, blk)))
                carry = carry + jnp.sum(x[:, t * blk:(t + 1) * blk], axis=1, keepdims=True)
            return (cols[0] if n == 1 else jnp.concatenate(cols, axis=1)), carry

        def tile(kb, carry, masked):
            p, gc, dq = carry
            keys = pl.ds(pl.multiple_of(kb * blk, blk), n * blk)
            kv = k_ref[keys, :]
            z = _nt(q2, kv)
            dw = _nt(dy2, v_ref[keys, :])
            sp = _softplus_att(z)
            if masked:
                valid = _causal_mask(blk, n, qi, kb, 2, True)
            spm = jnp.where(valid, sp, 0.0) if masked else sp
            before, p = prefixed(spm, p)
            w = jnp.exp((z - sp) + (_cum(spm, u_incl) + before))
            if masked:
                w = jnp.where(valid, w, 0.0)
            g = w * dw
            gbefore, gc = prefixed(g, gc)
            gb = g.astype(MXU_DTYPE)
            gin = _nn(jnp.concatenate([gb[:, t * blk:(t + 1) * blk] for t in range(n)], axis=0), u_excl)
            gex = gbefore + jnp.concatenate([gin[t * 2 * blk:(t + 1) * 2 * blk] for t in range(n)], axis=1)
            keep = jnp.exp(-spm)
            dz = keep * (g + gex) - gex
            if masked:
                dz = jnp.where(valid, dz, 0.0)
            dzb = dz.astype(MXU_DTYPE)
            dk_ref[keys, :] += _tn(dzb, q2)
            dv_ref[keys, :] += _tn(w.astype(MXU_DTYPE), dy2)
            return p, gc, dq + _nn(dzb, kv)

        zero = jnp.zeros((2 * blk, 1), F32)
        _, _, dq = _causal_loop(qi, tile, (tv, zero, jnp.zeros((2 * blk, LANES), F32)), False, SB_WIDE)
        dq_ref[...] = _unstack_heads(dq, first, blk) * scale
        if carried is not None:
            @pl.when((pair == npair - 1) & (qi == nq - 1))
            def _():
                carried.wait(*ride)

    _, qspec, kspec, vspec, full = _sb_specs(s, blk)
    out = jax.ShapeDtypeStruct((s, SB_WIDTH), F32)
    extra = [] if carried is None else carried.arrays
    return _call(body, name=name, out_shape=[out, out, out] + ([] if carried is None else carried.out_shape),
                 grid=(npair, nq), in_specs=[qspec, kspec, vspec, qspec, qspec] + [HBM] * len(extra),
                 out_specs=[qspec, full, full] + [HBM] * len(extra), scratch=[] if carried is None else carried.scratch,
                 sem=("arbitrary", "arbitrary"))(qkv, qkv, qkv, dy, tot, *extra)


ATT_PAIR = 2


def _mla_specs(s, blk, dk, dv):
    q = pl.BlockSpec((ATT_PAIR, blk, dk), lambda hp, i: (hp, i, 0))
    k = pl.BlockSpec((ATT_PAIR, s, dk), lambda hp, i: (hp, 0, 0))
    v = pl.BlockSpec((ATT_PAIR, s, dv), lambda hp, i: (hp, 0, 0))
    y = pl.BlockSpec((ATT_PAIR, blk, dv), lambda hp, i: (hp, i, 0))
    lse = pl.BlockSpec((ATT_PAIR, blk, LANES), lambda hp, i: (hp, i, 0))
    return q, k, v, y, lse


def mla_fwd(q, k, v, *, name):
    h, s, dk = q.shape
    dv = v.shape[-1]
    blk = _tile(s, (ATT_BLK,))
    scale = dk ** -0.5
    assert (s // blk) % MLA_WIDE == 0

    def body(q_ref, k_ref, v_ref, y_ref, l_ref):
        qi = pl.program_id(1)
        n = MLA_WIDE

        def tile(kb, carry, masked):
            keys = pl.ds(pl.multiple_of(kb * blk, blk), n * blk)
            out = []
            for hh in range(ATT_PAIR):
                m, l, acc = carry[hh]
                sc = _nt(q_ref[hh], k_ref[hh, keys, :]) * scale
                if masked:
                    sc = jnp.where(_causal_mask(blk, n, qi, kb, 1, False), sc, -1e30)
                m2 = jnp.maximum(m, jnp.max(sc, axis=1, keepdims=True))
                p = jnp.exp(sc - m2)
                a = jnp.exp(m - m2)
                out.append((m2, a * l + jnp.sum(p, axis=1, keepdims=True),
                            a * acc + _nn(p.astype(MXU_DTYPE), v_ref[hh, keys, :])))
            return tuple(out)

        init = (jnp.full((blk, 1), -1e30, F32), jnp.zeros((blk, 1), F32), jnp.zeros((blk, dv), F32))
        for hh, (m, l, acc) in enumerate(_causal_loop(qi, tile, (init,) * ATT_PAIR, False, MLA_WIDE)):
            y_ref[hh] = acc / l
            l_ref[hh] = jnp.broadcast_to(m + jnp.log(l), (blk, LANES))

    qspec, kspec, vspec, yspec, lspec = _mla_specs(s, blk, dk, dv)
    return _call(body, name=name,
                 out_shape=[jax.ShapeDtypeStruct((h, s, dv), F32), jax.ShapeDtypeStruct((h, s, LANES), F32)],
                 grid=(h // ATT_PAIR, s // blk), in_specs=[qspec, kspec, vspec], out_specs=[yspec, lspec],
                 sem=("parallel", "arbitrary"))(q, k, v)


def mla_bwd(q, k, v, y, dy, lse, *, name):
    h, s, dk = q.shape
    dv = v.shape[-1]
    blk = _tile(s, (ATT_BLK,))
    scale = dk ** -0.5
    assert (s // blk) % MLA_WIDE == 0

    def body(q_ref, k_ref, v_ref, y_ref, dy_ref, l_ref, dq_ref, dk_ref, dv_ref):
        qi = pl.program_id(1)

        @pl.when(qi == 0)
        def _():
            dk_ref[...] = jnp.zeros_like(dk_ref)
            dv_ref[...] = jnp.zeros_like(dv_ref)

        as_row = lambda col: jnp.transpose(jnp.broadcast_to(col, (blk, LANES)))[0:1, :]
        dyv = [dy_ref[hh].astype(MXU_DTYPE) for hh in range(ATT_PAIR)]
        delta = [as_row(jnp.sum(dy_ref[hh] * y_ref[hh], axis=1, keepdims=True)) for hh in range(ATT_PAIR)]
        lv = [as_row(l_ref[hh, :, 0:1]) for hh in range(ATT_PAIR)]
        n = MLA_WIDE

        def tile(kb, dqs, masked):
            keys = pl.ds(pl.multiple_of(kb * blk, blk), n * blk)
            out = []
            for hh in range(ATT_PAIR):
                qv = q_ref[hh]
                kv = k_ref[hh, keys, :]
                p = jnp.exp(_nt(kv, qv) * scale - lv[hh])
                if masked:
                    p = jnp.where(_causal_mask(blk, n, qi, kb, 1, False, keys_on_rows=True), p, 0.0)
                ds = (p * (_nt(v_ref[hh, keys, :], dyv[hh]) - delta[hh])).astype(MXU_DTYPE)
                dk_ref[hh, keys, :] += _nn(ds, qv) * scale
                dv_ref[hh, keys, :] += _nn(p.astype(MXU_DTYPE), dyv[hh])
                out.append(dqs[hh] + _tn(ds, kv))
            return tuple(out)

        for hh, dq in enumerate(_causal_loop(qi, tile, (jnp.zeros((blk, dk), F32),) * ATT_PAIR, False, MLA_WIDE)):
            dq_ref[hh] = dq * scale

    qspec, kspec, vspec, yspec, lspec = _mla_specs(s, blk, dk, dv)
    return _call(body, name=name,
                 out_shape=[jax.ShapeDtypeStruct((h, s, dk), F32), jax.ShapeDtypeStruct((h, s, dk), F32),
                            jax.ShapeDtypeStruct((h, s, dv), F32)],
                 grid=(h // ATT_PAIR, s // blk), in_specs=[qspec, kspec, vspec, yspec, yspec, lspec],
                 out_specs=[qspec, kspec, vspec], sem=("parallel", "arbitrary"))(q, k, v, y, dy, lse)


HALO = 8
CONV_CHUNK = 16


def _conv_tiles(x):
    s, c = x.shape[-2:]
    return s, c, _tile(s, (ROW_BLK,)), _tile(c, (CONV_COLS,))


def _conv_specs(bs, cw, lead=()):
    zero = (0,) * len(lead)
    blk = pl.BlockSpec(lead + (None, bs, cw), lambda p, j, i: zero + (p, i, j))
    halo = pl.BlockSpec(lead + (None, HALO, cw), lambda p, j, i: zero + (p, jnp.maximum(i * (bs // HALO) - 1, 0), j))
    w = lambda kk: pl.BlockSpec(lead + (None, kk, cw), lambda p, j, i: zero + (p, 0, j))
    return blk, halo, w


def _stage(scr, x_ref, halo_ref, first):
    scr[0:HALO, :] = jnp.where(first, 0.0, halo_ref[...])
    scr[HALO:, :] = x_ref[...]


def _shifted(ext, shift):
    return ext[HALO:] if shift == 0 else pltpu.roll(ext, shift, 0)[HALO:]


def _conv_taps(scr, kk, r0):
    ext = scr[pl.ds(r0, CONV_CHUNK + HALO), :]
    return [_shifted(ext, kk - 1 - k) for k in range(kk)]


def _conv_sum(taps, w_ref, b_ref):
    u = b_ref[...] + taps[0] * w_ref[0:1, :]
    for k in range(1, len(taps)):
        u = u + taps[k] * w_ref[k:k + 1, :]
    return u


def _fold(x):
    out = x[0:8]
    for r in range(8, CONV_CHUNK, 8):
        out = out + x[r:r + 8]
    return out


class _TapSums:
    def __init__(self, kk, cw):
        self.w = [jnp.zeros((8, cw), F32) for _ in range(kk)]
        self.b = jnp.zeros((8, cw), F32)

    def add(self, du, taps):
        self.w = [a + _fold(du * t) for a, t in zip(self.w, taps)]
        self.b = self.b + _fold(du)

    def flush(self, dw_ref, db_ref):
        for k, a in enumerate(self.w):
            dw_ref[k:k + 1, :] += jnp.sum(a, axis=0, keepdims=True)
        db_ref[...] += jnp.sum(self.b, axis=0, keepdims=True)


def _silu_grad(u):
    sg = _sigmoid(u)
    return sg * (1.0 + u * (1.0 - sg))


def conv_silu_fwd(x, w, b, *, name):
    s, c, bs, cw = _conv_tiles(x)
    kk = w.shape[1]

    def body(x_ref, h_ref, w_ref, b_ref, o_ref, scr):
        _stage(scr, x_ref, h_ref, pl.program_id(2) == 0)
        for r0 in range(0, bs, CONV_CHUNK):
            u = _conv_sum(_conv_taps(scr, kk, r0), w_ref, b_ref)
            o_ref[pl.ds(r0, CONV_CHUNK), :] = u * _sigmoid(u)

    blk, halo, wspec = _conv_specs(bs, cw)
    return _call(body, name=name, out_shape=jax.ShapeDtypeStruct(x.shape, F32), grid=(x.shape[0], c // cw, s // bs),
                 in_specs=[blk, halo, wspec(kk), wspec(1)], out_specs=blk, scratch=[pltpu.VMEM((bs + HALO, cw), F32)],
                 sem=("parallel", "parallel", "arbitrary"))(x, x, w, b)


def conv_silu_bwd(x, dy, w, b, *, name):
    s, c, bs, cw = _conv_tiles(x)
    kk = w.shape[1]

    def body(x_ref, h_ref, w_ref, b_ref, dy_ref, du_ref, dw_ref, db_ref, scr):
        i = pl.program_id(2)

        @pl.when(i == 0)
        def _():
            dw_ref[...] = jnp.zeros_like(dw_ref)
            db_ref[...] = jnp.zeros_like(db_ref)

        _stage(scr, x_ref, h_ref, i == 0)
        sums = _TapSums(kk, cw)
        for r0 in range(0, bs, CONV_CHUNK):
            taps = _conv_taps(scr, kk, r0)
            du = dy_ref[pl.ds(r0, CONV_CHUNK), :] * _silu_grad(_conv_sum(taps, w_ref, b_ref))
            du_ref[pl.ds(r0, CONV_CHUNK), :] = du
            sums.add(du, taps)
        sums.flush(dw_ref, db_ref)

    blk, halo, wspec = _conv_specs(bs, cw)
    return _call(body, name=name,
                 out_shape=[jax.ShapeDtypeStruct(x.shape, F32), jax.ShapeDtypeStruct(w.shape, F32),
                            jax.ShapeDtypeStruct(b.shape, F32)],
                 grid=(x.shape[0], c // cw, s // bs), in_specs=[blk, halo, wspec(kk), wspec(1), blk],
                 out_specs=[blk, wspec(kk), wspec(1)], scratch=[pltpu.VMEM((bs + HALO, cw), F32)],
                 sem=("parallel", "parallel", "arbitrary"))(x, x, w, b, dy)


def _glu_view(a):
    return a.reshape((2, a.shape[0] // 2) + a.shape[1:])


def conv_glu_fwd(x, w, b, *, name):
    s, c, bs, cw = _conv_tiles(x)
    kk = w.shape[1]
    half = x.shape[0] // 2

    def body(x_ref, h_ref, w_ref, b_ref, o_ref, gscr, vscr):
        first = pl.program_id(2) == 0
        _stage(gscr, x_ref.at[0], h_ref.at[0], first)
        _stage(vscr, x_ref.at[1], h_ref.at[1], first)
        for r0 in range(0, bs, CONV_CHUNK):
            gate = _conv_sum(_conv_taps(gscr, kk, r0), w_ref.at[0], b_ref.at[0])
            val = _conv_sum(_conv_taps(vscr, kk, r0), w_ref.at[1], b_ref.at[1])
            o_ref[pl.ds(r0, CONV_CHUNK), :] = (gate * _sigmoid(gate) * val).astype(o_ref.dtype)

    blk, halo, wspec = _conv_specs(bs, cw, lead=(2,))
    out, _, _ = _conv_specs(bs, cw)
    xv = _glu_view(x)
    return _call(body, name=name, out_shape=jax.ShapeDtypeStruct((half, s, c), MXU_DTYPE), grid=(half, c // cw, s // bs),
                 in_specs=[blk, halo, wspec(kk), wspec(1)], out_specs=out, scratch=[pltpu.VMEM((bs + HALO, cw), F32)] * 2,
                 sem=("parallel", "parallel", "arbitrary"))(xv, xv, _glu_view(w), _glu_view(b))


def conv_glu_bwd(x, da, w, b, *, name):
    s, c, bs, cw = _conv_tiles(x)
    kk = w.shape[1]
    half = x.shape[0] // 2

    def body(x_ref, h_ref, w_ref, b_ref, da_ref, du_ref, dw_ref, db_ref, gscr, vscr):
        i = pl.program_id(2)

        @pl.when(i == 0)
        def _():
            dw_ref[...] = jnp.zeros_like(dw_ref)
            db_ref[...] = jnp.zeros_like(db_ref)

        _stage(gscr, x_ref.at[0], h_ref.at[0], i == 0)
        _stage(vscr, x_ref.at[1], h_ref.at[1], i == 0)
        gsums, vsums = _TapSums(kk, cw), _TapSums(kk, cw)
        for r0 in range(0, bs, CONV_CHUNK):
            gtaps, vtaps = _conv_taps(gscr, kk, r0), _conv_taps(vscr, kk, r0)
            gate = _conv_sum(gtaps, w_ref.at[0], b_ref.at[0])
            val = _conv_sum(vtaps, w_ref.at[1], b_ref.at[1])
            dav = da_ref[pl.ds(r0, CONV_CHUNK), :]
            dgate = dav * val * _silu_grad(gate)
            dval = dav * gate * _sigmoid(gate)
            du_ref[0, pl.ds(r0, CONV_CHUNK), :] = dgate
            du_ref[1, pl.ds(r0, CONV_CHUNK), :] = dval
            gsums.add(dgate, gtaps)
            vsums.add(dval, vtaps)
        gsums.flush(dw_ref.at[0], db_ref.at[0])
        vsums.flush(dw_ref.at[1], db_ref.at[1])

    blk, halo, wspec = _conv_specs(bs, cw, lead=(2,))
    daspec, _, _ = _conv_specs(bs, cw)
    xv, wv, bv = _glu_view(x), _glu_view(w), _glu_view(b)
    du, dw, db = _call(body, name=name,
                       out_shape=[jax.ShapeDtypeStruct(xv.shape, F32), jax.ShapeDtypeStruct(wv.shape, F32),
                                  jax.ShapeDtypeStruct(bv.shape, F32)],
                       grid=(half, c // cw, s // bs), in_specs=[blk, halo, wspec(kk), wspec(1), daspec],
                       out_specs=[blk, wspec(kk), wspec(1)], scratch=[pltpu.VMEM((bs + HALO, cw), F32)] * 2,
                       sem=("parallel", "parallel", "arbitrary"))(xv, xv, wv, bv, da)
    return du.reshape(x.shape), dw.reshape(w.shape), db.reshape(b.shape)


def conv_t(du, w, *, name, out_dtype=F32):
    s, c, bs, cw = _conv_tiles(du)
    kk = w.shape[1]
    nb = s // bs

    def body(d_ref, h_ref, w_ref, o_ref, scr):
        last = pl.program_id(2) == nb - 1
        scr[0:bs, :] = d_ref[...]
        scr[bs:, :] = jnp.where(last, 0.0, h_ref[...])
        for r0 in range(0, bs, CONV_CHUNK):
            ext = scr[pl.ds(r0, CONV_CHUNK + HALO), :]
            ahead = lambda j: ext[:CONV_CHUNK] if j == 0 else pltpu.roll(ext, CONV_CHUNK + HALO - j, 0)[:CONV_CHUNK]
            acc = ahead(kk - 1) * w_ref[0:1, :]
            for k in range(1, kk):
                acc = acc + ahead(kk - 1 - k) * w_ref[k:k + 1, :]
            o_ref[pl.ds(r0, CONV_CHUNK), :] = acc.astype(out_dtype)

    blk, _, wspec = _conv_specs(bs, cw)
    halo = pl.BlockSpec((None, HALO, cw), lambda q, j, i: (q, jnp.minimum((i + 1) * (bs // HALO), s // HALO - 1), j))
    return _call(body, name=name, out_shape=jax.ShapeDtypeStruct(du.shape, out_dtype), grid=(du.shape[0], c // cw, nb),
                 in_specs=[blk, halo, wspec(kk)], out_specs=blk, scratch=[pltpu.VMEM((bs + HALO, cw), F32)],
                 sem=("parallel", "parallel", "arbitrary"))(du, du, w)


def _ssd_common(xbc_ref, tail_ref, dtrt_ref, bias_ref, biast_ref, alog_ref, alogt_ref):
    L = SSM_CHUNK
    raw = tail_ref[...] + bias_ref[...]
    dt = _softplus(raw)
    dtt = _softplus(dtrt_ref[...] + biast_ref[...])
    a = -jnp.exp(alog_ref[...])
    at = -jnp.exp(alogt_ref[...])
    row, col, lower = _tri(L, lambda r, c: r >= c)
    tril = lower.astype(F32)
    cs = _nn(tril, dt * a, HIGHEST)
    cst = _nt(dtt * at, tril, HIGHEST)
    bm = [xbc_ref[:, SSM_INNER + g * SSM_N: SSM_INNER + (g + 1) * SSM_N] for g in range(SSM_GROUPS)]
    off = SSM_INNER + SSM_GROUPS * SSM_N
    cm = [xbc_ref[:, off + g * SSM_N: off + (g + 1) * SSM_N] for g in range(SSM_GROUPS)]
    cb = [_bnt(cm[g], bm[g]) for g in range(SSM_GROUPS)]
    return raw, dt, a, lower, tril, cs, cst, bm, cm, cb


def _ssd_head(hh, xbc_ref, dt, cs, cst, lower):
    L = SSM_CHUNK
    ln = DT_LANE + hh
    x = xbc_ref[:, hh * SSM_P:(hh + 1) * SSM_P]
    dtc = dt[:, ln:ln + 1]
    csc = cs[:, ln:ln + 1]
    csr = cst[hh:hh + 1, :]
    decay = jnp.exp(jnp.where(lower, csc - csr, -1e30))
    last = cs[L - 1:L, ln:ln + 1]
    return x, dtc, csc, decay, jnp.exp(csc), jnp.exp(last - csc), jnp.exp(last)


def _ssd_inputs(tail, dt_bias, a_log, d_skip):
    H = SSM_HEADS
    lanes = lambda vec: jnp.pad(vec.reshape(1, H), ((0, 0), (DT_LANE, LANES - DT_LANE - H)))
    return (tail, tail[:, DT_LANE:DT_LANE + H].T, lanes(dt_bias), dt_bias.reshape(H, 1), lanes(a_log),
            a_log.reshape(H, 1), lanes(d_skip))


def ssd_fwd(xbc, tail, dt_bias, a_log, d_skip, *, name):
    s = xbc.shape[0]
    L, H, P, N = SSM_CHUNK, SSM_HEADS, SSM_P, SSM_N
    nc = s // L

    def body(xbc_ref, tail_ref, dtrt_ref, bias_ref, biast_ref, alog_ref, alogt_ref, d_ref, y_ref, hp_ref, state):
        @pl.when(pl.program_id(0) == 0)
        def _():
            state[...] = jnp.zeros_like(state)

        raw, dt, a, lower, tril, cs, cst, bm, cm, cb = _ssd_common(
            xbc_ref, tail_ref, dtrt_ref, bias_ref, biast_ref, alog_ref, alogt_ref)
        for hh in range(H):
            g = hh // (H // SSM_GROUPS)
            x, dtc, csc, decay, e, tau, gamma = _ssd_head(hh, xbc_ref, dt, cs, cst, lower)
            xdt = x * dtc
            hprev = state[hh]
            hp_ref[hh] = hprev
            skip = d_ref[:, DT_LANE + hh:DT_LANE + hh + 1]
            y = _bnn(cb[g] * decay, xdt) + _bnn(cm[g], hprev) * e + x * skip
            y_ref[:, hh * P:(hh + 1) * P] = y
            state[hh] = hprev * gamma + _btn(bm[g] * tau, xdt)

    row = lambda w: pl.BlockSpec((L, w), lambda c: (c, 0))
    small = lambda shp: pl.BlockSpec(shp, lambda c: (0, 0))
    return _call(body, name=name,
                 out_shape=[jax.ShapeDtypeStruct((s, SSM_INNER), F32), jax.ShapeDtypeStruct((nc, H, N, P), F32)],
                 grid=(nc,),
                 in_specs=[row(SSM_CONV_DIM), row(LANES), pl.BlockSpec((H, L), lambda c: (0, c)), small((1, LANES)),
                           small((H, 1)), small((1, LANES)), small((H, 1)), small((1, LANES))],
                 out_specs=[row(SSM_INNER), pl.BlockSpec((None, H, N, P), lambda c: (c, 0, 0, 0))],
                 scratch=[pltpu.VMEM((H, N, P), F32)], sem=("arbitrary",))(xbc, *_ssd_inputs(tail, dt_bias, a_log, d_skip))


def ssd_bwd(xbc, tail, dt_bias, a_log, d_skip, hprev_all, dy, *, name):
    s = xbc.shape[0]
    L, H, P, N = SSM_CHUNK, SSM_HEADS, SSM_P, SSM_N
    nc = s // L
    hg = H // SSM_GROUPS

    def body(xbc_ref, tail_ref, dtrt_ref, bias_ref, biast_ref, alog_ref, alogt_ref, d_ref, hp_ref, dy_ref,
             dxbc_ref, ddt_ref, dbias_ref, dalog_ref, dd_ref, dstate):
        @pl.when(pl.program_id(0) == 0)
        def _():
            dstate[...] = jnp.zeros_like(dstate)
            dbias_ref[...] = jnp.zeros_like(dbias_ref)
            dalog_ref[...] = jnp.zeros_like(dalog_ref)
            dd_ref[...] = jnp.zeros_like(dd_ref)

        raw, dt, a, lower, tril, cs, cst, bm, cm, cb = _ssd_common(
            xbc_ref, tail_ref, dtrt_ref, bias_ref, biast_ref, alog_ref, alogt_ref)
        lane = lax.broadcasted_iota(jnp.int32, (L, LANES), 1)
        lane1 = lax.broadcasted_iota(jnp.int32, (1, LANES), 1)
        rowi = lax.broadcasted_iota(jnp.int32, (L, 1), 0)
        slot = lax.broadcasted_iota(jnp.int32, (LANES, L), 0)
        col_sums = jnp.zeros((LANES, L), F32)
        dcs_all = jnp.zeros((L, LANES), F32)
        ddt_x = jnp.zeros((L, LANES), F32)
        dd_row = jnp.zeros((1, LANES), F32)
        dbm = [jnp.zeros((L, N), F32) for _ in range(SSM_GROUPS)]
        dcm = [jnp.zeros((L, N), F32) for _ in range(SSM_GROUPS)]
        dcb = [jnp.zeros((L, L), F32) for _ in range(SSM_GROUPS)]
        for hh in range(H):
            g = hh // hg
            ln = DT_LANE + hh
            x, dtc, csc, decay, e, tau, gamma = _ssd_head(hh, xbc_ref, dt, cs, cst, lower)
            xdt = x * dtc
            hprev = hp_ref[hh]
            dhn = dstate[hh]
            dyh = dy_ref[:, hh * P:(hh + 1) * P]
            m = cb[g] * decay
            dxdt = _btn(m, dyh) + _bnn(bm[g] * tau, dhn)
            dm = jnp.where(lower, _bnt(dyh, xdt), 0.0)
            dcb[g] = dcb[g] + dm * decay
            dseg = dm * m
            dcs = jnp.sum(dseg, axis=1, keepdims=True)
            col_sums = jnp.where(slot == ln, jnp.sum(dseg, axis=0, keepdims=True), col_sums)
            edy = e * dyh
            dcm[g] = dcm[g] + _bnt(edy, hprev)
            dcs = dcs + e * jnp.sum(dyh * _bnn(cm[g], hprev), axis=1, keepdims=True)
            xdh = _bnt(xdt, dhn)
            dbm[g] = dbm[g] + tau * xdh
            dtau_tau = jnp.sum(bm[g] * xdh, axis=1, keepdims=True) * tau
            dlast = jnp.sum(dtau_tau, axis=0, keepdims=True) + gamma * jnp.sum(dhn * hprev, keepdims=True)
            dcs = dcs - dtau_tau + jnp.where(rowi == L - 1, dlast, 0.0)
            dstate[hh] = gamma * dhn + _btn(cm[g], edy)
            dcs_all = jnp.where(lane == ln, dcs, dcs_all)
            ddt_x = jnp.where(lane == ln, jnp.sum(dxdt * x, axis=1, keepdims=True), ddt_x)
            dxbc_ref[:, hh * P:(hh + 1) * P] = dxdt * dtc + d_ref[:, ln:ln + 1] * dyh
            dd_row = jnp.where(lane1 == ln, jnp.sum(dyh * x, keepdims=True), dd_row)
        off = SSM_INNER + SSM_GROUPS * SSM_N
        for g in range(SSM_GROUPS):
            dxbc_ref[:, SSM_INNER + g * N: SSM_INNER + (g + 1) * N] = dbm[g] + _btn(dcb[g], cm[g])
            dxbc_ref[:, off + g * N: off + (g + 1) * N] = dcm[g] + _bnn(dcb[g], bm[g])
        dcs_all = dcs_all - jnp.transpose(col_sums)
        dda = _tn(tril, dcs_all, HIGHEST)
        head_lane = (lane >= DT_LANE) & (lane < DT_LANE + H)
        draw = jnp.where(head_lane, (dda * a + ddt_x) * _sigmoid(raw), 0.0)
        ddt_ref[...] = draw
        dbias_ref[...] += jnp.sum(draw, axis=0, keepdims=True)
        dalog_ref[...] += jnp.sum(jnp.where(head_lane, dda * dt, 0.0), axis=0, keepdims=True) * a
        dd_ref[...] += dd_row

    rev = lambda c: nc - 1 - c
    row = lambda w: pl.BlockSpec((L, w), lambda c: (rev(c), 0))
    small = lambda shp: pl.BlockSpec(shp, lambda c: (0, 0))
    acc = pl.BlockSpec((1, LANES), lambda c: (0, 0))
    vec = jax.ShapeDtypeStruct((1, LANES), F32)
    return _call(body, name=name,
                 out_shape=[jax.ShapeDtypeStruct((s, SSM_CONV_DIM), F32), jax.ShapeDtypeStruct((s, LANES), F32), vec, vec, vec],
                 grid=(nc,),
                 in_specs=[row(SSM_CONV_DIM), row(LANES), pl.BlockSpec((H, L), lambda c: (0, rev(c))), small((1, LANES)),
                           small((H, 1)), small((1, LANES)), small((H, 1)), small((1, LANES)),
                           pl.BlockSpec((None, H, N, P), lambda c: (rev(c), 0, 0, 0)), row(SSM_INNER)],
                 out_specs=[row(SSM_CONV_DIM), row(LANES), acc, acc, acc],
                 scratch=[pltpu.VMEM((H, N, P), F32)], sem=("arbitrary",))(
        xbc, *_ssd_inputs(tail, dt_bias, a_log, d_skip), hprev_all, dy)


def _heads(x2d, n, d):
    s = x2d.shape[0]
    return x2d.reshape(s, n, d).transpose(1, 0, 2)


def _unheads(x3d):
    n, s, d = x3d.shape
    return x3d.transpose(1, 0, 2).reshape(s, n * d)


def layer_fwd(h, p, tabs, li, gather_late=None):
    s = h.shape[0]
    tabq, tabt = tabs
    nm = lambda t: f"L{li}_{t}"
    r = {'h': h}
    hn = rms_fwd(h, p['mix_norm'], name=nm('mixnorm'), out_dtype=MXU_DTYPE)
    proj = mm(hn, p['w_in'], name=nm('proj'))
    r.update(hn=hn, proj=proj)
    qkv = proj[:, :3 * SB_WIDTH].astype(MXU_DTYPE)
    late = None
    if gather_late is None:
        ya, tot = sb_fwd(qkv, name=nm('sb_fwd'))
    else:
        ya, tot, *got = sb_fwd(qkv, name=nm('sb_fwd'), carried=Carried('gather', [gather_late[n] for n in LATE]))
        late = assemble_late(dict(zip(LATE, got)))
        p = dict(p, **{n: late[n][li] for n in LATE})
    yan = rms_fwd(ya, p['sb_out_norm'], name=nm('sbnorm'), out_dtype=MXU_DTYPE)
    r.update(qkv=qkv, ya=ya, tot=tot)
    z = proj[:, 768:1280]
    xbc = proj[None, :, 1280:2048]
    tail = proj[:, TAIL:TAIL + LANES]
    xbc_act = conv_silu_fwd(xbc, p['ssm_conv_w'], p['ssm_conv_b'], name=nm('ssmconv'))[0]
    y_ssm, hprev = ssd_fwd(xbc_act, tail, p['ssm_dt_bias'], p['ssm_a_log'], p['ssm_d'], name=nm('ssd_fwd'))
    ybn = rms_fwd(y_ssm, p['ssm_out_norm'], name=nm('ssmnorm'), gate=z, out_dtype=MXU_DTYPE)
    r.update(z=z, xbc=xbc, tail=tail, xbc_act=xbc_act, y_ssm=y_ssm, hprev=hprev)
    cq = proj[:, 2048:2304]
    ckv = proj[:, 2304:2432]
    qn = rms_fwd(cq, p['mla_q_norm'], name=nm('qnorm'), out_dtype=MXU_DTYPE)
    q_r = rope(mm(qn, p['mla_w_uq'], name=nm('uq'))[None], tabq, name=nm('ropeq'))
    kvn = rms_fwd(ckv, p['mla_kv_norm'], name=nm('kvnorm'), out_dtype=MXU_DTYPE)
    kv = mm(kvn, p['mla_w_ukv'], name=nm('ukv'))
    k_pe = rope(tail[None], tabt, name=nm('ropek'))[:, :MLA_ROPE]
    qh = _heads(q_r, MLA_HEADS, MLA_QK).astype(MXU_DTYPE)
    kvh = _heads(kv, MLA_HEADS, MLA_NOPE + MLA_V)
    kh = jnp.concatenate([kvh[..., :MLA_NOPE], jnp.broadcast_to(k_pe[None], (MLA_HEADS, s, MLA_ROPE))],
                         axis=-1).astype(MXU_DTYPE)
    vh = kvh[..., MLA_NOPE:].astype(MXU_DTYPE)
    yc_h, lse = mla_fwd(qh, kh, vh, name=nm('mla_fwd'))
    yc = _unheads(yc_h)
    ycn = rms_fwd(yc, p['mla_out_norm'], name=nm('mlanorm'), out_dtype=MXU_DTYPE)
    r.update(cq=cq, ckv=ckv, qn=qn, kvn=kvn, qh=qh, kh=kh, vh=vh, yc_h=yc_h, yc=yc, lse=lse)
    ycat = jnp.concatenate([yan, ybn, ycn], axis=1)
    h1 = mm(ycat, p['w_out'], name=nm('outproj'), res=h)
    hn2 = rms_fwd(h1, p['ffn_norm'], name=nm('ffnnorm'), out_dtype=MXU_DTYPE)
    up = mm(hn2, p['ffn_w_up'], name=nm('up'), bb='o')
    act = conv_glu_fwd(up, p['ffn_conv_w'], p['ffn_conv_b'], name=nm('glu'))
    h2 = mm(act, p['ffn_w_down'], name=nm('down'), ab='k', bb='k', res=h1)
    r.update(ycat=ycat, h1=h1, hn2=hn2, up=up, act=act)
    return h2, r, p, late


def layer_bwd(dh2, p, r, tabs, li, scatter_late=None):
    s = dh2.shape[0]
    tabq, tabt = tabs
    nm = lambda t: f"L{li}_{t}"
    g = {}
    dact = mm(dh2, p['ffn_w_down'], name=nm('d_down_x'), tb=True, bb='o')
    g['ffn_w_down'] = mm(r['act'], dh2, name=nm('d_down_w'), out_dtype=WIRE_DTYPE, ta=True, ab='o')
    du, g['ffn_conv_w'], g['ffn_conv_b'] = conv_glu_bwd(r['up'], dact, p['ffn_conv_w'], p['ffn_conv_b'], name=nm('d_glu'))
    dup = conv_t(du, p['ffn_conv_w'], name=nm('d_ffnconv'), out_dtype=MXU_DTYPE)
    g['ffn_w_up'] = mm(r['hn2'], dup, name=nm('d_up_w'), out_dtype=WIRE_DTYPE, ta=True, bb='o')
    dhn2 = mm(dup, p['ffn_w_up'], name=nm('d_up_x'), tb=True, ab='k', bb='k')
    dh1, dg = rms_bwd(r['h1'], p['ffn_norm'], dhn2, name=nm('d_ffnnorm'), add=dh2)
    g['ffn_norm'] = dg[0]
    dycat = mm(dh1, p['w_out'], name=nm('d_out_x'), tb=True)
    g['w_out'] = mm(r['ycat'], dh1, name=nm('d_out_w'), out_dtype=WIRE_DTYPE, ta=True)
    dya, dg = rms_bwd(r['ya'], p['sb_out_norm'], dycat[:, :256], name=nm('d_sbnorm'))
    g['sb_out_norm'] = dg[0]
    recv_late = None
    if scatter_late is None:
        dq, dk, dv = sb_bwd(r['qkv'], dya, r['tot'], name=nm('sb_bwd'))
    else:
        parts = owner_parts_late([g] + list(scatter_late))
        dq, dk, dv, *got = sb_bwd(r['qkv'], dya, r['tot'], name=nm('sb_bwd'),
                                  carried=Carried('scatter', [parts[n].astype(WIRE_DTYPE) for n in LATE]))
        recv_late = dict(zip(LATE, got))
    dyssm, dz, dg = rms_bwd(r['y_ssm'], p['ssm_out_norm'], dycat[:, 256:768], name=nm('d_ssmnorm'), gate=r['z'])
    g['ssm_out_norm'] = dg[0]
    dxbc_act, ddt_tail, dbias, dalog, dd = ssd_bwd(r['xbc_act'], r['tail'], p['ssm_dt_bias'], p['ssm_a_log'],
                                                   p['ssm_d'], r['hprev'], dyssm, name=nm('ssd_bwd'))
    hl = slice(DT_LANE, DT_LANE + SSM_HEADS)
    g['ssm_dt_bias'], g['ssm_a_log'], g['ssm_d'] = dbias[0, hl], dalog[0, hl], dd[0, hl]
    dxbc_u, g['ssm_conv_w'], g['ssm_conv_b'] = conv_silu_bwd(r['xbc'], dxbc_act[None], p['ssm_conv_w'], p['ssm_conv_b'],
                                                             name=nm('d_ssmact'))
    dxbc = conv_t(dxbc_u, p['ssm_conv_w'], name=nm('d_ssmconv'))[0]
    dyc, dg = rms_bwd(r['yc'], p['mla_out_norm'], dycat[:, 768:], name=nm('d_mlanorm'))
    g['mla_out_norm'] = dg[0]
    dqh, dkh, dvh = mla_bwd(r['qh'], r['kh'], r['vh'], r['yc_h'], _heads(dyc, MLA_HEADS, MLA_V), r['lse'], name=nm('mla_bwd'))
    dq_c = rope(_unheads(dqh)[None], tabq, name=nm('d_ropeq'), backward=True)
    g['mla_w_uq'] = mm(r['qn'], dq_c, name=nm('d_uq_w'), out_dtype=WIRE_DTYPE, ta=True)
    dcq, dg = rms_bwd(r['cq'], p['mla_q_norm'], mm(dq_c, p['mla_w_uq'], name=nm('d_uq_x'), tb=True), name=nm('d_qnorm'))
    g['mla_q_norm'] = dg[0]
    dkv = _unheads(jnp.concatenate([dkh[..., :MLA_NOPE], dvh], axis=-1))
    g['mla_w_ukv'] = mm(r['kvn'], dkv, name=nm('d_ukv_w'), out_dtype=WIRE_DTYPE, ta=True)
    dckv, dg = rms_bwd(r['ckv'], p['mla_kv_norm'], mm(dkv, p['mla_w_ukv'], name=nm('d_ukv_x'), tb=True), name=nm('d_kvnorm'))
    g['mla_kv_norm'] = dg[0]
    dkpe = jnp.pad(dkh[..., MLA_NOPE:], ((0, 0), (0, 0), (0, LANES - MLA_ROPE)))
    dtail = rope(dkpe, tabt, name=nm('d_ropek'), backward=True, add=ddt_tail)
    dproj = jnp.concatenate([dq, dk, dv, dz, dxbc, dcq, dckv, dtail], axis=1).astype(MXU_DTYPE)
    g['w_in'] = mm(r['hn'], dproj, name=nm('d_proj_w'), out_dtype=WIRE_DTYPE, ta=True)
    dhn = mm(dproj, p['w_in'], name=nm('d_proj_x'), tb=True)
    dh, dg = rms_bwd(r['h'], p['mix_norm'], dhn, name=nm('d_mixnorm'), add=dh1)
    g['mix_norm'] = dg[0]
    return dh, g, recv_late


def _w_in_placement():
    c = np.arange(D_IN)
    dest = np.where(c < 2048, c, np.where(c < 2056, c + (D_IN - 2056), c - 8))
    dest = jnp.asarray(dest.reshape(N_DEV, D_IN // N_DEV, 1), jnp.int32)
    return (dest == jnp.arange(D_IN_PAD, dtype=jnp.int32)).astype(MXU_DTYPE)


def _owner_major(full, axis):
    shp = full.shape
    return jnp.moveaxis(full.reshape(shp[:axis] + (N_DEV, shp[axis] // N_DEV) + shp[axis + 1:]), axis, 0)


def _owner_join(parts, axis):
    moved = jnp.moveaxis(parts, 0, axis)
    shp = moved.shape
    return moved.reshape(shp[:axis] + (shp[axis] * shp[axis + 1],) + shp[axis + 2:])


def assemble_early(gathered, replicated):
    L = DEPTH
    out = dict(replicated)
    out['w_in'] = mm(gathered['w_in'].reshape(N_DEV, L * D_MODEL, D_IN // N_DEV), _w_in_placement(), name='place_w_in',
                     ab='k', bb='k', out_dtype=MXU_DTYPE).reshape(L, D_MODEL, D_IN_PAD)
    out['mla_w_uq'] = _owner_join(gathered['mla_w_uq'], 2)
    out['mla_w_ukv'] = _owner_join(gathered['mla_w_ukv'], 2)
    out['ssm_conv_w'] = _owner_join(gathered['ssm_conv_w'], 2)[:, None]
    out['ssm_conv_b'] = replicated['ssm_conv_b'].reshape(L, 1, 1, SSM_CONV_DIM)
    out['ffn_conv_b'] = replicated['ffn_conv_b'].reshape(L, N_DEV, 1, FF_SHARD)
    return out


def assemble_late(gathered):
    L = DEPTH
    return {'ffn_w_up': jnp.moveaxis(gathered['ffn_w_up'], 1, 0),
            'w_out': _owner_join(gathered['w_out'], 1),
            'ffn_w_down': _owner_join(gathered['ffn_w_down'], 1).reshape(L, N_DEV // 2, FF_SHARD, D_MODEL),
            'ffn_conv_w': jnp.moveaxis(gathered['ffn_conv_w'], 1, 0)}


def owner_parts_late(grads):
    L = DEPTH
    st = lambda n: jnp.stack([g[n] for g in grads])
    return {'ffn_w_up': jnp.moveaxis(st('ffn_w_up'), 1, 0),
            'w_out': _owner_major(st('w_out'), 1),
            'ffn_w_down': _owner_major(st('ffn_w_down').reshape(L, D_FF, D_MODEL), 1),
            'ffn_conv_w': jnp.moveaxis(st('ffn_conv_w'), 1, 0)}


def owner_parts_early(grads):
    L = DEPTH
    st = lambda n: jnp.stack([g[n] for g in grads])
    parts = {
        'w_in': mm(st('w_in').reshape(L * D_MODEL, D_IN_PAD), _w_in_placement(), name='unplace_w_in', tb=True, bb='o',
                   out_dtype=WIRE_DTYPE).reshape(N_DEV, L, D_MODEL, D_IN // N_DEV),
        'mla_w_uq': _owner_major(st('mla_w_uq'), 2),
        'mla_w_ukv': _owner_major(st('mla_w_ukv'), 2),
        'ssm_conv_w': _owner_major(st('ssm_conv_w')[:, 0], 2),
    }
    rep = {n: st(n) for n in REPLICATED if n not in ('final_norm', 'ssm_conv_b', 'ffn_conv_b')}
    rep['ssm_conv_b'] = st('ssm_conv_b').reshape(L, SSM_CONV_DIM)
    rep['ffn_conv_b'] = st('ffn_conv_b').reshape(L, 2 * D_FF)
    return parts, rep


def local_step(x, positions, target, early, late_shards, replicated):
    s = x.shape[0]
    tabs = _rope_tables(positions, s)
    params = assemble_early(early, replicated)
    layer = lambda li: {n: params[n][li] for n in params if n != 'final_norm'}
    h, r0, p0, late = layer_fwd(x, layer(0), tabs, 0, gather_late=late_shards)
    saved = [(p0, r0)]
    for li in range(1, DEPTH):
        h, r, p, _ = layer_fwd(h, dict(layer(li), **{n: late[n][li] for n in LATE}), tabs, li)
        saved.append((p, r))
    y = rms_fwd(h, params['final_norm'], name='finalnorm')
    dy, loss = loss_head(y, target, name='loss')
    dh, dg = rms_bwd(h, params['final_norm'], dy, name='d_finalnorm')
    above = []
    for li in reversed(range(1, DEPTH)):
        dh, g, _ = layer_bwd(dh, *saved[li], tabs, li)
        above.insert(0, g)
    dh, g0, recv_late = layer_bwd(dh, *saved[0], tabs, 0, scatter_late=above)
    parts, rep = owner_parts_early([g0] + above)
    rep['final_norm'] = dg[0]
    return loss[0, 0], dh, parts, recv_late, rep


def all_gather(blocks, *, name):
    n = len(blocks)

    def body(*refs):
        x_refs, out_refs = refs[:n], refs[n:2 * n]
        send_sems, recv_sems, local_sems = refs[2 * n:]
        x, y, c = lax.axis_index("x"), lax.axis_index("y"), lax.axis_index("c")
        me, sibling = (x, y, c), (x, y, 1 - c)
        chips = [(1 - x, y), (x, 1 - y), (1 - x, 1 - y)]

        def slot(b, px, py, pc):
            return out_refs[b].at[4 * px + 2 * py + pc]

        def copy(b, k, blk, to, src=None):
            return pltpu.make_async_remote_copy(src_ref=slot(b, *blk) if src is None else src, dst_ref=slot(b, *blk),
                                                send_sem=send_sems.at[b, k], recv_sem=recv_sems.at[b, k],
                                                device_id=to, device_id_type=MESH)

        mine = [pltpu.make_async_copy(x_refs[b], slot(b, *me), local_sems.at[b]) for b in range(n)]
        for cp in mine:
            cp.start()
        first = []
        for b in range(n):
            first.append(copy(b, 0, me, sibling, src=x_refs[b]))
            first += [copy(b, 1 + j, me, (*chip, c), src=x_refs[b]) for j, chip in enumerate(chips)]
        for cp in first:
            cp.start()
        passed = []
        for j, chip in enumerate(chips):
            for b in range(n):
                copy(b, 1 + j, (*chip, c), me).wait_recv()
                fwd = copy(b, 4 + j, (*chip, c), sibling)
                fwd.start()
                passed.append(fwd)
        for b in range(n):
            copy(b, 0, sibling, me).wait_recv()
            for j, chip in enumerate(chips):
                copy(b, 4 + j, (*chip, 1 - c), me).wait_recv()
        for cp in first + passed:
            cp.wait_send()
        for cp in mine:
            cp.wait()

    return pl.pallas_call(
        body, name=name, out_shape=[jax.ShapeDtypeStruct((N_DEV,) + b.shape, b.dtype) for b in blocks],
        in_specs=[HBM] * n, out_specs=[HBM] * n,
        scratch_shapes=[pltpu.SemaphoreType.DMA((n, 7)), pltpu.SemaphoreType.DMA((n, 7)), pltpu.SemaphoreType.DMA((n,))],
    )(*blocks)


def all_to_all(parts, *, name):
    n = len(parts)

    def body(*refs):
        g_refs, r_refs = refs[:n], refs[n:2 * n]
        send_sems, recv_sems, local_sems = refs[2 * n:]
        x, y, c = lax.axis_index("x"), lax.axis_index("y"), lax.axis_index("c")
        me = 4 * x + 2 * y + c
        mine = [pltpu.make_async_copy(g_refs[b].at[me], r_refs[b].at[me], local_sems.at[b]) for b in range(n)]
        for cp in mine:
            cp.start()
        copies = []
        for k in range(1, N_DEV):
            px, py, pc = _flip(x, k & 4), _flip(y, k & 2), _flip(c, k & 1)
            peer = 4 * px + 2 * py + pc
            for b in range(n):
                cp = pltpu.make_async_remote_copy(src_ref=g_refs[b].at[peer], dst_ref=r_refs[b].at[me],
                                                  send_sem=send_sems.at[b, k - 1], recv_sem=recv_sems.at[b, k - 1],
                                                  device_id=(px, py, pc), device_id_type=MESH)
                cp.start()
                copies.append(cp)
        for cp in copies:
            cp.wait_send()
            cp.wait_recv()
        for cp in mine:
            cp.wait()

    return pl.pallas_call(
        body, name=name, out_shape=[jax.ShapeDtypeStruct(p.shape, p.dtype) for p in parts],
        in_specs=[HBM] * n, out_specs=[HBM] * n,
        scratch_shapes=[pltpu.SemaphoreType.DMA((n, 7)), pltpu.SemaphoreType.DMA((n, 7)), pltpu.SemaphoreType.DMA((n,))],
    )(*parts)


def adamw(parts, w, m, v, *, name):
    r, wd = w.shape
    br = _tile(r, (256, 128, 64, 32, 16, 8))
    c1 = 1.0 - ADAM_B1 ** ADAM_STEP
    c2 = 1.0 - ADAM_B2 ** ADAM_STEP

    def body(p_ref, w_ref, m_ref, v_ref, g_ref, d_ref, mo_ref, vo_ref):
        g = p_ref[0].astype(F32)
        for j in range(1, N_DEV):
            g = g + p_ref[j].astype(F32)
        mn = ADAM_B1 * m_ref[...] + (1.0 - ADAM_B1) * g
        vn = ADAM_B2 * v_ref[...] + (1.0 - ADAM_B2) * (g * g)
        g_ref[...] = g
        mo_ref[...] = mn
        vo_ref[...] = vn
        d_ref[...] = -ADAM_LR * ((mn / c1) / (jnp.sqrt(vn / c2) + ADAM_EPS) + ADAM_WD * w_ref[...])

    blk = pl.BlockSpec((br, wd), lambda i: (i, 0))
    out = jax.ShapeDtypeStruct((r, wd), F32)
    return _call(body, name=name, out_shape=[out] * 4, grid=(r // br,),
                 in_specs=[pl.BlockSpec((N_DEV, br, wd), lambda i: (0, i, 0)), blk, blk, blk], out_specs=[blk] * 4,
                 sem=("parallel",))(parts, w, m, v)


def _pack(arrs):
    flat = jnp.concatenate([a.reshape(-1) for a in arrs])
    rows = -(-flat.shape[0] // (8 * FLAT_W)) * 8
    return jnp.pad(flat, (0, rows * FLAT_W - flat.shape[0])).reshape(rows, FLAT_W)


def _unpack(flat, shapes):
    flat = flat.reshape(-1)
    out, off = [], 0
    for shp in shapes:
        n = int(np.prod(shp))
        out.append(flat[off:off + n].reshape(shp))
        off += n
    return out


def kernel(x, positions, mix_norm, w_in, sb_out_norm, ssm_conv_w, ssm_conv_b, ssm_dt_bias, ssm_a_log, ssm_d, ssm_out_norm, mla_q_norm, mla_w_uq, mla_kv_norm, mla_w_ukv, mla_out_norm, w_out, ffn_norm, ffn_w_up, ffn_conv_w, ffn_conv_b, ffn_w_down, final_norm, loss_target, m_mix_norm, m_w_in, m_sb_out_norm, m_ssm_conv_w, m_ssm_conv_b, m_ssm_dt_bias, m_ssm_a_log, m_ssm_d, m_ssm_out_norm, m_mla_q_norm, m_mla_w_uq, m_mla_kv_norm, m_mla_w_ukv, m_mla_out_norm, m_w_out, m_ffn_norm, m_ffn_w_up, m_ffn_conv_w, m_ffn_conv_b, m_ffn_w_down, m_final_norm, v_mix_norm, v_w_in, v_sb_out_norm, v_ssm_conv_w, v_ssm_conv_b, v_ssm_dt_bias, v_ssm_a_log, v_ssm_d, v_ssm_out_norm, v_mla_q_norm, v_mla_w_uq, v_mla_kv_norm, v_mla_w_ukv, v_mla_out_norm, v_w_out, v_ffn_norm, v_ffn_w_up, v_ffn_conv_w, v_ffn_conv_b, v_ffn_w_down, v_final_norm):
    args = locals()
    w = {n: args[n] for n in WEIGHTS}
    m = {n: args['m_' + n] for n in WEIGHTS}
    v = {n: args['v_' + n] for n in WEIGHTS}
    wire = lambda n: w[n] if n in VPU_WEIGHTS else w[n].astype(BF16)
    early = dict(zip(EARLY, all_gather([wire(n) for n in EARLY], name='gather_early')))

    loss, dx, parts, recv, rep = local_step(x[0], positions[0], loss_target[0], early, {n: wire(n) for n in LATE},
                                            {n: w[n] for n in REPLICATED})
    loss = lax.psum(loss, ("x", "y", "c"))

    recv.update(zip(EARLY, all_to_all([parts[n].astype(WIRE_DTYPE) for n in EARLY], name='scatter_early')))
    res = {kind: {} for kind in 'gdmv'}
    for n, rv in recv.items():
        shp = w[n].shape
        two_d = (int(np.prod(shp[:-1])), shp[-1])
        outs = adamw(rv.reshape((N_DEV,) + two_d), w[n].reshape(two_d), m[n].reshape(two_d), v[n].reshape(two_d),
                     name='adamw_' + n)
        for kind, o in zip('gdmv', outs):
            res[kind][n] = o.reshape(shp)

    rep_shapes = [w[n].shape for n in REPLICATED]
    (rparts,) = all_gather([_pack([rep[n] for n in REPLICATED])], name='gather_small_grads')
    rflat = lambda d: _pack([d[n] for n in REPLICATED])
    routs = adamw(rparts, rflat(w), rflat(m), rflat(v), name='adamw_replicated')
    for kind, o in zip('gdmv', routs):
        res[kind].update(zip(REPLICATED, _unpack(o, rep_shapes)))

    return (loss, dx[None], *[res['g'][n] for n in WEIGHTS], *[res['d'][n] for n in WEIGHTS],
            *[res['m'][n] for n in WEIGHTS], *[res['v'][n] for n in WEIGHTS])
```

```python
import numpy as np
import jax
import jax.numpy as jnp
from jax import lax
from jax.experimental import pallas as pl
from jax.experimental.pallas import tpu as pltpu

F32 = jnp.float32
BF16 = jnp.bfloat16
MXU_DTYPE = jnp.bfloat16
HIGHEST = lax.Precision.HIGHEST
WIRE_DTYPE = jnp.bfloat16

N_DEV = 8
D_MODEL = 1024
DEPTH = 2
EPS = 1e-6
SB_HEADS, SB_DIM = 4, 64
SB_WIDTH = SB_HEADS * SB_DIM
SSM_HEADS, SSM_P, SSM_GROUPS, SSM_N, SSM_CONV, SSM_CHUNK = 8, 64, 2, 64, 4, 128
SSM_INNER = SSM_HEADS * SSM_P
SSM_CONV_DIM = SSM_INNER + 2 * SSM_GROUPS * SSM_N
MLA_HEADS, MLA_NOPE, MLA_ROPE, MLA_V, MLA_Q_RANK, MLA_KV_RANK = 4, 64, 32, 64, 256, 128
MLA_QK = MLA_NOPE + MLA_ROPE
ROPE_THETA = 10000.0
D_IN = 2472
D_IN_PAD = 2560
TAIL = 2432
DT_LANE = 32
D_FF = 2816
FF_SHARD = 2 * D_FF // N_DEV
ADAM_LR, ADAM_B1, ADAM_B2, ADAM_EPS, ADAM_WD, ADAM_STEP = 0.001, 0.9, 0.999, 1e-08, 0.01, 10

LANES = 128
ATT_BLK = 256
SB_WIDE = 2
MLA_WIDE = 4
ROW_BLK = 512
ROW_BLOCK_BYTES = 2 << 20
CONV_COLS = 256
FLAT_W = 1024
VMEM_LIMIT = 56 << 20
MM_TM = (1024, 512, 256, 128)
MM_TN = (1280, 1024, 768, 640, 512, 384, 256, 128)
MM_TK = (1280, 1024, 512, 256, 128)

WEIGHTS = ['mix_norm', 'w_in', 'sb_out_norm', 'ssm_conv_w', 'ssm_conv_b', 'ssm_dt_bias', 'ssm_a_log', 'ssm_d',
           'ssm_out_norm', 'mla_q_norm', 'mla_w_uq', 'mla_kv_norm', 'mla_w_ukv', 'mla_out_norm', 'w_out',
           'ffn_norm', 'ffn_w_up', 'ffn_conv_w', 'ffn_conv_b', 'ffn_w_down', 'final_norm']
SHARDED = {'w_in': 2, 'ssm_conv_w': 2, 'mla_w_uq': 2, 'mla_w_ukv': 2, 'w_out': 1, 'ffn_w_up': 2, 'ffn_conv_w': 2,
           'ffn_w_down': 1}
VPU_WEIGHTS = ('ssm_conv_w', 'ffn_conv_w')
EARLY = ('w_in', 'mla_w_uq', 'mla_w_ukv', 'ssm_conv_w')
LATE = ('w_out', 'ffn_w_up', 'ffn_conv_w', 'ffn_w_down')
REPLICATED = [n for n in WEIGHTS if n not in SHARDED]


def _call(body, *, name, out_shape, grid=(), in_specs=None, out_specs=None, scratch=(), sem=None, **kw):
    params = dict(vmem_limit_bytes=VMEM_LIMIT)
    if sem is not None:
        params['dimension_semantics'] = sem
    return pl.pallas_call(body, name=name, out_shape=out_shape, grid=grid, in_specs=in_specs, out_specs=out_specs,
                          scratch_shapes=list(scratch), compiler_params=pltpu.CompilerParams(**params), **kw)


def _tile(n, prefs):
    for t in prefs:
        if n % t == 0:
            return t
    return n


def _rows(s, w):
    rows = ROW_BLK
    while rows * 2 <= s and s % (rows * 2) == 0 and rows * 2 * w * 4 <= ROW_BLOCK_BYTES:
        rows *= 2
    return _tile(s, (rows,))


def _dot(a, b, dims, precision=None):
    return lax.dot_general(a, b, (dims, ((), ())), preferred_element_type=F32, precision=precision)


def _nn(a, b, precision=None):
    return _dot(a, b, ((1,), (0,)), precision)


def _nt(a, b, precision=None):
    return _dot(a, b, ((1,), (1,)), precision)


def _tn(a, b, precision=None):
    return _dot(a, b, ((0,), (0,)), precision)


def _mxu(f):
    return lambda a, b: f(a.astype(MXU_DTYPE), b.astype(MXU_DTYPE))


_bnn, _bnt, _btn = _mxu(_nn), _mxu(_nt), _mxu(_tn)


def _split2(x):
    hi = x.astype(MXU_DTYPE)
    lo = (x - hi.astype(F32)).astype(MXU_DTYPE)
    return hi, lo


def _sigmoid(x):
    return 0.5 * jnp.tanh(0.5 * x) + 0.5


def _softplus(x):
    return jnp.maximum(x, 0.0) + jnp.log1p(jnp.exp(-jnp.abs(x)))


def _softplus_att(x):
    return jnp.maximum(x, 0.0) + jnp.log(1.0 + jnp.exp(-jnp.abs(x)))


def _cum(x, u):
    rows, b = x.shape[0], u.shape[0]
    n = x.shape[1] // b
    hi, lo = _split2(x)
    stack = [part[:, t * b:(t + 1) * b] for part in (hi, lo) for t in range(n)]
    r = _nn(jnp.concatenate(stack, axis=0), u)
    return jnp.concatenate([r[t * rows:(t + 1) * rows] + r[(n + t) * rows:(n + t + 1) * rows] for t in range(n)], axis=1)


def _causal_loop(qi, tile, carry, reverse, width):
    last = qi // width
    if reverse:
        return lax.fori_loop(0, last, lambda i, c: tile((last - 1 - i) * width, c, False), tile(last * width, carry, True))
    return tile(last * width, lax.fori_loop(0, last, lambda i, c: tile(i * width, c, False), carry), True)


def _causal_mask(blk, width, qi, kb, heads, strict, keys_on_rows=False):
    shape = (width * blk, blk) if keys_on_rows else (heads * blk, width * blk)
    q_idx = lax.broadcasted_iota(jnp.int32, shape, 1 if keys_on_rows else 0)
    k_idx = lax.broadcasted_iota(jnp.int32, shape, 0 if keys_on_rows else 1)
    if heads > 1:
        q_idx = q_idx % blk
    gap = (qi - kb) * blk
    return k_idx < q_idx + gap if strict else k_idx <= q_idx + gap


def mm(a, b, *, name, ta=False, tb=False, res=None, out_dtype=F32, ab=None, bb=None, precision=None):
    a2, b2 = a.shape[-2:], b.shape[-2:]
    (kdim, m) = a2 if ta else a2[::-1]
    (n, k2) = b2 if tb else b2[::-1]
    assert kdim == k2, (a.shape, b.shape, ta, tb)
    assert (ab == 'k') == (bb == 'k')
    kb = ab == 'k'
    nb = a.shape[0] if ab == 'o' else (b.shape[0] if bb == 'o' else None)
    tm, tn = _tile(m, MM_TM), _tile(n, MM_TN)
    tk = kdim if kb else _tile(kdim, MM_TK)
    nk = a.shape[0] if kb else kdim // tk
    dims = ((0 if ta else 1,), (1 if tb else 0,))
    op_dtype = F32 if precision is not None else MXU_DTYPE

    def body(*refs):
        a_ref, b_ref = refs[0], refs[1]
        r_ref = refs[2] if res is not None else None
        o_ref = refs[3] if res is not None else refs[2]
        part = _dot(a_ref[...].astype(op_dtype), b_ref[...].astype(op_dtype), dims, precision)

        def finish(out):
            if res is not None:
                out = out + r_ref[...]
            o_ref[...] = out.astype(out_dtype)

        if nk == 1:
            finish(part)
            return
        acc = refs[-1]
        k = pl.program_id(3)

        @pl.when(k == 0)
        def _():
            acc[...] = part

        @pl.when(k > 0)
        def _():
            acc[...] += part

        @pl.when(k == nk - 1)
        def _():
            finish(acc[...])

    def spec(blk, idx, how):
        if how is None:
            return pl.BlockSpec(blk, idx)
        if how == 'o':
            return pl.BlockSpec((None,) + blk, lambda p, i, j, k: (p,) + idx(p, i, j, k))
        return pl.BlockSpec((None,) + blk, lambda p, i, j, k: (k,) + idx(p, i, j, 0))

    a_spec = spec((tk, tm), lambda p, i, j, k: (k, i), ab) if ta else spec((tm, tk), lambda p, i, j, k: (i, k), ab)
    b_spec = spec((tn, tk), lambda p, i, j, k: (j, k), bb) if tb else spec((tk, tn), lambda p, i, j, k: (k, j), bb)
    o_spec = spec((tm, tn), lambda p, i, j, k: (i, j), None if nb is None else 'o')
    ins, specs = [a, b], [a_spec, b_spec]
    if res is not None:
        ins.append(res)
        specs.append(o_spec)
    out_shape = (m, n) if nb is None else (nb, m, n)
    return _call(body, name=name, out_shape=jax.ShapeDtypeStruct(out_shape, out_dtype),
                 grid=(1 if nb is None else nb, m // tm, n // tn, nk), in_specs=specs, out_specs=o_spec,
                 scratch=[] if nk == 1 else [pltpu.VMEM((tm, tn), F32)],
                 sem=("parallel", "parallel", "parallel", "arbitrary"))(*ins)


def rms_fwd(x, g, *, name, gate=None, out_dtype=F32):
    s, w = x.shape
    bs = _rows(s, w)

    def body(*refs):
        if gate is None:
            x_ref, g_ref, o_ref = refs
            u = x_ref[...]
        else:
            x_ref, z_ref, g_ref, o_ref = refs
            z = z_ref[...]
            u = x_ref[...] * (z * _sigmoid(z))
        r = lax.rsqrt(jnp.mean(u * u, axis=1, keepdims=True) + EPS)
        o_ref[...] = (u * r * g_ref[...]).astype(out_dtype)

    row = pl.BlockSpec((bs, w), lambda i: (i, 0))
    vec = pl.BlockSpec((1, w), lambda i: (0, 0))
    ins = [x] + ([] if gate is None else [gate]) + [g.reshape(1, w)]
    specs = [row] + ([] if gate is None else [row]) + [vec]
    return _call(body, name=name, out_shape=jax.ShapeDtypeStruct((s, w), out_dtype), grid=(s // bs,),
                 in_specs=specs, out_specs=row, sem=("parallel",))(*ins)


def rms_bwd(x, g, dy, *, name, gate=None, add=None):
    s, w = x.shape
    bs = _rows(s, w)

    def body(*refs):
        refs = list(refs)
        x_ref = refs.pop(0)
        z_ref = refs.pop(0) if gate is not None else None
        g_ref = refs.pop(0)
        dy_ref = refs.pop(0)
        add_ref = refs.pop(0) if add is not None else None
        dx_ref = refs.pop(0)
        dz_ref = refs.pop(0) if gate is not None else None
        dg_ref = refs.pop(0)
        i = pl.program_id(0)

        @pl.when(i == 0)
        def _():
            dg_ref[...] = jnp.zeros_like(dg_ref)

        xv = x_ref[...]
        if gate is not None:
            z = z_ref[...]
            sg = _sigmoid(z)
            act = z * sg
            u = xv * act
        else:
            u = xv
        r = lax.rsqrt(jnp.mean(u * u, axis=1, keepdims=True) + EPS)
        dy_v = dy_ref[...]
        dyg = dy_v * g_ref[...]
        du = r * dyg - u * (r * r * r * jnp.mean(dyg * u, axis=1, keepdims=True))
        dg_ref[...] += jnp.sum(dy_v * u * r, axis=0, keepdims=True)
        if gate is not None:
            dx = du * act
            dz_ref[...] = du * xv * (sg * (1.0 + z * (1.0 - sg)))
        else:
            dx = du
        if add is not None:
            dx = dx + add_ref[...]
        dx_ref[...] = dx

    row = pl.BlockSpec((bs, w), lambda i: (i, 0))
    vec = pl.BlockSpec((1, w), lambda i: (0, 0))
    ins = [x] + ([] if gate is None else [gate]) + [g.reshape(1, w), dy] + ([] if add is None else [add])
    specs = [row] + ([] if gate is None else [row]) + [vec, row] + ([] if add is None else [row])
    outs = [jax.ShapeDtypeStruct((s, w), F32)] + ([] if gate is None else [jax.ShapeDtypeStruct((s, w), F32)])
    outs.append(jax.ShapeDtypeStruct((1, w), F32))
    ospecs = [row] + ([] if gate is None else [row]) + [vec]
    return _call(body, name=name, out_shape=outs, grid=(s // bs,), in_specs=specs, out_specs=ospecs,
                 sem=("arbitrary",))(*ins)


def loss_head(y, target, *, name):
    s, w = y.shape
    bs = _rows(s, w)
    nb = s // bs

    def body(y_ref, t_ref, dy_ref, loss_ref, acc):
        i = pl.program_id(0)

        @pl.when(i == 0)
        def _():
            acc[...] = jnp.zeros_like(acc)

        e = y_ref[...] - t_ref[...]
        dy_ref[...] = e * (1.0 / w)
        acc[...] += jnp.sum(e * e, axis=0, keepdims=True)

        @pl.when(i == nb - 1)
        def _():
            loss_ref[...] = jnp.sum(acc[...], axis=1, keepdims=True) * (0.5 / w)

    row = pl.BlockSpec((bs, w), lambda i: (i, 0))
    return _call(body, name=name, out_shape=[jax.ShapeDtypeStruct((s, w), F32), jax.ShapeDtypeStruct((1, 1), F32)],
                 grid=(nb,), in_specs=[row, row], out_specs=[row, pl.BlockSpec((1, 1), lambda i: (0, 0))],
                 scratch=[pltpu.VMEM((1, w), F32)], sem=("arbitrary",))(y, target)


def _rope_tables(positions, s):
    inv_freq = 1.0 / (ROPE_THETA ** (jnp.arange(0, MLA_ROPE, 2, dtype=F32) / MLA_ROPE))
    ang = positions.reshape(s, 1).astype(F32) * inv_freq
    cos, sin = jnp.cos(ang), jnp.sin(ang)
    one, zero = jnp.ones((s, MLA_NOPE), F32), jnp.zeros((s, MLA_NOPE), F32)
    cq = jnp.tile(jnp.concatenate([one, cos, cos], axis=1), (1, MLA_HEADS))
    sq = jnp.tile(jnp.concatenate([zero, sin, sin], axis=1), (1, MLA_HEADS))
    pad1, pad0 = jnp.ones((s, LANES - MLA_ROPE), F32), jnp.zeros((s, LANES - MLA_ROPE), F32)
    ct = jnp.concatenate([cos, cos, pad1], axis=1)
    st = jnp.concatenate([sin, sin, pad0], axis=1)
    half = MLA_ROPE // 2

    def swap(width, starts):
        r = np.zeros((width, width), np.float32)
        for o in starts:
            for i in range(half):
                r[o + half + i, o + i] = -1.0
                r[o + i, o + half + i] = 1.0
        return jnp.asarray(r)

    rq = swap(MLA_HEADS * MLA_QK, [h * MLA_QK + MLA_NOPE for h in range(MLA_HEADS)])
    rt = swap(LANES, [0])
    return (cq, sq, rq), (ct, st, rt)


def rope(x, tabs, *, name, backward=False, add=None):
    cos, sin, rot = tabs
    n, s, w = x.shape
    bs = _rows(s, w)

    def body(*refs):
        if add is None:
            x_ref, c_ref, s_ref, r_ref, o_ref = refs
        else:
            x_ref, c_ref, s_ref, r_ref, a_ref, o_ref = refs
        xv = x_ref[0]
        for j in range(1, n):
            xv = xv + x_ref[j]
        if backward:
            out = xv * c_ref[...] + _nt(xv * s_ref[...], r_ref[...], HIGHEST)
        else:
            out = xv * c_ref[...] + _nn(xv, r_ref[...], HIGHEST) * s_ref[...]
        if add is not None:
            out = out + a_ref[...]
        o_ref[...] = out

    row = pl.BlockSpec((bs, w), lambda i: (i, 0))
    ins = [x, cos, sin, rot] + ([] if add is None else [add])
    specs = [pl.BlockSpec((n, bs, w), lambda i: (0, i, 0)), row, row, pl.BlockSpec((w, w), lambda i: (0, 0))]
    specs += [] if add is None else [row]
    return _call(body, name=name, out_shape=jax.ShapeDtypeStruct((s, w), F32), grid=(s // bs,), in_specs=specs,
                 out_specs=row, sem=("parallel",))(*ins)


MESH = pl.DeviceIdType.MESH
HBM = pl.BlockSpec(memory_space=pltpu.HBM)


def _flip(v, bit):
    return 1 - v if bit else v


class Carried:
    def __init__(self, kind, arrays):
        assert kind in ('gather', 'scatter')
        self.kind, self.arrays, self.n = kind, list(arrays), len(arrays)

    @property
    def out_shape(self):
        lead = (N_DEV,) if self.kind == 'gather' else ()
        return [jax.ShapeDtypeStruct(lead + a.shape, a.dtype) for a in self.arrays]

    @property
    def scratch(self):
        return [pltpu.SemaphoreType.DMA((self.n, N_DEV - 1)), pltpu.SemaphoreType.DMA((self.n, N_DEV - 1)),
                pltpu.SemaphoreType.DMA((self.n,))]

    def _copies(self, in_refs, out_refs, sems):
        send_sems, recv_sems, local_sems = sems
        x, y, c = lax.axis_index("x"), lax.axis_index("y"), lax.axis_index("c")
        me = 4 * x + 2 * y + c
        part = (lambda b, p: in_refs[b]) if self.kind == 'gather' else (lambda b, p: in_refs[b].at[p])
        local = [pltpu.make_async_copy(part(b, me), out_refs[b].at[me], local_sems.at[b]) for b in range(self.n)]
        remote = []
        for k in range(1, N_DEV):
            px, py, pc = _flip(x, k & 4), _flip(y, k & 2), _flip(c, k & 1)
            for b in range(self.n):
                remote.append(pltpu.make_async_remote_copy(
                    src_ref=part(b, 4 * px + 2 * py + pc), dst_ref=out_refs[b].at[me], send_sem=send_sems.at[b, k - 1],
                    recv_sem=recv_sems.at[b, k - 1], device_id=(px, py, pc), device_id_type=MESH))
        return local, remote

    def start(self, in_refs, out_refs, sems):
        local, remote = self._copies(in_refs, out_refs, sems)
        for cp in local + remote:
            cp.start()

    def wait(self, in_refs, out_refs, sems):
        local, remote = self._copies(in_refs, out_refs, sems)
        for cp in remote:
            cp.wait_send()
            cp.wait_recv()
        for cp in local:
            cp.wait()


def _ride(carried, refs, n_in, n_out):
    n = 0 if carried is None else carried.n
    own_in, ride_in = refs[:n_in], refs[n_in:n_in + n]
    own_out, ride_out = refs[n_in + n:n_in + n + n_out], refs[n_in + n + n_out:n_in + 2 * n + n_out]
    return own_in, own_out, (ride_in, ride_out, refs[n_in + 2 * n + n_out:])


def _tri(n, op):
    r = lax.broadcasted_iota(jnp.int32, (n, n), 0)
    c = lax.broadcasted_iota(jnp.int32, (n, n), 1)
    return r, c, op(r, c)


def _pair_split(x, first):
    zero = jnp.zeros_like(x)
    return jnp.where(first, x, zero), jnp.where(first, zero, x)


def _sb_specs(s, blk):
    npair = SB_WIDTH // LANES
    q = pl.BlockSpec((blk, LANES), lambda j, i: (i, j))
    k = pl.BlockSpec((s, LANES), lambda j, i: (0, npair + j))
    v = pl.BlockSpec((s, LANES), lambda j, i: (0, 2 * npair + j))
    full = pl.BlockSpec((s, LANES), lambda j, i: (0, j))
    return npair, q, k, v, full


def _stack_heads(x, first):
    return jnp.concatenate(_pair_split(x, first), axis=0)


def _unstack_heads(x, first, blk):
    return jnp.where(first, x[:blk], x[blk:])


def sb_fwd(qkv, *, name, carried=None):
    s = qkv.shape[0]
    blk = _tile(s, (ATT_BLK,))
    scale = SB_DIM ** -0.5
    npair, nq = SB_WIDTH // LANES, s // blk
    assert nq % SB_WIDE == 0

    def body(*refs):
        (q_ref, k_ref, v_ref), (y_ref, t_ref), ride = _ride(carried, refs, 3, 2)
        pair, qi = pl.program_id(0), pl.program_id(1)
        if carried is not None:
            @pl.when((pair == 0) & (qi == 0))
            def _():
                carried.start(*ride)

        first = lax.broadcasted_iota(jnp.int32, (blk, LANES), 1) < SB_DIM
        q2 = _stack_heads((q_ref[...].astype(F32) * scale).astype(MXU_DTYPE), first)
        row, col, later_mask = _tri(blk, lambda r, c: r > c)
        u_later = later_mask.astype(MXU_DTYPE)
        n = SB_WIDE

        def tile(kb, carry, masked):
            c, acc = carry
            keys = pl.ds(pl.multiple_of(kb * blk, blk), n * blk)
            z = _nt(q2, k_ref[keys, :])
            sp = _softplus_att(z)
            if masked:
                valid = _causal_mask(blk, n, qi, kb, 2, True)
            spm = jnp.where(valid, sp, 0.0) if masked else sp
            later = _cum(spm, u_later)
            sums = [jnp.sum(spm[:, t * blk:(t + 1) * blk], axis=1, keepdims=True) for t in range(n)]
            after, cols = c, [None] * n
            for t in reversed(range(n)):
                cols[t] = jnp.broadcast_to(after, (2 * blk, blk))
                after = after - sums[t]
            w = jnp.exp((z - sp) - later + (cols[0] if n == 1 else jnp.concatenate(cols, axis=1)))
            if masked:
                w = jnp.where(valid, w, 0.0)
            return after, acc + _nn(w.astype(MXU_DTYPE), v_ref[keys, :])

        zero = (jnp.zeros((2 * blk, 1), F32), jnp.zeros((2 * blk, LANES), F32))
        c, acc = _causal_loop(qi, tile, zero, True, SB_WIDE)
        y_ref[...] = _unstack_heads(acc, first, blk)
        t_ref[...] = _unstack_heads(c, first, blk)
        if carried is not None:
            @pl.when((pair == npair - 1) & (qi == nq - 1))
            def _():
                carried.wait(*ride)

    _, qspec, kspec, vspec, _ = _sb_specs(s, blk)
    out = jax.ShapeDtypeStruct((s, SB_WIDTH), F32)
    extra = [] if carried is None else carried.arrays
    return _call(body, name=name, out_shape=[out, out] + ([] if carried is None else carried.out_shape),
                 grid=(npair, nq), in_specs=[qspec, kspec, vspec] + [HBM] * len(extra),
                 out_specs=[qspec, qspec] + [HBM] * len(extra), scratch=[] if carried is None else carried.scratch,
                 sem=("arbitrary", "arbitrary"))(qkv, qkv, qkv, *extra)


def sb_bwd(qkv, dy, tot, *, name, carried=None):
    s = qkv.shape[0]
    blk = _tile(s, (ATT_BLK,))
    scale = SB_DIM ** -0.5
    npair, nq = SB_WIDTH // LANES, s // blk
    assert nq % SB_WIDE == 0

    def body(*refs):
        (q_ref, k_ref, v_ref, dy_ref, t_ref), (dq_ref, dk_ref, dv_ref), ride = _ride(carried, refs, 5, 3)
        pair, qi = pl.program_id(0), pl.program_id(1)
        if carried is not None:
            @pl.when((pair == 0) & (qi == 0))
            def _():
                carried.start(*ride)

        @pl.when(qi == 0)
        def _():
            dk_ref[...] = jnp.zeros_like(dk_ref)
            dv_ref[...] = jnp.zeros_like(dv_ref)

        first = lax.broadcasted_iota(jnp.int32, (blk, LANES), 1) < SB_DIM
        q2 = _stack_heads((q_ref[...].astype(F32) * scale).astype(MXU_DTYPE), first)
        dy2 = _stack_heads(dy_ref[...].astype(MXU_DTYPE), first)
        tv = jnp.concatenate([t_ref[:, 0:1], t_ref[:, SB_DIM:SB_DIM + 1]], axis=0)
        row, col, incl_mask = _tri(blk, lambda r, c: r <= c)
        u_incl = incl_mask.astype(MXU_DTYPE)
        u_excl = (row < col).astype(MXU_DTYPE)
        n = SB_WIDE

        def prefixed(x, carry):
            cols = []
            for t in range(n):
                cols.append(jnp.broadcast_to(carry, (2 * blk, blk)))
                carry = carry + jnp.sum(x[:, t * blk:(t + 1) * blk], axis=1, keepdims=True)
            return (cols[0] if n == 1 else jnp.concatenate(cols, axis=1)), carry

        def tile(kb, carry, masked):
            p, gc, dq = carry
            keys = pl.ds(pl.multiple_of(kb * blk, blk), n * blk)
            kv = k_ref[keys, :]
            z = _nt(q2, kv)
            dw = _nt(dy2, v_ref[keys, :])
            sp = _softplus_att(z)
            if masked:
                valid = _causal_mask(blk, n, qi, kb, 2, True)
            spm = jnp.where(valid, sp, 0.0) if masked else sp
            before, p = prefixed(spm, p)
            w = jnp.exp((z - sp) + (_cum(spm, u_incl) + before))
            if masked:
                w = jnp.where(valid, w, 0.0)
            g = w * dw
            gbefore, gc = prefixed(g, gc)
            gb = g.astype(MXU_DTYPE)
            gin = _nn(jnp.concatenate([gb[:, t * blk:(t + 1) * blk] for t in range(n)], axis=0), u_excl)
            gex = gbefore + jnp.concatenate([gin[t * 2 * blk:(t + 1) * 2 * blk] for t in range(n)], axis=1)
            keep = jnp.exp(-spm)
            dz = keep * (g + gex) - gex
            if masked:
                dz = jnp.where(valid, dz, 0.0)
            dzb = dz.astype(MXU_DTYPE)
            dk_ref[keys, :] += _tn(dzb, q2)
            dv_ref[keys, :] += _tn(w.astype(MXU_DTYPE), dy2)
            return p, gc, dq + _nn(dzb, kv)

        zero = jnp.zeros((2 * blk, 1), F32)
        _, _, dq = _causal_loop(qi, tile, (tv, zero, jnp.zeros((2 * blk, LANES), F32)), False, SB_WIDE)
        dq_ref[...] = _unstack_heads(dq, first, blk) * scale
        if carried is not None:
            @pl.when((pair == npair - 1) & (qi == nq - 1))
            def _():
                carried.wait(*ride)

    _, qspec, kspec, vspec, full = _sb_specs(s, blk)
    out = jax.ShapeDtypeStruct((s, SB_WIDTH), F32)
    extra = [] if carried is None else carried.arrays
    return _call(body, name=name, out_shape=[out, out, out] + ([] if carried is None else carried.out_shape),
                 grid=(npair, nq), in_specs=[qspec, kspec, vspec, qspec, qspec] + [HBM] * len(extra),
                 out_specs=[qspec, full, full] + [HBM] * len(extra), scratch=[] if carried is None else carried.scratch,
                 sem=("arbitrary", "arbitrary"))(qkv, qkv, qkv, dy, tot, *extra)


ATT_PAIR = 2


def _mla_specs(s, blk, dk, dv):
    q = pl.BlockSpec((ATT_PAIR, blk, dk), lambda hp, i: (hp, i, 0))
    k = pl.BlockSpec((ATT_PAIR, s, dk), lambda hp, i: (hp, 0, 0))
    v = pl.BlockSpec((ATT_PAIR, s, dv), lambda hp, i: (hp, 0, 0))
    y = pl.BlockSpec((ATT_PAIR, blk, dv), lambda hp, i: (hp, i, 0))
    lse = pl.BlockSpec((ATT_PAIR, blk, LANES), lambda hp, i: (hp, i, 0))
    return q, k, v, y, lse


def mla_fwd(q, k, v, *, name):
    h, s, dk = q.shape
    dv = v.shape[-1]
    blk = _tile(s, (ATT_BLK,))
    scale = dk ** -0.5
    assert (s // blk) % MLA_WIDE == 0

    def body(q_ref, k_ref, v_ref, y_ref, l_ref):
        qi = pl.program_id(1)
        n = MLA_WIDE

        def tile(kb, carry, masked):
            keys = pl.ds(pl.multiple_of(kb * blk, blk), n * blk)
            out = []
            for hh in range(ATT_PAIR):
                m, l, acc = carry[hh]
                sc = _nt(q_ref[hh], k_ref[hh, keys, :]) * scale
                if masked:
                    sc = jnp.where(_causal_mask(blk, n, qi, kb, 1, False), sc, -1e30)
                m2 = jnp.maximum(m, jnp.max(sc, axis=1, keepdims=True))
                p = jnp.exp(sc - m2)
                a = jnp.exp(m - m2)
                out.append((m2, a * l + jnp.sum(p, axis=1, keepdims=True),
                            a * acc + _nn(p.astype(MXU_DTYPE), v_ref[hh, keys, :])))
            return tuple(out)

        init = (jnp.full((blk, 1), -1e30, F32), jnp.zeros((blk, 1), F32), jnp.zeros((blk, dv), F32))
        for hh, (m, l, acc) in enumerate(_causal_loop(qi, tile, (init,) * ATT_PAIR, False, MLA_WIDE)):
            y_ref[hh] = acc / l
            l_ref[hh] = jnp.broadcast_to(m + jnp.log(l), (blk, LANES))

    qspec, kspec, vspec, yspec, lspec = _mla_specs(s, blk, dk, dv)
    return _call(body, name=name,
                 out_shape=[jax.ShapeDtypeStruct((h, s, dv), F32), jax.ShapeDtypeStruct((h, s, LANES), F32)],
                 grid=(h // ATT_PAIR, s // blk), in_specs=[qspec, kspec, vspec], out_specs=[yspec, lspec],
                 sem=("parallel", "arbitrary"))(q, k, v)


def mla_bwd(q, k, v, y, dy, lse, *, name):
    h, s, dk = q.shape
    dv = v.shape[-1]
    blk = _tile(s, (ATT_BLK,))
    scale = dk ** -0.5
    assert (s // blk) % MLA_WIDE == 0

    def body(q_ref, k_ref, v_ref, y_ref, dy_ref, l_ref, dq_ref, dk_ref, dv_ref):
        qi = pl.program_id(1)

        @pl.when(qi == 0)
        def _():
            dk_ref[...] = jnp.zeros_like(dk_ref)
            dv_ref[...] = jnp.zeros_like(dv_ref)

        as_row = lambda col: jnp.transpose(jnp.broadcast_to(col, (blk, LANES)))[0:1, :]
        dyv = [dy_ref[hh].astype(MXU_DTYPE) for hh in range(ATT_PAIR)]
        delta = [as_row(jnp.sum(dy_ref[hh] * y_ref[hh], axis=1, keepdims=True)) for hh in range(ATT_PAIR)]
        lv = [as_row(l_ref[hh, :, 0:1]) for hh in range(ATT_PAIR)]
        n = MLA_WIDE

        def tile(kb, dqs, masked):
            keys = pl.ds(pl.multiple_of(kb * blk, blk), n * blk)
            out = []
            for hh in range(ATT_PAIR):
                qv = q_ref[hh]
                kv = k_ref[hh, keys, :]
                p = jnp.exp(_nt(kv, qv) * scale - lv[hh])
                if masked:
                    p = jnp.where(_causal_mask(blk, n, qi, kb, 1, False, keys_on_rows=True), p, 0.0)
                ds = (p * (_nt(v_ref[hh, keys, :], dyv[hh]) - delta[hh])).astype(MXU_DTYPE)
                dk_ref[hh, keys, :] += _nn(ds, qv) * scale
                dv_ref[hh, keys, :] += _nn(p.astype(MXU_DTYPE), dyv[hh])
                out.append(dqs[hh] + _tn(ds, kv))
            return tuple(out)

        for hh, dq in enumerate(_causal_loop(qi, tile, (jnp.zeros((blk, dk), F32),) * ATT_PAIR, False, MLA_WIDE)):
            dq_ref[hh] = dq * scale

    qspec, kspec, vspec, yspec, lspec = _mla_specs(s, blk, dk, dv)
    return _call(body, name=name,
                 out_shape=[jax.ShapeDtypeStruct((h, s, dk), F32), jax.ShapeDtypeStruct((h, s, dk), F32),
                            jax.ShapeDtypeStruct((h, s, dv), F32)],
                 grid=(h // ATT_PAIR, s // blk), in_specs=[qspec, kspec, vspec, yspec, yspec, lspec],
                 out_specs=[qspec, kspec, vspec], sem=("parallel", "arbitrary"))(q, k, v, y, dy, lse)


HALO = 8
CONV_CHUNK = 16


def _conv_tiles(x):
    s, c = x.shape[-2:]
    return s, c, _tile(s, (ROW_BLK,)), _tile(c, (CONV_COLS,))


def _halo_rows(dtype):
    return HALO * 4 // jnp.dtype(dtype).itemsize


def _conv_specs(bs, cw, lead=(), dtype=F32):
    zero = (0,) * len(lead)
    hr = _halo_rows(dtype)
    blk = pl.BlockSpec(lead + (None, bs, cw), lambda p, j, i: zero + (p, i, j))
    halo = pl.BlockSpec(lead + (None, hr, cw), lambda p, j, i: zero + (p, jnp.maximum(i * (bs // hr) - 1, 0), j))
    w = lambda kk: pl.BlockSpec(lead + (None, kk, cw), lambda p, j, i: zero + (p, 0, j))
    return blk, halo, w


def _stage(scr, x_ref, halo_ref, first):
    hr = halo_ref.shape[0]
    scr[0:HALO, :] = jnp.where(first, 0.0, halo_ref[hr - HALO:hr, :].astype(F32))
    scr[HALO:, :] = x_ref[...].astype(F32)


def _shifted(ext, shift):
    return ext[HALO:] if shift == 0 else pltpu.roll(ext, shift, 0)[HALO:]


def _conv_taps(scr, kk, r0):
    ext = scr[pl.ds(r0, CONV_CHUNK + HALO), :]
    return [_shifted(ext, kk - 1 - k) for k in range(kk)]


def _conv_sum(taps, w_ref, b_ref):
    u = b_ref[...] + taps[0] * w_ref[0:1, :]
    for k in range(1, len(taps)):
        u = u + taps[k] * w_ref[k:k + 1, :]
    return u


def _fold(x):
    out = x[0:8]
    for r in range(8, CONV_CHUNK, 8):
        out = out + x[r:r + 8]
    return out


class _TapSums:
    def __init__(self, kk, cw):
        self.w = [jnp.zeros((8, cw), F32) for _ in range(kk)]
        self.b = jnp.zeros((8, cw), F32)

    def add(self, du, taps):
        self.w = [a + _fold(du * t) for a, t in zip(self.w, taps)]
        self.b = self.b + _fold(du)

    def flush(self, dw_ref, db_ref):
        for k, a in enumerate(self.w):
            dw_ref[k:k + 1, :] += jnp.sum(a, axis=0, keepdims=True)
        db_ref[...] += jnp.sum(self.b, axis=0, keepdims=True)


def _silu_grad(u):
    sg = _sigmoid(u)
    return sg * (1.0 + u * (1.0 - sg))


def conv_silu_fwd(x, w, b, *, name):
    s, c, bs, cw = _conv_tiles(x)
    kk = w.shape[1]

    def body(x_ref, h_ref, w_ref, b_ref, o_ref, scr):
        _stage(scr, x_ref, h_ref, pl.program_id(2) == 0)
        for r0 in range(0, bs, CONV_CHUNK):
            u = _conv_sum(_conv_taps(scr, kk, r0), w_ref, b_ref)
            o_ref[pl.ds(r0, CONV_CHUNK), :] = u * _sigmoid(u)

    blk, halo, wspec = _conv_specs(bs, cw, dtype=x.dtype)
    return _call(body, name=name, out_shape=jax.ShapeDtypeStruct(x.shape, F32), grid=(x.shape[0], c // cw, s // bs),
                 in_specs=[blk, halo, wspec(kk), wspec(1)], out_specs=blk, scratch=[pltpu.VMEM((bs + HALO, cw), F32)],
                 sem=("parallel", "parallel", "arbitrary"))(x, x, w, b)


def conv_silu_bwd(x, dy, w, b, *, name):
    s, c, bs, cw = _conv_tiles(x)
    kk = w.shape[1]

    def body(x_ref, h_ref, w_ref, b_ref, dy_ref, du_ref, dw_ref, db_ref, scr):
        i = pl.program_id(2)

        @pl.when(i == 0)
        def _():
            dw_ref[...] = jnp.zeros_like(dw_ref)
            db_ref[...] = jnp.zeros_like(db_ref)

        _stage(scr, x_ref, h_ref, i == 0)
        sums = _TapSums(kk, cw)
        for r0 in range(0, bs, CONV_CHUNK):
            taps = _conv_taps(scr, kk, r0)
            du = dy_ref[pl.ds(r0, CONV_CHUNK), :] * _silu_grad(_conv_sum(taps, w_ref, b_ref))
            du_ref[pl.ds(r0, CONV_CHUNK), :] = du
            sums.add(du, taps)
        sums.flush(dw_ref, db_ref)

    blk, halo, wspec = _conv_specs(bs, cw, dtype=x.dtype)
    return _call(body, name=name,
                 out_shape=[jax.ShapeDtypeStruct(x.shape, F32), jax.ShapeDtypeStruct(w.shape, F32),
                            jax.ShapeDtypeStruct(b.shape, F32)],
                 grid=(x.shape[0], c // cw, s // bs), in_specs=[blk, halo, wspec(kk), wspec(1), blk],
                 out_specs=[blk, wspec(kk), wspec(1)], scratch=[pltpu.VMEM((bs + HALO, cw), F32)],
                 sem=("parallel", "parallel", "arbitrary"))(x, x, w, b, dy)


def _glu_view(a):
    return a.reshape((2, a.shape[0] // 2) + a.shape[1:])


def conv_glu_fwd(x, w, b, *, name):
    s, c, bs, cw = _conv_tiles(x)
    kk = w.shape[1]
    half = x.shape[0] // 2

    def body(x_ref, h_ref, w_ref, b_ref, o_ref, gscr, vscr):
        first = pl.program_id(2) == 0
        _stage(gscr, x_ref.at[0], h_ref.at[0], first)
        _stage(vscr, x_ref.at[1], h_ref.at[1], first)
        for r0 in range(0, bs, CONV_CHUNK):
            gate = _conv_sum(_conv_taps(gscr, kk, r0), w_ref.at[0], b_ref.at[0])
            val = _conv_sum(_conv_taps(vscr, kk, r0), w_ref.at[1], b_ref.at[1])
            o_ref[pl.ds(r0, CONV_CHUNK), :] = (gate * _sigmoid(gate) * val).astype(o_ref.dtype)

    blk, halo, wspec = _conv_specs(bs, cw, lead=(2,), dtype=x.dtype)
    out, _, _ = _conv_specs(bs, cw)
    xv = _glu_view(x)
    return _call(body, name=name, out_shape=jax.ShapeDtypeStruct((half, s, c), MXU_DTYPE), grid=(half, c // cw, s // bs),
                 in_specs=[blk, halo, wspec(kk), wspec(1)], out_specs=out, scratch=[pltpu.VMEM((bs + HALO, cw), F32)] * 2,
                 sem=("parallel", "parallel", "arbitrary"))(xv, xv, _glu_view(w), _glu_view(b))


def conv_glu_bwd(x, da, w, b, *, name):
    s, c, bs, cw = _conv_tiles(x)
    kk = w.shape[1]
    half = x.shape[0] // 2

    def body(x_ref, h_ref, w_ref, b_ref, da_ref, du_ref, dw_ref, db_ref, gscr, vscr):
        i = pl.program_id(2)

        @pl.when(i == 0)
        def _():
            dw_ref[...] = jnp.zeros_like(dw_ref)
            db_ref[...] = jnp.zeros_like(db_ref)

        _stage(gscr, x_ref.at[0], h_ref.at[0], i == 0)
        _stage(vscr, x_ref.at[1], h_ref.at[1], i == 0)
        gsums, vsums = _TapSums(kk, cw), _TapSums(kk, cw)
        for r0 in range(0, bs, CONV_CHUNK):
            gtaps, vtaps = _conv_taps(gscr, kk, r0), _conv_taps(vscr, kk, r0)
            gate = _conv_sum(gtaps, w_ref.at[0], b_ref.at[0])
            val = _conv_sum(vtaps, w_ref.at[1], b_ref.at[1])
            dav = da_ref[pl.ds(r0, CONV_CHUNK), :]
            dgate = dav * val * _silu_grad(gate)
            dval = dav * gate * _sigmoid(gate)
            du_ref[0, pl.ds(r0, CONV_CHUNK), :] = dgate.astype(du_ref.dtype)
            du_ref[1, pl.ds(r0, CONV_CHUNK), :] = dval.astype(du_ref.dtype)
            gsums.add(dgate, gtaps)
            vsums.add(dval, vtaps)
        gsums.flush(dw_ref.at[0], db_ref.at[0])
        vsums.flush(dw_ref.at[1], db_ref.at[1])

    blk, halo, wspec = _conv_specs(bs, cw, lead=(2,), dtype=x.dtype)
    daspec, _, _ = _conv_specs(bs, cw)
    xv, wv, bv = _glu_view(x), _glu_view(w), _glu_view(b)
    du, dw, db = _call(body, name=name,
                       out_shape=[jax.ShapeDtypeStruct(xv.shape, x.dtype), jax.ShapeDtypeStruct(wv.shape, F32),
                                  jax.ShapeDtypeStruct(bv.shape, F32)],
                       grid=(half, c // cw, s // bs), in_specs=[blk, halo, wspec(kk), wspec(1), daspec],
                       out_specs=[blk, wspec(kk), wspec(1)], scratch=[pltpu.VMEM((bs + HALO, cw), F32)] * 2,
                       sem=("parallel", "parallel", "arbitrary"))(xv, xv, wv, bv, da)
    return du.reshape(x.shape), dw.reshape(w.shape), db.reshape(b.shape)


def conv_t(du, w, *, name, out_dtype=F32):
    s, c, bs, cw = _conv_tiles(du)
    kk = w.shape[1]
    nb = s // bs

    def body(d_ref, h_ref, w_ref, o_ref, scr):
        last = pl.program_id(2) == nb - 1
        scr[0:bs, :] = d_ref[...].astype(F32)
        scr[bs:, :] = jnp.where(last, 0.0, h_ref[0:HALO, :].astype(F32))
        for r0 in range(0, bs, CONV_CHUNK):
            ext = scr[pl.ds(r0, CONV_CHUNK + HALO), :]
            ahead = lambda j: ext[:CONV_CHUNK] if j == 0 else pltpu.roll(ext, CONV_CHUNK + HALO - j, 0)[:CONV_CHUNK]
            acc = ahead(kk - 1) * w_ref[0:1, :]
            for k in range(1, kk):
                acc = acc + ahead(kk - 1 - k) * w_ref[k:k + 1, :]
            o_ref[pl.ds(r0, CONV_CHUNK), :] = acc.astype(out_dtype)

    blk, _, wspec = _conv_specs(bs, cw)
    hr = _halo_rows(du.dtype)
    halo = pl.BlockSpec((None, hr, cw), lambda q, j, i: (q, jnp.minimum((i + 1) * (bs // hr), s // hr - 1), j))
    return _call(body, name=name, out_shape=jax.ShapeDtypeStruct(du.shape, out_dtype), grid=(du.shape[0], c // cw, nb),
                 in_specs=[blk, halo, wspec(kk)], out_specs=blk, scratch=[pltpu.VMEM((bs + HALO, cw), F32)],
                 sem=("parallel", "parallel", "arbitrary"))(du, du, w)


def _ssd_common(xbc_ref, tail_ref, dtrt_ref, bias_ref, biast_ref, alog_ref, alogt_ref):
    L = SSM_CHUNK
    raw = tail_ref[...] + bias_ref[...]
    dt = _softplus(raw)
    dtt = _softplus(dtrt_ref[...] + biast_ref[...])
    a = -jnp.exp(alog_ref[...])
    at = -jnp.exp(alogt_ref[...])
    row, col, lower = _tri(L, lambda r, c: r >= c)
    tril = lower.astype(F32)
    cs = _nn(tril, dt * a, HIGHEST)
    cst = _nt(dtt * at, tril, HIGHEST)
    bm = [xbc_ref[:, SSM_INNER + g * SSM_N: SSM_INNER + (g + 1) * SSM_N] for g in range(SSM_GROUPS)]
    off = SSM_INNER + SSM_GROUPS * SSM_N
    cm = [xbc_ref[:, off + g * SSM_N: off + (g + 1) * SSM_N] for g in range(SSM_GROUPS)]
    cb = [_bnt(cm[g], bm[g]) for g in range(SSM_GROUPS)]
    return raw, dt, a, lower, tril, cs, cst, bm, cm, cb


def _ssd_head(hh, xbc_ref, dt, cs, cst, lower):
    L = SSM_CHUNK
    ln = DT_LANE + hh
    x = xbc_ref[:, hh * SSM_P:(hh + 1) * SSM_P]
    dtc = dt[:, ln:ln + 1]
    csc = cs[:, ln:ln + 1]
    csr = cst[hh:hh + 1, :]
    decay = jnp.exp(jnp.where(lower, csc - csr, -1e30))
    last = cs[L - 1:L, ln:ln + 1]
    return x, dtc, csc, decay, jnp.exp(csc), jnp.exp(last - csc), jnp.exp(last)


def _ssd_inputs(tail, dt_bias, a_log, d_skip):
    H = SSM_HEADS
    lanes = lambda vec: jnp.pad(vec.reshape(1, H), ((0, 0), (DT_LANE, LANES - DT_LANE - H)))
    return (tail, tail[:, DT_LANE:DT_LANE + H].T, lanes(dt_bias), dt_bias.reshape(H, 1), lanes(a_log),
            a_log.reshape(H, 1), lanes(d_skip))


def ssd_fwd(xbc, tail, dt_bias, a_log, d_skip, *, name):
    s = xbc.shape[0]
    L, H, P, N = SSM_CHUNK, SSM_HEADS, SSM_P, SSM_N
    nc = s // L

    def body(xbc_ref, tail_ref, dtrt_ref, bias_ref, biast_ref, alog_ref, alogt_ref, d_ref, y_ref, hp_ref, state):
        @pl.when(pl.program_id(0) == 0)
        def _():
            state[...] = jnp.zeros_like(state)

        raw, dt, a, lower, tril, cs, cst, bm, cm, cb = _ssd_common(
            xbc_ref, tail_ref, dtrt_ref, bias_ref, biast_ref, alog_ref, alogt_ref)
        for hh in range(H):
            g = hh // (H // SSM_GROUPS)
            x, dtc, csc, decay, e, tau, gamma = _ssd_head(hh, xbc_ref, dt, cs, cst, lower)
            xdt = x * dtc
            hprev = state[hh]
            hp_ref[hh] = hprev
            skip = d_ref[:, DT_LANE + hh:DT_LANE + hh + 1]
            y = _bnn(cb[g] * decay, xdt) + _bnn(cm[g], hprev) * e + x * skip
            y_ref[:, hh * P:(hh + 1) * P] = y
            state[hh] = hprev * gamma + _btn(bm[g] * tau, xdt)

    row = lambda w: pl.BlockSpec((L, w), lambda c: (c, 0))
    small = lambda shp: pl.BlockSpec(shp, lambda c: (0, 0))
    return _call(body, name=name,
                 out_shape=[jax.ShapeDtypeStruct((s, SSM_INNER), F32), jax.ShapeDtypeStruct((nc, H, N, P), F32)],
                 grid=(nc,),
                 in_specs=[row(SSM_CONV_DIM), row(LANES), pl.BlockSpec((H, L), lambda c: (0, c)), small((1, LANES)),
                           small((H, 1)), small((1, LANES)), small((H, 1)), small((1, LANES))],
                 out_specs=[row(SSM_INNER), pl.BlockSpec((None, H, N, P), lambda c: (c, 0, 0, 0))],
                 scratch=[pltpu.VMEM((H, N, P), F32)], sem=("arbitrary",))(xbc, *_ssd_inputs(tail, dt_bias, a_log, d_skip))


def ssd_bwd(xbc, tail, dt_bias, a_log, d_skip, hprev_all, dy, *, name):
    s = xbc.shape[0]
    L, H, P, N = SSM_CHUNK, SSM_HEADS, SSM_P, SSM_N
    nc = s // L
    hg = H // SSM_GROUPS

    def body(xbc_ref, tail_ref, dtrt_ref, bias_ref, biast_ref, alog_ref, alogt_ref, d_ref, hp_ref, dy_ref,
             dxbc_ref, ddt_ref, dbias_ref, dalog_ref, dd_ref, dstate):
        @pl.when(pl.program_id(0) == 0)
        def _():
            dstate[...] = jnp.zeros_like(dstate)
            dbias_ref[...] = jnp.zeros_like(dbias_ref)
            dalog_ref[...] = jnp.zeros_like(dalog_ref)
            dd_ref[...] = jnp.zeros_like(dd_ref)

        raw, dt, a, lower, tril, cs, cst, bm, cm, cb = _ssd_common(
            xbc_ref, tail_ref, dtrt_ref, bias_ref, biast_ref, alog_ref, alogt_ref)
        lane = lax.broadcasted_iota(jnp.int32, (L, LANES), 1)
        lane1 = lax.broadcasted_iota(jnp.int32, (1, LANES), 1)
        rowi = lax.broadcasted_iota(jnp.int32, (L, 1), 0)
        slot = lax.broadcasted_iota(jnp.int32, (LANES, L), 0)
        col_sums = jnp.zeros((LANES, L), F32)
        dcs_all = jnp.zeros((L, LANES), F32)
        ddt_x = jnp.zeros((L, LANES), F32)
        dd_row = jnp.zeros((1, LANES), F32)
        dbm = [jnp.zeros((L, N), F32) for _ in range(SSM_GROUPS)]
        dcm = [jnp.zeros((L, N), F32) for _ in range(SSM_GROUPS)]
        dcb = [jnp.zeros((L, L), F32) for _ in range(SSM_GROUPS)]
        for hh in range(H):
            g = hh // hg
            ln = DT_LANE + hh
            x, dtc, csc, decay, e, tau, gamma = _ssd_head(hh, xbc_ref, dt, cs, cst, lower)
            xdt = x * dtc
            hprev = hp_ref[hh]
            dhn = dstate[hh]
            dyh = dy_ref[:, hh * P:(hh + 1) * P]
            m = cb[g] * decay
            dxdt = _btn(m, dyh) + _bnn(bm[g] * tau, dhn)
            dm = jnp.where(lower, _bnt(dyh, xdt), 0.0)
            dcb[g] = dcb[g] + dm * decay
            dseg = dm * m
            dcs = jnp.sum(dseg, axis=1, keepdims=True)
            col_sums = jnp.where(slot == ln, jnp.sum(dseg, axis=0, keepdims=True), col_sums)
            edy = e * dyh
            dcm[g] = dcm[g] + _bnt(edy, hprev)
            dcs = dcs + e * jnp.sum(dyh * _bnn(cm[g], hprev), axis=1, keepdims=True)
            xdh = _bnt(xdt, dhn)
            dbm[g] = dbm[g] + tau * xdh
            dtau_tau = jnp.sum(bm[g] * xdh, axis=1, keepdims=True) * tau
            dlast = jnp.sum(dtau_tau, axis=0, keepdims=True) + gamma * jnp.sum(dhn * hprev, keepdims=True)
            dcs = dcs - dtau_tau + jnp.where(rowi == L - 1, dlast, 0.0)
            dstate[hh] = gamma * dhn + _btn(cm[g], edy)
            dcs_all = jnp.where(lane == ln, dcs, dcs_all)
            ddt_x = jnp.where(lane == ln, jnp.sum(dxdt * x, axis=1, keepdims=True), ddt_x)
            dxbc_ref[:, hh * P:(hh + 1) * P] = dxdt * dtc + d_ref[:, ln:ln + 1] * dyh
            dd_row = jnp.where(lane1 == ln, jnp.sum(dyh * x, keepdims=True), dd_row)
        off = SSM_INNER + SSM_GROUPS * SSM_N
        for g in range(SSM_GROUPS):
            dxbc_ref[:, SSM_INNER + g * N: SSM_INNER + (g + 1) * N] = dbm[g] + _btn(dcb[g], cm[g])
            dxbc_ref[:, off + g * N: off + (g + 1) * N] = dcm[g] + _bnn(dcb[g], bm[g])
        dcs_all = dcs_all - jnp.transpose(col_sums)
        dda = _tn(tril, dcs_all, HIGHEST)
        head_lane = (lane >= DT_LANE) & (lane < DT_LANE + H)
        draw = jnp.where(head_lane, (dda * a + ddt_x) * _sigmoid(raw), 0.0)
        ddt_ref[...] = draw
        dbias_ref[...] += jnp.sum(draw, axis=0, keepdims=True)
        dalog_ref[...] += jnp.sum(jnp.where(head_lane, dda * dt, 0.0), axis=0, keepdims=True) * a
        dd_ref[...] += dd_row

    rev = lambda c: nc - 1 - c
    row = lambda w: pl.BlockSpec((L, w), lambda c: (rev(c), 0))
    small = lambda shp: pl.BlockSpec(shp, lambda c: (0, 0))
    acc = pl.BlockSpec((1, LANES), lambda c: (0, 0))
    vec = jax.ShapeDtypeStruct((1, LANES), F32)
    return _call(body, name=name,
                 out_shape=[jax.ShapeDtypeStruct((s, SSM_CONV_DIM), F32), jax.ShapeDtypeStruct((s, LANES), F32), vec, vec, vec],
                 grid=(nc,),
                 in_specs=[row(SSM_CONV_DIM), row(LANES), pl.BlockSpec((H, L), lambda c: (0, rev(c))), small((1, LANES)),
                           small((H, 1)), small((1, LANES)), small((H, 1)), small((1, LANES)),
                           pl.BlockSpec((None, H, N, P), lambda c: (rev(c), 0, 0, 0)), row(SSM_INNER)],
                 out_specs=[row(SSM_CONV_DIM), row(LANES), acc, acc, acc],
                 scratch=[pltpu.VMEM((H, N, P), F32)], sem=("arbitrary",))(
        xbc, *_ssd_inputs(tail, dt_bias, a_log, d_skip), hprev_all, dy)


def _heads(x2d, n, d):
    s = x2d.shape[0]
    return x2d.reshape(s, n, d).transpose(1, 0, 2)


def _unheads(x3d):
    n, s, d = x3d.shape
    return x3d.transpose(1, 0, 2).reshape(s, n * d)


def layer_fwd(h, p, tabs, li, gather_late=None):
    s = h.shape[0]
    tabq, tabt = tabs
    nm = lambda t: f"L{li}_{t}"
    r = {'h': h}
    hn = rms_fwd(h, p['mix_norm'], name=nm('mixnorm'), out_dtype=MXU_DTYPE)
    proj = mm(hn, p['w_in'], name=nm('proj'))
    r.update(hn=hn, proj=proj)
    qkv = proj[:, :3 * SB_WIDTH].astype(MXU_DTYPE)
    late = None
    if gather_late is None:
        ya, tot = sb_fwd(qkv, name=nm('sb_fwd'))
    else:
        ya, tot, *got = sb_fwd(qkv, name=nm('sb_fwd'), carried=Carried('gather', [gather_late[n] for n in LATE]))
        late = assemble_late(dict(zip(LATE, got)))
        p = dict(p, **{n: late[n][li] for n in LATE})
    yan = rms_fwd(ya, p['sb_out_norm'], name=nm('sbnorm'), out_dtype=MXU_DTYPE)
    r.update(qkv=qkv, ya=ya, tot=tot)
    z = proj[:, 768:1280]
    xbc = proj[None, :, 1280:2048]
    tail = proj[:, TAIL:TAIL + LANES]
    xbc_act = conv_silu_fwd(xbc, p['ssm_conv_w'], p['ssm_conv_b'], name=nm('ssmconv'))[0]
    y_ssm, hprev = ssd_fwd(xbc_act, tail, p['ssm_dt_bias'], p['ssm_a_log'], p['ssm_d'], name=nm('ssd_fwd'))
    ybn = rms_fwd(y_ssm, p['ssm_out_norm'], name=nm('ssmnorm'), gate=z, out_dtype=MXU_DTYPE)
    r.update(z=z, xbc=xbc, tail=tail, xbc_act=xbc_act, y_ssm=y_ssm, hprev=hprev)
    cq = proj[:, 2048:2304]
    ckv = proj[:, 2304:2432]
    qn = rms_fwd(cq, p['mla_q_norm'], name=nm('qnorm'), out_dtype=MXU_DTYPE)
    q_r = rope(mm(qn, p['mla_w_uq'], name=nm('uq'))[None], tabq, name=nm('ropeq'))
    kvn = rms_fwd(ckv, p['mla_kv_norm'], name=nm('kvnorm'), out_dtype=MXU_DTYPE)
    kv = mm(kvn, p['mla_w_ukv'], name=nm('ukv'))
    k_pe = rope(tail[None], tabt, name=nm('ropek'))[:, :MLA_ROPE]
    qh = _heads(q_r, MLA_HEADS, MLA_QK).astype(MXU_DTYPE)
    kvh = _heads(kv, MLA_HEADS, MLA_NOPE + MLA_V)
    kh = jnp.concatenate([kvh[..., :MLA_NOPE], jnp.broadcast_to(k_pe[None], (MLA_HEADS, s, MLA_ROPE))],
                         axis=-1).astype(MXU_DTYPE)
    vh = kvh[..., MLA_NOPE:].astype(MXU_DTYPE)
    yc_h, lse = mla_fwd(qh, kh, vh, name=nm('mla_fwd'))
    yc = _unheads(yc_h)
    ycn = rms_fwd(yc, p['mla_out_norm'], name=nm('mlanorm'), out_dtype=MXU_DTYPE)
    r.update(cq=cq, ckv=ckv, qn=qn, kvn=kvn, qh=qh, kh=kh, vh=vh, yc_h=yc_h, yc=yc, lse=lse)
    ycat = jnp.concatenate([yan, ybn, ycn], axis=1)
    h1 = mm(ycat, p['w_out'], name=nm('outproj'), res=h)
    hn2 = rms_fwd(h1, p['ffn_norm'], name=nm('ffnnorm'), out_dtype=MXU_DTYPE)
    up = mm(hn2, p['ffn_w_up'], name=nm('up'), bb='o', out_dtype=MXU_DTYPE)
    act = conv_glu_fwd(up, p['ffn_conv_w'], p['ffn_conv_b'], name=nm('glu'))
    h2 = mm(act, p['ffn_w_down'], name=nm('down'), ab='k', bb='k', res=h1)
    r.update(ycat=ycat, h1=h1, hn2=hn2, up=up, act=act)
    return h2, r, p, late


def layer_bwd(dh2, p, r, tabs, li, scatter_late=None):
    s = dh2.shape[0]
    tabq, tabt = tabs
    nm = lambda t: f"L{li}_{t}"
    g = {}
    dact = mm(dh2, p['ffn_w_down'], name=nm('d_down_x'), tb=True, bb='o')
    g['ffn_w_down'] = mm(r['act'], dh2, name=nm('d_down_w'), out_dtype=WIRE_DTYPE, ta=True, ab='o')
    du, g['ffn_conv_w'], g['ffn_conv_b'] = conv_glu_bwd(r['up'], dact, p['ffn_conv_w'], p['ffn_conv_b'], name=nm('d_glu'))
    dup = conv_t(du, p['ffn_conv_w'], name=nm('d_ffnconv'), out_dtype=MXU_DTYPE)
    g['ffn_w_up'] = mm(r['hn2'], dup, name=nm('d_up_w'), out_dtype=WIRE_DTYPE, ta=True, bb='o')
    dhn2 = mm(dup, p['ffn_w_up'], name=nm('d_up_x'), tb=True, ab='k', bb='k')
    dh1, dg = rms_bwd(r['h1'], p['ffn_norm'], dhn2, name=nm('d_ffnnorm'), add=dh2)
    g['ffn_norm'] = dg[0]
    dycat = mm(dh1, p['w_out'], name=nm('d_out_x'), tb=True)
    g['w_out'] = mm(r['ycat'], dh1, name=nm('d_out_w'), out_dtype=WIRE_DTYPE, ta=True)
    dya, dg = rms_bwd(r['ya'], p['sb_out_norm'], dycat[:, :256], name=nm('d_sbnorm'))
    g['sb_out_norm'] = dg[0]
    recv_late = None
    if scatter_late is None:
        dq, dk, dv = sb_bwd(r['qkv'], dya, r['tot'], name=nm('sb_bwd'))
    else:
        parts = owner_parts_late([g] + list(scatter_late))
        dq, dk, dv, *got = sb_bwd(r['qkv'], dya, r['tot'], name=nm('sb_bwd'),
                                  carried=Carried('scatter', [parts[n].astype(WIRE_DTYPE) for n in LATE]))
        recv_late = dict(zip(LATE, got))
    dyssm, dz, dg = rms_bwd(r['y_ssm'], p['ssm_out_norm'], dycat[:, 256:768], name=nm('d_ssmnorm'), gate=r['z'])
    g['ssm_out_norm'] = dg[0]
    dxbc_act, ddt_tail, dbias, dalog, dd = ssd_bwd(r['xbc_act'], r['tail'], p['ssm_dt_bias'], p['ssm_a_log'],
                                                   p['ssm_d'], r['hprev'], dyssm, name=nm('ssd_bwd'))
    hl = slice(DT_LANE, DT_LANE + SSM_HEADS)
    g['ssm_dt_bias'], g['ssm_a_log'], g['ssm_d'] = dbias[0, hl], dalog[0, hl], dd[0, hl]
    dxbc_u, g['ssm_conv_w'], g['ssm_conv_b'] = conv_silu_bwd(r['xbc'], dxbc_act[None], p['ssm_conv_w'], p['ssm_conv_b'],
                                                             name=nm('d_ssmact'))
    dxbc = conv_t(dxbc_u, p['ssm_conv_w'], name=nm('d_ssmconv'))[0]
    dyc, dg = rms_bwd(r['yc'], p['mla_out_norm'], dycat[:, 768:], name=nm('d_mlanorm'))
    g['mla_out_norm'] = dg[0]
    dqh, dkh, dvh = mla_bwd(r['qh'], r['kh'], r['vh'], r['yc_h'], _heads(dyc, MLA_HEADS, MLA_V), r['lse'], name=nm('mla_bwd'))
    dq_c = rope(_unheads(dqh)[None], tabq, name=nm('d_ropeq'), backward=True)
    g['mla_w_uq'] = mm(r['qn'], dq_c, name=nm('d_uq_w'), out_dtype=WIRE_DTYPE, ta=True)
    dcq, dg = rms_bwd(r['cq'], p['mla_q_norm'], mm(dq_c, p['mla_w_uq'], name=nm('d_uq_x'), tb=True), name=nm('d_qnorm'))
    g['mla_q_norm'] = dg[0]
    dkv = _unheads(jnp.concatenate([dkh[..., :MLA_NOPE], dvh], axis=-1))
    g['mla_w_ukv'] = mm(r['kvn'], dkv, name=nm('d_ukv_w'), out_dtype=WIRE_DTYPE, ta=True)
    dckv, dg = rms_bwd(r['ckv'], p['mla_kv_norm'], mm(dkv, p['mla_w_ukv'], name=nm('d_ukv_x'), tb=True), name=nm('d_kvnorm'))
    g['mla_kv_norm'] = dg[0]
    dkpe = jnp.pad(dkh[..., MLA_NOPE:], ((0, 0), (0, 0), (0, LANES - MLA_ROPE)))
    dtail = rope(dkpe, tabt, name=nm('d_ropek'), backward=True, add=ddt_tail)
    dproj = jnp.concatenate([dq, dk, dv, dz, dxbc, dcq, dckv, dtail], axis=1).astype(MXU_DTYPE)
    g['w_in'] = mm(r['hn'], dproj, name=nm('d_proj_w'), out_dtype=WIRE_DTYPE, ta=True)
    dhn = mm(dproj, p['w_in'], name=nm('d_proj_x'), tb=True)
    dh, dg = rms_bwd(r['h'], p['mix_norm'], dhn, name=nm('d_mixnorm'), add=dh1)
    g['mix_norm'] = dg[0]
    return dh, g, recv_late


def _w_in_placement():
    c = np.arange(D_IN)
    dest = np.where(c < 2048, c, np.where(c < 2056, c + (D_IN - 2056), c - 8))
    dest = jnp.asarray(dest.reshape(N_DEV, D_IN // N_DEV, 1), jnp.int32)
    return (dest == jnp.arange(D_IN_PAD, dtype=jnp.int32)).astype(MXU_DTYPE)


def _owner_major(full, axis):
    shp = full.shape
    return jnp.moveaxis(full.reshape(shp[:axis] + (N_DEV, shp[axis] // N_DEV) + shp[axis + 1:]), axis, 0)


def _owner_join(parts, axis):
    moved = jnp.moveaxis(parts, 0, axis)
    shp = moved.shape
    return moved.reshape(shp[:axis] + (shp[axis] * shp[axis + 1],) + shp[axis + 2:])


def assemble_early(gathered, replicated):
    L = DEPTH
    out = dict(replicated)
    out['w_in'] = mm(gathered['w_in'].reshape(N_DEV, L * D_MODEL, D_IN // N_DEV), _w_in_placement(), name='place_w_in',
                     ab='k', bb='k', out_dtype=MXU_DTYPE).reshape(L, D_MODEL, D_IN_PAD)
    out['mla_w_uq'] = _owner_join(gathered['mla_w_uq'], 2)
    out['mla_w_ukv'] = _owner_join(gathered['mla_w_ukv'], 2)
    out['ssm_conv_w'] = _owner_join(gathered['ssm_conv_w'], 2)[:, None]
    out['ssm_conv_b'] = replicated['ssm_conv_b'].reshape(L, 1, 1, SSM_CONV_DIM)
    out['ffn_conv_b'] = replicated['ffn_conv_b'].reshape(L, N_DEV, 1, FF_SHARD)
    return out


def assemble_late(gathered):
    L = DEPTH
    return {'ffn_w_up': jnp.moveaxis(gathered['ffn_w_up'], 1, 0),
            'w_out': _owner_join(gathered['w_out'], 1),
            'ffn_w_down': _owner_join(gathered['ffn_w_down'], 1).reshape(L, N_DEV // 2, FF_SHARD, D_MODEL),
            'ffn_conv_w': jnp.moveaxis(gathered['ffn_conv_w'], 1, 0)}


def owner_parts_late(grads):
    L = DEPTH
    st = lambda n: jnp.stack([g[n] for g in grads])
    return {'ffn_w_up': jnp.moveaxis(st('ffn_w_up'), 1, 0),
            'w_out': _owner_major(st('w_out'), 1),
            'ffn_w_down': _owner_major(st('ffn_w_down').reshape(L, D_FF, D_MODEL), 1),
            'ffn_conv_w': jnp.moveaxis(st('ffn_conv_w'), 1, 0)}


def owner_parts_early(grads):
    L = DEPTH
    st = lambda n: jnp.stack([g[n] for g in grads])
    parts = {
        'w_in': mm(st('w_in').reshape(L * D_MODEL, D_IN_PAD), _w_in_placement(), name='unplace_w_in', tb=True, bb='o',
                   out_dtype=WIRE_DTYPE).reshape(N_DEV, L, D_MODEL, D_IN // N_DEV),
        'mla_w_uq': _owner_major(st('mla_w_uq'), 2),
        'mla_w_ukv': _owner_major(st('mla_w_ukv'), 2),
        'ssm_conv_w': _owner_major(st('ssm_conv_w')[:, 0], 2),
    }
    rep = {n: st(n) for n in REPLICATED if n not in ('final_norm', 'ssm_conv_b', 'ffn_conv_b')}
    rep['ssm_conv_b'] = st('ssm_conv_b').reshape(L, SSM_CONV_DIM)
    rep['ffn_conv_b'] = st('ffn_conv_b').reshape(L, 2 * D_FF)
    return parts, rep


def local_step(x, positions, target, early, late_shards, replicated):
    s = x.shape[0]
    tabs = _rope_tables(positions, s)
    params = assemble_early(early, replicated)
    layer = lambda li: {n: params[n][li] for n in params if n != 'final_norm'}
    h, r0, p0, late = layer_fwd(x, layer(0), tabs, 0, gather_late=late_shards)
    saved = [(p0, r0)]
    for li in range(1, DEPTH):
        h, r, p, _ = layer_fwd(h, dict(layer(li), **{n: late[n][li] for n in LATE}), tabs, li)
        saved.append((p, r))
    y = rms_fwd(h, params['final_norm'], name='finalnorm')
    dy, loss = loss_head(y, target, name='loss')
    dh, dg = rms_bwd(h, params['final_norm'], dy, name='d_finalnorm')
    above = []
    for li in reversed(range(1, DEPTH)):
        dh, g, _ = layer_bwd(dh, *saved[li], tabs, li)
        above.insert(0, g)
    dh, g0, recv_late = layer_bwd(dh, *saved[0], tabs, 0, scatter_late=above)
    parts, rep = owner_parts_early([g0] + above)
    rep['final_norm'] = dg[0]
    return loss[0, 0], dh, parts, recv_late, rep


def all_gather(blocks, *, name):
    n = len(blocks)

    def body(*refs):
        x_refs, out_refs = refs[:n], refs[n:2 * n]
        send_sems, recv_sems, local_sems = refs[2 * n:]
        x, y, c = lax.axis_index("x"), lax.axis_index("y"), lax.axis_index("c")
        me, sibling = (x, y, c), (x, y, 1 - c)
        chips = [(1 - x, y), (x, 1 - y), (1 - x, 1 - y)]

        def slot(b, px, py, pc):
            return out_refs[b].at[4 * px + 2 * py + pc]

        def copy(b, k, blk, to, src=None):
            return pltpu.make_async_remote_copy(src_ref=slot(b, *blk) if src is None else src, dst_ref=slot(b, *blk),
                                                send_sem=send_sems.at[b, k], recv_sem=recv_sems.at[b, k],
                                                device_id=to, device_id_type=MESH)

        mine = [pltpu.make_async_copy(x_refs[b], slot(b, *me), local_sems.at[b]) for b in range(n)]
        for cp in mine:
            cp.start()
        first = []
        for b in range(n):
            first.append(copy(b, 0, me, sibling, src=x_refs[b]))
            first += [copy(b, 1 + j, me, (*chip, c), src=x_refs[b]) for j, chip in enumerate(chips)]
        for cp in first:
            cp.start()
        passed = []
        for j, chip in enumerate(chips):
            for b in range(n):
                copy(b, 1 + j, (*chip, c), me).wait_recv()
                fwd = copy(b, 4 + j, (*chip, c), sibling)
                fwd.start()
                passed.append(fwd)
        for b in range(n):
            copy(b, 0, sibling, me).wait_recv()
            for j, chip in enumerate(chips):
                copy(b, 4 + j, (*chip, 1 - c), me).wait_recv()
        for cp in first + passed:
            cp.wait_send()
        for cp in mine:
            cp.wait()

    return pl.pallas_call(
        body, name=name, out_shape=[jax.ShapeDtypeStruct((N_DEV,) + b.shape, b.dtype) for b in blocks],
        in_specs=[HBM] * n, out_specs=[HBM] * n,
        scratch_shapes=[pltpu.SemaphoreType.DMA((n, 7)), pltpu.SemaphoreType.DMA((n, 7)), pltpu.SemaphoreType.DMA((n,))],
    )(*blocks)


def all_to_all(parts, *, name):
    n = len(parts)

    def body(*refs):
        g_refs, r_refs = refs[:n], refs[n:2 * n]
        send_sems, recv_sems, local_sems = refs[2 * n:]
        x, y, c = lax.axis_index("x"), lax.axis_index("y"), lax.axis_index("c")
        me = 4 * x + 2 * y + c
        mine = [pltpu.make_async_copy(g_refs[b].at[me], r_refs[b].at[me], local_sems.at[b]) for b in range(n)]
        for cp in mine:
            cp.start()
        copies = []
        for k in range(1, N_DEV):
            px, py, pc = _flip(x, k & 4), _flip(y, k & 2), _flip(c, k & 1)
            peer = 4 * px + 2 * py + pc
            for b in range(n):
                cp = pltpu.make_async_remote_copy(src_ref=g_refs[b].at[peer], dst_ref=r_refs[b].at[me],
                                                  send_sem=send_sems.at[b, k - 1], recv_sem=recv_sems.at[b, k - 1],
                                                  device_id=(px, py, pc), device_id_type=MESH)
                cp.start()
                copies.append(cp)
        for cp in copies:
            cp.wait_send()
            cp.wait_recv()
        for cp in mine:
            cp.wait()

    return pl.pallas_call(
        body, name=name, out_shape=[jax.ShapeDtypeStruct(p.shape, p.dtype) for p in parts],
        in_specs=[HBM] * n, out_specs=[HBM] * n,
        scratch_shapes=[pltpu.SemaphoreType.DMA((n, 7)), pltpu.SemaphoreType.DMA((n, 7)), pltpu.SemaphoreType.DMA((n,))],
    )(*parts)


def adamw(parts, w, m, v, *, name):
    r, wd = w.shape
    br = _tile(r, (256, 128, 64, 32, 16, 8))
    c1 = 1.0 - ADAM_B1 ** ADAM_STEP
    c2 = 1.0 - ADAM_B2 ** ADAM_STEP

    def body(p_ref, w_ref, m_ref, v_ref, g_ref, d_ref, mo_ref, vo_ref):
        g = p_ref[0].astype(F32)
        for j in range(1, N_DEV):
            g = g + p_ref[j].astype(F32)
        mn = ADAM_B1 * m_ref[...] + (1.0 - ADAM_B1) * g
        vn = ADAM_B2 * v_ref[...] + (1.0 - ADAM_B2) * (g * g)
        g_ref[...] = g
        mo_ref[...] = mn
        vo_ref[...] = vn
        d_ref[...] = -ADAM_LR * ((mn / c1) / (jnp.sqrt(vn / c2) + ADAM_EPS) + ADAM_WD * w_ref[...])

    blk = pl.BlockSpec((br, wd), lambda i: (i, 0))
    out = jax.ShapeDtypeStruct((r, wd), F32)
    return _call(body, name=name, out_shape=[out] * 4, grid=(r // br,),
                 in_specs=[pl.BlockSpec((N_DEV, br, wd), lambda i: (0, i, 0)), blk, blk, blk], out_specs=[blk] * 4,
                 sem=("parallel",))(parts, w, m, v)


def _pack(arrs):
    flat = jnp.concatenate([a.reshape(-1) for a in arrs])
    rows = -(-flat.shape[0] // (8 * FLAT_W)) * 8
    return jnp.pad(flat, (0, rows * FLAT_W - flat.shape[0])).reshape(rows, FLAT_W)


def _unpack(flat, shapes):
    flat = flat.reshape(-1)
    out, off = [], 0
    for shp in shapes:
        n = int(np.prod(shp))
        out.append(flat[off:off + n].reshape(shp))
        off += n
    return out


def kernel(x, positions, mix_norm, w_in, sb_out_norm, ssm_conv_w, ssm_conv_b, ssm_dt_bias, ssm_a_log, ssm_d, ssm_out_norm, mla_q_norm, mla_w_uq, mla_kv_norm, mla_w_ukv, mla_out_norm, w_out, ffn_norm, ffn_w_up, ffn_conv_w, ffn_conv_b, ffn_w_down, final_norm, loss_target, m_mix_norm, m_w_in, m_sb_out_norm, m_ssm_conv_w, m_ssm_conv_b, m_ssm_dt_bias, m_ssm_a_log, m_ssm_d, m_ssm_out_norm, m_mla_q_norm, m_mla_w_uq, m_mla_kv_norm, m_mla_w_ukv, m_mla_out_norm, m_w_out, m_ffn_norm, m_ffn_w_up, m_ffn_conv_w, m_ffn_conv_b, m_ffn_w_down, m_final_norm, v_mix_norm, v_w_in, v_sb_out_norm, v_ssm_conv_w, v_ssm_conv_b, v_ssm_dt_bias, v_ssm_a_log, v_ssm_d, v_ssm_out_norm, v_mla_q_norm, v_mla_w_uq, v_mla_kv_norm, v_mla_w_ukv, v_mla_out_norm, v_w_out, v_ffn_norm, v_ffn_w_up, v_ffn_conv_w, v_ffn_conv_b, v_ffn_w_down, v_final_norm):
    args = locals()
    w = {n: args[n] for n in WEIGHTS}
    m = {n: args['m_' + n] for n in WEIGHTS}
    v = {n: args['v_' + n] for n in WEIGHTS}
    wire = lambda n: w[n] if n in VPU_WEIGHTS else w[n].astype(BF16)
    early = dict(zip(EARLY, all_gather([wire(n) for n in EARLY], name='gather_early')))

    loss, dx, parts, recv, rep = local_step(x[0], positions[0], loss_target[0], early, {n: wire(n) for n in LATE},
                                            {n: w[n] for n in REPLICATED})
    loss = lax.psum(loss, ("x", "y", "c"))

    recv.update(zip(EARLY, all_to_all([parts[n].astype(WIRE_DTYPE) for n in EARLY], name='scatter_early')))
    res = {kind: {} for kind in 'gdmv'}
    for n, rv in recv.items():
        shp = w[n].shape
        two_d = (int(np.prod(shp[:-1])), shp[-1])
        outs = adamw(rv.reshape((N_DEV,) + two_d), w[n].reshape(two_d), m[n].reshape(two_d), v[n].reshape(two_d),
                     name='adamw_' + n)
        for kind, o in zip('gdmv', outs):
            res[kind][n] = o.reshape(shp)

    rep_shapes = [w[n].shape for n in REPLICATED]
    (rparts,) = all_gather([_pack([rep[n] for n in REPLICATED])], name='gather_small_grads')
    rflat = lambda d: _pack([d[n] for n in REPLICATED])
    routs = adamw(rparts, rflat(w), rflat(m), rflat(v), name='adamw_replicated')
    for kind, o in zip('gdmv', routs):
        res[kind].update(zip(REPLICATED, _unpack(o, rep_shapes)))

    return (loss, dx[None], *[res['g'][n] for n in WEIGHTS], *[res['d'][n] for n in WEIGHTS],
            *[res['m'][n] for n in WEIGHTS], *[res['v'][n] for n in WEIGHTS])
```

```python
import numpy as np
import jax
import jax.numpy as jnp
from jax import lax
from jax.experimental import pallas as pl
from jax.experimental.pallas import tpu as pltpu

F32 = jnp.float32
BF16 = jnp.bfloat16
MXU_DTYPE = jnp.bfloat16
HIGHEST = lax.Precision.HIGHEST
WIRE_DTYPE = jnp.bfloat16

N_DEV = 8
D_MODEL = 1024
DEPTH = 2
EPS = 1e-6
SB_HEADS, SB_DIM = 4, 64
SB_WIDTH = SB_HEADS * SB_DIM
SSM_HEADS, SSM_P, SSM_GROUPS, SSM_N, SSM_CONV, SSM_CHUNK = 8, 64, 2, 64, 4, 128
SSM_INNER = SSM_HEADS * SSM_P
SSM_CONV_DIM = SSM_INNER + 2 * SSM_GROUPS * SSM_N
MLA_HEADS, MLA_NOPE, MLA_ROPE, MLA_V, MLA_Q_RANK, MLA_KV_RANK = 4, 64, 32, 64, 256, 128
MLA_QK = MLA_NOPE + MLA_ROPE
ROPE_THETA = 10000.0
D_IN = 2472
D_IN_PAD = 2560
TAIL = 2432
DT_LANE = 32
D_FF = 2816
FF_SHARD = 2 * D_FF // N_DEV
ADAM_LR, ADAM_B1, ADAM_B2, ADAM_EPS, ADAM_WD, ADAM_STEP = 0.001, 0.9, 0.999, 1e-08, 0.01, 10

LANES = 128
ATT_BLK = 256
SB_WIDE = 2
SB_SPENT = -110.0
MLA_WIDE = 4
ROW_BLK = 512
ROW_BLOCK_BYTES = 2 << 20
CONV_COLS = 256
FLAT_W = 1024
VMEM_LIMIT = 56 << 20
MM_TM = (1024, 512, 256, 128)
MM_TN = (1280, 1024, 768, 640, 512, 384, 256, 128)
MM_TK = (1280, 1024, 512, 256, 128)

WEIGHTS = ['mix_norm', 'w_in', 'sb_out_norm', 'ssm_conv_w', 'ssm_conv_b', 'ssm_dt_bias', 'ssm_a_log', 'ssm_d',
           'ssm_out_norm', 'mla_q_norm', 'mla_w_uq', 'mla_kv_norm', 'mla_w_ukv', 'mla_out_norm', 'w_out',
           'ffn_norm', 'ffn_w_up', 'ffn_conv_w', 'ffn_conv_b', 'ffn_w_down', 'final_norm']
SHARDED = {'w_in': 2, 'ssm_conv_w': 2, 'mla_w_uq': 2, 'mla_w_ukv': 2, 'w_out': 1, 'ffn_w_up': 2, 'ffn_conv_w': 2,
           'ffn_w_down': 1}
VPU_WEIGHTS = ('ssm_conv_w', 'ffn_conv_w')
EARLY = ('w_in', 'mla_w_uq', 'mla_w_ukv', 'ssm_conv_w')
LATE = ('w_out', 'ffn_w_up', 'ffn_conv_w', 'ffn_w_down')
REPLICATED = [n for n in WEIGHTS if n not in SHARDED]


def _call(body, *, name, out_shape, grid=(), in_specs=None, out_specs=None, scratch=(), sem=None, **kw):
    params = dict(vmem_limit_bytes=VMEM_LIMIT)
    if sem is not None:
        params['dimension_semantics'] = sem
    return pl.pallas_call(body, name=name, out_shape=out_shape, grid=grid, in_specs=in_specs, out_specs=out_specs,
                          scratch_shapes=list(scratch), compiler_params=pltpu.CompilerParams(**params), **kw)


def _tile(n, prefs):
    for t in prefs:
        if n % t == 0:
            return t
    return n


def _rows(s, w):
    rows = ROW_BLK
    while rows * 2 <= s and s % (rows * 2) == 0 and rows * 2 * w * 4 <= ROW_BLOCK_BYTES:
        rows *= 2
    return _tile(s, (rows,))


def _dot(a, b, dims, precision=None):
    return lax.dot_general(a, b, (dims, ((), ())), preferred_element_type=F32, precision=precision)


def _nn(a, b, precision=None):
    return _dot(a, b, ((1,), (0,)), precision)


def _nt(a, b, precision=None):
    return _dot(a, b, ((1,), (1,)), precision)


def _tn(a, b, precision=None):
    return _dot(a, b, ((0,), (0,)), precision)


def _mxu(f):
    return lambda a, b: f(a.astype(MXU_DTYPE), b.astype(MXU_DTYPE))


_bnn, _bnt, _btn = _mxu(_nn), _mxu(_nt), _mxu(_tn)


def _split2(x):
    hi = x.astype(MXU_DTYPE)
    lo = (x - hi.astype(F32)).astype(MXU_DTYPE)
    return hi, lo


def _sigmoid(x):
    return 0.5 * jnp.tanh(0.5 * x) + 0.5


def _softplus(x):
    return jnp.maximum(x, 0.0) + jnp.log1p(jnp.exp(-jnp.abs(x)))


def _softplus_att(x):
    return jnp.maximum(x, 0.0) + jnp.log(1.0 + jnp.exp(-jnp.abs(x)))


def _cum(x, u):
    rows, b = x.shape[0], u.shape[0]
    n = x.shape[1] // b
    hi, lo = _split2(x)
    stack = [part[:, t * b:(t + 1) * b] for part in (hi, lo) for t in range(n)]
    r = _nn(jnp.concatenate(stack, axis=0), u)
    return jnp.concatenate([r[t * rows:(t + 1) * rows] + r[(n + t) * rows:(n + t + 1) * rows] for t in range(n)], axis=1)


def _causal_loop(qi, tile, carry, reverse, width, first=0):
    last = qi // width
    if reverse:
        return lax.fori_loop(first, last, lambda i, c: tile((last - 1 - i + first) * width, c, False),
                             tile(last * width, carry, True))
    return tile(last * width, lax.fori_loop(first, last, lambda i, c: tile(i * width, c, False), carry), True)


def _spent_loop(qi, tile, carry, width, live):
    last = qi // width
    carry = tile(last * width, carry, True)
    step = lambda st: (st[0] - 1, tile((st[0] - 1) * width, st[1], False))
    return lax.while_loop(lambda st: (st[0] > 0) & live(st[1]), step, (last, carry))


def _causal_mask(blk, width, qi, kb, heads, strict, keys_on_rows=False):
    shape = (width * blk, blk) if keys_on_rows else (heads * blk, width * blk)
    q_idx = lax.broadcasted_iota(jnp.int32, shape, 1 if keys_on_rows else 0)
    k_idx = lax.broadcasted_iota(jnp.int32, shape, 0 if keys_on_rows else 1)
    if heads > 1:
        q_idx = q_idx % blk
    gap = (qi - kb) * blk
    return k_idx < q_idx + gap if strict else k_idx <= q_idx + gap


def mm(a, b, *, name, ta=False, tb=False, res=None, out_dtype=F32, ab=None, bb=None, precision=None):
    a2, b2 = a.shape[-2:], b.shape[-2:]
    (kdim, m) = a2 if ta else a2[::-1]
    (n, k2) = b2 if tb else b2[::-1]
    assert kdim == k2, (a.shape, b.shape, ta, tb)
    assert (ab == 'k') == (bb == 'k')
    kb = ab == 'k'
    nb = a.shape[0] if ab == 'o' else (b.shape[0] if bb == 'o' else None)
    tm, tn = _tile(m, MM_TM), _tile(n, MM_TN)
    tk = kdim if kb else _tile(kdim, MM_TK)
    nk = a.shape[0] if kb else kdim // tk
    dims = ((0 if ta else 1,), (1 if tb else 0,))
    op_dtype = F32 if precision is not None else MXU_DTYPE

    def body(*refs):
        a_ref, b_ref = refs[0], refs[1]
        r_ref = refs[2] if res is not None else None
        o_ref = refs[3] if res is not None else refs[2]
        part = _dot(a_ref[...].astype(op_dtype), b_ref[...].astype(op_dtype), dims, precision)

        def finish(out):
            if res is not None:
                out = out + r_ref[...]
            o_ref[...] = out.astype(out_dtype)

        if nk == 1:
            finish(part)
            return
        acc = refs[-1]
        k = pl.program_id(3)

        @pl.when(k == 0)
        def _():
            acc[...] = part

        @pl.when(k > 0)
        def _():
            acc[...] += part

        @pl.when(k == nk - 1)
        def _():
            finish(acc[...])

    def spec(blk, idx, how):
        if how is None:
            return pl.BlockSpec(blk, idx)
        if how == 'o':
            return pl.BlockSpec((None,) + blk, lambda p, i, j, k: (p,) + idx(p, i, j, k))
        return pl.BlockSpec((None,) + blk, lambda p, i, j, k: (k,) + idx(p, i, j, 0))

    a_spec = spec((tk, tm), lambda p, i, j, k: (k, i), ab) if ta else spec((tm, tk), lambda p, i, j, k: (i, k), ab)
    b_spec = spec((tn, tk), lambda p, i, j, k: (j, k), bb) if tb else spec((tk, tn), lambda p, i, j, k: (k, j), bb)
    o_spec = spec((tm, tn), lambda p, i, j, k: (i, j), None if nb is None else 'o')
    ins, specs = [a, b], [a_spec, b_spec]
    if res is not None:
        ins.append(res)
        specs.append(o_spec)
    out_shape = (m, n) if nb is None else (nb, m, n)
    return _call(body, name=name, out_shape=jax.ShapeDtypeStruct(out_shape, out_dtype),
                 grid=(1 if nb is None else nb, m // tm, n // tn, nk), in_specs=specs, out_specs=o_spec,
                 scratch=[] if nk == 1 else [pltpu.VMEM((tm, tn), F32)],
                 sem=("parallel", "parallel", "parallel", "arbitrary"))(*ins)


def rms_fwd(x, g, *, name, gate=None, out_dtype=F32):
    s, w = x.shape
    bs = _rows(s, w)

    def body(*refs):
        if gate is None:
            x_ref, g_ref, o_ref = refs
            u = x_ref[...]
        else:
            x_ref, z_ref, g_ref, o_ref = refs
            z = z_ref[...]
            u = x_ref[...] * (z * _sigmoid(z))
        r = lax.rsqrt(jnp.mean(u * u, axis=1, keepdims=True) + EPS)
        o_ref[...] = (u * r * g_ref[...]).astype(out_dtype)

    row = pl.BlockSpec((bs, w), lambda i: (i, 0))
    vec = pl.BlockSpec((1, w), lambda i: (0, 0))
    ins = [x] + ([] if gate is None else [gate]) + [g.reshape(1, w)]
    specs = [row] + ([] if gate is None else [row]) + [vec]
    return _call(body, name=name, out_shape=jax.ShapeDtypeStruct((s, w), out_dtype), grid=(s // bs,),
                 in_specs=specs, out_specs=row, sem=("parallel",))(*ins)


def rms_bwd(x, g, dy, *, name, gate=None, add=None):
    s, w = x.shape
    bs = _rows(s, w)

    def body(*refs):
        refs = list(refs)
        x_ref = refs.pop(0)
        z_ref = refs.pop(0) if gate is not None else None
        g_ref = refs.pop(0)
        dy_ref = refs.pop(0)
        add_ref = refs.pop(0) if add is not None else None
        dx_ref = refs.pop(0)
        dz_ref = refs.pop(0) if gate is not None else None
        dg_ref = refs.pop(0)
        i = pl.program_id(0)

        @pl.when(i == 0)
        def _():
            dg_ref[...] = jnp.zeros_like(dg_ref)

        xv = x_ref[...]
        if gate is not None:
            z = z_ref[...]
            sg = _sigmoid(z)
            act = z * sg
            u = xv * act
        else:
            u = xv
        r = lax.rsqrt(jnp.mean(u * u, axis=1, keepdims=True) + EPS)
        dy_v = dy_ref[...]
        dyg = dy_v * g_ref[...]
        du = r * dyg - u * (r * r * r * jnp.mean(dyg * u, axis=1, keepdims=True))
        dg_ref[...] += jnp.sum(dy_v * u * r, axis=0, keepdims=True)
        if gate is not None:
            dx = du * act
            dz_ref[...] = du * xv * (sg * (1.0 + z * (1.0 - sg)))
        else:
            dx = du
        if add is not None:
            dx = dx + add_ref[...]
        dx_ref[...] = dx

    row = pl.BlockSpec((bs, w), lambda i: (i, 0))
    vec = pl.BlockSpec((1, w), lambda i: (0, 0))
    ins = [x] + ([] if gate is None else [gate]) + [g.reshape(1, w), dy] + ([] if add is None else [add])
    specs = [row] + ([] if gate is None else [row]) + [vec, row] + ([] if add is None else [row])
    outs = [jax.ShapeDtypeStruct((s, w), F32)] + ([] if gate is None else [jax.ShapeDtypeStruct((s, w), F32)])
    outs.append(jax.ShapeDtypeStruct((1, w), F32))
    ospecs = [row] + ([] if gate is None else [row]) + [vec]
    return _call(body, name=name, out_shape=outs, grid=(s // bs,), in_specs=specs, out_specs=ospecs,
                 sem=("arbitrary",))(*ins)


def loss_head(y, target, *, name):
    s, w = y.shape
    bs = _rows(s, w)
    nb = s // bs

    def body(y_ref, t_ref, dy_ref, loss_ref, acc):
        i = pl.program_id(0)

        @pl.when(i == 0)
        def _():
            acc[...] = jnp.zeros_like(acc)

        e = y_ref[...] - t_ref[...]
        dy_ref[...] = e * (1.0 / w)
        acc[...] += jnp.sum(e * e, axis=0, keepdims=True)

        @pl.when(i == nb - 1)
        def _():
            loss_ref[...] = jnp.sum(acc[...], axis=1, keepdims=True) * (0.5 / w)

    row = pl.BlockSpec((bs, w), lambda i: (i, 0))
    return _call(body, name=name, out_shape=[jax.ShapeDtypeStruct((s, w), F32), jax.ShapeDtypeStruct((1, 1), F32)],
                 grid=(nb,), in_specs=[row, row], out_specs=[row, pl.BlockSpec((1, 1), lambda i: (0, 0))],
                 scratch=[pltpu.VMEM((1, w), F32)], sem=("arbitrary",))(y, target)


def _rope_tables(positions, s):
    inv_freq = 1.0 / (ROPE_THETA ** (jnp.arange(0, MLA_ROPE, 2, dtype=F32) / MLA_ROPE))
    ang = positions.reshape(s, 1).astype(F32) * inv_freq
    cos, sin = jnp.cos(ang), jnp.sin(ang)
    one, zero = jnp.ones((s, MLA_NOPE), F32), jnp.zeros((s, MLA_NOPE), F32)
    cq = jnp.tile(jnp.concatenate([one, cos, cos], axis=1), (1, MLA_HEADS))
    sq = jnp.tile(jnp.concatenate([zero, sin, sin], axis=1), (1, MLA_HEADS))
    pad1, pad0 = jnp.ones((s, LANES - MLA_ROPE), F32), jnp.zeros((s, LANES - MLA_ROPE), F32)
    ct = jnp.concatenate([cos, cos, pad1], axis=1)
    st = jnp.concatenate([sin, sin, pad0], axis=1)
    half = MLA_ROPE // 2

    def swap(width, starts):
        r = np.zeros((width, width), np.float32)
        for o in starts:
            for i in range(half):
                r[o + half + i, o + i] = -1.0
                r[o + i, o + half + i] = 1.0
        return jnp.asarray(r)

    rq = swap(MLA_HEADS * MLA_QK, [h * MLA_QK + MLA_NOPE for h in range(MLA_HEADS)])
    rt = swap(LANES, [0])
    return (cq, sq, rq), (ct, st, rt)


def rope(x, tabs, *, name, backward=False, add=None):
    cos, sin, rot = tabs
    n, s, w = x.shape
    bs = _rows(s, w)

    def body(*refs):
        if add is None:
            x_ref, c_ref, s_ref, r_ref, o_ref = refs
        else:
            x_ref, c_ref, s_ref, r_ref, a_ref, o_ref = refs
        xv = x_ref[0]
        for j in range(1, n):
            xv = xv + x_ref[j]
        if backward:
            out = xv * c_ref[...] + _nt(xv * s_ref[...], r_ref[...], HIGHEST)
        else:
            out = xv * c_ref[...] + _nn(xv, r_ref[...], HIGHEST) * s_ref[...]
        if add is not None:
            out = out + a_ref[...]
        o_ref[...] = out

    row = pl.BlockSpec((bs, w), lambda i: (i, 0))
    ins = [x, cos, sin, rot] + ([] if add is None else [add])
    specs = [pl.BlockSpec((n, bs, w), lambda i: (0, i, 0)), row, row, pl.BlockSpec((w, w), lambda i: (0, 0))]
    specs += [] if add is None else [row]
    return _call(body, name=name, out_shape=jax.ShapeDtypeStruct((s, w), F32), grid=(s // bs,), in_specs=specs,
                 out_specs=row, sem=("parallel",))(*ins)


MESH = pl.DeviceIdType.MESH
HBM = pl.BlockSpec(memory_space=pltpu.HBM)


def _flip(v, bit):
    return 1 - v if bit else v


class Carried:
    def __init__(self, kind, arrays):
        assert kind in ('gather', 'scatter')
        self.kind, self.arrays, self.n = kind, list(arrays), len(arrays)

    @property
    def out_shape(self):
        lead = (N_DEV,) if self.kind == 'gather' else ()
        return [jax.ShapeDtypeStruct(lead + a.shape, a.dtype) for a in self.arrays]

    @property
    def scratch(self):
        return [pltpu.SemaphoreType.DMA((self.n, N_DEV - 1)), pltpu.SemaphoreType.DMA((self.n, N_DEV - 1)),
                pltpu.SemaphoreType.DMA((self.n,))]

    def _copies(self, in_refs, out_refs, sems):
        send_sems, recv_sems, local_sems = sems
        x, y, c = lax.axis_index("x"), lax.axis_index("y"), lax.axis_index("c")
        me = 4 * x + 2 * y + c
        part = (lambda b, p: in_refs[b]) if self.kind == 'gather' else (lambda b, p: in_refs[b].at[p])
        local = [pltpu.make_async_copy(part(b, me), out_refs[b].at[me], local_sems.at[b]) for b in range(self.n)]
        remote = []
        for k in range(1, N_DEV):
            px, py, pc = _flip(x, k & 4), _flip(y, k & 2), _flip(c, k & 1)
            for b in range(self.n):
                remote.append(pltpu.make_async_remote_copy(
                    src_ref=part(b, 4 * px + 2 * py + pc), dst_ref=out_refs[b].at[me], send_sem=send_sems.at[b, k - 1],
                    recv_sem=recv_sems.at[b, k - 1], device_id=(px, py, pc), device_id_type=MESH))
        return local, remote

    def start(self, in_refs, out_refs, sems):
        local, remote = self._copies(in_refs, out_refs, sems)
        for cp in local + remote:
            cp.start()

    def wait(self, in_refs, out_refs, sems):
        local, remote = self._copies(in_refs, out_refs, sems)
        for cp in remote:
            cp.wait_send()
            cp.wait_recv()
        for cp in local:
            cp.wait()


def _ride(carried, refs, n_in, n_out):
    n = 0 if carried is None else carried.n
    own_in, ride_in = refs[:n_in], refs[n_in:n_in + n]
    own_out, ride_out = refs[n_in + n:n_in + n + n_out], refs[n_in + n + n_out:n_in + 2 * n + n_out]
    return own_in, own_out, (ride_in, ride_out, refs[n_in + 2 * n + n_out:])


def _tri(n, op):
    r = lax.broadcasted_iota(jnp.int32, (n, n), 0)
    c = lax.broadcasted_iota(jnp.int32, (n, n), 1)
    return r, c, op(r, c)


def _pair_split(x, first):
    zero = jnp.zeros_like(x)
    return jnp.where(first, x, zero), jnp.where(first, zero, x)


def _sb_specs(s, blk):
    npair = SB_WIDTH // LANES
    q = pl.BlockSpec((blk, LANES), lambda j, i: (i, j))
    k = pl.BlockSpec((s, LANES), lambda j, i: (0, npair + j))
    v = pl.BlockSpec((s, LANES), lambda j, i: (0, 2 * npair + j))
    full = pl.BlockSpec((s, LANES), lambda j, i: (0, j))
    return npair, q, k, v, full


def _stack_heads(x, first):
    return jnp.concatenate(_pair_split(x, first), axis=0)


def _unstack_heads(x, first, blk):
    return jnp.where(first, x[:blk], x[blk:])


def sb_fwd(qkv, *, name, carried=None):
    s = qkv.shape[0]
    blk = _tile(s, (ATT_BLK,))
    scale = SB_DIM ** -0.5
    npair, nq = SB_WIDTH // LANES, s // blk
    assert nq % SB_WIDE == 0

    def body(*refs):
        (q_ref, k_ref, v_ref), (y_ref,), ride = _ride(carried, refs, 3, 1)
        pair, qi = pl.program_id(0), pl.program_id(1)
        if carried is not None:
            @pl.when((pair == 0) & (qi == 0))
            def _():
                carried.start(*ride)

        first = lax.broadcasted_iota(jnp.int32, (blk, LANES), 1) < SB_DIM
        q2 = _stack_heads((q_ref[...].astype(F32) * scale).astype(MXU_DTYPE), first)
        row, col, later_mask = _tri(blk, lambda r, c: r > c)
        u_later = later_mask.astype(MXU_DTYPE)
        n = SB_WIDE

        def tile(kb, carry, masked):
            c, acc = carry
            keys = pl.ds(pl.multiple_of(kb * blk, blk), n * blk)
            z = _nt(q2, k_ref[keys, :])
            sp = _softplus_att(z)
            if masked:
                valid = _causal_mask(blk, n, qi, kb, 2, True)
            spm = jnp.where(valid, sp, 0.0) if masked else sp
            later = _cum(spm, u_later)
            sums = [jnp.sum(spm[:, t * blk:(t + 1) * blk], axis=1, keepdims=True) for t in range(n)]
            after, cols = c, [None] * n
            for t in reversed(range(n)):
                cols[t] = jnp.broadcast_to(after, (2 * blk, blk))
                after = after - sums[t]
            w = jnp.exp((z - sp) - later + (cols[0] if n == 1 else jnp.concatenate(cols, axis=1)))
            if masked:
                w = jnp.where(valid, w, 0.0)
            return after, acc + _nn(w.astype(MXU_DTYPE), v_ref[keys, :])

        zero = (jnp.zeros((2 * blk, 1), F32), jnp.zeros((2 * blk, LANES), F32))
        _, (c, acc) = _spent_loop(qi, tile, zero, SB_WIDE, lambda cr: jnp.max(cr[0]) >= SB_SPENT)
        y_ref[...] = _unstack_heads(acc, first, blk)
        if carried is not None:
            @pl.when((pair == npair - 1) & (qi == nq - 1))
            def _():
                carried.wait(*ride)

    _, qspec, kspec, vspec, _ = _sb_specs(s, blk)
    out = jax.ShapeDtypeStruct((s, SB_WIDTH), F32)
    extra = [] if carried is None else carried.arrays
    return _call(body, name=name, out_shape=[out] + ([] if carried is None else carried.out_shape),
                 grid=(npair, nq), in_specs=[qspec, kspec, vspec] + [HBM] * len(extra),
                 out_specs=[qspec] + [HBM] * len(extra), scratch=[] if carried is None else carried.scratch,
                 sem=("arbitrary", "arbitrary"))(qkv, qkv, qkv, *extra)


def sb_bwd(qkv, dy, *, name, carried=None):
    s = qkv.shape[0]
    blk = _tile(s, (ATT_BLK,))
    scale = SB_DIM ** -0.5
    npair, nq = SB_WIDTH // LANES, s // blk
    assert nq % SB_WIDE == 0

    def body(*refs):
        (q_ref, k_ref, v_ref, dy_ref), (dq_ref, dk_ref, dv_ref), ride = _ride(carried, refs, 4, 3)
        pair, qi = pl.program_id(0), pl.program_id(1)
        if carried is not None:
            @pl.when((pair == 0) & (qi == 0))
            def _():
                carried.start(*ride)

        @pl.when(qi == 0)
        def _():
            dk_ref[...] = jnp.zeros_like(dk_ref)
            dv_ref[...] = jnp.zeros_like(dv_ref)

        first = lax.broadcasted_iota(jnp.int32, (blk, LANES), 1) < SB_DIM
        q2 = _stack_heads((q_ref[...].astype(F32) * scale).astype(MXU_DTYPE), first)
        dy2 = _stack_heads(dy_ref[...].astype(MXU_DTYPE), first)
        row, col, incl_mask = _tri(blk, lambda r, c: r <= c)
        u_incl = incl_mask.astype(MXU_DTYPE)
        u_excl = (row < col).astype(MXU_DTYPE)
        n = SB_WIDE

        def walk(kb, c, masked):
            sp = _softplus_att(_nt(q2, k_ref[pl.ds(pl.multiple_of(kb * blk, blk), n * blk), :]))
            if masked:
                sp = jnp.where(_causal_mask(blk, n, qi, kb, 2, True), sp, 0.0)
            for t in reversed(range(n)):
                c = c - jnp.sum(sp[:, t * blk:(t + 1) * blk], axis=1, keepdims=True)
            return c

        start, tv = _spent_loop(qi, walk, jnp.zeros((2 * blk, 1), F32), SB_WIDE, lambda c: jnp.max(c) >= SB_SPENT)

        def prefixed(x, carry):
            cols = []
            for t in range(n):
                cols.append(jnp.broadcast_to(carry, (2 * blk, blk)))
                carry = carry + jnp.sum(x[:, t * blk:(t + 1) * blk], axis=1, keepdims=True)
            return (cols[0] if n == 1 else jnp.concatenate(cols, axis=1)), carry

        def tile(kb, carry, masked):
            p, gc, dq = carry
            keys = pl.ds(pl.multiple_of(kb * blk, blk), n * blk)
            kv = k_ref[keys, :]
            z = _nt(q2, kv)
            dw = _nt(dy2, v_ref[keys, :])
            sp = _softplus_att(z)
            if masked:
                valid = _causal_mask(blk, n, qi, kb, 2, True)
            spm = jnp.where(valid, sp, 0.0) if masked else sp
            before, p = prefixed(spm, p)
            w = jnp.exp((z - sp) + (_cum(spm, u_incl) + before))
            if masked:
                w = jnp.where(valid, w, 0.0)
            g = w * dw
            gbefore, gc = prefixed(g, gc)
            gb = g.astype(MXU_DTYPE)
            gin = _nn(jnp.concatenate([gb[:, t * blk:(t + 1) * blk] for t in range(n)], axis=0), u_excl)
            gex = gbefore + jnp.concatenate([gin[t * 2 * blk:(t + 1) * 2 * blk] for t in range(n)], axis=1)
            keep = jnp.exp(-spm)
            dz = keep * (g + gex) - gex
            if masked:
                dz = jnp.where(valid, dz, 0.0)
            dzb = dz.astype(MXU_DTYPE)
            dk_ref[keys, :] += _tn(dzb, q2)
            dv_ref[keys, :] += _tn(w.astype(MXU_DTYPE), dy2)
            return p, gc, dq + _nn(dzb, kv)

        zero = jnp.zeros((2 * blk, 1), F32)
        _, _, dq = _causal_loop(qi, tile, (tv, zero, jnp.zeros((2 * blk, LANES), F32)), False, SB_WIDE, first=start)
        dq_ref[...] = _unstack_heads(dq, first, blk) * scale
        if carried is not None:
            @pl.when((pair == npair - 1) & (qi == nq - 1))
            def _():
                carried.wait(*ride)

    _, qspec, kspec, vspec, full = _sb_specs(s, blk)
    out = jax.ShapeDtypeStruct((s, SB_WIDTH), F32)
    extra = [] if carried is None else carried.arrays
    return _call(body, name=name, out_shape=[out, out, out] + ([] if carried is None else carried.out_shape),
                 grid=(npair, nq), in_specs=[qspec, kspec, vspec, qspec] + [HBM] * len(extra),
                 out_specs=[qspec, full, full] + [HBM] * len(extra), scratch=[] if carried is None else carried.scratch,
                 sem=("arbitrary", "arbitrary"))(qkv, qkv, qkv, dy, *extra)


ATT_PAIR = 2


def _mla_specs(s, blk, dk, dv):
    q = pl.BlockSpec((ATT_PAIR, blk, dk), lambda hp, i: (hp, i, 0))
    k = pl.BlockSpec((ATT_PAIR, s, dk), lambda hp, i: (hp, 0, 0))
    v = pl.BlockSpec((ATT_PAIR, s, dv), lambda hp, i: (hp, 0, 0))
    y = pl.BlockSpec((ATT_PAIR, blk, dv), lambda hp, i: (hp, i, 0))
    lse = pl.BlockSpec((ATT_PAIR, blk, LANES), lambda hp, i: (hp, i, 0))
    return q, k, v, y, lse


def mla_fwd(q, k, v, *, name):
    h, s, dk = q.shape
    dv = v.shape[-1]
    blk = _tile(s, (ATT_BLK,))
    scale = dk ** -0.5
    assert (s // blk) % MLA_WIDE == 0

    def body(q_ref, k_ref, v_ref, y_ref, l_ref):
        qi = pl.program_id(1)
        n = MLA_WIDE

        def tile(kb, carry, masked):
            keys = pl.ds(pl.multiple_of(kb * blk, blk), n * blk)
            out = []
            for hh in range(ATT_PAIR):
                m, l, acc = carry[hh]
                sc = _nt(q_ref[hh], k_ref[hh, keys, :]) * scale
                if masked:
                    sc = jnp.where(_causal_mask(blk, n, qi, kb, 1, False), sc, -1e30)
                m2 = jnp.maximum(m, jnp.max(sc, axis=1, keepdims=True))
                p = jnp.exp(sc - m2)
                a = jnp.exp(m - m2)
                out.append((m2, a * l + jnp.sum(p, axis=1, keepdims=True),
                            a * acc + _nn(p.astype(MXU_DTYPE), v_ref[hh, keys, :])))
            return tuple(out)

        init = (jnp.full((blk, 1), -1e30, F32), jnp.zeros((blk, 1), F32), jnp.zeros((blk, dv), F32))
        for hh, (m, l, acc) in enumerate(_causal_loop(qi, tile, (init,) * ATT_PAIR, False, MLA_WIDE)):
            y_ref[hh] = acc / l
            l_ref[hh] = jnp.broadcast_to(m + jnp.log(l), (blk, LANES))

    qspec, kspec, vspec, yspec, lspec = _mla_specs(s, blk, dk, dv)
    return _call(body, name=name,
                 out_shape=[jax.ShapeDtypeStruct((h, s, dv), F32), jax.ShapeDtypeStruct((h, s, LANES), F32)],
                 grid=(h // ATT_PAIR, s // blk), in_specs=[qspec, kspec, vspec], out_specs=[yspec, lspec],
                 sem=("parallel", "arbitrary"))(q, k, v)


def mla_bwd(q, k, v, y, dy, lse, *, name):
    h, s, dk = q.shape
    dv = v.shape[-1]
    blk = _tile(s, (ATT_BLK,))
    scale = dk ** -0.5
    assert (s // blk) % MLA_WIDE == 0

    def body(q_ref, k_ref, v_ref, y_ref, dy_ref, l_ref, dq_ref, dk_ref, dv_ref):
        qi = pl.program_id(1)

        @pl.when(qi == 0)
        def _():
            dk_ref[...] = jnp.zeros_like(dk_ref)
            dv_ref[...] = jnp.zeros_like(dv_ref)

        as_row = lambda col: jnp.transpose(jnp.broadcast_to(col, (blk, LANES)))[0:1, :]
        dyv = [dy_ref[hh].astype(MXU_DTYPE) for hh in range(ATT_PAIR)]
        delta = [as_row(jnp.sum(dy_ref[hh] * y_ref[hh], axis=1, keepdims=True)) for hh in range(ATT_PAIR)]
        lv = [as_row(l_ref[hh, :, 0:1]) for hh in range(ATT_PAIR)]
        n = MLA_WIDE

        def tile(kb, dqs, masked):
            keys = pl.ds(pl.multiple_of(kb * blk, blk), n * blk)
            out = []
            for hh in range(ATT_PAIR):
                qv = q_ref[hh]
                kv = k_ref[hh, keys, :]
                p = jnp.exp(_nt(kv, qv) * scale - lv[hh])
                if masked:
                    p = jnp.where(_causal_mask(blk, n, qi, kb, 1, False, keys_on_rows=True), p, 0.0)
                ds = (p * (_nt(v_ref[hh, keys, :], dyv[hh]) - delta[hh])).astype(MXU_DTYPE)
                dk_ref[hh, keys, :] += _nn(ds, qv) * scale
                dv_ref[hh, keys, :] += _nn(p.astype(MXU_DTYPE), dyv[hh])
                out.append(dqs[hh] + _tn(ds, kv))
            return tuple(out)

        for hh, dq in enumerate(_causal_loop(qi, tile, (jnp.zeros((blk, dk), F32),) * ATT_PAIR, False, MLA_WIDE)):
            dq_ref[hh] = dq * scale

    qspec, kspec, vspec, yspec, lspec = _mla_specs(s, blk, dk, dv)
    return _call(body, name=name,
                 out_shape=[jax.ShapeDtypeStruct((h, s, dk), F32), jax.ShapeDtypeStruct((h, s, dk), F32),
                            jax.ShapeDtypeStruct((h, s, dv), F32)],
                 grid=(h // ATT_PAIR, s // blk), in_specs=[qspec, kspec, vspec, yspec, yspec, lspec],
                 out_specs=[qspec, kspec, vspec], sem=("parallel", "arbitrary"))(q, k, v, y, dy, lse)


HALO = 8
CONV_CHUNK = 16


def _conv_tiles(x):
    s, c = x.shape[-2:]
    return s, c, _tile(s, (ROW_BLK,)), _tile(c, (CONV_COLS,))


def _halo_rows(dtype):
    return HALO * 4 // jnp.dtype(dtype).itemsize


def _conv_specs(bs, cw, lead=(), dtype=F32):
    zero = (0,) * len(lead)
    hr = _halo_rows(dtype)
    blk = pl.BlockSpec(lead + (None, bs, cw), lambda p, j, i: zero + (p, i, j))
    halo = pl.BlockSpec(lead + (None, hr, cw), lambda p, j, i: zero + (p, jnp.maximum(i * (bs // hr) - 1, 0), j))
    w = lambda kk: pl.BlockSpec(lead + (None, kk, cw), lambda p, j, i: zero + (p, 0, j))
    return blk, halo, w


def _stage(scr, x_ref, halo_ref, first):
    hr = halo_ref.shape[0]
    scr[0:HALO, :] = jnp.where(first, 0.0, halo_ref[hr - HALO:hr, :].astype(F32))
    scr[HALO:, :] = x_ref[...].astype(F32)


def _shifted(ext, shift):
    return ext[HALO:] if shift == 0 else pltpu.roll(ext, shift, 0)[HALO:]


def _conv_taps(scr, kk, r0):
    ext = scr[pl.ds(r0, CONV_CHUNK + HALO), :]
    return [_shifted(ext, kk - 1 - k) for k in range(kk)]


def _conv_sum(taps, w_ref, b_ref):
    u = b_ref[...] + taps[0] * w_ref[0:1, :]
    for k in range(1, len(taps)):
        u = u + taps[k] * w_ref[k:k + 1, :]
    return u


def _fold(x):
    out = x[0:8]
    for r in range(8, CONV_CHUNK, 8):
        out = out + x[r:r + 8]
    return out


class _TapSums:
    def __init__(self, kk, cw):
        self.w = [jnp.zeros((8, cw), F32) for _ in range(kk)]
        self.b = jnp.zeros((8, cw), F32)

    def add(self, du, taps):
        self.w = [a + _fold(du * t) for a, t in zip(self.w, taps)]
        self.b = self.b + _fold(du)

    def flush(self, dw_ref, db_ref):
        for k, a in enumerate(self.w):
            dw_ref[k:k + 1, :] += jnp.sum(a, axis=0, keepdims=True)
        db_ref[...] += jnp.sum(self.b, axis=0, keepdims=True)


def _silu_grad(u):
    sg = _sigmoid(u)
    return sg * (1.0 + u * (1.0 - sg))


def conv_silu_fwd(x, w, b, *, name):
    s, c, bs, cw = _conv_tiles(x)
    kk = w.shape[1]

    def body(x_ref, h_ref, w_ref, b_ref, o_ref, scr):
        _stage(scr, x_ref, h_ref, pl.program_id(2) == 0)
        for r0 in range(0, bs, CONV_CHUNK):
            u = _conv_sum(_conv_taps(scr, kk, r0), w_ref, b_ref)
            o_ref[pl.ds(r0, CONV_CHUNK), :] = u * _sigmoid(u)

    blk, halo, wspec = _conv_specs(bs, cw, dtype=x.dtype)
    return _call(body, name=name, out_shape=jax.ShapeDtypeStruct(x.shape, F32), grid=(x.shape[0], c // cw, s // bs),
                 in_specs=[blk, halo, wspec(kk), wspec(1)], out_specs=blk, scratch=[pltpu.VMEM((bs + HALO, cw), F32)],
                 sem=("parallel", "parallel", "arbitrary"))(x, x, w, b)


def conv_silu_bwd(x, dy, w, b, *, name):
    s, c, bs, cw = _conv_tiles(x)
    kk = w.shape[1]

    def body(x_ref, h_ref, w_ref, b_ref, dy_ref, du_ref, dw_ref, db_ref, scr):
        i = pl.program_id(2)

        @pl.when(i == 0)
        def _():
            dw_ref[...] = jnp.zeros_like(dw_ref)
            db_ref[...] = jnp.zeros_like(db_ref)

        _stage(scr, x_ref, h_ref, i == 0)
        sums = _TapSums(kk, cw)
        for r0 in range(0, bs, CONV_CHUNK):
            taps = _conv_taps(scr, kk, r0)
            du = dy_ref[pl.ds(r0, CONV_CHUNK), :] * _silu_grad(_conv_sum(taps, w_ref, b_ref))
            du_ref[pl.ds(r0, CONV_CHUNK), :] = du
            sums.add(du, taps)
        sums.flush(dw_ref, db_ref)

    blk, halo, wspec = _conv_specs(bs, cw, dtype=x.dtype)
    return _call(body, name=name,
                 out_shape=[jax.ShapeDtypeStruct(x.shape, F32), jax.ShapeDtypeStruct(w.shape, F32),
                            jax.ShapeDtypeStruct(b.shape, F32)],
                 grid=(x.shape[0], c // cw, s // bs), in_specs=[blk, halo, wspec(kk), wspec(1), blk],
                 out_specs=[blk, wspec(kk), wspec(1)], scratch=[pltpu.VMEM((bs + HALO, cw), F32)],
                 sem=("parallel", "parallel", "arbitrary"))(x, x, w, b, dy)


def _glu_view(a):
    return a.reshape((2, a.shape[0] // 2) + a.shape[1:])


def conv_glu_fwd(x, w, b, *, name):
    s, c, bs, cw = _conv_tiles(x)
    kk = w.shape[1]
    half = x.shape[0] // 2

    def body(x_ref, h_ref, w_ref, b_ref, o_ref, gscr, vscr):
        first = pl.program_id(2) == 0
        _stage(gscr, x_ref.at[0], h_ref.at[0], first)
        _stage(vscr, x_ref.at[1], h_ref.at[1], first)
        for r0 in range(0, bs, CONV_CHUNK):
            gate = _conv_sum(_conv_taps(gscr, kk, r0), w_ref.at[0], b_ref.at[0])
            val = _conv_sum(_conv_taps(vscr, kk, r0), w_ref.at[1], b_ref.at[1])
            o_ref[pl.ds(r0, CONV_CHUNK), :] = (gate * _sigmoid(gate) * val).astype(o_ref.dtype)

    blk, halo, wspec = _conv_specs(bs, cw, lead=(2,), dtype=x.dtype)
    out, _, _ = _conv_specs(bs, cw)
    xv = _glu_view(x)
    return _call(body, name=name, out_shape=jax.ShapeDtypeStruct((half, s, c), MXU_DTYPE), grid=(half, c // cw, s // bs),
                 in_specs=[blk, halo, wspec(kk), wspec(1)], out_specs=out, scratch=[pltpu.VMEM((bs + HALO, cw), F32)] * 2,
                 sem=("parallel", "parallel", "arbitrary"))(xv, xv, _glu_view(w), _glu_view(b))


def conv_glu_bwd(x, da, w, b, *, name):
    s, c, bs, cw = _conv_tiles(x)
    kk = w.shape[1]
    half = x.shape[0] // 2

    def body(x_ref, h_ref, w_ref, b_ref, da_ref, du_ref, dw_ref, db_ref, gscr, vscr):
        i = pl.program_id(2)

        @pl.when(i == 0)
        def _():
            dw_ref[...] = jnp.zeros_like(dw_ref)
            db_ref[...] = jnp.zeros_like(db_ref)

        _stage(gscr, x_ref.at[0], h_ref.at[0], i == 0)
        _stage(vscr, x_ref.at[1], h_ref.at[1], i == 0)
        gsums, vsums = _TapSums(kk, cw), _TapSums(kk, cw)
        for r0 in range(0, bs, CONV_CHUNK):
            gtaps, vtaps = _conv_taps(gscr, kk, r0), _conv_taps(vscr, kk, r0)
            gate = _conv_sum(gtaps, w_ref.at[0], b_ref.at[0])
            val = _conv_sum(vtaps, w_ref.at[1], b_ref.at[1])
            dav = da_ref[pl.ds(r0, CONV_CHUNK), :]
            dgate = dav * val * _silu_grad(gate)
            dval = dav * gate * _sigmoid(gate)
            du_ref[0, pl.ds(r0, CONV_CHUNK), :] = dgate.astype(du_ref.dtype)
            du_ref[1, pl.ds(r0, CONV_CHUNK), :] = dval.astype(du_ref.dtype)
            gsums.add(dgate, gtaps)
            vsums.add(dval, vtaps)
        gsums.flush(dw_ref.at[0], db_ref.at[0])
        vsums.flush(dw_ref.at[1], db_ref.at[1])

    blk, halo, wspec = _conv_specs(bs, cw, lead=(2,), dtype=x.dtype)
    daspec, _, _ = _conv_specs(bs, cw)
    xv, wv, bv = _glu_view(x), _glu_view(w), _glu_view(b)
    du, dw, db = _call(body, name=name,
                       out_shape=[jax.ShapeDtypeStruct(xv.shape, x.dtype), jax.ShapeDtypeStruct(wv.shape, F32),
                                  jax.ShapeDtypeStruct(bv.shape, F32)],
                       grid=(half, c // cw, s // bs), in_specs=[blk, halo, wspec(kk), wspec(1), daspec],
                       out_specs=[blk, wspec(kk), wspec(1)], scratch=[pltpu.VMEM((bs + HALO, cw), F32)] * 2,
                       sem=("parallel", "parallel", "arbitrary"))(xv, xv, wv, bv, da)
    return du.reshape(x.shape), dw.reshape(w.shape), db.reshape(b.shape)


def conv_t(du, w, *, name, out_dtype=F32):
    s, c, bs, cw = _conv_tiles(du)
    kk = w.shape[1]
    nb = s // bs

    def body(d_ref, h_ref, w_ref, o_ref, scr):
        last = pl.program_id(2) == nb - 1
        scr[0:bs, :] = d_ref[...].astype(F32)
        scr[bs:, :] = jnp.where(last, 0.0, h_ref[0:HALO, :].astype(F32))
        for r0 in range(0, bs, CONV_CHUNK):
            ext = scr[pl.ds(r0, CONV_CHUNK + HALO), :]
            ahead = lambda j: ext[:CONV_CHUNK] if j == 0 else pltpu.roll(ext, CONV_CHUNK + HALO - j, 0)[:CONV_CHUNK]
            acc = ahead(kk - 1) * w_ref[0:1, :]
            for k in range(1, kk):
                acc = acc + ahead(kk - 1 - k) * w_ref[k:k + 1, :]
            o_ref[pl.ds(r0, CONV_CHUNK), :] = acc.astype(out_dtype)

    blk, _, wspec = _conv_specs(bs, cw)
    hr = _halo_rows(du.dtype)
    halo = pl.BlockSpec((None, hr, cw), lambda q, j, i: (q, jnp.minimum((i + 1) * (bs // hr), s // hr - 1), j))
    return _call(body, name=name, out_shape=jax.ShapeDtypeStruct(du.shape, out_dtype), grid=(du.shape[0], c // cw, nb),
                 in_specs=[blk, halo, wspec(kk)], out_specs=blk, scratch=[pltpu.VMEM((bs + HALO, cw), F32)],
                 sem=("parallel", "parallel", "arbitrary"))(du, du, w)


def _ssd_common(xbc_ref, tail_ref, dtrt_ref, bias_ref, biast_ref, alog_ref, alogt_ref):
    L = SSM_CHUNK
    raw = tail_ref[...] + bias_ref[...]
    dt = _softplus(raw)
    dtt = _softplus(dtrt_ref[...] + biast_ref[...])
    a = -jnp.exp(alog_ref[...])
    at = -jnp.exp(alogt_ref[...])
    row, col, lower = _tri(L, lambda r, c: r >= c)
    tril = lower.astype(F32)
    cs = _nn(tril, dt * a, HIGHEST)
    cst = _nt(dtt * at, tril, HIGHEST)
    bm = [xbc_ref[:, SSM_INNER + g * SSM_N: SSM_INNER + (g + 1) * SSM_N] for g in range(SSM_GROUPS)]
    off = SSM_INNER + SSM_GROUPS * SSM_N
    cm = [xbc_ref[:, off + g * SSM_N: off + (g + 1) * SSM_N] for g in range(SSM_GROUPS)]
    cb = [_bnt(cm[g], bm[g]) for g in range(SSM_GROUPS)]
    return raw, dt, a, lower, tril, cs, cst, bm, cm, cb


def _ssd_head(hh, xbc_ref, dt, cs, cst, lower):
    L = SSM_CHUNK
    ln = DT_LANE + hh
    x = xbc_ref[:, hh * SSM_P:(hh + 1) * SSM_P]
    dtc = dt[:, ln:ln + 1]
    csc = cs[:, ln:ln + 1]
    csr = cst[hh:hh + 1, :]
    decay = jnp.exp(jnp.where(lower, csc - csr, -1e30))
    last = cs[L - 1:L, ln:ln + 1]
    return x, dtc, csc, decay, jnp.exp(csc), jnp.exp(last - csc), jnp.exp(last)


def _ssd_inputs(tail, dt_bias, a_log, d_skip):
    H = SSM_HEADS
    lanes = lambda vec: jnp.pad(vec.reshape(1, H), ((0, 0), (DT_LANE, LANES - DT_LANE - H)))
    return (tail, tail[:, DT_LANE:DT_LANE + H].T, lanes(dt_bias), dt_bias.reshape(H, 1), lanes(a_log),
            a_log.reshape(H, 1), lanes(d_skip))


def ssd_fwd(xbc, tail, dt_bias, a_log, d_skip, *, name):
    s = xbc.shape[0]
    L, H, P, N = SSM_CHUNK, SSM_HEADS, SSM_P, SSM_N
    nc = s // L

    def body(xbc_ref, tail_ref, dtrt_ref, bias_ref, biast_ref, alog_ref, alogt_ref, d_ref, y_ref, hp_ref, state):
        @pl.when(pl.program_id(0) == 0)
        def _():
            state[...] = jnp.zeros_like(state)

        raw, dt, a, lower, tril, cs, cst, bm, cm, cb = _ssd_common(
            xbc_ref, tail_ref, dtrt_ref, bias_ref, biast_ref, alog_ref, alogt_ref)
        for hh in range(H):
            g = hh // (H // SSM_GROUPS)
            x, dtc, csc, decay, e, tau, gamma = _ssd_head(hh, xbc_ref, dt, cs, cst, lower)
            xdt = x * dtc
            hprev = state[hh]
            hp_ref[hh] = hprev
            skip = d_ref[:, DT_LANE + hh:DT_LANE + hh + 1]
            y = _bnn(cb[g] * decay, xdt) + _bnn(cm[g], hprev) * e + x * skip
            y_ref[:, hh * P:(hh + 1) * P] = y
            state[hh] = hprev * gamma + _btn(bm[g] * tau, xdt)

    row = lambda w: pl.BlockSpec((L, w), lambda c: (c, 0))
    small = lambda shp: pl.BlockSpec(shp, lambda c: (0, 0))
    return _call(body, name=name,
                 out_shape=[jax.ShapeDtypeStruct((s, SSM_INNER), F32), jax.ShapeDtypeStruct((nc, H, N, P), F32)],
                 grid=(nc,),
                 in_specs=[row(SSM_CONV_DIM), row(LANES), pl.BlockSpec((H, L), lambda c: (0, c)), small((1, LANES)),
                           small((H, 1)), small((1, LANES)), small((H, 1)), small((1, LANES))],
                 out_specs=[row(SSM_INNER), pl.BlockSpec((None, H, N, P), lambda c: (c, 0, 0, 0))],
                 scratch=[pltpu.VMEM((H, N, P), F32)], sem=("arbitrary",))(xbc, *_ssd_inputs(tail, dt_bias, a_log, d_skip))


def ssd_bwd(xbc, tail, dt_bias, a_log, d_skip, hprev_all, dy, *, name):
    s = xbc.shape[0]
    L, H, P, N = SSM_CHUNK, SSM_HEADS, SSM_P, SSM_N
    nc = s // L
    hg = H // SSM_GROUPS

    def body(xbc_ref, tail_ref, dtrt_ref, bias_ref, biast_ref, alog_ref, alogt_ref, d_ref, hp_ref, dy_ref,
             dxbc_ref, ddt_ref, dbias_ref, dalog_ref, dd_ref, dstate):
        @pl.when(pl.program_id(0) == 0)
        def _():
            dstate[...] = jnp.zeros_like(dstate)
            dbias_ref[...] = jnp.zeros_like(dbias_ref)
            dalog_ref[...] = jnp.zeros_like(dalog_ref)
            dd_ref[...] = jnp.zeros_like(dd_ref)

        raw, dt, a, lower, tril, cs, cst, bm, cm, cb = _ssd_common(
            xbc_ref, tail_ref, dtrt_ref, bias_ref, biast_ref, alog_ref, alogt_ref)
        lane = lax.broadcasted_iota(jnp.int32, (L, LANES), 1)
        lane1 = lax.broadcasted_iota(jnp.int32, (1, LANES), 1)
        rowi = lax.broadcasted_iota(jnp.int32, (L, 1), 0)
        slot = lax.broadcasted_iota(jnp.int32, (LANES, L), 0)
        col_sums = jnp.zeros((LANES, L), F32)
        dcs_all = jnp.zeros((L, LANES), F32)
        ddt_x = jnp.zeros((L, LANES), F32)
        dd_row = jnp.zeros((1, LANES), F32)
        dbm = [jnp.zeros((L, N), F32) for _ in range(SSM_GROUPS)]
        dcm = [jnp.zeros((L, N), F32) for _ in range(SSM_GROUPS)]
        dcb = [jnp.zeros((L, L), F32) for _ in range(SSM_GROUPS)]
        for hh in range(H):
            g = hh // hg
            ln = DT_LANE + hh
            x, dtc, csc, decay, e, tau, gamma = _ssd_head(hh, xbc_ref, dt, cs, cst, lower)
            xdt = x * dtc
            hprev = hp_ref[hh]
            dhn = dstate[hh]
            dyh = dy_ref[:, hh * P:(hh + 1) * P]
            m = cb[g] * decay
            dxdt = _btn(m, dyh) + _bnn(bm[g] * tau, dhn)
            dm = jnp.where(lower, _bnt(dyh, xdt), 0.0)
            dcb[g] = dcb[g] + dm * decay
            dseg = dm * m
            dcs = jnp.sum(dseg, axis=1, keepdims=True)
            col_sums = jnp.where(slot == ln, jnp.sum(dseg, axis=0, keepdims=True), col_sums)
            edy = e * dyh
            dcm[g] = dcm[g] + _bnt(edy, hprev)
            dcs = dcs + e * jnp.sum(dyh * _bnn(cm[g], hprev), axis=1, keepdims=True)
            xdh = _bnt(xdt, dhn)
            dbm[g] = dbm[g] + tau * xdh
            dtau_tau = jnp.sum(bm[g] * xdh, axis=1, keepdims=True) * tau
            dlast = jnp.sum(dtau_tau, axis=0, keepdims=True) + gamma * jnp.sum(dhn * hprev, keepdims=True)
            dcs = dcs - dtau_tau + jnp.where(rowi == L - 1, dlast, 0.0)
            dstate[hh] = gamma * dhn + _btn(cm[g], edy)
            dcs_all = jnp.where(lane == ln, dcs, dcs_all)
            ddt_x = jnp.where(lane == ln, jnp.sum(dxdt * x, axis=1, keepdims=True), ddt_x)
            dxbc_ref[:, hh * P:(hh + 1) * P] = dxdt * dtc + d_ref[:, ln:ln + 1] * dyh
            dd_row = jnp.where(lane1 == ln, jnp.sum(dyh * x, keepdims=True), dd_row)
        off = SSM_INNER + SSM_GROUPS * SSM_N
        for g in range(SSM_GROUPS):
            dxbc_ref[:, SSM_INNER + g * N: SSM_INNER + (g + 1) * N] = dbm[g] + _btn(dcb[g], cm[g])
            dxbc_ref[:, off + g * N: off + (g + 1) * N] = dcm[g] + _bnn(dcb[g], bm[g])
        dcs_all = dcs_all - jnp.transpose(col_sums)
        dda = _tn(tril, dcs_all, HIGHEST)
        head_lane = (lane >= DT_LANE) & (lane < DT_LANE + H)
        draw = jnp.where(head_lane, (dda * a + ddt_x) * _sigmoid(raw), 0.0)
        ddt_ref[...] = draw
        dbias_ref[...] += jnp.sum(draw, axis=0, keepdims=True)
        dalog_ref[...] += jnp.sum(jnp.where(head_lane, dda * dt, 0.0), axis=0, keepdims=True) * a
        dd_ref[...] += dd_row

    rev = lambda c: nc - 1 - c
    row = lambda w: pl.BlockSpec((L, w), lambda c: (rev(c), 0))
    small = lambda shp: pl.BlockSpec(shp, lambda c: (0, 0))
    acc = pl.BlockSpec((1, LANES), lambda c: (0, 0))
    vec = jax.ShapeDtypeStruct((1, LANES), F32)
    return _call(body, name=name,
                 out_shape=[jax.ShapeDtypeStruct((s, SSM_CONV_DIM), F32), jax.ShapeDtypeStruct((s, LANES), F32), vec, vec, vec],
                 grid=(nc,),
                 in_specs=[row(SSM_CONV_DIM), row(LANES), pl.BlockSpec((H, L), lambda c: (0, rev(c))), small((1, LANES)),
                           small((H, 1)), small((1, LANES)), small((H, 1)), small((1, LANES)),
                           pl.BlockSpec((None, H, N, P), lambda c: (rev(c), 0, 0, 0)), row(SSM_INNER)],
                 out_specs=[row(SSM_CONV_DIM), row(LANES), acc, acc, acc],
                 scratch=[pltpu.VMEM((H, N, P), F32)], sem=("arbitrary",))(
        xbc, *_ssd_inputs(tail, dt_bias, a_log, d_skip), hprev_all, dy)


def _heads(x2d, n, d):
    s = x2d.shape[0]
    return x2d.reshape(s, n, d).transpose(1, 0, 2)


def _unheads(x3d):
    n, s, d = x3d.shape
    return x3d.transpose(1, 0, 2).reshape(s, n * d)


def layer_fwd(h, p, tabs, li, gather_late=None):
    s = h.shape[0]
    tabq, tabt = tabs
    nm = lambda t: f"L{li}_{t}"
    r = {'h': h}
    hn = rms_fwd(h, p['mix_norm'], name=nm('mixnorm'), out_dtype=MXU_DTYPE)
    proj = mm(hn, p['w_in'], name=nm('proj'))
    r.update(hn=hn, proj=proj)
    qkv = proj[:, :3 * SB_WIDTH].astype(MXU_DTYPE)
    late = None
    if gather_late is None:
        (ya,) = sb_fwd(qkv, name=nm('sb_fwd'))
    else:
        ya, *got = sb_fwd(qkv, name=nm('sb_fwd'), carried=Carried('gather', [gather_late[n] for n in LATE]))
        late = assemble_late(dict(zip(LATE, got)))
        p = dict(p, **{n: late[n][li] for n in LATE})
    yan = rms_fwd(ya, p['sb_out_norm'], name=nm('sbnorm'), out_dtype=MXU_DTYPE)
    r.update(qkv=qkv, ya=ya)
    z = proj[:, 768:1280]
    xbc = proj[None, :, 1280:2048]
    tail = proj[:, TAIL:TAIL + LANES]
    xbc_act = conv_silu_fwd(xbc, p['ssm_conv_w'], p['ssm_conv_b'], name=nm('ssmconv'))[0]
    y_ssm, hprev = ssd_fwd(xbc_act, tail, p['ssm_dt_bias'], p['ssm_a_log'], p['ssm_d'], name=nm('ssd_fwd'))
    ybn = rms_fwd(y_ssm, p['ssm_out_norm'], name=nm('ssmnorm'), gate=z, out_dtype=MXU_DTYPE)
    r.update(z=z, xbc=xbc, tail=tail, xbc_act=xbc_act, y_ssm=y_ssm, hprev=hprev)
    cq = proj[:, 2048:2304]
    ckv = proj[:, 2304:2432]
    qn = rms_fwd(cq, p['mla_q_norm'], name=nm('qnorm'), out_dtype=MXU_DTYPE)
    q_r = rope(mm(qn, p['mla_w_uq'], name=nm('uq'))[None], tabq, name=nm('ropeq'))
    kvn = rms_fwd(ckv, p['mla_kv_norm'], name=nm('kvnorm'), out_dtype=MXU_DTYPE)
    kv = mm(kvn, p['mla_w_ukv'], name=nm('ukv'))
    k_pe = rope(tail[None], tabt, name=nm('ropek'))[:, :MLA_ROPE]
    qh = _heads(q_r, MLA_HEADS, MLA_QK).astype(MXU_DTYPE)
    kvh = _heads(kv, MLA_HEADS, MLA_NOPE + MLA_V)
    kh = jnp.concatenate([kvh[..., :MLA_NOPE], jnp.broadcast_to(k_pe[None], (MLA_HEADS, s, MLA_ROPE))],
                         axis=-1).astype(MXU_DTYPE)
    vh = kvh[..., MLA_NOPE:].astype(MXU_DTYPE)
    yc_h, lse = mla_fwd(qh, kh, vh, name=nm('mla_fwd'))
    yc = _unheads(yc_h)
    ycn = rms_fwd(yc, p['mla_out_norm'], name=nm('mlanorm'), out_dtype=MXU_DTYPE)
    r.update(cq=cq, ckv=ckv, qn=qn, kvn=kvn, qh=qh, kh=kh, vh=vh, yc_h=yc_h, yc=yc, lse=lse)
    ycat = jnp.concatenate([yan, ybn, ycn], axis=1)
    h1 = mm(ycat, p['w_out'], name=nm('outproj'), res=h)
    hn2 = rms_fwd(h1, p['ffn_norm'], name=nm('ffnnorm'), out_dtype=MXU_DTYPE)
    up = mm(hn2, p['ffn_w_up'], name=nm('up'), bb='o', out_dtype=MXU_DTYPE)
    act = conv_glu_fwd(up, p['ffn_conv_w'], p['ffn_conv_b'], name=nm('glu'))
    h2 = mm(act, p['ffn_w_down'], name=nm('down'), ab='k', bb='k', res=h1)
    r.update(ycat=ycat, h1=h1, hn2=hn2, up=up, act=act)
    return h2, r, p, late


def layer_bwd(dh2, p, r, tabs, li, scatter_late=None):
    s = dh2.shape[0]
    tabq, tabt = tabs
    nm = lambda t: f"L{li}_{t}"
    g = {}
    dact = mm(dh2, p['ffn_w_down'], name=nm('d_down_x'), tb=True, bb='o')
    g['ffn_w_down'] = mm(r['act'], dh2, name=nm('d_down_w'), out_dtype=WIRE_DTYPE, ta=True, ab='o')
    du, g['ffn_conv_w'], g['ffn_conv_b'] = conv_glu_bwd(r['up'], dact, p['ffn_conv_w'], p['ffn_conv_b'], name=nm('d_glu'))
    dup = conv_t(du, p['ffn_conv_w'], name=nm('d_ffnconv'), out_dtype=MXU_DTYPE)
    g['ffn_w_up'] = mm(r['hn2'], dup, name=nm('d_up_w'), out_dtype=WIRE_DTYPE, ta=True, bb='o')
    dhn2 = mm(dup, p['ffn_w_up'], name=nm('d_up_x'), tb=True, ab='k', bb='k')
    dh1, dg = rms_bwd(r['h1'], p['ffn_norm'], dhn2, name=nm('d_ffnnorm'), add=dh2)
    g['ffn_norm'] = dg[0]
    dycat = mm(dh1, p['w_out'], name=nm('d_out_x'), tb=True)
    g['w_out'] = mm(r['ycat'], dh1, name=nm('d_out_w'), out_dtype=WIRE_DTYPE, ta=True)
    dya, dg = rms_bwd(r['ya'], p['sb_out_norm'], dycat[:, :256], name=nm('d_sbnorm'))
    g['sb_out_norm'] = dg[0]
    recv_late = None
    if scatter_late is None:
        dq, dk, dv = sb_bwd(r['qkv'], dya, name=nm('sb_bwd'))
    else:
        parts = owner_parts_late([g] + list(scatter_late))
        dq, dk, dv, *got = sb_bwd(r['qkv'], dya, name=nm('sb_bwd'),
                                  carried=Carried('scatter', [parts[n].astype(WIRE_DTYPE) for n in LATE]))
        recv_late = dict(zip(LATE, got))
    dyssm, dz, dg = rms_bwd(r['y_ssm'], p['ssm_out_norm'], dycat[:, 256:768], name=nm('d_ssmnorm'), gate=r['z'])
    g['ssm_out_norm'] = dg[0]
    dxbc_act, ddt_tail, dbias, dalog, dd = ssd_bwd(r['xbc_act'], r['tail'], p['ssm_dt_bias'], p['ssm_a_log'],
                                                   p['ssm_d'], r['hprev'], dyssm, name=nm('ssd_bwd'))
    hl = slice(DT_LANE, DT_LANE + SSM_HEADS)
    g['ssm_dt_bias'], g['ssm_a_log'], g['ssm_d'] = dbias[0, hl], dalog[0, hl], dd[0, hl]
    dxbc_u, g['ssm_conv_w'], g['ssm_conv_b'] = conv_silu_bwd(r['xbc'], dxbc_act[None], p['ssm_conv_w'], p['ssm_conv_b'],
                                                             name=nm('d_ssmact'))
    dxbc = conv_t(dxbc_u, p['ssm_conv_w'], name=nm('d_ssmconv'))[0]
    dyc, dg = rms_bwd(r['yc'], p['mla_out_norm'], dycat[:, 768:], name=nm('d_mlanorm'))
    g['mla_out_norm'] = dg[0]
    dqh, dkh, dvh = mla_bwd(r['qh'], r['kh'], r['vh'], r['yc_h'], _heads(dyc, MLA_HEADS, MLA_V), r['lse'], name=nm('mla_bwd'))
    dq_c = rope(_unheads(dqh)[None], tabq, name=nm('d_ropeq'), backward=True)
    g['mla_w_uq'] = mm(r['qn'], dq_c, name=nm('d_uq_w'), out_dtype=WIRE_DTYPE, ta=True)
    dcq, dg = rms_bwd(r['cq'], p['mla_q_norm'], mm(dq_c, p['mla_w_uq'], name=nm('d_uq_x'), tb=True), name=nm('d_qnorm'))
    g['mla_q_norm'] = dg[0]
    dkv = _unheads(jnp.concatenate([dkh[..., :MLA_NOPE], dvh], axis=-1))
    g['mla_w_ukv'] = mm(r['kvn'], dkv, name=nm('d_ukv_w'), out_dtype=WIRE_DTYPE, ta=True)
    dckv, dg = rms_bwd(r['ckv'], p['mla_kv_norm'], mm(dkv, p['mla_w_ukv'], name=nm('d_ukv_x'), tb=True), name=nm('d_kvnorm'))
    g['mla_kv_norm'] = dg[0]
    dkpe = jnp.pad(dkh[..., MLA_NOPE:], ((0, 0), (0, 0), (0, LANES - MLA_ROPE)))
    dtail = rope(dkpe, tabt, name=nm('d_ropek'), backward=True, add=ddt_tail)
    dproj = jnp.concatenate([dq, dk, dv, dz, dxbc, dcq, dckv, dtail], axis=1).astype(MXU_DTYPE)
    g['w_in'] = mm(r['hn'], dproj, name=nm('d_proj_w'), out_dtype=WIRE_DTYPE, ta=True)
    dhn = mm(dproj, p['w_in'], name=nm('d_proj_x'), tb=True)
    dh, dg = rms_bwd(r['h'], p['mix_norm'], dhn, name=nm('d_mixnorm'), add=dh1)
    g['mix_norm'] = dg[0]
    return dh, g, recv_late


def _w_in_placement():
    c = np.arange(D_IN)
    dest = np.where(c < 2048, c, np.where(c < 2056, c + (D_IN - 2056), c - 8))
    dest = jnp.asarray(dest.reshape(N_DEV, D_IN // N_DEV, 1), jnp.int32)
    return (dest == jnp.arange(D_IN_PAD, dtype=jnp.int32)).astype(MXU_DTYPE)


def _owner_major(full, axis):
    shp = full.shape
    return jnp.moveaxis(full.reshape(shp[:axis] + (N_DEV, shp[axis] // N_DEV) + shp[axis + 1:]), axis, 0)


def _owner_join(parts, axis):
    moved = jnp.moveaxis(parts, 0, axis)
    shp = moved.shape
    return moved.reshape(shp[:axis] + (shp[axis] * shp[axis + 1],) + shp[axis + 2:])


def assemble_early(gathered, replicated):
    L = DEPTH
    out = dict(replicated)
    out['w_in'] = mm(gathered['w_in'].reshape(N_DEV, L * D_MODEL, D_IN // N_DEV), _w_in_placement(), name='place_w_in',
                     ab='k', bb='k', out_dtype=MXU_DTYPE).reshape(L, D_MODEL, D_IN_PAD)
    out['mla_w_uq'] = _owner_join(gathered['mla_w_uq'], 2)
    out['mla_w_ukv'] = _owner_join(gathered['mla_w_ukv'], 2)
    out['ssm_conv_w'] = _owner_join(gathered['ssm_conv_w'], 2)[:, None]
    out['ssm_conv_b'] = replicated['ssm_conv_b'].reshape(L, 1, 1, SSM_CONV_DIM)
    out['ffn_conv_b'] = replicated['ffn_conv_b'].reshape(L, N_DEV, 1, FF_SHARD)
    return out


def assemble_late(gathered):
    L = DEPTH
    return {'ffn_w_up': jnp.moveaxis(gathered['ffn_w_up'], 1, 0),
            'w_out': _owner_join(gathered['w_out'], 1),
            'ffn_w_down': _owner_join(gathered['ffn_w_down'], 1).reshape(L, N_DEV // 2, FF_SHARD, D_MODEL),
            'ffn_conv_w': jnp.moveaxis(gathered['ffn_conv_w'], 1, 0)}


def owner_parts_late(grads):
    L = DEPTH
    st = lambda n: jnp.stack([g[n] for g in grads])
    return {'ffn_w_up': jnp.moveaxis(st('ffn_w_up'), 1, 0),
            'w_out': _owner_major(st('w_out'), 1),
            'ffn_w_down': _owner_major(st('ffn_w_down').reshape(L, D_FF, D_MODEL), 1),
            'ffn_conv_w': jnp.moveaxis(st('ffn_conv_w'), 1, 0)}


def owner_parts_early(grads):
    L = DEPTH
    st = lambda n: jnp.stack([g[n] for g in grads])
    parts = {
        'w_in': mm(st('w_in').reshape(L * D_MODEL, D_IN_PAD), _w_in_placement(), name='unplace_w_in', tb=True, bb='o',
                   out_dtype=WIRE_DTYPE).reshape(N_DEV, L, D_MODEL, D_IN // N_DEV),
        'mla_w_uq': _owner_major(st('mla_w_uq'), 2),
        'mla_w_ukv': _owner_major(st('mla_w_ukv'), 2),
        'ssm_conv_w': _owner_major(st('ssm_conv_w')[:, 0], 2),
    }
    rep = {n: st(n) for n in REPLICATED if n not in ('final_norm', 'ssm_conv_b', 'ffn_conv_b')}
    rep['ssm_conv_b'] = st('ssm_conv_b').reshape(L, SSM_CONV_DIM)
    rep['ffn_conv_b'] = st('ffn_conv_b').reshape(L, 2 * D_FF)
    return parts, rep


def local_step(x, positions, target, early, late_shards, replicated):
    s = x.shape[0]
    tabs = _rope_tables(positions, s)
    params = assemble_early(early, replicated)
    layer = lambda li: {n: params[n][li] for n in params if n != 'final_norm'}
    h, r0, p0, late = layer_fwd(x, layer(0), tabs, 0, gather_late=late_shards)
    saved = [(p0, r0)]
    for li in range(1, DEPTH):
        h, r, p, _ = layer_fwd(h, dict(layer(li), **{n: late[n][li] for n in LATE}), tabs, li)
        saved.append((p, r))
    y = rms_fwd(h, params['final_norm'], name='finalnorm')
    dy, loss = loss_head(y, target, name='loss')
    dh, dg = rms_bwd(h, params['final_norm'], dy, name='d_finalnorm')
    above = []
    for li in reversed(range(1, DEPTH)):
        dh, g, _ = layer_bwd(dh, *saved[li], tabs, li)
        above.insert(0, g)
    dh, g0, recv_late = layer_bwd(dh, *saved[0], tabs, 0, scatter_late=above)
    parts, rep = owner_parts_early([g0] + above)
    rep['final_norm'] = dg[0]
    return loss[0, 0], dh, parts, recv_late, rep


def all_gather(blocks, *, name):
    n = len(blocks)

    def body(*refs):
        x_refs, out_refs = refs[:n], refs[n:2 * n]
        send_sems, recv_sems, local_sems = refs[2 * n:]
        x, y, c = lax.axis_index("x"), lax.axis_index("y"), lax.axis_index("c")
        me, sibling = (x, y, c), (x, y, 1 - c)
        chips = [(1 - x, y), (x, 1 - y), (1 - x, 1 - y)]

        def slot(b, px, py, pc):
            return out_refs[b].at[4 * px + 2 * py + pc]

        def copy(b, k, blk, to, src=None):
            return pltpu.make_async_remote_copy(src_ref=slot(b, *blk) if src is None else src, dst_ref=slot(b, *blk),
                                                send_sem=send_sems.at[b, k], recv_sem=recv_sems.at[b, k],
                                                device_id=to, device_id_type=MESH)

        mine = [pltpu.make_async_copy(x_refs[b], slot(b, *me), local_sems.at[b]) for b in range(n)]
        for cp in mine:
            cp.start()
        first = []
        for b in range(n):
            first.append(copy(b, 0, me, sibling, src=x_refs[b]))
            first += [copy(b, 1 + j, me, (*chip, c), src=x_refs[b]) for j, chip in enumerate(chips)]
        for cp in first:
            cp.start()
        passed = []
        for j, chip in enumerate(chips):
            for b in range(n):
                copy(b, 1 + j, (*chip, c), me).wait_recv()
                fwd = copy(b, 4 + j, (*chip, c), sibling)
                fwd.start()
                passed.append(fwd)
        for b in range(n):
            copy(b, 0, sibling, me).wait_recv()
            for j, chip in enumerate(chips):
                copy(b, 4 + j, (*chip, 1 - c), me).wait_recv()
        for cp in first + passed:
            cp.wait_send()
        for cp in mine:
            cp.wait()

    return pl.pallas_call(
        body, name=name, out_shape=[jax.ShapeDtypeStruct((N_DEV,) + b.shape, b.dtype) for b in blocks],
        in_specs=[HBM] * n, out_specs=[HBM] * n,
        scratch_shapes=[pltpu.SemaphoreType.DMA((n, 7)), pltpu.SemaphoreType.DMA((n, 7)), pltpu.SemaphoreType.DMA((n,))],
    )(*blocks)


def all_to_all(parts, *, name):
    n = len(parts)

    def body(*refs):
        g_refs, r_refs = refs[:n], refs[n:2 * n]
        send_sems, recv_sems, local_sems = refs[2 * n:]
        x, y, c = lax.axis_index("x"), lax.axis_index("y"), lax.axis_index("c")
        me = 4 * x + 2 * y + c
        mine = [pltpu.make_async_copy(g_refs[b].at[me], r_refs[b].at[me], local_sems.at[b]) for b in range(n)]
        for cp in mine:
            cp.start()
        copies = []
        for k in range(1, N_DEV):
            px, py, pc = _flip(x, k & 4), _flip(y, k & 2), _flip(c, k & 1)
            peer = 4 * px + 2 * py + pc
            for b in range(n):
                cp = pltpu.make_async_remote_copy(src_ref=g_refs[b].at[peer], dst_ref=r_refs[b].at[me],
                                                  send_sem=send_sems.at[b, k - 1], recv_sem=recv_sems.at[b, k - 1],
                                                  device_id=(px, py, pc), device_id_type=MESH)
                cp.start()
                copies.append(cp)
        for cp in copies:
            cp.wait_send()
            cp.wait_recv()
        for cp in mine:
            cp.wait()

    return pl.pallas_call(
        body, name=name, out_shape=[jax.ShapeDtypeStruct(p.shape, p.dtype) for p in parts],
        in_specs=[HBM] * n, out_specs=[HBM] * n,
        scratch_shapes=[pltpu.SemaphoreType.DMA((n, 7)), pltpu.SemaphoreType.DMA((n, 7)), pltpu.SemaphoreType.DMA((n,))],
    )(*parts)


def adamw(parts, w, m, v, *, name):
    r, wd = w.shape
    br = _tile(r, (256, 128, 64, 32, 16, 8))
    c1 = 1.0 - ADAM_B1 ** ADAM_STEP
    c2 = 1.0 - ADAM_B2 ** ADAM_STEP

    def body(p_ref, w_ref, m_ref, v_ref, g_ref, d_ref, mo_ref, vo_ref):
        g = p_ref[0].astype(F32)
        for j in range(1, N_DEV):
            g = g + p_ref[j].astype(F32)
        mn = ADAM_B1 * m_ref[...] + (1.0 - ADAM_B1) * g
        vn = ADAM_B2 * v_ref[...] + (1.0 - ADAM_B2) * (g * g)
        g_ref[...] = g
        mo_ref[...] = mn
        vo_ref[...] = vn
        d_ref[...] = -ADAM_LR * ((mn / c1) / (jnp.sqrt(vn / c2) + ADAM_EPS) + ADAM_WD * w_ref[...])

    blk = pl.BlockSpec((br, wd), lambda i: (i, 0))
    out = jax.ShapeDtypeStruct((r, wd), F32)
    return _call(body, name=name, out_shape=[out] * 4, grid=(r // br,),
                 in_specs=[pl.BlockSpec((N_DEV, br, wd), lambda i: (0, i, 0)), blk, blk, blk], out_specs=[blk] * 4,
                 sem=("parallel",))(parts, w, m, v)


def _pack(arrs):
    flat = jnp.concatenate([a.reshape(-1) for a in arrs])
    rows = -(-flat.shape[0] // (8 * FLAT_W)) * 8
    return jnp.pad(flat, (0, rows * FLAT_W - flat.shape[0])).reshape(rows, FLAT_W)


def _unpack(flat, shapes):
    flat = flat.reshape(-1)
    out, off = [], 0
    for shp in shapes:
        n = int(np.prod(shp))
        out.append(flat[off:off + n].reshape(shp))
        off += n
    return out


def kernel(x, positions, mix_norm, w_in, sb_out_norm, ssm_conv_w, ssm_conv_b, ssm_dt_bias, ssm_a_log, ssm_d, ssm_out_norm, mla_q_norm, mla_w_uq, mla_kv_norm, mla_w_ukv, mla_out_norm, w_out, ffn_norm, ffn_w_up, ffn_conv_w, ffn_conv_b, ffn_w_down, final_norm, loss_target, m_mix_norm, m_w_in, m_sb_out_norm, m_ssm_conv_w, m_ssm_conv_b, m_ssm_dt_bias, m_ssm_a_log, m_ssm_d, m_ssm_out_norm, m_mla_q_norm, m_mla_w_uq, m_mla_kv_norm, m_mla_w_ukv, m_mla_out_norm, m_w_out, m_ffn_norm, m_ffn_w_up, m_ffn_conv_w, m_ffn_conv_b, m_ffn_w_down, m_final_norm, v_mix_norm, v_w_in, v_sb_out_norm, v_ssm_conv_w, v_ssm_conv_b, v_ssm_dt_bias, v_ssm_a_log, v_ssm_d, v_ssm_out_norm, v_mla_q_norm, v_mla_w_uq, v_mla_kv_norm, v_mla_w_ukv, v_mla_out_norm, v_w_out, v_ffn_norm, v_ffn_w_up, v_ffn_conv_w, v_ffn_conv_b, v_ffn_w_down, v_final_norm):
    args = locals()
    w = {n: args[n] for n in WEIGHTS}
    m = {n: args['m_' + n] for n in WEIGHTS}
    v = {n: args['v_' + n] for n in WEIGHTS}
    wire = lambda n: w[n] if n in VPU_WEIGHTS else w[n].astype(BF16)
    early = dict(zip(EARLY, all_gather([wire(n) for n in EARLY], name='gather_early')))

    loss, dx, parts, recv, rep = local_step(x[0], positions[0], loss_target[0], early, {n: wire(n) for n in LATE},
                                            {n: w[n] for n in REPLICATED})
    loss = lax.psum(loss, ("x", "y", "c"))

    recv.update(zip(EARLY, all_to_all([parts[n].astype(WIRE_DTYPE) for n in EARLY], name='scatter_early')))
    res = {kind: {} for kind in 'gdmv'}
    for n, rv in recv.items():
        shp = w[n].shape
        two_d = (int(np.prod(shp[:-1])), shp[-1])
        outs = adamw(rv.reshape((N_DEV,) + two_d), w[n].reshape(two_d), m[n].reshape(two_d), v[n].reshape(two_d),
                     name='adamw_' + n)
        for kind, o in zip('gdmv', outs):
            res[kind][n] = o.reshape(shp)

    rep_shapes = [w[n].shape for n in REPLICATED]
    (rparts,) = all_gather([_pack([rep[n] for n in REPLICATED])], name='gather_small_grads')
    rflat = lambda d: _pack([d[n] for n in REPLICATED])
    routs = adamw(rparts, rflat(w), rflat(m), rflat(v), name='adamw_replicated')
    for kind, o in zip('gdmv', routs):
        res[kind].update(zip(REPLICATED, _unpack(o, rep_shapes)))

    return (loss, dx[None], *[res['g'][n] for n in WEIGHTS], *[res['d'][n] for n in WEIGHTS],
            *[res['m'][n] for n in WEIGHTS], *[res['v'][n] for n in WEIGHTS])
```

```python
import numpy as np
import jax
import jax.numpy as jnp
from jax import lax
from jax.experimental import pallas as pl
from jax.experimental.pallas import tpu as pltpu

F32 = jnp.float32
BF16 = jnp.bfloat16
MXU_DTYPE = jnp.bfloat16
HIGHEST = lax.Precision.HIGHEST
WIRE_DTYPE = jnp.bfloat16

N_DEV = 8
D_MODEL = 1024
DEPTH = 2
EPS = 1e-6
SB_HEADS, SB_DIM = 4, 64
SB_WIDTH = SB_HEADS * SB_DIM
SSM_HEADS, SSM_P, SSM_GROUPS, SSM_N, SSM_CONV, SSM_CHUNK = 8, 64, 2, 64, 4, 128
SSM_INNER = SSM_HEADS * SSM_P
SSM_CONV_DIM = SSM_INNER + 2 * SSM_GROUPS * SSM_N
MLA_HEADS, MLA_NOPE, MLA_ROPE, MLA_V, MLA_Q_RANK, MLA_KV_RANK = 4, 64, 32, 64, 256, 128
MLA_QK = MLA_NOPE + MLA_ROPE
ROPE_THETA = 10000.0
D_IN = 2472
D_IN_PAD = 2560
TAIL = 2432
DT_LANE = 32
D_FF = 2816
FF_SHARD = 2 * D_FF // N_DEV
ADAM_LR, ADAM_B1, ADAM_B2, ADAM_EPS, ADAM_WD, ADAM_STEP = 0.001, 0.9, 0.999, 1e-08, 0.01, 10

LANES = 128
ATT_BLK = 256
SB_WIDE = 2
SB_SPENT = -110.0
MLA_WIDE = 4
ROW_BLK = 512
ROW_BLOCK_BYTES = 2 << 20
CONV_COLS = 256
FLAT_W = 1024
VMEM_LIMIT = 56 << 20
MM_TM = (1024, 512, 256, 128)
MM_TN = (1280, 1024, 768, 640, 512, 384, 256, 128)
MM_TK = (1280, 1024, 512, 256, 128)

WEIGHTS = ['mix_norm', 'w_in', 'sb_out_norm', 'ssm_conv_w', 'ssm_conv_b', 'ssm_dt_bias', 'ssm_a_log', 'ssm_d',
           'ssm_out_norm', 'mla_q_norm', 'mla_w_uq', 'mla_kv_norm', 'mla_w_ukv', 'mla_out_norm', 'w_out',
           'ffn_norm', 'ffn_w_up', 'ffn_conv_w', 'ffn_conv_b', 'ffn_w_down', 'final_norm']
SHARDED = {'w_in': 2, 'ssm_conv_w': 2, 'mla_w_uq': 2, 'mla_w_ukv': 2, 'w_out': 1, 'ffn_w_up': 2, 'ffn_conv_w': 2,
           'ffn_w_down': 1}
VPU_WEIGHTS = ('ssm_conv_w', 'ffn_conv_w')
EARLY = ('w_in', 'mla_w_uq', 'mla_w_ukv', 'ssm_conv_w')
LATE = ('w_out', 'ffn_w_up', 'ffn_conv_w', 'ffn_w_down')
REPLICATED = [n for n in WEIGHTS if n not in SHARDED]


def _call(body, *, name, out_shape, grid=(), in_specs=None, out_specs=None, scratch=(), sem=None, **kw):
    params = dict(vmem_limit_bytes=VMEM_LIMIT)
    if sem is not None:
        params['dimension_semantics'] = sem
    return pl.pallas_call(body, name=name, out_shape=out_shape, grid=grid, in_specs=in_specs, out_specs=out_specs,
                          scratch_shapes=list(scratch), compiler_params=pltpu.CompilerParams(**params), **kw)


def _tile(n, prefs):
    for t in prefs:
        if n % t == 0:
            return t
    return n


def _rows(s, w):
    rows = ROW_BLK
    while rows * 2 <= s and s % (rows * 2) == 0 and rows * 2 * w * 4 <= ROW_BLOCK_BYTES:
        rows *= 2
    return _tile(s, (rows,))


def _dot(a, b, dims, precision=None):
    return lax.dot_general(a, b, (dims, ((), ())), preferred_element_type=F32, precision=precision)


def _nn(a, b, precision=None):
    return _dot(a, b, ((1,), (0,)), precision)


def _nt(a, b, precision=None):
    return _dot(a, b, ((1,), (1,)), precision)


def _tn(a, b, precision=None):
    return _dot(a, b, ((0,), (0,)), precision)


def _mxu(f):
    return lambda a, b: f(a.astype(MXU_DTYPE), b.astype(MXU_DTYPE))


_bnn, _bnt, _btn = _mxu(_nn), _mxu(_nt), _mxu(_tn)


def _split2(x):
    hi = x.astype(MXU_DTYPE)
    lo = (x - hi.astype(F32)).astype(MXU_DTYPE)
    return hi, lo


def _sigmoid(x):
    return 0.5 * jnp.tanh(0.5 * x) + 0.5


def _softplus(x):
    return jnp.maximum(x, 0.0) + jnp.log1p(jnp.exp(-jnp.abs(x)))


def _softplus_att(x):
    return jnp.maximum(x, 0.0) + jnp.log(1.0 + jnp.exp(-jnp.abs(x)))


def _cum(x, u):
    rows, b = x.shape[0], u.shape[0]
    n = x.shape[1] // b
    hi, lo = _split2(x)
    stack = [part[:, t * b:(t + 1) * b] for part in (hi, lo) for t in range(n)]
    r = _nn(jnp.concatenate(stack, axis=0), u)
    return jnp.concatenate([r[t * rows:(t + 1) * rows] + r[(n + t) * rows:(n + t + 1) * rows] for t in range(n)], axis=1)


def _causal_loop(qi, tile, carry, reverse, width, first=0):
    last = qi // width
    if reverse:
        return lax.fori_loop(first, last, lambda i, c: tile((last - 1 - i + first) * width, c, False),
                             tile(last * width, carry, True))
    return tile(last * width, lax.fori_loop(first, last, lambda i, c: tile(i * width, c, False), carry), True)


def _spent_loop(qi, tile, carry, width, live):
    last = qi // width
    carry = tile(last * width, carry, True)
    step = lambda st: (st[0] - 1, tile((st[0] - 1) * width, st[1], False))
    return lax.while_loop(lambda st: (st[0] > 0) & live(st[1]), step, (last, carry))


def _causal_mask(blk, width, qi, kb, heads, strict, keys_on_rows=False):
    shape = (width * blk, blk) if keys_on_rows else (heads * blk, width * blk)
    q_idx = lax.broadcasted_iota(jnp.int32, shape, 1 if keys_on_rows else 0)
    k_idx = lax.broadcasted_iota(jnp.int32, shape, 0 if keys_on_rows else 1)
    if heads > 1:
        q_idx = q_idx % blk
    gap = (qi - kb) * blk
    return k_idx < q_idx + gap if strict else k_idx <= q_idx + gap


def mm(a, b, *, name, ta=False, tb=False, res=None, out_dtype=F32, ab=None, bb=None, precision=None):
    a2, b2 = a.shape[-2:], b.shape[-2:]
    (kdim, m) = a2 if ta else a2[::-1]
    (n, k2) = b2 if tb else b2[::-1]
    assert kdim == k2, (a.shape, b.shape, ta, tb)
    assert (ab == 'k') == (bb == 'k')
    kb = ab == 'k'
    nb = a.shape[0] if ab == 'o' else (b.shape[0] if bb == 'o' else None)
    tm, tn = _tile(m, MM_TM), _tile(n, MM_TN)
    tk = kdim if kb else _tile(kdim, MM_TK)
    nk = a.shape[0] if kb else kdim // tk
    dims = ((0 if ta else 1,), (1 if tb else 0,))
    op_dtype = F32 if precision is not None else MXU_DTYPE

    def body(*refs):
        a_ref, b_ref = refs[0], refs[1]
        r_ref = refs[2] if res is not None else None
        o_ref = refs[3] if res is not None else refs[2]
        part = _dot(a_ref[...].astype(op_dtype), b_ref[...].astype(op_dtype), dims, precision)

        def finish(out):
            if res is not None:
                out = out + r_ref[...]
            o_ref[...] = out.astype(out_dtype)

        if nk == 1:
            finish(part)
            return
        acc = refs[-1]
        k = pl.program_id(3)

        @pl.when(k == 0)
        def _():
            acc[...] = part

        @pl.when(k > 0)
        def _():
            acc[...] += part

        @pl.when(k == nk - 1)
        def _():
            finish(acc[...])

    def spec(blk, idx, how):
        if how is None:
            return pl.BlockSpec(blk, idx)
        if how == 'o':
            return pl.BlockSpec((None,) + blk, lambda p, i, j, k: (p,) + idx(p, i, j, k))
        return pl.BlockSpec((None,) + blk, lambda p, i, j, k: (k,) + idx(p, i, j, 0))

    a_spec = spec((tk, tm), lambda p, i, j, k: (k, i), ab) if ta else spec((tm, tk), lambda p, i, j, k: (i, k), ab)
    b_spec = spec((tn, tk), lambda p, i, j, k: (j, k), bb) if tb else spec((tk, tn), lambda p, i, j, k: (k, j), bb)
    o_spec = spec((tm, tn), lambda p, i, j, k: (i, j), None if nb is None else 'o')
    ins, specs = [a, b], [a_spec, b_spec]
    if res is not None:
        ins.append(res)
        specs.append(o_spec)
    out_shape = (m, n) if nb is None else (nb, m, n)
    return _call(body, name=name, out_shape=jax.ShapeDtypeStruct(out_shape, out_dtype),
                 grid=(1 if nb is None else nb, m // tm, n // tn, nk), in_specs=specs, out_specs=o_spec,
                 scratch=[] if nk == 1 else [pltpu.VMEM((tm, tn), F32)],
                 sem=("parallel", "parallel", "parallel", "arbitrary"))(*ins)


def rms_fwd(x, g, *, name, gate=None, out_dtype=F32):
    s, w = x.shape
    bs = _rows(s, w)

    def body(*refs):
        if gate is None:
            x_ref, g_ref, o_ref = refs
            u = x_ref[...]
        else:
            x_ref, z_ref, g_ref, o_ref = refs
            z = z_ref[...]
            u = x_ref[...] * (z * _sigmoid(z))
        r = lax.rsqrt(jnp.mean(u * u, axis=1, keepdims=True) + EPS)
        o_ref[...] = (u * r * g_ref[...]).astype(out_dtype)

    row = pl.BlockSpec((bs, w), lambda i: (i, 0))
    vec = pl.BlockSpec((1, w), lambda i: (0, 0))
    ins = [x] + ([] if gate is None else [gate]) + [g.reshape(1, w)]
    specs = [row] + ([] if gate is None else [row]) + [vec]
    return _call(body, name=name, out_shape=jax.ShapeDtypeStruct((s, w), out_dtype), grid=(s // bs,),
                 in_specs=specs, out_specs=row, sem=("parallel",))(*ins)


def rms_bwd(x, g, dy, *, name, gate=None, add=None):
    s, w = x.shape
    bs = _rows(s, w)

    def body(*refs):
        refs = list(refs)
        x_ref = refs.pop(0)
        z_ref = refs.pop(0) if gate is not None else None
        g_ref = refs.pop(0)
        dy_ref = refs.pop(0)
        add_ref = refs.pop(0) if add is not None else None
        dx_ref = refs.pop(0)
        dz_ref = refs.pop(0) if gate is not None else None
        dg_ref = refs.pop(0)
        i = pl.program_id(0)

        @pl.when(i == 0)
        def _():
            dg_ref[...] = jnp.zeros_like(dg_ref)

        xv = x_ref[...]
        if gate is not None:
            z = z_ref[...]
            sg = _sigmoid(z)
            act = z * sg
            u = xv * act
        else:
            u = xv
        r = lax.rsqrt(jnp.mean(u * u, axis=1, keepdims=True) + EPS)
        dy_v = dy_ref[...]
        dyg = dy_v * g_ref[...]
        du = r * dyg - u * (r * r * r * jnp.mean(dyg * u, axis=1, keepdims=True))
        dg_ref[...] += jnp.sum(dy_v * u * r, axis=0, keepdims=True)
        if gate is not None:
            dx = du * act
            dz_ref[...] = du * xv * (sg * (1.0 + z * (1.0 - sg)))
        else:
            dx = du
        if add is not None:
            dx = dx + add_ref[...]
        dx_ref[...] = dx

    row = pl.BlockSpec((bs, w), lambda i: (i, 0))
    vec = pl.BlockSpec((1, w), lambda i: (0, 0))
    ins = [x] + ([] if gate is None else [gate]) + [g.reshape(1, w), dy] + ([] if add is None else [add])
    specs = [row] + ([] if gate is None else [row]) + [vec, row] + ([] if add is None else [row])
    outs = [jax.ShapeDtypeStruct((s, w), F32)] + ([] if gate is None else [jax.ShapeDtypeStruct((s, w), F32)])
    outs.append(jax.ShapeDtypeStruct((1, w), F32))
    ospecs = [row] + ([] if gate is None else [row]) + [vec]
    return _call(body, name=name, out_shape=outs, grid=(s // bs,), in_specs=specs, out_specs=ospecs,
                 sem=("arbitrary",))(*ins)


def loss_head(y, target, *, name):
    s, w = y.shape
    bs = _rows(s, w)
    nb = s // bs

    def body(y_ref, t_ref, dy_ref, loss_ref, acc):
        i = pl.program_id(0)

        @pl.when(i == 0)
        def _():
            acc[...] = jnp.zeros_like(acc)

        e = y_ref[...] - t_ref[...]
        dy_ref[...] = e * (1.0 / w)
        acc[...] += jnp.sum(e * e, axis=0, keepdims=True)

        @pl.when(i == nb - 1)
        def _():
            loss_ref[...] = jnp.sum(acc[...], axis=1, keepdims=True) * (0.5 / w)

    row = pl.BlockSpec((bs, w), lambda i: (i, 0))
    return _call(body, name=name, out_shape=[jax.ShapeDtypeStruct((s, w), F32), jax.ShapeDtypeStruct((1, 1), F32)],
                 grid=(nb,), in_specs=[row, row], out_specs=[row, pl.BlockSpec((1, 1), lambda i: (0, 0))],
                 scratch=[pltpu.VMEM((1, w), F32)], sem=("arbitrary",))(y, target)


def _rope_tables(positions, s):
    inv_freq = 1.0 / (ROPE_THETA ** (jnp.arange(0, MLA_ROPE, 2, dtype=F32) / MLA_ROPE))
    ang = positions.reshape(s, 1).astype(F32) * inv_freq
    cos, sin = jnp.cos(ang), jnp.sin(ang)
    one, zero = jnp.ones((s, MLA_NOPE), F32), jnp.zeros((s, MLA_NOPE), F32)
    cq = jnp.tile(jnp.concatenate([one, cos, cos], axis=1), (1, MLA_HEADS))
    sq = jnp.tile(jnp.concatenate([zero, sin, sin], axis=1), (1, MLA_HEADS))
    pad1, pad0 = jnp.ones((s, LANES - MLA_ROPE), F32), jnp.zeros((s, LANES - MLA_ROPE), F32)
    ct = jnp.concatenate([cos, cos, pad1], axis=1)
    st = jnp.concatenate([sin, sin, pad0], axis=1)
    half = MLA_ROPE // 2

    def swap(width, starts):
        r = np.zeros((width, width), np.float32)
        for o in starts:
            for i in range(half):
                r[o + half + i, o + i] = -1.0
                r[o + i, o + half + i] = 1.0
        return jnp.asarray(r)

    rq = swap(MLA_HEADS * MLA_QK, [h * MLA_QK + MLA_NOPE for h in range(MLA_HEADS)])
    rt = swap(LANES, [0])
    return (cq, sq, rq), (ct, st, rt)


def rope(x, tabs, *, name, backward=False, add=None):
    cos, sin, rot = tabs
    n, s, w = x.shape
    bs = _rows(s, w)

    def body(*refs):
        if add is None:
            x_ref, c_ref, s_ref, r_ref, o_ref = refs
        else:
            x_ref, c_ref, s_ref, r_ref, a_ref, o_ref = refs
        xv = x_ref[0]
        for j in range(1, n):
            xv = xv + x_ref[j]
        if backward:
            out = xv * c_ref[...] + _nt(xv * s_ref[...], r_ref[...], HIGHEST)
        else:
            out = xv * c_ref[...] + _nn(xv, r_ref[...], HIGHEST) * s_ref[...]
        if add is not None:
            out = out + a_ref[...]
        o_ref[...] = out

    row = pl.BlockSpec((bs, w), lambda i: (i, 0))
    ins = [x, cos, sin, rot] + ([] if add is None else [add])
    specs = [pl.BlockSpec((n, bs, w), lambda i: (0, i, 0)), row, row, pl.BlockSpec((w, w), lambda i: (0, 0))]
    specs += [] if add is None else [row]
    return _call(body, name=name, out_shape=jax.ShapeDtypeStruct((s, w), F32), grid=(s // bs,), in_specs=specs,
                 out_specs=row, sem=("parallel",))(*ins)


MESH = pl.DeviceIdType.MESH
HBM = pl.BlockSpec(memory_space=pltpu.HBM)


def _flip(v, bit):
    return 1 - v if bit else v


class Carried:
    def __init__(self, kind, arrays):
        assert kind in ('gather', 'scatter')
        self.kind, self.arrays, self.n = kind, list(arrays), len(arrays)

    @property
    def out_shape(self):
        lead = (N_DEV,) if self.kind == 'gather' else ()
        return [jax.ShapeDtypeStruct(lead + a.shape, a.dtype) for a in self.arrays]

    @property
    def scratch(self):
        return [pltpu.SemaphoreType.DMA((self.n, N_DEV - 1)), pltpu.SemaphoreType.DMA((self.n, N_DEV - 1)),
                pltpu.SemaphoreType.DMA((self.n,))]

    def _copies(self, in_refs, out_refs, sems):
        send_sems, recv_sems, local_sems = sems
        x, y, c = lax.axis_index("x"), lax.axis_index("y"), lax.axis_index("c")
        me = 4 * x + 2 * y + c
        part = (lambda b, p: in_refs[b]) if self.kind == 'gather' else (lambda b, p: in_refs[b].at[p])
        local = [pltpu.make_async_copy(part(b, me), out_refs[b].at[me], local_sems.at[b]) for b in range(self.n)]
        remote = []
        for k in range(1, N_DEV):
            px, py, pc = _flip(x, k & 4), _flip(y, k & 2), _flip(c, k & 1)
            for b in range(self.n):
                remote.append(pltpu.make_async_remote_copy(
                    src_ref=part(b, 4 * px + 2 * py + pc), dst_ref=out_refs[b].at[me], send_sem=send_sems.at[b, k - 1],
                    recv_sem=recv_sems.at[b, k - 1], device_id=(px, py, pc), device_id_type=MESH))
        return local, remote

    def start(self, in_refs, out_refs, sems):
        local, remote = self._copies(in_refs, out_refs, sems)
        for cp in local + remote:
            cp.start()

    def wait(self, in_refs, out_refs, sems):
        local, remote = self._copies(in_refs, out_refs, sems)
        for cp in remote:
            cp.wait_send()
            cp.wait_recv()
        for cp in local:
            cp.wait()


def _ride(carried, refs, n_in, n_out):
    n = 0 if carried is None else carried.n
    own_in, ride_in = refs[:n_in], refs[n_in:n_in + n]
    own_out, ride_out = refs[n_in + n:n_in + n + n_out], refs[n_in + n + n_out:n_in + 2 * n + n_out]
    return own_in, own_out, (ride_in, ride_out, refs[n_in + 2 * n + n_out:])


def _carry_call(body, carried, first, last, ins, *, out_shape, in_specs, out_specs, scratch=(), **kw):
    if carried is None:
        return _call(body, out_shape=out_shape, in_specs=in_specs, out_specs=out_specs, scratch=scratch, **kw)(*ins)
    n, n_in, n_out, n_scr = carried.n, len(in_specs), len(out_specs), len(scratch)

    def riding(*refs):
        own_in, ride_in = refs[:n_in], refs[n_in:n_in + n]
        rest = refs[n_in + n:]
        own_out, ride_out = rest[:n_out], rest[n_out:n_out + n]
        own_scr, sems = rest[n_out + n:n_out + n + n_scr], rest[n_out + n + n_scr:]
        pl.when(first())(lambda: carried.start(ride_in, ride_out, sems))
        body(*own_in, *own_out, *own_scr)
        pl.when(last())(lambda: carried.wait(ride_in, ride_out, sems))

    return _call(riding, out_shape=list(out_shape) + carried.out_shape, in_specs=list(in_specs) + [HBM] * n,
                 out_specs=list(out_specs) + [HBM] * n, scratch=list(scratch) + carried.scratch, **kw)(*ins, *carried.arrays)


def _tri(n, op):
    r = lax.broadcasted_iota(jnp.int32, (n, n), 0)
    c = lax.broadcasted_iota(jnp.int32, (n, n), 1)
    return r, c, op(r, c)


def _pair_split(x, first):
    zero = jnp.zeros_like(x)
    return jnp.where(first, x, zero), jnp.where(first, zero, x)


def _sb_specs(s, blk):
    npair = SB_WIDTH // LANES
    q = pl.BlockSpec((blk, LANES), lambda j, i: (i, j))
    k = pl.BlockSpec((s, LANES), lambda j, i: (0, npair + j))
    v = pl.BlockSpec((s, LANES), lambda j, i: (0, 2 * npair + j))
    full = pl.BlockSpec((s, LANES), lambda j, i: (0, j))
    return npair, q, k, v, full


def _stack_heads(x, first):
    return jnp.concatenate(_pair_split(x, first), axis=0)


def _unstack_heads(x, first, blk):
    return jnp.where(first, x[:blk], x[blk:])


def sb_fwd(qkv, *, name, carried=None):
    s = qkv.shape[0]
    blk = _tile(s, (ATT_BLK,))
    scale = SB_DIM ** -0.5
    npair, nq = SB_WIDTH // LANES, s // blk
    assert nq % SB_WIDE == 0

    def body(*refs):
        (q_ref, k_ref, v_ref), (y_ref,), ride = _ride(carried, refs, 3, 1)
        pair, qi = pl.program_id(0), pl.program_id(1)
        if carried is not None:
            @pl.when((pair == 0) & (qi == 0))
            def _():
                carried.start(*ride)

        first = lax.broadcasted_iota(jnp.int32, (blk, LANES), 1) < SB_DIM
        q2 = _stack_heads((q_ref[...].astype(F32) * scale).astype(MXU_DTYPE), first)
        row, col, later_mask = _tri(blk, lambda r, c: r > c)
        u_later = later_mask.astype(MXU_DTYPE)
        n = SB_WIDE

        def tile(kb, carry, masked):
            c, acc = carry
            keys = pl.ds(pl.multiple_of(kb * blk, blk), n * blk)
            z = _nt(q2, k_ref[keys, :])
            sp = _softplus_att(z)
            if masked:
                valid = _causal_mask(blk, n, qi, kb, 2, True)
            spm = jnp.where(valid, sp, 0.0) if masked else sp
            later = _cum(spm, u_later)
            sums = [jnp.sum(spm[:, t * blk:(t + 1) * blk], axis=1, keepdims=True) for t in range(n)]
            after, cols = c, [None] * n
            for t in reversed(range(n)):
                cols[t] = jnp.broadcast_to(after, (2 * blk, blk))
                after = after - sums[t]
            w = jnp.exp((z - sp) - later + (cols[0] if n == 1 else jnp.concatenate(cols, axis=1)))
            if masked:
                w = jnp.where(valid, w, 0.0)
            return after, acc + _nn(w.astype(MXU_DTYPE), v_ref[keys, :])

        zero = (jnp.zeros((2 * blk, 1), F32), jnp.zeros((2 * blk, LANES), F32))
        _, (c, acc) = _spent_loop(qi, tile, zero, SB_WIDE, lambda cr: jnp.max(cr[0]) >= SB_SPENT)
        y_ref[...] = _unstack_heads(acc, first, blk)
        if carried is not None:
            @pl.when((pair == npair - 1) & (qi == nq - 1))
            def _():
                carried.wait(*ride)

    _, qspec, kspec, vspec, _ = _sb_specs(s, blk)
    out = jax.ShapeDtypeStruct((s, SB_WIDTH), F32)
    extra = [] if carried is None else carried.arrays
    return _call(body, name=name, out_shape=[out] + ([] if carried is None else carried.out_shape),
                 grid=(npair, nq), in_specs=[qspec, kspec, vspec] + [HBM] * len(extra),
                 out_specs=[qspec] + [HBM] * len(extra), scratch=[] if carried is None else carried.scratch,
                 sem=("arbitrary", "arbitrary"))(qkv, qkv, qkv, *extra)


def sb_bwd(qkv, dy, *, name, carried=None):
    s = qkv.shape[0]
    blk = _tile(s, (ATT_BLK,))
    scale = SB_DIM ** -0.5
    npair, nq = SB_WIDTH // LANES, s // blk
    assert nq % SB_WIDE == 0

    def body(*refs):
        (q_ref, k_ref, v_ref, dy_ref), (dq_ref, dk_ref, dv_ref), ride = _ride(carried, refs, 4, 3)
        pair, qi = pl.program_id(0), pl.program_id(1)
        if carried is not None:
            @pl.when((pair == 0) & (qi == 0))
            def _():
                carried.start(*ride)

        @pl.when(qi == 0)
        def _():
            dk_ref[...] = jnp.zeros_like(dk_ref)
            dv_ref[...] = jnp.zeros_like(dv_ref)

        first = lax.broadcasted_iota(jnp.int32, (blk, LANES), 1) < SB_DIM
        q2 = _stack_heads((q_ref[...].astype(F32) * scale).astype(MXU_DTYPE), first)
        dy2 = _stack_heads(dy_ref[...].astype(MXU_DTYPE), first)
        row, col, incl_mask = _tri(blk, lambda r, c: r <= c)
        u_incl = incl_mask.astype(MXU_DTYPE)
        u_excl = (row < col).astype(MXU_DTYPE)
        n = SB_WIDE

        def walk(kb, c, masked):
            sp = _softplus_att(_nt(q2, k_ref[pl.ds(pl.multiple_of(kb * blk, blk), n * blk), :]))
            if masked:
                sp = jnp.where(_causal_mask(blk, n, qi, kb, 2, True), sp, 0.0)
            for t in reversed(range(n)):
                c = c - jnp.sum(sp[:, t * blk:(t + 1) * blk], axis=1, keepdims=True)
            return c

        start, tv = _spent_loop(qi, walk, jnp.zeros((2 * blk, 1), F32), SB_WIDE, lambda c: jnp.max(c) >= SB_SPENT)

        def prefixed(x, carry):
            cols = []
            for t in range(n):
                cols.append(jnp.broadcast_to(carry, (2 * blk, blk)))
                carry = carry + jnp.sum(x[:, t * blk:(t + 1) * blk], axis=1, keepdims=True)
            return (cols[0] if n == 1 else jnp.concatenate(cols, axis=1)), carry

        def tile(kb, carry, masked):
            p, gc, dq = carry
            keys = pl.ds(pl.multiple_of(kb * blk, blk), n * blk)
            kv = k_ref[keys, :]
            z = _nt(q2, kv)
            dw = _nt(dy2, v_ref[keys, :])
            sp = _softplus_att(z)
            if masked:
                valid = _causal_mask(blk, n, qi, kb, 2, True)
            spm = jnp.where(valid, sp, 0.0) if masked else sp
            before, p = prefixed(spm, p)
            w = jnp.exp((z - sp) + (_cum(spm, u_incl) + before))
            if masked:
                w = jnp.where(valid, w, 0.0)
            g = w * dw
            gbefore, gc = prefixed(g, gc)
            gb = g.astype(MXU_DTYPE)
            gin = _nn(jnp.concatenate([gb[:, t * blk:(t + 1) * blk] for t in range(n)], axis=0), u_excl)
            gex = gbefore + jnp.concatenate([gin[t * 2 * blk:(t + 1) * 2 * blk] for t in range(n)], axis=1)
            keep = jnp.exp(-spm)
            dz = keep * (g + gex) - gex
            if masked:
                dz = jnp.where(valid, dz, 0.0)
            dzb = dz.astype(MXU_DTYPE)
            dk_ref[keys, :] += _tn(dzb, q2)
            dv_ref[keys, :] += _tn(w.astype(MXU_DTYPE), dy2)
            return p, gc, dq + _nn(dzb, kv)

        zero = jnp.zeros((2 * blk, 1), F32)
        _, _, dq = _causal_loop(qi, tile, (tv, zero, jnp.zeros((2 * blk, LANES), F32)), False, SB_WIDE, first=start)
        dq_ref[...] = _unstack_heads(dq, first, blk) * scale
        if carried is not None:
            @pl.when((pair == npair - 1) & (qi == nq - 1))
            def _():
                carried.wait(*ride)

    _, qspec, kspec, vspec, full = _sb_specs(s, blk)
    out = jax.ShapeDtypeStruct((s, SB_WIDTH), F32)
    extra = [] if carried is None else carried.arrays
    return _call(body, name=name, out_shape=[out, out, out] + ([] if carried is None else carried.out_shape),
                 grid=(npair, nq), in_specs=[qspec, kspec, vspec, qspec] + [HBM] * len(extra),
                 out_specs=[qspec, full, full] + [HBM] * len(extra), scratch=[] if carried is None else carried.scratch,
                 sem=("arbitrary", "arbitrary"))(qkv, qkv, qkv, dy, *extra)


ATT_PAIR = 2


def _mla_specs(s, blk, dk, dv):
    q = pl.BlockSpec((ATT_PAIR, blk, dk), lambda hp, i: (hp, i, 0))
    k = pl.BlockSpec((ATT_PAIR, s, dk), lambda hp, i: (hp, 0, 0))
    v = pl.BlockSpec((ATT_PAIR, s, dv), lambda hp, i: (hp, 0, 0))
    y = pl.BlockSpec((ATT_PAIR, blk, dv), lambda hp, i: (hp, i, 0))
    lse = pl.BlockSpec((ATT_PAIR, blk, LANES), lambda hp, i: (hp, i, 0))
    return q, k, v, y, lse


def mla_fwd(q, k, v, *, name, carried=None):
    h, s, dk = q.shape
    dv = v.shape[-1]
    blk = _tile(s, (ATT_BLK,))
    scale = dk ** -0.5
    assert (s // blk) % MLA_WIDE == 0

    def body(q_ref, k_ref, v_ref, y_ref, l_ref):
        qi = pl.program_id(1)
        n = MLA_WIDE

        def tile(kb, carry, masked):
            keys = pl.ds(pl.multiple_of(kb * blk, blk), n * blk)
            out = []
            for hh in range(ATT_PAIR):
                m, l, acc = carry[hh]
                sc = _nt(q_ref[hh], k_ref[hh, keys, :]) * scale
                if masked:
                    sc = jnp.where(_causal_mask(blk, n, qi, kb, 1, False), sc, -1e30)
                m2 = jnp.maximum(m, jnp.max(sc, axis=1, keepdims=True))
                p = jnp.exp(sc - m2)
                a = jnp.exp(m - m2)
                out.append((m2, a * l + jnp.sum(p, axis=1, keepdims=True),
                            a * acc + _nn(p.astype(MXU_DTYPE), v_ref[hh, keys, :])))
            return tuple(out)

        init = (jnp.full((blk, 1), -1e30, F32), jnp.zeros((blk, 1), F32), jnp.zeros((blk, dv), F32))
        for hh, (m, l, acc) in enumerate(_causal_loop(qi, tile, (init,) * ATT_PAIR, False, MLA_WIDE)):
            y_ref[hh] = acc / l
            l_ref[hh] = jnp.broadcast_to(m + jnp.log(l), (blk, LANES))

    qspec, kspec, vspec, yspec, lspec = _mla_specs(s, blk, dk, dv)
    grid = (h // ATT_PAIR, s // blk)
    at = lambda a, b: (pl.program_id(0) == a) & (pl.program_id(1) == b)
    return _carry_call(body, carried, lambda: at(0, 0), lambda: at(grid[0] - 1, grid[1] - 1), (q, k, v), name=name,
                       out_shape=[jax.ShapeDtypeStruct((h, s, dv), F32), jax.ShapeDtypeStruct((h, s, LANES), F32)],
                       grid=grid, in_specs=[qspec, kspec, vspec], out_specs=[yspec, lspec],
                       sem=("arbitrary", "arbitrary"))


def mla_bwd(q, k, v, y, dy, lse, *, name):
    h, s, dk = q.shape
    dv = v.shape[-1]
    blk = _tile(s, (ATT_BLK,))
    scale = dk ** -0.5
    assert (s // blk) % MLA_WIDE == 0

    def body(q_ref, k_ref, v_ref, y_ref, dy_ref, l_ref, dq_ref, dk_ref, dv_ref):
        qi = pl.program_id(1)

        @pl.when(qi == 0)
        def _():
            dk_ref[...] = jnp.zeros_like(dk_ref)
            dv_ref[...] = jnp.zeros_like(dv_ref)

        as_row = lambda col: jnp.transpose(jnp.broadcast_to(col, (blk, LANES)))[0:1, :]
        dyv = [dy_ref[hh].astype(MXU_DTYPE) for hh in range(ATT_PAIR)]
        delta = [as_row(jnp.sum(dy_ref[hh] * y_ref[hh], axis=1, keepdims=True)) for hh in range(ATT_PAIR)]
        lv = [as_row(l_ref[hh, :, 0:1]) for hh in range(ATT_PAIR)]
        n = MLA_WIDE

        def tile(kb, dqs, masked):
            keys = pl.ds(pl.multiple_of(kb * blk, blk), n * blk)
            out = []
            for hh in range(ATT_PAIR):
                qv = q_ref[hh]
                kv = k_ref[hh, keys, :]
                p = jnp.exp(_nt(kv, qv) * scale - lv[hh])
                if masked:
                    p = jnp.where(_causal_mask(blk, n, qi, kb, 1, False, keys_on_rows=True), p, 0.0)
                ds = (p * (_nt(v_ref[hh, keys, :], dyv[hh]) - delta[hh])).astype(MXU_DTYPE)
                dk_ref[hh, keys, :] += _nn(ds, qv) * scale
                dv_ref[hh, keys, :] += _nn(p.astype(MXU_DTYPE), dyv[hh])
                out.append(dqs[hh] + _tn(ds, kv))
            return tuple(out)

        for hh, dq in enumerate(_causal_loop(qi, tile, (jnp.zeros((blk, dk), F32),) * ATT_PAIR, False, MLA_WIDE)):
            dq_ref[hh] = dq * scale

    qspec, kspec, vspec, yspec, lspec = _mla_specs(s, blk, dk, dv)
    return _call(body, name=name,
                 out_shape=[jax.ShapeDtypeStruct((h, s, dk), F32), jax.ShapeDtypeStruct((h, s, dk), F32),
                            jax.ShapeDtypeStruct((h, s, dv), F32)],
                 grid=(h // ATT_PAIR, s // blk), in_specs=[qspec, kspec, vspec, yspec, yspec, lspec],
                 out_specs=[qspec, kspec, vspec], sem=("parallel", "arbitrary"))(q, k, v, y, dy, lse)


HALO = 8
CONV_CHUNK = 16


def _conv_tiles(x):
    s, c = x.shape[-2:]
    return s, c, _tile(s, (ROW_BLK,)), _tile(c, (CONV_COLS,))


def _halo_rows(dtype):
    return HALO * 4 // jnp.dtype(dtype).itemsize


def _conv_specs(bs, cw, lead=(), dtype=F32):
    zero = (0,) * len(lead)
    hr = _halo_rows(dtype)
    blk = pl.BlockSpec(lead + (None, bs, cw), lambda p, j, i: zero + (p, i, j))
    halo = pl.BlockSpec(lead + (None, hr, cw), lambda p, j, i: zero + (p, jnp.maximum(i * (bs // hr) - 1, 0), j))
    w = lambda kk: pl.BlockSpec(lead + (None, kk, cw), lambda p, j, i: zero + (p, 0, j))
    return blk, halo, w


def _stage(scr, x_ref, halo_ref, first):
    hr = halo_ref.shape[0]
    scr[0:HALO, :] = jnp.where(first, 0.0, halo_ref[hr - HALO:hr, :].astype(F32))
    scr[HALO:, :] = x_ref[...].astype(F32)


def _shifted(ext, shift):
    return ext[HALO:] if shift == 0 else pltpu.roll(ext, shift, 0)[HALO:]


def _conv_taps(scr, kk, r0):
    ext = scr[pl.ds(r0, CONV_CHUNK + HALO), :]
    return [_shifted(ext, kk - 1 - k) for k in range(kk)]


def _conv_sum(taps, w_ref, b_ref):
    u = b_ref[...] + taps[0] * w_ref[0:1, :]
    for k in range(1, len(taps)):
        u = u + taps[k] * w_ref[k:k + 1, :]
    return u


def _fold(x):
    out = x[0:8]
    for r in range(8, CONV_CHUNK, 8):
        out = out + x[r:r + 8]
    return out


class _TapSums:
    def __init__(self, kk, cw):
        self.w = [jnp.zeros((8, cw), F32) for _ in range(kk)]
        self.b = jnp.zeros((8, cw), F32)

    def add(self, du, taps):
        self.w = [a + _fold(du * t) for a, t in zip(self.w, taps)]
        self.b = self.b + _fold(du)

    def flush(self, dw_ref, db_ref):
        for k, a in enumerate(self.w):
            dw_ref[k:k + 1, :] += jnp.sum(a, axis=0, keepdims=True)
        db_ref[...] += jnp.sum(self.b, axis=0, keepdims=True)


def _silu_grad(u):
    sg = _sigmoid(u)
    return sg * (1.0 + u * (1.0 - sg))


def conv_silu_fwd(x, w, b, *, name):
    s, c, bs, cw = _conv_tiles(x)
    kk = w.shape[1]

    def body(x_ref, h_ref, w_ref, b_ref, o_ref, scr):
        _stage(scr, x_ref, h_ref, pl.program_id(2) == 0)
        for r0 in range(0, bs, CONV_CHUNK):
            u = _conv_sum(_conv_taps(scr, kk, r0), w_ref, b_ref)
            o_ref[pl.ds(r0, CONV_CHUNK), :] = u * _sigmoid(u)

    blk, halo, wspec = _conv_specs(bs, cw, dtype=x.dtype)
    return _call(body, name=name, out_shape=jax.ShapeDtypeStruct(x.shape, F32), grid=(x.shape[0], c // cw, s // bs),
                 in_specs=[blk, halo, wspec(kk), wspec(1)], out_specs=blk, scratch=[pltpu.VMEM((bs + HALO, cw), F32)],
                 sem=("parallel", "parallel", "arbitrary"))(x, x, w, b)


def conv_silu_bwd(x, dy, w, b, *, name):
    s, c, bs, cw = _conv_tiles(x)
    kk = w.shape[1]

    def body(x_ref, h_ref, w_ref, b_ref, dy_ref, du_ref, dw_ref, db_ref, scr):
        i = pl.program_id(2)

        @pl.when(i == 0)
        def _():
            dw_ref[...] = jnp.zeros_like(dw_ref)
            db_ref[...] = jnp.zeros_like(db_ref)

        _stage(scr, x_ref, h_ref, i == 0)
        sums = _TapSums(kk, cw)
        for r0 in range(0, bs, CONV_CHUNK):
            taps = _conv_taps(scr, kk, r0)
            du = dy_ref[pl.ds(r0, CONV_CHUNK), :] * _silu_grad(_conv_sum(taps, w_ref, b_ref))
            du_ref[pl.ds(r0, CONV_CHUNK), :] = du
            sums.add(du, taps)
        sums.flush(dw_ref, db_ref)

    blk, halo, wspec = _conv_specs(bs, cw, dtype=x.dtype)
    return _call(body, name=name,
                 out_shape=[jax.ShapeDtypeStruct(x.shape, F32), jax.ShapeDtypeStruct(w.shape, F32),
                            jax.ShapeDtypeStruct(b.shape, F32)],
                 grid=(x.shape[0], c // cw, s // bs), in_specs=[blk, halo, wspec(kk), wspec(1), blk],
                 out_specs=[blk, wspec(kk), wspec(1)], scratch=[pltpu.VMEM((bs + HALO, cw), F32)],
                 sem=("parallel", "parallel", "arbitrary"))(x, x, w, b, dy)


def _glu_view(a):
    return a.reshape((2, a.shape[0] // 2) + a.shape[1:])


def conv_glu_fwd(x, w, b, *, name):
    s, c, bs, cw = _conv_tiles(x)
    kk = w.shape[1]
    half = x.shape[0] // 2

    def body(x_ref, h_ref, w_ref, b_ref, o_ref, gscr, vscr):
        first = pl.program_id(2) == 0
        _stage(gscr, x_ref.at[0], h_ref.at[0], first)
        _stage(vscr, x_ref.at[1], h_ref.at[1], first)
        for r0 in range(0, bs, CONV_CHUNK):
            gate = _conv_sum(_conv_taps(gscr, kk, r0), w_ref.at[0], b_ref.at[0])
            val = _conv_sum(_conv_taps(vscr, kk, r0), w_ref.at[1], b_ref.at[1])
            o_ref[pl.ds(r0, CONV_CHUNK), :] = (gate * _sigmoid(gate) * val).astype(o_ref.dtype)

    blk, halo, wspec = _conv_specs(bs, cw, lead=(2,), dtype=x.dtype)
    out, _, _ = _conv_specs(bs, cw)
    xv = _glu_view(x)
    return _call(body, name=name, out_shape=jax.ShapeDtypeStruct((half, s, c), MXU_DTYPE), grid=(half, c // cw, s // bs),
                 in_specs=[blk, halo, wspec(kk), wspec(1)], out_specs=out, scratch=[pltpu.VMEM((bs + HALO, cw), F32)] * 2,
                 sem=("parallel", "parallel", "arbitrary"))(xv, xv, _glu_view(w), _glu_view(b))


def conv_glu_bwd(x, da, w, b, *, name):
    s, c, bs, cw = _conv_tiles(x)
    kk = w.shape[1]
    half = x.shape[0] // 2

    def body(x_ref, h_ref, w_ref, b_ref, da_ref, du_ref, dw_ref, db_ref, gscr, vscr):
        i = pl.program_id(2)

        @pl.when(i == 0)
        def _():
            dw_ref[...] = jnp.zeros_like(dw_ref)
            db_ref[...] = jnp.zeros_like(db_ref)

        _stage(gscr, x_ref.at[0], h_ref.at[0], i == 0)
        _stage(vscr, x_ref.at[1], h_ref.at[1], i == 0)
        gsums, vsums = _TapSums(kk, cw), _TapSums(kk, cw)
        for r0 in range(0, bs, CONV_CHUNK):
            gtaps, vtaps = _conv_taps(gscr, kk, r0), _conv_taps(vscr, kk, r0)
            gate = _conv_sum(gtaps, w_ref.at[0], b_ref.at[0])
            val = _conv_sum(vtaps, w_ref.at[1], b_ref.at[1])
            dav = da_ref[pl.ds(r0, CONV_CHUNK), :]
            dgate = dav * val * _silu_grad(gate)
            dval = dav * gate * _sigmoid(gate)
            du_ref[0, pl.ds(r0, CONV_CHUNK), :] = dgate.astype(du_ref.dtype)
            du_ref[1, pl.ds(r0, CONV_CHUNK), :] = dval.astype(du_ref.dtype)
            gsums.add(dgate, gtaps)
            vsums.add(dval, vtaps)
        gsums.flush(dw_ref.at[0], db_ref.at[0])
        vsums.flush(dw_ref.at[1], db_ref.at[1])

    blk, halo, wspec = _conv_specs(bs, cw, lead=(2,), dtype=x.dtype)
    daspec, _, _ = _conv_specs(bs, cw)
    xv, wv, bv = _glu_view(x), _glu_view(w), _glu_view(b)
    du, dw, db = _call(body, name=name,
                       out_shape=[jax.ShapeDtypeStruct(xv.shape, x.dtype), jax.ShapeDtypeStruct(wv.shape, F32),
                                  jax.ShapeDtypeStruct(bv.shape, F32)],
                       grid=(half, c // cw, s // bs), in_specs=[blk, halo, wspec(kk), wspec(1), daspec],
                       out_specs=[blk, wspec(kk), wspec(1)], scratch=[pltpu.VMEM((bs + HALO, cw), F32)] * 2,
                       sem=("parallel", "parallel", "arbitrary"))(xv, xv, wv, bv, da)
    return du.reshape(x.shape), dw.reshape(w.shape), db.reshape(b.shape)


def conv_t(du, w, *, name, out_dtype=F32):
    s, c, bs, cw = _conv_tiles(du)
    kk = w.shape[1]
    nb = s // bs

    def body(d_ref, h_ref, w_ref, o_ref, scr):
        last = pl.program_id(2) == nb - 1
        scr[0:bs, :] = d_ref[...].astype(F32)
        scr[bs:, :] = jnp.where(last, 0.0, h_ref[0:HALO, :].astype(F32))
        for r0 in range(0, bs, CONV_CHUNK):
            ext = scr[pl.ds(r0, CONV_CHUNK + HALO), :]
            ahead = lambda j: ext[:CONV_CHUNK] if j == 0 else pltpu.roll(ext, CONV_CHUNK + HALO - j, 0)[:CONV_CHUNK]
            acc = ahead(kk - 1) * w_ref[0:1, :]
            for k in range(1, kk):
                acc = acc + ahead(kk - 1 - k) * w_ref[k:k + 1, :]
            o_ref[pl.ds(r0, CONV_CHUNK), :] = acc.astype(out_dtype)

    blk, _, wspec = _conv_specs(bs, cw)
    hr = _halo_rows(du.dtype)
    halo = pl.BlockSpec((None, hr, cw), lambda q, j, i: (q, jnp.minimum((i + 1) * (bs // hr), s // hr - 1), j))
    return _call(body, name=name, out_shape=jax.ShapeDtypeStruct(du.shape, out_dtype), grid=(du.shape[0], c // cw, nb),
                 in_specs=[blk, halo, wspec(kk)], out_specs=blk, scratch=[pltpu.VMEM((bs + HALO, cw), F32)],
                 sem=("parallel", "parallel", "arbitrary"))(du, du, w)


def _ssd_common(xbc_ref, tail_ref, dtrt_ref, bias_ref, biast_ref, alog_ref, alogt_ref):
    L = SSM_CHUNK
    raw = tail_ref[...] + bias_ref[...]
    dt = _softplus(raw)
    dtt = _softplus(dtrt_ref[...] + biast_ref[...])
    a = -jnp.exp(alog_ref[...])
    at = -jnp.exp(alogt_ref[...])
    row, col, lower = _tri(L, lambda r, c: r >= c)
    tril = lower.astype(F32)
    cs = _nn(tril, dt * a, HIGHEST)
    cst = _nt(dtt * at, tril, HIGHEST)
    bm = [xbc_ref[:, SSM_INNER + g * SSM_N: SSM_INNER + (g + 1) * SSM_N] for g in range(SSM_GROUPS)]
    off = SSM_INNER + SSM_GROUPS * SSM_N
    cm = [xbc_ref[:, off + g * SSM_N: off + (g + 1) * SSM_N] for g in range(SSM_GROUPS)]
    cb = [_bnt(cm[g], bm[g]) for g in range(SSM_GROUPS)]
    return raw, dt, a, lower, tril, cs, cst, bm, cm, cb


def _ssd_head(hh, xbc_ref, dt, cs, cst, lower):
    L = SSM_CHUNK
    ln = DT_LANE + hh
    x = xbc_ref[:, hh * SSM_P:(hh + 1) * SSM_P]
    dtc = dt[:, ln:ln + 1]
    csc = cs[:, ln:ln + 1]
    csr = cst[hh:hh + 1, :]
    decay = jnp.exp(jnp.where(lower, csc - csr, -1e30))
    last = cs[L - 1:L, ln:ln + 1]
    return x, dtc, csc, decay, jnp.exp(csc), jnp.exp(last - csc), jnp.exp(last)


def _ssd_inputs(tail, dt_bias, a_log, d_skip):
    H = SSM_HEADS
    lanes = lambda vec: jnp.pad(vec.reshape(1, H), ((0, 0), (DT_LANE, LANES - DT_LANE - H)))
    return (tail, tail[:, DT_LANE:DT_LANE + H].T, lanes(dt_bias), dt_bias.reshape(H, 1), lanes(a_log),
            a_log.reshape(H, 1), lanes(d_skip))


def ssd_fwd(xbc, tail, dt_bias, a_log, d_skip, *, name, carried=None):
    s = xbc.shape[0]
    L, H, P, N = SSM_CHUNK, SSM_HEADS, SSM_P, SSM_N
    nc = s // L

    def body(xbc_ref, tail_ref, dtrt_ref, bias_ref, biast_ref, alog_ref, alogt_ref, d_ref, y_ref, hp_ref, state):
        @pl.when(pl.program_id(0) == 0)
        def _():
            state[...] = jnp.zeros_like(state)

        raw, dt, a, lower, tril, cs, cst, bm, cm, cb = _ssd_common(
            xbc_ref, tail_ref, dtrt_ref, bias_ref, biast_ref, alog_ref, alogt_ref)
        for hh in range(H):
            g = hh // (H // SSM_GROUPS)
            x, dtc, csc, decay, e, tau, gamma = _ssd_head(hh, xbc_ref, dt, cs, cst, lower)
            xdt = x * dtc
            hprev = state[hh]
            hp_ref[hh] = hprev
            skip = d_ref[:, DT_LANE + hh:DT_LANE + hh + 1]
            y = _bnn(cb[g] * decay, xdt) + _bnn(cm[g], hprev) * e + x * skip
            y_ref[:, hh * P:(hh + 1) * P] = y
            state[hh] = hprev * gamma + _btn(bm[g] * tau, xdt)

    row = lambda w: pl.BlockSpec((L, w), lambda c: (c, 0))
    small = lambda shp: pl.BlockSpec(shp, lambda c: (0, 0))
    return _carry_call(
        body, carried, lambda: pl.program_id(0) == 0, lambda: pl.program_id(0) == nc - 1,
        (xbc, *_ssd_inputs(tail, dt_bias, a_log, d_skip)), name=name,
        out_shape=[jax.ShapeDtypeStruct((s, SSM_INNER), F32), jax.ShapeDtypeStruct((nc, H, N, P), F32)], grid=(nc,),
        in_specs=[row(SSM_CONV_DIM), row(LANES), pl.BlockSpec((H, L), lambda c: (0, c)), small((1, LANES)),
                  small((H, 1)), small((1, LANES)), small((H, 1)), small((1, LANES))],
        out_specs=[row(SSM_INNER), pl.BlockSpec((None, H, N, P), lambda c: (c, 0, 0, 0))],
        scratch=[pltpu.VMEM((H, N, P), F32)], sem=("arbitrary",))


def ssd_bwd(xbc, tail, dt_bias, a_log, d_skip, hprev_all, dy, *, name, carried=None):
    s = xbc.shape[0]
    L, H, P, N = SSM_CHUNK, SSM_HEADS, SSM_P, SSM_N
    nc = s // L
    hg = H // SSM_GROUPS

    def body(xbc_ref, tail_ref, dtrt_ref, bias_ref, biast_ref, alog_ref, alogt_ref, d_ref, hp_ref, dy_ref,
             dxbc_ref, ddt_ref, dbias_ref, dalog_ref, dd_ref, dstate):
        @pl.when(pl.program_id(0) == 0)
        def _():
            dstate[...] = jnp.zeros_like(dstate)
            dbias_ref[...] = jnp.zeros_like(dbias_ref)
            dalog_ref[...] = jnp.zeros_like(dalog_ref)
            dd_ref[...] = jnp.zeros_like(dd_ref)

        raw, dt, a, lower, tril, cs, cst, bm, cm, cb = _ssd_common(
            xbc_ref, tail_ref, dtrt_ref, bias_ref, biast_ref, alog_ref, alogt_ref)
        lane = lax.broadcasted_iota(jnp.int32, (L, LANES), 1)
        lane1 = lax.broadcasted_iota(jnp.int32, (1, LANES), 1)
        rowi = lax.broadcasted_iota(jnp.int32, (L, 1), 0)
        slot = lax.broadcasted_iota(jnp.int32, (LANES, L), 0)
        col_sums = jnp.zeros((LANES, L), F32)
        dcs_all = jnp.zeros((L, LANES), F32)
        ddt_x = jnp.zeros((L, LANES), F32)
        dd_row = jnp.zeros((1, LANES), F32)
        dbm = [jnp.zeros((L, N), F32) for _ in range(SSM_GROUPS)]
        dcm = [jnp.zeros((L, N), F32) for _ in range(SSM_GROUPS)]
        dcb = [jnp.zeros((L, L), F32) for _ in range(SSM_GROUPS)]
        for hh in range(H):
            g = hh // hg
            ln = DT_LANE + hh
            x, dtc, csc, decay, e, tau, gamma = _ssd_head(hh, xbc_ref, dt, cs, cst, lower)
            xdt = x * dtc
            hprev = hp_ref[hh]
            dhn = dstate[hh]
            dyh = dy_ref[:, hh * P:(hh + 1) * P]
            m = cb[g] * decay
            dxdt = _btn(m, dyh) + _bnn(bm[g] * tau, dhn)
            dm = jnp.where(lower, _bnt(dyh, xdt), 0.0)
            dcb[g] = dcb[g] + dm * decay
            dseg = dm * m
            dcs = jnp.sum(dseg, axis=1, keepdims=True)
            col_sums = jnp.where(slot == ln, jnp.sum(dseg, axis=0, keepdims=True), col_sums)
            edy = e * dyh
            dcm[g] = dcm[g] + _bnt(edy, hprev)
            dcs = dcs + e * jnp.sum(dyh * _bnn(cm[g], hprev), axis=1, keepdims=True)
            xdh = _bnt(xdt, dhn)
            dbm[g] = dbm[g] + tau * xdh
            dtau_tau = jnp.sum(bm[g] * xdh, axis=1, keepdims=True) * tau
            dlast = jnp.sum(dtau_tau, axis=0, keepdims=True) + gamma * jnp.sum(dhn * hprev, keepdims=True)
            dcs = dcs - dtau_tau + jnp.where(rowi == L - 1, dlast, 0.0)
            dstate[hh] = gamma * dhn + _btn(cm[g], edy)
            dcs_all = jnp.where(lane == ln, dcs, dcs_all)
            ddt_x = jnp.where(lane == ln, jnp.sum(dxdt * x, axis=1, keepdims=True), ddt_x)
            dxbc_ref[:, hh * P:(hh + 1) * P] = dxdt * dtc + d_ref[:, ln:ln + 1] * dyh
            dd_row = jnp.where(lane1 == ln, jnp.sum(dyh * x, keepdims=True), dd_row)
        off = SSM_INNER + SSM_GROUPS * SSM_N
        for g in range(SSM_GROUPS):
            dxbc_ref[:, SSM_INNER + g * N: SSM_INNER + (g + 1) * N] = dbm[g] + _btn(dcb[g], cm[g])
            dxbc_ref[:, off + g * N: off + (g + 1) * N] = dcm[g] + _bnn(dcb[g], bm[g])
        dcs_all = dcs_all - jnp.transpose(col_sums)
        dda = _tn(tril, dcs_all, HIGHEST)
        head_lane = (lane >= DT_LANE) & (lane < DT_LANE + H)
        draw = jnp.where(head_lane, (dda * a + ddt_x) * _sigmoid(raw), 0.0)
        ddt_ref[...] = draw
        dbias_ref[...] += jnp.sum(draw, axis=0, keepdims=True)
        dalog_ref[...] += jnp.sum(jnp.where(head_lane, dda * dt, 0.0), axis=0, keepdims=True) * a
        dd_ref[...] += dd_row

    rev = lambda c: nc - 1 - c
    row = lambda w: pl.BlockSpec((L, w), lambda c: (rev(c), 0))
    small = lambda shp: pl.BlockSpec(shp, lambda c: (0, 0))
    acc = pl.BlockSpec((1, LANES), lambda c: (0, 0))
    vec = jax.ShapeDtypeStruct((1, LANES), F32)
    return _carry_call(
        body, carried, lambda: pl.program_id(0) == 0, lambda: pl.program_id(0) == nc - 1,
        (xbc, *_ssd_inputs(tail, dt_bias, a_log, d_skip), hprev_all, dy), name=name,
        out_shape=[jax.ShapeDtypeStruct((s, SSM_CONV_DIM), F32), jax.ShapeDtypeStruct((s, LANES), F32), vec, vec, vec],
        grid=(nc,),
        in_specs=[row(SSM_CONV_DIM), row(LANES), pl.BlockSpec((H, L), lambda c: (0, rev(c))), small((1, LANES)),
                  small((H, 1)), small((1, LANES)), small((H, 1)), small((1, LANES)),
                  pl.BlockSpec((None, H, N, P), lambda c: (rev(c), 0, 0, 0)), row(SSM_INNER)],
        out_specs=[row(SSM_CONV_DIM), row(LANES), acc, acc, acc],
        scratch=[pltpu.VMEM((H, N, P), F32)], sem=("arbitrary",))


def _heads(x2d, n, d):
    s = x2d.shape[0]
    return x2d.reshape(s, n, d).transpose(1, 0, 2)


def _unheads(x3d):
    n, s, d = x3d.shape
    return x3d.transpose(1, 0, 2).reshape(s, n * d)


def layer_fwd(h, p, tabs, li, gather_late=None):
    s = h.shape[0]
    tabq, tabt = tabs
    nm = lambda t: f"L{li}_{t}"
    r = {'h': h}
    hn = rms_fwd(h, p['mix_norm'], name=nm('mixnorm'), out_dtype=MXU_DTYPE)
    proj = mm(hn, p['w_in'], name=nm('proj'))
    r.update(hn=hn, proj=proj)
    qkv = proj[:, :3 * SB_WIDTH].astype(MXU_DTYPE)
    riders = {} if gather_late is None else {
        'sb': ('w_out', 'ffn_conv_w'), 'ssd': ('ffn_w_down',), 'mla': ('ffn_w_up',)}
    ride = lambda k: Carried('gather', [gather_late[n] for n in riders[k]]) if riders else None
    got = {}
    ya, *rest = sb_fwd(qkv, name=nm('sb_fwd'), carried=ride('sb'))
    got.update(zip(riders.get('sb', ()), rest))
    yan = rms_fwd(ya, p['sb_out_norm'], name=nm('sbnorm'), out_dtype=MXU_DTYPE)
    r.update(qkv=qkv, ya=ya)
    z = proj[:, 768:1280]
    xbc = proj[None, :, 1280:2048]
    tail = proj[:, TAIL:TAIL + LANES]
    xbc_act = conv_silu_fwd(xbc, p['ssm_conv_w'], p['ssm_conv_b'], name=nm('ssmconv'))[0]
    y_ssm, hprev, *rest = ssd_fwd(xbc_act, tail, p['ssm_dt_bias'], p['ssm_a_log'], p['ssm_d'], name=nm('ssd_fwd'),
                                  carried=ride('ssd'))
    got.update(zip(riders.get('ssd', ()), rest))
    ybn = rms_fwd(y_ssm, p['ssm_out_norm'], name=nm('ssmnorm'), gate=z, out_dtype=MXU_DTYPE)
    r.update(z=z, xbc=xbc, tail=tail, xbc_act=xbc_act, y_ssm=y_ssm, hprev=hprev)
    cq = proj[:, 2048:2304]
    ckv = proj[:, 2304:2432]
    qn = rms_fwd(cq, p['mla_q_norm'], name=nm('qnorm'), out_dtype=MXU_DTYPE)
    q_r = rope(mm(qn, p['mla_w_uq'], name=nm('uq'))[None], tabq, name=nm('ropeq'))
    kvn = rms_fwd(ckv, p['mla_kv_norm'], name=nm('kvnorm'), out_dtype=MXU_DTYPE)
    kv = mm(kvn, p['mla_w_ukv'], name=nm('ukv'))
    k_pe = rope(tail[None], tabt, name=nm('ropek'))[:, :MLA_ROPE]
    qh = _heads(q_r, MLA_HEADS, MLA_QK).astype(MXU_DTYPE)
    kvh = _heads(kv, MLA_HEADS, MLA_NOPE + MLA_V)
    kh = jnp.concatenate([kvh[..., :MLA_NOPE], jnp.broadcast_to(k_pe[None], (MLA_HEADS, s, MLA_ROPE))],
                         axis=-1).astype(MXU_DTYPE)
    vh = kvh[..., MLA_NOPE:].astype(MXU_DTYPE)
    yc_h, lse, *rest = mla_fwd(qh, kh, vh, name=nm('mla_fwd'), carried=ride('mla'))
    got.update(zip(riders.get('mla', ()), rest))
    late = None
    if riders:
        late = assemble_late(got)
        p = dict(p, **{n: late[n][li] for n in LATE})
    yc = _unheads(yc_h)
    ycn = rms_fwd(yc, p['mla_out_norm'], name=nm('mlanorm'), out_dtype=MXU_DTYPE)
    r.update(cq=cq, ckv=ckv, qn=qn, kvn=kvn, qh=qh, kh=kh, vh=vh, yc_h=yc_h, yc=yc, lse=lse)
    ycat = jnp.concatenate([yan, ybn, ycn], axis=1)
    h1 = mm(ycat, p['w_out'], name=nm('outproj'), res=h)
    hn2 = rms_fwd(h1, p['ffn_norm'], name=nm('ffnnorm'), out_dtype=MXU_DTYPE)
    up = mm(hn2, p['ffn_w_up'], name=nm('up'), bb='o', out_dtype=MXU_DTYPE)
    act = conv_glu_fwd(up, p['ffn_conv_w'], p['ffn_conv_b'], name=nm('glu'))
    h2 = mm(act, p['ffn_w_down'], name=nm('down'), ab='k', bb='k', res=h1)
    r.update(ycat=ycat, h1=h1, hn2=hn2, up=up, act=act)
    return h2, r, p, late


def layer_bwd(dh2, p, r, tabs, li, scatter_late=None):
    s = dh2.shape[0]
    tabq, tabt = tabs
    nm = lambda t: f"L{li}_{t}"
    g = {}
    dact = mm(dh2, p['ffn_w_down'], name=nm('d_down_x'), tb=True, bb='o')
    g['ffn_w_down'] = mm(r['act'], dh2, name=nm('d_down_w'), out_dtype=WIRE_DTYPE, ta=True, ab='o')
    du, g['ffn_conv_w'], g['ffn_conv_b'] = conv_glu_bwd(r['up'], dact, p['ffn_conv_w'], p['ffn_conv_b'], name=nm('d_glu'))
    dup = conv_t(du, p['ffn_conv_w'], name=nm('d_ffnconv'), out_dtype=MXU_DTYPE)
    g['ffn_w_up'] = mm(r['hn2'], dup, name=nm('d_up_w'), out_dtype=WIRE_DTYPE, ta=True, bb='o')
    dhn2 = mm(dup, p['ffn_w_up'], name=nm('d_up_x'), tb=True, ab='k', bb='k')
    dh1, dg = rms_bwd(r['h1'], p['ffn_norm'], dhn2, name=nm('d_ffnnorm'), add=dh2)
    g['ffn_norm'] = dg[0]
    dycat = mm(dh1, p['w_out'], name=nm('d_out_x'), tb=True)
    g['w_out'] = mm(r['ycat'], dh1, name=nm('d_out_w'), out_dtype=WIRE_DTYPE, ta=True)
    dya, dg = rms_bwd(r['ya'], p['sb_out_norm'], dycat[:, :256], name=nm('d_sbnorm'))
    g['sb_out_norm'] = dg[0]
    riders = {} if scatter_late is None else {'sb': ('ffn_w_up',), 'ssd': ('ffn_w_down', 'w_out', 'ffn_conv_w')}
    parts = owner_parts_late([g] + list(scatter_late)) if riders else None
    ride = lambda k: Carried('scatter', [parts[n].astype(WIRE_DTYPE) for n in riders[k]]) if riders else None
    recv_late = {} if riders else None
    dq, dk, dv, *rest = sb_bwd(r['qkv'], dya, name=nm('sb_bwd'), carried=ride('sb'))
    if riders:
        recv_late.update(zip(riders['sb'], rest))
    dyssm, dz, dg = rms_bwd(r['y_ssm'], p['ssm_out_norm'], dycat[:, 256:768], name=nm('d_ssmnorm'), gate=r['z'])
    g['ssm_out_norm'] = dg[0]
    dxbc_act, ddt_tail, dbias, dalog, dd, *rest = ssd_bwd(r['xbc_act'], r['tail'], p['ssm_dt_bias'], p['ssm_a_log'],
                                                          p['ssm_d'], r['hprev'], dyssm, name=nm('ssd_bwd'),
                                                          carried=ride('ssd'))
    if riders:
        recv_late.update(zip(riders['ssd'], rest))
    hl = slice(DT_LANE, DT_LANE + SSM_HEADS)
    g['ssm_dt_bias'], g['ssm_a_log'], g['ssm_d'] = dbias[0, hl], dalog[0, hl], dd[0, hl]
    dxbc_u, g['ssm_conv_w'], g['ssm_conv_b'] = conv_silu_bwd(r['xbc'], dxbc_act[None], p['ssm_conv_w'], p['ssm_conv_b'],
                                                             name=nm('d_ssmact'))
    dxbc = conv_t(dxbc_u, p['ssm_conv_w'], name=nm('d_ssmconv'))[0]
    dyc, dg = rms_bwd(r['yc'], p['mla_out_norm'], dycat[:, 768:], name=nm('d_mlanorm'))
    g['mla_out_norm'] = dg[0]
    dqh, dkh, dvh = mla_bwd(r['qh'], r['kh'], r['vh'], r['yc_h'], _heads(dyc, MLA_HEADS, MLA_V), r['lse'], name=nm('mla_bwd'))
    dq_c = rope(_unheads(dqh)[None], tabq, name=nm('d_ropeq'), backward=True)
    g['mla_w_uq'] = mm(r['qn'], dq_c, name=nm('d_uq_w'), out_dtype=WIRE_DTYPE, ta=True)
    dcq, dg = rms_bwd(r['cq'], p['mla_q_norm'], mm(dq_c, p['mla_w_uq'], name=nm('d_uq_x'), tb=True), name=nm('d_qnorm'))
    g['mla_q_norm'] = dg[0]
    dkv = _unheads(jnp.concatenate([dkh[..., :MLA_NOPE], dvh], axis=-1))
    g['mla_w_ukv'] = mm(r['kvn'], dkv, name=nm('d_ukv_w'), out_dtype=WIRE_DTYPE, ta=True)
    dckv, dg = rms_bwd(r['ckv'], p['mla_kv_norm'], mm(dkv, p['mla_w_ukv'], name=nm('d_ukv_x'), tb=True), name=nm('d_kvnorm'))
    g['mla_kv_norm'] = dg[0]
    dkpe = jnp.pad(dkh[..., MLA_NOPE:], ((0, 0), (0, 0), (0, LANES - MLA_ROPE)))
    dtail = rope(dkpe, tabt, name=nm('d_ropek'), backward=True, add=ddt_tail)
    dproj = jnp.concatenate([dq, dk, dv, dz, dxbc, dcq, dckv, dtail], axis=1).astype(MXU_DTYPE)
    g['w_in'] = mm(r['hn'], dproj, name=nm('d_proj_w'), out_dtype=WIRE_DTYPE, ta=True)
    dhn = mm(dproj, p['w_in'], name=nm('d_proj_x'), tb=True)
    dh, dg = rms_bwd(r['h'], p['mix_norm'], dhn, name=nm('d_mixnorm'), add=dh1)
    g['mix_norm'] = dg[0]
    return dh, g, recv_late


def _w_in_placement():
    c = np.arange(D_IN)
    dest = np.where(c < 2048, c, np.where(c < 2056, c + (D_IN - 2056), c - 8))
    dest = jnp.asarray(dest.reshape(N_DEV, D_IN // N_DEV, 1), jnp.int32)
    return (dest == jnp.arange(D_IN_PAD, dtype=jnp.int32)).astype(MXU_DTYPE)


def _owner_major(full, axis):
    shp = full.shape
    return jnp.moveaxis(full.reshape(shp[:axis] + (N_DEV, shp[axis] // N_DEV) + shp[axis + 1:]), axis, 0)


def _owner_join(parts, axis):
    moved = jnp.moveaxis(parts, 0, axis)
    shp = moved.shape
    return moved.reshape(shp[:axis] + (shp[axis] * shp[axis + 1],) + shp[axis + 2:])


def assemble_early(gathered, replicated):
    L = DEPTH
    out = dict(replicated)
    out['w_in'] = mm(gathered['w_in'].reshape(N_DEV, L * D_MODEL, D_IN // N_DEV), _w_in_placement(), name='place_w_in',
                     ab='k', bb='k', out_dtype=MXU_DTYPE).reshape(L, D_MODEL, D_IN_PAD)
    out['mla_w_uq'] = _owner_join(gathered['mla_w_uq'], 2)
    out['mla_w_ukv'] = _owner_join(gathered['mla_w_ukv'], 2)
    out['ssm_conv_w'] = _owner_join(gathered['ssm_conv_w'], 2)[:, None]
    out['ssm_conv_b'] = replicated['ssm_conv_b'].reshape(L, 1, 1, SSM_CONV_DIM)
    out['ffn_conv_b'] = replicated['ffn_conv_b'].reshape(L, N_DEV, 1, FF_SHARD)
    return out


def assemble_late(gathered):
    L = DEPTH
    return {'ffn_w_up': jnp.moveaxis(gathered['ffn_w_up'], 1, 0),
            'w_out': _owner_join(gathered['w_out'], 1),
            'ffn_w_down': _owner_join(gathered['ffn_w_down'], 1).reshape(L, N_DEV // 2, FF_SHARD, D_MODEL),
            'ffn_conv_w': jnp.moveaxis(gathered['ffn_conv_w'], 1, 0)}


def owner_parts_late(grads):
    L = DEPTH
    st = lambda n: jnp.stack([g[n] for g in grads])
    return {'ffn_w_up': jnp.moveaxis(st('ffn_w_up'), 1, 0),
            'w_out': _owner_major(st('w_out'), 1),
            'ffn_w_down': _owner_major(st('ffn_w_down').reshape(L, D_FF, D_MODEL), 1),
            'ffn_conv_w': jnp.moveaxis(st('ffn_conv_w'), 1, 0)}


def owner_parts_early(grads):
    L = DEPTH
    st = lambda n: jnp.stack([g[n] for g in grads])
    parts = {
        'w_in': mm(st('w_in').reshape(L * D_MODEL, D_IN_PAD), _w_in_placement(), name='unplace_w_in', tb=True, bb='o',
                   out_dtype=WIRE_DTYPE).reshape(N_DEV, L, D_MODEL, D_IN // N_DEV),
        'mla_w_uq': _owner_major(st('mla_w_uq'), 2),
        'mla_w_ukv': _owner_major(st('mla_w_ukv'), 2),
        'ssm_conv_w': _owner_major(st('ssm_conv_w')[:, 0], 2),
    }
    rep = {n: st(n) for n in REPLICATED if n not in ('final_norm', 'ssm_conv_b', 'ffn_conv_b')}
    rep['ssm_conv_b'] = st('ssm_conv_b').reshape(L, SSM_CONV_DIM)
    rep['ffn_conv_b'] = st('ffn_conv_b').reshape(L, 2 * D_FF)
    return parts, rep


def local_step(x, positions, target, early, late_shards, replicated):
    s = x.shape[0]
    tabs = _rope_tables(positions, s)
    params = assemble_early(early, replicated)
    layer = lambda li: {n: params[n][li] for n in params if n != 'final_norm'}
    h, r0, p0, late = layer_fwd(x, layer(0), tabs, 0, gather_late=late_shards)
    saved = [(p0, r0)]
    for li in range(1, DEPTH):
        h, r, p, _ = layer_fwd(h, dict(layer(li), **{n: late[n][li] for n in LATE}), tabs, li)
        saved.append((p, r))
    y = rms_fwd(h, params['final_norm'], name='finalnorm')
    dy, loss = loss_head(y, target, name='loss')
    dh, dg = rms_bwd(h, params['final_norm'], dy, name='d_finalnorm')
    above = []
    for li in reversed(range(1, DEPTH)):
        dh, g, _ = layer_bwd(dh, *saved[li], tabs, li)
        above.insert(0, g)
    dh, g0, recv_late = layer_bwd(dh, *saved[0], tabs, 0, scatter_late=above)
    parts, rep = owner_parts_early([g0] + above)
    rep['final_norm'] = dg[0]
    return loss[0, 0], dh, parts, recv_late, rep


def all_gather(blocks, *, name):
    n = len(blocks)

    def body(*refs):
        x_refs, out_refs = refs[:n], refs[n:2 * n]
        send_sems, recv_sems, local_sems = refs[2 * n:]
        x, y, c = lax.axis_index("x"), lax.axis_index("y"), lax.axis_index("c")
        me, sibling = (x, y, c), (x, y, 1 - c)
        chips = [(1 - x, y), (x, 1 - y), (1 - x, 1 - y)]

        def slot(b, px, py, pc):
            return out_refs[b].at[4 * px + 2 * py + pc]

        def copy(b, k, blk, to, src=None):
            return pltpu.make_async_remote_copy(src_ref=slot(b, *blk) if src is None else src, dst_ref=slot(b, *blk),
                                                send_sem=send_sems.at[b, k], recv_sem=recv_sems.at[b, k],
                                                device_id=to, device_id_type=MESH)

        mine = [pltpu.make_async_copy(x_refs[b], slot(b, *me), local_sems.at[b]) for b in range(n)]
        for cp in mine:
            cp.start()
        first = []
        for b in range(n):
            first.append(copy(b, 0, me, sibling, src=x_refs[b]))
            first += [copy(b, 1 + j, me, (*chip, c), src=x_refs[b]) for j, chip in enumerate(chips)]
        for cp in first:
            cp.start()
        passed = []
        for j, chip in enumerate(chips):
            for b in range(n):
                copy(b, 1 + j, (*chip, c), me).wait_recv()
                fwd = copy(b, 4 + j, (*chip, c), sibling)
                fwd.start()
                passed.append(fwd)
        for b in range(n):
            copy(b, 0, sibling, me).wait_recv()
            for j, chip in enumerate(chips):
                copy(b, 4 + j, (*chip, 1 - c), me).wait_recv()
        for cp in first + passed:
            cp.wait_send()
        for cp in mine:
            cp.wait()

    return pl.pallas_call(
        body, name=name, out_shape=[jax.ShapeDtypeStruct((N_DEV,) + b.shape, b.dtype) for b in blocks],
        in_specs=[HBM] * n, out_specs=[HBM] * n,
        scratch_shapes=[pltpu.SemaphoreType.DMA((n, 7)), pltpu.SemaphoreType.DMA((n, 7)), pltpu.SemaphoreType.DMA((n,))],
    )(*blocks)


def all_to_all(parts, *, name):
    n = len(parts)

    def body(*refs):
        g_refs, r_refs = refs[:n], refs[n:2 * n]
        send_sems, recv_sems, local_sems = refs[2 * n:]
        x, y, c = lax.axis_index("x"), lax.axis_index("y"), lax.axis_index("c")
        me = 4 * x + 2 * y + c
        mine = [pltpu.make_async_copy(g_refs[b].at[me], r_refs[b].at[me], local_sems.at[b]) for b in range(n)]
        for cp in mine:
            cp.start()
        copies = []
        for k in range(1, N_DEV):
            px, py, pc = _flip(x, k & 4), _flip(y, k & 2), _flip(c, k & 1)
            peer = 4 * px + 2 * py + pc
            for b in range(n):
                cp = pltpu.make_async_remote_copy(src_ref=g_refs[b].at[peer], dst_ref=r_refs[b].at[me],
                                                  send_sem=send_sems.at[b, k - 1], recv_sem=recv_sems.at[b, k - 1],
                                                  device_id=(px, py, pc), device_id_type=MESH)
                cp.start()
                copies.append(cp)
        for cp in copies:
            cp.wait_send()
            cp.wait_recv()
        for cp in mine:
            cp.wait()

    return pl.pallas_call(
        body, name=name, out_shape=[jax.ShapeDtypeStruct(p.shape, p.dtype) for p in parts],
        in_specs=[HBM] * n, out_specs=[HBM] * n,
        scratch_shapes=[pltpu.SemaphoreType.DMA((n, 7)), pltpu.SemaphoreType.DMA((n, 7)), pltpu.SemaphoreType.DMA((n,))],
    )(*parts)


def adamw(parts, w, m, v, *, name):
    r, wd = w.shape
    br = _tile(r, (256, 128, 64, 32, 16, 8))
    c1 = 1.0 - ADAM_B1 ** ADAM_STEP
    c2 = 1.0 - ADAM_B2 ** ADAM_STEP

    def body(p_ref, w_ref, m_ref, v_ref, g_ref, d_ref, mo_ref, vo_ref):
        g = p_ref[0].astype(F32)
        for j in range(1, N_DEV):
            g = g + p_ref[j].astype(F32)
        mn = ADAM_B1 * m_ref[...] + (1.0 - ADAM_B1) * g
        vn = ADAM_B2 * v_ref[...] + (1.0 - ADAM_B2) * (g * g)
        g_ref[...] = g
        mo_ref[...] = mn
        vo_ref[...] = vn
        d_ref[...] = -ADAM_LR * ((mn / c1) / (jnp.sqrt(vn / c2) + ADAM_EPS) + ADAM_WD * w_ref[...])

    blk = pl.BlockSpec((br, wd), lambda i: (i, 0))
    out = jax.ShapeDtypeStruct((r, wd), F32)
    return _call(body, name=name, out_shape=[out] * 4, grid=(r // br,),
                 in_specs=[pl.BlockSpec((N_DEV, br, wd), lambda i: (0, i, 0)), blk, blk, blk], out_specs=[blk] * 4,
                 sem=("parallel",))(parts, w, m, v)


def _pack(arrs):
    flat = jnp.concatenate([a.reshape(-1) for a in arrs])
    rows = -(-flat.shape[0] // (8 * FLAT_W)) * 8
    return jnp.pad(flat, (0, rows * FLAT_W - flat.shape[0])).reshape(rows, FLAT_W)


def _unpack(flat, shapes):
    flat = flat.reshape(-1)
    out, off = [], 0
    for shp in shapes:
        n = int(np.prod(shp))
        out.append(flat[off:off + n].reshape(shp))
        off += n
    return out


def kernel(x, positions, mix_norm, w_in, sb_out_norm, ssm_conv_w, ssm_conv_b, ssm_dt_bias, ssm_a_log, ssm_d, ssm_out_norm, mla_q_norm, mla_w_uq, mla_kv_norm, mla_w_ukv, mla_out_norm, w_out, ffn_norm, ffn_w_up, ffn_conv_w, ffn_conv_b, ffn_w_down, final_norm, loss_target, m_mix_norm, m_w_in, m_sb_out_norm, m_ssm_conv_w, m_ssm_conv_b, m_ssm_dt_bias, m_ssm_a_log, m_ssm_d, m_ssm_out_norm, m_mla_q_norm, m_mla_w_uq, m_mla_kv_norm, m_mla_w_ukv, m_mla_out_norm, m_w_out, m_ffn_norm, m_ffn_w_up, m_ffn_conv_w, m_ffn_conv_b, m_ffn_w_down, m_final_norm, v_mix_norm, v_w_in, v_sb_out_norm, v_ssm_conv_w, v_ssm_conv_b, v_ssm_dt_bias, v_ssm_a_log, v_ssm_d, v_ssm_out_norm, v_mla_q_norm, v_mla_w_uq, v_mla_kv_norm, v_mla_w_ukv, v_mla_out_norm, v_w_out, v_ffn_norm, v_ffn_w_up, v_ffn_conv_w, v_ffn_conv_b, v_ffn_w_down, v_final_norm):
    args = locals()
    w = {n: args[n] for n in WEIGHTS}
    m = {n: args['m_' + n] for n in WEIGHTS}
    v = {n: args['v_' + n] for n in WEIGHTS}
    wire = lambda n: w[n] if n in VPU_WEIGHTS else w[n].astype(BF16)
    early = dict(zip(EARLY, all_gather([wire(n) for n in EARLY], name='gather_early')))

    loss, dx, parts, recv, rep = local_step(x[0], positions[0], loss_target[0], early, {n: wire(n) for n in LATE},
                                            {n: w[n] for n in REPLICATED})
    loss = lax.psum(loss, ("x", "y", "c"))

    recv.update(zip(EARLY, all_to_all([parts[n].astype(WIRE_DTYPE) for n in EARLY], name='scatter_early')))
    res = {kind: {} for kind in 'gdmv'}
    for n, rv in recv.items():
        shp = w[n].shape
        two_d = (int(np.prod(shp[:-1])), shp[-1])
        outs = adamw(rv.reshape((N_DEV,) + two_d), w[n].reshape(two_d), m[n].reshape(two_d), v[n].reshape(two_d),
                     name='adamw_' + n)
        for kind, o in zip('gdmv', outs):
            res[kind][n] = o.reshape(shp)

    rep_shapes = [w[n].shape for n in REPLICATED]
    (rparts,) = all_gather([_pack([rep[n] for n in REPLICATED])], name='gather_small_grads')
    rflat = lambda d: _pack([d[n] for n in REPLICATED])
    routs = adamw(rparts, rflat(w), rflat(m), rflat(v), name='adamw_replicated')
    for kind, o in zip('gdmv', routs):
        res[kind].update(zip(REPLICATED, _unpack(o, rep_shapes)))

    return (loss, dx[None], *[res['g'][n] for n in WEIGHTS], *[res['d'][n] for n in WEIGHTS],
            *[res['m'][n] for n in WEIGHTS], *[res['v'][n] for n in WEIGHTS])
```

```python
import numpy as np
import jax
import jax.numpy as jnp
from jax import lax
from jax.experimental import pallas as pl
from jax.experimental.pallas import tpu as pltpu

F32 = jnp.float32
BF16 = jnp.bfloat16
MXU_DTYPE = jnp.bfloat16
HIGHEST = lax.Precision.HIGHEST
WIRE_DTYPE = jnp.bfloat16

N_DEV = 8
D_MODEL = 1024
DEPTH = 2
EPS = 1e-6
SB_HEADS, SB_DIM = 4, 64
SB_WIDTH = SB_HEADS * SB_DIM
SSM_HEADS, SSM_P, SSM_GROUPS, SSM_N, SSM_CONV, SSM_CHUNK = 8, 64, 2, 64, 4, 128
SSM_INNER = SSM_HEADS * SSM_P
SSM_CONV_DIM = SSM_INNER + 2 * SSM_GROUPS * SSM_N
MLA_HEADS, MLA_NOPE, MLA_ROPE, MLA_V, MLA_Q_RANK, MLA_KV_RANK = 4, 64, 32, 64, 256, 128
MLA_QK = MLA_NOPE + MLA_ROPE
ROPE_THETA = 10000.0
D_IN = 2472
D_IN_PAD = 2560
TAIL = 2432
DT_LANE = 32
D_FF = 2816
FF_SHARD = 2 * D_FF // N_DEV
ADAM_LR, ADAM_B1, ADAM_B2, ADAM_EPS, ADAM_WD, ADAM_STEP = 0.001, 0.9, 0.999, 1e-08, 0.01, 10

LANES = 128
ATT_BLK = 256
SB_WIDE = 2
SB_SPENT = -110.0
MLA_WIDE = 4
ROW_BLK = 512
ROW_BLOCK_BYTES = 2 << 20
CONV_COLS = 256
FLAT_W = 1024
VMEM_LIMIT = 56 << 20
MM_TM = (1024, 512, 256, 128)
MM_TN = (1280, 1024, 768, 640, 512, 384, 256, 128)
MM_TK = (1280, 1024, 512, 256, 128)

WEIGHTS = ['mix_norm', 'w_in', 'sb_out_norm', 'ssm_conv_w', 'ssm_conv_b', 'ssm_dt_bias', 'ssm_a_log', 'ssm_d',
           'ssm_out_norm', 'mla_q_norm', 'mla_w_uq', 'mla_kv_norm', 'mla_w_ukv', 'mla_out_norm', 'w_out',
           'ffn_norm', 'ffn_w_up', 'ffn_conv_w', 'ffn_conv_b', 'ffn_w_down', 'final_norm']
SHARDED = {'w_in': 2, 'ssm_conv_w': 2, 'mla_w_uq': 2, 'mla_w_ukv': 2, 'w_out': 1, 'ffn_w_up': 2, 'ffn_conv_w': 2,
           'ffn_w_down': 1}
VPU_WEIGHTS = ('ssm_conv_w', 'ffn_conv_w')
EARLY = ('w_in', 'mla_w_uq', 'mla_w_ukv', 'ssm_conv_w')
LATE = ('w_out', 'ffn_w_up', 'ffn_conv_w', 'ffn_w_down')
REPLICATED = [n for n in WEIGHTS if n not in SHARDED]


def _call(body, *, name, out_shape, grid=(), in_specs=None, out_specs=None, scratch=(), sem=None, **kw):
    params = dict(vmem_limit_bytes=VMEM_LIMIT)
    if sem is not None:
        params['dimension_semantics'] = sem
    return pl.pallas_call(body, name=name, out_shape=out_shape, grid=grid, in_specs=in_specs, out_specs=out_specs,
                          scratch_shapes=list(scratch), compiler_params=pltpu.CompilerParams(**params), **kw)


def _tile(n, prefs):
    for t in prefs:
        if n % t == 0:
            return t
    return n


def _rows(s, w):
    rows = ROW_BLK
    while rows * 2 <= s and s % (rows * 2) == 0 and rows * 2 * w * 4 <= ROW_BLOCK_BYTES:
        rows *= 2
    return _tile(s, (rows,))


def _dot(a, b, dims, precision=None):
    return lax.dot_general(a, b, (dims, ((), ())), preferred_element_type=F32, precision=precision)


def _nn(a, b, precision=None):
    return _dot(a, b, ((1,), (0,)), precision)


def _nt(a, b, precision=None):
    return _dot(a, b, ((1,), (1,)), precision)


def _tn(a, b, precision=None):
    return _dot(a, b, ((0,), (0,)), precision)


def _mxu(f):
    return lambda a, b: f(a.astype(MXU_DTYPE), b.astype(MXU_DTYPE))


_bnn, _bnt, _btn = _mxu(_nn), _mxu(_nt), _mxu(_tn)


def _split2(x):
    hi = x.astype(MXU_DTYPE)
    lo = (x - hi.astype(F32)).astype(MXU_DTYPE)
    return hi, lo


def _sigmoid(x):
    return 0.5 * jnp.tanh(0.5 * x) + 0.5


def _softplus(x):
    return jnp.maximum(x, 0.0) + jnp.log1p(jnp.exp(-jnp.abs(x)))


def _softplus_att(x):
    return jnp.maximum(x, 0.0) + jnp.log(1.0 + jnp.exp(-jnp.abs(x)))


def _cum(x, u):
    rows, b = x.shape[0], u.shape[0]
    n = x.shape[1] // b
    hi, lo = _split2(x)
    stack = [part[:, t * b:(t + 1) * b] for part in (hi, lo) for t in range(n)]
    r = _nn(jnp.concatenate(stack, axis=0), u)
    return jnp.concatenate([r[t * rows:(t + 1) * rows] + r[(n + t) * rows:(n + t + 1) * rows] for t in range(n)], axis=1)


def _diagonal_group(qi, tile, carry, width):
    base = (qi // width) * width
    return lax.switch(qi - base, [lambda c, n=n: tile(base, c, True, n) for n in range(1, width + 1)], carry)


def _causal_loop(qi, tile, carry, width, first=0):
    carry = lax.fori_loop(first, qi // width, lambda i, c: tile(i * width, c, False, width), carry)
    return _diagonal_group(qi, tile, carry, width)


def _spent_loop(qi, tile, carry, width, live):
    carry = _diagonal_group(qi, tile, carry, width)
    step = lambda st: (st[0] - 1, tile((st[0] - 1) * width, st[1], False, width))
    return lax.while_loop(lambda st: (st[0] > 0) & live(st[1]), step, (qi // width, carry))


def _causal_mask(blk, width, qi, kb, heads, strict, keys_on_rows=False):
    shape = (width * blk, blk) if keys_on_rows else (heads * blk, width * blk)
    q_idx = lax.broadcasted_iota(jnp.int32, shape, 1 if keys_on_rows else 0)
    k_idx = lax.broadcasted_iota(jnp.int32, shape, 0 if keys_on_rows else 1)
    if heads > 1:
        q_idx = q_idx % blk
    gap = (qi - kb) * blk
    return k_idx < q_idx + gap if strict else k_idx <= q_idx + gap


def mm(a, b, *, name, ta=False, tb=False, res=None, out_dtype=F32, ab=None, bb=None, precision=None):
    a2, b2 = a.shape[-2:], b.shape[-2:]
    (kdim, m) = a2 if ta else a2[::-1]
    (n, k2) = b2 if tb else b2[::-1]
    assert kdim == k2, (a.shape, b.shape, ta, tb)
    assert (ab == 'k') == (bb == 'k')
    kb = ab == 'k'
    nb = a.shape[0] if ab == 'o' else (b.shape[0] if bb == 'o' else None)
    tm, tn = _tile(m, MM_TM), _tile(n, MM_TN)
    tk = kdim if kb else _tile(kdim, MM_TK)
    nk = a.shape[0] if kb else kdim // tk
    dims = ((0 if ta else 1,), (1 if tb else 0,))
    op_dtype = F32 if precision is not None else MXU_DTYPE

    def body(*refs):
        a_ref, b_ref = refs[0], refs[1]
        r_ref = refs[2] if res is not None else None
        o_ref = refs[3] if res is not None else refs[2]
        part = _dot(a_ref[...].astype(op_dtype), b_ref[...].astype(op_dtype), dims, precision)

        def finish(out):
            if res is not None:
                out = out + r_ref[...]
            o_ref[...] = out.astype(out_dtype)

        if nk == 1:
            finish(part)
            return
        acc = refs[-1]
        k = pl.program_id(3)

        @pl.when(k == 0)
        def _():
            acc[...] = part

        @pl.when(k > 0)
        def _():
            acc[...] += part

        @pl.when(k == nk - 1)
        def _():
            finish(acc[...])

    def spec(blk, idx, how):
        if how is None:
            return pl.BlockSpec(blk, idx)
        if how == 'o':
            return pl.BlockSpec((None,) + blk, lambda p, i, j, k: (p,) + idx(p, i, j, k))
        return pl.BlockSpec((None,) + blk, lambda p, i, j, k: (k,) + idx(p, i, j, 0))

    a_spec = spec((tk, tm), lambda p, i, j, k: (k, i), ab) if ta else spec((tm, tk), lambda p, i, j, k: (i, k), ab)
    b_spec = spec((tn, tk), lambda p, i, j, k: (j, k), bb) if tb else spec((tk, tn), lambda p, i, j, k: (k, j), bb)
    o_spec = spec((tm, tn), lambda p, i, j, k: (i, j), None if nb is None else 'o')
    ins, specs = [a, b], [a_spec, b_spec]
    if res is not None:
        ins.append(res)
        specs.append(o_spec)
    out_shape = (m, n) if nb is None else (nb, m, n)
    return _call(body, name=name, out_shape=jax.ShapeDtypeStruct(out_shape, out_dtype),
                 grid=(1 if nb is None else nb, m // tm, n // tn, nk), in_specs=specs, out_specs=o_spec,
                 scratch=[] if nk == 1 else [pltpu.VMEM((tm, tn), F32)],
                 sem=("parallel", "parallel", "parallel", "arbitrary"))(*ins)


def rms_fwd(x, g, *, name, gate=None, out_dtype=F32):
    s, w = x.shape
    bs = _rows(s, w)

    def body(*refs):
        if gate is None:
            x_ref, g_ref, o_ref = refs
            u = x_ref[...]
        else:
            x_ref, z_ref, g_ref, o_ref = refs
            z = z_ref[...]
            u = x_ref[...] * (z * _sigmoid(z))
        r = lax.rsqrt(jnp.mean(u * u, axis=1, keepdims=True) + EPS)
        o_ref[...] = (u * r * g_ref[...]).astype(out_dtype)

    row = pl.BlockSpec((bs, w), lambda i: (i, 0))
    vec = pl.BlockSpec((1, w), lambda i: (0, 0))
    ins = [x] + ([] if gate is None else [gate]) + [g.reshape(1, w)]
    specs = [row] + ([] if gate is None else [row]) + [vec]
    return _call(body, name=name, out_shape=jax.ShapeDtypeStruct((s, w), out_dtype), grid=(s // bs,),
                 in_specs=specs, out_specs=row, sem=("parallel",))(*ins)


def rms_bwd(x, g, dy, *, name, gate=None, add=None):
    s, w = x.shape
    bs = _rows(s, w)

    def body(*refs):
        refs = list(refs)
        x_ref = refs.pop(0)
        z_ref = refs.pop(0) if gate is not None else None
        g_ref = refs.pop(0)
        dy_ref = refs.pop(0)
        add_ref = refs.pop(0) if add is not None else None
        dx_ref = refs.pop(0)
        dz_ref = refs.pop(0) if gate is not None else None
        dg_ref = refs.pop(0)
        i = pl.program_id(0)

        @pl.when(i == 0)
        def _():
            dg_ref[...] = jnp.zeros_like(dg_ref)

        xv = x_ref[...]
        if gate is not None:
            z = z_ref[...]
            sg = _sigmoid(z)
            act = z * sg
            u = xv * act
        else:
            u = xv
        r = lax.rsqrt(jnp.mean(u * u, axis=1, keepdims=True) + EPS)
        dy_v = dy_ref[...]
        dyg = dy_v * g_ref[...]
        du = r * dyg - u * (r * r * r * jnp.mean(dyg * u, axis=1, keepdims=True))
        dg_ref[...] += jnp.sum(dy_v * u * r, axis=0, keepdims=True)
        if gate is not None:
            dx = du * act
            dz_ref[...] = du * xv * (sg * (1.0 + z * (1.0 - sg)))
        else:
            dx = du
        if add is not None:
            dx = dx + add_ref[...]
        dx_ref[...] = dx

    row = pl.BlockSpec((bs, w), lambda i: (i, 0))
    vec = pl.BlockSpec((1, w), lambda i: (0, 0))
    ins = [x] + ([] if gate is None else [gate]) + [g.reshape(1, w), dy] + ([] if add is None else [add])
    specs = [row] + ([] if gate is None else [row]) + [vec, row] + ([] if add is None else [row])
    outs = [jax.ShapeDtypeStruct((s, w), F32)] + ([] if gate is None else [jax.ShapeDtypeStruct((s, w), F32)])
    outs.append(jax.ShapeDtypeStruct((1, w), F32))
    ospecs = [row] + ([] if gate is None else [row]) + [vec]
    return _call(body, name=name, out_shape=outs, grid=(s // bs,), in_specs=specs, out_specs=ospecs,
                 sem=("arbitrary",))(*ins)


def loss_head(y, target, *, name):
    s, w = y.shape
    bs = _rows(s, w)
    nb = s // bs

    def body(y_ref, t_ref, dy_ref, loss_ref, acc):
        i = pl.program_id(0)

        @pl.when(i == 0)
        def _():
            acc[...] = jnp.zeros_like(acc)

        e = y_ref[...] - t_ref[...]
        dy_ref[...] = e * (1.0 / w)
        acc[...] += jnp.sum(e * e, axis=0, keepdims=True)

        @pl.when(i == nb - 1)
        def _():
            loss_ref[...] = jnp.sum(acc[...], axis=1, keepdims=True) * (0.5 / w)

    row = pl.BlockSpec((bs, w), lambda i: (i, 0))
    return _call(body, name=name, out_shape=[jax.ShapeDtypeStruct((s, w), F32), jax.ShapeDtypeStruct((1, 1), F32)],
                 grid=(nb,), in_specs=[row, row], out_specs=[row, pl.BlockSpec((1, 1), lambda i: (0, 0))],
                 scratch=[pltpu.VMEM((1, w), F32)], sem=("arbitrary",))(y, target)


def _rope_tables(positions, s):
    inv_freq = 1.0 / (ROPE_THETA ** (jnp.arange(0, MLA_ROPE, 2, dtype=F32) / MLA_ROPE))
    ang = positions.reshape(s, 1).astype(F32) * inv_freq
    cos, sin = jnp.cos(ang), jnp.sin(ang)
    one, zero = jnp.ones((s, MLA_NOPE), F32), jnp.zeros((s, MLA_NOPE), F32)
    cq = jnp.tile(jnp.concatenate([one, cos, cos], axis=1), (1, MLA_HEADS))
    sq = jnp.tile(jnp.concatenate([zero, sin, sin], axis=1), (1, MLA_HEADS))
    pad1, pad0 = jnp.ones((s, LANES - MLA_ROPE), F32), jnp.zeros((s, LANES - MLA_ROPE), F32)
    ct = jnp.concatenate([cos, cos, pad1], axis=1)
    st = jnp.concatenate([sin, sin, pad0], axis=1)
    half = MLA_ROPE // 2

    def swap(width, starts):
        r = np.zeros((width, width), np.float32)
        for o in starts:
            for i in range(half):
                r[o + half + i, o + i] = -1.0
                r[o + i, o + half + i] = 1.0
        return jnp.asarray(r)

    rq = swap(MLA_HEADS * MLA_QK, [h * MLA_QK + MLA_NOPE for h in range(MLA_HEADS)])
    rt = swap(LANES, [0])
    return (cq, sq, rq), (ct, st, rt)


def rope(x, tabs, *, name, backward=False, add=None):
    cos, sin, rot = tabs
    n, s, w = x.shape
    bs = _rows(s, w)

    def body(*refs):
        if add is None:
            x_ref, c_ref, s_ref, r_ref, o_ref = refs
        else:
            x_ref, c_ref, s_ref, r_ref, a_ref, o_ref = refs
        xv = x_ref[0]
        for j in range(1, n):
            xv = xv + x_ref[j]
        if backward:
            out = xv * c_ref[...] + _nt(xv * s_ref[...], r_ref[...], HIGHEST)
        else:
            out = xv * c_ref[...] + _nn(xv, r_ref[...], HIGHEST) * s_ref[...]
        if add is not None:
            out = out + a_ref[...]
        o_ref[...] = out

    row = pl.BlockSpec((bs, w), lambda i: (i, 0))
    ins = [x, cos, sin, rot] + ([] if add is None else [add])
    specs = [pl.BlockSpec((n, bs, w), lambda i: (0, i, 0)), row, row, pl.BlockSpec((w, w), lambda i: (0, 0))]
    specs += [] if add is None else [row]
    return _call(body, name=name, out_shape=jax.ShapeDtypeStruct((s, w), F32), grid=(s // bs,), in_specs=specs,
                 out_specs=row, sem=("parallel",))(*ins)


MESH = pl.DeviceIdType.MESH
HBM = pl.BlockSpec(memory_space=pltpu.HBM)


def _flip(v, bit):
    return 1 - v if bit else v


class Carried:
    def __init__(self, kind, arrays):
        assert kind in ('gather', 'scatter')
        self.kind, self.arrays, self.n = kind, list(arrays), len(arrays)

    @property
    def out_shape(self):
        lead = (N_DEV,) if self.kind == 'gather' else ()
        return [jax.ShapeDtypeStruct(lead + a.shape, a.dtype) for a in self.arrays]

    @property
    def scratch(self):
        return [pltpu.SemaphoreType.DMA((self.n, N_DEV - 1)), pltpu.SemaphoreType.DMA((self.n, N_DEV - 1)),
                pltpu.SemaphoreType.DMA((self.n,))]

    def _copies(self, in_refs, out_refs, sems):
        send_sems, recv_sems, local_sems = sems
        x, y, c = lax.axis_index("x"), lax.axis_index("y"), lax.axis_index("c")
        me = 4 * x + 2 * y + c
        part = (lambda b, p: in_refs[b]) if self.kind == 'gather' else (lambda b, p: in_refs[b].at[p])
        local = [pltpu.make_async_copy(part(b, me), out_refs[b].at[me], local_sems.at[b]) for b in range(self.n)]
        remote = []
        for k in range(1, N_DEV):
            px, py, pc = _flip(x, k & 4), _flip(y, k & 2), _flip(c, k & 1)
            for b in range(self.n):
                remote.append(pltpu.make_async_remote_copy(
                    src_ref=part(b, 4 * px + 2 * py + pc), dst_ref=out_refs[b].at[me], send_sem=send_sems.at[b, k - 1],
                    recv_sem=recv_sems.at[b, k - 1], device_id=(px, py, pc), device_id_type=MESH))
        return local, remote

    def start(self, in_refs, out_refs, sems):
        local, remote = self._copies(in_refs, out_refs, sems)
        for cp in local + remote:
            cp.start()

    def wait(self, in_refs, out_refs, sems):
        local, remote = self._copies(in_refs, out_refs, sems)
        for cp in remote:
            cp.wait_send()
            cp.wait_recv()
        for cp in local:
            cp.wait()


def _first_last_step(n0, n1):
    at = lambda a, b: (pl.program_id(0) == a) & (pl.program_id(1) == b)
    return (lambda: at(0, 0)), (lambda: at(n0 - 1, n1 - 1))


def _carry_call(body, carried, first, last, ins, *, out_shape, in_specs, out_specs, scratch=(), **kw):
    if carried is None:
        return _call(body, out_shape=out_shape, in_specs=in_specs, out_specs=out_specs, scratch=scratch, **kw)(*ins)
    n, n_in, n_out, n_scr = carried.n, len(in_specs), len(out_specs), len(scratch)

    def riding(*refs):
        own_in, ride_in = refs[:n_in], refs[n_in:n_in + n]
        rest = refs[n_in + n:]
        own_out, ride_out = rest[:n_out], rest[n_out:n_out + n]
        own_scr, sems = rest[n_out + n:n_out + n + n_scr], rest[n_out + n + n_scr:]
        pl.when(first())(lambda: carried.start(ride_in, ride_out, sems))
        body(*own_in, *own_out, *own_scr)
        pl.when(last())(lambda: carried.wait(ride_in, ride_out, sems))

    return _call(riding, out_shape=list(out_shape) + carried.out_shape, in_specs=list(in_specs) + [HBM] * n,
                 out_specs=list(out_specs) + [HBM] * n, scratch=list(scratch) + carried.scratch, **kw)(*ins, *carried.arrays)


def _tri(n, op):
    r = lax.broadcasted_iota(jnp.int32, (n, n), 0)
    c = lax.broadcasted_iota(jnp.int32, (n, n), 1)
    return r, c, op(r, c)


def _pair_split(x, first):
    zero = jnp.zeros_like(x)
    return jnp.where(first, x, zero), jnp.where(first, zero, x)


def _sb_specs(s, blk):
    npair = SB_WIDTH // LANES
    q = pl.BlockSpec((blk, LANES), lambda j, i: (i, j))
    k = pl.BlockSpec((s, LANES), lambda j, i: (0, npair + j))
    v = pl.BlockSpec((s, LANES), lambda j, i: (0, 2 * npair + j))
    full = pl.BlockSpec((s, LANES), lambda j, i: (0, j))
    return q, k, v, full


def _stack_heads(x, first):
    return jnp.concatenate(_pair_split(x, first), axis=0)


def _unstack_heads(x, first, blk):
    return jnp.where(first, x[:blk], x[blk:])


def sb_fwd(qkv, *, name, carried=None):
    s = qkv.shape[0]
    blk = _tile(s, (ATT_BLK,))
    scale = SB_DIM ** -0.5
    npair, nq = SB_WIDTH // LANES, s // blk
    assert nq % SB_WIDE == 0

    def body(q_ref, k_ref, v_ref, y_ref):
        qi = pl.program_id(1)
        first = lax.broadcasted_iota(jnp.int32, (blk, LANES), 1) < SB_DIM
        q2 = _stack_heads((q_ref[...].astype(F32) * scale).astype(MXU_DTYPE), first)
        row, col, later_mask = _tri(blk, lambda r, c: r > c)
        u_later = later_mask.astype(MXU_DTYPE)

        def tile(kb, carry, masked, n):
            c, acc = carry
            keys = pl.ds(pl.multiple_of(kb * blk, blk), n * blk)
            z = _nt(q2, k_ref[keys, :])
            sp = _softplus_att(z)
            if masked:
                valid = _causal_mask(blk, n, qi, kb, 2, True)
            spm = jnp.where(valid, sp, 0.0) if masked else sp
            later = _cum(spm, u_later)
            sums = [jnp.sum(spm[:, t * blk:(t + 1) * blk], axis=1, keepdims=True) for t in range(n)]
            after, cols = c, [None] * n
            for t in reversed(range(n)):
                cols[t] = jnp.broadcast_to(after, (2 * blk, blk))
                after = after - sums[t]
            w = jnp.exp((z - sp) - later + (cols[0] if n == 1 else jnp.concatenate(cols, axis=1)))
            if masked:
                w = jnp.where(valid, w, 0.0)
            return after, acc + _nn(w.astype(MXU_DTYPE), v_ref[keys, :])

        zero = (jnp.zeros((2 * blk, 1), F32), jnp.zeros((2 * blk, LANES), F32))
        _, (c, acc) = _spent_loop(qi, tile, zero, SB_WIDE, lambda cr: jnp.max(cr[0]) >= SB_SPENT)
        y_ref[...] = _unstack_heads(acc, first, blk)

    qspec, kspec, vspec, _ = _sb_specs(s, blk)
    return _carry_call(body, carried, *_first_last_step(npair, nq), (qkv, qkv, qkv), name=name,
                       out_shape=[jax.ShapeDtypeStruct((s, SB_WIDTH), F32)], grid=(npair, nq),
                       in_specs=[qspec, kspec, vspec], out_specs=[qspec], sem=("arbitrary", "arbitrary"))


def sb_bwd(qkv, dy, *, name, carried=None):
    s = qkv.shape[0]
    blk = _tile(s, (ATT_BLK,))
    scale = SB_DIM ** -0.5
    npair, nq = SB_WIDTH // LANES, s // blk
    assert nq % SB_WIDE == 0

    def body(q_ref, k_ref, v_ref, dy_ref, dq_ref, dk_ref, dv_ref):
        qi = pl.program_id(1)

        @pl.when(qi == 0)
        def _():
            dk_ref[...] = jnp.zeros_like(dk_ref)
            dv_ref[...] = jnp.zeros_like(dv_ref)

        first = lax.broadcasted_iota(jnp.int32, (blk, LANES), 1) < SB_DIM
        q2 = _stack_heads((q_ref[...].astype(F32) * scale).astype(MXU_DTYPE), first)
        dy2 = _stack_heads(dy_ref[...].astype(MXU_DTYPE), first)
        row, col, incl_mask = _tri(blk, lambda r, c: r <= c)
        u_incl = incl_mask.astype(MXU_DTYPE)
        u_excl = (row < col).astype(MXU_DTYPE)

        def walk(kb, c, masked, n):
            sp = _softplus_att(_nt(q2, k_ref[pl.ds(pl.multiple_of(kb * blk, blk), n * blk), :]))
            if masked:
                sp = jnp.where(_causal_mask(blk, n, qi, kb, 2, True), sp, 0.0)
            for t in reversed(range(n)):
                c = c - jnp.sum(sp[:, t * blk:(t + 1) * blk], axis=1, keepdims=True)
            return c

        start, tv = _spent_loop(qi, walk, jnp.zeros((2 * blk, 1), F32), SB_WIDE, lambda c: jnp.max(c) >= SB_SPENT)

        def prefixed(x, carry):
            n = x.shape[1] // blk
            cols = []
            for t in range(n):
                cols.append(jnp.broadcast_to(carry, (2 * blk, blk)))
                carry = carry + jnp.sum(x[:, t * blk:(t + 1) * blk], axis=1, keepdims=True)
            return (cols[0] if n == 1 else jnp.concatenate(cols, axis=1)), carry

        def tile(kb, carry, masked, n):
            p, gc, dq = carry
            keys = pl.ds(pl.multiple_of(kb * blk, blk), n * blk)
            kv = k_ref[keys, :]
            z = _nt(q2, kv)
            dw = _nt(dy2, v_ref[keys, :])
            sp = _softplus_att(z)
            if masked:
                valid = _causal_mask(blk, n, qi, kb, 2, True)
            spm = jnp.where(valid, sp, 0.0) if masked else sp
            before, p = prefixed(spm, p)
            w = jnp.exp((z - sp) + (_cum(spm, u_incl) + before))
            if masked:
                w = jnp.where(valid, w, 0.0)
            g = w * dw
            gbefore, gc = prefixed(g, gc)
            gb = g.astype(MXU_DTYPE)
            gin = _nn(jnp.concatenate([gb[:, t * blk:(t + 1) * blk] for t in range(n)], axis=0), u_excl)
            gex = gbefore + jnp.concatenate([gin[t * 2 * blk:(t + 1) * 2 * blk] for t in range(n)], axis=1)
            keep = jnp.exp(-spm)
            dz = keep * (g + gex) - gex
            if masked:
                dz = jnp.where(valid, dz, 0.0)
            dzb = dz.astype(MXU_DTYPE)
            dk_ref[keys, :] += _tn(dzb, q2)
            dv_ref[keys, :] += _tn(w.astype(MXU_DTYPE), dy2)
            return p, gc, dq + _nn(dzb, kv)

        zero = jnp.zeros((2 * blk, 1), F32)
        _, _, dq = _causal_loop(qi, tile, (tv, zero, jnp.zeros((2 * blk, LANES), F32)), SB_WIDE, first=start)
        dq_ref[...] = _unstack_heads(dq, first, blk) * scale

    qspec, kspec, vspec, full = _sb_specs(s, blk)
    out = jax.ShapeDtypeStruct((s, SB_WIDTH), F32)
    return _carry_call(body, carried, *_first_last_step(npair, nq), (qkv, qkv, qkv, dy), name=name,
                       out_shape=[out, out, out], grid=(npair, nq), in_specs=[qspec, kspec, vspec, qspec],
                       out_specs=[qspec, full, full], sem=("arbitrary", "arbitrary"))


ATT_PAIR = 2


def _mla_specs(s, blk, dk, dv):
    q = pl.BlockSpec((ATT_PAIR, blk, dk), lambda hp, i: (hp, i, 0))
    k = pl.BlockSpec((ATT_PAIR, s, dk), lambda hp, i: (hp, 0, 0))
    v = pl.BlockSpec((ATT_PAIR, s, dv), lambda hp, i: (hp, 0, 0))
    y = pl.BlockSpec((ATT_PAIR, blk, dv), lambda hp, i: (hp, i, 0))
    lse = pl.BlockSpec((ATT_PAIR, blk, LANES), lambda hp, i: (hp, i, 0))
    return q, k, v, y, lse


def mla_fwd(q, k, v, *, name, carried=None):
    h, s, dk = q.shape
    dv = v.shape[-1]
    blk = _tile(s, (ATT_BLK,))
    scale = dk ** -0.5
    assert (s // blk) % MLA_WIDE == 0

    def body(q_ref, k_ref, v_ref, y_ref, l_ref):
        qi = pl.program_id(1)

        def tile(kb, carry, masked, n):
            keys = pl.ds(pl.multiple_of(kb * blk, blk), n * blk)
            out = []
            for hh in range(ATT_PAIR):
                m, l, acc = carry[hh]
                sc = _nt(q_ref[hh], k_ref[hh, keys, :]) * scale
                if masked:
                    sc = jnp.where(_causal_mask(blk, n, qi, kb, 1, False), sc, -1e30)
                m2 = jnp.maximum(m, jnp.max(sc, axis=1, keepdims=True))
                p = jnp.exp(sc - m2)
                a = jnp.exp(m - m2)
                out.append((m2, a * l + jnp.sum(p, axis=1, keepdims=True),
                            a * acc + _nn(p.astype(MXU_DTYPE), v_ref[hh, keys, :])))
            return tuple(out)

        init = (jnp.full((blk, 1), -1e30, F32), jnp.zeros((blk, 1), F32), jnp.zeros((blk, dv), F32))
        for hh, (m, l, acc) in enumerate(_causal_loop(qi, tile, (init,) * ATT_PAIR, MLA_WIDE)):
            y_ref[hh] = acc / l
            l_ref[hh] = jnp.broadcast_to(m + jnp.log(l), (blk, LANES))

    qspec, kspec, vspec, yspec, lspec = _mla_specs(s, blk, dk, dv)
    grid = (h // ATT_PAIR, s // blk)
    return _carry_call(body, carried, *_first_last_step(*grid), (q, k, v), name=name,
                       out_shape=[jax.ShapeDtypeStruct((h, s, dv), F32), jax.ShapeDtypeStruct((h, s, LANES), F32)],
                       grid=grid, in_specs=[qspec, kspec, vspec], out_specs=[yspec, lspec],
                       sem=("arbitrary", "arbitrary"))


def mla_bwd(q, k, v, y, dy, lse, *, name):
    h, s, dk = q.shape
    dv = v.shape[-1]
    blk = _tile(s, (ATT_BLK,))
    scale = dk ** -0.5
    assert (s // blk) % MLA_WIDE == 0

    def body(q_ref, k_ref, v_ref, y_ref, dy_ref, l_ref, dq_ref, dk_ref, dv_ref):
        qi = pl.program_id(1)

        @pl.when(qi == 0)
        def _():
            dk_ref[...] = jnp.zeros_like(dk_ref)
            dv_ref[...] = jnp.zeros_like(dv_ref)

        as_row = lambda col: jnp.transpose(jnp.broadcast_to(col, (blk, LANES)))[0:1, :]
        dyv = [dy_ref[hh].astype(MXU_DTYPE) for hh in range(ATT_PAIR)]
        delta = [as_row(jnp.sum(dy_ref[hh] * y_ref[hh], axis=1, keepdims=True)) for hh in range(ATT_PAIR)]
        lv = [as_row(l_ref[hh, :, 0:1]) for hh in range(ATT_PAIR)]
        def tile(kb, dqs, masked, n):
            keys = pl.ds(pl.multiple_of(kb * blk, blk), n * blk)
            out = []
            for hh in range(ATT_PAIR):
                qv = q_ref[hh]
                kv = k_ref[hh, keys, :]
                p = jnp.exp(_nt(kv, qv) * scale - lv[hh])
                if masked:
                    p = jnp.where(_causal_mask(blk, n, qi, kb, 1, False, keys_on_rows=True), p, 0.0)
                ds = (p * (_nt(v_ref[hh, keys, :], dyv[hh]) - delta[hh])).astype(MXU_DTYPE)
                dk_ref[hh, keys, :] += _nn(ds, qv) * scale
                dv_ref[hh, keys, :] += _nn(p.astype(MXU_DTYPE), dyv[hh])
                out.append(dqs[hh] + _tn(ds, kv))
            return tuple(out)

        for hh, dq in enumerate(_causal_loop(qi, tile, (jnp.zeros((blk, dk), F32),) * ATT_PAIR, MLA_WIDE)):
            dq_ref[hh] = dq * scale

    qspec, kspec, vspec, yspec, lspec = _mla_specs(s, blk, dk, dv)
    return _call(body, name=name,
                 out_shape=[jax.ShapeDtypeStruct((h, s, dk), F32), jax.ShapeDtypeStruct((h, s, dk), F32),
                            jax.ShapeDtypeStruct((h, s, dv), F32)],
                 grid=(h // ATT_PAIR, s // blk), in_specs=[qspec, kspec, vspec, yspec, yspec, lspec],
                 out_specs=[qspec, kspec, vspec], sem=("parallel", "arbitrary"))(q, k, v, y, dy, lse)


HALO = 8
CONV_CHUNK = 16


def _conv_tiles(x):
    s, c = x.shape[-2:]
    return s, c, _tile(s, (ROW_BLK,)), _tile(c, (CONV_COLS,))


def _halo_rows(dtype):
    return HALO * 4 // jnp.dtype(dtype).itemsize


def _conv_specs(bs, cw, lead=(), dtype=F32):
    zero = (0,) * len(lead)
    hr = _halo_rows(dtype)
    blk = pl.BlockSpec(lead + (None, bs, cw), lambda p, j, i: zero + (p, i, j))
    halo = pl.BlockSpec(lead + (None, hr, cw), lambda p, j, i: zero + (p, jnp.maximum(i * (bs // hr) - 1, 0), j))
    w = lambda kk: pl.BlockSpec(lead + (None, kk, cw), lambda p, j, i: zero + (p, 0, j))
    return blk, halo, w


def _stage(scr, x_ref, halo_ref, first):
    hr = halo_ref.shape[0]
    scr[0:HALO, :] = jnp.where(first, 0.0, halo_ref[hr - HALO:hr, :].astype(F32))
    scr[HALO:, :] = x_ref[...].astype(F32)


def _shifted(ext, shift):
    return ext[HALO:] if shift == 0 else pltpu.roll(ext, shift, 0)[HALO:]


def _conv_taps(scr, kk, r0):
    ext = scr[pl.ds(r0, CONV_CHUNK + HALO), :]
    return [_shifted(ext, kk - 1 - k) for k in range(kk)]


def _conv_sum(taps, w_ref, b_ref):
    u = b_ref[...] + taps[0] * w_ref[0:1, :]
    for k in range(1, len(taps)):
        u = u + taps[k] * w_ref[k:k + 1, :]
    return u


def _fold(x):
    out = x[0:8]
    for r in range(8, CONV_CHUNK, 8):
        out = out + x[r:r + 8]
    return out


class _TapSums:
    def __init__(self, kk, cw):
        self.w = [jnp.zeros((8, cw), F32) for _ in range(kk)]
        self.b = jnp.zeros((8, cw), F32)

    def add(self, du, taps):
        self.w = [a + _fold(du * t) for a, t in zip(self.w, taps)]
        self.b = self.b + _fold(du)

    def flush(self, dw_ref, db_ref):
        for k, a in enumerate(self.w):
            dw_ref[k:k + 1, :] += jnp.sum(a, axis=0, keepdims=True)
        db_ref[...] += jnp.sum(self.b, axis=0, keepdims=True)


def _silu_grad(u):
    sg = _sigmoid(u)
    return sg * (1.0 + u * (1.0 - sg))


def conv_silu_fwd(x, w, b, *, name):
    s, c, bs, cw = _conv_tiles(x)
    kk = w.shape[1]

    def body(x_ref, h_ref, w_ref, b_ref, o_ref, scr):
        _stage(scr, x_ref, h_ref, pl.program_id(2) == 0)
        for r0 in range(0, bs, CONV_CHUNK):
            u = _conv_sum(_conv_taps(scr, kk, r0), w_ref, b_ref)
            o_ref[pl.ds(r0, CONV_CHUNK), :] = u * _sigmoid(u)

    blk, halo, wspec = _conv_specs(bs, cw, dtype=x.dtype)
    return _call(body, name=name, out_shape=jax.ShapeDtypeStruct(x.shape, F32), grid=(x.shape[0], c // cw, s // bs),
                 in_specs=[blk, halo, wspec(kk), wspec(1)], out_specs=blk, scratch=[pltpu.VMEM((bs + HALO, cw), F32)],
                 sem=("parallel", "parallel", "arbitrary"))(x, x, w, b)


def conv_silu_bwd(x, dy, w, b, *, name):
    s, c, bs, cw = _conv_tiles(x)
    kk = w.shape[1]

    def body(x_ref, h_ref, w_ref, b_ref, dy_ref, du_ref, dw_ref, db_ref, scr):
        i = pl.program_id(2)

        @pl.when(i == 0)
        def _():
            dw_ref[...] = jnp.zeros_like(dw_ref)
            db_ref[...] = jnp.zeros_like(db_ref)

        _stage(scr, x_ref, h_ref, i == 0)
        sums = _TapSums(kk, cw)
        for r0 in range(0, bs, CONV_CHUNK):
            taps = _conv_taps(scr, kk, r0)
            du = dy_ref[pl.ds(r0, CONV_CHUNK), :] * _silu_grad(_conv_sum(taps, w_ref, b_ref))
            du_ref[pl.ds(r0, CONV_CHUNK), :] = du
            sums.add(du, taps)
        sums.flush(dw_ref, db_ref)

    blk, halo, wspec = _conv_specs(bs, cw, dtype=x.dtype)
    return _call(body, name=name,
                 out_shape=[jax.ShapeDtypeStruct(x.shape, F32), jax.ShapeDtypeStruct(w.shape, F32),
                            jax.ShapeDtypeStruct(b.shape, F32)],
                 grid=(x.shape[0], c // cw, s // bs), in_specs=[blk, halo, wspec(kk), wspec(1), blk],
                 out_specs=[blk, wspec(kk), wspec(1)], scratch=[pltpu.VMEM((bs + HALO, cw), F32)],
                 sem=("parallel", "parallel", "arbitrary"))(x, x, w, b, dy)


def _glu_view(a):
    return a.reshape((2, a.shape[0] // 2) + a.shape[1:])


def conv_glu_fwd(x, w, b, *, name):
    s, c, bs, cw = _conv_tiles(x)
    kk = w.shape[1]
    half = x.shape[0] // 2

    def body(x_ref, h_ref, w_ref, b_ref, o_ref, gscr, vscr):
        first = pl.program_id(2) == 0
        _stage(gscr, x_ref.at[0], h_ref.at[0], first)
        _stage(vscr, x_ref.at[1], h_ref.at[1], first)
        for r0 in range(0, bs, CONV_CHUNK):
            gate = _conv_sum(_conv_taps(gscr, kk, r0), w_ref.at[0], b_ref.at[0])
            val = _conv_sum(_conv_taps(vscr, kk, r0), w_ref.at[1], b_ref.at[1])
            o_ref[pl.ds(r0, CONV_CHUNK), :] = (gate * _sigmoid(gate) * val).astype(o_ref.dtype)

    blk, halo, wspec = _conv_specs(bs, cw, lead=(2,), dtype=x.dtype)
    out, _, _ = _conv_specs(bs, cw)
    xv = _glu_view(x)
    return _call(body, name=name, out_shape=jax.ShapeDtypeStruct((half, s, c), MXU_DTYPE), grid=(half, c // cw, s // bs),
                 in_specs=[blk, halo, wspec(kk), wspec(1)], out_specs=out, scratch=[pltpu.VMEM((bs + HALO, cw), F32)] * 2,
                 sem=("parallel", "parallel", "arbitrary"))(xv, xv, _glu_view(w), _glu_view(b))


def conv_glu_bwd(x, da, w, b, *, name):
    s, c, bs, cw = _conv_tiles(x)
    kk = w.shape[1]
    half = x.shape[0] // 2

    def body(x_ref, h_ref, w_ref, b_ref, da_ref, du_ref, dw_ref, db_ref, gscr, vscr):
        i = pl.program_id(2)

        @pl.when(i == 0)
        def _():
            dw_ref[...] = jnp.zeros_like(dw_ref)
            db_ref[...] = jnp.zeros_like(db_ref)

        _stage(gscr, x_ref.at[0], h_ref.at[0], i == 0)
        _stage(vscr, x_ref.at[1], h_ref.at[1], i == 0)
        gsums, vsums = _TapSums(kk, cw), _TapSums(kk, cw)
        for r0 in range(0, bs, CONV_CHUNK):
            gtaps, vtaps = _conv_taps(gscr, kk, r0), _conv_taps(vscr, kk, r0)
            gate = _conv_sum(gtaps, w_ref.at[0], b_ref.at[0])
            val = _conv_sum(vtaps, w_ref.at[1], b_ref.at[1])
            dav = da_ref[pl.ds(r0, CONV_CHUNK), :]
            dgate = dav * val * _silu_grad(gate)
            dval = dav * gate * _sigmoid(gate)
            du_ref[0, pl.ds(r0, CONV_CHUNK), :] = dgate.astype(du_ref.dtype)
            du_ref[1, pl.ds(r0, CONV_CHUNK), :] = dval.astype(du_ref.dtype)
            gsums.add(dgate, gtaps)
            vsums.add(dval, vtaps)
        gsums.flush(dw_ref.at[0], db_ref.at[0])
        vsums.flush(dw_ref.at[1], db_ref.at[1])

    blk, halo, wspec = _conv_specs(bs, cw, lead=(2,), dtype=x.dtype)
    daspec, _, _ = _conv_specs(bs, cw)
    xv, wv, bv = _glu_view(x), _glu_view(w), _glu_view(b)
    du, dw, db = _call(body, name=name,
                       out_shape=[jax.ShapeDtypeStruct(xv.shape, x.dtype), jax.ShapeDtypeStruct(wv.shape, F32),
                                  jax.ShapeDtypeStruct(bv.shape, F32)],
                       grid=(half, c // cw, s // bs), in_specs=[blk, halo, wspec(kk), wspec(1), daspec],
                       out_specs=[blk, wspec(kk), wspec(1)], scratch=[pltpu.VMEM((bs + HALO, cw), F32)] * 2,
                       sem=("parallel", "parallel", "arbitrary"))(xv, xv, wv, bv, da)
    return du.reshape(x.shape), dw.reshape(w.shape), db.reshape(b.shape)


def conv_t(du, w, *, name, out_dtype=F32):
    s, c, bs, cw = _conv_tiles(du)
    kk = w.shape[1]
    nb = s // bs

    def body(d_ref, h_ref, w_ref, o_ref, scr):
        last = pl.program_id(2) == nb - 1
        scr[0:bs, :] = d_ref[...].astype(F32)
        scr[bs:, :] = jnp.where(last, 0.0, h_ref[0:HALO, :].astype(F32))
        for r0 in range(0, bs, CONV_CHUNK):
            ext = scr[pl.ds(r0, CONV_CHUNK + HALO), :]
            ahead = lambda j: ext[:CONV_CHUNK] if j == 0 else pltpu.roll(ext, CONV_CHUNK + HALO - j, 0)[:CONV_CHUNK]
            acc = ahead(kk - 1) * w_ref[0:1, :]
            for k in range(1, kk):
                acc = acc + ahead(kk - 1 - k) * w_ref[k:k + 1, :]
            o_ref[pl.ds(r0, CONV_CHUNK), :] = acc.astype(out_dtype)

    blk, _, wspec = _conv_specs(bs, cw)
    hr = _halo_rows(du.dtype)
    halo = pl.BlockSpec((None, hr, cw), lambda q, j, i: (q, jnp.minimum((i + 1) * (bs // hr), s // hr - 1), j))
    return _call(body, name=name, out_shape=jax.ShapeDtypeStruct(du.shape, out_dtype), grid=(du.shape[0], c // cw, nb),
                 in_specs=[blk, halo, wspec(kk)], out_specs=blk, scratch=[pltpu.VMEM((bs + HALO, cw), F32)],
                 sem=("parallel", "parallel", "arbitrary"))(du, du, w)


def _ssd_common(xbc_ref, tail_ref, dtrt_ref, bias_ref, biast_ref, alog_ref, alogt_ref):
    L = SSM_CHUNK
    raw = tail_ref[...] + bias_ref[...]
    dt = _softplus(raw)
    dtt = _softplus(dtrt_ref[...] + biast_ref[...])
    a = -jnp.exp(alog_ref[...])
    at = -jnp.exp(alogt_ref[...])
    row, col, lower = _tri(L, lambda r, c: r >= c)
    tril = lower.astype(F32)
    cs = _nn(tril, dt * a, HIGHEST)
    cst = _nt(dtt * at, tril, HIGHEST)
    bm = [xbc_ref[:, SSM_INNER + g * SSM_N: SSM_INNER + (g + 1) * SSM_N] for g in range(SSM_GROUPS)]
    off = SSM_INNER + SSM_GROUPS * SSM_N
    cm = [xbc_ref[:, off + g * SSM_N: off + (g + 1) * SSM_N] for g in range(SSM_GROUPS)]
    cb = [_bnt(cm[g], bm[g]) for g in range(SSM_GROUPS)]
    return raw, dt, a, lower, tril, cs, cst, bm, cm, cb


def _ssd_head(hh, xbc_ref, dt, cs, cst, lower):
    L = SSM_CHUNK
    ln = DT_LANE + hh
    x = xbc_ref[:, hh * SSM_P:(hh + 1) * SSM_P]
    dtc = dt[:, ln:ln + 1]
    csc = cs[:, ln:ln + 1]
    csr = cst[hh:hh + 1, :]
    decay = jnp.exp(jnp.where(lower, csc - csr, -1e30))
    last = cs[L - 1:L, ln:ln + 1]
    return x, dtc, csc, decay, jnp.exp(csc), jnp.exp(last - csc), jnp.exp(last)


def _ssd_inputs(tail, dt_bias, a_log, d_skip):
    H = SSM_HEADS
    lanes = lambda vec: jnp.pad(vec.reshape(1, H), ((0, 0), (DT_LANE, LANES - DT_LANE - H)))
    return (tail, tail[:, DT_LANE:DT_LANE + H].T, lanes(dt_bias), dt_bias.reshape(H, 1), lanes(a_log),
            a_log.reshape(H, 1), lanes(d_skip))


def ssd_fwd(xbc, tail, dt_bias, a_log, d_skip, *, name, carried=None):
    s = xbc.shape[0]
    L, H, P, N = SSM_CHUNK, SSM_HEADS, SSM_P, SSM_N
    nc = s // L

    def body(xbc_ref, tail_ref, dtrt_ref, bias_ref, biast_ref, alog_ref, alogt_ref, d_ref, y_ref, hp_ref, state):
        @pl.when(pl.program_id(0) == 0)
        def _():
            state[...] = jnp.zeros_like(state)

        raw, dt, a, lower, tril, cs, cst, bm, cm, cb = _ssd_common(
            xbc_ref, tail_ref, dtrt_ref, bias_ref, biast_ref, alog_ref, alogt_ref)
        for hh in range(H):
            g = hh // (H // SSM_GROUPS)
            x, dtc, csc, decay, e, tau, gamma = _ssd_head(hh, xbc_ref, dt, cs, cst, lower)
            xdt = x * dtc
            hprev = state[hh]
            hp_ref[hh] = hprev
            skip = d_ref[:, DT_LANE + hh:DT_LANE + hh + 1]
            y = _bnn(cb[g] * decay, xdt) + _bnn(cm[g], hprev) * e + x * skip
            y_ref[:, hh * P:(hh + 1) * P] = y
            state[hh] = hprev * gamma + _btn(bm[g] * tau, xdt)

    row = lambda w: pl.BlockSpec((L, w), lambda c: (c, 0))
    small = lambda shp: pl.BlockSpec(shp, lambda c: (0, 0))
    return _carry_call(
        body, carried, lambda: pl.program_id(0) == 0, lambda: pl.program_id(0) == nc - 1,
        (xbc, *_ssd_inputs(tail, dt_bias, a_log, d_skip)), name=name,
        out_shape=[jax.ShapeDtypeStruct((s, SSM_INNER), F32), jax.ShapeDtypeStruct((nc, H, N, P), F32)], grid=(nc,),
        in_specs=[row(SSM_CONV_DIM), row(LANES), pl.BlockSpec((H, L), lambda c: (0, c)), small((1, LANES)),
                  small((H, 1)), small((1, LANES)), small((H, 1)), small((1, LANES))],
        out_specs=[row(SSM_INNER), pl.BlockSpec((None, H, N, P), lambda c: (c, 0, 0, 0))],
        scratch=[pltpu.VMEM((H, N, P), F32)], sem=("arbitrary",))


def ssd_bwd(xbc, tail, dt_bias, a_log, d_skip, hprev_all, dy, *, name, carried=None):
    s = xbc.shape[0]
    L, H, P, N = SSM_CHUNK, SSM_HEADS, SSM_P, SSM_N
    nc = s // L
    hg = H // SSM_GROUPS

    def body(xbc_ref, tail_ref, dtrt_ref, bias_ref, biast_ref, alog_ref, alogt_ref, d_ref, hp_ref, dy_ref,
             dxbc_ref, ddt_ref, dbias_ref, dalog_ref, dd_ref, dstate):
        @pl.when(pl.program_id(0) == 0)
        def _():
            dstate[...] = jnp.zeros_like(dstate)
            dbias_ref[...] = jnp.zeros_like(dbias_ref)
            dalog_ref[...] = jnp.zeros_like(dalog_ref)
            dd_ref[...] = jnp.zeros_like(dd_ref)

        raw, dt, a, lower, tril, cs, cst, bm, cm, cb = _ssd_common(
            xbc_ref, tail_ref, dtrt_ref, bias_ref, biast_ref, alog_ref, alogt_ref)
        lane = lax.broadcasted_iota(jnp.int32, (L, LANES), 1)
        lane1 = lax.broadcasted_iota(jnp.int32, (1, LANES), 1)
        rowi = lax.broadcasted_iota(jnp.int32, (L, 1), 0)
        slot = lax.broadcasted_iota(jnp.int32, (LANES, L), 0)
        col_sums = jnp.zeros((LANES, L), F32)
        dcs_all = jnp.zeros((L, LANES), F32)
        ddt_x = jnp.zeros((L, LANES), F32)
        dd_row = jnp.zeros((1, LANES), F32)
        dbm = [jnp.zeros((L, N), F32) for _ in range(SSM_GROUPS)]
        dcm = [jnp.zeros((L, N), F32) for _ in range(SSM_GROUPS)]
        dcb = [jnp.zeros((L, L), F32) for _ in range(SSM_GROUPS)]
        for hh in range(H):
            g = hh // hg
            ln = DT_LANE + hh
            x, dtc, csc, decay, e, tau, gamma = _ssd_head(hh, xbc_ref, dt, cs, cst, lower)
            xdt = x * dtc
            hprev = hp_ref[hh]
            dhn = dstate[hh]
            dyh = dy_ref[:, hh * P:(hh + 1) * P]
            m = cb[g] * decay
            dxdt = _btn(m, dyh) + _bnn(bm[g] * tau, dhn)
            dm = jnp.where(lower, _bnt(dyh, xdt), 0.0)
            dcb[g] = dcb[g] + dm * decay
            dseg = dm * m
            dcs = jnp.sum(dseg, axis=1, keepdims=True)
            col_sums = jnp.where(slot == ln, jnp.sum(dseg, axis=0, keepdims=True), col_sums)
            edy = e * dyh
            dcm[g] = dcm[g] + _bnt(edy, hprev)
            dcs = dcs + e * jnp.sum(dyh * _bnn(cm[g], hprev), axis=1, keepdims=True)
            xdh = _bnt(xdt, dhn)
            dbm[g] = dbm[g] + tau * xdh
            dtau_tau = jnp.sum(bm[g] * xdh, axis=1, keepdims=True) * tau
            dlast = jnp.sum(dtau_tau, axis=0, keepdims=True) + gamma * jnp.sum(dhn * hprev, keepdims=True)
            dcs = dcs - dtau_tau + jnp.where(rowi == L - 1, dlast, 0.0)
            dstate[hh] = gamma * dhn + _btn(cm[g], edy)
            dcs_all = jnp.where(lane == ln, dcs, dcs_all)
            ddt_x = jnp.where(lane == ln, jnp.sum(dxdt * x, axis=1, keepdims=True), ddt_x)
            dxbc_ref[:, hh * P:(hh + 1) * P] = dxdt * dtc + d_ref[:, ln:ln + 1] * dyh
            dd_row = jnp.where(lane1 == ln, jnp.sum(dyh * x, keepdims=True), dd_row)
        off = SSM_INNER + SSM_GROUPS * SSM_N
        for g in range(SSM_GROUPS):
            dxbc_ref[:, SSM_INNER + g * N: SSM_INNER + (g + 1) * N] = dbm[g] + _btn(dcb[g], cm[g])
            dxbc_ref[:, off + g * N: off + (g + 1) * N] = dcm[g] + _bnn(dcb[g], bm[g])
        dcs_all = dcs_all - jnp.transpose(col_sums)
        dda = _tn(tril, dcs_all, HIGHEST)
        head_lane = (lane >= DT_LANE) & (lane < DT_LANE + H)
        draw = jnp.where(head_lane, (dda * a + ddt_x) * _sigmoid(raw), 0.0)
        ddt_ref[...] = draw
        dbias_ref[...] += jnp.sum(draw, axis=0, keepdims=True)
        dalog_ref[...] += jnp.sum(jnp.where(head_lane, dda * dt, 0.0), axis=0, keepdims=True) * a
        dd_ref[...] += dd_row

    rev = lambda c: nc - 1 - c
    row = lambda w: pl.BlockSpec((L, w), lambda c: (rev(c), 0))
    small = lambda shp: pl.BlockSpec(shp, lambda c: (0, 0))
    acc = pl.BlockSpec((1, LANES), lambda c: (0, 0))
    vec = jax.ShapeDtypeStruct((1, LANES), F32)
    return _carry_call(
        body, carried, lambda: pl.program_id(0) == 0, lambda: pl.program_id(0) == nc - 1,
        (xbc, *_ssd_inputs(tail, dt_bias, a_log, d_skip), hprev_all, dy), name=name,
        out_shape=[jax.ShapeDtypeStruct((s, SSM_CONV_DIM), F32), jax.ShapeDtypeStruct((s, LANES), F32), vec, vec, vec],
        grid=(nc,),
        in_specs=[row(SSM_CONV_DIM), row(LANES), pl.BlockSpec((H, L), lambda c: (0, rev(c))), small((1, LANES)),
                  small((H, 1)), small((1, LANES)), small((H, 1)), small((1, LANES)),
                  pl.BlockSpec((None, H, N, P), lambda c: (rev(c), 0, 0, 0)), row(SSM_INNER)],
        out_specs=[row(SSM_CONV_DIM), row(LANES), acc, acc, acc],
        scratch=[pltpu.VMEM((H, N, P), F32)], sem=("arbitrary",))


def _heads(x2d, n, d):
    s = x2d.shape[0]
    return x2d.reshape(s, n, d).transpose(1, 0, 2)


def _unheads(x3d):
    n, s, d = x3d.shape
    return x3d.transpose(1, 0, 2).reshape(s, n * d)


def layer_fwd(h, p, tabs, li, gather_late=None):
    s = h.shape[0]
    tabq, tabt = tabs
    nm = lambda t: f"L{li}_{t}"
    r = {'h': h}
    hn = rms_fwd(h, p['mix_norm'], name=nm('mixnorm'), out_dtype=MXU_DTYPE)
    proj = mm(hn, p['w_in'], name=nm('proj'))
    r.update(hn=hn, proj=proj)
    qkv = proj[:, :3 * SB_WIDTH].astype(MXU_DTYPE)
    riders = {} if gather_late is None else {
        'sb': ('w_out', 'ffn_conv_w'), 'ssd': ('ffn_w_down',), 'mla': ('ffn_w_up',)}
    ride = lambda k: Carried('gather', [gather_late[n] for n in riders[k]]) if riders else None
    got = {}
    ya, *rest = sb_fwd(qkv, name=nm('sb_fwd'), carried=ride('sb'))
    got.update(zip(riders.get('sb', ()), rest))
    yan = rms_fwd(ya, p['sb_out_norm'], name=nm('sbnorm'), out_dtype=MXU_DTYPE)
    r.update(qkv=qkv, ya=ya)
    z = proj[:, 768:1280]
    xbc = proj[None, :, 1280:2048]
    tail = proj[:, TAIL:TAIL + LANES]
    xbc_act = conv_silu_fwd(xbc, p['ssm_conv_w'], p['ssm_conv_b'], name=nm('ssmconv'))[0]
    y_ssm, hprev, *rest = ssd_fwd(xbc_act, tail, p['ssm_dt_bias'], p['ssm_a_log'], p['ssm_d'], name=nm('ssd_fwd'),
                                  carried=ride('ssd'))
    got.update(zip(riders.get('ssd', ()), rest))
    ybn = rms_fwd(y_ssm, p['ssm_out_norm'], name=nm('ssmnorm'), gate=z, out_dtype=MXU_DTYPE)
    r.update(z=z, xbc=xbc, tail=tail, xbc_act=xbc_act, y_ssm=y_ssm, hprev=hprev)
    cq = proj[:, 2048:2304]
    ckv = proj[:, 2304:2432]
    qn = rms_fwd(cq, p['mla_q_norm'], name=nm('qnorm'), out_dtype=MXU_DTYPE)
    q_r = rope(mm(qn, p['mla_w_uq'], name=nm('uq'))[None], tabq, name=nm('ropeq'))
    kvn = rms_fwd(ckv, p['mla_kv_norm'], name=nm('kvnorm'), out_dtype=MXU_DTYPE)
    kv = mm(kvn, p['mla_w_ukv'], name=nm('ukv'))
    k_pe = rope(tail[None], tabt, name=nm('ropek'))[:, :MLA_ROPE]
    qh = _heads(q_r, MLA_HEADS, MLA_QK).astype(MXU_DTYPE)
    kvh = _heads(kv, MLA_HEADS, MLA_NOPE + MLA_V)
    kh = jnp.concatenate([kvh[..., :MLA_NOPE], jnp.broadcast_to(k_pe[None], (MLA_HEADS, s, MLA_ROPE))],
                         axis=-1).astype(MXU_DTYPE)
    vh = kvh[..., MLA_NOPE:].astype(MXU_DTYPE)
    yc_h, lse, *rest = mla_fwd(qh, kh, vh, name=nm('mla_fwd'), carried=ride('mla'))
    got.update(zip(riders.get('mla', ()), rest))
    late = None
    if riders:
        late = assemble_late(got)
        p = dict(p, **{n: late[n][li] for n in LATE})
    yc = _unheads(yc_h)
    ycn = rms_fwd(yc, p['mla_out_norm'], name=nm('mlanorm'), out_dtype=MXU_DTYPE)
    r.update(cq=cq, ckv=ckv, qn=qn, kvn=kvn, qh=qh, kh=kh, vh=vh, yc_h=yc_h, yc=yc, lse=lse)
    ycat = jnp.concatenate([yan, ybn, ycn], axis=1)
    h1 = mm(ycat, p['w_out'], name=nm('outproj'), res=h)
    hn2 = rms_fwd(h1, p['ffn_norm'], name=nm('ffnnorm'), out_dtype=MXU_DTYPE)
    up = mm(hn2, p['ffn_w_up'], name=nm('up'), bb='o', out_dtype=MXU_DTYPE)
    act = conv_glu_fwd(up, p['ffn_conv_w'], p['ffn_conv_b'], name=nm('glu'))
    h2 = mm(act, p['ffn_w_down'], name=nm('down'), ab='k', bb='k', res=h1)
    r.update(ycat=ycat, h1=h1, hn2=hn2, up=up, act=act)
    return h2, r, p, late


def layer_bwd(dh2, p, r, tabs, li, scatter_late=None):
    s = dh2.shape[0]
    tabq, tabt = tabs
    nm = lambda t: f"L{li}_{t}"
    g = {}
    dact = mm(dh2, p['ffn_w_down'], name=nm('d_down_x'), tb=True, bb='o')
    g['ffn_w_down'] = mm(r['act'], dh2, name=nm('d_down_w'), out_dtype=WIRE_DTYPE, ta=True, ab='o')
    du, g['ffn_conv_w'], g['ffn_conv_b'] = conv_glu_bwd(r['up'], dact, p['ffn_conv_w'], p['ffn_conv_b'], name=nm('d_glu'))
    dup = conv_t(du, p['ffn_conv_w'], name=nm('d_ffnconv'), out_dtype=MXU_DTYPE)
    g['ffn_w_up'] = mm(r['hn2'], dup, name=nm('d_up_w'), out_dtype=WIRE_DTYPE, ta=True, bb='o')
    dhn2 = mm(dup, p['ffn_w_up'], name=nm('d_up_x'), tb=True, ab='k', bb='k')
    dh1, dg = rms_bwd(r['h1'], p['ffn_norm'], dhn2, name=nm('d_ffnnorm'), add=dh2)
    g['ffn_norm'] = dg[0]
    dycat = mm(dh1, p['w_out'], name=nm('d_out_x'), tb=True)
    g['w_out'] = mm(r['ycat'], dh1, name=nm('d_out_w'), out_dtype=WIRE_DTYPE, ta=True)
    dya, dg = rms_bwd(r['ya'], p['sb_out_norm'], dycat[:, :256], name=nm('d_sbnorm'))
    g['sb_out_norm'] = dg[0]
    riders = {} if scatter_late is None else {'sb': ('ffn_w_up',), 'ssd': ('ffn_w_down', 'w_out', 'ffn_conv_w')}
    parts = owner_parts_late([g] + list(scatter_late)) if riders else None
    ride = lambda k: Carried('scatter', [parts[n].astype(WIRE_DTYPE) for n in riders[k]]) if riders else None
    recv_late = {} if riders else None
    dq, dk, dv, *rest = sb_bwd(r['qkv'], dya, name=nm('sb_bwd'), carried=ride('sb'))
    if riders:
        recv_late.update(zip(riders['sb'], rest))
    dyssm, dz, dg = rms_bwd(r['y_ssm'], p['ssm_out_norm'], dycat[:, 256:768], name=nm('d_ssmnorm'), gate=r['z'])
    g['ssm_out_norm'] = dg[0]
    dxbc_act, ddt_tail, dbias, dalog, dd, *rest = ssd_bwd(r['xbc_act'], r['tail'], p['ssm_dt_bias'], p['ssm_a_log'],
                                                          p['ssm_d'], r['hprev'], dyssm, name=nm('ssd_bwd'),
                                                          carried=ride('ssd'))
    if riders:
        recv_late.update(zip(riders['ssd'], rest))
    hl = slice(DT_LANE, DT_LANE + SSM_HEADS)
    g['ssm_dt_bias'], g['ssm_a_log'], g['ssm_d'] = dbias[0, hl], dalog[0, hl], dd[0, hl]
    dxbc_u, g['ssm_conv_w'], g['ssm_conv_b'] = conv_silu_bwd(r['xbc'], dxbc_act[None], p['ssm_conv_w'], p['ssm_conv_b'],
                                                             name=nm('d_ssmact'))
    dxbc = conv_t(dxbc_u, p['ssm_conv_w'], name=nm('d_ssmconv'))[0]
    dyc, dg = rms_bwd(r['yc'], p['mla_out_norm'], dycat[:, 768:], name=nm('d_mlanorm'))
    g['mla_out_norm'] = dg[0]
    dqh, dkh, dvh = mla_bwd(r['qh'], r['kh'], r['vh'], r['yc_h'], _heads(dyc, MLA_HEADS, MLA_V), r['lse'], name=nm('mla_bwd'))
    dq_c = rope(_unheads(dqh)[None], tabq, name=nm('d_ropeq'), backward=True)
    g['mla_w_uq'] = mm(r['qn'], dq_c, name=nm('d_uq_w'), out_dtype=WIRE_DTYPE, ta=True)
    dcq, dg = rms_bwd(r['cq'], p['mla_q_norm'], mm(dq_c, p['mla_w_uq'], name=nm('d_uq_x'), tb=True), name=nm('d_qnorm'))
    g['mla_q_norm'] = dg[0]
    dkv = _unheads(jnp.concatenate([dkh[..., :MLA_NOPE], dvh], axis=-1))
    g['mla_w_ukv'] = mm(r['kvn'], dkv, name=nm('d_ukv_w'), out_dtype=WIRE_DTYPE, ta=True)
    dckv, dg = rms_bwd(r['ckv'], p['mla_kv_norm'], mm(dkv, p['mla_w_ukv'], name=nm('d_ukv_x'), tb=True), name=nm('d_kvnorm'))
    g['mla_kv_norm'] = dg[0]
    dkpe = jnp.pad(dkh[..., MLA_NOPE:], ((0, 0), (0, 0), (0, LANES - MLA_ROPE)))
    dtail = rope(dkpe, tabt, name=nm('d_ropek'), backward=True, add=ddt_tail)
    dproj = jnp.concatenate([dq, dk, dv, dz, dxbc, dcq, dckv, dtail], axis=1).astype(MXU_DTYPE)
    g['w_in'] = mm(r['hn'], dproj, name=nm('d_proj_w'), out_dtype=WIRE_DTYPE, ta=True)
    dhn = mm(dproj, p['w_in'], name=nm('d_proj_x'), tb=True)
    dh, dg = rms_bwd(r['h'], p['mix_norm'], dhn, name=nm('d_mixnorm'), add=dh1)
    g['mix_norm'] = dg[0]
    return dh, g, recv_late


def _w_in_placement():
    c = np.arange(D_IN)
    dest = np.where(c < 2048, c, np.where(c < 2056, c + (D_IN - 2056), c - 8))
    dest = jnp.asarray(dest.reshape(N_DEV, D_IN // N_DEV, 1), jnp.int32)
    return (dest == jnp.arange(D_IN_PAD, dtype=jnp.int32)).astype(MXU_DTYPE)


def _owner_major(full, axis):
    shp = full.shape
    return jnp.moveaxis(full.reshape(shp[:axis] + (N_DEV, shp[axis] // N_DEV) + shp[axis + 1:]), axis, 0)


def _owner_join(parts, axis):
    moved = jnp.moveaxis(parts, 0, axis)
    shp = moved.shape
    return moved.reshape(shp[:axis] + (shp[axis] * shp[axis + 1],) + shp[axis + 2:])


def assemble_early(gathered, replicated):
    L = DEPTH
    out = dict(replicated)
    out['w_in'] = mm(gathered['w_in'].reshape(N_DEV, L * D_MODEL, D_IN // N_DEV), _w_in_placement(), name='place_w_in',
                     ab='k', bb='k', out_dtype=MXU_DTYPE).reshape(L, D_MODEL, D_IN_PAD)
    out['mla_w_uq'] = _owner_join(gathered['mla_w_uq'], 2)
    out['mla_w_ukv'] = _owner_join(gathered['mla_w_ukv'], 2)
    out['ssm_conv_w'] = _owner_join(gathered['ssm_conv_w'], 2)[:, None]
    out['ssm_conv_b'] = replicated['ssm_conv_b'].reshape(L, 1, 1, SSM_CONV_DIM)
    out['ffn_conv_b'] = replicated['ffn_conv_b'].reshape(L, N_DEV, 1, FF_SHARD)
    return out


def assemble_late(gathered):
    L = DEPTH
    return {'ffn_w_up': jnp.moveaxis(gathered['ffn_w_up'], 1, 0),
            'w_out': _owner_join(gathered['w_out'], 1),
            'ffn_w_down': _owner_join(gathered['ffn_w_down'], 1).reshape(L, N_DEV // 2, FF_SHARD, D_MODEL),
            'ffn_conv_w': jnp.moveaxis(gathered['ffn_conv_w'], 1, 0)}


def owner_parts_late(grads):
    L = DEPTH
    st = lambda n: jnp.stack([g[n] for g in grads])
    return {'ffn_w_up': jnp.moveaxis(st('ffn_w_up'), 1, 0),
            'w_out': _owner_major(st('w_out'), 1),
            'ffn_w_down': _owner_major(st('ffn_w_down').reshape(L, D_FF, D_MODEL), 1),
            'ffn_conv_w': jnp.moveaxis(st('ffn_conv_w'), 1, 0)}


def owner_parts_early(grads):
    L = DEPTH
    st = lambda n: jnp.stack([g[n] for g in grads])
    parts = {
        'w_in': mm(st('w_in').reshape(L * D_MODEL, D_IN_PAD), _w_in_placement(), name='unplace_w_in', tb=True, bb='o',
                   out_dtype=WIRE_DTYPE).reshape(N_DEV, L, D_MODEL, D_IN // N_DEV),
        'mla_w_uq': _owner_major(st('mla_w_uq'), 2),
        'mla_w_ukv': _owner_major(st('mla_w_ukv'), 2),
        'ssm_conv_w': _owner_major(st('ssm_conv_w')[:, 0], 2),
    }
    rep = {n: st(n) for n in REPLICATED if n not in ('final_norm', 'ssm_conv_b', 'ffn_conv_b')}
    rep['ssm_conv_b'] = st('ssm_conv_b').reshape(L, SSM_CONV_DIM)
    rep['ffn_conv_b'] = st('ffn_conv_b').reshape(L, 2 * D_FF)
    return parts, rep


def local_step(x, positions, target, early, late_shards, replicated):
    s = x.shape[0]
    tabs = _rope_tables(positions, s)
    params = assemble_early(early, replicated)
    layer = lambda li: {n: params[n][li] for n in params if n != 'final_norm'}
    h, r0, p0, late = layer_fwd(x, layer(0), tabs, 0, gather_late=late_shards)
    saved = [(p0, r0)]
    for li in range(1, DEPTH):
        h, r, p, _ = layer_fwd(h, dict(layer(li), **{n: late[n][li] for n in LATE}), tabs, li)
        saved.append((p, r))
    y = rms_fwd(h, params['final_norm'], name='finalnorm')
    dy, loss = loss_head(y, target, name='loss')
    dh, dg = rms_bwd(h, params['final_norm'], dy, name='d_finalnorm')
    above = []
    for li in reversed(range(1, DEPTH)):
        dh, g, _ = layer_bwd(dh, *saved[li], tabs, li)
        above.insert(0, g)
    dh, g0, recv_late = layer_bwd(dh, *saved[0], tabs, 0, scatter_late=above)
    parts, rep = owner_parts_early([g0] + above)
    rep['final_norm'] = dg[0]
    return loss[0, 0], dh, parts, recv_late, rep


def all_gather(blocks, *, name):
    n = len(blocks)

    def body(*refs):
        x_refs, out_refs = refs[:n], refs[n:2 * n]
        send_sems, recv_sems, local_sems = refs[2 * n:]
        x, y, c = lax.axis_index("x"), lax.axis_index("y"), lax.axis_index("c")
        me, sibling = (x, y, c), (x, y, 1 - c)
        chips = [(1 - x, y), (x, 1 - y), (1 - x, 1 - y)]

        def slot(b, px, py, pc):
            return out_refs[b].at[4 * px + 2 * py + pc]

        def copy(b, k, blk, to, src=None):
            return pltpu.make_async_remote_copy(src_ref=slot(b, *blk) if src is None else src, dst_ref=slot(b, *blk),
                                                send_sem=send_sems.at[b, k], recv_sem=recv_sems.at[b, k],
                                                device_id=to, device_id_type=MESH)

        mine = [pltpu.make_async_copy(x_refs[b], slot(b, *me), local_sems.at[b]) for b in range(n)]
        for cp in mine:
            cp.start()
        first = []
        for b in range(n):
            first.append(copy(b, 0, me, sibling, src=x_refs[b]))
            first += [copy(b, 1 + j, me, (*chip, c), src=x_refs[b]) for j, chip in enumerate(chips)]
        for cp in first:
            cp.start()
        passed = []
        for j, chip in enumerate(chips):
            for b in range(n):
                copy(b, 1 + j, (*chip, c), me).wait_recv()
                fwd = copy(b, 4 + j, (*chip, c), sibling)
                fwd.start()
                passed.append(fwd)
        for b in range(n):
            copy(b, 0, sibling, me).wait_recv()
            for j, chip in enumerate(chips):
                copy(b, 4 + j, (*chip, 1 - c), me).wait_recv()
        for cp in first + passed:
            cp.wait_send()
        for cp in mine:
            cp.wait()

    return pl.pallas_call(
        body, name=name, out_shape=[jax.ShapeDtypeStruct((N_DEV,) + b.shape, b.dtype) for b in blocks],
        in_specs=[HBM] * n, out_specs=[HBM] * n,
        scratch_shapes=[pltpu.SemaphoreType.DMA((n, 7)), pltpu.SemaphoreType.DMA((n, 7)), pltpu.SemaphoreType.DMA((n,))],
    )(*blocks)


def all_to_all(parts, *, name):
    n = len(parts)

    def body(*refs):
        g_refs, r_refs = refs[:n], refs[n:2 * n]
        send_sems, recv_sems, local_sems = refs[2 * n:]
        x, y, c = lax.axis_index("x"), lax.axis_index("y"), lax.axis_index("c")
        me = 4 * x + 2 * y + c
        mine = [pltpu.make_async_copy(g_refs[b].at[me], r_refs[b].at[me], local_sems.at[b]) for b in range(n)]
        for cp in mine:
            cp.start()
        copies = []
        for k in range(1, N_DEV):
            px, py, pc = _flip(x, k & 4), _flip(y, k & 2), _flip(c, k & 1)
            peer = 4 * px + 2 * py + pc
            for b in range(n):
                cp = pltpu.make_async_remote_copy(src_ref=g_refs[b].at[peer], dst_ref=r_refs[b].at[me],
                                                  send_sem=send_sems.at[b, k - 1], recv_sem=recv_sems.at[b, k - 1],
                                                  device_id=(px, py, pc), device_id_type=MESH)
                cp.start()
                copies.append(cp)
        for cp in copies:
            cp.wait_send()
            cp.wait_recv()
        for cp in mine:
            cp.wait()

    return pl.pallas_call(
        body, name=name, out_shape=[jax.ShapeDtypeStruct(p.shape, p.dtype) for p in parts],
        in_specs=[HBM] * n, out_specs=[HBM] * n,
        scratch_shapes=[pltpu.SemaphoreType.DMA((n, 7)), pltpu.SemaphoreType.DMA((n, 7)), pltpu.SemaphoreType.DMA((n,))],
    )(*parts)


def adamw(parts, w, m, v, *, name):
    r, wd = w.shape
    br = _tile(r, (256, 128, 64, 32, 16, 8))
    c1 = 1.0 - ADAM_B1 ** ADAM_STEP
    c2 = 1.0 - ADAM_B2 ** ADAM_STEP

    def body(p_ref, w_ref, m_ref, v_ref, g_ref, d_ref, mo_ref, vo_ref):
        g = p_ref[0].astype(F32)
        for j in range(1, N_DEV):
            g = g + p_ref[j].astype(F32)
        mn = ADAM_B1 * m_ref[...] + (1.0 - ADAM_B1) * g
        vn = ADAM_B2 * v_ref[...] + (1.0 - ADAM_B2) * (g * g)
        g_ref[...] = g
        mo_ref[...] = mn
        vo_ref[...] = vn
        d_ref[...] = -ADAM_LR * ((mn / c1) / (jnp.sqrt(vn / c2) + ADAM_EPS) + ADAM_WD * w_ref[...])

    blk = pl.BlockSpec((br, wd), lambda i: (i, 0))
    out = jax.ShapeDtypeStruct((r, wd), F32)
    return _call(body, name=name, out_shape=[out] * 4, grid=(r // br,),
                 in_specs=[pl.BlockSpec((N_DEV, br, wd), lambda i: (0, i, 0)), blk, blk, blk], out_specs=[blk] * 4,
                 sem=("parallel",))(parts, w, m, v)


def _pack(arrs):
    flat = jnp.concatenate([a.reshape(-1) for a in arrs])
    rows = -(-flat.shape[0] // (8 * FLAT_W)) * 8
    return jnp.pad(flat, (0, rows * FLAT_W - flat.shape[0])).reshape(rows, FLAT_W)


def _unpack(flat, shapes):
    flat = flat.reshape(-1)
    out, off = [], 0
    for shp in shapes:
        n = int(np.prod(shp))
        out.append(flat[off:off + n].reshape(shp))
        off += n
    return out


def kernel(x, positions, mix_norm, w_in, sb_out_norm, ssm_conv_w, ssm_conv_b, ssm_dt_bias, ssm_a_log, ssm_d, ssm_out_norm, mla_q_norm, mla_w_uq, mla_kv_norm, mla_w_ukv, mla_out_norm, w_out, ffn_norm, ffn_w_up, ffn_conv_w, ffn_conv_b, ffn_w_down, final_norm, loss_target, m_mix_norm, m_w_in, m_sb_out_norm, m_ssm_conv_w, m_ssm_conv_b, m_ssm_dt_bias, m_ssm_a_log, m_ssm_d, m_ssm_out_norm, m_mla_q_norm, m_mla_w_uq, m_mla_kv_norm, m_mla_w_ukv, m_mla_out_norm, m_w_out, m_ffn_norm, m_ffn_w_up, m_ffn_conv_w, m_ffn_conv_b, m_ffn_w_down, m_final_norm, v_mix_norm, v_w_in, v_sb_out_norm, v_ssm_conv_w, v_ssm_conv_b, v_ssm_dt_bias, v_ssm_a_log, v_ssm_d, v_ssm_out_norm, v_mla_q_norm, v_mla_w_uq, v_mla_kv_norm, v_mla_w_ukv, v_mla_out_norm, v_w_out, v_ffn_norm, v_ffn_w_up, v_ffn_conv_w, v_ffn_conv_b, v_ffn_w_down, v_final_norm):
    args = locals()
    w = {n: args[n] for n in WEIGHTS}
    m = {n: args['m_' + n] for n in WEIGHTS}
    v = {n: args['v_' + n] for n in WEIGHTS}
    wire = lambda n: w[n] if n in VPU_WEIGHTS else w[n].astype(BF16)
    early = dict(zip(EARLY, all_gather([wire(n) for n in EARLY], name='gather_early')))

    loss, dx, parts, recv, rep = local_step(x[0], positions[0], loss_target[0], early, {n: wire(n) for n in LATE},
                                            {n: w[n] for n in REPLICATED})
    loss = lax.psum(loss, ("x", "y", "c"))

    recv.update(zip(EARLY, all_to_all([parts[n].astype(WIRE_DTYPE) for n in EARLY], name='scatter_early')))
    res = {kind: {} for kind in 'gdmv'}
    for n, rv in recv.items():
        shp = w[n].shape
        two_d = (int(np.prod(shp[:-1])), shp[-1])
        outs = adamw(rv.reshape((N_DEV,) + two_d), w[n].reshape(two_d), m[n].reshape(two_d), v[n].reshape(two_d),
                     name='adamw_' + n)
        for kind, o in zip('gdmv', outs):
            res[kind][n] = o.reshape(shp)

    rep_shapes = [w[n].shape for n in REPLICATED]
    (rparts,) = all_gather([_pack([rep[n] for n in REPLICATED])], name='gather_small_grads')
    rflat = lambda d: _pack([d[n] for n in REPLICATED])
    routs = adamw(rparts, rflat(w), rflat(m), rflat(v), name='adamw_replicated')
    for kind, o in zip('gdmv', routs):
        res[kind].update(zip(REPLICATED, _unpack(o, rep_shapes)))

    return (loss, dx[None], *[res['g'][n] for n in WEIGHTS], *[res['d'][n] for n in WEIGHTS],
            *[res['m'][n] for n in WEIGHTS], *[res['v'][n] for n in WEIGHTS])
```

```python
import numpy as np
import jax
import jax.numpy as jnp
from jax import lax
from jax.experimental import pallas as pl
from jax.experimental.pallas import tpu as pltpu

F32 = jnp.float32
BF16 = jnp.bfloat16
MXU_DTYPE = jnp.bfloat16
HIGHEST = lax.Precision.HIGHEST
WIRE_DTYPE = jnp.bfloat16

N_DEV = 8
D_MODEL = 1024
DEPTH = 2
EPS = 1e-6
SB_HEADS, SB_DIM = 4, 64
SB_WIDTH = SB_HEADS * SB_DIM
SSM_HEADS, SSM_P, SSM_GROUPS, SSM_N, SSM_CONV, SSM_CHUNK = 8, 64, 2, 64, 4, 128
SSM_INNER = SSM_HEADS * SSM_P
SSM_CONV_DIM = SSM_INNER + 2 * SSM_GROUPS * SSM_N
MLA_HEADS, MLA_NOPE, MLA_ROPE, MLA_V, MLA_Q_RANK, MLA_KV_RANK = 4, 64, 32, 64, 256, 128
MLA_QK = MLA_NOPE + MLA_ROPE
ROPE_THETA = 10000.0
D_IN = 2472
D_IN_PAD = 2560
TAIL = 2432
DT_LANE = 32
D_FF = 2816
FF_SHARD = 2 * D_FF // N_DEV
ADAM_LR, ADAM_B1, ADAM_B2, ADAM_EPS, ADAM_WD, ADAM_STEP = 0.001, 0.9, 0.999, 1e-08, 0.01, 10

LANES = 128
ATT_BLK = 256
SB_WIDE = 2
SB_SPENT = -110.0
MLA_WIDE = 4
ROW_BLK = 512
ROW_BLOCK_BYTES = 2 << 20
CONV_COLS = 256
FLAT_W = 1024
VMEM_LIMIT = 56 << 20
MM_TM = (1024, 512, 256, 128)
MM_TN = (1280, 1024, 768, 640, 512, 384, 256, 128)
MM_TK = (1280, 1024, 512, 256, 128)

WEIGHTS = ['mix_norm', 'w_in', 'sb_out_norm', 'ssm_conv_w', 'ssm_conv_b', 'ssm_dt_bias', 'ssm_a_log', 'ssm_d',
           'ssm_out_norm', 'mla_q_norm', 'mla_w_uq', 'mla_kv_norm', 'mla_w_ukv', 'mla_out_norm', 'w_out',
           'ffn_norm', 'ffn_w_up', 'ffn_conv_w', 'ffn_conv_b', 'ffn_w_down', 'final_norm']
SHARDED = {'w_in': 2, 'ssm_conv_w': 2, 'mla_w_uq': 2, 'mla_w_ukv': 2, 'w_out': 1, 'ffn_w_up': 2, 'ffn_conv_w': 2,
           'ffn_w_down': 1}
VPU_WEIGHTS = ('ssm_conv_w', 'ffn_conv_w')
EARLY = ('w_in', 'mla_w_uq', 'mla_w_ukv', 'ssm_conv_w')
LATE = ('w_out', 'ffn_w_up', 'ffn_conv_w', 'ffn_w_down')
REPLICATED = [n for n in WEIGHTS if n not in SHARDED]


def _call(body, *, name, out_shape, grid=(), in_specs=None, out_specs=None, scratch=(), sem=None, **kw):
    params = dict(vmem_limit_bytes=VMEM_LIMIT)
    if sem is not None:
        params['dimension_semantics'] = sem
    return pl.pallas_call(body, name=name, out_shape=out_shape, grid=grid, in_specs=in_specs, out_specs=out_specs,
                          scratch_shapes=list(scratch), compiler_params=pltpu.CompilerParams(**params), **kw)


def _tile(n, prefs):
    for t in prefs:
        if n % t == 0:
            return t
    return n


def _rows(s, w):
    rows = ROW_BLK
    while rows * 2 <= s and s % (rows * 2) == 0 and rows * 2 * w * 4 <= ROW_BLOCK_BYTES:
        rows *= 2
    return _tile(s, (rows,))


def _dot(a, b, dims, precision=None):
    return lax.dot_general(a, b, (dims, ((), ())), preferred_element_type=F32, precision=precision)


def _nn(a, b, precision=None):
    return _dot(a, b, ((1,), (0,)), precision)


def _nt(a, b, precision=None):
    return _dot(a, b, ((1,), (1,)), precision)


def _tn(a, b, precision=None):
    return _dot(a, b, ((0,), (0,)), precision)


def _mxu(f):
    return lambda a, b: f(a.astype(MXU_DTYPE), b.astype(MXU_DTYPE))


_bnn, _bnt, _btn = _mxu(_nn), _mxu(_nt), _mxu(_tn)


def _split2(x):
    hi = x.astype(MXU_DTYPE)
    lo = (x - hi.astype(F32)).astype(MXU_DTYPE)
    return hi, lo


def _sigmoid(x):
    return 0.5 * jnp.tanh(0.5 * x) + 0.5


def _softplus(x):
    return jnp.maximum(x, 0.0) + jnp.log1p(jnp.exp(-jnp.abs(x)))


def _softplus_att(x):
    return jnp.maximum(x, 0.0) + jnp.log(1.0 + jnp.exp(-jnp.abs(x)))


def _cum(x, u):
    rows, b = x.shape[0], u.shape[0]
    n = x.shape[1] // b
    hi, lo = _split2(x)
    stack = [part[:, t * b:(t + 1) * b] for part in (hi, lo) for t in range(n)]
    r = _nn(jnp.concatenate(stack, axis=0), u)
    return jnp.concatenate([r[t * rows:(t + 1) * rows] + r[(n + t) * rows:(n + t + 1) * rows] for t in range(n)], axis=1)


def _diagonal_group(qi, tile, carry, width):
    base = (qi // width) * width
    return lax.switch(qi - base, [lambda c, n=n: tile(base, c, True, n) for n in range(1, width + 1)], carry)


def _causal_loop(qi, tile, carry, width, first=0):
    carry = lax.fori_loop(first, qi // width, lambda i, c: tile(i * width, c, False, width), carry)
    return _diagonal_group(qi, tile, carry, width)


def _spent_loop(qi, tile, carry, width, live):
    carry = _diagonal_group(qi, tile, carry, width)
    step = lambda st: (st[0] - 1, tile((st[0] - 1) * width, st[1], False, width))
    return lax.while_loop(lambda st: (st[0] > 0) & live(st[1]), step, (qi // width, carry))


def _causal_mask(blk, width, qi, kb, heads, strict, keys_on_rows=False):
    shape = (width * blk, blk) if keys_on_rows else (heads * blk, width * blk)
    q_idx = lax.broadcasted_iota(jnp.int32, shape, 1 if keys_on_rows else 0)
    k_idx = lax.broadcasted_iota(jnp.int32, shape, 0 if keys_on_rows else 1)
    if heads > 1:
        q_idx = q_idx % blk
    gap = (qi - kb) * blk
    return k_idx < q_idx + gap if strict else k_idx <= q_idx + gap


def mm(a, b, *, name, ta=False, tb=False, res=None, out_dtype=F32, ab=None, bb=None, precision=None):
    a2, b2 = a.shape[-2:], b.shape[-2:]
    (kdim, m) = a2 if ta else a2[::-1]
    (n, k2) = b2 if tb else b2[::-1]
    assert kdim == k2, (a.shape, b.shape, ta, tb)
    assert (ab == 'k') == (bb == 'k')
    kb = ab == 'k'
    nb = a.shape[0] if ab == 'o' else (b.shape[0] if bb == 'o' else None)
    tm, tn = _tile(m, MM_TM if res is not None else (2 * MM_TM[0],) + MM_TM), _tile(n, MM_TN)
    tk = kdim if kb else _tile(kdim, MM_TK)
    nk = a.shape[0] if kb else kdim // tk
    dims = ((0 if ta else 1,), (1 if tb else 0,))
    op_dtype = F32 if precision is not None else MXU_DTYPE

    def body(*refs):
        a_ref, b_ref = refs[0], refs[1]
        r_ref = refs[2] if res is not None else None
        o_ref = refs[3] if res is not None else refs[2]
        part = _dot(a_ref[...].astype(op_dtype), b_ref[...].astype(op_dtype), dims, precision)

        def finish(out):
            if res is not None:
                out = out + r_ref[...]
            o_ref[...] = out.astype(out_dtype)

        if nk == 1:
            finish(part)
            return
        acc = refs[-1]
        k = pl.program_id(3)

        @pl.when(k == 0)
        def _():
            acc[...] = part

        @pl.when(k > 0)
        def _():
            acc[...] += part

        @pl.when(k == nk - 1)
        def _():
            finish(acc[...])

    def spec(blk, idx, how):
        if how is None:
            return pl.BlockSpec(blk, idx)
        if how == 'o':
            return pl.BlockSpec((None,) + blk, lambda p, i, j, k: (p,) + idx(p, i, j, k))
        return pl.BlockSpec((None,) + blk, lambda p, i, j, k: (k,) + idx(p, i, j, 0))

    a_spec = spec((tk, tm), lambda p, i, j, k: (k, i), ab) if ta else spec((tm, tk), lambda p, i, j, k: (i, k), ab)
    b_spec = spec((tn, tk), lambda p, i, j, k: (j, k), bb) if tb else spec((tk, tn), lambda p, i, j, k: (k, j), bb)
    o_spec = spec((tm, tn), lambda p, i, j, k: (i, j), None if nb is None else 'o')
    ins, specs = [a, b], [a_spec, b_spec]
    if res is not None:
        ins.append(res)
        specs.append(o_spec)
    out_shape = (m, n) if nb is None else (nb, m, n)
    return _call(body, name=name, out_shape=jax.ShapeDtypeStruct(out_shape, out_dtype),
                 grid=(1 if nb is None else nb, m // tm, n // tn, nk), in_specs=specs, out_specs=o_spec,
                 scratch=[] if nk == 1 else [pltpu.VMEM((tm, tn), F32)],
                 sem=("parallel", "parallel", "parallel", "arbitrary"))(*ins)


def rms_fwd(x, g, *, name, gate=None, out_dtype=F32):
    s, w = x.shape
    bs = _rows(s, w)

    def body(*refs):
        if gate is None:
            x_ref, g_ref, o_ref = refs
            u = x_ref[...]
        else:
            x_ref, z_ref, g_ref, o_ref = refs
            z = z_ref[...]
            u = x_ref[...] * (z * _sigmoid(z))
        r = lax.rsqrt(jnp.mean(u * u, axis=1, keepdims=True) + EPS)
        o_ref[...] = (u * r * g_ref[...]).astype(out_dtype)

    row = pl.BlockSpec((bs, w), lambda i: (i, 0))
    vec = pl.BlockSpec((1, w), lambda i: (0, 0))
    ins = [x] + ([] if gate is None else [gate]) + [g.reshape(1, w)]
    specs = [row] + ([] if gate is None else [row]) + [vec]
    return _call(body, name=name, out_shape=jax.ShapeDtypeStruct((s, w), out_dtype), grid=(s // bs,),
                 in_specs=specs, out_specs=row, sem=("parallel",))(*ins)


def rms_bwd(x, g, dy, *, name, gate=None, add=None):
    s, w = x.shape
    bs = _rows(s, w)

    def body(*refs):
        refs = list(refs)
        x_ref = refs.pop(0)
        z_ref = refs.pop(0) if gate is not None else None
        g_ref = refs.pop(0)
        dy_ref = refs.pop(0)
        add_ref = refs.pop(0) if add is not None else None
        dx_ref = refs.pop(0)
        dz_ref = refs.pop(0) if gate is not None else None
        dg_ref = refs.pop(0)
        i = pl.program_id(0)

        @pl.when(i == 0)
        def _():
            dg_ref[...] = jnp.zeros_like(dg_ref)

        xv = x_ref[...]
        if gate is not None:
            z = z_ref[...]
            sg = _sigmoid(z)
            act = z * sg
            u = xv * act
        else:
            u = xv
        r = lax.rsqrt(jnp.mean(u * u, axis=1, keepdims=True) + EPS)
        dy_v = dy_ref[...]
        dyg = dy_v * g_ref[...]
        du = r * dyg - u * (r * r * r * jnp.mean(dyg * u, axis=1, keepdims=True))
        dg_ref[...] += jnp.sum(dy_v * u * r, axis=0, keepdims=True)
        if gate is not None:
            dx = du * act
            dz_ref[...] = du * xv * (sg * (1.0 + z * (1.0 - sg)))
        else:
            dx = du
        if add is not None:
            dx = dx + add_ref[...]
        dx_ref[...] = dx

    row = pl.BlockSpec((bs, w), lambda i: (i, 0))
    vec = pl.BlockSpec((1, w), lambda i: (0, 0))
    ins = [x] + ([] if gate is None else [gate]) + [g.reshape(1, w), dy] + ([] if add is None else [add])
    specs = [row] + ([] if gate is None else [row]) + [vec, row] + ([] if add is None else [row])
    outs = [jax.ShapeDtypeStruct((s, w), F32)] + ([] if gate is None else [jax.ShapeDtypeStruct((s, w), F32)])
    outs.append(jax.ShapeDtypeStruct((1, w), F32))
    ospecs = [row] + ([] if gate is None else [row]) + [vec]
    return _call(body, name=name, out_shape=outs, grid=(s // bs,), in_specs=specs, out_specs=ospecs,
                 sem=("arbitrary",))(*ins)


def loss_head(y, target, *, name):
    s, w = y.shape
    bs = _rows(s, w)
    nb = s // bs

    def body(y_ref, t_ref, dy_ref, loss_ref, acc):
        i = pl.program_id(0)

        @pl.when(i == 0)
        def _():
            acc[...] = jnp.zeros_like(acc)

        e = y_ref[...] - t_ref[...]
        dy_ref[...] = e * (1.0 / w)
        acc[...] += jnp.sum(e * e, axis=0, keepdims=True)

        @pl.when(i == nb - 1)
        def _():
            loss_ref[...] = jnp.sum(acc[...], axis=1, keepdims=True) * (0.5 / w)

    row = pl.BlockSpec((bs, w), lambda i: (i, 0))
    return _call(body, name=name, out_shape=[jax.ShapeDtypeStruct((s, w), F32), jax.ShapeDtypeStruct((1, 1), F32)],
                 grid=(nb,), in_specs=[row, row], out_specs=[row, pl.BlockSpec((1, 1), lambda i: (0, 0))],
                 scratch=[pltpu.VMEM((1, w), F32)], sem=("arbitrary",))(y, target)


def _rope_tables(positions, s):
    inv_freq = 1.0 / (ROPE_THETA ** (jnp.arange(0, MLA_ROPE, 2, dtype=F32) / MLA_ROPE))
    ang = positions.reshape(s, 1).astype(F32) * inv_freq
    cos, sin = jnp.cos(ang), jnp.sin(ang)
    one, zero = jnp.ones((s, MLA_NOPE), F32), jnp.zeros((s, MLA_NOPE), F32)
    cq = jnp.tile(jnp.concatenate([one, cos, cos], axis=1), (1, MLA_HEADS))
    sq = jnp.tile(jnp.concatenate([zero, sin, sin], axis=1), (1, MLA_HEADS))
    pad1, pad0 = jnp.ones((s, LANES - MLA_ROPE), F32), jnp.zeros((s, LANES - MLA_ROPE), F32)
    ct = jnp.concatenate([cos, cos, pad1], axis=1)
    st = jnp.concatenate([sin, sin, pad0], axis=1)
    half = MLA_ROPE // 2

    def swap(width, starts):
        r = np.zeros((width, width), np.float32)
        for o in starts:
            for i in range(half):
                r[o + half + i, o + i] = -1.0
                r[o + i, o + half + i] = 1.0
        return jnp.asarray(r)

    rq = swap(MLA_HEADS * MLA_QK, [h * MLA_QK + MLA_NOPE for h in range(MLA_HEADS)])
    rt = swap(LANES, [0])
    return (cq, sq, rq), (ct, st, rt)


def rope(x, tabs, *, name, backward=False, add=None):
    cos, sin, rot = tabs
    n, s, w = x.shape
    bs = _rows(s, w)

    def body(*refs):
        if add is None:
            x_ref, c_ref, s_ref, r_ref, o_ref = refs
        else:
            x_ref, c_ref, s_ref, r_ref, a_ref, o_ref = refs
        xv = x_ref[0]
        for j in range(1, n):
            xv = xv + x_ref[j]
        if backward:
            out = xv * c_ref[...] + _nt(xv * s_ref[...], r_ref[...], HIGHEST)
        else:
            out = xv * c_ref[...] + _nn(xv, r_ref[...], HIGHEST) * s_ref[...]
        if add is not None:
            out = out + a_ref[...]
        o_ref[...] = out

    row = pl.BlockSpec((bs, w), lambda i: (i, 0))
    ins = [x, cos, sin, rot] + ([] if add is None else [add])
    specs = [pl.BlockSpec((n, bs, w), lambda i: (0, i, 0)), row, row, pl.BlockSpec((w, w), lambda i: (0, 0))]
    specs += [] if add is None else [row]
    return _call(body, name=name, out_shape=jax.ShapeDtypeStruct((s, w), F32), grid=(s // bs,), in_specs=specs,
                 out_specs=row, sem=("parallel",))(*ins)


MESH = pl.DeviceIdType.MESH
HBM = pl.BlockSpec(memory_space=pltpu.HBM)


def _flip(v, bit):
    return 1 - v if bit else v


class Carried:
    def __init__(self, kind, arrays):
        assert kind in ('gather', 'scatter')
        self.kind, self.arrays, self.n = kind, list(arrays), len(arrays)

    @property
    def out_shape(self):
        lead = (N_DEV,) if self.kind == 'gather' else ()
        return [jax.ShapeDtypeStruct(lead + a.shape, a.dtype) for a in self.arrays]

    @property
    def scratch(self):
        return [pltpu.SemaphoreType.DMA((self.n, N_DEV - 1)), pltpu.SemaphoreType.DMA((self.n, N_DEV - 1)),
                pltpu.SemaphoreType.DMA((self.n,))]

    def _copies(self, in_refs, out_refs, sems):
        send_sems, recv_sems, local_sems = sems
        x, y, c = lax.axis_index("x"), lax.axis_index("y"), lax.axis_index("c")
        me = 4 * x + 2 * y + c
        part = (lambda b, p: in_refs[b]) if self.kind == 'gather' else (lambda b, p: in_refs[b].at[p])
        local = [pltpu.make_async_copy(part(b, me), out_refs[b].at[me], local_sems.at[b]) for b in range(self.n)]
        remote = []
        for k in range(1, N_DEV):
            px, py, pc = _flip(x, k & 4), _flip(y, k & 2), _flip(c, k & 1)
            for b in range(self.n):
                remote.append(pltpu.make_async_remote_copy(
                    src_ref=part(b, 4 * px + 2 * py + pc), dst_ref=out_refs[b].at[me], send_sem=send_sems.at[b, k - 1],
                    recv_sem=recv_sems.at[b, k - 1], device_id=(px, py, pc), device_id_type=MESH))
        return local, remote

    def start(self, in_refs, out_refs, sems):
        local, remote = self._copies(in_refs, out_refs, sems)
        for cp in local + remote:
            cp.start()

    def wait(self, in_refs, out_refs, sems):
        local, remote = self._copies(in_refs, out_refs, sems)
        for cp in remote:
            cp.wait_send()
            cp.wait_recv()
        for cp in local:
            cp.wait()


def _first_last_step(n0, n1):
    at = lambda a, b: (pl.program_id(0) == a) & (pl.program_id(1) == b)
    return (lambda: at(0, 0)), (lambda: at(n0 - 1, n1 - 1))


def _carry_call(body, carried, first, last, ins, *, out_shape, in_specs, out_specs, scratch=(), **kw):
    if carried is None:
        return _call(body, out_shape=out_shape, in_specs=in_specs, out_specs=out_specs, scratch=scratch, **kw)(*ins)
    n, n_in, n_out, n_scr = carried.n, len(in_specs), len(out_specs), len(scratch)

    def riding(*refs):
        own_in, ride_in = refs[:n_in], refs[n_in:n_in + n]
        rest = refs[n_in + n:]
        own_out, ride_out = rest[:n_out], rest[n_out:n_out + n]
        own_scr, sems = rest[n_out + n:n_out + n + n_scr], rest[n_out + n + n_scr:]
        pl.when(first())(lambda: carried.start(ride_in, ride_out, sems))
        body(*own_in, *own_out, *own_scr)
        pl.when(last())(lambda: carried.wait(ride_in, ride_out, sems))

    return _call(riding, out_shape=list(out_shape) + carried.out_shape, in_specs=list(in_specs) + [HBM] * n,
                 out_specs=list(out_specs) + [HBM] * n, scratch=list(scratch) + carried.scratch, **kw)(*ins, *carried.arrays)


def _tri(n, op):
    r = lax.broadcasted_iota(jnp.int32, (n, n), 0)
    c = lax.broadcasted_iota(jnp.int32, (n, n), 1)
    return r, c, op(r, c)


def _pair_split(x, first):
    zero = jnp.zeros_like(x)
    return jnp.where(first, x, zero), jnp.where(first, zero, x)


def _sb_specs(s, blk):
    npair = SB_WIDTH // LANES
    q = pl.BlockSpec((blk, LANES), lambda j, i: (i, j))
    k = pl.BlockSpec((s, LANES), lambda j, i: (0, npair + j))
    v = pl.BlockSpec((s, LANES), lambda j, i: (0, 2 * npair + j))
    full = pl.BlockSpec((s, LANES), lambda j, i: (0, j))
    return q, k, v, full


def _stack_heads(x, first):
    return jnp.concatenate(_pair_split(x, first), axis=0)


def _unstack_heads(x, first, blk):
    return jnp.where(first, x[:blk], x[blk:])


def sb_fwd(qkv, *, name, carried=None):
    s = qkv.shape[0]
    blk = _tile(s, (ATT_BLK,))
    scale = SB_DIM ** -0.5
    npair, nq = SB_WIDTH // LANES, s // blk
    assert nq % SB_WIDE == 0

    def body(q_ref, k_ref, v_ref, y_ref):
        qi = pl.program_id(1)
        first = lax.broadcasted_iota(jnp.int32, (blk, LANES), 1) < SB_DIM
        q2 = _stack_heads((q_ref[...].astype(F32) * scale).astype(MXU_DTYPE), first)
        row, col, later_mask = _tri(blk, lambda r, c: r > c)
        u_later = later_mask.astype(MXU_DTYPE)

        def tile(kb, carry, masked, n):
            c, acc = carry
            keys = pl.ds(pl.multiple_of(kb * blk, blk), n * blk)
            z = _nt(q2, k_ref[keys, :])
            sp = _softplus_att(z)
            if masked:
                valid = _causal_mask(blk, n, qi, kb, 2, True)
            spm = jnp.where(valid, sp, 0.0) if masked else sp
            later = _cum(spm, u_later)
            sums = [jnp.sum(spm[:, t * blk:(t + 1) * blk], axis=1, keepdims=True) for t in range(n)]
            after, cols = c, [None] * n
            for t in reversed(range(n)):
                cols[t] = jnp.broadcast_to(after, (2 * blk, blk))
                after = after - sums[t]
            w = jnp.exp((z - sp) - later + (cols[0] if n == 1 else jnp.concatenate(cols, axis=1)))
            if masked:
                w = jnp.where(valid, w, 0.0)
            return after, acc + _nn(w.astype(MXU_DTYPE), v_ref[keys, :])

        zero = (jnp.zeros((2 * blk, 1), F32), jnp.zeros((2 * blk, LANES), F32))
        _, (c, acc) = _spent_loop(qi, tile, zero, SB_WIDE, lambda cr: jnp.max(cr[0]) >= SB_SPENT)
        y_ref[...] = _unstack_heads(acc, first, blk)

    qspec, kspec, vspec, _ = _sb_specs(s, blk)
    return _carry_call(body, carried, *_first_last_step(npair, nq), (qkv, qkv, qkv), name=name,
                       out_shape=[jax.ShapeDtypeStruct((s, SB_WIDTH), F32)], grid=(npair, nq),
                       in_specs=[qspec, kspec, vspec], out_specs=[qspec], sem=("arbitrary", "arbitrary"))


def sb_bwd(qkv, dy, *, name, carried=None):
    s = qkv.shape[0]
    blk = _tile(s, (ATT_BLK,))
    scale = SB_DIM ** -0.5
    npair, nq = SB_WIDTH // LANES, s // blk
    assert nq % SB_WIDE == 0

    def body(q_ref, k_ref, v_ref, dy_ref, dq_ref, dk_ref, dv_ref):
        qi = pl.program_id(1)

        @pl.when(qi == 0)
        def _():
            dk_ref[...] = jnp.zeros_like(dk_ref)
            dv_ref[...] = jnp.zeros_like(dv_ref)

        first = lax.broadcasted_iota(jnp.int32, (blk, LANES), 1) < SB_DIM
        q2 = _stack_heads((q_ref[...].astype(F32) * scale).astype(MXU_DTYPE), first)
        dy2 = _stack_heads(dy_ref[...].astype(MXU_DTYPE), first)
        row, col, incl_mask = _tri(blk, lambda r, c: r <= c)
        u_incl = incl_mask.astype(MXU_DTYPE)
        u_excl = (row < col).astype(MXU_DTYPE)

        def walk(kb, c, masked, n):
            sp = _softplus_att(_nt(q2, k_ref[pl.ds(pl.multiple_of(kb * blk, blk), n * blk), :]))
            if masked:
                sp = jnp.where(_causal_mask(blk, n, qi, kb, 2, True), sp, 0.0)
            for t in reversed(range(n)):
                c = c - jnp.sum(sp[:, t * blk:(t + 1) * blk], axis=1, keepdims=True)
            return c

        start, tv = _spent_loop(qi, walk, jnp.zeros((2 * blk, 1), F32), SB_WIDE, lambda c: jnp.max(c) >= SB_SPENT)

        def prefixed(x, carry):
            n = x.shape[1] // blk
            cols = []
            for t in range(n):
                cols.append(jnp.broadcast_to(carry, (2 * blk, blk)))
                carry = carry + jnp.sum(x[:, t * blk:(t + 1) * blk], axis=1, keepdims=True)
            return (cols[0] if n == 1 else jnp.concatenate(cols, axis=1)), carry

        def tile(kb, carry, masked, n):
            p, gc, dq = carry
            keys = pl.ds(pl.multiple_of(kb * blk, blk), n * blk)
            kv = k_ref[keys, :]
            z = _nt(q2, kv)
            dw = _nt(dy2, v_ref[keys, :])
            sp = _softplus_att(z)
            if masked:
                valid = _causal_mask(blk, n, qi, kb, 2, True)
            spm = jnp.where(valid, sp, 0.0) if masked else sp
            before, p = prefixed(spm, p)
            w = jnp.exp((z - sp) + (_cum(spm, u_incl) + before))
            if masked:
                w = jnp.where(valid, w, 0.0)
            g = w * dw
            gbefore, gc = prefixed(g, gc)
            gb = g.astype(MXU_DTYPE)
            gin = _nn(jnp.concatenate([gb[:, t * blk:(t + 1) * blk] for t in range(n)], axis=0), u_excl)
            gex = gbefore + jnp.concatenate([gin[t * 2 * blk:(t + 1) * 2 * blk] for t in range(n)], axis=1)
            keep = jnp.exp(-spm)
            dz = keep * (g + gex) - gex
            if masked:
                dz = jnp.where(valid, dz, 0.0)
            dzb = dz.astype(MXU_DTYPE)
            dk_ref[keys, :] += _tn(dzb, q2)
            dv_ref[keys, :] += _tn(w.astype(MXU_DTYPE), dy2)
            return p, gc, dq + _nn(dzb, kv)

        zero = jnp.zeros((2 * blk, 1), F32)
        _, _, dq = _causal_loop(qi, tile, (tv, zero, jnp.zeros((2 * blk, LANES), F32)), SB_WIDE, first=start)
        dq_ref[...] = _unstack_heads(dq, first, blk) * scale

    qspec, kspec, vspec, full = _sb_specs(s, blk)
    out = jax.ShapeDtypeStruct((s, SB_WIDTH), F32)
    return _carry_call(body, carried, *_first_last_step(npair, nq), (qkv, qkv, qkv, dy), name=name,
                       out_shape=[out, out, out], grid=(npair, nq), in_specs=[qspec, kspec, vspec, qspec],
                       out_specs=[qspec, full, full], sem=("arbitrary", "arbitrary"))


ATT_PAIR = 2


def _mla_specs(s, blk, dk, dv):
    q = pl.BlockSpec((ATT_PAIR, blk, dk), lambda hp, i: (hp, i, 0))
    k = pl.BlockSpec((ATT_PAIR, s, dk), lambda hp, i: (hp, 0, 0))
    v = pl.BlockSpec((ATT_PAIR, s, dv), lambda hp, i: (hp, 0, 0))
    y = pl.BlockSpec((ATT_PAIR, blk, dv), lambda hp, i: (hp, i, 0))
    lse = pl.BlockSpec((ATT_PAIR, blk, LANES), lambda hp, i: (hp, i, 0))
    return q, k, v, y, lse


def mla_fwd(q, k, v, *, name, carried=None):
    h, s, dk = q.shape
    dv = v.shape[-1]
    blk = _tile(s, (ATT_BLK,))
    scale = dk ** -0.5
    assert (s // blk) % MLA_WIDE == 0

    def body(q_ref, k_ref, v_ref, y_ref, l_ref):
        qi = pl.program_id(1)

        def tile(kb, carry, masked, n):
            keys = pl.ds(pl.multiple_of(kb * blk, blk), n * blk)
            out = []
            for hh in range(ATT_PAIR):
                m, l, acc = carry[hh]
                sc = _nt(q_ref[hh], k_ref[hh, keys, :]) * scale
                if masked:
                    sc = jnp.where(_causal_mask(blk, n, qi, kb, 1, False), sc, -1e30)
                m2 = jnp.maximum(m, jnp.max(sc, axis=1, keepdims=True))
                p = jnp.exp(sc - m2)
                a = jnp.exp(m - m2)
                out.append((m2, a * l + jnp.sum(p, axis=1, keepdims=True),
                            a * acc + _nn(p.astype(MXU_DTYPE), v_ref[hh, keys, :])))
            return tuple(out)

        init = (jnp.full((blk, 1), -1e30, F32), jnp.zeros((blk, 1), F32), jnp.zeros((blk, dv), F32))
        for hh, (m, l, acc) in enumerate(_causal_loop(qi, tile, (init,) * ATT_PAIR, MLA_WIDE)):
            y_ref[hh] = acc / l
            l_ref[hh] = jnp.broadcast_to(m + jnp.log(l), (blk, LANES))

    qspec, kspec, vspec, yspec, lspec = _mla_specs(s, blk, dk, dv)
    grid = (h // ATT_PAIR, s // blk)
    return _carry_call(body, carried, *_first_last_step(*grid), (q, k, v), name=name,
                       out_shape=[jax.ShapeDtypeStruct((h, s, dv), F32), jax.ShapeDtypeStruct((h, s, LANES), F32)],
                       grid=grid, in_specs=[qspec, kspec, vspec], out_specs=[yspec, lspec],
                       sem=("arbitrary", "arbitrary"))


def mla_bwd(q, k, v, y, dy, lse, *, name):
    h, s, dk = q.shape
    dv = v.shape[-1]
    blk = _tile(s, (ATT_BLK,))
    scale = dk ** -0.5
    assert (s // blk) % MLA_WIDE == 0

    def body(q_ref, k_ref, v_ref, y_ref, dy_ref, l_ref, dq_ref, dk_ref, dv_ref):
        qi = pl.program_id(1)

        @pl.when(qi == 0)
        def _():
            dk_ref[...] = jnp.zeros_like(dk_ref)
            dv_ref[...] = jnp.zeros_like(dv_ref)

        as_row = lambda col: jnp.transpose(jnp.broadcast_to(col, (blk, LANES)))[0:1, :]
        dyv = [dy_ref[hh].astype(MXU_DTYPE) for hh in range(ATT_PAIR)]
        delta = [as_row(jnp.sum(dy_ref[hh] * y_ref[hh], axis=1, keepdims=True)) for hh in range(ATT_PAIR)]
        lv = [as_row(l_ref[hh, :, 0:1]) for hh in range(ATT_PAIR)]
        def tile(kb, dqs, masked, n):
            keys = pl.ds(pl.multiple_of(kb * blk, blk), n * blk)
            out = []
            for hh in range(ATT_PAIR):
                qv = q_ref[hh]
                kv = k_ref[hh, keys, :]
                p = jnp.exp(_nt(kv, qv) * scale - lv[hh])
                if masked:
                    p = jnp.where(_causal_mask(blk, n, qi, kb, 1, False, keys_on_rows=True), p, 0.0)
                ds = (p * (_nt(v_ref[hh, keys, :], dyv[hh]) - delta[hh])).astype(MXU_DTYPE)
                dk_ref[hh, keys, :] += _nn(ds, qv) * scale
                dv_ref[hh, keys, :] += _nn(p.astype(MXU_DTYPE), dyv[hh])
                out.append(dqs[hh] + _tn(ds, kv))
            return tuple(out)

        for hh, dq in enumerate(_causal_loop(qi, tile, (jnp.zeros((blk, dk), F32),) * ATT_PAIR, MLA_WIDE)):
            dq_ref[hh] = dq * scale

    qspec, kspec, vspec, yspec, lspec = _mla_specs(s, blk, dk, dv)
    return _call(body, name=name,
                 out_shape=[jax.ShapeDtypeStruct((h, s, dk), F32), jax.ShapeDtypeStruct((h, s, dk), F32),
                            jax.ShapeDtypeStruct((h, s, dv), F32)],
                 grid=(h // ATT_PAIR, s // blk), in_specs=[qspec, kspec, vspec, yspec, yspec, lspec],
                 out_specs=[qspec, kspec, vspec], sem=("parallel", "arbitrary"))(q, k, v, y, dy, lse)


HALO = 8
CONV_CHUNK = 16


def _conv_tiles(x):
    s, c = x.shape[-2:]
    return s, c, _tile(s, (ROW_BLK,)), _tile(c, (CONV_COLS,))


def _halo_rows(dtype):
    return HALO * 4 // jnp.dtype(dtype).itemsize


def _conv_specs(bs, cw, lead=(), dtype=F32):
    zero = (0,) * len(lead)
    hr = _halo_rows(dtype)
    blk = pl.BlockSpec(lead + (None, bs, cw), lambda p, j, i: zero + (p, i, j))
    halo = pl.BlockSpec(lead + (None, hr, cw), lambda p, j, i: zero + (p, jnp.maximum(i * (bs // hr) - 1, 0), j))
    w = lambda kk: pl.BlockSpec(lead + (None, kk, cw), lambda p, j, i: zero + (p, 0, j))
    return blk, halo, w


def _stage(scr, x_ref, halo_ref, first):
    hr = halo_ref.shape[0]
    scr[0:HALO, :] = jnp.where(first, 0.0, halo_ref[hr - HALO:hr, :].astype(F32))
    scr[HALO:, :] = x_ref[...].astype(F32)


def _shifted(ext, shift):
    return ext[HALO:] if shift == 0 else pltpu.roll(ext, shift, 0)[HALO:]


def _conv_taps(scr, kk, r0):
    ext = scr[pl.ds(r0, CONV_CHUNK + HALO), :]
    return [_shifted(ext, kk - 1 - k) for k in range(kk)]


def _conv_sum(taps, w_ref, b_ref):
    u = b_ref[...] + taps[0] * w_ref[0:1, :]
    for k in range(1, len(taps)):
        u = u + taps[k] * w_ref[k:k + 1, :]
    return u


def _fold(x):
    out = x[0:8]
    for r in range(8, CONV_CHUNK, 8):
        out = out + x[r:r + 8]
    return out


class _TapSums:
    def __init__(self, kk, cw):
        self.w = [jnp.zeros((8, cw), F32) for _ in range(kk)]
        self.b = jnp.zeros((8, cw), F32)

    def add(self, du, taps):
        self.w = [a + _fold(du * t) for a, t in zip(self.w, taps)]
        self.b = self.b + _fold(du)

    def flush(self, dw_ref, db_ref):
        for k, a in enumerate(self.w):
            dw_ref[k:k + 1, :] += jnp.sum(a, axis=0, keepdims=True)
        db_ref[...] += jnp.sum(self.b, axis=0, keepdims=True)


def _silu_grad(u):
    sg = _sigmoid(u)
    return sg * (1.0 + u * (1.0 - sg))


def conv_silu_fwd(x, w, b, *, name):
    s, c, bs, cw = _conv_tiles(x)
    kk = w.shape[1]

    def body(x_ref, h_ref, w_ref, b_ref, o_ref, scr):
        _stage(scr, x_ref, h_ref, pl.program_id(2) == 0)
        for r0 in range(0, bs, CONV_CHUNK):
            u = _conv_sum(_conv_taps(scr, kk, r0), w_ref, b_ref)
            o_ref[pl.ds(r0, CONV_CHUNK), :] = u * _sigmoid(u)

    blk, halo, wspec = _conv_specs(bs, cw, dtype=x.dtype)
    return _call(body, name=name, out_shape=jax.ShapeDtypeStruct(x.shape, F32), grid=(x.shape[0], c // cw, s // bs),
                 in_specs=[blk, halo, wspec(kk), wspec(1)], out_specs=blk, scratch=[pltpu.VMEM((bs + HALO, cw), F32)],
                 sem=("parallel", "parallel", "arbitrary"))(x, x, w, b)


def conv_silu_bwd(x, dy, w, b, *, name):
    s, c, bs, cw = _conv_tiles(x)
    kk = w.shape[1]

    def body(x_ref, h_ref, w_ref, b_ref, dy_ref, du_ref, dw_ref, db_ref, scr):
        i = pl.program_id(2)

        @pl.when(i == 0)
        def _():
            dw_ref[...] = jnp.zeros_like(dw_ref)
            db_ref[...] = jnp.zeros_like(db_ref)

        _stage(scr, x_ref, h_ref, i == 0)
        sums = _TapSums(kk, cw)
        for r0 in range(0, bs, CONV_CHUNK):
            taps = _conv_taps(scr, kk, r0)
            du = dy_ref[pl.ds(r0, CONV_CHUNK), :] * _silu_grad(_conv_sum(taps, w_ref, b_ref))
            du_ref[pl.ds(r0, CONV_CHUNK), :] = du
            sums.add(du, taps)
        sums.flush(dw_ref, db_ref)

    blk, halo, wspec = _conv_specs(bs, cw, dtype=x.dtype)
    return _call(body, name=name,
                 out_shape=[jax.ShapeDtypeStruct(x.shape, F32), jax.ShapeDtypeStruct(w.shape, F32),
                            jax.ShapeDtypeStruct(b.shape, F32)],
                 grid=(x.shape[0], c // cw, s // bs), in_specs=[blk, halo, wspec(kk), wspec(1), blk],
                 out_specs=[blk, wspec(kk), wspec(1)], scratch=[pltpu.VMEM((bs + HALO, cw), F32)],
                 sem=("parallel", "parallel", "arbitrary"))(x, x, w, b, dy)


def _glu_view(a):
    return a.reshape((2, a.shape[0] // 2) + a.shape[1:])


def conv_glu_fwd(x, w, b, *, name):
    s, c, bs, cw = _conv_tiles(x)
    kk = w.shape[1]
    half = x.shape[0] // 2

    def body(x_ref, h_ref, w_ref, b_ref, o_ref, gscr, vscr):
        first = pl.program_id(2) == 0
        _stage(gscr, x_ref.at[0], h_ref.at[0], first)
        _stage(vscr, x_ref.at[1], h_ref.at[1], first)
        for r0 in range(0, bs, CONV_CHUNK):
            gate = _conv_sum(_conv_taps(gscr, kk, r0), w_ref.at[0], b_ref.at[0])
            val = _conv_sum(_conv_taps(vscr, kk, r0), w_ref.at[1], b_ref.at[1])
            o_ref[pl.ds(r0, CONV_CHUNK), :] = (gate * _sigmoid(gate) * val).astype(o_ref.dtype)

    blk, halo, wspec = _conv_specs(bs, cw, lead=(2,), dtype=x.dtype)
    out, _, _ = _conv_specs(bs, cw)
    xv = _glu_view(x)
    return _call(body, name=name, out_shape=jax.ShapeDtypeStruct((half, s, c), MXU_DTYPE), grid=(half, c // cw, s // bs),
                 in_specs=[blk, halo, wspec(kk), wspec(1)], out_specs=out, scratch=[pltpu.VMEM((bs + HALO, cw), F32)] * 2,
                 sem=("parallel", "parallel", "arbitrary"))(xv, xv, _glu_view(w), _glu_view(b))


def conv_glu_bwd(x, da, w, b, *, name):
    s, c, bs, cw = _conv_tiles(x)
    kk = w.shape[1]
    half = x.shape[0] // 2

    def body(x_ref, h_ref, w_ref, b_ref, da_ref, du_ref, dw_ref, db_ref, gscr, vscr):
        i = pl.program_id(2)

        @pl.when(i == 0)
        def _():
            dw_ref[...] = jnp.zeros_like(dw_ref)
            db_ref[...] = jnp.zeros_like(db_ref)

        _stage(gscr, x_ref.at[0], h_ref.at[0], i == 0)
        _stage(vscr, x_ref.at[1], h_ref.at[1], i == 0)
        gsums, vsums = _TapSums(kk, cw), _TapSums(kk, cw)
        for r0 in range(0, bs, CONV_CHUNK):
            gtaps, vtaps = _conv_taps(gscr, kk, r0), _conv_taps(vscr, kk, r0)
            gate = _conv_sum(gtaps, w_ref.at[0], b_ref.at[0])
            val = _conv_sum(vtaps, w_ref.at[1], b_ref.at[1])
            dav = da_ref[pl.ds(r0, CONV_CHUNK), :]
            dgate = dav * val * _silu_grad(gate)
            dval = dav * gate * _sigmoid(gate)
            du_ref[0, pl.ds(r0, CONV_CHUNK), :] = dgate.astype(du_ref.dtype)
            du_ref[1, pl.ds(r0, CONV_CHUNK), :] = dval.astype(du_ref.dtype)
            gsums.add(dgate, gtaps)
            vsums.add(dval, vtaps)
        gsums.flush(dw_ref.at[0], db_ref.at[0])
        vsums.flush(dw_ref.at[1], db_ref.at[1])

    blk, halo, wspec = _conv_specs(bs, cw, lead=(2,), dtype=x.dtype)
    daspec, _, _ = _conv_specs(bs, cw)
    xv, wv, bv = _glu_view(x), _glu_view(w), _glu_view(b)
    du, dw, db = _call(body, name=name,
                       out_shape=[jax.ShapeDtypeStruct(xv.shape, x.dtype), jax.ShapeDtypeStruct(wv.shape, F32),
                                  jax.ShapeDtypeStruct(bv.shape, F32)],
                       grid=(half, c // cw, s // bs), in_specs=[blk, halo, wspec(kk), wspec(1), daspec],
                       out_specs=[blk, wspec(kk), wspec(1)], scratch=[pltpu.VMEM((bs + HALO, cw), F32)] * 2,
                       sem=("parallel", "parallel", "arbitrary"))(xv, xv, wv, bv, da)
    return du.reshape(x.shape), dw.reshape(w.shape), db.reshape(b.shape)


def conv_t(du, w, *, name, out_dtype=F32):
    s, c, bs, cw = _conv_tiles(du)
    kk = w.shape[1]
    nb = s // bs

    def body(d_ref, h_ref, w_ref, o_ref, scr):
        last = pl.program_id(2) == nb - 1
        scr[0:bs, :] = d_ref[...].astype(F32)
        scr[bs:, :] = jnp.where(last, 0.0, h_ref[0:HALO, :].astype(F32))
        for r0 in range(0, bs, CONV_CHUNK):
            ext = scr[pl.ds(r0, CONV_CHUNK + HALO), :]
            ahead = lambda j: ext[:CONV_CHUNK] if j == 0 else pltpu.roll(ext, CONV_CHUNK + HALO - j, 0)[:CONV_CHUNK]
            acc = ahead(kk - 1) * w_ref[0:1, :]
            for k in range(1, kk):
                acc = acc + ahead(kk - 1 - k) * w_ref[k:k + 1, :]
            o_ref[pl.ds(r0, CONV_CHUNK), :] = acc.astype(out_dtype)

    blk, _, wspec = _conv_specs(bs, cw)
    hr = _halo_rows(du.dtype)
    halo = pl.BlockSpec((None, hr, cw), lambda q, j, i: (q, jnp.minimum((i + 1) * (bs // hr), s // hr - 1), j))
    return _call(body, name=name, out_shape=jax.ShapeDtypeStruct(du.shape, out_dtype), grid=(du.shape[0], c // cw, nb),
                 in_specs=[blk, halo, wspec(kk)], out_specs=blk, scratch=[pltpu.VMEM((bs + HALO, cw), F32)],
                 sem=("parallel", "parallel", "arbitrary"))(du, du, w)


def _ssd_common(xbc_ref, tail_ref, dtrt_ref, bias_ref, biast_ref, alog_ref, alogt_ref):
    L = SSM_CHUNK
    raw = tail_ref[...] + bias_ref[...]
    dt = _softplus(raw)
    dtt = _softplus(dtrt_ref[...] + biast_ref[...])
    a = -jnp.exp(alog_ref[...])
    at = -jnp.exp(alogt_ref[...])
    row, col, lower = _tri(L, lambda r, c: r >= c)
    tril = lower.astype(F32)
    cs = _nn(tril, dt * a, HIGHEST)
    cst = _nt(dtt * at, tril, HIGHEST)
    bm = [xbc_ref[:, SSM_INNER + g * SSM_N: SSM_INNER + (g + 1) * SSM_N] for g in range(SSM_GROUPS)]
    off = SSM_INNER + SSM_GROUPS * SSM_N
    cm = [xbc_ref[:, off + g * SSM_N: off + (g + 1) * SSM_N] for g in range(SSM_GROUPS)]
    cb = [_bnt(cm[g], bm[g]) for g in range(SSM_GROUPS)]
    return raw, dt, a, lower, tril, cs, cst, bm, cm, cb


def _ssd_head(hh, xbc_ref, dt, cs, cst, lower):
    L = SSM_CHUNK
    ln = DT_LANE + hh
    x = xbc_ref[:, hh * SSM_P:(hh + 1) * SSM_P]
    dtc = dt[:, ln:ln + 1]
    csc = cs[:, ln:ln + 1]
    csr = cst[hh:hh + 1, :]
    decay = jnp.exp(jnp.where(lower, csc - csr, -1e30))
    last = cs[L - 1:L, ln:ln + 1]
    return x, dtc, csc, decay, jnp.exp(csc), jnp.exp(last - csc), jnp.exp(last)


def _ssd_inputs(tail, dt_bias, a_log, d_skip):
    H = SSM_HEADS
    lanes = lambda vec: jnp.pad(vec.reshape(1, H), ((0, 0), (DT_LANE, LANES - DT_LANE - H)))
    return (tail, tail[:, DT_LANE:DT_LANE + H].T, lanes(dt_bias), dt_bias.reshape(H, 1), lanes(a_log),
            a_log.reshape(H, 1), lanes(d_skip))


def ssd_fwd(xbc, tail, dt_bias, a_log, d_skip, *, name, carried=None):
    s = xbc.shape[0]
    L, H, P, N = SSM_CHUNK, SSM_HEADS, SSM_P, SSM_N
    nc = s // L

    def body(xbc_ref, tail_ref, dtrt_ref, bias_ref, biast_ref, alog_ref, alogt_ref, d_ref, y_ref, hp_ref, state):
        @pl.when(pl.program_id(0) == 0)
        def _():
            state[...] = jnp.zeros_like(state)

        raw, dt, a, lower, tril, cs, cst, bm, cm, cb = _ssd_common(
            xbc_ref, tail_ref, dtrt_ref, bias_ref, biast_ref, alog_ref, alogt_ref)
        for hh in range(H):
            g = hh // (H // SSM_GROUPS)
            x, dtc, csc, decay, e, tau, gamma = _ssd_head(hh, xbc_ref, dt, cs, cst, lower)
            xdt = x * dtc
            hprev = state[hh]
            hp_ref[hh] = hprev
            skip = d_ref[:, DT_LANE + hh:DT_LANE + hh + 1]
            y = _bnn(cb[g] * decay, xdt) + _bnn(cm[g], hprev) * e + x * skip
            y_ref[:, hh * P:(hh + 1) * P] = y
            state[hh] = hprev * gamma + _btn(bm[g] * tau, xdt)

    row = lambda w: pl.BlockSpec((L, w), lambda c: (c, 0))
    small = lambda shp: pl.BlockSpec(shp, lambda c: (0, 0))
    return _carry_call(
        body, carried, lambda: pl.program_id(0) == 0, lambda: pl.program_id(0) == nc - 1,
        (xbc, *_ssd_inputs(tail, dt_bias, a_log, d_skip)), name=name,
        out_shape=[jax.ShapeDtypeStruct((s, SSM_INNER), F32), jax.ShapeDtypeStruct((nc, H, N, P), F32)], grid=(nc,),
        in_specs=[row(SSM_CONV_DIM), row(LANES), pl.BlockSpec((H, L), lambda c: (0, c)), small((1, LANES)),
                  small((H, 1)), small((1, LANES)), small((H, 1)), small((1, LANES))],
        out_specs=[row(SSM_INNER), pl.BlockSpec((None, H, N, P), lambda c: (c, 0, 0, 0))],
        scratch=[pltpu.VMEM((H, N, P), F32)], sem=("arbitrary",))


def ssd_bwd(xbc, tail, dt_bias, a_log, d_skip, hprev_all, dy, *, name, carried=None):
    s = xbc.shape[0]
    L, H, P, N = SSM_CHUNK, SSM_HEADS, SSM_P, SSM_N
    nc = s // L
    hg = H // SSM_GROUPS

    def body(xbc_ref, tail_ref, dtrt_ref, bias_ref, biast_ref, alog_ref, alogt_ref, d_ref, hp_ref, dy_ref,
             dxbc_ref, ddt_ref, dbias_ref, dalog_ref, dd_ref, dstate):
        @pl.when(pl.program_id(0) == 0)
        def _():
            dstate[...] = jnp.zeros_like(dstate)
            dbias_ref[...] = jnp.zeros_like(dbias_ref)
            dalog_ref[...] = jnp.zeros_like(dalog_ref)
            dd_ref[...] = jnp.zeros_like(dd_ref)

        raw, dt, a, lower, tril, cs, cst, bm, cm, cb = _ssd_common(
            xbc_ref, tail_ref, dtrt_ref, bias_ref, biast_ref, alog_ref, alogt_ref)
        lane = lax.broadcasted_iota(jnp.int32, (L, LANES), 1)
        lane1 = lax.broadcasted_iota(jnp.int32, (1, LANES), 1)
        rowi = lax.broadcasted_iota(jnp.int32, (L, 1), 0)
        slot = lax.broadcasted_iota(jnp.int32, (LANES, L), 0)
        col_sums = jnp.zeros((LANES, L), F32)
        dcs_all = jnp.zeros((L, LANES), F32)
        ddt_x = jnp.zeros((L, LANES), F32)
        dd_row = jnp.zeros((1, LANES), F32)
        dbm = [jnp.zeros((L, N), F32) for _ in range(SSM_GROUPS)]
        dcm = [jnp.zeros((L, N), F32) for _ in range(SSM_GROUPS)]
        dcb = [jnp.zeros((L, L), F32) for _ in range(SSM_GROUPS)]
        for hh in range(H):
            g = hh // hg
            ln = DT_LANE + hh
            x, dtc, csc, decay, e, tau, gamma = _ssd_head(hh, xbc_ref, dt, cs, cst, lower)
            xdt = x * dtc
            hprev = hp_ref[hh]
            dhn = dstate[hh]
            dyh = dy_ref[:, hh * P:(hh + 1) * P]
            m = cb[g] * decay
            dxdt = _btn(m, dyh) + _bnn(bm[g] * tau, dhn)
            dm = jnp.where(lower, _bnt(dyh, xdt), 0.0)
            dcb[g] = dcb[g] + dm * decay
            dseg = dm * m
            dcs = jnp.sum(dseg, axis=1, keepdims=True)
            col_sums = jnp.where(slot == ln, jnp.sum(dseg, axis=0, keepdims=True), col_sums)
            edy = e * dyh
            dcm[g] = dcm[g] + _bnt(edy, hprev)
            dcs = dcs + e * jnp.sum(dyh * _bnn(cm[g], hprev), axis=1, keepdims=True)
            xdh = _bnt(xdt, dhn)
            dbm[g] = dbm[g] + tau * xdh
            dtau_tau = jnp.sum(bm[g] * xdh, axis=1, keepdims=True) * tau
            dlast = jnp.sum(dtau_tau, axis=0, keepdims=True) + gamma * jnp.sum(dhn * hprev, keepdims=True)
            dcs = dcs - dtau_tau + jnp.where(rowi == L - 1, dlast, 0.0)
            dstate[hh] = gamma * dhn + _btn(cm[g], edy)
            dcs_all = jnp.where(lane == ln, dcs, dcs_all)
            ddt_x = jnp.where(lane == ln, jnp.sum(dxdt * x, axis=1, keepdims=True), ddt_x)
            dxbc_ref[:, hh * P:(hh + 1) * P] = dxdt * dtc + d_ref[:, ln:ln + 1] * dyh
            dd_row = jnp.where(lane1 == ln, jnp.sum(dyh * x, keepdims=True), dd_row)
        off = SSM_INNER + SSM_GROUPS * SSM_N
        for g in range(SSM_GROUPS):
            dxbc_ref[:, SSM_INNER + g * N: SSM_INNER + (g + 1) * N] = dbm[g] + _btn(dcb[g], cm[g])
            dxbc_ref[:, off + g * N: off + (g + 1) * N] = dcm[g] + _bnn(dcb[g], bm[g])
        dcs_all = dcs_all - jnp.transpose(col_sums)
        dda = _tn(tril, dcs_all, HIGHEST)
        head_lane = (lane >= DT_LANE) & (lane < DT_LANE + H)
        draw = jnp.where(head_lane, (dda * a + ddt_x) * _sigmoid(raw), 0.0)
        ddt_ref[...] = draw
        dbias_ref[...] += jnp.sum(draw, axis=0, keepdims=True)
        dalog_ref[...] += jnp.sum(jnp.where(head_lane, dda * dt, 0.0), axis=0, keepdims=True) * a
        dd_ref[...] += dd_row

    rev = lambda c: nc - 1 - c
    row = lambda w: pl.BlockSpec((L, w), lambda c: (rev(c), 0))
    small = lambda shp: pl.BlockSpec(shp, lambda c: (0, 0))
    acc = pl.BlockSpec((1, LANES), lambda c: (0, 0))
    vec = jax.ShapeDtypeStruct((1, LANES), F32)
    return _carry_call(
        body, carried, lambda: pl.program_id(0) == 0, lambda: pl.program_id(0) == nc - 1,
        (xbc, *_ssd_inputs(tail, dt_bias, a_log, d_skip), hprev_all, dy), name=name,
        out_shape=[jax.ShapeDtypeStruct((s, SSM_CONV_DIM), F32), jax.ShapeDtypeStruct((s, LANES), F32), vec, vec, vec],
        grid=(nc,),
        in_specs=[row(SSM_CONV_DIM), row(LANES), pl.BlockSpec((H, L), lambda c: (0, rev(c))), small((1, LANES)),
                  small((H, 1)), small((1, LANES)), small((H, 1)), small((1, LANES)),
                  pl.BlockSpec((None, H, N, P), lambda c: (rev(c), 0, 0, 0)), row(SSM_INNER)],
        out_specs=[row(SSM_CONV_DIM), row(LANES), acc, acc, acc],
        scratch=[pltpu.VMEM((H, N, P), F32)], sem=("arbitrary",))


def _heads(x2d, n, d):
    s = x2d.shape[0]
    return x2d.reshape(s, n, d).transpose(1, 0, 2)


def _unheads(x3d):
    n, s, d = x3d.shape
    return x3d.transpose(1, 0, 2).reshape(s, n * d)


def layer_fwd(h, p, tabs, li, gather_late=None):
    s = h.shape[0]
    tabq, tabt = tabs
    nm = lambda t: f"L{li}_{t}"
    r = {'h': h}
    hn = rms_fwd(h, p['mix_norm'], name=nm('mixnorm'), out_dtype=MXU_DTYPE)
    proj = mm(hn, p['w_in'], name=nm('proj'))
    r.update(hn=hn, proj=proj)
    qkv = proj[:, :3 * SB_WIDTH].astype(MXU_DTYPE)
    riders = {} if gather_late is None else {
        'sb': ('w_out', 'ffn_conv_w'), 'ssd': ('ffn_w_down',), 'mla': ('ffn_w_up',)}
    ride = lambda k: Carried('gather', [gather_late[n] for n in riders[k]]) if riders else None
    got = {}
    ya, *rest = sb_fwd(qkv, name=nm('sb_fwd'), carried=ride('sb'))
    got.update(zip(riders.get('sb', ()), rest))
    yan = rms_fwd(ya, p['sb_out_norm'], name=nm('sbnorm'), out_dtype=MXU_DTYPE)
    r.update(qkv=qkv, ya=ya)
    z = proj[:, 768:1280]
    xbc = proj[None, :, 1280:2048]
    tail = proj[:, TAIL:TAIL + LANES]
    xbc_act = conv_silu_fwd(xbc, p['ssm_conv_w'], p['ssm_conv_b'], name=nm('ssmconv'))[0]
    y_ssm, hprev, *rest = ssd_fwd(xbc_act, tail, p['ssm_dt_bias'], p['ssm_a_log'], p['ssm_d'], name=nm('ssd_fwd'),
                                  carried=ride('ssd'))
    got.update(zip(riders.get('ssd', ()), rest))
    ybn = rms_fwd(y_ssm, p['ssm_out_norm'], name=nm('ssmnorm'), gate=z, out_dtype=MXU_DTYPE)
    r.update(z=z, xbc=xbc, tail=tail, xbc_act=xbc_act, y_ssm=y_ssm, hprev=hprev)
    cq = proj[:, 2048:2304]
    ckv = proj[:, 2304:2432]
    qn = rms_fwd(cq, p['mla_q_norm'], name=nm('qnorm'), out_dtype=MXU_DTYPE)
    q_r = rope(mm(qn, p['mla_w_uq'], name=nm('uq'))[None], tabq, name=nm('ropeq'))
    kvn = rms_fwd(ckv, p['mla_kv_norm'], name=nm('kvnorm'), out_dtype=MXU_DTYPE)
    kv = mm(kvn, p['mla_w_ukv'], name=nm('ukv'))
    k_pe = rope(tail[None], tabt, name=nm('ropek'))[:, :MLA_ROPE]
    qh = _heads(q_r, MLA_HEADS, MLA_QK).astype(MXU_DTYPE)
    kvh = _heads(kv, MLA_HEADS, MLA_NOPE + MLA_V)
    kh = jnp.concatenate([kvh[..., :MLA_NOPE], jnp.broadcast_to(k_pe[None], (MLA_HEADS, s, MLA_ROPE))],
                         axis=-1).astype(MXU_DTYPE)
    vh = kvh[..., MLA_NOPE:].astype(MXU_DTYPE)
    yc_h, lse, *rest = mla_fwd(qh, kh, vh, name=nm('mla_fwd'), carried=ride('mla'))
    got.update(zip(riders.get('mla', ()), rest))
    late = None
    if riders:
        late = assemble_late(got)
        p = dict(p, **{n: late[n][li] for n in LATE})
    yc = _unheads(yc_h)
    ycn = rms_fwd(yc, p['mla_out_norm'], name=nm('mlanorm'), out_dtype=MXU_DTYPE)
    r.update(cq=cq, ckv=ckv, qn=qn, kvn=kvn, qh=qh, kh=kh, vh=vh, yc_h=yc_h, yc=yc, lse=lse)
    ycat = jnp.concatenate([yan, ybn, ycn], axis=1)
    h1 = mm(ycat, p['w_out'], name=nm('outproj'), res=h)
    hn2 = rms_fwd(h1, p['ffn_norm'], name=nm('ffnnorm'), out_dtype=MXU_DTYPE)
    up = mm(hn2, p['ffn_w_up'], name=nm('up'), bb='o', out_dtype=MXU_DTYPE)
    act = conv_glu_fwd(up, p['ffn_conv_w'], p['ffn_conv_b'], name=nm('glu'))
    h2 = mm(act, p['ffn_w_down'], name=nm('down'), ab='k', bb='k', res=h1)
    r.update(ycat=ycat, h1=h1, hn2=hn2, up=up, act=act)
    return h2, r, p, late


def layer_bwd(dh2, p, r, tabs, li, scatter_late=None):
    s = dh2.shape[0]
    tabq, tabt = tabs
    nm = lambda t: f"L{li}_{t}"
    g = {}
    dact = mm(dh2, p['ffn_w_down'], name=nm('d_down_x'), tb=True, bb='o')
    g['ffn_w_down'] = mm(r['act'], dh2, name=nm('d_down_w'), out_dtype=WIRE_DTYPE, ta=True, ab='o')
    du, g['ffn_conv_w'], g['ffn_conv_b'] = conv_glu_bwd(r['up'], dact, p['ffn_conv_w'], p['ffn_conv_b'], name=nm('d_glu'))
    dup = conv_t(du, p['ffn_conv_w'], name=nm('d_ffnconv'), out_dtype=MXU_DTYPE)
    g['ffn_w_up'] = mm(r['hn2'], dup, name=nm('d_up_w'), out_dtype=WIRE_DTYPE, ta=True, bb='o')
    dhn2 = mm(dup, p['ffn_w_up'], name=nm('d_up_x'), tb=True, ab='k', bb='k')
    dh1, dg = rms_bwd(r['h1'], p['ffn_norm'], dhn2, name=nm('d_ffnnorm'), add=dh2)
    g['ffn_norm'] = dg[0]
    dycat = mm(dh1, p['w_out'], name=nm('d_out_x'), tb=True)
    g['w_out'] = mm(r['ycat'], dh1, name=nm('d_out_w'), out_dtype=WIRE_DTYPE, ta=True)
    dya, dg = rms_bwd(r['ya'], p['sb_out_norm'], dycat[:, :256], name=nm('d_sbnorm'))
    g['sb_out_norm'] = dg[0]
    riders = {} if scatter_late is None else {'sb': ('ffn_w_up',), 'ssd': ('ffn_w_down', 'w_out', 'ffn_conv_w')}
    parts = owner_parts_late([g] + list(scatter_late)) if riders else None
    ride = lambda k: Carried('scatter', [parts[n].astype(WIRE_DTYPE) for n in riders[k]]) if riders else None
    recv_late = {} if riders else None
    dq, dk, dv, *rest = sb_bwd(r['qkv'], dya, name=nm('sb_bwd'), carried=ride('sb'))
    if riders:
        recv_late.update(zip(riders['sb'], rest))
    dyssm, dz, dg = rms_bwd(r['y_ssm'], p['ssm_out_norm'], dycat[:, 256:768], name=nm('d_ssmnorm'), gate=r['z'])
    g['ssm_out_norm'] = dg[0]
    dxbc_act, ddt_tail, dbias, dalog, dd, *rest = ssd_bwd(r['xbc_act'], r['tail'], p['ssm_dt_bias'], p['ssm_a_log'],
                                                          p['ssm_d'], r['hprev'], dyssm, name=nm('ssd_bwd'),
                                                          carried=ride('ssd'))
    if riders:
        recv_late.update(zip(riders['ssd'], rest))
    hl = slice(DT_LANE, DT_LANE + SSM_HEADS)
    g['ssm_dt_bias'], g['ssm_a_log'], g['ssm_d'] = dbias[0, hl], dalog[0, hl], dd[0, hl]
    dxbc_u, g['ssm_conv_w'], g['ssm_conv_b'] = conv_silu_bwd(r['xbc'], dxbc_act[None], p['ssm_conv_w'], p['ssm_conv_b'],
                                                             name=nm('d_ssmact'))
    dxbc = conv_t(dxbc_u, p['ssm_conv_w'], name=nm('d_ssmconv'))[0]
    dyc, dg = rms_bwd(r['yc'], p['mla_out_norm'], dycat[:, 768:], name=nm('d_mlanorm'))
    g['mla_out_norm'] = dg[0]
    dqh, dkh, dvh = mla_bwd(r['qh'], r['kh'], r['vh'], r['yc_h'], _heads(dyc, MLA_HEADS, MLA_V), r['lse'], name=nm('mla_bwd'))
    dq_c = rope(_unheads(dqh)[None], tabq, name=nm('d_ropeq'), backward=True)
    g['mla_w_uq'] = mm(r['qn'], dq_c, name=nm('d_uq_w'), out_dtype=WIRE_DTYPE, ta=True)
    dcq, dg = rms_bwd(r['cq'], p['mla_q_norm'], mm(dq_c, p['mla_w_uq'], name=nm('d_uq_x'), tb=True), name=nm('d_qnorm'))
    g['mla_q_norm'] = dg[0]
    dkv = _unheads(jnp.concatenate([dkh[..., :MLA_NOPE], dvh], axis=-1))
    g['mla_w_ukv'] = mm(r['kvn'], dkv, name=nm('d_ukv_w'), out_dtype=WIRE_DTYPE, ta=True)
    dckv, dg = rms_bwd(r['ckv'], p['mla_kv_norm'], mm(dkv, p['mla_w_ukv'], name=nm('d_ukv_x'), tb=True), name=nm('d_kvnorm'))
    g['mla_kv_norm'] = dg[0]
    dkpe = jnp.pad(dkh[..., MLA_NOPE:], ((0, 0), (0, 0), (0, LANES - MLA_ROPE)))
    dtail = rope(dkpe, tabt, name=nm('d_ropek'), backward=True, add=ddt_tail)
    dproj = jnp.concatenate([dq, dk, dv, dz, dxbc, dcq, dckv, dtail], axis=1).astype(MXU_DTYPE)
    g['w_in'] = mm(r['hn'], dproj, name=nm('d_proj_w'), out_dtype=WIRE_DTYPE, ta=True)
    dhn = mm(dproj, p['w_in'], name=nm('d_proj_x'), tb=True)
    dh, dg = rms_bwd(r['h'], p['mix_norm'], dhn, name=nm('d_mixnorm'), add=dh1)
    g['mix_norm'] = dg[0]
    return dh, g, recv_late


def _w_in_placement():
    c = np.arange(D_IN)
    dest = np.where(c < 2048, c, np.where(c < 2056, c + (D_IN - 2056), c - 8))
    dest = jnp.asarray(dest.reshape(N_DEV, D_IN // N_DEV, 1), jnp.int32)
    return (dest == jnp.arange(D_IN_PAD, dtype=jnp.int32)).astype(MXU_DTYPE)


def _owner_major(full, axis):
    shp = full.shape
    return jnp.moveaxis(full.reshape(shp[:axis] + (N_DEV, shp[axis] // N_DEV) + shp[axis + 1:]), axis, 0)


def _owner_join(parts, axis):
    moved = jnp.moveaxis(parts, 0, axis)
    shp = moved.shape
    return moved.reshape(shp[:axis] + (shp[axis] * shp[axis + 1],) + shp[axis + 2:])


def assemble_early(gathered, replicated):
    L = DEPTH
    out = dict(replicated)
    out['w_in'] = mm(gathered['w_in'].reshape(N_DEV, L * D_MODEL, D_IN // N_DEV), _w_in_placement(), name='place_w_in',
                     ab='k', bb='k', out_dtype=MXU_DTYPE).reshape(L, D_MODEL, D_IN_PAD)
    out['mla_w_uq'] = _owner_join(gathered['mla_w_uq'], 2)
    out['mla_w_ukv'] = _owner_join(gathered['mla_w_ukv'], 2)
    out['ssm_conv_w'] = _owner_join(gathered['ssm_conv_w'], 2)[:, None]
    out['ssm_conv_b'] = replicated['ssm_conv_b'].reshape(L, 1, 1, SSM_CONV_DIM)
    out['ffn_conv_b'] = replicated['ffn_conv_b'].reshape(L, N_DEV, 1, FF_SHARD)
    return out


def assemble_late(gathered):
    L = DEPTH
    return {'ffn_w_up': jnp.moveaxis(gathered['ffn_w_up'], 1, 0),
            'w_out': _owner_join(gathered['w_out'], 1),
            'ffn_w_down': _owner_join(gathered['ffn_w_down'], 1).reshape(L, N_DEV // 2, FF_SHARD, D_MODEL),
            'ffn_conv_w': jnp.moveaxis(gathered['ffn_conv_w'], 1, 0)}


def owner_parts_late(grads):
    L = DEPTH
    st = lambda n: jnp.stack([g[n] for g in grads])
    return {'ffn_w_up': jnp.moveaxis(st('ffn_w_up'), 1, 0),
            'w_out': _owner_major(st('w_out'), 1),
            'ffn_w_down': _owner_major(st('ffn_w_down').reshape(L, D_FF, D_MODEL), 1),
            'ffn_conv_w': jnp.moveaxis(st('ffn_conv_w'), 1, 0)}


def owner_parts_early(grads):
    L = DEPTH
    st = lambda n: jnp.stack([g[n] for g in grads])
    parts = {
        'w_in': mm(st('w_in').reshape(L * D_MODEL, D_IN_PAD), _w_in_placement(), name='unplace_w_in', tb=True, bb='o',
                   out_dtype=WIRE_DTYPE).reshape(N_DEV, L, D_MODEL, D_IN // N_DEV),
        'mla_w_uq': _owner_major(st('mla_w_uq'), 2),
        'mla_w_ukv': _owner_major(st('mla_w_ukv'), 2),
        'ssm_conv_w': _owner_major(st('ssm_conv_w')[:, 0], 2),
    }
    rep = {n: st(n) for n in REPLICATED if n not in ('final_norm', 'ssm_conv_b', 'ffn_conv_b')}
    rep['ssm_conv_b'] = st('ssm_conv_b').reshape(L, SSM_CONV_DIM)
    rep['ffn_conv_b'] = st('ffn_conv_b').reshape(L, 2 * D_FF)
    return parts, rep


def local_step(x, positions, target, early, late_shards, replicated):
    s = x.shape[0]
    tabs = _rope_tables(positions, s)
    params = assemble_early(early, replicated)
    layer = lambda li: {n: params[n][li] for n in params if n != 'final_norm'}
    h, r0, p0, late = layer_fwd(x, layer(0), tabs, 0, gather_late=late_shards)
    saved = [(p0, r0)]
    for li in range(1, DEPTH):
        h, r, p, _ = layer_fwd(h, dict(layer(li), **{n: late[n][li] for n in LATE}), tabs, li)
        saved.append((p, r))
    y = rms_fwd(h, params['final_norm'], name='finalnorm')
    dy, loss = loss_head(y, target, name='loss')
    dh, dg = rms_bwd(h, params['final_norm'], dy, name='d_finalnorm')
    above = []
    for li in reversed(range(1, DEPTH)):
        dh, g, _ = layer_bwd(dh, *saved[li], tabs, li)
        above.insert(0, g)
    dh, g0, recv_late = layer_bwd(dh, *saved[0], tabs, 0, scatter_late=above)
    parts, rep = owner_parts_early([g0] + above)
    rep['final_norm'] = dg[0]
    return loss[0, 0], dh, parts, recv_late, rep


def all_gather(blocks, *, name):
    n = len(blocks)

    def body(*refs):
        x_refs, out_refs = refs[:n], refs[n:2 * n]
        send_sems, recv_sems, local_sems = refs[2 * n:]
        x, y, c = lax.axis_index("x"), lax.axis_index("y"), lax.axis_index("c")
        me, sibling = (x, y, c), (x, y, 1 - c)
        chips = [(1 - x, y), (x, 1 - y), (1 - x, 1 - y)]

        def slot(b, px, py, pc):
            return out_refs[b].at[4 * px + 2 * py + pc]

        def copy(b, k, blk, to, src=None):
            return pltpu.make_async_remote_copy(src_ref=slot(b, *blk) if src is None else src, dst_ref=slot(b, *blk),
                                                send_sem=send_sems.at[b, k], recv_sem=recv_sems.at[b, k],
                                                device_id=to, device_id_type=MESH)

        mine = [pltpu.make_async_copy(x_refs[b], slot(b, *me), local_sems.at[b]) for b in range(n)]
        for cp in mine:
            cp.start()
        first = []
        for b in range(n):
            first.append(copy(b, 0, me, sibling, src=x_refs[b]))
            first += [copy(b, 1 + j, me, (*chip, c), src=x_refs[b]) for j, chip in enumerate(chips)]
        for cp in first:
            cp.start()
        passed = []
        for j, chip in enumerate(chips):
            for b in range(n):
                copy(b, 1 + j, (*chip, c), me).wait_recv()
                fwd = copy(b, 4 + j, (*chip, c), sibling)
                fwd.start()
                passed.append(fwd)
        for b in range(n):
            copy(b, 0, sibling, me).wait_recv()
            for j, chip in enumerate(chips):
                copy(b, 4 + j, (*chip, 1 - c), me).wait_recv()
        for cp in first + passed:
            cp.wait_send()
        for cp in mine:
            cp.wait()

    return pl.pallas_call(
        body, name=name, out_shape=[jax.ShapeDtypeStruct((N_DEV,) + b.shape, b.dtype) for b in blocks],
        in_specs=[HBM] * n, out_specs=[HBM] * n,
        scratch_shapes=[pltpu.SemaphoreType.DMA((n, 7)), pltpu.SemaphoreType.DMA((n, 7)), pltpu.SemaphoreType.DMA((n,))],
    )(*blocks)


def all_to_all(parts, *, name):
    n = len(parts)

    def body(*refs):
        g_refs, r_refs = refs[:n], refs[n:2 * n]
        send_sems, recv_sems, local_sems = refs[2 * n:]
        x, y, c = lax.axis_index("x"), lax.axis_index("y"), lax.axis_index("c")
        me = 4 * x + 2 * y + c
        mine = [pltpu.make_async_copy(g_refs[b].at[me], r_refs[b].at[me], local_sems.at[b]) for b in range(n)]
        for cp in mine:
            cp.start()
        copies = []
        for k in range(1, N_DEV):
            px, py, pc = _flip(x, k & 4), _flip(y, k & 2), _flip(c, k & 1)
            peer = 4 * px + 2 * py + pc
            for b in range(n):
                cp = pltpu.make_async_remote_copy(src_ref=g_refs[b].at[peer], dst_ref=r_refs[b].at[me],
                                                  send_sem=send_sems.at[b, k - 1], recv_sem=recv_sems.at[b, k - 1],
                                                  device_id=(px, py, pc), device_id_type=MESH)
                cp.start()
                copies.append(cp)
        for cp in copies:
            cp.wait_send()
            cp.wait_recv()
        for cp in mine:
            cp.wait()

    return pl.pallas_call(
        body, name=name, out_shape=[jax.ShapeDtypeStruct(p.shape, p.dtype) for p in parts],
        in_specs=[HBM] * n, out_specs=[HBM] * n,
        scratch_shapes=[pltpu.SemaphoreType.DMA((n, 7)), pltpu.SemaphoreType.DMA((n, 7)), pltpu.SemaphoreType.DMA((n,))],
    )(*parts)


def adamw(parts, w, m, v, *, name):
    r, wd = w.shape
    br = _tile(r, (256, 128, 64, 32, 16, 8))
    c1 = 1.0 - ADAM_B1 ** ADAM_STEP
    c2 = 1.0 - ADAM_B2 ** ADAM_STEP

    def body(p_ref, w_ref, m_ref, v_ref, g_ref, d_ref, mo_ref, vo_ref):
        g = p_ref[0].astype(F32)
        for j in range(1, N_DEV):
            g = g + p_ref[j].astype(F32)
        mn = ADAM_B1 * m_ref[...] + (1.0 - ADAM_B1) * g
        vn = ADAM_B2 * v_ref[...] + (1.0 - ADAM_B2) * (g * g)
        g_ref[...] = g
        mo_ref[...] = mn
        vo_ref[...] = vn
        d_ref[...] = -ADAM_LR * ((mn / c1) / (jnp.sqrt(vn / c2) + ADAM_EPS) + ADAM_WD * w_ref[...])

    blk = pl.BlockSpec((br, wd), lambda i: (i, 0))
    out = jax.ShapeDtypeStruct((r, wd), F32)
    return _call(body, name=name, out_shape=[out] * 4, grid=(r // br,),
                 in_specs=[pl.BlockSpec((N_DEV, br, wd), lambda i: (0, i, 0)), blk, blk, blk], out_specs=[blk] * 4,
                 sem=("parallel",))(parts, w, m, v)


def _pack(arrs):
    flat = jnp.concatenate([a.reshape(-1) for a in arrs])
    rows = -(-flat.shape[0] // (8 * FLAT_W)) * 8
    return jnp.pad(flat, (0, rows * FLAT_W - flat.shape[0])).reshape(rows, FLAT_W)


def _unpack(flat, shapes):
    flat = flat.reshape(-1)
    out, off = [], 0
    for shp in shapes:
        n = int(np.prod(shp))
        out.append(flat[off:off + n].reshape(shp))
        off += n
    return out


def kernel(x, positions, mix_norm, w_in, sb_out_norm, ssm_conv_w, ssm_conv_b, ssm_dt_bias, ssm_a_log, ssm_d, ssm_out_norm, mla_q_norm, mla_w_uq, mla_kv_norm, mla_w_ukv, mla_out_norm, w_out, ffn_norm, ffn_w_up, ffn_conv_w, ffn_conv_b, ffn_w_down, final_norm, loss_target, m_mix_norm, m_w_in, m_sb_out_norm, m_ssm_conv_w, m_ssm_conv_b, m_ssm_dt_bias, m_ssm_a_log, m_ssm_d, m_ssm_out_norm, m_mla_q_norm, m_mla_w_uq, m_mla_kv_norm, m_mla_w_ukv, m_mla_out_norm, m_w_out, m_ffn_norm, m_ffn_w_up, m_ffn_conv_w, m_ffn_conv_b, m_ffn_w_down, m_final_norm, v_mix_norm, v_w_in, v_sb_out_norm, v_ssm_conv_w, v_ssm_conv_b, v_ssm_dt_bias, v_ssm_a_log, v_ssm_d, v_ssm_out_norm, v_mla_q_norm, v_mla_w_uq, v_mla_kv_norm, v_mla_w_ukv, v_mla_out_norm, v_w_out, v_ffn_norm, v_ffn_w_up, v_ffn_conv_w, v_ffn_conv_b, v_ffn_w_down, v_final_norm):
    args = locals()
    w = {n: args[n] for n in WEIGHTS}
    m = {n: args['m_' + n] for n in WEIGHTS}
    v = {n: args['v_' + n] for n in WEIGHTS}
    wire = lambda n: w[n] if n in VPU_WEIGHTS else w[n].astype(BF16)
    early = dict(zip(EARLY, all_gather([wire(n) for n in EARLY], name='gather_early')))

    loss, dx, parts, recv, rep = local_step(x[0], positions[0], loss_target[0], early, {n: wire(n) for n in LATE},
                                            {n: w[n] for n in REPLICATED})
    loss = lax.psum(loss, ("x", "y", "c"))

    recv.update(zip(EARLY, all_to_all([parts[n].astype(WIRE_DTYPE) for n in EARLY], name='scatter_early')))
    res = {kind: {} for kind in 'gdmv'}
    for n, rv in recv.items():
        shp = w[n].shape
        two_d = (int(np.prod(shp[:-1])), shp[-1])
        outs = adamw(rv.reshape((N_DEV,) + two_d), w[n].reshape(two_d), m[n].reshape(two_d), v[n].reshape(two_d),
                     name='adamw_' + n)
        for kind, o in zip('gdmv', outs):
            res[kind][n] = o.reshape(shp)

    rep_shapes = [w[n].shape for n in REPLICATED]
    (rparts,) = all_gather([_pack([rep[n] for n in REPLICATED])], name='gather_small_grads')
    rflat = lambda d: _pack([d[n] for n in REPLICATED])
    routs = adamw(rparts, rflat(w), rflat(m), rflat(v), name='adamw_replicated')
    for kind, o in zip('gdmv', routs):
        res[kind].update(zip(REPLICATED, _unpack(o, rep_shapes)))

    return (loss, dx[None], *[res['g'][n] for n in WEIGHTS], *[res['d'][n] for n in WEIGHTS],
            *[res['m'][n] for n in WEIGHTS], *[res['v'][n] for n in WEIGHTS])
```

```python
import numpy as np
import jax
import jax.numpy as jnp
from jax import lax
from jax.experimental import pallas as pl
from jax.experimental.pallas import tpu as pltpu

F32 = jnp.float32
BF16 = jnp.bfloat16
MXU_DTYPE = jnp.bfloat16
HIGHEST = lax.Precision.HIGHEST
WIRE_DTYPE = jnp.bfloat16

N_DEV = 8
D_MODEL = 1024
DEPTH = 2
EPS = 1e-6
SB_HEADS, SB_DIM = 4, 64
SB_WIDTH = SB_HEADS * SB_DIM
SSM_HEADS, SSM_P, SSM_GROUPS, SSM_N, SSM_CONV, SSM_CHUNK = 8, 64, 2, 64, 4, 128
SSM_INNER = SSM_HEADS * SSM_P
SSM_CONV_DIM = SSM_INNER + 2 * SSM_GROUPS * SSM_N
MLA_HEADS, MLA_NOPE, MLA_ROPE, MLA_V, MLA_Q_RANK, MLA_KV_RANK = 4, 64, 32, 64, 256, 128
MLA_QK = MLA_NOPE + MLA_ROPE
ROPE_THETA = 10000.0
D_IN = 2472
D_IN_PAD = 2560
TAIL = 2432
DT_LANE = 32
D_FF = 2816
FF_SHARD = 2 * D_FF // N_DEV
ADAM_LR, ADAM_B1, ADAM_B2, ADAM_EPS, ADAM_WD, ADAM_STEP = 0.001, 0.9, 0.999, 1e-08, 0.01, 10

LANES = 128
ATT_BLK = 256
SB_WIDE = 2
SB_SPENT = -110.0
MLA_WIDE = 4
ROW_BLK = 512
ROW_BLOCK_BYTES = 2 << 20
CONV_COLS = 256
FLAT_W = 1024
VMEM_LIMIT = 56 << 20
MM_TM = (1024, 512, 256, 128)
MM_TN = (1280, 1024, 768, 640, 512, 384, 256, 128)
MM_TK = (2048, 1280, 1024, 512, 256, 128)

WEIGHTS = ['mix_norm', 'w_in', 'sb_out_norm', 'ssm_conv_w', 'ssm_conv_b', 'ssm_dt_bias', 'ssm_a_log', 'ssm_d',
           'ssm_out_norm', 'mla_q_norm', 'mla_w_uq', 'mla_kv_norm', 'mla_w_ukv', 'mla_out_norm', 'w_out',
           'ffn_norm', 'ffn_w_up', 'ffn_conv_w', 'ffn_conv_b', 'ffn_w_down', 'final_norm']
SHARDED = {'w_in': 2, 'ssm_conv_w': 2, 'mla_w_uq': 2, 'mla_w_ukv': 2, 'w_out': 1, 'ffn_w_up': 2, 'ffn_conv_w': 2,
           'ffn_w_down': 1}
VPU_WEIGHTS = ('ssm_conv_w', 'ffn_conv_w')
EARLY = ('w_in', 'mla_w_uq', 'mla_w_ukv', 'ssm_conv_w')
LATE = ('w_out', 'ffn_w_up', 'ffn_conv_w', 'ffn_w_down')
REPLICATED = [n for n in WEIGHTS if n not in SHARDED]


def _call(body, *, name, out_shape, grid=(), in_specs=None, out_specs=None, scratch=(), sem=None, **kw):
    params = dict(vmem_limit_bytes=VMEM_LIMIT)
    if sem is not None:
        params['dimension_semantics'] = sem
    return pl.pallas_call(body, name=name, out_shape=out_shape, grid=grid, in_specs=in_specs, out_specs=out_specs,
                          scratch_shapes=list(scratch), compiler_params=pltpu.CompilerParams(**params), **kw)


def _tile(n, prefs):
    for t in prefs:
        if n % t == 0:
            return t
    return n


def _rows(s, w):
    rows = ROW_BLK
    while rows * 2 <= s and s % (rows * 2) == 0 and rows * 2 * w * 4 <= ROW_BLOCK_BYTES:
        rows *= 2
    return _tile(s, (rows,))


def _dot(a, b, dims, precision=None):
    return lax.dot_general(a, b, (dims, ((), ())), preferred_element_type=F32, precision=precision)


def _nn(a, b, precision=None):
    return _dot(a, b, ((1,), (0,)), precision)


def _nt(a, b, precision=None):
    return _dot(a, b, ((1,), (1,)), precision)


def _tn(a, b, precision=None):
    return _dot(a, b, ((0,), (0,)), precision)


def _mxu(f):
    return lambda a, b: f(a.astype(MXU_DTYPE), b.astype(MXU_DTYPE))


_bnn, _bnt, _btn = _mxu(_nn), _mxu(_nt), _mxu(_tn)


def _split2(x):
    hi = x.astype(MXU_DTYPE)
    lo = (x - hi.astype(F32)).astype(MXU_DTYPE)
    return hi, lo


def _sigmoid(x):
    return 0.5 * jnp.tanh(0.5 * x) + 0.5


def _softplus(x):
    return jnp.maximum(x, 0.0) + jnp.log1p(jnp.exp(-jnp.abs(x)))


def _softplus_att(x):
    return jnp.maximum(x, 0.0) + jnp.log(1.0 + jnp.exp(-jnp.abs(x)))


def _cum(x, u):
    rows, b = x.shape[0], u.shape[0]
    n = x.shape[1] // b
    hi, lo = _split2(x)
    stack = [part[:, t * b:(t + 1) * b] for part in (hi, lo) for t in range(n)]
    r = _nn(jnp.concatenate(stack, axis=0), u)
    return jnp.concatenate([r[t * rows:(t + 1) * rows] + r[(n + t) * rows:(n + t + 1) * rows] for t in range(n)], axis=1)


def _diagonal_group(qi, tile, carry, width):
    base = (qi // width) * width
    return lax.switch(qi - base, [lambda c, n=n: tile(base, c, True, n) for n in range(1, width + 1)], carry)


def _causal_loop(qi, tile, carry, width, first=0):
    carry = lax.fori_loop(first, qi // width, lambda i, c: tile(i * width, c, False, width), carry)
    return _diagonal_group(qi, tile, carry, width)


def _spent_loop(qi, tile, carry, width, live):
    carry = _diagonal_group(qi, tile, carry, width)
    step = lambda st: (st[0] - 1, tile((st[0] - 1) * width, st[1], False, width))
    return lax.while_loop(lambda st: (st[0] > 0) & live(st[1]), step, (qi // width, carry))


def _causal_mask(blk, width, qi, kb, heads, strict, keys_on_rows=False):
    shape = (width * blk, blk) if keys_on_rows else (heads * blk, width * blk)
    q_idx = lax.broadcasted_iota(jnp.int32, shape, 1 if keys_on_rows else 0)
    k_idx = lax.broadcasted_iota(jnp.int32, shape, 0 if keys_on_rows else 1)
    if heads > 1:
        q_idx = q_idx % blk
    gap = (qi - kb) * blk
    return k_idx < q_idx + gap if strict else k_idx <= q_idx + gap


def mm(a, b, *, name, ta=False, tb=False, res=None, out_dtype=F32, ab=None, bb=None, precision=None):
    a2, b2 = a.shape[-2:], b.shape[-2:]
    (kdim, m) = a2 if ta else a2[::-1]
    (n, k2) = b2 if tb else b2[::-1]
    assert kdim == k2, (a.shape, b.shape, ta, tb)
    assert (ab == 'k') == (bb == 'k')
    kb = ab == 'k'
    nb = a.shape[0] if ab == 'o' else (b.shape[0] if bb == 'o' else None)
    tm, tn = _tile(m, MM_TM if res is not None else (2 * MM_TM[0],) + MM_TM), _tile(n, MM_TN)
    tk = kdim if kb else _tile(kdim, MM_TK)
    nk = a.shape[0] if kb else kdim // tk
    dims = ((0 if ta else 1,), (1 if tb else 0,))
    op_dtype = F32 if precision is not None else MXU_DTYPE

    def body(*refs):
        a_ref, b_ref = refs[0], refs[1]
        r_ref = refs[2] if res is not None else None
        o_ref = refs[3] if res is not None else refs[2]
        part = _dot(a_ref[...].astype(op_dtype), b_ref[...].astype(op_dtype), dims, precision)

        def finish(out):
            if res is not None:
                out = out + r_ref[...]
            o_ref[...] = out.astype(out_dtype)

        if nk == 1:
            finish(part)
            return
        acc = refs[-1]
        k = pl.program_id(3)

        @pl.when(k == 0)
        def _():
            acc[...] = part

        @pl.when(k > 0)
        def _():
            acc[...] += part

        @pl.when(k == nk - 1)
        def _():
            finish(acc[...])

    def spec(blk, idx, how):
        if how is None:
            return pl.BlockSpec(blk, idx)
        if how == 'o':
            return pl.BlockSpec((None,) + blk, lambda p, i, j, k: (p,) + idx(p, i, j, k))
        return pl.BlockSpec((None,) + blk, lambda p, i, j, k: (k,) + idx(p, i, j, 0))

    a_spec = spec((tk, tm), lambda p, i, j, k: (k, i), ab) if ta else spec((tm, tk), lambda p, i, j, k: (i, k), ab)
    b_spec = spec((tn, tk), lambda p, i, j, k: (j, k), bb) if tb else spec((tk, tn), lambda p, i, j, k: (k, j), bb)
    o_spec = spec((tm, tn), lambda p, i, j, k: (i, j), None if nb is None else 'o')
    ins, specs = [a, b], [a_spec, b_spec]
    if res is not None:
        ins.append(res)
        specs.append(o_spec)
    out_shape = (m, n) if nb is None else (nb, m, n)
    return _call(body, name=name, out_shape=jax.ShapeDtypeStruct(out_shape, out_dtype),
                 grid=(1 if nb is None else nb, m // tm, n // tn, nk), in_specs=specs, out_specs=o_spec,
                 scratch=[] if nk == 1 else [pltpu.VMEM((tm, tn), F32)],
                 sem=("parallel", "parallel", "parallel", "arbitrary"))(*ins)


def rms_fwd(x, g, *, name, gate=None, out_dtype=F32):
    s, w = x.shape
    bs = _rows(s, w)

    def body(*refs):
        if gate is None:
            x_ref, g_ref, o_ref = refs
            u = x_ref[...]
        else:
            x_ref, z_ref, g_ref, o_ref = refs
            z = z_ref[...]
            u = x_ref[...] * (z * _sigmoid(z))
        r = lax.rsqrt(jnp.mean(u * u, axis=1, keepdims=True) + EPS)
        o_ref[...] = (u * r * g_ref[...]).astype(out_dtype)

    row = pl.BlockSpec((bs, w), lambda i: (i, 0))
    vec = pl.BlockSpec((1, w), lambda i: (0, 0))
    ins = [x] + ([] if gate is None else [gate]) + [g.reshape(1, w)]
    specs = [row] + ([] if gate is None else [row]) + [vec]
    return _call(body, name=name, out_shape=jax.ShapeDtypeStruct((s, w), out_dtype), grid=(s // bs,),
                 in_specs=specs, out_specs=row, sem=("parallel",))(*ins)


def rms_bwd(x, g, dy, *, name, gate=None, add=None):
    s, w = x.shape
    bs = _rows(s, w)

    def body(*refs):
        refs = list(refs)
        x_ref = refs.pop(0)
        z_ref = refs.pop(0) if gate is not None else None
        g_ref = refs.pop(0)
        dy_ref = refs.pop(0)
        add_ref = refs.pop(0) if add is not None else None
        dx_ref = refs.pop(0)
        dz_ref = refs.pop(0) if gate is not None else None
        dg_ref = refs.pop(0)
        i = pl.program_id(0)

        @pl.when(i == 0)
        def _():
            dg_ref[...] = jnp.zeros_like(dg_ref)

        xv = x_ref[...]
        if gate is not None:
            z = z_ref[...]
            sg = _sigmoid(z)
            act = z * sg
            u = xv * act
        else:
            u = xv
        r = lax.rsqrt(jnp.mean(u * u, axis=1, keepdims=True) + EPS)
        dy_v = dy_ref[...]
        dyg = dy_v * g_ref[...]
        du = r * dyg - u * (r * r * r * jnp.mean(dyg * u, axis=1, keepdims=True))
        dg_ref[...] += jnp.sum(dy_v * u * r, axis=0, keepdims=True)
        if gate is not None:
            dx = du * act
            dz_ref[...] = du * xv * (sg * (1.0 + z * (1.0 - sg)))
        else:
            dx = du
        if add is not None:
            dx = dx + add_ref[...]
        dx_ref[...] = dx

    row = pl.BlockSpec((bs, w), lambda i: (i, 0))
    vec = pl.BlockSpec((1, w), lambda i: (0, 0))
    ins = [x] + ([] if gate is None else [gate]) + [g.reshape(1, w), dy] + ([] if add is None else [add])
    specs = [row] + ([] if gate is None else [row]) + [vec, row] + ([] if add is None else [row])
    outs = [jax.ShapeDtypeStruct((s, w), F32)] + ([] if gate is None else [jax.ShapeDtypeStruct((s, w), F32)])
    outs.append(jax.ShapeDtypeStruct((1, w), F32))
    ospecs = [row] + ([] if gate is None else [row]) + [vec]
    return _call(body, name=name, out_shape=outs, grid=(s // bs,), in_specs=specs, out_specs=ospecs,
                 sem=("arbitrary",))(*ins)


def loss_head(y, target, *, name):
    s, w = y.shape
    bs = _rows(s, w)
    nb = s // bs

    def body(y_ref, t_ref, dy_ref, loss_ref, acc):
        i = pl.program_id(0)

        @pl.when(i == 0)
        def _():
            acc[...] = jnp.zeros_like(acc)

        e = y_ref[...] - t_ref[...]
        dy_ref[...] = e * (1.0 / w)
        acc[...] += jnp.sum(e * e, axis=0, keepdims=True)

        @pl.when(i == nb - 1)
        def _():
            loss_ref[...] = jnp.sum(acc[...], axis=1, keepdims=True) * (0.5 / w)

    row = pl.BlockSpec((bs, w), lambda i: (i, 0))
    return _call(body, name=name, out_shape=[jax.ShapeDtypeStruct((s, w), F32), jax.ShapeDtypeStruct((1, 1), F32)],
                 grid=(nb,), in_specs=[row, row], out_specs=[row, pl.BlockSpec((1, 1), lambda i: (0, 0))],
                 scratch=[pltpu.VMEM((1, w), F32)], sem=("arbitrary",))(y, target)


def _rope_tables(positions, s):
    inv_freq = 1.0 / (ROPE_THETA ** (jnp.arange(0, MLA_ROPE, 2, dtype=F32) / MLA_ROPE))
    ang = positions.reshape(s, 1).astype(F32) * inv_freq
    cos, sin = jnp.cos(ang), jnp.sin(ang)
    one, zero = jnp.ones((s, MLA_NOPE), F32), jnp.zeros((s, MLA_NOPE), F32)
    cq = jnp.tile(jnp.concatenate([one, cos, cos], axis=1), (1, MLA_HEADS))
    sq = jnp.tile(jnp.concatenate([zero, sin, sin], axis=1), (1, MLA_HEADS))
    pad1, pad0 = jnp.ones((s, LANES - MLA_ROPE), F32), jnp.zeros((s, LANES - MLA_ROPE), F32)
    ct = jnp.concatenate([cos, cos, pad1], axis=1)
    st = jnp.concatenate([sin, sin, pad0], axis=1)
    half = MLA_ROPE // 2

    def swap(width, starts):
        r = np.zeros((width, width), np.float32)
        for o in starts:
            for i in range(half):
                r[o + half + i, o + i] = -1.0
                r[o + i, o + half + i] = 1.0
        return jnp.asarray(r)

    rq = swap(MLA_HEADS * MLA_QK, [h * MLA_QK + MLA_NOPE for h in range(MLA_HEADS)])
    rt = swap(LANES, [0])
    return (cq, sq, rq), (ct, st, rt)


def rope(x, tabs, *, name, backward=False, add=None):
    cos, sin, rot = tabs
    n, s, w = x.shape
    bs = _rows(s, w)

    def body(*refs):
        if add is None:
            x_ref, c_ref, s_ref, r_ref, o_ref = refs
        else:
            x_ref, c_ref, s_ref, r_ref, a_ref, o_ref = refs
        xv = x_ref[0]
        for j in range(1, n):
            xv = xv + x_ref[j]
        if backward:
            out = xv * c_ref[...] + _nt(xv * s_ref[...], r_ref[...], HIGHEST)
        else:
            out = xv * c_ref[...] + _nn(xv, r_ref[...], HIGHEST) * s_ref[...]
        if add is not None:
            out = out + a_ref[...]
        o_ref[...] = out

    row = pl.BlockSpec((bs, w), lambda i: (i, 0))
    ins = [x, cos, sin, rot] + ([] if add is None else [add])
    specs = [pl.BlockSpec((n, bs, w), lambda i: (0, i, 0)), row, row, pl.BlockSpec((w, w), lambda i: (0, 0))]
    specs += [] if add is None else [row]
    return _call(body, name=name, out_shape=jax.ShapeDtypeStruct((s, w), F32), grid=(s // bs,), in_specs=specs,
                 out_specs=row, sem=("parallel",))(*ins)


MESH = pl.DeviceIdType.MESH
HBM = pl.BlockSpec(memory_space=pltpu.HBM)


def _flip(v, bit):
    return 1 - v if bit else v


class Carried:
    def __init__(self, kind, arrays):
        assert kind in ('gather', 'scatter')
        self.kind, self.arrays, self.n = kind, list(arrays), len(arrays)

    @property
    def out_shape(self):
        lead = (N_DEV,) if self.kind == 'gather' else ()
        return [jax.ShapeDtypeStruct(lead + a.shape, a.dtype) for a in self.arrays]

    @property
    def scratch(self):
        return [pltpu.SemaphoreType.DMA((self.n, N_DEV - 1)), pltpu.SemaphoreType.DMA((self.n, N_DEV - 1)),
                pltpu.SemaphoreType.DMA((self.n,))]

    def _copies(self, in_refs, out_refs, sems):
        send_sems, recv_sems, local_sems = sems
        x, y, c = lax.axis_index("x"), lax.axis_index("y"), lax.axis_index("c")
        me = 4 * x + 2 * y + c
        part = (lambda b, p: in_refs[b]) if self.kind == 'gather' else (lambda b, p: in_refs[b].at[p])
        local = [pltpu.make_async_copy(part(b, me), out_refs[b].at[me], local_sems.at[b]) for b in range(self.n)]
        remote = []
        for k in range(1, N_DEV):
            px, py, pc = _flip(x, k & 4), _flip(y, k & 2), _flip(c, k & 1)
            for b in range(self.n):
                remote.append(pltpu.make_async_remote_copy(
                    src_ref=part(b, 4 * px + 2 * py + pc), dst_ref=out_refs[b].at[me], send_sem=send_sems.at[b, k - 1],
                    recv_sem=recv_sems.at[b, k - 1], device_id=(px, py, pc), device_id_type=MESH))
        return local, remote

    def start(self, in_refs, out_refs, sems):
        local, remote = self._copies(in_refs, out_refs, sems)
        for cp in local + remote:
            cp.start()

    def wait(self, in_refs, out_refs, sems):
        local, remote = self._copies(in_refs, out_refs, sems)
        for cp in remote:
            cp.wait_send()
            cp.wait_recv()
        for cp in local:
            cp.wait()


def _first_last_step(n0, n1):
    at = lambda a, b: (pl.program_id(0) == a) & (pl.program_id(1) == b)
    return (lambda: at(0, 0)), (lambda: at(n0 - 1, n1 - 1))


def _carry_call(body, carried, first, last, ins, *, out_shape, in_specs, out_specs, scratch=(), **kw):
    if carried is None:
        return _call(body, out_shape=out_shape, in_specs=in_specs, out_specs=out_specs, scratch=scratch, **kw)(*ins)
    n, n_in, n_out, n_scr = carried.n, len(in_specs), len(out_specs), len(scratch)

    def riding(*refs):
        own_in, ride_in = refs[:n_in], refs[n_in:n_in + n]
        rest = refs[n_in + n:]
        own_out, ride_out = rest[:n_out], rest[n_out:n_out + n]
        own_scr, sems = rest[n_out + n:n_out + n + n_scr], rest[n_out + n + n_scr:]
        pl.when(first())(lambda: carried.start(ride_in, ride_out, sems))
        body(*own_in, *own_out, *own_scr)
        pl.when(last())(lambda: carried.wait(ride_in, ride_out, sems))

    return _call(riding, out_shape=list(out_shape) + carried.out_shape, in_specs=list(in_specs) + [HBM] * n,
                 out_specs=list(out_specs) + [HBM] * n, scratch=list(scratch) + carried.scratch, **kw)(*ins, *carried.arrays)


def _tri(n, op):
    r = lax.broadcasted_iota(jnp.int32, (n, n), 0)
    c = lax.broadcasted_iota(jnp.int32, (n, n), 1)
    return r, c, op(r, c)


def _pair_split(x, first):
    zero = jnp.zeros_like(x)
    return jnp.where(first, x, zero), jnp.where(first, zero, x)


def _sb_specs(s, blk):
    npair = SB_WIDTH // LANES
    q = pl.BlockSpec((blk, LANES), lambda j, i: (i, j))
    k = pl.BlockSpec((s, LANES), lambda j, i: (0, npair + j))
    v = pl.BlockSpec((s, LANES), lambda j, i: (0, 2 * npair + j))
    full = pl.BlockSpec((s, LANES), lambda j, i: (0, j))
    return q, k, v, full


def _stack_heads(x, first):
    return jnp.concatenate(_pair_split(x, first), axis=0)


def _unstack_heads(x, first, blk):
    return jnp.where(first, x[:blk], x[blk:])


def sb_fwd(qkv, *, name, carried=None):
    s = qkv.shape[0]
    blk = _tile(s, (ATT_BLK,))
    scale = SB_DIM ** -0.5
    npair, nq = SB_WIDTH // LANES, s // blk
    assert nq % SB_WIDE == 0

    def body(q_ref, k_ref, v_ref, y_ref):
        qi = pl.program_id(1)
        first = lax.broadcasted_iota(jnp.int32, (blk, LANES), 1) < SB_DIM
        q2 = _stack_heads((q_ref[...].astype(F32) * scale).astype(MXU_DTYPE), first)
        row, col, later_mask = _tri(blk, lambda r, c: r > c)
        u_later = later_mask.astype(MXU_DTYPE)

        def tile(kb, carry, masked, n):
            c, acc = carry
            keys = pl.ds(pl.multiple_of(kb * blk, blk), n * blk)
            z = _nt(q2, k_ref[keys, :])
            sp = _softplus_att(z)
            if masked:
                valid = _causal_mask(blk, n, qi, kb, 2, True)
            spm = jnp.where(valid, sp, 0.0) if masked else sp
            later = _cum(spm, u_later)
            sums = [jnp.sum(spm[:, t * blk:(t + 1) * blk], axis=1, keepdims=True) for t in range(n)]
            after, cols = c, [None] * n
            for t in reversed(range(n)):
                cols[t] = jnp.broadcast_to(after, (2 * blk, blk))
                after = after - sums[t]
            w = jnp.exp((z - sp) - later + (cols[0] if n == 1 else jnp.concatenate(cols, axis=1)))
            if masked:
                w = jnp.where(valid, w, 0.0)
            return after, acc + _nn(w.astype(MXU_DTYPE), v_ref[keys, :])

        zero = (jnp.zeros((2 * blk, 1), F32), jnp.zeros((2 * blk, LANES), F32))
        _, (c, acc) = _spent_loop(qi, tile, zero, SB_WIDE, lambda cr: jnp.max(cr[0]) >= SB_SPENT)
        y_ref[...] = _unstack_heads(acc, first, blk)

    qspec, kspec, vspec, _ = _sb_specs(s, blk)
    return _carry_call(body, carried, *_first_last_step(npair, nq), (qkv, qkv, qkv), name=name,
                       out_shape=[jax.ShapeDtypeStruct((s, SB_WIDTH), F32)], grid=(npair, nq),
                       in_specs=[qspec, kspec, vspec], out_specs=[qspec], sem=("arbitrary", "arbitrary"))


def sb_bwd(qkv, dy, *, name, carried=None):
    s = qkv.shape[0]
    blk = _tile(s, (ATT_BLK,))
    scale = SB_DIM ** -0.5
    npair, nq = SB_WIDTH // LANES, s // blk
    assert nq % SB_WIDE == 0

    def body(q_ref, k_ref, v_ref, dy_ref, dq_ref, dk_ref, dv_ref):
        qi = pl.program_id(1)

        @pl.when(qi == 0)
        def _():
            dk_ref[...] = jnp.zeros_like(dk_ref)
            dv_ref[...] = jnp.zeros_like(dv_ref)

        first = lax.broadcasted_iota(jnp.int32, (blk, LANES), 1) < SB_DIM
        q2 = _stack_heads((q_ref[...].astype(F32) * scale).astype(MXU_DTYPE), first)
        dy2 = _stack_heads(dy_ref[...].astype(MXU_DTYPE), first)
        row, col, incl_mask = _tri(blk, lambda r, c: r <= c)
        u_incl = incl_mask.astype(MXU_DTYPE)
        u_excl = (row < col).astype(MXU_DTYPE)

        def walk(kb, c, masked, n):
            sp = _softplus_att(_nt(q2, k_ref[pl.ds(pl.multiple_of(kb * blk, blk), n * blk), :]))
            if masked:
                sp = jnp.where(_causal_mask(blk, n, qi, kb, 2, True), sp, 0.0)
            for t in reversed(range(n)):
                c = c - jnp.sum(sp[:, t * blk:(t + 1) * blk], axis=1, keepdims=True)
            return c

        start, tv = _spent_loop(qi, walk, jnp.zeros((2 * blk, 1), F32), SB_WIDE, lambda c: jnp.max(c) >= SB_SPENT)

        def prefixed(x, carry):
            n = x.shape[1] // blk
            cols = []
            for t in range(n):
                cols.append(jnp.broadcast_to(carry, (2 * blk, blk)))
                carry = carry + jnp.sum(x[:, t * blk:(t + 1) * blk], axis=1, keepdims=True)
            return (cols[0] if n == 1 else jnp.concatenate(cols, axis=1)), carry

        def tile(kb, carry, masked, n):
            p, gc, dq = carry
            keys = pl.ds(pl.multiple_of(kb * blk, blk), n * blk)
            kv = k_ref[keys, :]
            z = _nt(q2, kv)
            dw = _nt(dy2, v_ref[keys, :])
            sp = _softplus_att(z)
            if masked:
                valid = _causal_mask(blk, n, qi, kb, 2, True)
            spm = jnp.where(valid, sp, 0.0) if masked else sp
            before, p = prefixed(spm, p)
            w = jnp.exp((z - sp) + (_cum(spm, u_incl) + before))
            if masked:
                w = jnp.where(valid, w, 0.0)
            g = w * dw
            gbefore, gc = prefixed(g, gc)
            gb = g.astype(MXU_DTYPE)
            gin = _nn(jnp.concatenate([gb[:, t * blk:(t + 1) * blk] for t in range(n)], axis=0), u_excl)
            gex = gbefore + jnp.concatenate([gin[t * 2 * blk:(t + 1) * 2 * blk] for t in range(n)], axis=1)
            keep = jnp.exp(-spm)
            dz = keep * (g + gex) - gex
            if masked:
                dz = jnp.where(valid, dz, 0.0)
            dzb = dz.astype(MXU_DTYPE)
            dk_ref[keys, :] += _tn(dzb, q2)
            dv_ref[keys, :] += _tn(w.astype(MXU_DTYPE), dy2)
            return p, gc, dq + _nn(dzb, kv)

        zero = jnp.zeros((2 * blk, 1), F32)
        _, _, dq = _causal_loop(qi, tile, (tv, zero, jnp.zeros((2 * blk, LANES), F32)), SB_WIDE, first=start)
        dq_ref[...] = _unstack_heads(dq, first, blk) * scale

    qspec, kspec, vspec, full = _sb_specs(s, blk)
    out = jax.ShapeDtypeStruct((s, SB_WIDTH), F32)
    return _carry_call(body, carried, *_first_last_step(npair, nq), (qkv, qkv, qkv, dy), name=name,
                       out_shape=[out, out, out], grid=(npair, nq), in_specs=[qspec, kspec, vspec, qspec],
                       out_specs=[qspec, full, full], sem=("arbitrary", "arbitrary"))


ATT_PAIR = 2


def _mla_specs(s, blk, dk, dv):
    q = pl.BlockSpec((ATT_PAIR, blk, dk), lambda hp, i: (hp, i, 0))
    k = pl.BlockSpec((ATT_PAIR, s, dk), lambda hp, i: (hp, 0, 0))
    v = pl.BlockSpec((ATT_PAIR, s, dv), lambda hp, i: (hp, 0, 0))
    y = pl.BlockSpec((ATT_PAIR, blk, dv), lambda hp, i: (hp, i, 0))
    lse = pl.BlockSpec((ATT_PAIR, blk, LANES), lambda hp, i: (hp, i, 0))
    return q, k, v, y, lse


def mla_fwd(q, k, v, *, name, carried=None):
    h, s, dk = q.shape
    dv = v.shape[-1]
    blk = _tile(s, (ATT_BLK,))
    scale = dk ** -0.5
    assert (s // blk) % MLA_WIDE == 0

    def body(q_ref, k_ref, v_ref, y_ref, l_ref):
        qi = pl.program_id(1)

        def tile(kb, carry, masked, n):
            keys = pl.ds(pl.multiple_of(kb * blk, blk), n * blk)
            out = []
            for hh in range(ATT_PAIR):
                m, l, acc = carry[hh]
                sc = _nt(q_ref[hh], k_ref[hh, keys, :]) * scale
                if masked:
                    sc = jnp.where(_causal_mask(blk, n, qi, kb, 1, False), sc, -1e30)
                m2 = jnp.maximum(m, jnp.max(sc, axis=1, keepdims=True))
                p = jnp.exp(sc - m2)
                a = jnp.exp(m - m2)
                out.append((m2, a * l + jnp.sum(p, axis=1, keepdims=True),
                            a * acc + _nn(p.astype(MXU_DTYPE), v_ref[hh, keys, :])))
            return tuple(out)

        init = (jnp.full((blk, 1), -1e30, F32), jnp.zeros((blk, 1), F32), jnp.zeros((blk, dv), F32))
        for hh, (m, l, acc) in enumerate(_causal_loop(qi, tile, (init,) * ATT_PAIR, MLA_WIDE)):
            y_ref[hh] = acc / l
            l_ref[hh] = jnp.broadcast_to(m + jnp.log(l), (blk, LANES))

    qspec, kspec, vspec, yspec, lspec = _mla_specs(s, blk, dk, dv)
    grid = (h // ATT_PAIR, s // blk)
    return _carry_call(body, carried, *_first_last_step(*grid), (q, k, v), name=name,
                       out_shape=[jax.ShapeDtypeStruct((h, s, dv), F32), jax.ShapeDtypeStruct((h, s, LANES), F32)],
                       grid=grid, in_specs=[qspec, kspec, vspec], out_specs=[yspec, lspec],
                       sem=("arbitrary", "arbitrary"))


def mla_bwd(q, k, v, y, dy, lse, *, name):
    h, s, dk = q.shape
    dv = v.shape[-1]
    blk = _tile(s, (ATT_BLK,))
    scale = dk ** -0.5
    assert (s // blk) % MLA_WIDE == 0

    def body(q_ref, k_ref, v_ref, y_ref, dy_ref, l_ref, dq_ref, dk_ref, dv_ref):
        qi = pl.program_id(1)

        @pl.when(qi == 0)
        def _():
            dk_ref[...] = jnp.zeros_like(dk_ref)
            dv_ref[...] = jnp.zeros_like(dv_ref)

        as_row = lambda col: jnp.transpose(jnp.broadcast_to(col, (blk, LANES)))[0:1, :]
        dyv = [dy_ref[hh].astype(MXU_DTYPE) for hh in range(ATT_PAIR)]
        delta = [as_row(jnp.sum(dy_ref[hh] * y_ref[hh], axis=1, keepdims=True)) for hh in range(ATT_PAIR)]
        lv = [as_row(l_ref[hh, :, 0:1]) for hh in range(ATT_PAIR)]
        def tile(kb, dqs, masked, n):
            keys = pl.ds(pl.multiple_of(kb * blk, blk), n * blk)
            out = []
            for hh in range(ATT_PAIR):
                qv = q_ref[hh]
                kv = k_ref[hh, keys, :]
                p = jnp.exp(_nt(kv, qv) * scale - lv[hh])
                if masked:
                    p = jnp.where(_causal_mask(blk, n, qi, kb, 1, False, keys_on_rows=True), p, 0.0)
                ds = (p * (_nt(v_ref[hh, keys, :], dyv[hh]) - delta[hh])).astype(MXU_DTYPE)
                dk_ref[hh, keys, :] += _nn(ds, qv) * scale
                dv_ref[hh, keys, :] += _nn(p.astype(MXU_DTYPE), dyv[hh])
                out.append(dqs[hh] + _tn(ds, kv))
            return tuple(out)

        for hh, dq in enumerate(_causal_loop(qi, tile, (jnp.zeros((blk, dk), F32),) * ATT_PAIR, MLA_WIDE)):
            dq_ref[hh] = dq * scale

    qspec, kspec, vspec, yspec, lspec = _mla_specs(s, blk, dk, dv)
    return _call(body, name=name,
                 out_shape=[jax.ShapeDtypeStruct((h, s, dk), F32), jax.ShapeDtypeStruct((h, s, dk), F32),
                            jax.ShapeDtypeStruct((h, s, dv), F32)],
                 grid=(h // ATT_PAIR, s // blk), in_specs=[qspec, kspec, vspec, yspec, yspec, lspec],
                 out_specs=[qspec, kspec, vspec], sem=("parallel", "arbitrary"))(q, k, v, y, dy, lse)


HALO = 8
CONV_CHUNK = 16


def _conv_tiles(x):
    s, c = x.shape[-2:]
    return s, c, _tile(s, (ROW_BLK,)), _tile(c, (CONV_COLS,))


def _halo_rows(dtype):
    return HALO * 4 // jnp.dtype(dtype).itemsize


def _conv_specs(bs, cw, lead=(), dtype=F32):
    zero = (0,) * len(lead)
    hr = _halo_rows(dtype)
    blk = pl.BlockSpec(lead + (None, bs, cw), lambda p, j, i: zero + (p, i, j))
    halo = pl.BlockSpec(lead + (None, hr, cw), lambda p, j, i: zero + (p, jnp.maximum(i * (bs // hr) - 1, 0), j))
    w = lambda kk: pl.BlockSpec(lead + (None, kk, cw), lambda p, j, i: zero + (p, 0, j))
    return blk, halo, w


def _stage(scr, x_ref, halo_ref, first):
    hr = halo_ref.shape[0]
    scr[0:HALO, :] = jnp.where(first, 0.0, halo_ref[hr - HALO:hr, :].astype(F32))
    scr[HALO:, :] = x_ref[...].astype(F32)


def _shifted(ext, shift):
    return ext[HALO:] if shift == 0 else pltpu.roll(ext, shift, 0)[HALO:]


def _conv_taps(scr, kk, r0):
    ext = scr[pl.ds(r0, CONV_CHUNK + HALO), :]
    return [_shifted(ext, kk - 1 - k) for k in range(kk)]


def _conv_sum(taps, w_ref, b_ref):
    u = b_ref[...] + taps[0] * w_ref[0:1, :]
    for k in range(1, len(taps)):
        u = u + taps[k] * w_ref[k:k + 1, :]
    return u


def _fold(x):
    out = x[0:8]
    for r in range(8, CONV_CHUNK, 8):
        out = out + x[r:r + 8]
    return out


class _TapSums:
    def __init__(self, kk, cw):
        self.w = [jnp.zeros((8, cw), F32) for _ in range(kk)]
        self.b = jnp.zeros((8, cw), F32)

    def add(self, du, taps):
        self.w = [a + _fold(du * t) for a, t in zip(self.w, taps)]
        self.b = self.b + _fold(du)

    def flush(self, dw_ref, db_ref):
        for k, a in enumerate(self.w):
            dw_ref[k:k + 1, :] += jnp.sum(a, axis=0, keepdims=True)
        db_ref[...] += jnp.sum(self.b, axis=0, keepdims=True)


def _silu_grad(u):
    sg = _sigmoid(u)
    return sg * (1.0 + u * (1.0 - sg))


def conv_silu_fwd(x, w, b, *, name):
    s, c, bs, cw = _conv_tiles(x)
    kk = w.shape[1]

    def body(x_ref, h_ref, w_ref, b_ref, o_ref, scr):
        _stage(scr, x_ref, h_ref, pl.program_id(2) == 0)
        for r0 in range(0, bs, CONV_CHUNK):
            u = _conv_sum(_conv_taps(scr, kk, r0), w_ref, b_ref)
            o_ref[pl.ds(r0, CONV_CHUNK), :] = u * _sigmoid(u)

    blk, halo, wspec = _conv_specs(bs, cw, dtype=x.dtype)
    return _call(body, name=name, out_shape=jax.ShapeDtypeStruct(x.shape, F32), grid=(x.shape[0], c // cw, s // bs),
                 in_specs=[blk, halo, wspec(kk), wspec(1)], out_specs=blk, scratch=[pltpu.VMEM((bs + HALO, cw), F32)],
                 sem=("parallel", "parallel", "arbitrary"))(x, x, w, b)


def conv_silu_bwd(x, dy, w, b, *, name):
    s, c, bs, cw = _conv_tiles(x)
    kk = w.shape[1]

    def body(x_ref, h_ref, w_ref, b_ref, dy_ref, du_ref, dw_ref, db_ref, scr):
        i = pl.program_id(2)

        @pl.when(i == 0)
        def _():
            dw_ref[...] = jnp.zeros_like(dw_ref)
            db_ref[...] = jnp.zeros_like(db_ref)

        _stage(scr, x_ref, h_ref, i == 0)
        sums = _TapSums(kk, cw)
        for r0 in range(0, bs, CONV_CHUNK):
            taps = _conv_taps(scr, kk, r0)
            du = dy_ref[pl.ds(r0, CONV_CHUNK), :] * _silu_grad(_conv_sum(taps, w_ref, b_ref))
            du_ref[pl.ds(r0, CONV_CHUNK), :] = du
            sums.add(du, taps)
        sums.flush(dw_ref, db_ref)

    blk, halo, wspec = _conv_specs(bs, cw, dtype=x.dtype)
    return _call(body, name=name,
                 out_shape=[jax.ShapeDtypeStruct(x.shape, F32), jax.ShapeDtypeStruct(w.shape, F32),
                            jax.ShapeDtypeStruct(b.shape, F32)],
                 grid=(x.shape[0], c // cw, s // bs), in_specs=[blk, halo, wspec(kk), wspec(1), blk],
                 out_specs=[blk, wspec(kk), wspec(1)], scratch=[pltpu.VMEM((bs + HALO, cw), F32)],
                 sem=("parallel", "parallel", "arbitrary"))(x, x, w, b, dy)


def _glu_view(a):
    return a.reshape((2, a.shape[0] // 2) + a.shape[1:])


def conv_glu_fwd(x, w, b, *, name):
    s, c, bs, cw = _conv_tiles(x)
    kk = w.shape[1]
    half = x.shape[0] // 2

    def body(x_ref, h_ref, w_ref, b_ref, o_ref, gscr, vscr):
        first = pl.program_id(2) == 0
        _stage(gscr, x_ref.at[0], h_ref.at[0], first)
        _stage(vscr, x_ref.at[1], h_ref.at[1], first)
        for r0 in range(0, bs, CONV_CHUNK):
            gate = _conv_sum(_conv_taps(gscr, kk, r0), w_ref.at[0], b_ref.at[0])
            val = _conv_sum(_conv_taps(vscr, kk, r0), w_ref.at[1], b_ref.at[1])
            o_ref[pl.ds(r0, CONV_CHUNK), :] = (gate * _sigmoid(gate) * val).astype(o_ref.dtype)

    blk, halo, wspec = _conv_specs(bs, cw, lead=(2,), dtype=x.dtype)
    out, _, _ = _conv_specs(bs, cw)
    xv = _glu_view(x)
    return _call(body, name=name, out_shape=jax.ShapeDtypeStruct((half, s, c), MXU_DTYPE), grid=(half, c // cw, s // bs),
                 in_specs=[blk, halo, wspec(kk), wspec(1)], out_specs=out, scratch=[pltpu.VMEM((bs + HALO, cw), F32)] * 2,
                 sem=("parallel", "parallel", "arbitrary"))(xv, xv, _glu_view(w), _glu_view(b))


def conv_glu_bwd(x, da, w, b, *, name):
    s, c, bs, cw = _conv_tiles(x)
    kk = w.shape[1]
    half = x.shape[0] // 2

    def body(x_ref, h_ref, w_ref, b_ref, da_ref, du_ref, dw_ref, db_ref, gscr, vscr):
        i = pl.program_id(2)

        @pl.when(i == 0)
        def _():
            dw_ref[...] = jnp.zeros_like(dw_ref)
            db_ref[...] = jnp.zeros_like(db_ref)

        _stage(gscr, x_ref.at[0], h_ref.at[0], i == 0)
        _stage(vscr, x_ref.at[1], h_ref.at[1], i == 0)
        gsums, vsums = _TapSums(kk, cw), _TapSums(kk, cw)
        for r0 in range(0, bs, CONV_CHUNK):
            gtaps, vtaps = _conv_taps(gscr, kk, r0), _conv_taps(vscr, kk, r0)
            gate = _conv_sum(gtaps, w_ref.at[0], b_ref.at[0])
            val = _conv_sum(vtaps, w_ref.at[1], b_ref.at[1])
            dav = da_ref[pl.ds(r0, CONV_CHUNK), :]
            dgate = dav * val * _silu_grad(gate)
            dval = dav * gate * _sigmoid(gate)
            du_ref[0, pl.ds(r0, CONV_CHUNK), :] = dgate.astype(du_ref.dtype)
            du_ref[1, pl.ds(r0, CONV_CHUNK), :] = dval.astype(du_ref.dtype)
            gsums.add(dgate, gtaps)
            vsums.add(dval, vtaps)
        gsums.flush(dw_ref.at[0], db_ref.at[0])
        vsums.flush(dw_ref.at[1], db_ref.at[1])

    blk, halo, wspec = _conv_specs(bs, cw, lead=(2,), dtype=x.dtype)
    daspec, _, _ = _conv_specs(bs, cw)
    xv, wv, bv = _glu_view(x), _glu_view(w), _glu_view(b)
    du, dw, db = _call(body, name=name,
                       out_shape=[jax.ShapeDtypeStruct(xv.shape, x.dtype), jax.ShapeDtypeStruct(wv.shape, F32),
                                  jax.ShapeDtypeStruct(bv.shape, F32)],
                       grid=(half, c // cw, s // bs), in_specs=[blk, halo, wspec(kk), wspec(1), daspec],
                       out_specs=[blk, wspec(kk), wspec(1)], scratch=[pltpu.VMEM((bs + HALO, cw), F32)] * 2,
                       sem=("parallel", "parallel", "arbitrary"))(xv, xv, wv, bv, da)
    return du.reshape(x.shape), dw.reshape(w.shape), db.reshape(b.shape)


def conv_t(du, w, *, name, out_dtype=F32):
    s, c, bs, cw = _conv_tiles(du)
    kk = w.shape[1]
    nb = s // bs

    def body(d_ref, h_ref, w_ref, o_ref, scr):
        last = pl.program_id(2) == nb - 1
        scr[0:bs, :] = d_ref[...].astype(F32)
        scr[bs:, :] = jnp.where(last, 0.0, h_ref[0:HALO, :].astype(F32))
        for r0 in range(0, bs, CONV_CHUNK):
            ext = scr[pl.ds(r0, CONV_CHUNK + HALO), :]
            ahead = lambda j: ext[:CONV_CHUNK] if j == 0 else pltpu.roll(ext, CONV_CHUNK + HALO - j, 0)[:CONV_CHUNK]
            acc = ahead(kk - 1) * w_ref[0:1, :]
            for k in range(1, kk):
                acc = acc + ahead(kk - 1 - k) * w_ref[k:k + 1, :]
            o_ref[pl.ds(r0, CONV_CHUNK), :] = acc.astype(out_dtype)

    blk, _, wspec = _conv_specs(bs, cw)
    hr = _halo_rows(du.dtype)
    halo = pl.BlockSpec((None, hr, cw), lambda q, j, i: (q, jnp.minimum((i + 1) * (bs // hr), s // hr - 1), j))
    return _call(body, name=name, out_shape=jax.ShapeDtypeStruct(du.shape, out_dtype), grid=(du.shape[0], c // cw, nb),
                 in_specs=[blk, halo, wspec(kk)], out_specs=blk, scratch=[pltpu.VMEM((bs + HALO, cw), F32)],
                 sem=("parallel", "parallel", "arbitrary"))(du, du, w)


def _ssd_common(xbc_ref, tail_ref, dtrt_ref, bias_ref, biast_ref, alog_ref, alogt_ref):
    L = SSM_CHUNK
    raw = tail_ref[...] + bias_ref[...]
    dt = _softplus(raw)
    dtt = _softplus(dtrt_ref[...] + biast_ref[...])
    a = -jnp.exp(alog_ref[...])
    at = -jnp.exp(alogt_ref[...])
    row, col, lower = _tri(L, lambda r, c: r >= c)
    tril = lower.astype(F32)
    cs = _nn(tril, dt * a, HIGHEST)
    cst = _nt(dtt * at, tril, HIGHEST)
    bm = [xbc_ref[:, SSM_INNER + g * SSM_N: SSM_INNER + (g + 1) * SSM_N] for g in range(SSM_GROUPS)]
    off = SSM_INNER + SSM_GROUPS * SSM_N
    cm = [xbc_ref[:, off + g * SSM_N: off + (g + 1) * SSM_N] for g in range(SSM_GROUPS)]
    cb = [_bnt(cm[g], bm[g]) for g in range(SSM_GROUPS)]
    return raw, dt, a, lower, tril, cs, cst, bm, cm, cb


def _ssd_head(hh, xbc_ref, dt, cs, cst, lower):
    L = SSM_CHUNK
    ln = DT_LANE + hh
    x = xbc_ref[:, hh * SSM_P:(hh + 1) * SSM_P]
    dtc = dt[:, ln:ln + 1]
    csc = cs[:, ln:ln + 1]
    csr = cst[hh:hh + 1, :]
    decay = jnp.exp(jnp.where(lower, csc - csr, -1e30))
    last = cs[L - 1:L, ln:ln + 1]
    return x, dtc, csc, decay, jnp.exp(csc), jnp.exp(last - csc), jnp.exp(last)


def _ssd_inputs(tail, dt_bias, a_log, d_skip):
    H = SSM_HEADS
    lanes = lambda vec: jnp.pad(vec.reshape(1, H), ((0, 0), (DT_LANE, LANES - DT_LANE - H)))
    return (tail, tail[:, DT_LANE:DT_LANE + H].T, lanes(dt_bias), dt_bias.reshape(H, 1), lanes(a_log),
            a_log.reshape(H, 1), lanes(d_skip))


def ssd_fwd(xbc, tail, dt_bias, a_log, d_skip, *, name, carried=None):
    s = xbc.shape[0]
    L, H, P, N = SSM_CHUNK, SSM_HEADS, SSM_P, SSM_N
    nc = s // L

    def body(xbc_ref, tail_ref, dtrt_ref, bias_ref, biast_ref, alog_ref, alogt_ref, d_ref, y_ref, hp_ref, state):
        @pl.when(pl.program_id(0) == 0)
        def _():
            state[...] = jnp.zeros_like(state)

        raw, dt, a, lower, tril, cs, cst, bm, cm, cb = _ssd_common(
            xbc_ref, tail_ref, dtrt_ref, bias_ref, biast_ref, alog_ref, alogt_ref)
        for hh in range(H):
            g = hh // (H // SSM_GROUPS)
            x, dtc, csc, decay, e, tau, gamma = _ssd_head(hh, xbc_ref, dt, cs, cst, lower)
            xdt = x * dtc
            hprev = state[hh]
            hp_ref[hh] = hprev
            skip = d_ref[:, DT_LANE + hh:DT_LANE + hh + 1]
            y = _bnn(cb[g] * decay, xdt) + _bnn(cm[g], hprev) * e + x * skip
            y_ref[:, hh * P:(hh + 1) * P] = y
            state[hh] = hprev * gamma + _btn(bm[g] * tau, xdt)

    row = lambda w: pl.BlockSpec((L, w), lambda c: (c, 0))
    small = lambda shp: pl.BlockSpec(shp, lambda c: (0, 0))
    return _carry_call(
        body, carried, lambda: pl.program_id(0) == 0, lambda: pl.program_id(0) == nc - 1,
        (xbc, *_ssd_inputs(tail, dt_bias, a_log, d_skip)), name=name,
        out_shape=[jax.ShapeDtypeStruct((s, SSM_INNER), F32), jax.ShapeDtypeStruct((nc, H, N, P), F32)], grid=(nc,),
        in_specs=[row(SSM_CONV_DIM), row(LANES), pl.BlockSpec((H, L), lambda c: (0, c)), small((1, LANES)),
                  small((H, 1)), small((1, LANES)), small((H, 1)), small((1, LANES))],
        out_specs=[row(SSM_INNER), pl.BlockSpec((None, H, N, P), lambda c: (c, 0, 0, 0))],
        scratch=[pltpu.VMEM((H, N, P), F32)], sem=("arbitrary",))


def ssd_bwd(xbc, tail, dt_bias, a_log, d_skip, hprev_all, dy, *, name, carried=None):
    s = xbc.shape[0]
    L, H, P, N = SSM_CHUNK, SSM_HEADS, SSM_P, SSM_N
    nc = s // L
    hg = H // SSM_GROUPS

    def body(xbc_ref, tail_ref, dtrt_ref, bias_ref, biast_ref, alog_ref, alogt_ref, d_ref, hp_ref, dy_ref,
             dxbc_ref, ddt_ref, dbias_ref, dalog_ref, dd_ref, dstate):
        @pl.when(pl.program_id(0) == 0)
        def _():
            dstate[...] = jnp.zeros_like(dstate)
            dbias_ref[...] = jnp.zeros_like(dbias_ref)
            dalog_ref[...] = jnp.zeros_like(dalog_ref)
            dd_ref[...] = jnp.zeros_like(dd_ref)

        raw, dt, a, lower, tril, cs, cst, bm, cm, cb = _ssd_common(
            xbc_ref, tail_ref, dtrt_ref, bias_ref, biast_ref, alog_ref, alogt_ref)
        lane = lax.broadcasted_iota(jnp.int32, (L, LANES), 1)
        lane1 = lax.broadcasted_iota(jnp.int32, (1, LANES), 1)
        rowi = lax.broadcasted_iota(jnp.int32, (L, 1), 0)
        slot = lax.broadcasted_iota(jnp.int32, (LANES, L), 0)
        col_sums = jnp.zeros((LANES, L), F32)
        dcs_all = jnp.zeros((L, LANES), F32)
        ddt_x = jnp.zeros((L, LANES), F32)
        dd_row = jnp.zeros((1, LANES), F32)
        dbm = [jnp.zeros((L, N), F32) for _ in range(SSM_GROUPS)]
        dcm = [jnp.zeros((L, N), F32) for _ in range(SSM_GROUPS)]
        dcb = [jnp.zeros((L, L), F32) for _ in range(SSM_GROUPS)]
        for hh in range(H):
            g = hh // hg
            ln = DT_LANE + hh
            x, dtc, csc, decay, e, tau, gamma = _ssd_head(hh, xbc_ref, dt, cs, cst, lower)
            xdt = x * dtc
            hprev = hp_ref[hh]
            dhn = dstate[hh]
            dyh = dy_ref[:, hh * P:(hh + 1) * P]
            m = cb[g] * decay
            dxdt = _btn(m, dyh) + _bnn(bm[g] * tau, dhn)
            dm = jnp.where(lower, _bnt(dyh, xdt), 0.0)
            dcb[g] = dcb[g] + dm * decay
            dseg = dm * m
            dcs = jnp.sum(dseg, axis=1, keepdims=True)
            col_sums = jnp.where(slot == ln, jnp.sum(dseg, axis=0, keepdims=True), col_sums)
            edy = e * dyh
            dcm[g] = dcm[g] + _bnt(edy, hprev)
            dcs = dcs + e * jnp.sum(dyh * _bnn(cm[g], hprev), axis=1, keepdims=True)
            xdh = _bnt(xdt, dhn)
            dbm[g] = dbm[g] + tau * xdh
            dtau_tau = jnp.sum(bm[g] * xdh, axis=1, keepdims=True) * tau
            dlast = jnp.sum(dtau_tau, axis=0, keepdims=True) + gamma * jnp.sum(dhn * hprev, keepdims=True)
            dcs = dcs - dtau_tau + jnp.where(rowi == L - 1, dlast, 0.0)
            dstate[hh] = gamma * dhn + _btn(cm[g], edy)
            dcs_all = jnp.where(lane == ln, dcs, dcs_all)
            ddt_x = jnp.where(lane == ln, jnp.sum(dxdt * x, axis=1, keepdims=True), ddt_x)
            dxbc_ref[:, hh * P:(hh + 1) * P] = dxdt * dtc + d_ref[:, ln:ln + 1] * dyh
            dd_row = jnp.where(lane1 == ln, jnp.sum(dyh * x, keepdims=True), dd_row)
        off = SSM_INNER + SSM_GROUPS * SSM_N
        for g in range(SSM_GROUPS):
            dxbc_ref[:, SSM_INNER + g * N: SSM_INNER + (g + 1) * N] = dbm[g] + _btn(dcb[g], cm[g])
            dxbc_ref[:, off + g * N: off + (g + 1) * N] = dcm[g] + _bnn(dcb[g], bm[g])
        dcs_all = dcs_all - jnp.transpose(col_sums)
        dda = _tn(tril, dcs_all, HIGHEST)
        head_lane = (lane >= DT_LANE) & (lane < DT_LANE + H)
        draw = jnp.where(head_lane, (dda * a + ddt_x) * _sigmoid(raw), 0.0)
        ddt_ref[...] = draw
        dbias_ref[...] += jnp.sum(draw, axis=0, keepdims=True)
        dalog_ref[...] += jnp.sum(jnp.where(head_lane, dda * dt, 0.0), axis=0, keepdims=True) * a
        dd_ref[...] += dd_row

    rev = lambda c: nc - 1 - c
    row = lambda w: pl.BlockSpec((L, w), lambda c: (rev(c), 0))
    small = lambda shp: pl.BlockSpec(shp, lambda c: (0, 0))
    acc = pl.BlockSpec((1, LANES), lambda c: (0, 0))
    vec = jax.ShapeDtypeStruct((1, LANES), F32)
    return _carry_call(
        body, carried, lambda: pl.program_id(0) == 0, lambda: pl.program_id(0) == nc - 1,
        (xbc, *_ssd_inputs(tail, dt_bias, a_log, d_skip), hprev_all, dy), name=name,
        out_shape=[jax.ShapeDtypeStruct((s, SSM_CONV_DIM), F32), jax.ShapeDtypeStruct((s, LANES), F32), vec, vec, vec],
        grid=(nc,),
        in_specs=[row(SSM_CONV_DIM), row(LANES), pl.BlockSpec((H, L), lambda c: (0, rev(c))), small((1, LANES)),
                  small((H, 1)), small((1, LANES)), small((H, 1)), small((1, LANES)),
                  pl.BlockSpec((None, H, N, P), lambda c: (rev(c), 0, 0, 0)), row(SSM_INNER)],
        out_specs=[row(SSM_CONV_DIM), row(LANES), acc, acc, acc],
        scratch=[pltpu.VMEM((H, N, P), F32)], sem=("arbitrary",))


def _heads(x2d, n, d):
    s = x2d.shape[0]
    return x2d.reshape(s, n, d).transpose(1, 0, 2)


def _unheads(x3d):
    n, s, d = x3d.shape
    return x3d.transpose(1, 0, 2).reshape(s, n * d)


def layer_fwd(h, p, tabs, li, gather_late=None):
    s = h.shape[0]
    tabq, tabt = tabs
    nm = lambda t: f"L{li}_{t}"
    r = {'h': h}
    hn = rms_fwd(h, p['mix_norm'], name=nm('mixnorm'), out_dtype=MXU_DTYPE)
    proj = mm(hn, p['w_in'], name=nm('proj'))
    r.update(hn=hn, proj=proj)
    qkv = proj[:, :3 * SB_WIDTH].astype(MXU_DTYPE)
    riders = {} if gather_late is None else {
        'sb': ('w_out', 'ffn_conv_w'), 'ssd': ('ffn_w_down',), 'mla': ('ffn_w_up',)}
    ride = lambda k: Carried('gather', [gather_late[n] for n in riders[k]]) if riders else None
    got = {}
    ya, *rest = sb_fwd(qkv, name=nm('sb_fwd'), carried=ride('sb'))
    got.update(zip(riders.get('sb', ()), rest))
    yan = rms_fwd(ya, p['sb_out_norm'], name=nm('sbnorm'), out_dtype=MXU_DTYPE)
    r.update(qkv=qkv, ya=ya)
    z = proj[:, 768:1280]
    xbc = proj[None, :, 1280:2048]
    tail = proj[:, TAIL:TAIL + LANES]
    xbc_act = conv_silu_fwd(xbc, p['ssm_conv_w'], p['ssm_conv_b'], name=nm('ssmconv'))[0]
    y_ssm, hprev, *rest = ssd_fwd(xbc_act, tail, p['ssm_dt_bias'], p['ssm_a_log'], p['ssm_d'], name=nm('ssd_fwd'),
                                  carried=ride('ssd'))
    got.update(zip(riders.get('ssd', ()), rest))
    ybn = rms_fwd(y_ssm, p['ssm_out_norm'], name=nm('ssmnorm'), gate=z, out_dtype=MXU_DTYPE)
    r.update(z=z, xbc=xbc, tail=tail, xbc_act=xbc_act, y_ssm=y_ssm, hprev=hprev)
    cq = proj[:, 2048:2304]
    ckv = proj[:, 2304:2432]
    qn = rms_fwd(cq, p['mla_q_norm'], name=nm('qnorm'), out_dtype=MXU_DTYPE)
    q_r = rope(mm(qn, p['mla_w_uq'], name=nm('uq'))[None], tabq, name=nm('ropeq'))
    kvn = rms_fwd(ckv, p['mla_kv_norm'], name=nm('kvnorm'), out_dtype=MXU_DTYPE)
    kv = mm(kvn, p['mla_w_ukv'], name=nm('ukv'))
    k_pe = rope(tail[None], tabt, name=nm('ropek'))[:, :MLA_ROPE]
    qh = _heads(q_r, MLA_HEADS, MLA_QK).astype(MXU_DTYPE)
    kvh = _heads(kv, MLA_HEADS, MLA_NOPE + MLA_V)
    kh = jnp.concatenate([kvh[..., :MLA_NOPE], jnp.broadcast_to(k_pe[None], (MLA_HEADS, s, MLA_ROPE))],
                         axis=-1).astype(MXU_DTYPE)
    vh = kvh[..., MLA_NOPE:].astype(MXU_DTYPE)
    yc_h, lse, *rest = mla_fwd(qh, kh, vh, name=nm('mla_fwd'), carried=ride('mla'))
    got.update(zip(riders.get('mla', ()), rest))
    late = None
    if riders:
        late = assemble_late(got)
        p = dict(p, **{n: late[n][li] for n in LATE})
    yc = _unheads(yc_h)
    ycn = rms_fwd(yc, p['mla_out_norm'], name=nm('mlanorm'), out_dtype=MXU_DTYPE)
    r.update(cq=cq, ckv=ckv, qn=qn, kvn=kvn, qh=qh, kh=kh, vh=vh, yc_h=yc_h, yc=yc, lse=lse)
    ycat = jnp.concatenate([yan, ybn, ycn], axis=1)
    h1 = mm(ycat, p['w_out'], name=nm('outproj'), res=h)
    hn2 = rms_fwd(h1, p['ffn_norm'], name=nm('ffnnorm'), out_dtype=MXU_DTYPE)
    up = mm(hn2, p['ffn_w_up'], name=nm('up'), bb='o', out_dtype=MXU_DTYPE)
    act = conv_glu_fwd(up, p['ffn_conv_w'], p['ffn_conv_b'], name=nm('glu'))
    h2 = mm(act, p['ffn_w_down'], name=nm('down'), ab='k', bb='k', res=h1)
    r.update(ycat=ycat, h1=h1, hn2=hn2, up=up, act=act)
    return h2, r, p, late


def layer_bwd(dh2, p, r, tabs, li, scatter_late=None):
    s = dh2.shape[0]
    tabq, tabt = tabs
    nm = lambda t: f"L{li}_{t}"
    g = {}
    dact = mm(dh2, p['ffn_w_down'], name=nm('d_down_x'), tb=True, bb='o')
    g['ffn_w_down'] = mm(r['act'], dh2, name=nm('d_down_w'), out_dtype=WIRE_DTYPE, ta=True, ab='o')
    du, g['ffn_conv_w'], g['ffn_conv_b'] = conv_glu_bwd(r['up'], dact, p['ffn_conv_w'], p['ffn_conv_b'], name=nm('d_glu'))
    dup = conv_t(du, p['ffn_conv_w'], name=nm('d_ffnconv'), out_dtype=MXU_DTYPE)
    g['ffn_w_up'] = mm(r['hn2'], dup, name=nm('d_up_w'), out_dtype=WIRE_DTYPE, ta=True, bb='o')
    dhn2 = mm(dup, p['ffn_w_up'], name=nm('d_up_x'), tb=True, ab='k', bb='k')
    dh1, dg = rms_bwd(r['h1'], p['ffn_norm'], dhn2, name=nm('d_ffnnorm'), add=dh2)
    g['ffn_norm'] = dg[0]
    dycat = mm(dh1, p['w_out'], name=nm('d_out_x'), tb=True)
    g['w_out'] = mm(r['ycat'], dh1, name=nm('d_out_w'), out_dtype=WIRE_DTYPE, ta=True)
    dya, dg = rms_bwd(r['ya'], p['sb_out_norm'], dycat[:, :256], name=nm('d_sbnorm'))
    g['sb_out_norm'] = dg[0]
    riders = {} if scatter_late is None else {'sb': ('ffn_w_up',), 'ssd': ('ffn_w_down', 'w_out', 'ffn_conv_w')}
    parts = owner_parts_late([g] + list(scatter_late)) if riders else None
    ride = lambda k: Carried('scatter', [parts[n].astype(WIRE_DTYPE) for n in riders[k]]) if riders else None
    recv_late = {} if riders else None
    dq, dk, dv, *rest = sb_bwd(r['qkv'], dya, name=nm('sb_bwd'), carried=ride('sb'))
    if riders:
        recv_late.update(zip(riders['sb'], rest))
    dyssm, dz, dg = rms_bwd(r['y_ssm'], p['ssm_out_norm'], dycat[:, 256:768], name=nm('d_ssmnorm'), gate=r['z'])
    g['ssm_out_norm'] = dg[0]
    dxbc_act, ddt_tail, dbias, dalog, dd, *rest = ssd_bwd(r['xbc_act'], r['tail'], p['ssm_dt_bias'], p['ssm_a_log'],
                                                          p['ssm_d'], r['hprev'], dyssm, name=nm('ssd_bwd'),
                                                          carried=ride('ssd'))
    if riders:
        recv_late.update(zip(riders['ssd'], rest))
    hl = slice(DT_LANE, DT_LANE + SSM_HEADS)
    g['ssm_dt_bias'], g['ssm_a_log'], g['ssm_d'] = dbias[0, hl], dalog[0, hl], dd[0, hl]
    dxbc_u, g['ssm_conv_w'], g['ssm_conv_b'] = conv_silu_bwd(r['xbc'], dxbc_act[None], p['ssm_conv_w'], p['ssm_conv_b'],
                                                             name=nm('d_ssmact'))
    dxbc = conv_t(dxbc_u, p['ssm_conv_w'], name=nm('d_ssmconv'))[0]
    dyc, dg = rms_bwd(r['yc'], p['mla_out_norm'], dycat[:, 768:], name=nm('d_mlanorm'))
    g['mla_out_norm'] = dg[0]
    dqh, dkh, dvh = mla_bwd(r['qh'], r['kh'], r['vh'], r['yc_h'], _heads(dyc, MLA_HEADS, MLA_V), r['lse'], name=nm('mla_bwd'))
    dq_c = rope(_unheads(dqh)[None], tabq, name=nm('d_ropeq'), backward=True)
    g['mla_w_uq'] = mm(r['qn'], dq_c, name=nm('d_uq_w'), out_dtype=WIRE_DTYPE, ta=True)
    dcq, dg = rms_bwd(r['cq'], p['mla_q_norm'], mm(dq_c, p['mla_w_uq'], name=nm('d_uq_x'), tb=True), name=nm('d_qnorm'))
    g['mla_q_norm'] = dg[0]
    dkv = _unheads(jnp.concatenate([dkh[..., :MLA_NOPE], dvh], axis=-1))
    g['mla_w_ukv'] = mm(r['kvn'], dkv, name=nm('d_ukv_w'), out_dtype=WIRE_DTYPE, ta=True)
    dckv, dg = rms_bwd(r['ckv'], p['mla_kv_norm'], mm(dkv, p['mla_w_ukv'], name=nm('d_ukv_x'), tb=True), name=nm('d_kvnorm'))
    g['mla_kv_norm'] = dg[0]
    dkpe = jnp.pad(dkh[..., MLA_NOPE:], ((0, 0), (0, 0), (0, LANES - MLA_ROPE)))
    dtail = rope(dkpe, tabt, name=nm('d_ropek'), backward=True, add=ddt_tail)
    dproj = jnp.concatenate([dq, dk, dv, dz, dxbc, dcq, dckv, dtail], axis=1).astype(MXU_DTYPE)
    g['w_in'] = mm(r['hn'], dproj, name=nm('d_proj_w'), out_dtype=WIRE_DTYPE, ta=True)
    dhn = mm(dproj, p['w_in'], name=nm('d_proj_x'), tb=True)
    dh, dg = rms_bwd(r['h'], p['mix_norm'], dhn, name=nm('d_mixnorm'), add=dh1)
    g['mix_norm'] = dg[0]
    return dh, g, recv_late


def _w_in_placement():
    c = np.arange(D_IN)
    dest = np.where(c < 2048, c, np.where(c < 2056, c + (D_IN - 2056), c - 8))
    dest = jnp.asarray(dest.reshape(N_DEV, D_IN // N_DEV, 1), jnp.int32)
    return (dest == jnp.arange(D_IN_PAD, dtype=jnp.int32)).astype(MXU_DTYPE)


def _owner_major(full, axis):
    shp = full.shape
    return jnp.moveaxis(full.reshape(shp[:axis] + (N_DEV, shp[axis] // N_DEV) + shp[axis + 1:]), axis, 0)


def _owner_join(parts, axis):
    moved = jnp.moveaxis(parts, 0, axis)
    shp = moved.shape
    return moved.reshape(shp[:axis] + (shp[axis] * shp[axis + 1],) + shp[axis + 2:])


def assemble_early(gathered, replicated):
    L = DEPTH
    out = dict(replicated)
    out['w_in'] = mm(gathered['w_in'].reshape(N_DEV, L * D_MODEL, D_IN // N_DEV), _w_in_placement(), name='place_w_in',
                     ab='k', bb='k', out_dtype=MXU_DTYPE).reshape(L, D_MODEL, D_IN_PAD)
    out['mla_w_uq'] = _owner_join(gathered['mla_w_uq'], 2)
    out['mla_w_ukv'] = _owner_join(gathered['mla_w_ukv'], 2)
    out['ssm_conv_w'] = _owner_join(gathered['ssm_conv_w'], 2)[:, None]
    out['ssm_conv_b'] = replicated['ssm_conv_b'].reshape(L, 1, 1, SSM_CONV_DIM)
    out['ffn_conv_b'] = replicated['ffn_conv_b'].reshape(L, N_DEV, 1, FF_SHARD)
    return out


def assemble_late(gathered):
    L = DEPTH
    return {'ffn_w_up': jnp.moveaxis(gathered['ffn_w_up'], 1, 0),
            'w_out': _owner_join(gathered['w_out'], 1),
            'ffn_w_down': _owner_join(gathered['ffn_w_down'], 1).reshape(L, N_DEV // 2, FF_SHARD, D_MODEL),
            'ffn_conv_w': jnp.moveaxis(gathered['ffn_conv_w'], 1, 0)}


def owner_parts_late(grads):
    L = DEPTH
    st = lambda n: jnp.stack([g[n] for g in grads])
    return {'ffn_w_up': jnp.moveaxis(st('ffn_w_up'), 1, 0),
            'w_out': _owner_major(st('w_out'), 1),
            'ffn_w_down': _owner_major(st('ffn_w_down').reshape(L, D_FF, D_MODEL), 1),
            'ffn_conv_w': jnp.moveaxis(st('ffn_conv_w'), 1, 0)}


def owner_parts_early(grads):
    L = DEPTH
    st = lambda n: jnp.stack([g[n] for g in grads])
    parts = {
        'w_in': mm(st('w_in').reshape(L * D_MODEL, D_IN_PAD), _w_in_placement(), name='unplace_w_in', tb=True, bb='o',
                   out_dtype=WIRE_DTYPE).reshape(N_DEV, L, D_MODEL, D_IN // N_DEV),
        'mla_w_uq': _owner_major(st('mla_w_uq'), 2),
        'mla_w_ukv': _owner_major(st('mla_w_ukv'), 2),
        'ssm_conv_w': _owner_major(st('ssm_conv_w')[:, 0], 2),
    }
    rep = {n: st(n) for n in REPLICATED if n not in ('final_norm', 'ssm_conv_b', 'ffn_conv_b')}
    rep['ssm_conv_b'] = st('ssm_conv_b').reshape(L, SSM_CONV_DIM)
    rep['ffn_conv_b'] = st('ffn_conv_b').reshape(L, 2 * D_FF)
    return parts, rep


def local_step(x, positions, target, early, late_shards, replicated):
    s = x.shape[0]
    tabs = _rope_tables(positions, s)
    params = assemble_early(early, replicated)
    layer = lambda li: {n: params[n][li] for n in params if n != 'final_norm'}
    h, r0, p0, late = layer_fwd(x, layer(0), tabs, 0, gather_late=late_shards)
    saved = [(p0, r0)]
    for li in range(1, DEPTH):
        h, r, p, _ = layer_fwd(h, dict(layer(li), **{n: late[n][li] for n in LATE}), tabs, li)
        saved.append((p, r))
    y = rms_fwd(h, params['final_norm'], name='finalnorm')
    dy, loss = loss_head(y, target, name='loss')
    dh, dg = rms_bwd(h, params['final_norm'], dy, name='d_finalnorm')
    above = []
    for li in reversed(range(1, DEPTH)):
        dh, g, _ = layer_bwd(dh, *saved[li], tabs, li)
        above.insert(0, g)
    dh, g0, recv_late = layer_bwd(dh, *saved[0], tabs, 0, scatter_late=above)
    parts, rep = owner_parts_early([g0] + above)
    rep['final_norm'] = dg[0]
    return loss[0, 0], dh, parts, recv_late, rep


def all_gather(blocks, *, name):
    n = len(blocks)

    def body(*refs):
        x_refs, out_refs = refs[:n], refs[n:2 * n]
        send_sems, recv_sems, local_sems = refs[2 * n:]
        x, y, c = lax.axis_index("x"), lax.axis_index("y"), lax.axis_index("c")
        me, sibling = (x, y, c), (x, y, 1 - c)
        chips = [(1 - x, y), (x, 1 - y), (1 - x, 1 - y)]

        def slot(b, px, py, pc):
            return out_refs[b].at[4 * px + 2 * py + pc]

        def copy(b, k, blk, to, src=None):
            return pltpu.make_async_remote_copy(src_ref=slot(b, *blk) if src is None else src, dst_ref=slot(b, *blk),
                                                send_sem=send_sems.at[b, k], recv_sem=recv_sems.at[b, k],
                                                device_id=to, device_id_type=MESH)

        mine = [pltpu.make_async_copy(x_refs[b], slot(b, *me), local_sems.at[b]) for b in range(n)]
        for cp in mine:
            cp.start()
        first = []
        for b in range(n):
            first.append(copy(b, 0, me, sibling, src=x_refs[b]))
            first += [copy(b, 1 + j, me, (*chip, c), src=x_refs[b]) for j, chip in enumerate(chips)]
        for cp in first:
            cp.start()
        passed = []
        for j, chip in enumerate(chips):
            for b in range(n):
                copy(b, 1 + j, (*chip, c), me).wait_recv()
                fwd = copy(b, 4 + j, (*chip, c), sibling)
                fwd.start()
                passed.append(fwd)
        for b in range(n):
            copy(b, 0, sibling, me).wait_recv()
            for j, chip in enumerate(chips):
                copy(b, 4 + j, (*chip, 1 - c), me).wait_recv()
        for cp in first + passed:
            cp.wait_send()
        for cp in mine:
            cp.wait()

    return pl.pallas_call(
        body, name=name, out_shape=[jax.ShapeDtypeStruct((N_DEV,) + b.shape, b.dtype) for b in blocks],
        in_specs=[HBM] * n, out_specs=[HBM] * n,
        scratch_shapes=[pltpu.SemaphoreType.DMA((n, 7)), pltpu.SemaphoreType.DMA((n, 7)), pltpu.SemaphoreType.DMA((n,))],
    )(*blocks)


def all_to_all(parts, *, name):
    n = len(parts)

    def body(*refs):
        g_refs, r_refs = refs[:n], refs[n:2 * n]
        send_sems, recv_sems, local_sems = refs[2 * n:]
        x, y, c = lax.axis_index("x"), lax.axis_index("y"), lax.axis_index("c")
        me = 4 * x + 2 * y + c
        mine = [pltpu.make_async_copy(g_refs[b].at[me], r_refs[b].at[me], local_sems.at[b]) for b in range(n)]
        for cp in mine:
            cp.start()
        copies = []
        for k in range(1, N_DEV):
            px, py, pc = _flip(x, k & 4), _flip(y, k & 2), _flip(c, k & 1)
            peer = 4 * px + 2 * py + pc
            for b in range(n):
                cp = pltpu.make_async_remote_copy(src_ref=g_refs[b].at[peer], dst_ref=r_refs[b].at[me],
                                                  send_sem=send_sems.at[b, k - 1], recv_sem=recv_sems.at[b, k - 1],
                                                  device_id=(px, py, pc), device_id_type=MESH)
                cp.start()
                copies.append(cp)
        for cp in copies:
            cp.wait_send()
            cp.wait_recv()
        for cp in mine:
            cp.wait()

    return pl.pallas_call(
        body, name=name, out_shape=[jax.ShapeDtypeStruct(p.shape, p.dtype) for p in parts],
        in_specs=[HBM] * n, out_specs=[HBM] * n,
        scratch_shapes=[pltpu.SemaphoreType.DMA((n, 7)), pltpu.SemaphoreType.DMA((n, 7)), pltpu.SemaphoreType.DMA((n,))],
    )(*parts)


def adamw(parts, w, m, v, *, name):
    r, wd = w.shape
    br = _tile(r, (256, 128, 64, 32, 16, 8))
    c1 = 1.0 - ADAM_B1 ** ADAM_STEP
    c2 = 1.0 - ADAM_B2 ** ADAM_STEP

    def body(p_ref, w_ref, m_ref, v_ref, g_ref, d_ref, mo_ref, vo_ref):
        g = p_ref[0].astype(F32)
        for j in range(1, N_DEV):
            g = g + p_ref[j].astype(F32)
        mn = ADAM_B1 * m_ref[...] + (1.0 - ADAM_B1) * g
        vn = ADAM_B2 * v_ref[...] + (1.0 - ADAM_B2) * (g * g)
        g_ref[...] = g
        mo_ref[...] = mn
        vo_ref[...] = vn
        d_ref[...] = -ADAM_LR * ((mn / c1) / (jnp.sqrt(vn / c2) + ADAM_EPS) + ADAM_WD * w_ref[...])

    blk = pl.BlockSpec((br, wd), lambda i: (i, 0))
    out = jax.ShapeDtypeStruct((r, wd), F32)
    return _call(body, name=name, out_shape=[out] * 4, grid=(r // br,),
                 in_specs=[pl.BlockSpec((N_DEV, br, wd), lambda i: (0, i, 0)), blk, blk, blk], out_specs=[blk] * 4,
                 sem=("parallel",))(parts, w, m, v)


def _pack(arrs):
    flat = jnp.concatenate([a.reshape(-1) for a in arrs])
    rows = -(-flat.shape[0] // (8 * FLAT_W)) * 8
    return jnp.pad(flat, (0, rows * FLAT_W - flat.shape[0])).reshape(rows, FLAT_W)


def _unpack(flat, shapes):
    flat = flat.reshape(-1)
    out, off = [], 0
    for shp in shapes:
        n = int(np.prod(shp))
        out.append(flat[off:off + n].reshape(shp))
        off += n
    return out


def kernel(x, positions, mix_norm, w_in, sb_out_norm, ssm_conv_w, ssm_conv_b, ssm_dt_bias, ssm_a_log, ssm_d, ssm_out_norm, mla_q_norm, mla_w_uq, mla_kv_norm, mla_w_ukv, mla_out_norm, w_out, ffn_norm, ffn_w_up, ffn_conv_w, ffn_conv_b, ffn_w_down, final_norm, loss_target, m_mix_norm, m_w_in, m_sb_out_norm, m_ssm_conv_w, m_ssm_conv_b, m_ssm_dt_bias, m_ssm_a_log, m_ssm_d, m_ssm_out_norm, m_mla_q_norm, m_mla_w_uq, m_mla_kv_norm, m_mla_w_ukv, m_mla_out_norm, m_w_out, m_ffn_norm, m_ffn_w_up, m_ffn_conv_w, m_ffn_conv_b, m_ffn_w_down, m_final_norm, v_mix_norm, v_w_in, v_sb_out_norm, v_ssm_conv_w, v_ssm_conv_b, v_ssm_dt_bias, v_ssm_a_log, v_ssm_d, v_ssm_out_norm, v_mla_q_norm, v_mla_w_uq, v_mla_kv_norm, v_mla_w_ukv, v_mla_out_norm, v_w_out, v_ffn_norm, v_ffn_w_up, v_ffn_conv_w, v_ffn_conv_b, v_ffn_w_down, v_final_norm):
    args = locals()
    w = {n: args[n] for n in WEIGHTS}
    m = {n: args['m_' + n] for n in WEIGHTS}
    v = {n: args['v_' + n] for n in WEIGHTS}
    wire = lambda n: w[n] if n in VPU_WEIGHTS else w[n].astype(BF16)
    early = dict(zip(EARLY, all_gather([wire(n) for n in EARLY], name='gather_early')))

    loss, dx, parts, recv, rep = local_step(x[0], positions[0], loss_target[0], early, {n: wire(n) for n in LATE},
                                            {n: w[n] for n in REPLICATED})
    loss = lax.psum(loss, ("x", "y", "c"))

    recv.update(zip(EARLY, all_to_all([parts[n].astype(WIRE_DTYPE) for n in EARLY], name='scatter_early')))
    res = {kind: {} for kind in 'gdmv'}
    for n, rv in recv.items():
        shp = w[n].shape
        two_d = (int(np.prod(shp[:-1])), shp[-1])
        outs = adamw(rv.reshape((N_DEV,) + two_d), w[n].reshape(two_d), m[n].reshape(two_d), v[n].reshape(two_d),
                     name='adamw_' + n)
        for kind, o in zip('gdmv', outs):
            res[kind][n] = o.reshape(shp)

    rep_shapes = [w[n].shape for n in REPLICATED]
    (rparts,) = all_gather([_pack([rep[n] for n in REPLICATED])], name='gather_small_grads')
    rflat = lambda d: _pack([d[n] for n in REPLICATED])
    routs = adamw(rparts, rflat(w), rflat(m), rflat(v), name='adamw_replicated')
    for kind, o in zip('gdmv', routs):
        res[kind].update(zip(REPLICATED, _unpack(o, rep_shapes)))

    return (loss, dx[None], *[res['g'][n] for n in WEIGHTS], *[res['d'][n] for n in WEIGHTS],
            *[res['m'][n] for n in WEIGHTS], *[res['v'][n] for n in WEIGHTS])
```

```python
import numpy as np
import jax
import jax.numpy as jnp
from jax import lax
from jax.experimental import pallas as pl
from jax.experimental.pallas import tpu as pltpu

F32 = jnp.float32
BF16 = jnp.bfloat16
MXU_DTYPE = jnp.bfloat16
HIGHEST = lax.Precision.HIGHEST
WIRE_DTYPE = jnp.bfloat16

N_DEV = 8
D_MODEL = 1024
DEPTH = 2
EPS = 1e-6
SB_HEADS, SB_DIM = 4, 64
SB_WIDTH = SB_HEADS * SB_DIM
SSM_HEADS, SSM_P, SSM_GROUPS, SSM_N, SSM_CONV, SSM_CHUNK = 8, 64, 2, 64, 4, 128
SSM_INNER = SSM_HEADS * SSM_P
SSM_CONV_DIM = SSM_INNER + 2 * SSM_GROUPS * SSM_N
MLA_HEADS, MLA_NOPE, MLA_ROPE, MLA_V, MLA_Q_RANK, MLA_KV_RANK = 4, 64, 32, 64, 256, 128
MLA_QK = MLA_NOPE + MLA_ROPE
ROPE_THETA = 10000.0
D_IN = 2472
D_IN_PAD = 2560
TAIL = 2432
DT_LANE = 32
D_FF = 2816
FF_SHARD = 2 * D_FF // N_DEV
ADAM_LR, ADAM_B1, ADAM_B2, ADAM_EPS, ADAM_WD, ADAM_STEP = 0.001, 0.9, 0.999, 1e-08, 0.01, 10

LANES = 128
ATT_BLK = 256
SB_WIDE = 2
SB_SPENT = -110.0
MLA_WIDE = 4
ROW_BLK = 512
ROW_BLOCK_BYTES = 2 << 20
CONV_COLS = 256
FLAT_W = 1024
VMEM_LIMIT = 56 << 20
MM_TM = (1024, 512, 256, 128)
MM_TN = (1280, 1024, 768, 640, 512, 384, 256, 128)
MM_TK = (2048, 1280, 1024, 512, 256, 128)

WEIGHTS = ['mix_norm', 'w_in', 'sb_out_norm', 'ssm_conv_w', 'ssm_conv_b', 'ssm_dt_bias', 'ssm_a_log', 'ssm_d',
           'ssm_out_norm', 'mla_q_norm', 'mla_w_uq', 'mla_kv_norm', 'mla_w_ukv', 'mla_out_norm', 'w_out',
           'ffn_norm', 'ffn_w_up', 'ffn_conv_w', 'ffn_conv_b', 'ffn_w_down', 'final_norm']
SHARDED = {'w_in': 2, 'ssm_conv_w': 2, 'mla_w_uq': 2, 'mla_w_ukv': 2, 'w_out': 1, 'ffn_w_up': 2, 'ffn_conv_w': 2,
           'ffn_w_down': 1}
VPU_WEIGHTS = ('ssm_conv_w', 'ffn_conv_w')
EARLY = ('w_in', 'mla_w_uq', 'mla_w_ukv', 'ssm_conv_w')
LATE = ('w_out', 'ffn_w_up', 'ffn_conv_w', 'ffn_w_down')
REPLICATED = [n for n in WEIGHTS if n not in SHARDED]


def _call(body, *, name, out_shape, grid=(), in_specs=None, out_specs=None, scratch=(), sem=None, **kw):
    params = dict(vmem_limit_bytes=VMEM_LIMIT)
    if sem is not None:
        params['dimension_semantics'] = sem
    return pl.pallas_call(body, name=name, out_shape=out_shape, grid=grid, in_specs=in_specs, out_specs=out_specs,
                          scratch_shapes=list(scratch), compiler_params=pltpu.CompilerParams(**params), **kw)


def _tile(n, prefs):
    for t in prefs:
        if n % t == 0:
            return t
    return n


def _rows(s, w):
    rows = ROW_BLK
    while rows * 2 <= s and s % (rows * 2) == 0 and rows * 2 * w * 4 <= ROW_BLOCK_BYTES:
        rows *= 2
    return _tile(s, (rows,))


def _dot(a, b, dims, precision=None):
    return lax.dot_general(a, b, (dims, ((), ())), preferred_element_type=F32, precision=precision)


def _nn(a, b, precision=None):
    return _dot(a, b, ((1,), (0,)), precision)


def _nt(a, b, precision=None):
    return _dot(a, b, ((1,), (1,)), precision)


def _tn(a, b, precision=None):
    return _dot(a, b, ((0,), (0,)), precision)


def _mxu(f):
    return lambda a, b: f(a.astype(MXU_DTYPE), b.astype(MXU_DTYPE))


_bnn, _bnt, _btn = _mxu(_nn), _mxu(_nt), _mxu(_tn)


def _split2(x):
    hi = x.astype(MXU_DTYPE)
    lo = (x - hi.astype(F32)).astype(MXU_DTYPE)
    return hi, lo


def _sigmoid(x):
    return 0.5 * jnp.tanh(0.5 * x) + 0.5


def _softplus(x):
    return jnp.maximum(x, 0.0) + jnp.log1p(jnp.exp(-jnp.abs(x)))


def _softplus_att(x):
    return jnp.maximum(x, 0.0) + jnp.log(1.0 + jnp.exp(-jnp.abs(x)))


def _cum(x, u):
    rows, b = x.shape[0], u.shape[0]
    n = x.shape[1] // b
    hi, lo = _split2(x)
    stack = [part[:, t * b:(t + 1) * b] for part in (hi, lo) for t in range(n)]
    r = _nn(jnp.concatenate(stack, axis=0), u)
    return jnp.concatenate([r[t * rows:(t + 1) * rows] + r[(n + t) * rows:(n + t + 1) * rows] for t in range(n)], axis=1)


def _diagonal_group(qi, tile, carry, width):
    base = (qi // width) * width
    return lax.switch(qi - base, [lambda c, n=n: tile(base, c, True, n) for n in range(1, width + 1)], carry)


def _causal_loop(qi, tile, carry, width, first=0):
    carry = lax.fori_loop(first, qi // width, lambda i, c: tile(i * width, c, False, width), carry)
    return _diagonal_group(qi, tile, carry, width)


def _spent_loop(qi, tile, carry, width, live):
    carry = _diagonal_group(qi, tile, carry, width)
    step = lambda st: (st[0] - 1, tile((st[0] - 1) * width, st[1], False, width))
    return lax.while_loop(lambda st: (st[0] > 0) & live(st[1]), step, (qi // width, carry))


def _causal_mask(blk, width, qi, kb, heads, strict, keys_on_rows=False):
    shape = (width * blk, blk) if keys_on_rows else (heads * blk, width * blk)
    q_idx = lax.broadcasted_iota(jnp.int32, shape, 1 if keys_on_rows else 0)
    k_idx = lax.broadcasted_iota(jnp.int32, shape, 0 if keys_on_rows else 1)
    if heads > 1:
        q_idx = q_idx % blk
    gap = (qi - kb) * blk
    return k_idx < q_idx + gap if strict else k_idx <= q_idx + gap


def mm(a, b, *, name, ta=False, tb=False, res=None, out_dtype=F32, ab=None, bb=None, precision=None):
    a2, b2 = a.shape[-2:], b.shape[-2:]
    (kdim, m) = a2 if ta else a2[::-1]
    (n, k2) = b2 if tb else b2[::-1]
    assert kdim == k2, (a.shape, b.shape, ta, tb)
    assert (ab == 'k') == (bb == 'k')
    kb = ab == 'k'
    nb = a.shape[0] if ab == 'o' else (b.shape[0] if bb == 'o' else None)
    tm, tn = _tile(m, MM_TM if res is not None else (2 * MM_TM[0],) + MM_TM), _tile(n, MM_TN)
    tk = kdim if kb else _tile(kdim, MM_TK)
    nk = a.shape[0] if kb else kdim // tk
    dims = ((0 if ta else 1,), (1 if tb else 0,))
    op_dtype = F32 if precision is not None else MXU_DTYPE

    def body(*refs):
        a_ref, b_ref = refs[0], refs[1]
        r_ref = refs[2] if res is not None else None
        o_ref = refs[3] if res is not None else refs[2]
        part = _dot(a_ref[...].astype(op_dtype), b_ref[...].astype(op_dtype), dims, precision)

        def finish(out):
            if res is not None:
                out = out + r_ref[...]
            o_ref[...] = out.astype(out_dtype)

        if nk == 1:
            finish(part)
            return
        acc = refs[-1]
        k = pl.program_id(3)

        @pl.when(k == 0)
        def _():
            acc[...] = part

        @pl.when(k > 0)
        def _():
            acc[...] += part

        @pl.when(k == nk - 1)
        def _():
            finish(acc[...])

    def spec(blk, idx, how):
        if how is None:
            return pl.BlockSpec(blk, idx)
        if how == 'o':
            return pl.BlockSpec((None,) + blk, lambda p, i, j, k: (p,) + idx(p, i, j, k))
        return pl.BlockSpec((None,) + blk, lambda p, i, j, k: (k,) + idx(p, i, j, 0))

    a_spec = spec((tk, tm), lambda p, i, j, k: (k, i), ab) if ta else spec((tm, tk), lambda p, i, j, k: (i, k), ab)
    b_spec = spec((tn, tk), lambda p, i, j, k: (j, k), bb) if tb else spec((tk, tn), lambda p, i, j, k: (k, j), bb)
    o_spec = spec((tm, tn), lambda p, i, j, k: (i, j), None if nb is None else 'o')
    ins, specs = [a, b], [a_spec, b_spec]
    if res is not None:
        ins.append(res)
        specs.append(o_spec)
    out_shape = (m, n) if nb is None else (nb, m, n)
    return _call(body, name=name, out_shape=jax.ShapeDtypeStruct(out_shape, out_dtype),
                 grid=(1 if nb is None else nb, m // tm, n // tn, nk), in_specs=specs, out_specs=o_spec,
                 scratch=[] if nk == 1 else [pltpu.VMEM((tm, tn), F32)],
                 sem=("parallel", "parallel", "parallel", "arbitrary"))(*ins)


def rms_fwd(x, g, *, name, gate=None, out_dtype=F32, carried=None):
    s, w = x.shape
    bs = _rows(s, w)

    def body(*refs):
        if gate is None:
            x_ref, g_ref, o_ref = refs
            u = x_ref[...]
        else:
            x_ref, z_ref, g_ref, o_ref = refs
            z = z_ref[...]
            u = x_ref[...] * (z * _sigmoid(z))
        r = lax.rsqrt(jnp.mean(u * u, axis=1, keepdims=True) + EPS)
        o_ref[...] = (u * r * g_ref[...]).astype(out_dtype)

    row = pl.BlockSpec((bs, w), lambda i: (i, 0))
    vec = pl.BlockSpec((1, w), lambda i: (0, 0))
    ins = [x] + ([] if gate is None else [gate]) + [g.reshape(1, w)]
    specs = [row] + ([] if gate is None else [row]) + [vec]
    nb = s // bs
    res = _carry_call(body, carried, lambda: pl.program_id(0) == 0, lambda: pl.program_id(0) == nb - 1, ins, name=name,
                      out_shape=[jax.ShapeDtypeStruct((s, w), out_dtype)], grid=(nb,), in_specs=specs, out_specs=[row],
                      sem=("arbitrary",))
    return res[0] if carried is None else res


def rms_bwd(x, g, dy, *, name, gate=None, add=None):
    s, w = x.shape
    bs = _rows(s, w)

    def body(*refs):
        refs = list(refs)
        x_ref = refs.pop(0)
        z_ref = refs.pop(0) if gate is not None else None
        g_ref = refs.pop(0)
        dy_ref = refs.pop(0)
        add_ref = refs.pop(0) if add is not None else None
        dx_ref = refs.pop(0)
        dz_ref = refs.pop(0) if gate is not None else None
        dg_ref = refs.pop(0)
        i = pl.program_id(0)

        @pl.when(i == 0)
        def _():
            dg_ref[...] = jnp.zeros_like(dg_ref)

        xv = x_ref[...]
        if gate is not None:
            z = z_ref[...]
            sg = _sigmoid(z)
            act = z * sg
            u = xv * act
        else:
            u = xv
        r = lax.rsqrt(jnp.mean(u * u, axis=1, keepdims=True) + EPS)
        dy_v = dy_ref[...]
        dyg = dy_v * g_ref[...]
        du = r * dyg - u * (r * r * r * jnp.mean(dyg * u, axis=1, keepdims=True))
        dg_ref[...] += jnp.sum(dy_v * u * r, axis=0, keepdims=True)
        if gate is not None:
            dx = du * act
            dz_ref[...] = du * xv * (sg * (1.0 + z * (1.0 - sg)))
        else:
            dx = du
        if add is not None:
            dx = dx + add_ref[...]
        dx_ref[...] = dx

    row = pl.BlockSpec((bs, w), lambda i: (i, 0))
    vec = pl.BlockSpec((1, w), lambda i: (0, 0))
    ins = [x] + ([] if gate is None else [gate]) + [g.reshape(1, w), dy] + ([] if add is None else [add])
    specs = [row] + ([] if gate is None else [row]) + [vec, row] + ([] if add is None else [row])
    outs = [jax.ShapeDtypeStruct((s, w), F32)] + ([] if gate is None else [jax.ShapeDtypeStruct((s, w), F32)])
    outs.append(jax.ShapeDtypeStruct((1, w), F32))
    ospecs = [row] + ([] if gate is None else [row]) + [vec]
    return _call(body, name=name, out_shape=outs, grid=(s // bs,), in_specs=specs, out_specs=ospecs,
                 sem=("arbitrary",))(*ins)


def loss_head(y, target, *, name):
    s, w = y.shape
    bs = _rows(s, w)
    nb = s // bs

    def body(y_ref, t_ref, dy_ref, loss_ref, acc):
        i = pl.program_id(0)

        @pl.when(i == 0)
        def _():
            acc[...] = jnp.zeros_like(acc)

        e = y_ref[...] - t_ref[...]
        dy_ref[...] = e * (1.0 / w)
        acc[...] += jnp.sum(e * e, axis=0, keepdims=True)

        @pl.when(i == nb - 1)
        def _():
            loss_ref[...] = jnp.sum(acc[...], axis=1, keepdims=True) * (0.5 / w)

    row = pl.BlockSpec((bs, w), lambda i: (i, 0))
    return _call(body, name=name, out_shape=[jax.ShapeDtypeStruct((s, w), F32), jax.ShapeDtypeStruct((1, 1), F32)],
                 grid=(nb,), in_specs=[row, row], out_specs=[row, pl.BlockSpec((1, 1), lambda i: (0, 0))],
                 scratch=[pltpu.VMEM((1, w), F32)], sem=("arbitrary",))(y, target)


def _rope_tables(positions, s):
    inv_freq = 1.0 / (ROPE_THETA ** (jnp.arange(0, MLA_ROPE, 2, dtype=F32) / MLA_ROPE))
    ang = positions.reshape(s, 1).astype(F32) * inv_freq
    cos, sin = jnp.cos(ang), jnp.sin(ang)
    one, zero = jnp.ones((s, MLA_NOPE), F32), jnp.zeros((s, MLA_NOPE), F32)
    cq = jnp.tile(jnp.concatenate([one, cos, cos], axis=1), (1, MLA_HEADS))
    sq = jnp.tile(jnp.concatenate([zero, sin, sin], axis=1), (1, MLA_HEADS))
    pad1, pad0 = jnp.ones((s, LANES - MLA_ROPE), F32), jnp.zeros((s, LANES - MLA_ROPE), F32)
    ct = jnp.concatenate([cos, cos, pad1], axis=1)
    st = jnp.concatenate([sin, sin, pad0], axis=1)
    half = MLA_ROPE // 2

    def swap(width, starts):
        r = np.zeros((width, width), np.float32)
        for o in starts:
            for i in range(half):
                r[o + half + i, o + i] = -1.0
                r[o + i, o + half + i] = 1.0
        return jnp.asarray(r)

    rq = swap(MLA_HEADS * MLA_QK, [h * MLA_QK + MLA_NOPE for h in range(MLA_HEADS)])
    rt = swap(LANES, [0])
    return (cq, sq, rq), (ct, st, rt)


def rope(x, tabs, *, name, backward=False, add=None):
    cos, sin, rot = tabs
    n, s, w = x.shape
    bs = _rows(s, w)

    def body(*refs):
        if add is None:
            x_ref, c_ref, s_ref, r_ref, o_ref = refs
        else:
            x_ref, c_ref, s_ref, r_ref, a_ref, o_ref = refs
        xv = x_ref[0]
        for j in range(1, n):
            xv = xv + x_ref[j]
        if backward:
            out = xv * c_ref[...] + _nt(xv * s_ref[...], r_ref[...], HIGHEST)
        else:
            out = xv * c_ref[...] + _nn(xv, r_ref[...], HIGHEST) * s_ref[...]
        if add is not None:
            out = out + a_ref[...]
        o_ref[...] = out

    row = pl.BlockSpec((bs, w), lambda i: (i, 0))
    ins = [x, cos, sin, rot] + ([] if add is None else [add])
    specs = [pl.BlockSpec((n, bs, w), lambda i: (0, i, 0)), row, row, pl.BlockSpec((w, w), lambda i: (0, 0))]
    specs += [] if add is None else [row]
    return _call(body, name=name, out_shape=jax.ShapeDtypeStruct((s, w), F32), grid=(s // bs,), in_specs=specs,
                 out_specs=row, sem=("parallel",))(*ins)


MESH = pl.DeviceIdType.MESH
HBM = pl.BlockSpec(memory_space=pltpu.HBM)


def _flip(v, bit):
    return 1 - v if bit else v


class Carried:
    def __init__(self, kind, arrays):
        assert kind in ('gather', 'scatter')
        self.kind, self.arrays, self.n = kind, list(arrays), len(arrays)

    @property
    def out_shape(self):
        lead = (N_DEV,) if self.kind == 'gather' else ()
        return [jax.ShapeDtypeStruct(lead + a.shape, a.dtype) for a in self.arrays]

    @property
    def scratch(self):
        return [pltpu.SemaphoreType.DMA((self.n, N_DEV - 1)), pltpu.SemaphoreType.DMA((self.n, N_DEV - 1)),
                pltpu.SemaphoreType.DMA((self.n,))]

    def _copies(self, in_refs, out_refs, sems):
        send_sems, recv_sems, local_sems = sems
        x, y, c = lax.axis_index("x"), lax.axis_index("y"), lax.axis_index("c")
        me = 4 * x + 2 * y + c
        part = (lambda b, p: in_refs[b]) if self.kind == 'gather' else (lambda b, p: in_refs[b].at[p])
        local = [pltpu.make_async_copy(part(b, me), out_refs[b].at[me], local_sems.at[b]) for b in range(self.n)]
        remote = []
        for k in range(1, N_DEV):
            px, py, pc = _flip(x, k & 4), _flip(y, k & 2), _flip(c, k & 1)
            for b in range(self.n):
                remote.append(pltpu.make_async_remote_copy(
                    src_ref=part(b, 4 * px + 2 * py + pc), dst_ref=out_refs[b].at[me], send_sem=send_sems.at[b, k - 1],
                    recv_sem=recv_sems.at[b, k - 1], device_id=(px, py, pc), device_id_type=MESH))
        return local, remote

    def start(self, in_refs, out_refs, sems):
        local, remote = self._copies(in_refs, out_refs, sems)
        for cp in local + remote:
            cp.start()

    def wait(self, in_refs, out_refs, sems):
        local, remote = self._copies(in_refs, out_refs, sems)
        for cp in remote:
            cp.wait_send()
            cp.wait_recv()
        for cp in local:
            cp.wait()


def _first_last_step(n0, n1):
    at = lambda a, b: (pl.program_id(0) == a) & (pl.program_id(1) == b)
    return (lambda: at(0, 0)), (lambda: at(n0 - 1, n1 - 1))


def _carry_call(body, carried, first, last, ins, *, out_shape, in_specs, out_specs, scratch=(), **kw):
    if carried is None:
        return _call(body, out_shape=out_shape, in_specs=in_specs, out_specs=out_specs, scratch=scratch, **kw)(*ins)
    n, n_in, n_out, n_scr = carried.n, len(in_specs), len(out_specs), len(scratch)

    def riding(*refs):
        own_in, ride_in = refs[:n_in], refs[n_in:n_in + n]
        rest = refs[n_in + n:]
        own_out, ride_out = rest[:n_out], rest[n_out:n_out + n]
        own_scr, sems = rest[n_out + n:n_out + n + n_scr], rest[n_out + n + n_scr:]
        pl.when(first())(lambda: carried.start(ride_in, ride_out, sems))
        body(*own_in, *own_out, *own_scr)
        pl.when(last())(lambda: carried.wait(ride_in, ride_out, sems))

    return _call(riding, out_shape=list(out_shape) + carried.out_shape, in_specs=list(in_specs) + [HBM] * n,
                 out_specs=list(out_specs) + [HBM] * n, scratch=list(scratch) + carried.scratch, **kw)(*ins, *carried.arrays)


def _tri(n, op):
    r = lax.broadcasted_iota(jnp.int32, (n, n), 0)
    c = lax.broadcasted_iota(jnp.int32, (n, n), 1)
    return r, c, op(r, c)


def _pair_split(x, first):
    zero = jnp.zeros_like(x)
    return jnp.where(first, x, zero), jnp.where(first, zero, x)


def _sb_specs(s, blk):
    npair = SB_WIDTH // LANES
    q = pl.BlockSpec((blk, LANES), lambda j, i: (i, j))
    k = pl.BlockSpec((s, LANES), lambda j, i: (0, npair + j))
    v = pl.BlockSpec((s, LANES), lambda j, i: (0, 2 * npair + j))
    full = pl.BlockSpec((s, LANES), lambda j, i: (0, j))
    return q, k, v, full


def _stack_heads(x, first):
    return jnp.concatenate(_pair_split(x, first), axis=0)


def _unstack_heads(x, first, blk):
    return jnp.where(first, x[:blk], x[blk:])


def sb_fwd(qkv, *, name, carried=None):
    s = qkv.shape[0]
    blk = _tile(s, (ATT_BLK,))
    scale = SB_DIM ** -0.5
    npair, nq = SB_WIDTH // LANES, s // blk
    assert nq % SB_WIDE == 0

    def body(q_ref, k_ref, v_ref, y_ref):
        qi = pl.program_id(1)
        first = lax.broadcasted_iota(jnp.int32, (blk, LANES), 1) < SB_DIM
        q2 = _stack_heads((q_ref[...].astype(F32) * scale).astype(MXU_DTYPE), first)
        row, col, later_mask = _tri(blk, lambda r, c: r > c)
        u_later = later_mask.astype(MXU_DTYPE)

        def tile(kb, carry, masked, n):
            c, acc = carry
            keys = pl.ds(pl.multiple_of(kb * blk, blk), n * blk)
            z = _nt(q2, k_ref[keys, :])
            sp = _softplus_att(z)
            if masked:
                valid = _causal_mask(blk, n, qi, kb, 2, True)
            spm = jnp.where(valid, sp, 0.0) if masked else sp
            later = _cum(spm, u_later)
            sums = [jnp.sum(spm[:, t * blk:(t + 1) * blk], axis=1, keepdims=True) for t in range(n)]
            after, cols = c, [None] * n
            for t in reversed(range(n)):
                cols[t] = jnp.broadcast_to(after, (2 * blk, blk))
                after = after - sums[t]
            w = jnp.exp((z - sp) - later + (cols[0] if n == 1 else jnp.concatenate(cols, axis=1)))
            if masked:
                w = jnp.where(valid, w, 0.0)
            return after, acc + _nn(w.astype(MXU_DTYPE), v_ref[keys, :])

        zero = (jnp.zeros((2 * blk, 1), F32), jnp.zeros((2 * blk, LANES), F32))
        _, (c, acc) = _spent_loop(qi, tile, zero, SB_WIDE, lambda cr: jnp.max(cr[0]) >= SB_SPENT)
        y_ref[...] = _unstack_heads(acc, first, blk)

    qspec, kspec, vspec, _ = _sb_specs(s, blk)
    return _carry_call(body, carried, *_first_last_step(npair, nq), (qkv, qkv, qkv), name=name,
                       out_shape=[jax.ShapeDtypeStruct((s, SB_WIDTH), F32)], grid=(npair, nq),
                       in_specs=[qspec, kspec, vspec], out_specs=[qspec], sem=("arbitrary", "arbitrary"))


def sb_bwd(qkv, dy, *, name, carried=None):
    s = qkv.shape[0]
    blk = _tile(s, (ATT_BLK,))
    scale = SB_DIM ** -0.5
    npair, nq = SB_WIDTH // LANES, s // blk
    assert nq % SB_WIDE == 0

    def body(q_ref, k_ref, v_ref, dy_ref, dq_ref, dk_ref, dv_ref):
        qi = pl.program_id(1)

        @pl.when(qi == 0)
        def _():
            dk_ref[...] = jnp.zeros_like(dk_ref)
            dv_ref[...] = jnp.zeros_like(dv_ref)

        first = lax.broadcasted_iota(jnp.int32, (blk, LANES), 1) < SB_DIM
        q2 = _stack_heads((q_ref[...].astype(F32) * scale).astype(MXU_DTYPE), first)
        dy2 = _stack_heads(dy_ref[...].astype(MXU_DTYPE), first)
        row, col, incl_mask = _tri(blk, lambda r, c: r <= c)
        u_incl = incl_mask.astype(MXU_DTYPE)
        u_excl = (row < col).astype(MXU_DTYPE)

        def walk(kb, c, masked, n):
            sp = _softplus_att(_nt(q2, k_ref[pl.ds(pl.multiple_of(kb * blk, blk), n * blk), :]))
            if masked:
                sp = jnp.where(_causal_mask(blk, n, qi, kb, 2, True), sp, 0.0)
            for t in reversed(range(n)):
                c = c - jnp.sum(sp[:, t * blk:(t + 1) * blk], axis=1, keepdims=True)
            return c

        start, tv = _spent_loop(qi, walk, jnp.zeros((2 * blk, 1), F32), SB_WIDE, lambda c: jnp.max(c) >= SB_SPENT)

        def prefixed(x, carry):
            n = x.shape[1] // blk
            cols = []
            for t in range(n):
                cols.append(jnp.broadcast_to(carry, (2 * blk, blk)))
                carry = carry + jnp.sum(x[:, t * blk:(t + 1) * blk], axis=1, keepdims=True)
            return (cols[0] if n == 1 else jnp.concatenate(cols, axis=1)), carry

        def tile(kb, carry, masked, n):
            p, gc, dq = carry
            keys = pl.ds(pl.multiple_of(kb * blk, blk), n * blk)
            kv = k_ref[keys, :]
            z = _nt(q2, kv)
            dw = _nt(dy2, v_ref[keys, :])
            sp = _softplus_att(z)
            if masked:
                valid = _causal_mask(blk, n, qi, kb, 2, True)
            spm = jnp.where(valid, sp, 0.0) if masked else sp
            before, p = prefixed(spm, p)
            w = jnp.exp((z - sp) + (_cum(spm, u_incl) + before))
            if masked:
                w = jnp.where(valid, w, 0.0)
            g = w * dw
            gbefore, gc = prefixed(g, gc)
            gb = g.astype(MXU_DTYPE)
            gin = _nn(jnp.concatenate([gb[:, t * blk:(t + 1) * blk] for t in range(n)], axis=0), u_excl)
            gex = gbefore + jnp.concatenate([gin[t * 2 * blk:(t + 1) * 2 * blk] for t in range(n)], axis=1)
            keep = jnp.exp(-spm)
            dz = keep * (g + gex) - gex
            if masked:
                dz = jnp.where(valid, dz, 0.0)
            dzb = dz.astype(MXU_DTYPE)
            dk_ref[keys, :] += _tn(dzb, q2)
            dv_ref[keys, :] += _tn(w.astype(MXU_DTYPE), dy2)
            return p, gc, dq + _nn(dzb, kv)

        zero = jnp.zeros((2 * blk, 1), F32)
        _, _, dq = _causal_loop(qi, tile, (tv, zero, jnp.zeros((2 * blk, LANES), F32)), SB_WIDE, first=start)
        dq_ref[...] = _unstack_heads(dq, first, blk) * scale

    qspec, kspec, vspec, full = _sb_specs(s, blk)
    out = jax.ShapeDtypeStruct((s, SB_WIDTH), F32)
    return _carry_call(body, carried, *_first_last_step(npair, nq), (qkv, qkv, qkv, dy), name=name,
                       out_shape=[out, out, out], grid=(npair, nq), in_specs=[qspec, kspec, vspec, qspec],
                       out_specs=[qspec, full, full], sem=("arbitrary", "arbitrary"))


ATT_PAIR = 2


def _mla_specs(s, blk, dk, dv):
    q = pl.BlockSpec((ATT_PAIR, blk, dk), lambda hp, i: (hp, i, 0))
    k = pl.BlockSpec((ATT_PAIR, s, dk), lambda hp, i: (hp, 0, 0))
    v = pl.BlockSpec((ATT_PAIR, s, dv), lambda hp, i: (hp, 0, 0))
    y = pl.BlockSpec((ATT_PAIR, blk, dv), lambda hp, i: (hp, i, 0))
    lse = pl.BlockSpec((ATT_PAIR, blk, LANES), lambda hp, i: (hp, i, 0))
    return q, k, v, y, lse


def mla_fwd(q, k, v, *, name, carried=None):
    h, s, dk = q.shape
    dv = v.shape[-1]
    blk = _tile(s, (ATT_BLK,))
    scale = dk ** -0.5
    assert (s // blk) % MLA_WIDE == 0

    def body(q_ref, k_ref, v_ref, y_ref, l_ref):
        qi = pl.program_id(1)

        def tile(kb, carry, masked, n):
            keys = pl.ds(pl.multiple_of(kb * blk, blk), n * blk)
            out = []
            for hh in range(ATT_PAIR):
                m, l, acc = carry[hh]
                sc = _nt(q_ref[hh], k_ref[hh, keys, :]) * scale
                if masked:
                    sc = jnp.where(_causal_mask(blk, n, qi, kb, 1, False), sc, -1e30)
                m2 = jnp.maximum(m, jnp.max(sc, axis=1, keepdims=True))
                p = jnp.exp(sc - m2)
                a = jnp.exp(m - m2)
                out.append((m2, a * l + jnp.sum(p, axis=1, keepdims=True),
                            a * acc + _nn(p.astype(MXU_DTYPE), v_ref[hh, keys, :])))
            return tuple(out)

        init = (jnp.full((blk, 1), -1e30, F32), jnp.zeros((blk, 1), F32), jnp.zeros((blk, dv), F32))
        for hh, (m, l, acc) in enumerate(_causal_loop(qi, tile, (init,) * ATT_PAIR, MLA_WIDE)):
            y_ref[hh] = acc / l
            l_ref[hh] = jnp.broadcast_to(m + jnp.log(l), (blk, LANES))

    qspec, kspec, vspec, yspec, lspec = _mla_specs(s, blk, dk, dv)
    grid = (h // ATT_PAIR, s // blk)
    return _carry_call(body, carried, *_first_last_step(*grid), (q, k, v), name=name,
                       out_shape=[jax.ShapeDtypeStruct((h, s, dv), F32), jax.ShapeDtypeStruct((h, s, LANES), F32)],
                       grid=grid, in_specs=[qspec, kspec, vspec], out_specs=[yspec, lspec],
                       sem=("arbitrary", "arbitrary"))


def mla_bwd(q, k, v, y, dy, lse, *, name):
    h, s, dk = q.shape
    dv = v.shape[-1]
    blk = _tile(s, (ATT_BLK,))
    scale = dk ** -0.5
    assert (s // blk) % MLA_WIDE == 0

    def body(q_ref, k_ref, v_ref, y_ref, dy_ref, l_ref, dq_ref, dk_ref, dv_ref):
        qi = pl.program_id(1)

        @pl.when(qi == 0)
        def _():
            dk_ref[...] = jnp.zeros_like(dk_ref)
            dv_ref[...] = jnp.zeros_like(dv_ref)

        as_row = lambda col: jnp.transpose(jnp.broadcast_to(col, (blk, LANES)))[0:1, :]
        dyv = [dy_ref[hh].astype(MXU_DTYPE) for hh in range(ATT_PAIR)]
        delta = [as_row(jnp.sum(dy_ref[hh] * y_ref[hh], axis=1, keepdims=True)) for hh in range(ATT_PAIR)]
        lv = [as_row(l_ref[hh, :, 0:1]) for hh in range(ATT_PAIR)]
        def tile(kb, dqs, masked, n):
            keys = pl.ds(pl.multiple_of(kb * blk, blk), n * blk)
            out = []
            for hh in range(ATT_PAIR):
                qv = q_ref[hh]
                kv = k_ref[hh, keys, :]
                p = jnp.exp(_nt(kv, qv) * scale - lv[hh])
                if masked:
                    p = jnp.where(_causal_mask(blk, n, qi, kb, 1, False, keys_on_rows=True), p, 0.0)
                ds = (p * (_nt(v_ref[hh, keys, :], dyv[hh]) - delta[hh])).astype(MXU_DTYPE)
                dk_ref[hh, keys, :] += _nn(ds, qv) * scale
                dv_ref[hh, keys, :] += _nn(p.astype(MXU_DTYPE), dyv[hh])
                out.append(dqs[hh] + _tn(ds, kv))
            return tuple(out)

        for hh, dq in enumerate(_causal_loop(qi, tile, (jnp.zeros((blk, dk), F32),) * ATT_PAIR, MLA_WIDE)):
            dq_ref[hh] = dq * scale

    qspec, kspec, vspec, yspec, lspec = _mla_specs(s, blk, dk, dv)
    return _call(body, name=name,
                 out_shape=[jax.ShapeDtypeStruct((h, s, dk), F32), jax.ShapeDtypeStruct((h, s, dk), F32),
                            jax.ShapeDtypeStruct((h, s, dv), F32)],
                 grid=(h // ATT_PAIR, s // blk), in_specs=[qspec, kspec, vspec, yspec, yspec, lspec],
                 out_specs=[qspec, kspec, vspec], sem=("parallel", "arbitrary"))(q, k, v, y, dy, lse)


HALO = 8
CONV_CHUNK = 16


def _conv_tiles(x):
    s, c = x.shape[-2:]
    return s, c, _tile(s, (ROW_BLK,)), _tile(c, (CONV_COLS,))


def _halo_rows(dtype):
    return HALO * 4 // jnp.dtype(dtype).itemsize


def _conv_specs(bs, cw, lead=(), dtype=F32):
    zero = (0,) * len(lead)
    hr = _halo_rows(dtype)
    blk = pl.BlockSpec(lead + (None, bs, cw), lambda p, j, i: zero + (p, i, j))
    halo = pl.BlockSpec(lead + (None, hr, cw), lambda p, j, i: zero + (p, jnp.maximum(i * (bs // hr) - 1, 0), j))
    w = lambda kk: pl.BlockSpec(lead + (None, kk, cw), lambda p, j, i: zero + (p, 0, j))
    return blk, halo, w


def _stage(scr, x_ref, halo_ref, first):
    hr = halo_ref.shape[0]
    scr[0:HALO, :] = jnp.where(first, 0.0, halo_ref[hr - HALO:hr, :].astype(F32))
    scr[HALO:, :] = x_ref[...].astype(F32)


def _shifted(ext, shift):
    return ext[HALO:] if shift == 0 else pltpu.roll(ext, shift, 0)[HALO:]


def _conv_taps(scr, kk, r0):
    ext = scr[pl.ds(r0, CONV_CHUNK + HALO), :]
    return [_shifted(ext, kk - 1 - k) for k in range(kk)]


def _conv_sum(taps, w_ref, b_ref):
    u = b_ref[...] + taps[0] * w_ref[0:1, :]
    for k in range(1, len(taps)):
        u = u + taps[k] * w_ref[k:k + 1, :]
    return u


def _fold(x):
    out = x[0:8]
    for r in range(8, CONV_CHUNK, 8):
        out = out + x[r:r + 8]
    return out


class _TapSums:
    def __init__(self, kk, cw):
        self.w = [jnp.zeros((8, cw), F32) for _ in range(kk)]
        self.b = jnp.zeros((8, cw), F32)

    def add(self, du, taps):
        self.w = [a + _fold(du * t) for a, t in zip(self.w, taps)]
        self.b = self.b + _fold(du)

    def flush(self, dw_ref, db_ref):
        for k, a in enumerate(self.w):
            dw_ref[k:k + 1, :] += jnp.sum(a, axis=0, keepdims=True)
        db_ref[...] += jnp.sum(self.b, axis=0, keepdims=True)


def _silu_grad(u):
    sg = _sigmoid(u)
    return sg * (1.0 + u * (1.0 - sg))


def conv_silu_fwd(x, w, b, *, name):
    s, c, bs, cw = _conv_tiles(x)
    kk = w.shape[1]

    def body(x_ref, h_ref, w_ref, b_ref, o_ref, scr):
        _stage(scr, x_ref, h_ref, pl.program_id(2) == 0)
        for r0 in range(0, bs, CONV_CHUNK):
            u = _conv_sum(_conv_taps(scr, kk, r0), w_ref, b_ref)
            o_ref[pl.ds(r0, CONV_CHUNK), :] = u * _sigmoid(u)

    blk, halo, wspec = _conv_specs(bs, cw, dtype=x.dtype)
    return _call(body, name=name, out_shape=jax.ShapeDtypeStruct(x.shape, F32), grid=(x.shape[0], c // cw, s // bs),
                 in_specs=[blk, halo, wspec(kk), wspec(1)], out_specs=blk, scratch=[pltpu.VMEM((bs + HALO, cw), F32)],
                 sem=("parallel", "parallel", "arbitrary"))(x, x, w, b)


def conv_silu_bwd(x, dy, w, b, *, name):
    s, c, bs, cw = _conv_tiles(x)
    kk = w.shape[1]

    def body(x_ref, h_ref, w_ref, b_ref, dy_ref, du_ref, dw_ref, db_ref, scr):
        i = pl.program_id(2)

        @pl.when(i == 0)
        def _():
            dw_ref[...] = jnp.zeros_like(dw_ref)
            db_ref[...] = jnp.zeros_like(db_ref)

        _stage(scr, x_ref, h_ref, i == 0)
        sums = _TapSums(kk, cw)
        for r0 in range(0, bs, CONV_CHUNK):
            taps = _conv_taps(scr, kk, r0)
            du = dy_ref[pl.ds(r0, CONV_CHUNK), :] * _silu_grad(_conv_sum(taps, w_ref, b_ref))
            du_ref[pl.ds(r0, CONV_CHUNK), :] = du
            sums.add(du, taps)
        sums.flush(dw_ref, db_ref)

    blk, halo, wspec = _conv_specs(bs, cw, dtype=x.dtype)
    return _call(body, name=name,
                 out_shape=[jax.ShapeDtypeStruct(x.shape, F32), jax.ShapeDtypeStruct(w.shape, F32),
                            jax.ShapeDtypeStruct(b.shape, F32)],
                 grid=(x.shape[0], c // cw, s // bs), in_specs=[blk, halo, wspec(kk), wspec(1), blk],
                 out_specs=[blk, wspec(kk), wspec(1)], scratch=[pltpu.VMEM((bs + HALO, cw), F32)],
                 sem=("parallel", "parallel", "arbitrary"))(x, x, w, b, dy)


def _glu_view(a):
    return a.reshape((2, a.shape[0] // 2) + a.shape[1:])


def conv_glu_fwd(x, w, b, *, name):
    s, c, bs, cw = _conv_tiles(x)
    kk = w.shape[1]
    half = x.shape[0] // 2

    def body(x_ref, h_ref, w_ref, b_ref, o_ref, gscr, vscr):
        first = pl.program_id(2) == 0
        _stage(gscr, x_ref.at[0], h_ref.at[0], first)
        _stage(vscr, x_ref.at[1], h_ref.at[1], first)
        for r0 in range(0, bs, CONV_CHUNK):
            gate = _conv_sum(_conv_taps(gscr, kk, r0), w_ref.at[0], b_ref.at[0])
            val = _conv_sum(_conv_taps(vscr, kk, r0), w_ref.at[1], b_ref.at[1])
            o_ref[pl.ds(r0, CONV_CHUNK), :] = (gate * _sigmoid(gate) * val).astype(o_ref.dtype)

    blk, halo, wspec = _conv_specs(bs, cw, lead=(2,), dtype=x.dtype)
    out, _, _ = _conv_specs(bs, cw)
    xv = _glu_view(x)
    return _call(body, name=name, out_shape=jax.ShapeDtypeStruct((half, s, c), MXU_DTYPE), grid=(half, c // cw, s // bs),
                 in_specs=[blk, halo, wspec(kk), wspec(1)], out_specs=out, scratch=[pltpu.VMEM((bs + HALO, cw), F32)] * 2,
                 sem=("parallel", "parallel", "arbitrary"))(xv, xv, _glu_view(w), _glu_view(b))


def conv_glu_bwd(x, da, w, b, *, name):
    s, c, bs, cw = _conv_tiles(x)
    kk = w.shape[1]
    half = x.shape[0] // 2

    def body(x_ref, h_ref, w_ref, b_ref, da_ref, du_ref, dw_ref, db_ref, gscr, vscr):
        i = pl.program_id(2)

        @pl.when(i == 0)
        def _():
            dw_ref[...] = jnp.zeros_like(dw_ref)
            db_ref[...] = jnp.zeros_like(db_ref)

        _stage(gscr, x_ref.at[0], h_ref.at[0], i == 0)
        _stage(vscr, x_ref.at[1], h_ref.at[1], i == 0)
        gsums, vsums = _TapSums(kk, cw), _TapSums(kk, cw)
        for r0 in range(0, bs, CONV_CHUNK):
            gtaps, vtaps = _conv_taps(gscr, kk, r0), _conv_taps(vscr, kk, r0)
            gate = _conv_sum(gtaps, w_ref.at[0], b_ref.at[0])
            val = _conv_sum(vtaps, w_ref.at[1], b_ref.at[1])
            dav = da_ref[pl.ds(r0, CONV_CHUNK), :]
            dgate = dav * val * _silu_grad(gate)
            dval = dav * gate * _sigmoid(gate)
            du_ref[0, pl.ds(r0, CONV_CHUNK), :] = dgate.astype(du_ref.dtype)
            du_ref[1, pl.ds(r0, CONV_CHUNK), :] = dval.astype(du_ref.dtype)
            gsums.add(dgate, gtaps)
            vsums.add(dval, vtaps)
        gsums.flush(dw_ref.at[0], db_ref.at[0])
        vsums.flush(dw_ref.at[1], db_ref.at[1])

    blk, halo, wspec = _conv_specs(bs, cw, lead=(2,), dtype=x.dtype)
    daspec, _, _ = _conv_specs(bs, cw)
    xv, wv, bv = _glu_view(x), _glu_view(w), _glu_view(b)
    du, dw, db = _call(body, name=name,
                       out_shape=[jax.ShapeDtypeStruct(xv.shape, x.dtype), jax.ShapeDtypeStruct(wv.shape, F32),
                                  jax.ShapeDtypeStruct(bv.shape, F32)],
                       grid=(half, c // cw, s // bs), in_specs=[blk, halo, wspec(kk), wspec(1), daspec],
                       out_specs=[blk, wspec(kk), wspec(1)], scratch=[pltpu.VMEM((bs + HALO, cw), F32)] * 2,
                       sem=("parallel", "parallel", "arbitrary"))(xv, xv, wv, bv, da)
    return du.reshape(x.shape), dw.reshape(w.shape), db.reshape(b.shape)


def conv_t(du, w, *, name, out_dtype=F32):
    s, c, bs, cw = _conv_tiles(du)
    kk = w.shape[1]
    nb = s // bs

    def body(d_ref, h_ref, w_ref, o_ref, scr):
        last = pl.program_id(2) == nb - 1
        scr[0:bs, :] = d_ref[...].astype(F32)
        scr[bs:, :] = jnp.where(last, 0.0, h_ref[0:HALO, :].astype(F32))
        for r0 in range(0, bs, CONV_CHUNK):
            ext = scr[pl.ds(r0, CONV_CHUNK + HALO), :]
            ahead = lambda j: ext[:CONV_CHUNK] if j == 0 else pltpu.roll(ext, CONV_CHUNK + HALO - j, 0)[:CONV_CHUNK]
            acc = ahead(kk - 1) * w_ref[0:1, :]
            for k in range(1, kk):
                acc = acc + ahead(kk - 1 - k) * w_ref[k:k + 1, :]
            o_ref[pl.ds(r0, CONV_CHUNK), :] = acc.astype(out_dtype)

    blk, _, wspec = _conv_specs(bs, cw)
    hr = _halo_rows(du.dtype)
    halo = pl.BlockSpec((None, hr, cw), lambda q, j, i: (q, jnp.minimum((i + 1) * (bs // hr), s // hr - 1), j))
    return _call(body, name=name, out_shape=jax.ShapeDtypeStruct(du.shape, out_dtype), grid=(du.shape[0], c // cw, nb),
                 in_specs=[blk, halo, wspec(kk)], out_specs=blk, scratch=[pltpu.VMEM((bs + HALO, cw), F32)],
                 sem=("parallel", "parallel", "arbitrary"))(du, du, w)


def _ssd_common(xbc_ref, tail_ref, dtrt_ref, bias_ref, biast_ref, alog_ref, alogt_ref):
    L = SSM_CHUNK
    raw = tail_ref[...] + bias_ref[...]
    dt = _softplus(raw)
    dtt = _softplus(dtrt_ref[...] + biast_ref[...])
    a = -jnp.exp(alog_ref[...])
    at = -jnp.exp(alogt_ref[...])
    row, col, lower = _tri(L, lambda r, c: r >= c)
    tril = lower.astype(F32)
    cs = _nn(tril, dt * a, HIGHEST)
    cst = _nt(dtt * at, tril, HIGHEST)
    bm = [xbc_ref[:, SSM_INNER + g * SSM_N: SSM_INNER + (g + 1) * SSM_N] for g in range(SSM_GROUPS)]
    off = SSM_INNER + SSM_GROUPS * SSM_N
    cm = [xbc_ref[:, off + g * SSM_N: off + (g + 1) * SSM_N] for g in range(SSM_GROUPS)]
    cb = [_bnt(cm[g], bm[g]) for g in range(SSM_GROUPS)]
    return raw, dt, a, lower, tril, cs, cst, bm, cm, cb


def _ssd_head(hh, xbc_ref, dt, cs, cst, lower):
    L = SSM_CHUNK
    ln = DT_LANE + hh
    x = xbc_ref[:, hh * SSM_P:(hh + 1) * SSM_P]
    dtc = dt[:, ln:ln + 1]
    csc = cs[:, ln:ln + 1]
    csr = cst[hh:hh + 1, :]
    decay = jnp.exp(jnp.where(lower, csc - csr, -1e30))
    last = cs[L - 1:L, ln:ln + 1]
    return x, dtc, csc, decay, jnp.exp(csc), jnp.exp(last - csc), jnp.exp(last)


def _ssd_inputs(tail, dt_bias, a_log, d_skip):
    H = SSM_HEADS
    lanes = lambda vec: jnp.pad(vec.reshape(1, H), ((0, 0), (DT_LANE, LANES - DT_LANE - H)))
    return (tail, tail[:, DT_LANE:DT_LANE + H].T, lanes(dt_bias), dt_bias.reshape(H, 1), lanes(a_log),
            a_log.reshape(H, 1), lanes(d_skip))


def ssd_fwd(xbc, tail, dt_bias, a_log, d_skip, *, name, carried=None):
    s = xbc.shape[0]
    L, H, P, N = SSM_CHUNK, SSM_HEADS, SSM_P, SSM_N
    nc = s // L

    def body(xbc_ref, tail_ref, dtrt_ref, bias_ref, biast_ref, alog_ref, alogt_ref, d_ref, y_ref, hp_ref, state):
        @pl.when(pl.program_id(0) == 0)
        def _():
            state[...] = jnp.zeros_like(state)

        raw, dt, a, lower, tril, cs, cst, bm, cm, cb = _ssd_common(
            xbc_ref, tail_ref, dtrt_ref, bias_ref, biast_ref, alog_ref, alogt_ref)
        for hh in range(H):
            g = hh // (H // SSM_GROUPS)
            x, dtc, csc, decay, e, tau, gamma = _ssd_head(hh, xbc_ref, dt, cs, cst, lower)
            xdt = x * dtc
            hprev = state[hh]
            hp_ref[hh] = hprev
            skip = d_ref[:, DT_LANE + hh:DT_LANE + hh + 1]
            y = _bnn(cb[g] * decay, xdt) + _bnn(cm[g], hprev) * e + x * skip
            y_ref[:, hh * P:(hh + 1) * P] = y
            state[hh] = hprev * gamma + _btn(bm[g] * tau, xdt)

    row = lambda w: pl.BlockSpec((L, w), lambda c: (c, 0))
    small = lambda shp: pl.BlockSpec(shp, lambda c: (0, 0))
    return _carry_call(
        body, carried, lambda: pl.program_id(0) == 0, lambda: pl.program_id(0) == nc - 1,
        (xbc, *_ssd_inputs(tail, dt_bias, a_log, d_skip)), name=name,
        out_shape=[jax.ShapeDtypeStruct((s, SSM_INNER), F32), jax.ShapeDtypeStruct((nc, H, N, P), F32)], grid=(nc,),
        in_specs=[row(SSM_CONV_DIM), row(LANES), pl.BlockSpec((H, L), lambda c: (0, c)), small((1, LANES)),
                  small((H, 1)), small((1, LANES)), small((H, 1)), small((1, LANES))],
        out_specs=[row(SSM_INNER), pl.BlockSpec((None, H, N, P), lambda c: (c, 0, 0, 0))],
        scratch=[pltpu.VMEM((H, N, P), F32)], sem=("arbitrary",))


def ssd_bwd(xbc, tail, dt_bias, a_log, d_skip, hprev_all, dy, *, name, carried=None):
    s = xbc.shape[0]
    L, H, P, N = SSM_CHUNK, SSM_HEADS, SSM_P, SSM_N
    nc = s // L
    hg = H // SSM_GROUPS

    def body(xbc_ref, tail_ref, dtrt_ref, bias_ref, biast_ref, alog_ref, alogt_ref, d_ref, hp_ref, dy_ref,
             dxbc_ref, ddt_ref, dbias_ref, dalog_ref, dd_ref, dstate):
        @pl.when(pl.program_id(0) == 0)
        def _():
            dstate[...] = jnp.zeros_like(dstate)
            dbias_ref[...] = jnp.zeros_like(dbias_ref)
            dalog_ref[...] = jnp.zeros_like(dalog_ref)
            dd_ref[...] = jnp.zeros_like(dd_ref)

        raw, dt, a, lower, tril, cs, cst, bm, cm, cb = _ssd_common(
            xbc_ref, tail_ref, dtrt_ref, bias_ref, biast_ref, alog_ref, alogt_ref)
        lane = lax.broadcasted_iota(jnp.int32, (L, LANES), 1)
        lane1 = lax.broadcasted_iota(jnp.int32, (1, LANES), 1)
        rowi = lax.broadcasted_iota(jnp.int32, (L, 1), 0)
        slot = lax.broadcasted_iota(jnp.int32, (LANES, L), 0)
        col_sums = jnp.zeros((LANES, L), F32)
        dcs_all = jnp.zeros((L, LANES), F32)
        ddt_x = jnp.zeros((L, LANES), F32)
        dd_row = jnp.zeros((1, LANES), F32)
        dbm = [jnp.zeros((L, N), F32) for _ in range(SSM_GROUPS)]
        dcm = [jnp.zeros((L, N), F32) for _ in range(SSM_GROUPS)]
        dcb = [jnp.zeros((L, L), F32) for _ in range(SSM_GROUPS)]
        for hh in range(H):
            g = hh // hg
            ln = DT_LANE + hh
            x, dtc, csc, decay, e, tau, gamma = _ssd_head(hh, xbc_ref, dt, cs, cst, lower)
            xdt = x * dtc
            hprev = hp_ref[hh]
            dhn = dstate[hh]
            dyh = dy_ref[:, hh * P:(hh + 1) * P]
            m = cb[g] * decay
            dxdt = _btn(m, dyh) + _bnn(bm[g] * tau, dhn)
            dm = jnp.where(lower, _bnt(dyh, xdt), 0.0)
            dcb[g] = dcb[g] + dm * decay
            dseg = dm * m
            dcs = jnp.sum(dseg, axis=1, keepdims=True)
            col_sums = jnp.where(slot == ln, jnp.sum(dseg, axis=0, keepdims=True), col_sums)
            edy = e * dyh
            dcm[g] = dcm[g] + _bnt(edy, hprev)
            dcs = dcs + e * jnp.sum(dyh * _bnn(cm[g], hprev), axis=1, keepdims=True)
            xdh = _bnt(xdt, dhn)
            dbm[g] = dbm[g] + tau * xdh
            dtau_tau = jnp.sum(bm[g] * xdh, axis=1, keepdims=True) * tau
            dlast = jnp.sum(dtau_tau, axis=0, keepdims=True) + gamma * jnp.sum(dhn * hprev, keepdims=True)
            dcs = dcs - dtau_tau + jnp.where(rowi == L - 1, dlast, 0.0)
            dstate[hh] = gamma * dhn + _btn(cm[g], edy)
            dcs_all = jnp.where(lane == ln, dcs, dcs_all)
            ddt_x = jnp.where(lane == ln, jnp.sum(dxdt * x, axis=1, keepdims=True), ddt_x)
            dxbc_ref[:, hh * P:(hh + 1) * P] = dxdt * dtc + d_ref[:, ln:ln + 1] * dyh
            dd_row = jnp.where(lane1 == ln, jnp.sum(dyh * x, keepdims=True), dd_row)
        off = SSM_INNER + SSM_GROUPS * SSM_N
        for g in range(SSM_GROUPS):
            dxbc_ref[:, SSM_INNER + g * N: SSM_INNER + (g + 1) * N] = dbm[g] + _btn(dcb[g], cm[g])
            dxbc_ref[:, off + g * N: off + (g + 1) * N] = dcm[g] + _bnn(dcb[g], bm[g])
        dcs_all = dcs_all - jnp.transpose(col_sums)
        dda = _tn(tril, dcs_all, HIGHEST)
        head_lane = (lane >= DT_LANE) & (lane < DT_LANE + H)
        draw = jnp.where(head_lane, (dda * a + ddt_x) * _sigmoid(raw), 0.0)
        ddt_ref[...] = draw
        dbias_ref[...] += jnp.sum(draw, axis=0, keepdims=True)
        dalog_ref[...] += jnp.sum(jnp.where(head_lane, dda * dt, 0.0), axis=0, keepdims=True) * a
        dd_ref[...] += dd_row

    rev = lambda c: nc - 1 - c
    row = lambda w: pl.BlockSpec((L, w), lambda c: (rev(c), 0))
    small = lambda shp: pl.BlockSpec(shp, lambda c: (0, 0))
    acc = pl.BlockSpec((1, LANES), lambda c: (0, 0))
    vec = jax.ShapeDtypeStruct((1, LANES), F32)
    return _carry_call(
        body, carried, lambda: pl.program_id(0) == 0, lambda: pl.program_id(0) == nc - 1,
        (xbc, *_ssd_inputs(tail, dt_bias, a_log, d_skip), hprev_all, dy), name=name,
        out_shape=[jax.ShapeDtypeStruct((s, SSM_CONV_DIM), F32), jax.ShapeDtypeStruct((s, LANES), F32), vec, vec, vec],
        grid=(nc,),
        in_specs=[row(SSM_CONV_DIM), row(LANES), pl.BlockSpec((H, L), lambda c: (0, rev(c))), small((1, LANES)),
                  small((H, 1)), small((1, LANES)), small((H, 1)), small((1, LANES)),
                  pl.BlockSpec((None, H, N, P), lambda c: (rev(c), 0, 0, 0)), row(SSM_INNER)],
        out_specs=[row(SSM_CONV_DIM), row(LANES), acc, acc, acc],
        scratch=[pltpu.VMEM((H, N, P), F32)], sem=("arbitrary",))


def _heads(x2d, n, d):
    s = x2d.shape[0]
    return x2d.reshape(s, n, d).transpose(1, 0, 2)


def _unheads(x3d):
    n, s, d = x3d.shape
    return x3d.transpose(1, 0, 2).reshape(s, n * d)


def layer_fwd(h, p, tabs, li, gather_late=None, hn=None):
    s = h.shape[0]
    tabq, tabt = tabs
    nm = lambda t: f"L{li}_{t}"
    r = {'h': h}
    if hn is None:
        hn = rms_fwd(h, p['mix_norm'], name=nm('mixnorm'), out_dtype=MXU_DTYPE)
    proj = mm(hn, p['w_in'], name=nm('proj'))
    r.update(hn=hn, proj=proj)
    qkv = proj[:, :3 * SB_WIDTH].astype(MXU_DTYPE)
    riders = {} if gather_late is None else {
        'sb': ('w_out', 'ffn_conv_w'), 'ssd': ('ffn_w_down',), 'mla': ('ffn_w_up',)}
    ride = lambda k: Carried('gather', [gather_late[n] for n in riders[k]]) if riders else None
    got = {}
    ya, *rest = sb_fwd(qkv, name=nm('sb_fwd'), carried=ride('sb'))
    got.update(zip(riders.get('sb', ()), rest))
    yan = rms_fwd(ya, p['sb_out_norm'], name=nm('sbnorm'), out_dtype=MXU_DTYPE)
    r.update(qkv=qkv, ya=ya)
    z = proj[:, 768:1280]
    xbc = proj[None, :, 1280:2048]
    tail = proj[:, TAIL:TAIL + LANES]
    xbc_act = conv_silu_fwd(xbc, p['ssm_conv_w'], p['ssm_conv_b'], name=nm('ssmconv'))[0]
    y_ssm, hprev, *rest = ssd_fwd(xbc_act, tail, p['ssm_dt_bias'], p['ssm_a_log'], p['ssm_d'], name=nm('ssd_fwd'),
                                  carried=ride('ssd'))
    got.update(zip(riders.get('ssd', ()), rest))
    ybn = rms_fwd(y_ssm, p['ssm_out_norm'], name=nm('ssmnorm'), gate=z, out_dtype=MXU_DTYPE)
    r.update(z=z, xbc=xbc, tail=tail, xbc_act=xbc_act, y_ssm=y_ssm, hprev=hprev)
    cq = proj[:, 2048:2304]
    ckv = proj[:, 2304:2432]
    qn = rms_fwd(cq, p['mla_q_norm'], name=nm('qnorm'), out_dtype=MXU_DTYPE)
    q_r = rope(mm(qn, p['mla_w_uq'], name=nm('uq'))[None], tabq, name=nm('ropeq'))
    kvn = rms_fwd(ckv, p['mla_kv_norm'], name=nm('kvnorm'), out_dtype=MXU_DTYPE)
    kv = mm(kvn, p['mla_w_ukv'], name=nm('ukv'))
    k_pe = rope(tail[None], tabt, name=nm('ropek'))[:, :MLA_ROPE]
    qh = _heads(q_r, MLA_HEADS, MLA_QK).astype(MXU_DTYPE)
    kvh = _heads(kv, MLA_HEADS, MLA_NOPE + MLA_V)
    kh = jnp.concatenate([kvh[..., :MLA_NOPE], jnp.broadcast_to(k_pe[None], (MLA_HEADS, s, MLA_ROPE))],
                         axis=-1).astype(MXU_DTYPE)
    vh = kvh[..., MLA_NOPE:].astype(MXU_DTYPE)
    yc_h, lse, *rest = mla_fwd(qh, kh, vh, name=nm('mla_fwd'), carried=ride('mla'))
    got.update(zip(riders.get('mla', ()), rest))
    late = None
    if riders:
        late = assemble_late(got)
        p = dict(p, **{n: late[n][li] for n in LATE})
    yc = _unheads(yc_h)
    ycn = rms_fwd(yc, p['mla_out_norm'], name=nm('mlanorm'), out_dtype=MXU_DTYPE)
    r.update(cq=cq, ckv=ckv, qn=qn, kvn=kvn, qh=qh, kh=kh, vh=vh, yc_h=yc_h, yc=yc, lse=lse)
    ycat = jnp.concatenate([yan, ybn, ycn], axis=1)
    h1 = mm(ycat, p['w_out'], name=nm('outproj'), res=h)
    hn2 = rms_fwd(h1, p['ffn_norm'], name=nm('ffnnorm'), out_dtype=MXU_DTYPE)
    up = mm(hn2, p['ffn_w_up'], name=nm('up'), bb='o', out_dtype=MXU_DTYPE)
    act = conv_glu_fwd(up, p['ffn_conv_w'], p['ffn_conv_b'], name=nm('glu'))
    h2 = mm(act, p['ffn_w_down'], name=nm('down'), ab='k', bb='k', res=h1)
    r.update(ycat=ycat, h1=h1, hn2=hn2, up=up, act=act)
    return h2, r, p, late


def layer_bwd(dh2, p, r, tabs, li, scatter_late=None):
    s = dh2.shape[0]
    tabq, tabt = tabs
    nm = lambda t: f"L{li}_{t}"
    g = {}
    dact = mm(dh2, p['ffn_w_down'], name=nm('d_down_x'), tb=True, bb='o')
    g['ffn_w_down'] = mm(r['act'], dh2, name=nm('d_down_w'), out_dtype=WIRE_DTYPE, ta=True, ab='o')
    du, g['ffn_conv_w'], g['ffn_conv_b'] = conv_glu_bwd(r['up'], dact, p['ffn_conv_w'], p['ffn_conv_b'], name=nm('d_glu'))
    dup = conv_t(du, p['ffn_conv_w'], name=nm('d_ffnconv'), out_dtype=MXU_DTYPE)
    g['ffn_w_up'] = mm(r['hn2'], dup, name=nm('d_up_w'), out_dtype=WIRE_DTYPE, ta=True, bb='o')
    dhn2 = mm(dup, p['ffn_w_up'], name=nm('d_up_x'), tb=True, ab='k', bb='k')
    dh1, dg = rms_bwd(r['h1'], p['ffn_norm'], dhn2, name=nm('d_ffnnorm'), add=dh2)
    g['ffn_norm'] = dg[0]
    dycat = mm(dh1, p['w_out'], name=nm('d_out_x'), tb=True)
    g['w_out'] = mm(r['ycat'], dh1, name=nm('d_out_w'), out_dtype=WIRE_DTYPE, ta=True)
    dya, dg = rms_bwd(r['ya'], p['sb_out_norm'], dycat[:, :256], name=nm('d_sbnorm'))
    g['sb_out_norm'] = dg[0]
    riders = {} if scatter_late is None else {'sb': ('ffn_w_up',), 'ssd': ('ffn_w_down', 'w_out', 'ffn_conv_w')}
    parts = owner_parts_late([g] + list(scatter_late)) if riders else None
    ride = lambda k: Carried('scatter', [parts[n].astype(WIRE_DTYPE) for n in riders[k]]) if riders else None
    recv_late = {} if riders else None
    dq, dk, dv, *rest = sb_bwd(r['qkv'], dya, name=nm('sb_bwd'), carried=ride('sb'))
    if riders:
        recv_late.update(zip(riders['sb'], rest))
    dyssm, dz, dg = rms_bwd(r['y_ssm'], p['ssm_out_norm'], dycat[:, 256:768], name=nm('d_ssmnorm'), gate=r['z'])
    g['ssm_out_norm'] = dg[0]
    dxbc_act, ddt_tail, dbias, dalog, dd, *rest = ssd_bwd(r['xbc_act'], r['tail'], p['ssm_dt_bias'], p['ssm_a_log'],
                                                          p['ssm_d'], r['hprev'], dyssm, name=nm('ssd_bwd'),
                                                          carried=ride('ssd'))
    if riders:
        recv_late.update(zip(riders['ssd'], rest))
    hl = slice(DT_LANE, DT_LANE + SSM_HEADS)
    g['ssm_dt_bias'], g['ssm_a_log'], g['ssm_d'] = dbias[0, hl], dalog[0, hl], dd[0, hl]
    dxbc_u, g['ssm_conv_w'], g['ssm_conv_b'] = conv_silu_bwd(r['xbc'], dxbc_act[None], p['ssm_conv_w'], p['ssm_conv_b'],
                                                             name=nm('d_ssmact'))
    dxbc = conv_t(dxbc_u, p['ssm_conv_w'], name=nm('d_ssmconv'))[0]
    dyc, dg = rms_bwd(r['yc'], p['mla_out_norm'], dycat[:, 768:], name=nm('d_mlanorm'))
    g['mla_out_norm'] = dg[0]
    dqh, dkh, dvh = mla_bwd(r['qh'], r['kh'], r['vh'], r['yc_h'], _heads(dyc, MLA_HEADS, MLA_V), r['lse'], name=nm('mla_bwd'))
    dq_c = rope(_unheads(dqh)[None], tabq, name=nm('d_ropeq'), backward=True)
    g['mla_w_uq'] = mm(r['qn'], dq_c, name=nm('d_uq_w'), out_dtype=WIRE_DTYPE, ta=True)
    dcq, dg = rms_bwd(r['cq'], p['mla_q_norm'], mm(dq_c, p['mla_w_uq'], name=nm('d_uq_x'), tb=True), name=nm('d_qnorm'))
    g['mla_q_norm'] = dg[0]
    dkv = _unheads(jnp.concatenate([dkh[..., :MLA_NOPE], dvh], axis=-1))
    g['mla_w_ukv'] = mm(r['kvn'], dkv, name=nm('d_ukv_w'), out_dtype=WIRE_DTYPE, ta=True)
    dckv, dg = rms_bwd(r['ckv'], p['mla_kv_norm'], mm(dkv, p['mla_w_ukv'], name=nm('d_ukv_x'), tb=True), name=nm('d_kvnorm'))
    g['mla_kv_norm'] = dg[0]
    dkpe = jnp.pad(dkh[..., MLA_NOPE:], ((0, 0), (0, 0), (0, LANES - MLA_ROPE)))
    dtail = rope(dkpe, tabt, name=nm('d_ropek'), backward=True, add=ddt_tail)
    dproj = jnp.concatenate([dq, dk, dv, dz, dxbc, dcq, dckv, dtail], axis=1).astype(MXU_DTYPE)
    g['w_in'] = mm(r['hn'], dproj, name=nm('d_proj_w'), out_dtype=WIRE_DTYPE, ta=True)
    dhn = mm(dproj, p['w_in'], name=nm('d_proj_x'), tb=True)
    dh, dg = rms_bwd(r['h'], p['mix_norm'], dhn, name=nm('d_mixnorm'), add=dh1)
    g['mix_norm'] = dg[0]
    return dh, g, recv_late


def _w_in_placement():
    c = np.arange(D_IN)
    dest = np.where(c < 2048, c, np.where(c < 2056, c + (D_IN - 2056), c - 8))
    dest = jnp.asarray(dest.reshape(N_DEV, D_IN // N_DEV, 1), jnp.int32)
    return (dest == jnp.arange(D_IN_PAD, dtype=jnp.int32)).astype(MXU_DTYPE)


def _owner_major(full, axis):
    shp = full.shape
    return jnp.moveaxis(full.reshape(shp[:axis] + (N_DEV, shp[axis] // N_DEV) + shp[axis + 1:]), axis, 0)


def _owner_join(parts, axis):
    moved = jnp.moveaxis(parts, 0, axis)
    shp = moved.shape
    return moved.reshape(shp[:axis] + (shp[axis] * shp[axis + 1],) + shp[axis + 2:])


def assemble_early(gathered, replicated):
    L = DEPTH
    out = dict(replicated)
    out['w_in'] = mm(gathered['w_in'].reshape(N_DEV, L * D_MODEL, D_IN // N_DEV), _w_in_placement(), name='place_w_in',
                     ab='k', bb='k', out_dtype=MXU_DTYPE).reshape(L, D_MODEL, D_IN_PAD)
    out['mla_w_uq'] = _owner_join(gathered['mla_w_uq'], 2)
    out['mla_w_ukv'] = _owner_join(gathered['mla_w_ukv'], 2)
    out['ssm_conv_w'] = _owner_join(gathered['ssm_conv_w'], 2)[:, None]
    out['ssm_conv_b'] = replicated['ssm_conv_b'].reshape(L, 1, 1, SSM_CONV_DIM)
    out['ffn_conv_b'] = replicated['ffn_conv_b'].reshape(L, N_DEV, 1, FF_SHARD)
    return out


def assemble_late(gathered):
    L = DEPTH
    return {'ffn_w_up': jnp.moveaxis(gathered['ffn_w_up'], 1, 0),
            'w_out': _owner_join(gathered['w_out'], 1),
            'ffn_w_down': _owner_join(gathered['ffn_w_down'], 1).reshape(L, N_DEV // 2, FF_SHARD, D_MODEL),
            'ffn_conv_w': jnp.moveaxis(gathered['ffn_conv_w'], 1, 0)}


def owner_parts_late(grads):
    L = DEPTH
    st = lambda n: jnp.stack([g[n] for g in grads])
    return {'ffn_w_up': jnp.moveaxis(st('ffn_w_up'), 1, 0),
            'w_out': _owner_major(st('w_out'), 1),
            'ffn_w_down': _owner_major(st('ffn_w_down').reshape(L, D_FF, D_MODEL), 1),
            'ffn_conv_w': jnp.moveaxis(st('ffn_conv_w'), 1, 0)}


def owner_parts_early(grads):
    L = DEPTH
    st = lambda n: jnp.stack([g[n] for g in grads])
    parts = {
        'w_in': mm(st('w_in').reshape(L * D_MODEL, D_IN_PAD), _w_in_placement(), name='unplace_w_in', tb=True, bb='o',
                   out_dtype=WIRE_DTYPE).reshape(N_DEV, L, D_MODEL, D_IN // N_DEV),
        'mla_w_uq': _owner_major(st('mla_w_uq'), 2),
        'mla_w_ukv': _owner_major(st('mla_w_ukv'), 2),
        'ssm_conv_w': _owner_major(st('ssm_conv_w')[:, 0], 2),
    }
    rep = {n: st(n) for n in REPLICATED if n not in ('final_norm', 'ssm_conv_b', 'ffn_conv_b')}
    rep['ssm_conv_b'] = st('ssm_conv_b').reshape(L, SSM_CONV_DIM)
    rep['ffn_conv_b'] = st('ffn_conv_b').reshape(L, 2 * D_FF)
    return parts, rep


def local_step(x, positions, target, early, late_shards, replicated):
    s = x.shape[0]
    tabs = _rope_tables(positions, s)
    hn0, *got = rms_fwd(x, replicated['mix_norm'][0], name='L0_mixnorm', out_dtype=MXU_DTYPE,
                        carried=Carried('gather', [early[n] for n in EARLY]))
    params = assemble_early(dict(zip(EARLY, got)), replicated)
    layer = lambda li: {n: params[n][li] for n in params if n != 'final_norm'}
    h, r0, p0, late = layer_fwd(x, layer(0), tabs, 0, gather_late=late_shards, hn=hn0)
    saved = [(p0, r0)]
    for li in range(1, DEPTH):
        h, r, p, _ = layer_fwd(h, dict(layer(li), **{n: late[n][li] for n in LATE}), tabs, li)
        saved.append((p, r))
    y = rms_fwd(h, params['final_norm'], name='finalnorm')
    dy, loss = loss_head(y, target, name='loss')
    dh, dg = rms_bwd(h, params['final_norm'], dy, name='d_finalnorm')
    above = []
    for li in reversed(range(1, DEPTH)):
        dh, g, _ = layer_bwd(dh, *saved[li], tabs, li)
        above.insert(0, g)
    dh, g0, recv_late = layer_bwd(dh, *saved[0], tabs, 0, scatter_late=above)
    parts, rep = owner_parts_early([g0] + above)
    rep['final_norm'] = dg[0]
    return loss[0, 0], dh, parts, recv_late, rep


def all_gather(blocks, *, name):
    n = len(blocks)

    def body(*refs):
        x_refs, out_refs = refs[:n], refs[n:2 * n]
        send_sems, recv_sems, local_sems = refs[2 * n:]
        x, y, c = lax.axis_index("x"), lax.axis_index("y"), lax.axis_index("c")
        me, sibling = (x, y, c), (x, y, 1 - c)
        chips = [(1 - x, y), (x, 1 - y), (1 - x, 1 - y)]

        def slot(b, px, py, pc):
            return out_refs[b].at[4 * px + 2 * py + pc]

        def copy(b, k, blk, to, src=None):
            return pltpu.make_async_remote_copy(src_ref=slot(b, *blk) if src is None else src, dst_ref=slot(b, *blk),
                                                send_sem=send_sems.at[b, k], recv_sem=recv_sems.at[b, k],
                                                device_id=to, device_id_type=MESH)

        mine = [pltpu.make_async_copy(x_refs[b], slot(b, *me), local_sems.at[b]) for b in range(n)]
        for cp in mine:
            cp.start()
        first = []
        for b in range(n):
            first.append(copy(b, 0, me, sibling, src=x_refs[b]))
            first += [copy(b, 1 + j, me, (*chip, c), src=x_refs[b]) for j, chip in enumerate(chips)]
        for cp in first:
            cp.start()
        passed = []
        for j, chip in enumerate(chips):
            for b in range(n):
                copy(b, 1 + j, (*chip, c), me).wait_recv()
                fwd = copy(b, 4 + j, (*chip, c), sibling)
                fwd.start()
                passed.append(fwd)
        for b in range(n):
            copy(b, 0, sibling, me).wait_recv()
            for j, chip in enumerate(chips):
                copy(b, 4 + j, (*chip, 1 - c), me).wait_recv()
        for cp in first + passed:
            cp.wait_send()
        for cp in mine:
            cp.wait()

    return pl.pallas_call(
        body, name=name, out_shape=[jax.ShapeDtypeStruct((N_DEV,) + b.shape, b.dtype) for b in blocks],
        in_specs=[HBM] * n, out_specs=[HBM] * n,
        scratch_shapes=[pltpu.SemaphoreType.DMA((n, 7)), pltpu.SemaphoreType.DMA((n, 7)), pltpu.SemaphoreType.DMA((n,))],
    )(*blocks)


def all_to_all(parts, *, name):
    n = len(parts)

    def body(*refs):
        g_refs, r_refs = refs[:n], refs[n:2 * n]
        send_sems, recv_sems, local_sems = refs[2 * n:]
        x, y, c = lax.axis_index("x"), lax.axis_index("y"), lax.axis_index("c")
        me = 4 * x + 2 * y + c
        mine = [pltpu.make_async_copy(g_refs[b].at[me], r_refs[b].at[me], local_sems.at[b]) for b in range(n)]
        for cp in mine:
            cp.start()
        copies = []
        for k in range(1, N_DEV):
            px, py, pc = _flip(x, k & 4), _flip(y, k & 2), _flip(c, k & 1)
            peer = 4 * px + 2 * py + pc
            for b in range(n):
                cp = pltpu.make_async_remote_copy(src_ref=g_refs[b].at[peer], dst_ref=r_refs[b].at[me],
                                                  send_sem=send_sems.at[b, k - 1], recv_sem=recv_sems.at[b, k - 1],
                                                  device_id=(px, py, pc), device_id_type=MESH)
                cp.start()
                copies.append(cp)
        for cp in copies:
            cp.wait_send()
            cp.wait_recv()
        for cp in mine:
            cp.wait()

    return pl.pallas_call(
        body, name=name, out_shape=[jax.ShapeDtypeStruct(p.shape, p.dtype) for p in parts],
        in_specs=[HBM] * n, out_specs=[HBM] * n,
        scratch_shapes=[pltpu.SemaphoreType.DMA((n, 7)), pltpu.SemaphoreType.DMA((n, 7)), pltpu.SemaphoreType.DMA((n,))],
    )(*parts)


def adamw(parts, w, m, v, *, name):
    r, wd = w.shape
    br = _tile(r, (256, 128, 64, 32, 16, 8))
    c1 = 1.0 - ADAM_B1 ** ADAM_STEP
    c2 = 1.0 - ADAM_B2 ** ADAM_STEP

    def body(p_ref, w_ref, m_ref, v_ref, g_ref, d_ref, mo_ref, vo_ref):
        g = p_ref[0].astype(F32)
        for j in range(1, N_DEV):
            g = g + p_ref[j].astype(F32)
        mn = ADAM_B1 * m_ref[...] + (1.0 - ADAM_B1) * g
        vn = ADAM_B2 * v_ref[...] + (1.0 - ADAM_B2) * (g * g)
        g_ref[...] = g
        mo_ref[...] = mn
        vo_ref[...] = vn
        d_ref[...] = -ADAM_LR * ((mn / c1) / (jnp.sqrt(vn / c2) + ADAM_EPS) + ADAM_WD * w_ref[...])

    blk = pl.BlockSpec((br, wd), lambda i: (i, 0))
    out = jax.ShapeDtypeStruct((r, wd), F32)
    return _call(body, name=name, out_shape=[out] * 4, grid=(r // br,),
                 in_specs=[pl.BlockSpec((N_DEV, br, wd), lambda i: (0, i, 0)), blk, blk, blk], out_specs=[blk] * 4,
                 sem=("parallel",))(parts, w, m, v)


def _pack(arrs):
    flat = jnp.concatenate([a.reshape(-1) for a in arrs])
    rows = -(-flat.shape[0] // (8 * FLAT_W)) * 8
    return jnp.pad(flat, (0, rows * FLAT_W - flat.shape[0])).reshape(rows, FLAT_W)


def _unpack(flat, shapes):
    flat = flat.reshape(-1)
    out, off = [], 0
    for shp in shapes:
        n = int(np.prod(shp))
        out.append(flat[off:off + n].reshape(shp))
        off += n
    return out


def kernel(x, positions, mix_norm, w_in, sb_out_norm, ssm_conv_w, ssm_conv_b, ssm_dt_bias, ssm_a_log, ssm_d, ssm_out_norm, mla_q_norm, mla_w_uq, mla_kv_norm, mla_w_ukv, mla_out_norm, w_out, ffn_norm, ffn_w_up, ffn_conv_w, ffn_conv_b, ffn_w_down, final_norm, loss_target, m_mix_norm, m_w_in, m_sb_out_norm, m_ssm_conv_w, m_ssm_conv_b, m_ssm_dt_bias, m_ssm_a_log, m_ssm_d, m_ssm_out_norm, m_mla_q_norm, m_mla_w_uq, m_mla_kv_norm, m_mla_w_ukv, m_mla_out_norm, m_w_out, m_ffn_norm, m_ffn_w_up, m_ffn_conv_w, m_ffn_conv_b, m_ffn_w_down, m_final_norm, v_mix_norm, v_w_in, v_sb_out_norm, v_ssm_conv_w, v_ssm_conv_b, v_ssm_dt_bias, v_ssm_a_log, v_ssm_d, v_ssm_out_norm, v_mla_q_norm, v_mla_w_uq, v_mla_kv_norm, v_mla_w_ukv, v_mla_out_norm, v_w_out, v_ffn_norm, v_ffn_w_up, v_ffn_conv_w, v_ffn_conv_b, v_ffn_w_down, v_final_norm):
    args = locals()
    w = {n: args[n] for n in WEIGHTS}
    m = {n: args['m_' + n] for n in WEIGHTS}
    v = {n: args['v_' + n] for n in WEIGHTS}
    wire = lambda n: w[n] if n in VPU_WEIGHTS else w[n].astype(BF16)
    early = {n: wire(n) for n in EARLY}

    loss, dx, parts, recv, rep = local_step(x[0], positions[0], loss_target[0], early, {n: wire(n) for n in LATE},
                                            {n: w[n] for n in REPLICATED})
    loss = lax.psum(loss, ("x", "y", "c"))

    recv.update(zip(EARLY, all_to_all([parts[n].astype(WIRE_DTYPE) for n in EARLY], name='scatter_early')))
    res = {kind: {} for kind in 'gdmv'}
    for n, rv in recv.items():
        shp = w[n].shape
        two_d = (int(np.prod(shp[:-1])), shp[-1])
        outs = adamw(rv.reshape((N_DEV,) + two_d), w[n].reshape(two_d), m[n].reshape(two_d), v[n].reshape(two_d),
                     name='adamw_' + n)
        for kind, o in zip('gdmv', outs):
            res[kind][n] = o.reshape(shp)

    rep_shapes = [w[n].shape for n in REPLICATED]
    (rparts,) = all_gather([_pack([rep[n] for n in REPLICATED])], name='gather_small_grads')
    rflat = lambda d: _pack([d[n] for n in REPLICATED])
    routs = adamw(rparts, rflat(w), rflat(m), rflat(v), name='adamw_replicated')
    for kind, o in zip('gdmv', routs):
        res[kind].update(zip(REPLICATED, _unpack(o, rep_shapes)))

    return (loss, dx[None], *[res['g'][n] for n in WEIGHTS], *[res['d'][n] for n in WEIGHTS],
            *[res['m'][n] for n in WEIGHTS], *[res['v'][n] for n in WEIGHTS])
```
